```python
import math
import jax, jax.numpy as jnp
from jax import lax
import numpy as np

D_MODEL = 2048
BATCH = 1
SEQ = 8192
DEPTH = 2

N_MIXERS = 2
N_HEADS = 16
HEAD_DIM = 128
N_KV_GROUPS = 4
HEADS_PER_GROUP = N_HEADS // N_KV_GROUPS
CMP_BLOCK = 32
CMP_STRIDE = 16
CMP_HIDDEN = 2 * HEAD_DIM
SEL_BLOCK = 64
SEL_TOPK = 16
WINDOW = 512
Q_BLOCK = 128
N_GATES = 3
KV_WIDTH = N_KV_GROUPS * HEAD_DIM
NSA_IN_COLS = N_HEADS * HEAD_DIM + 6 * KV_WIDTH + N_GATES * N_HEADS
REL_BUCKETS = 32
REL_MAX_DIST = 128
SSM_GROUP = 16
SSM_GROUPS = D_MODEL // SSM_GROUP
SSM_STATE = 64
D_FF = ((8 * D_MODEL + 2) // 3 + 255) // 256 * 256
N_NSA_LAYERS = (DEPTH + N_MIXERS - 1) // N_MIXERS
N_S5_LAYERS = DEPTH // N_MIXERS
RMS_EPS = 1e-6
NEG_INF = -1e30
FORCE_SCORE = 1e9

kernel_name = "nsa_s5_interleaved_hybrid"


def rmsnorm(x, g):
    x32 = x.astype(jnp.float32)
    y = x32 * lax.rsqrt(jnp.mean(x32 * x32, axis=-1, keepdims=True) + RMS_EPS)
    return (y * g.astype(jnp.float32)).astype(x.dtype)


def rel_bucket(dist):
    n = jnp.maximum(dist, 0)
    max_exact = REL_BUCKETS // 2
    large = max_exact + (jnp.log(jnp.maximum(n, 1).astype(jnp.float32) / max_exact)
                         / math.log(REL_MAX_DIST / max_exact)
                         * (REL_BUCKETS - max_exact)).astype(jnp.int32)
    large = jnp.minimum(large, REL_BUCKETS - 1)
    return jnp.where(n < max_exact, n, large)


def masked_softmax(s, mask):
    s = jnp.where(mask, s.astype(jnp.float32), NEG_INF)
    p = jax.nn.softmax(s, axis=-1)
    return jnp.where(mask, p, 0.0)


def nsa_mixer(h, rel_bias, w_in, w_out, pos_k, w1_k, w2_k, pos_v, w1_v, w2_v):
    Bsz, S, _ = h.shape
    G, HPG, DH = N_KV_GROUPS, HEADS_PER_GROUP, HEAD_DIM
    proj = h @ w_in
    qe = N_HEADS * DH
    bounds = [qe + i * KV_WIDTH for i in range(7)]
    q, kc, vc, ks, vs, kw, vw, gl = jnp.split(proj, bounds, axis=-1)
    q = q.reshape(Bsz, S, G, HPG, DH) * (DH ** -0.5)
    kc, vc, ks, vs, kw, vw = [a.reshape(Bsz, S, G, DH) for a in (kc, vc, ks, vs, kw, vw)]
    gates = jax.nn.sigmoid(gl.astype(jnp.float32)).astype(h.dtype).reshape(Bsz, S, N_GATES, G, HPG)
    t = jnp.arange(S, dtype=jnp.int32)

    n_cmp = (S - CMP_BLOCK) // CMP_STRIDE + 1
    starts = jnp.arange(n_cmp, dtype=jnp.int32) * CMP_STRIDE
    gidx = starts[:, None] + jnp.arange(CMP_BLOCK, dtype=jnp.int32)[None, :]

    def compress(kv, pos, w1, w2):
        blk = kv[:, gidx] + pos[None, None, :, None, :]
        blk = jnp.moveaxis(blk, 3, 2).reshape(Bsz, n_cmp, G, CMP_BLOCK * DH)
        return jax.nn.gelu(blk @ w1) @ w2

    kcc = compress(kc, pos_k, w1_k, w2_k)
    vcc = compress(vc, pos_v, w1_v, w2_v)
    dist_c = t[:, None] - (starts + CMP_BLOCK - 1)[None, :]
    mask_c = dist_c >= 0
    bias_c = rel_bias[rel_bucket(dist_c)].transpose(2, 0, 1).reshape(G, HPG, S, n_cmp)
    s_c = jnp.einsum('btghd,bcgd->bghtc', q, kcc).astype(jnp.float32) + bias_c
    p_c = masked_softmax(s_c, mask_c)
    o_c = jnp.einsum('bghtc,bcgd->btghd', p_c.astype(vcc.dtype), vcc)

    imp = p_c.sum(axis=2)
    n_sel = S // SEL_BLOCK
    ratio = SEL_BLOCK // CMP_STRIDE
    lo = CMP_BLOCK // CMP_STRIDE - 1
    count = ratio + lo
    right = max(0, (count - 1) + ratio * n_sel - lo - n_cmp)
    imp_pad = jnp.pad(imp, ((0, 0), (0, 0), (0, 0), (lo, right)))
    imp_slc = sum(imp_pad[..., k:k + ratio * n_sel:ratio] for k in range(count))
    j = jnp.arange(n_sel, dtype=jnp.int32)
    t_blk = t // SEL_BLOCK
    forced = (j[None, :] == 0) | (j[None, :] == t_blk[:, None]) | (j[None, :] == t_blk[:, None] - 1)
    causal_blk = (j * SEL_BLOCK)[None, :] <= t[:, None]
    score = jnp.where(forced, FORCE_SCORE, imp_slc)
    score = jnp.where(causal_blk, score, NEG_INF)
    k_top = min(SEL_TOPK, n_sel)
    _, sel_idx = lax.top_k(score, k_top)

    ks_blocks = ks.reshape(Bsz, n_sel, SEL_BLOCK, G, DH).transpose(0, 3, 1, 2, 4)
    vs_blocks = vs.reshape(Bsz, n_sel, SEL_BLOCK, G, DH).transpose(0, 3, 1, 2, 4)
    nq = S // Q_BLOCK
    q_chunks = q.reshape(Bsz, nq, Q_BLOCK, G, HPG, DH).transpose(1, 0, 2, 3, 4, 5)
    idx_chunks = sel_idx.reshape(Bsz, G, nq, Q_BLOCK, k_top).transpose(2, 0, 1, 3, 4)
    t0s = jnp.arange(nq, dtype=jnp.int32) * Q_BLOCK
    table_g = rel_bias.T.reshape(G, HPG, REL_BUCKETS)
    bi = jnp.arange(Bsz)[:, None, None, None]
    gi = jnp.arange(G)[None, :, None, None]
    gi5 = jnp.arange(G)[None, :, None, None, None]
    hi5 = jnp.arange(HPG)[None, None, :, None, None]
    in_blk = jnp.arange(SEL_BLOCK, dtype=jnp.int32)

    def sel_chunk(args):
        qc, ic, t0 = args
        kg = ks_blocks[bi, gi, ic].reshape(Bsz, G, Q_BLOCK, k_top * SEL_BLOCK, DH)
        vg = vs_blocks[bi, gi, ic].reshape(Bsz, G, Q_BLOCK, k_top * SEL_BLOCK, DH)
        kpos = (ic[..., None] * SEL_BLOCK + in_blk).reshape(Bsz, G, Q_BLOCK, k_top * SEL_BLOCK)
        tq = t0 + jnp.arange(Q_BLOCK, dtype=jnp.int32)
        dist = tq[None, None, :, None] - kpos
        bias = table_g[gi5, hi5, rel_bucket(dist)[:, :, None]]
        s = jnp.einsum('bqghd,bgqkd->bghqk', qc, kg).astype(jnp.float32) + bias
        p = masked_softmax(s, (dist >= 0)[:, :, None])
        return jnp.einsum('bghqk,bgqkd->bqghd', p.astype(vg.dtype), vg)

    o_s = lax.map(sel_chunk, (q_chunks, idx_chunks, t0s))
    o_s = o_s.transpose(1, 0, 2, 3, 4, 5).reshape(Bsz, S, G, HPG, DH)

    nb = S // Q_BLOCK
    n_band = WINDOW // Q_BLOCK

    def band(kv):
        padded = jnp.pad(kv, ((0, 0), (WINDOW, 0), (0, 0), (0, 0))).reshape(Bsz, nb + n_band, Q_BLOCK, G, DH)
        return jnp.concatenate([padded[:, i:i + nb] for i in range(n_band + 1)], axis=2)

    kwb, vwb = band(kw), band(vw)
    qi = jnp.arange(Q_BLOCK, dtype=jnp.int32)
    ki = jnp.arange((n_band + 1) * Q_BLOCK, dtype=jnp.int32)
    dist_w = qi[:, None] + WINDOW - ki[None, :]
    key_abs = jnp.arange(nb, dtype=jnp.int32)[:, None] * Q_BLOCK - WINDOW + ki[None, :]
    mask_w = ((dist_w >= 0) & (dist_w < WINDOW))[None] & (key_abs >= 0)[:, None, :]
    bias_w = rel_bias[rel_bucket(dist_w)].transpose(2, 0, 1).reshape(G, HPG, Q_BLOCK, -1)
    qw = q.reshape(Bsz, nb, Q_BLOCK, G, HPG, DH)
    s_w = jnp.einsum('bjqghd,bjkgd->bjghqk', qw, kwb).astype(jnp.float32) + bias_w
    p_w = masked_softmax(s_w, mask_w[:, None, None])
    o_w = jnp.einsum('bjghqk,bjkgd->bjqghd', p_w.astype(vwb.dtype), vwb).reshape(Bsz, S, G, HPG, DH)

    o = (gates[:, :, 0, :, :, None] * o_c + gates[:, :, 1, :, :, None] * o_s
         + gates[:, :, 2, :, :, None] * o_w)
    return o.reshape(Bsz, S, N_HEADS * DH) @ w_out


def s5_mixer(h, A_re, A_im, log_dt, B_re, B_im, C_re, C_im, D_skip, w_glu):
    Bsz, S, D = h.shape
    f32 = jnp.float32
    A_re, A_im, log_dt = A_re.astype(f32), A_im.astype(f32), log_dt.astype(f32)
    B_re, B_im, C_re, C_im = B_re.astype(f32), B_im.astype(f32), C_re.astype(f32), C_im.astype(f32)
    u = h.astype(f32).reshape(Bsz, S, SSM_GROUPS, SSM_GROUP)
    dt = jnp.exp(log_dt)[:, None]
    decay = jnp.exp(A_re * dt)
    ab_re = decay * jnp.cos(A_im * dt)
    ab_im = decay * jnp.sin(A_im * dt)
    pr, pi_ = ab_re - 1.0, ab_im
    den = A_re * A_re + A_im * A_im
    cf_re = (pr * A_re + pi_ * A_im) / den
    cf_im = (pi_ * A_re - pr * A_im) / den
    bb_re = cf_re[..., None] * B_re - cf_im[..., None] * B_im
    bb_im = cf_re[..., None] * B_im + cf_im[..., None] * B_re
    bu_re = jnp.einsum('bsgc,gnc->bsgn', u, bb_re)
    bu_im = jnp.einsum('bsgc,gnc->bsgn', u, bb_im)
    a_re = jnp.broadcast_to(ab_re, bu_re.shape)
    a_im = jnp.broadcast_to(ab_im, bu_im.shape)

    def combine(e1, e2):
        a1r, a1i, b1r, b1i = e1
        a2r, a2i, b2r, b2i = e2
        return (a1r * a2r - a1i * a2i, a1r * a2i + a1i * a2r,
                a2r * b1r - a2i * b1i + b2r, a2r * b1i + a2i * b1r + b2i)

    _, _, x_re, x_im = lax.associative_scan(combine, (a_re, a_im, bu_re, bu_im), axis=1)
    y = (jnp.einsum('bsgn,gcn->bsgc', x_re, C_re) - jnp.einsum('bsgn,gcn->bsgc', x_im, C_im)
         + D_skip.astype(f32).reshape(SSM_GROUPS, SSM_GROUP) * u)
    y = jax.nn.gelu(y).reshape(Bsz, S, D).astype(h.dtype)
    z = y @ w_glu
    return z[..., :D] * jax.nn.sigmoid(z[..., D:])


def swiglu(h, w_in, w_out):
    z = h @ w_in
    a, b = z[..., :D_FF], z[..., D_FF:]
    return (jax.nn.silu(a) * b) @ w_out


def setup_inputs(seed: int = 0) -> dict:
    key = jax.random.key(seed)
    ks = jax.random.split(key, 32)

    def nrm(k, shape, scale):
        return jax.random.normal(k, shape, jnp.float32) * scale

    D, H, DH, G, N, C = D_MODEL, N_HEADS, HEAD_DIM, SSM_GROUPS, SSM_STATE, SSM_GROUP
    n_idx = jnp.arange(N, dtype=jnp.float32)
    return {
        "x": nrm(ks[0], (BATCH, SEQ, D), 1.0),
        "rel_bias": nrm(ks[1], (REL_BUCKETS, H), 0.2),
        "mix_norm_g": 1.0 + nrm(ks[2], (DEPTH, D), 0.02),
        "ffn_norm_g": 1.0 + nrm(ks[3], (DEPTH, D), 0.02),
        "final_norm_g": 1.0 + nrm(ks[4], (D,), 0.02),
        "nsa_w_in": nrm(ks[5], (N_NSA_LAYERS, D, NSA_IN_COLS), D ** -0.5),
        "nsa_w_out": nrm(ks[6], (N_NSA_LAYERS, H * DH, D), (H * DH) ** -0.5),
        "cmp_pos_k": nrm(ks[7], (N_NSA_LAYERS, CMP_BLOCK, DH), 0.1),
        "cmp_w1_k": nrm(ks[8], (N_NSA_LAYERS, CMP_BLOCK * DH, CMP_HIDDEN), (CMP_BLOCK * DH) ** -0.5),
        "cmp_w2_k": nrm(ks[9], (N_NSA_LAYERS, CMP_HIDDEN, DH), CMP_HIDDEN ** -0.5),
        "cmp_pos_v": nrm(ks[10], (N_NSA_LAYERS, CMP_BLOCK, DH), 0.1),
        "cmp_w1_v": nrm(ks[11], (N_NSA_LAYERS, CMP_BLOCK * DH, CMP_HIDDEN), (CMP_BLOCK * DH) ** -0.5),
        "cmp_w2_v": nrm(ks[12], (N_NSA_LAYERS, CMP_HIDDEN, DH), CMP_HIDDEN ** -0.5),
        "s5_A_re": -0.5 + nrm(ks[13], (N_S5_LAYERS, G, N), 0.01),
        "s5_A_im": math.pi * n_idx + nrm(ks[14], (N_S5_LAYERS, G, N), 0.01),
        "s5_log_dt": jax.random.uniform(ks[15], (N_S5_LAYERS, G), jnp.float32,
                                        math.log(1e-3), math.log(1e-1)),
        "s5_B_re": nrm(ks[16], (N_S5_LAYERS, G, N, C), (2 * C) ** -0.5),
        "s5_B_im": nrm(ks[17], (N_S5_LAYERS, G, N, C), (2 * C) ** -0.5),
        "s5_C_re": nrm(ks[18], (N_S5_LAYERS, G, C, N), (2 * N) ** -0.5),
        "s5_C_im": nrm(ks[19], (N_S5_LAYERS, G, C, N), (2 * N) ** -0.5),
        "s5_D": nrm(ks[20], (N_S5_LAYERS, D), 1.0),
        "s5_w_glu": nrm(ks[21], (N_S5_LAYERS, D, 2 * D), D ** -0.5),
        "ffn_w_in": nrm(ks[22], (DEPTH, D, 2 * D_FF), D ** -0.5),
        "ffn_w_out": nrm(ks[23], (DEPTH, D_FF, D), D_FF ** -0.5),
    }


def reference(x, rel_bias, mix_norm_g, ffn_norm_g, final_norm_g, nsa_w_in, nsa_w_out,
              cmp_pos_k, cmp_w1_k, cmp_w2_k, cmp_pos_v, cmp_w1_v, cmp_w2_v,
              s5_A_re, s5_A_im, s5_log_dt, s5_B_re, s5_B_im, s5_C_re, s5_C_im, s5_D, s5_w_glu,
              ffn_w_in, ffn_w_out):
    h = x
    for i in range(DEPTH):
        hn = rmsnorm(h, mix_norm_g[i])
        li = i // N_MIXERS
        if i % N_MIXERS == 0:
            h = h + nsa_mixer(hn, rel_bias, nsa_w_in[li], nsa_w_out[li],
                              cmp_pos_k[li], cmp_w1_k[li], cmp_w2_k[li],
                              cmp_pos_v[li], cmp_w1_v[li], cmp_w2_v[li])
        else:
            h = h + s5_mixer(hn, s5_A_re[li], s5_A_im[li], s5_log_dt[li], s5_B_re[li], s5_B_im[li],
                             s5_C_re[li], s5_C_im[li], s5_D[li], s5_w_glu[li])
        h = h + swiglu(rmsnorm(h, ffn_norm_g[i]), ffn_w_in[i], ffn_w_out[i])
    return rmsnorm(h, final_norm_g)
```

```python
import functools
import math

import numpy as np
import jax
import jax.numpy as jnp
from jax import lax
from jax.experimental import pallas as pl
from jax.experimental.pallas import tpu as pltpu

D_MODEL = 2048
N_HEADS = 16
HEAD_DIM = 128
N_KV_GROUPS = 4
HPG = N_HEADS // N_KV_GROUPS
CMP_BLOCK = 32
CMP_STRIDE = 16
CMP_HIDDEN = 2 * HEAD_DIM
SEL_BLOCK = 64
SEL_TOPK = 16
WINDOW = 512
N_GATES = 3
KV_WIDTH = N_KV_GROUPS * HEAD_DIM
Q_WIDTH = N_HEADS * HEAD_DIM
PROJ_MAIN = Q_WIDTH + 6 * KV_WIDTH
REL_BUCKETS = 32
REL_MAX_DIST = 128
SSM_GROUP = 16
SSM_GROUPS = D_MODEL // SSM_GROUP
SSM_STATE = 64
D_FF = ((8 * D_MODEL + 2) // 3 + 255) // 256 * 256
RMS_EPS = 1e-6
NEG = -1e30
FORCE_SCORE = 1e9

LANES = 128
SUBLANES = 8
VMEM_LIMIT = 52 * 1024 * 1024

ATT_TILE = WINDOW // 2
CMP_TQ = 256
CMP_PAD = 8
S5_CHUNK = 32

BF16 = jnp.bfloat16
F32 = jnp.float32


def _dot(a, b):
    return jnp.dot(a, b, preferred_element_type=F32)


def _dot_nt(a, b):
    return lax.dot_general(a, b, (((1,), (1,)), ((), ())), preferred_element_type=F32)


def _params(*sem):
    return pltpu.CompilerParams(dimension_semantics=sem, vmem_limit_bytes=VMEM_LIMIT)


def _rmsnorm_kernel(x_ref, g_ref, o_ref):
    x = x_ref[...]
    ms = jnp.mean(x * x, axis=-1, keepdims=True)
    o_ref[...] = (x * lax.rsqrt(ms + RMS_EPS) * g_ref[...]).astype(o_ref.dtype)


def _rmsnorm(x, g, out_dtype, tm=512):
    s, d = x.shape
    return pl.pallas_call(
        _rmsnorm_kernel,
        out_shape=jax.ShapeDtypeStruct((s, d), out_dtype),
        grid=(s // tm,),
        in_specs=[pl.BlockSpec((tm, d), lambda i: (i, 0)), pl.BlockSpec((1, d), lambda i: (0, 0))],
        out_specs=pl.BlockSpec((tm, d), lambda i: (i, 0)),
        compiler_params=_params("parallel"),
    )(x, g.reshape(1, d))


def _mm_scale_kernel(a_ref, b_ref, s_ref, o_ref):
    o_ref[...] = (_dot(a_ref[...], b_ref[...]) * s_ref[...]).astype(o_ref.dtype)


def _mm_sigmoid_kernel(a_ref, b_ref, o_ref):
    o_ref[...] = jax.nn.sigmoid(_dot(a_ref[...], b_ref[...])).astype(o_ref.dtype)


def _mm_resid_kernel(a_ref, b_ref, r_ref, o_ref):
    o_ref[...] = r_ref[...] + _dot(a_ref[...], b_ref[...])


def _mm_swiglu_kernel(a_ref, b1_ref, b2_ref, o_ref):
    a = a_ref[...]
    za = _dot(a, b1_ref[...])
    zb = _dot(a, b2_ref[...])
    o_ref[...] = (jax.nn.silu(za) * zb).astype(o_ref.dtype)


def _mm_glu_resid_kernel(a_ref, b1_ref, b2_ref, r_ref, o_ref):
    a = a_ref[...]
    za = _dot(a, b1_ref[...])
    zb = _dot(a, b2_ref[...])
    o_ref[...] = r_ref[...] + za * jax.nn.sigmoid(zb)


def _matmul(kern, a, b, n_out, out_dtype, tm, tn, *, b_halves=1, row_extra=None, tile_extra=None):
    m, k = a.shape
    nj = n_out // tn
    in_specs = [pl.BlockSpec((tm, k), lambda i, j: (i, 0))]
    args = [a]
    for half in range(b_halves):
        in_specs.append(pl.BlockSpec((k, tn), functools.partial(lambda i, j, o: (0, j + o), o=half * nj)))
        args.append(b)
    if row_extra is not None:
        in_specs.append(pl.BlockSpec((1, tn), lambda i, j: (0, j)))
        args.append(row_extra)
    if tile_extra is not None:
        in_specs.append(pl.BlockSpec((tm, tn), lambda i, j: (i, j)))
        args.append(tile_extra)
    return pl.pallas_call(
        kern,
        out_shape=jax.ShapeDtypeStruct((m, n_out), out_dtype),
        grid=(m // tm, nj),
        in_specs=in_specs,
        out_specs=pl.BlockSpec((tm, tn), lambda i, j: (i, j)),
        compiler_params=_params("parallel", "arbitrary"),
    )(*args)


def _swiglu_ffn(h, g, w_in, w_out):
    hn = _rmsnorm(h, g, BF16)
    act = _matmul(_mm_swiglu_kernel, hn, w_in.astype(BF16), D_FF, BF16, 1024, 512, b_halves=2)
    return _matmul(_mm_resid_kernel, act, w_out.astype(BF16), D_MODEL, F32, 512, 512, tile_extra=h)


def _compress_kernel(r_ref, w1_ref, w2_ref, pos_ref, o_ref):
    r = r_ref[0, 0]
    half = CMP_STRIDE * HEAD_DIM
    top = _dot(r, w1_ref[0, :half, :])
    bot = _dot(r, w1_ref[0, half:, :])
    posb = _dot(pos_ref[0], w1_ref[0])[0:1]
    ncp = r.shape[0]
    hid = top + pltpu.roll(bot, ncp - 1, 0) + posb
    o_ref[0, 0] = _dot(jax.nn.gelu(hid).astype(BF16), w2_ref[0]).astype(o_ref.dtype)


def _compress(r, w1, w2, pos):
    _, g, ncp, width = r.shape
    return pl.pallas_call(
        _compress_kernel,
        out_shape=jax.ShapeDtypeStruct((2, g, ncp, HEAD_DIM), BF16),
        grid=(2, g),
        in_specs=[
            pl.BlockSpec((1, 1, ncp, width), lambda kv, gi: (kv, gi, 0, 0)),
            pl.BlockSpec((1, 2 * width, CMP_HIDDEN), lambda kv, gi: (kv, 0, 0)),
            pl.BlockSpec((1, CMP_HIDDEN, HEAD_DIM), lambda kv, gi: (kv, 0, 0)),
            pl.BlockSpec((1, SUBLANES, 2 * width), lambda kv, gi: (kv, 0, 0)),
        ],
        out_specs=pl.BlockSpec((1, 1, ncp, HEAD_DIM), lambda kv, gi: (kv, gi, 0, 0)),
        compiler_params=_params("parallel", "parallel"),
    )(r, w1, w2, pos)


def _cmp_attn_kernel(q_ref, kcc_ref, vcct_ref, band_ref, mt_ref, oc_ref, nm_ref, s_scr, *, tq, nb):
    i = pl.program_id(1)
    band_start = pl.multiple_of(i * (tq // CMP_STRIDE), SUBLANES)
    kcc = kcc_ref[0]
    vcct = vcct_ref[0]
    n_rows = kcc.shape[0]
    rows = lax.broadcasted_iota(jnp.int32, (n_rows, tq), 0)
    row_bias = jnp.where(rows >= CMP_PAD, jnp.where(rows < band_start + nb, 0.0, NEG), NEG)
    imp = jnp.zeros((n_rows, tq), F32)
    for h in range(HPG):
        qh = q_ref[:, h * HEAD_DIM:(h + 1) * HEAD_DIM]
        s_scr[...] = _dot_nt(kcc, qh) + row_bias
        s_scr[pl.ds(band_start, nb), :] += band_ref[h]
        s = s_scr[...]
        m = jnp.max(s, axis=0, keepdims=True)
        e = jnp.where(s > 0.5 * NEG, jnp.exp(s - m), 0.0)
        l = jnp.sum(e, axis=0, keepdims=True)
        pn = e * (1.0 / jnp.where(l > 0.0, l, 1.0))
        imp = imp + pn
        oct_h = _dot(vcct, pn.astype(BF16))
        oc_ref[:, h * HEAD_DIM:(h + 1) * HEAD_DIM] = oct_h.T
    slc = jnp.dot(mt_ref[...], imp, precision=lax.Precision.HIGHEST, preferred_element_type=F32)
    n_blk = slc.shape[0]
    jj = lax.broadcasted_iota(jnp.int32, (n_blk, tq), 0)
    t_blk = (i * tq + lax.broadcasted_iota(jnp.int32, (n_blk, tq), 1)) // SEL_BLOCK
    forced = jnp.where(jj == 0, 1, jnp.where(jj == t_blk, 1, jnp.where(jj == t_blk - 1, 1, 0)))
    score = jnp.where(forced == 1, FORCE_SCORE, slc)
    score = jnp.where(jj <= t_blk, score, NEG)
    picked = jnp.zeros((n_blk, tq), F32)
    for _ in range(SEL_TOPK):
        best = jnp.max(score, axis=0, keepdims=True)
        first = jnp.min(jnp.where(score == best, jj, n_blk), axis=0, keepdims=True)
        hit = jj == first
        picked = jnp.where(hit, 1.0, picked)
        score = jnp.where(hit, -jnp.inf, score)
    neg_mask = jnp.where(picked > 0.5, jnp.where(jj <= t_blk, 0.0, NEG), NEG)
    nm_ref[...] = neg_mask.T.astype(nm_ref.dtype)


def _cmp_attn(proj, kcc, vcct, band, mt, tq):
    s = proj.shape[0]
    n_rows = kcc.shape[1]
    nb = band.shape[1]
    kern = functools.partial(_cmp_attn_kernel, tq=tq, nb=nb)
    return pl.pallas_call(
        kern,
        out_shape=(jax.ShapeDtypeStruct((s, Q_WIDTH), F32),
                   jax.ShapeDtypeStruct((s, N_KV_GROUPS * LANES), BF16)),
        grid=(N_KV_GROUPS, s // tq),
        in_specs=[
            pl.BlockSpec((tq, HPG * HEAD_DIM), lambda g, i: (i, g)),
            pl.BlockSpec((1, n_rows, HEAD_DIM), lambda g, i: (g, 0, 0)),
            pl.BlockSpec((1, HEAD_DIM, n_rows), lambda g, i: (g, 0, 0)),
            pl.BlockSpec((HPG, nb, tq), lambda g, i: (g, 0, 0)),
            pl.BlockSpec((LANES, n_rows), lambda g, i: (0, 0)),
        ],
        out_specs=(pl.BlockSpec((tq, HPG * HEAD_DIM), lambda g, i: (i, g)),
                   pl.BlockSpec((tq, LANES), lambda g, i: (i, g))),
        scratch_shapes=[pltpu.VMEM((n_rows, tq), F32)],
        compiler_params=_params("parallel", "parallel"),
    )(proj, kcc, vcct, band, mt)


def _online_update(s, v, m_ref, l_ref, a_ref):
    m_prev = m_ref[...]
    m_new = jnp.maximum(m_prev, jnp.max(s, axis=1, keepdims=True))
    alpha = jnp.exp(m_prev - m_new)
    p = jnp.exp(s - m_new)
    l_ref[...] = alpha * l_ref[...] + jnp.sum(p, axis=1, keepdims=True)
    a_ref[...] = alpha * a_ref[...] + _dot(p.astype(BF16), v)
    m_ref[...] = m_new


def _sel_win_kernel(q_ref, nm_ref, kaug_ref, vs_ref, kw_ref, vw_ref, b0_ref, b1_ref, g_ref, oc_ref, o_ref,
                    q4, m_s, l_s, a_s, m_w, l_w, a_w, *, tile):
    i = pl.program_id(1)
    t = tile
    for h in range(HPG):
        q4[h * t:(h + 1) * t, :HEAD_DIM] = q_ref[:, h * HEAD_DIM:(h + 1) * HEAD_DIM]
        q4[h * t:(h + 1) * t, HEAD_DIM:] = nm_ref[...]
    for m_ref, l_ref, a_ref in ((m_s, l_s, a_s), (m_w, l_w, a_w)):
        m_ref[...] = jnp.full(m_ref.shape, -3e38, F32)
        l_ref[...] = jnp.zeros(l_ref.shape, F32)
        a_ref[...] = jnp.zeros(a_ref.shape, F32)

    def key_rows(kk):
        return pl.ds(pl.multiple_of(kk * t, t), t)

    def far_tile(kk, carry):
        rows = key_rows(kk)
        _online_update(_dot_nt(q4[...], kaug_ref[0, rows, :]), vs_ref[rows, :], m_s, l_s, a_s)
        return carry

    lax.fori_loop(0, jnp.maximum(i - 1, 0), far_tile, 0)

    b0 = b0_ref[...].reshape(HPG * t, t)
    b1 = b1_ref[...].reshape(HPG * t, t)
    qn = q4[:, :HEAD_DIM]

    @pl.when(i >= 2)
    def _():
        rows = key_rows(i - 2)
        tl = lax.broadcasted_iota(jnp.int32, (HPG * t, t), 0) & (t - 1)
        kl = lax.broadcasted_iota(jnp.int32, (HPG * t, t), 1)
        edge = jnp.where(tl < kl, 0.0, NEG)
        _online_update(_dot_nt(qn, kw_ref[rows, :]) + edge, vw_ref[rows, :], m_w, l_w, a_w)

    @pl.when(i >= 1)
    def _():
        rows = key_rows(i - 1)
        _online_update(_dot_nt(q4[...], kaug_ref[0, rows, :]) + b1, vs_ref[rows, :], m_s, l_s, a_s)
        _online_update(_dot_nt(qn, kw_ref[rows, :]) + b1, vw_ref[rows, :], m_w, l_w, a_w)

    rows = key_rows(i)
    _online_update(_dot_nt(q4[...], kaug_ref[0, rows, :]) + b0, vs_ref[rows, :], m_s, l_s, a_s)
    _online_update(_dot_nt(qn, kw_ref[rows, :]) + b0, vw_ref[rows, :], m_w, l_w, a_w)

    o_s = a_s[...] * (1.0 / l_s[...])
    o_w = a_w[...] * (1.0 / l_w[...])
    gates = g_ref[...]
    for h in range(HPG):
        cols = slice(h * HEAD_DIM, (h + 1) * HEAD_DIM)
        o = (gates[:, h:h + 1] * oc_ref[:, cols]
             + gates[:, HPG + h:HPG + h + 1] * o_s[h * t:(h + 1) * t]
             + gates[:, 2 * HPG + h:2 * HPG + h + 1] * o_w[h * t:(h + 1) * t])
        o_ref[:, cols] = o.astype(o_ref.dtype)


def _sel_win(proj, neg_mask, kaug, b0, b1, gates, o_c, tile):
    s = proj.shape[0]
    vs_blk = (Q_WIDTH + 3 * KV_WIDTH) // HEAD_DIM
    kw_blk = (Q_WIDTH + 4 * KV_WIDTH) // HEAD_DIM
    vw_blk = (Q_WIDTH + 5 * KV_WIDTH) // HEAD_DIM
    rows4 = HPG * tile
    kern = functools.partial(_sel_win_kernel, tile=tile)
    return pl.pallas_call(
        kern,
        out_shape=jax.ShapeDtypeStruct((s, Q_WIDTH), BF16),
        grid=(N_KV_GROUPS, s // tile),
        in_specs=[
            pl.BlockSpec((tile, HPG * HEAD_DIM), lambda g, i: (i, g)),
            pl.BlockSpec((tile, LANES), lambda g, i: (i, g)),
            pl.BlockSpec((1, s, 2 * HEAD_DIM), lambda g, i: (g, 0, 0)),
            pl.BlockSpec((s, HEAD_DIM), lambda g, i: (0, vs_blk + g)),
            pl.BlockSpec((s, HEAD_DIM), lambda g, i: (0, kw_blk + g)),
            pl.BlockSpec((s, HEAD_DIM), lambda g, i: (0, vw_blk + g)),
            pl.BlockSpec((HPG, tile, tile), lambda g, i: (g, 0, 0)),
            pl.BlockSpec((HPG, tile, tile), lambda g, i: (g, 0, 0)),
            pl.BlockSpec((tile, LANES), lambda g, i: (i, g)),
            pl.BlockSpec((tile, HPG * HEAD_DIM), lambda g, i: (i, g)),
        ],
        out_specs=pl.BlockSpec((tile, HPG * HEAD_DIM), lambda g, i: (i, g)),
        scratch_shapes=[
            pltpu.VMEM((rows4, 2 * HEAD_DIM), BF16),
            pltpu.VMEM((rows4, 1), F32), pltpu.VMEM((rows4, 1), F32), pltpu.VMEM((rows4, HEAD_DIM), F32),
            pltpu.VMEM((rows4, 1), F32), pltpu.VMEM((rows4, 1), F32), pltpu.VMEM((rows4, HEAD_DIM), F32),
        ],
        compiler_params=_params("parallel", "arbitrary"),
    )(proj, neg_mask, kaug, proj, proj, proj, b0, b1, gates, o_c)


def _rel_bucket_np(dist):
    n = np.maximum(dist, 0)
    max_exact = REL_BUCKETS // 2
    large = max_exact + (np.log(np.maximum(n, 1).astype(np.float32) / np.float32(max_exact))
                         / np.float32(math.log(REL_MAX_DIST / max_exact))
                         * np.float32(REL_BUCKETS - max_exact)).astype(np.int32)
    large = np.minimum(large, REL_BUCKETS - 1)
    return np.where(n < max_exact, n, large).astype(np.int32)


def _far_distance():
    d = np.arange(4 * REL_MAX_DIST)
    b = _rel_bucket_np(d)
    assert b[-1] == REL_BUCKETS - 1
    return int(np.max(np.nonzero(b != REL_BUCKETS - 1)[0])) + 1


def _bias_table(rel_bias, dist):
    bucket = jnp.asarray(_rel_bucket_np(dist))
    shifted = rel_bias[bucket] - rel_bias[REL_BUCKETS - 1]
    tab = jnp.where(jnp.asarray(dist >= 0)[..., None], shifted, NEG)
    return jnp.moveaxis(tab, -1, 0).astype(F32)


def _nsa_mixer(h, hn, rel_bias, w_in, w_out, pos_k, w1_k, w2_k, pos_v, w1_v, w2_v):
    s = h.shape[0]
    n_sel = s // SEL_BLOCK
    ncp = s // CMP_STRIDE
    tile = ATT_TILE
    far = _far_distance()
    assert n_sel <= LANES and s % CMP_TQ == 0 and s % tile == 0
    assert far <= tile and far <= CMP_STRIDE * CMP_PAD - CMP_BLOCK + 1 + CMP_STRIDE

    col_scale = jnp.where(jnp.arange(PROJ_MAIN) < Q_WIDTH, HEAD_DIM ** -0.5, 1.0).astype(F32)[None, :]
    proj = _matmul(_mm_scale_kernel, hn, w_in[:, :PROJ_MAIN].astype(BF16), PROJ_MAIN, BF16, 1024, 512,
                   row_extra=col_scale)
    w_gate = w_in[:, PROJ_MAIN:].reshape(D_MODEL, N_GATES, N_KV_GROUPS, HPG).transpose(0, 2, 1, 3)
    w_gate = w_gate.reshape(D_MODEL, N_KV_GROUPS, N_GATES * HPG)
    w_gate = jnp.pad(w_gate, ((0, 0), (0, 0), (0, LANES - N_GATES * HPG))).reshape(D_MODEL, N_KV_GROUPS * LANES)
    gates = _matmul(_mm_sigmoid_kernel, hn, w_gate.astype(BF16), N_KV_GROUPS * LANES, F32, 1024, 512)

    def stride_rows(col0):
        a = proj[:, col0:col0 + KV_WIDTH].reshape(ncp, CMP_STRIDE, N_KV_GROUPS, HEAD_DIM)
        return a.transpose(2, 0, 1, 3).reshape(N_KV_GROUPS, ncp, CMP_STRIDE * HEAD_DIM)

    r = jnp.stack([stride_rows(Q_WIDTH), stride_rows(Q_WIDTH + KV_WIDTH)])
    w1 = jnp.stack([w1_k, w1_v]).astype(BF16)
    w2 = jnp.stack([w2_k, w2_v]).astype(BF16)
    pos = jnp.stack([pos_k.reshape(1, -1), pos_v.reshape(1, -1)]).astype(BF16)
    pos = jnp.broadcast_to(pos, (2, SUBLANES, CMP_BLOCK * HEAD_DIM))
    cc = _compress(r, w1, w2, pos)
    kcc = jnp.pad(cc[0], ((0, 0), (CMP_PAD, 0), (0, 0)))
    vcct = jnp.pad(cc[1], ((0, 0), (CMP_PAD, 0), (0, 0))).transpose(0, 2, 1)
    n_rows = ncp + CMP_PAD

    nb = CMP_TQ // CMP_STRIDE + CMP_PAD
    b_idx = np.arange(nb)[:, None]
    t_idx = np.arange(CMP_TQ)[None, :]
    band = _bias_table(rel_bias, t_idx - CMP_STRIDE * (b_idx - CMP_PAD) - (CMP_BLOCK - 1))
    ratio = SEL_BLOCK // CMP_STRIDE
    lo = CMP_BLOCK // CMP_STRIDE - 1
    c_of_row = np.arange(n_rows)[None, :] - CMP_PAD
    j_of = np.arange(LANES)[:, None]
    mt = ((c_of_row >= ratio * j_of - lo) & (c_of_row <= ratio * j_of + ratio - 1)
          & (c_of_row >= 0) & (c_of_row <= ncp - 2) & (j_of < n_sel)).astype(np.float32)
    o_c, neg_mask = _cmp_attn(proj, kcc, vcct, band, jnp.asarray(mt), CMP_TQ)

    ks = proj[:, Q_WIDTH + 2 * KV_WIDTH:Q_WIDTH + 3 * KV_WIDTH].reshape(s, N_KV_GROUPS, HEAD_DIM).transpose(1, 0, 2)
    onehot = (np.arange(s)[:, None] // SEL_BLOCK == np.arange(LANES)[None, :]).astype(np.float32)
    onehot = jnp.broadcast_to(jnp.asarray(onehot, BF16), (N_KV_GROUPS, s, LANES))
    kaug = jnp.concatenate([ks, onehot], axis=-1)
    tl = np.arange(tile)[:, None]
    kl = np.arange(tile)[None, :]
    b0 = _bias_table(rel_bias, tl - kl)
    b1 = _bias_table(rel_bias, tile + tl - kl)
    o = _sel_win(proj, neg_mask, kaug, b0, b1, gates, o_c, tile)
    return _matmul(_mm_resid_kernel, o, w_out.astype(BF16), D_MODEL, F32, 1024, 512, tile_extra=h)


def _s5_kernel(u_ref, lam_ref, bt_ref, c_ref, d_ref, y_ref, *, chunk):
    L = chunk
    half = L // 2
    u = u_ref[0]
    n_chunks = u.shape[0]
    lam = lam_ref[0]
    a_re, a_im = lam[0:1], lam[1:2]
    dt = jnp.exp(lam[2:3])
    log_re, log_im = a_re * dt, a_im * dt

    def cpow(e):
        mag = jnp.exp(e * log_re)
        return mag * jnp.cos(e * log_im), mag * jnp.sin(e * log_im)

    def cmul(xr, xi, yr, yi):
        return xr * yr - xi * yi, xr * yi + xi * yr

    consts = jnp.where(lax.broadcasted_iota(jnp.int32, (SUBLANES, 1), 0) == 0, 1.0,
                       jnp.where(lax.broadcasted_iota(jnp.int32, (SUBLANES, 1), 0) == 1, float(half + 1),
                                 jnp.where(lax.broadcasted_iota(jnp.int32, (SUBLANES, 1), 0) == 2, float(half - 1),
                                           float(L))))
    kr, ki = cpow(consts)
    ab_r, ab_i = kr[0:1], ki[0:1]
    pr, pi = ab_r - 1.0, ab_i
    den = a_re * a_re + a_im * a_im
    cf_r = (pr * a_re + pi * a_im) / den
    cf_i = (pi * a_re - pr * a_im) / den
    bb_r, bb_i = cmul(cf_r, cf_i, bt_ref[0, 0], bt_ref[0, 1])
    c_r, c_i = c_ref[0, 0], c_ref[0, 1]

    tau = lax.broadcasted_iota(jnp.int32, (L, 1), 0).astype(F32)
    e1r, e1i = cpow(tau - half)
    e2r, e2i = cpow(half - tau)
    e3r, e3i = cmul(e1r, e1i, kr[1:2], ki[1:2])
    e4r, e4i = cmul(e2r, e2i, kr[2:3], ki[2:3])

    def outer(er, ei, wr, wi):
        xr, xi = cmul(er[:, None, :], ei[:, None, :], wr[None, :, :], wi[None, :, :])
        return xr.reshape(L * SSM_GROUP, SSM_STATE), xi.reshape(L * SSM_GROUP, SSM_STATE)

    qm_r, qm_i = outer(e1r, e1i, c_r, c_i)
    km_r, km_i = outer(e2r, e2i, bb_r, bb_i)
    qc_r, qc_i = outer(e3r, e3i, c_r, c_i)
    wz_r, wz_i = outer(e4r, e4i, bb_r, bb_i)

    width = L * SSM_GROUP
    tt = _dot_nt(km_r.astype(BF16), qm_r.astype(BF16)) - _dot_nt(km_i.astype(BF16), qm_i.astype(BF16))
    src = lax.broadcasted_iota(jnp.int32, (width, width), 0) // SSM_GROUP
    dst = lax.broadcasted_iota(jnp.int32, (width, width), 1) // SSM_GROUP
    tt = jnp.where(src <= dst, tt, 0.0).astype(BF16)
    ub = u.astype(BF16)
    y = _dot(ub, tt)

    x_r = _dot(ub, wz_r.astype(BF16))
    x_i = _dot(ub, wz_i.astype(BF16))
    row = lax.broadcasted_iota(jnp.int32, (n_chunks, SSM_STATE), 0)
    p_r, p_i = kr[3:4], ki[3:4]
    d = 1
    while d < n_chunks:
        s_r = jnp.where(row >= d, pltpu.roll(x_r, d, 0), 0.0)
        s_i = jnp.where(row >= d, pltpu.roll(x_i, d, 0), 0.0)
        inc_r, inc_i = cmul(p_r, p_i, s_r, s_i)
        x_r, x_i = x_r + inc_r, x_i + inc_i
        p_r, p_i = cmul(p_r, p_i, p_r, p_i)
        d *= 2
    prev_r = jnp.where(row >= 1, pltpu.roll(x_r, 1, 0), 0.0)
    prev_i = jnp.where(row >= 1, pltpu.roll(x_i, 1, 0), 0.0)
    y = y + _dot_nt(prev_r.astype(BF16), qc_r.astype(BF16)) - _dot_nt(prev_i.astype(BF16), qc_i.astype(BF16))
    y = y + d_ref[0] * u
    y_ref[0] = jax.nn.gelu(y).astype(y_ref.dtype)


def _s5_mixer(h, hn, a_re, a_im, log_dt, b_re, b_im, c_re, c_im, d_skip, w_glu):
    s = h.shape[0]
    L = S5_CHUNK
    n_chunks = s // L
    width = L * SSM_GROUP
    u = hn.reshape(n_chunks, L, SSM_GROUPS, SSM_GROUP).transpose(2, 0, 1, 3).reshape(SSM_GROUPS, n_chunks, width)
    lam = jnp.stack([a_re, a_im, jnp.broadcast_to(log_dt[:, None], a_re.shape)], axis=1).astype(F32)
    bt = jnp.stack([b_re.transpose(0, 2, 1), b_im.transpose(0, 2, 1)], axis=1).astype(F32)
    cc = jnp.stack([c_re, c_im], axis=1).astype(F32)
    dd = jnp.tile(d_skip.astype(F32).reshape(SSM_GROUPS, 1, SSM_GROUP), (1, 1, L))
    y = pl.pallas_call(
        functools.partial(_s5_kernel, chunk=L),
        out_shape=jax.ShapeDtypeStruct((SSM_GROUPS, n_chunks, width), BF16),
        grid=(SSM_GROUPS,),
        in_specs=[
            pl.BlockSpec((1, n_chunks, width), lambda g: (g, 0, 0)),
            pl.BlockSpec((1, 3, SSM_STATE), lambda g: (g, 0, 0)),
            pl.BlockSpec((1, 2, SSM_GROUP, SSM_STATE), lambda g: (g, 0, 0, 0)),
            pl.BlockSpec((1, 2, SSM_GROUP, SSM_STATE), lambda g: (g, 0, 0, 0)),
            pl.BlockSpec((1, 1, width), lambda g: (g, 0, 0)),
        ],
        out_specs=pl.BlockSpec((1, n_chunks, width), lambda g: (g, 0, 0)),
        compiler_params=_params("parallel"),
    )(u, lam, bt, cc, dd)
    y = y.reshape(SSM_GROUPS, n_chunks, L, SSM_GROUP).transpose(1, 2, 0, 3).reshape(s, D_MODEL)
    return _matmul(_mm_glu_resid_kernel, y, w_glu.astype(BF16), D_MODEL, F32, 1024, 512, b_halves=2, tile_extra=h)


def kernel(x, rel_bias, mix_norm_g, ffn_norm_g, final_norm_g, nsa_w_in, nsa_w_out, cmp_pos_k, cmp_w1_k, cmp_w2_k, cmp_pos_v, cmp_w1_v, cmp_w2_v, s5_A_re, s5_A_im, s5_log_dt, s5_B_re, s5_B_im, s5_C_re, s5_C_im, s5_D, s5_w_glu, ffn_w_in, ffn_w_out):
    assert x.shape[0] == 1
    h = x[0]
    hn = _rmsnorm(h, mix_norm_g[0], BF16)
    h = _nsa_mixer(h, hn, rel_bias, nsa_w_in[0], nsa_w_out[0], cmp_pos_k[0], cmp_w1_k[0], cmp_w2_k[0],
                   cmp_pos_v[0], cmp_w1_v[0], cmp_w2_v[0])
    h = _swiglu_ffn(h, ffn_norm_g[0], ffn_w_in[0], ffn_w_out[0])
    hn = _rmsnorm(h, mix_norm_g[1], F32)
    h = _s5_mixer(h, hn, s5_A_re[0], s5_A_im[0], s5_log_dt[0], s5_B_re[0], s5_B_im[0], s5_C_re[0], s5_C_im[0],
                  s5_D[0], s5_w_glu[0])
    h = _swiglu_ffn(h, ffn_norm_g[1], ffn_w_in[1], ffn_w_out[1])
    return _rmsnorm(h, final_norm_g, x.dtype)[None]
```

```python
import functools
import math

import numpy as np
import jax
import jax.numpy as jnp
from jax import lax
from jax.experimental import pallas as pl
from jax.experimental.pallas import tpu as pltpu

D_MODEL = 2048
N_HEADS = 16
HEAD_DIM = 128
N_KV_GROUPS = 4
HPG = N_HEADS // N_KV_GROUPS
CMP_BLOCK = 32
CMP_STRIDE = 16
CMP_HIDDEN = 2 * HEAD_DIM
SEL_BLOCK = 64
SEL_TOPK = 16
WINDOW = 512
N_GATES = 3
KV_WIDTH = N_KV_GROUPS * HEAD_DIM
Q_WIDTH = N_HEADS * HEAD_DIM
PROJ_MAIN = Q_WIDTH + 6 * KV_WIDTH
REL_BUCKETS = 32
REL_MAX_DIST = 128
SSM_GROUP = 16
SSM_GROUPS = D_MODEL // SSM_GROUP
SSM_STATE = 64
D_FF = ((8 * D_MODEL + 2) // 3 + 255) // 256 * 256
RMS_EPS = 1e-6
NEG = -1e30
FORCE_SCORE = 1e9
LOG2E = math.log2(math.e)

LANES = 128
SUBLANES = 8
VMEM_LIMIT = 52 * 1024 * 1024

ATT_TILE = WINDOW // 2
CMP_TQ = 256
CMP_PAD = 8
S5_CHUNK = 32

BF16 = jnp.bfloat16
F32 = jnp.float32


def _dot(a, b):
    return jnp.dot(a, b, preferred_element_type=F32)


def _dot_nt(a, b):
    return lax.dot_general(a, b, (((1,), (1,)), ((), ())), preferred_element_type=F32)


def _params(*sem):
    return pltpu.CompilerParams(dimension_semantics=sem, vmem_limit_bytes=VMEM_LIMIT)


def _rmsnorm_kernel(x_ref, g_ref, o_ref):
    x = x_ref[...]
    ms = jnp.mean(x * x, axis=-1, keepdims=True)
    o_ref[...] = (x * lax.rsqrt(ms + RMS_EPS) * g_ref[...]).astype(o_ref.dtype)


def _rmsnorm(x, g, out_dtype, tm=512):
    s, d = x.shape
    return pl.pallas_call(
        _rmsnorm_kernel,
        out_shape=jax.ShapeDtypeStruct((s, d), out_dtype),
        grid=(s // tm,),
        in_specs=[pl.BlockSpec((tm, d), lambda i: (i, 0)), pl.BlockSpec((1, d), lambda i: (0, 0))],
        out_specs=pl.BlockSpec((tm, d), lambda i: (i, 0)),
        compiler_params=_params("parallel"),
    )(x, g.reshape(1, d))


def _mm_scale_kernel(a_ref, b_ref, s_ref, o_ref):
    o_ref[...] = (_dot(a_ref[...], b_ref[...]) * s_ref[...]).astype(o_ref.dtype)


def _mm_sigmoid_kernel(a_ref, b_ref, o_ref):
    o_ref[...] = jax.nn.sigmoid(_dot(a_ref[...], b_ref[...])).astype(o_ref.dtype)


def _mm_resid_kernel(a_ref, b_ref, r_ref, o_ref):
    o_ref[...] = r_ref[...] + _dot(a_ref[...], b_ref[...])


def _mm_swiglu_kernel(a_ref, b1_ref, b2_ref, o_ref):
    a = a_ref[...]
    za = _dot(a, b1_ref[...])
    zb = _dot(a, b2_ref[...])
    o_ref[...] = (jax.nn.silu(za) * zb).astype(o_ref.dtype)


def _mm_glu_resid_kernel(a_ref, b1_ref, b2_ref, r_ref, o_ref):
    a = a_ref[...]
    za = _dot(a, b1_ref[...])
    zb = _dot(a, b2_ref[...])
    o_ref[...] = r_ref[...] + za * jax.nn.sigmoid(zb)


def _matmul(kern, a, b, n_out, out_dtype, tm, tn, *, b_halves=1, row_extra=None, tile_extra=None):
    m, k = a.shape
    nj = n_out // tn
    in_specs = [pl.BlockSpec((tm, k), lambda i, j: (i, 0))]
    args = [a]
    for half in range(b_halves):
        in_specs.append(pl.BlockSpec((k, tn), functools.partial(lambda i, j, o: (0, j + o), o=half * nj)))
        args.append(b)
    if row_extra is not None:
        in_specs.append(pl.BlockSpec((1, tn), lambda i, j: (0, j)))
        args.append(row_extra)
    if tile_extra is not None:
        in_specs.append(pl.BlockSpec((tm, tn), lambda i, j: (i, j)))
        args.append(tile_extra)
    return pl.pallas_call(
        kern,
        out_shape=jax.ShapeDtypeStruct((m, n_out), out_dtype),
        grid=(m // tm, nj),
        in_specs=in_specs,
        out_specs=pl.BlockSpec((tm, tn), lambda i, j: (i, j)),
        compiler_params=_params("parallel", "arbitrary"),
    )(*args)


def _swiglu_ffn(h, g, w_in, w_out):
    hn = _rmsnorm(h, g, BF16)
    act = _matmul(_mm_swiglu_kernel, hn, w_in.astype(BF16), D_FF, BF16, 1024, 512, b_halves=2)
    return _matmul(_mm_resid_kernel, act, w_out.astype(BF16), D_MODEL, F32, 512, 512, tile_extra=h)


def _compress_kernel(r_ref, w1_ref, w2_ref, pos_ref, o_ref):
    r = r_ref[0, 0]
    half = CMP_STRIDE * HEAD_DIM
    top = _dot(r, w1_ref[0, :half, :])
    bot = _dot(r, w1_ref[0, half:, :])
    posb = _dot(pos_ref[0], w1_ref[0])[0:1]
    ncp = r.shape[0]
    hid = top + pltpu.roll(bot, ncp - 1, 0) + posb
    o_ref[0, 0] = _dot(jax.nn.gelu(hid).astype(BF16), w2_ref[0]).astype(o_ref.dtype)


def _compress(r, w1, w2, pos):
    _, g, ncp, width = r.shape
    return pl.pallas_call(
        _compress_kernel,
        out_shape=jax.ShapeDtypeStruct((2, g, ncp, HEAD_DIM), BF16),
        grid=(2, g),
        in_specs=[
            pl.BlockSpec((1, 1, ncp, width), lambda kv, gi: (kv, gi, 0, 0)),
            pl.BlockSpec((1, 2 * width, CMP_HIDDEN), lambda kv, gi: (kv, 0, 0)),
            pl.BlockSpec((1, CMP_HIDDEN, HEAD_DIM), lambda kv, gi: (kv, 0, 0)),
            pl.BlockSpec((1, SUBLANES, 2 * width), lambda kv, gi: (kv, 0, 0)),
        ],
        out_specs=pl.BlockSpec((1, 1, ncp, HEAD_DIM), lambda kv, gi: (kv, gi, 0, 0)),
        compiler_params=_params("parallel", "parallel"),
    )(r, w1, w2, pos)


def _cmp_attn_kernel(q_ref, kcc_ref, vcct_ref, band_ref, mt_ref, oc_ref, nm_ref, s_scr, *, tq, nb):
    i = pl.program_id(1)
    band_start = pl.multiple_of(i * (tq // CMP_STRIDE), SUBLANES)
    kcc = kcc_ref[0]
    vcct = vcct_ref[0]
    n_rows = kcc.shape[0]
    rows = lax.broadcasted_iota(jnp.int32, (n_rows, tq), 0)
    row_bias = jnp.where(rows >= CMP_PAD, jnp.where(rows < band_start + nb, 0.0, NEG), NEG)
    imp = jnp.zeros((n_rows, tq), F32)
    for h in range(HPG):
        qh = q_ref[:, h * HEAD_DIM:(h + 1) * HEAD_DIM]
        s_scr[...] = _dot_nt(kcc, qh) + row_bias
        s_scr[pl.ds(band_start, nb), :] += band_ref[h]
        s = s_scr[...]
        m = jnp.max(s, axis=0, keepdims=True)
        e = jnp.where(s > 0.5 * NEG, jnp.exp2(s - m), 0.0)
        l = jnp.sum(e, axis=0, keepdims=True)
        pn = e * (1.0 / jnp.where(l > 0.0, l, 1.0))
        imp = imp + pn
        oct_h = _dot(vcct, pn.astype(BF16))
        oc_ref[:, h * HEAD_DIM:(h + 1) * HEAD_DIM] = oct_h.T
    slc = jnp.dot(mt_ref[...], imp, precision=lax.Precision.HIGHEST, preferred_element_type=F32)
    n_blk = slc.shape[0]
    jj = lax.broadcasted_iota(jnp.int32, (n_blk, tq), 0)
    t_blk = (i * tq + lax.broadcasted_iota(jnp.int32, (n_blk, tq), 1)) // SEL_BLOCK
    forced = jnp.where(jj == 0, 1, jnp.where(jj == t_blk, 1, jnp.where(jj == t_blk - 1, 1, 0)))
    score = jnp.where(forced == 1, FORCE_SCORE, slc)
    score = jnp.where(jj <= t_blk, score, NEG)
    picked = jnp.zeros((n_blk, tq), F32)
    for _ in range(SEL_TOPK):
        best = jnp.max(score, axis=0, keepdims=True)
        first = jnp.min(jnp.where(score == best, jj, n_blk), axis=0, keepdims=True)
        hit = jj == first
        picked = jnp.where(hit, 1.0, picked)
        score = jnp.where(hit, -jnp.inf, score)
    neg_mask = jnp.where(picked > 0.5, jnp.where(jj <= t_blk, 0.0, NEG), NEG)
    nm_ref[...] = neg_mask.T.astype(nm_ref.dtype)


def _cmp_attn(proj, kcc, vcct, band, mt, tq):
    s = proj.shape[0]
    n_rows = kcc.shape[1]
    nb = band.shape[1]
    kern = functools.partial(_cmp_attn_kernel, tq=tq, nb=nb)
    return pl.pallas_call(
        kern,
        out_shape=(jax.ShapeDtypeStruct((s, Q_WIDTH), F32),
                   jax.ShapeDtypeStruct((s, N_KV_GROUPS * LANES), BF16)),
        grid=(N_KV_GROUPS, s // tq),
        in_specs=[
            pl.BlockSpec((tq, HPG * HEAD_DIM), lambda g, i: (i, g)),
            pl.BlockSpec((1, n_rows, HEAD_DIM), lambda g, i: (g, 0, 0)),
            pl.BlockSpec((1, HEAD_DIM, n_rows), lambda g, i: (g, 0, 0)),
            pl.BlockSpec((HPG, nb, tq), lambda g, i: (g, 0, 0)),
            pl.BlockSpec((LANES, n_rows), lambda g, i: (0, 0)),
        ],
        out_specs=(pl.BlockSpec((tq, HPG * HEAD_DIM), lambda g, i: (i, g)),
                   pl.BlockSpec((tq, LANES), lambda g, i: (i, g))),
        scratch_shapes=[pltpu.VMEM((n_rows, tq), F32)],
        compiler_params=_params("parallel", "parallel"),
    )(proj, kcc, vcct, band, mt)


def _lane_tile(x, width):
    return jnp.concatenate([x] * (width // x.shape[1]), axis=1)


def _softmax_update(s, v_aug, m_ref, acc_ref, rows):
    m_prev = m_ref[rows, :]
    m_next = jnp.maximum(m_prev, jnp.max(s, axis=1, keepdims=True))
    alpha = jnp.exp2(m_prev - m_next)
    p = jnp.exp2(s - _lane_tile(m_next, s.shape[1]))
    acc_ref[rows, :] = _lane_tile(alpha, acc_ref.shape[1]) * acc_ref[rows, :] + _dot(p.astype(BF16), v_aug)
    m_ref[rows, :] = m_next


def _softmax_once(s, v_aug):
    m = jnp.broadcast_to(jnp.max(s, axis=1, keepdims=True), (s.shape[0], LANES))
    p = jnp.exp2(s - _lane_tile(m, s.shape[1]))
    return _dot(p.astype(BF16), v_aug)


def _sel_win_kernel(q_ref, nm_ref, kaug_ref, vs_ref, kw_ref, vw_ref, bias_ref, g_ref, oc_ref, o_ref,
                    q4, m_s, a_s, a_w, *, tile):
    i = pl.program_id(1)
    t = tile
    for h in range(HPG):
        q4[h * t:(h + 1) * t, :HEAD_DIM] = q_ref[:, h * HEAD_DIM:(h + 1) * HEAD_DIM]
        q4[h * t:(h + 1) * t, HEAD_DIM:] = nm_ref[...]
    m_s[...] = jnp.full(m_s.shape, -3e38, F32)
    a_s[...] = jnp.zeros(a_s.shape, F32)

    def sel_step(row0, n, first):
        keys = pl.ds(row0, n)
        for h in range(HPG):
            rows = slice(h * t, (h + 1) * t)
            s = _dot_nt(q4[rows, :], kaug_ref[0, keys, :])
            if first is not None:
                s = s + bias_ref[h, :, first * t:]
            _softmax_update(s, vs_ref[0, keys, :], m_s, a_s, rows)

    def win_once(row0, n, first):
        keys = pl.ds(row0, n)
        for h in range(HPG):
            rows = slice(h * t, (h + 1) * t)
            s = _dot_nt(q4[rows, :HEAD_DIM], kw_ref[keys, :]) + bias_ref[h, :, first * t:]
            a_w[rows, :] = _softmax_once(s, vw_ref[0, keys, :])

    n_far = jnp.maximum(i - 1, 0)

    def far_chunk(c, carry):
        sel_step(pl.multiple_of(c * 2 * t, 2 * t), 2 * t, None)
        return carry

    lax.fori_loop(0, n_far // 2, far_chunk, 0)

    @pl.when(n_far % 2 == 1)
    def _():
        sel_step(pl.multiple_of((i - 2) * t, t), t, None)

    @pl.when(i == 0)
    def _():
        sel_step(0, t, 2)
        win_once(0, t, 2)

    @pl.when(i >= 1)
    def _():
        sel_step(pl.multiple_of((i - 1) * t, t), 2 * t, 1)

    @pl.when(i == 1)
    def _():
        win_once(0, 2 * t, 1)

    @pl.when(i >= 2)
    def _():
        win_once(pl.multiple_of((i - 2) * t, t), 3 * t, 0)

    acc_s = a_s[...]
    acc_w = a_w[...]
    o_s = acc_s[:, :HEAD_DIM] * (1.0 / acc_s[:, HEAD_DIM:])
    o_w = acc_w[:, :HEAD_DIM] * (1.0 / acc_w[:, HEAD_DIM:])
    gates = g_ref[...]
    for h in range(HPG):
        cols = slice(h * HEAD_DIM, (h + 1) * HEAD_DIM)
        o = (gates[:, h:h + 1] * oc_ref[:, cols]
             + gates[:, HPG + h:HPG + h + 1] * o_s[h * t:(h + 1) * t]
             + gates[:, 2 * HPG + h:2 * HPG + h + 1] * o_w[h * t:(h + 1) * t])
        o_ref[:, cols] = o.astype(o_ref.dtype)


def _sel_win(proj, neg_mask, kaug, vs_aug, vw_aug, bias, gates, o_c, tile):
    s = proj.shape[0]
    kw_blk = (Q_WIDTH + 4 * KV_WIDTH) // HEAD_DIM
    rows4 = HPG * tile
    kern = functools.partial(_sel_win_kernel, tile=tile)
    return pl.pallas_call(
        kern,
        out_shape=jax.ShapeDtypeStruct((s, Q_WIDTH), BF16),
        grid=(N_KV_GROUPS, s // tile),
        in_specs=[
            pl.BlockSpec((tile, HPG * HEAD_DIM), lambda g, i: (i, g)),
            pl.BlockSpec((tile, LANES), lambda g, i: (i, g)),
            pl.BlockSpec((1, s, 2 * HEAD_DIM), lambda g, i: (g, 0, 0)),
            pl.BlockSpec((1, s, 2 * HEAD_DIM), lambda g, i: (g, 0, 0)),
            pl.BlockSpec((s, HEAD_DIM), lambda g, i: (0, kw_blk + g)),
            pl.BlockSpec((1, s, 2 * HEAD_DIM), lambda g, i: (g, 0, 0)),
            pl.BlockSpec((HPG, tile, 3 * tile), lambda g, i: (g, 0, 0)),
            pl.BlockSpec((tile, LANES), lambda g, i: (i, g)),
            pl.BlockSpec((tile, HPG * HEAD_DIM), lambda g, i: (i, g)),
        ],
        out_specs=pl.BlockSpec((tile, HPG * HEAD_DIM), lambda g, i: (i, g)),
        scratch_shapes=[
            pltpu.VMEM((rows4, 2 * HEAD_DIM), BF16),
            pltpu.VMEM((rows4, LANES), F32),
            pltpu.VMEM((rows4, 2 * HEAD_DIM), F32),
            pltpu.VMEM((rows4, 2 * HEAD_DIM), F32),
        ],
        compiler_params=_params("parallel", "arbitrary"),
    )(proj, neg_mask, kaug, vs_aug, proj, vw_aug, bias, gates, o_c)


def _rel_bucket_np(dist):
    n = np.maximum(dist, 0)
    max_exact = REL_BUCKETS // 2
    large = max_exact + (np.log(np.maximum(n, 1).astype(np.float32) / np.float32(max_exact))
                         / np.float32(math.log(REL_MAX_DIST / max_exact))
                         * np.float32(REL_BUCKETS - max_exact)).astype(np.int32)
    large = np.minimum(large, REL_BUCKETS - 1)
    return np.where(n < max_exact, n, large).astype(np.int32)


def _far_distance():
    d = np.arange(4 * REL_MAX_DIST)
    b = _rel_bucket_np(d)
    assert b[-1] == REL_BUCKETS - 1
    return int(np.max(np.nonzero(b != REL_BUCKETS - 1)[0])) + 1


def _bias_table(rel_bias, dist):
    bucket = jnp.asarray(_rel_bucket_np(dist))
    shifted = (rel_bias[bucket] - rel_bias[REL_BUCKETS - 1]) * LOG2E
    tab = jnp.where(jnp.asarray(dist >= 0)[..., None], shifted, NEG)
    return jnp.moveaxis(tab, -1, 0).astype(F32)


def _toeplitz_bias(rel_bias, offset, t):
    x = np.arange(2 * t)
    period = _bias_table(rel_bias, np.where(x < t, offset - x, offset + 2 * t - x))
    tiled = jnp.tile(period, (1, t))[:, :t * (2 * t - 1)]
    return tiled.reshape(-1, t, 2 * t - 1)[:, :, :t]


def _nsa_mixer(h, hn, rel_bias, w_in, w_out, pos_k, w1_k, w2_k, pos_v, w1_v, w2_v):
    s = h.shape[0]
    n_sel = s // SEL_BLOCK
    ncp = s // CMP_STRIDE
    tile = ATT_TILE
    far = _far_distance()
    assert n_sel <= LANES and s % CMP_TQ == 0 and s % tile == 0
    assert far <= tile and far <= CMP_STRIDE * CMP_PAD - CMP_BLOCK + 1 + CMP_STRIDE

    col_scale = jnp.where(jnp.arange(PROJ_MAIN) < Q_WIDTH, HEAD_DIM ** -0.5 * LOG2E, 1.0).astype(F32)[None, :]
    proj = _matmul(_mm_scale_kernel, hn, w_in[:, :PROJ_MAIN].astype(BF16), PROJ_MAIN, BF16, 1024, 512,
                   row_extra=col_scale)
    w_gate = w_in[:, PROJ_MAIN:].reshape(D_MODEL, N_GATES, N_KV_GROUPS, HPG).transpose(0, 2, 1, 3)
    w_gate = w_gate.reshape(D_MODEL, N_KV_GROUPS, N_GATES * HPG)
    w_gate = jnp.pad(w_gate, ((0, 0), (0, 0), (0, LANES - N_GATES * HPG))).reshape(D_MODEL, N_KV_GROUPS * LANES)
    gates = _matmul(_mm_sigmoid_kernel, hn, w_gate.astype(BF16), N_KV_GROUPS * LANES, F32, 1024, 512)

    def stride_rows(col0):
        a = proj[:, col0:col0 + KV_WIDTH].reshape(ncp, CMP_STRIDE, N_KV_GROUPS, HEAD_DIM)
        return a.transpose(2, 0, 1, 3).reshape(N_KV_GROUPS, ncp, CMP_STRIDE * HEAD_DIM)

    r = jnp.stack([stride_rows(Q_WIDTH), stride_rows(Q_WIDTH + KV_WIDTH)])
    w1 = jnp.stack([w1_k, w1_v]).astype(BF16)
    w2 = jnp.stack([w2_k, w2_v]).astype(BF16)
    pos = jnp.stack([pos_k.reshape(1, -1), pos_v.reshape(1, -1)]).astype(BF16)
    pos = jnp.broadcast_to(pos, (2, SUBLANES, CMP_BLOCK * HEAD_DIM))
    cc = _compress(r, w1, w2, pos)
    kcc = jnp.pad(cc[0], ((0, 0), (CMP_PAD, 0), (0, 0)))
    vcct = jnp.pad(cc[1], ((0, 0), (CMP_PAD, 0), (0, 0))).transpose(0, 2, 1)
    n_rows = ncp + CMP_PAD

    nb = CMP_TQ // CMP_STRIDE + CMP_PAD
    b_idx = np.arange(nb)[:, None]
    t_idx = np.arange(CMP_TQ)[None, :]
    band = _bias_table(rel_bias, t_idx - CMP_STRIDE * (b_idx - CMP_PAD) - (CMP_BLOCK - 1))
    ratio = SEL_BLOCK // CMP_STRIDE
    lo = CMP_BLOCK // CMP_STRIDE - 1
    c_of_row = np.arange(n_rows)[None, :] - CMP_PAD
    j_of = np.arange(LANES)[:, None]
    mt = ((c_of_row >= ratio * j_of - lo) & (c_of_row <= ratio * j_of + ratio - 1)
          & (c_of_row >= 0) & (c_of_row <= ncp - 2) & (j_of < n_sel)).astype(np.float32)
    o_c, neg_mask = _cmp_attn(proj, kcc, vcct, band, jnp.asarray(mt), CMP_TQ)

    def per_group(col0):
        return proj[:, col0:col0 + KV_WIDTH].reshape(s, N_KV_GROUPS, HEAD_DIM).transpose(1, 0, 2)

    onehot = (np.arange(s)[:, None] // SEL_BLOCK == np.arange(LANES)[None, :]).astype(np.float32)
    onehot = jnp.broadcast_to(jnp.asarray(onehot, BF16), (N_KV_GROUPS, s, LANES))
    kaug = jnp.concatenate([per_group(Q_WIDTH + 2 * KV_WIDTH), onehot], axis=-1)
    ones = jnp.ones((N_KV_GROUPS, s, LANES), BF16)
    vs_aug = jnp.concatenate([per_group(Q_WIDTH + 3 * KV_WIDTH), ones], axis=-1)
    vw_aug = jnp.concatenate([per_group(Q_WIDTH + 5 * KV_WIDTH), ones], axis=-1)
    tl = np.arange(tile)[:, None]
    kl = np.arange(tile)[None, :]
    edge = jnp.broadcast_to(jnp.asarray(np.where(tl < kl, 0.0, NEG), F32), (N_HEADS, tile, tile))
    bias = jnp.concatenate([edge, _toeplitz_bias(rel_bias, tile, tile), _toeplitz_bias(rel_bias, 0, tile)], axis=-1)
    o = _sel_win(proj, neg_mask, kaug, vs_aug, vw_aug, bias, gates, o_c, tile)
    return _matmul(_mm_resid_kernel, o, w_out.astype(BF16), D_MODEL, F32, 1024, 512, tile_extra=h)


def _s5_kernel(u_ref, lam_ref, bt_ref, c_ref, d_ref, y_ref, *, chunk):
    L = chunk
    half = L // 2
    u = u_ref[0]
    n_chunks = u.shape[0]
    lam = lam_ref[0]
    a_re, a_im = lam[0:1], lam[1:2]
    dt = jnp.exp(lam[2:3])
    log_re, log_im = a_re * dt, a_im * dt

    def cpow(e):
        mag = jnp.exp(e * log_re)
        return mag * jnp.cos(e * log_im), mag * jnp.sin(e * log_im)

    def cmul(xr, xi, yr, yi):
        return xr * yr - xi * yi, xr * yi + xi * yr

    consts = jnp.where(lax.broadcasted_iota(jnp.int32, (SUBLANES, 1), 0) == 0, 1.0,
                       jnp.where(lax.broadcasted_iota(jnp.int32, (SUBLANES, 1), 0) == 1, float(half + 1),
                                 jnp.where(lax.broadcasted_iota(jnp.int32, (SUBLANES, 1), 0) == 2, float(half - 1),
                                           float(L))))
    kr, ki = cpow(consts)
    ab_r, ab_i = kr[0:1], ki[0:1]
    pr, pi = ab_r - 1.0, ab_i
    den = a_re * a_re + a_im * a_im
    cf_r = (pr * a_re + pi * a_im) / den
    cf_i = (pi * a_re - pr * a_im) / den
    bb_r, bb_i = cmul(cf_r, cf_i, bt_ref[0, 0], bt_ref[0, 1])
    c_r, c_i = c_ref[0, 0], c_ref[0, 1]

    tau = lax.broadcasted_iota(jnp.int32, (L, 1), 0).astype(F32)
    e1r, e1i = cpow(tau - half)
    e2r, e2i = cpow(half - tau)
    e3r, e3i = cmul(e1r, e1i, kr[1:2], ki[1:2])
    e4r, e4i = cmul(e2r, e2i, kr[2:3], ki[2:3])

    def outer(er, ei, wr, wi):
        xr, xi = cmul(er[:, None, :], ei[:, None, :], wr[None, :, :], wi[None, :, :])
        return xr.reshape(L * SSM_GROUP, SSM_STATE), xi.reshape(L * SSM_GROUP, SSM_STATE)

    qm_r, qm_i = outer(e1r, e1i, c_r, c_i)
    km_r, km_i = outer(e2r, e2i, bb_r, bb_i)
    qc_r, qc_i = outer(e3r, e3i, c_r, c_i)
    wz_r, wz_i = outer(e4r, e4i, bb_r, bb_i)

    width = L * SSM_GROUP
    tt = _dot_nt(km_r.astype(BF16), qm_r.astype(BF16)) - _dot_nt(km_i.astype(BF16), qm_i.astype(BF16))
    src = lax.broadcasted_iota(jnp.int32, (width, width), 0) // SSM_GROUP
    dst = lax.broadcasted_iota(jnp.int32, (width, width), 1) // SSM_GROUP
    tt = jnp.where(src <= dst, tt, 0.0).astype(BF16)
    ub = u.astype(BF16)
    y = _dot(ub, tt)

    x_r = _dot(ub, wz_r.astype(BF16))
    x_i = _dot(ub, wz_i.astype(BF16))
    row = lax.broadcasted_iota(jnp.int32, (n_chunks, SSM_STATE), 0)
    p_r, p_i = kr[3:4], ki[3:4]
    d = 1
    while d < n_chunks:
        s_r = jnp.where(row >= d, pltpu.roll(x_r, d, 0), 0.0)
        s_i = jnp.where(row >= d, pltpu.roll(x_i, d, 0), 0.0)
        inc_r, inc_i = cmul(p_r, p_i, s_r, s_i)
        x_r, x_i = x_r + inc_r, x_i + inc_i
        p_r, p_i = cmul(p_r, p_i, p_r, p_i)
        d *= 2
    prev_r = jnp.where(row >= 1, pltpu.roll(x_r, 1, 0), 0.0)
    prev_i = jnp.where(row >= 1, pltpu.roll(x_i, 1, 0), 0.0)
    y = y + _dot_nt(prev_r.astype(BF16), qc_r.astype(BF16)) - _dot_nt(prev_i.astype(BF16), qc_i.astype(BF16))
    y = y + d_ref[0] * u
    y_ref[0] = jax.nn.gelu(y).astype(y_ref.dtype)


def _s5_mixer(h, hn, a_re, a_im, log_dt, b_re, b_im, c_re, c_im, d_skip, w_glu):
    s = h.shape[0]
    L = S5_CHUNK
    n_chunks = s // L
    width = L * SSM_GROUP
    u = hn.reshape(n_chunks, L, SSM_GROUPS, SSM_GROUP).transpose(2, 0, 1, 3).reshape(SSM_GROUPS, n_chunks, width)
    lam = jnp.stack([a_re, a_im, jnp.broadcast_to(log_dt[:, None], a_re.shape)], axis=1).astype(F32)
    bt = jnp.stack([b_re.transpose(0, 2, 1), b_im.transpose(0, 2, 1)], axis=1).astype(F32)
    cc = jnp.stack([c_re, c_im], axis=1).astype(F32)
    dd = jnp.tile(d_skip.astype(F32).reshape(SSM_GROUPS, 1, SSM_GROUP), (1, 1, L))
    y = pl.pallas_call(
        functools.partial(_s5_kernel, chunk=L),
        out_shape=jax.ShapeDtypeStruct((SSM_GROUPS, n_chunks, width), BF16),
        grid=(SSM_GROUPS,),
        in_specs=[
            pl.BlockSpec((1, n_chunks, width), lambda g: (g, 0, 0)),
            pl.BlockSpec((1, 3, SSM_STATE), lambda g: (g, 0, 0)),
            pl.BlockSpec((1, 2, SSM_GROUP, SSM_STATE), lambda g: (g, 0, 0, 0)),
            pl.BlockSpec((1, 2, SSM_GROUP, SSM_STATE), lambda g: (g, 0, 0, 0)),
            pl.BlockSpec((1, 1, width), lambda g: (g, 0, 0)),
        ],
        out_specs=pl.BlockSpec((1, n_chunks, width), lambda g: (g, 0, 0)),
        compiler_params=_params("parallel"),
    )(u, lam, bt, cc, dd)
    y = y.reshape(SSM_GROUPS, n_chunks, L, SSM_GROUP).transpose(1, 2, 0, 3).reshape(s, D_MODEL)
    return _matmul(_mm_glu_resid_kernel, y, w_glu.astype(BF16), D_MODEL, F32, 1024, 512, b_halves=2, tile_extra=h)


def kernel(x, rel_bias, mix_norm_g, ffn_norm_g, final_norm_g, nsa_w_in, nsa_w_out, cmp_pos_k, cmp_w1_k, cmp_w2_k, cmp_pos_v, cmp_w1_v, cmp_w2_v, s5_A_re, s5_A_im, s5_log_dt, s5_B_re, s5_B_im, s5_C_re, s5_C_im, s5_D, s5_w_glu, ffn_w_in, ffn_w_out):
    assert x.shape[0] == 1
    h = x[0]
    hn = _rmsnorm(h, mix_norm_g[0], BF16)
    h = _nsa_mixer(h, hn, rel_bias, nsa_w_in[0], nsa_w_out[0], cmp_pos_k[0], cmp_w1_k[0], cmp_w2_k[0],
                   cmp_pos_v[0], cmp_w1_v[0], cmp_w2_v[0])
    h = _swiglu_ffn(h, ffn_norm_g[0], ffn_w_in[0], ffn_w_out[0])
    hn = _rmsnorm(h, mix_norm_g[1], F32)
    h = _s5_mixer(h, hn, s5_A_re[0], s5_A_im[0], s5_log_dt[0], s5_B_re[0], s5_B_im[0], s5_C_re[0], s5_C_im[0],
                  s5_D[0], s5_w_glu[0])
    h = _swiglu_ffn(h, ffn_norm_g[1], ffn_w_in[1], ffn_w_out[1])
    return _rmsnorm(h, final_norm_g, x.dtype)[None]
```

```python
import functools
import math

import numpy as np
import jax
import jax.numpy as jnp
from jax import lax
from jax.experimental import pallas as pl
from jax.experimental.pallas import tpu as pltpu

D_MODEL = 2048
N_HEADS = 16
HEAD_DIM = 128
N_KV_GROUPS = 4
HPG = N_HEADS // N_KV_GROUPS
CMP_BLOCK = 32
CMP_STRIDE = 16
CMP_HIDDEN = 2 * HEAD_DIM
SEL_BLOCK = 64
SEL_TOPK = 16
WINDOW = 512
N_GATES = 3
KV_WIDTH = N_KV_GROUPS * HEAD_DIM
Q_WIDTH = N_HEADS * HEAD_DIM
PROJ_MAIN = Q_WIDTH + 6 * KV_WIDTH
REL_BUCKETS = 32
REL_MAX_DIST = 128
SSM_GROUP = 16
SSM_GROUPS = D_MODEL // SSM_GROUP
SSM_STATE = 64
D_FF = ((8 * D_MODEL + 2) // 3 + 255) // 256 * 256
RMS_EPS = 1e-6
NEG = -1e30
FORCE_SCORE = 1e9
LOG2E = math.log2(math.e)

LANES = 128
SUBLANES = 8
VMEM_LIMIT = 52 * 1024 * 1024

ATT_TILE = WINDOW // 2
FAR_TILES = 4
CMP_TQ = 256
CMP_PAD = 8
S5_CHUNK = 32

BF16 = jnp.bfloat16
F32 = jnp.float32


def _dot(a, b):
    return jnp.dot(a, b, preferred_element_type=F32)


def _dot_nt(a, b):
    return lax.dot_general(a, b, (((1,), (1,)), ((), ())), preferred_element_type=F32)


def _params(*sem):
    return pltpu.CompilerParams(dimension_semantics=sem, vmem_limit_bytes=VMEM_LIMIT)


def _rmsnorm_kernel(x_ref, g_ref, o_ref):
    x = x_ref[...]
    ms = jnp.mean(x * x, axis=-1, keepdims=True)
    o_ref[...] = (x * lax.rsqrt(ms + RMS_EPS) * g_ref[...]).astype(o_ref.dtype)


def _rmsnorm(x, g, out_dtype, tm=512):
    s, d = x.shape
    return pl.pallas_call(
        _rmsnorm_kernel,
        out_shape=jax.ShapeDtypeStruct((s, d), out_dtype),
        grid=(s // tm,),
        in_specs=[pl.BlockSpec((tm, d), lambda i: (i, 0)), pl.BlockSpec((1, d), lambda i: (0, 0))],
        out_specs=pl.BlockSpec((tm, d), lambda i: (i, 0)),
        compiler_params=_params("parallel"),
    )(x, g.reshape(1, d))


def _mm_kernel(*refs, n_b, n_extra, epilogue):
    a_ref = refs[0]
    b_refs = refs[1:1 + n_b]
    extra_refs = refs[1 + n_b:1 + n_b + n_extra]
    o_ref = refs[1 + n_b + n_extra]
    w_scr = refs[2 + n_b + n_extra:]

    @pl.when(pl.program_id(1) == 0)
    def _():
        for b_ref, w in zip(b_refs, w_scr):
            w[...] = b_ref[...].astype(BF16)

    a = a_ref[...]
    z = [_dot(a, w[...]) for w in w_scr]
    o_ref[...] = epilogue(*z, *[e[...] for e in extra_refs]).astype(o_ref.dtype)


def _ep_scale(z, scale):
    return z * scale


def _ep_resid(z, resid):
    return resid + z


def _ep_swiglu(za, zb):
    return jax.nn.silu(za) * zb


def _ep_glu_resid(za, zb, resid):
    return resid + za * jax.nn.sigmoid(zb)


def _matmul(epilogue, a, b, n_out, out_dtype, tm, tn, *, b_halves=1, row_extra=None, tile_extra=None):
    m, k = a.shape
    nj = n_out // tn
    b_mode = pl.Buffered(1) if 2 * b_halves * k * tn * 4 > VMEM_LIMIT // 3 else None
    in_specs = [pl.BlockSpec((tm, k), lambda j, i: (i, 0))]
    args = [a]
    for half in range(b_halves):
        in_specs.append(pl.BlockSpec((k, tn), functools.partial(lambda j, i, o: (0, j + o), o=half * nj),
                                     pipeline_mode=b_mode))
        args.append(b)
    extras = []
    if row_extra is not None:
        in_specs.append(pl.BlockSpec((1, tn), lambda j, i: (0, j)))
        extras.append(row_extra)
    if tile_extra is not None:
        in_specs.append(pl.BlockSpec((tm, tn), lambda j, i: (i, j)))
        extras.append(tile_extra)
    kern = functools.partial(_mm_kernel, n_b=b_halves, n_extra=len(extras), epilogue=epilogue)
    return pl.pallas_call(
        kern,
        out_shape=jax.ShapeDtypeStruct((m, n_out), out_dtype),
        grid=(nj, m // tm),
        in_specs=in_specs,
        out_specs=pl.BlockSpec((tm, tn), lambda j, i: (i, j)),
        scratch_shapes=[pltpu.VMEM((k, tn), BF16) for _ in range(b_halves)],
        compiler_params=_params("arbitrary", "arbitrary"),
    )(*args, *extras)


def _swiglu_ffn(h, g, w_in, w_out):
    hn = _rmsnorm(h, g, BF16)
    act = _matmul(_ep_swiglu, hn, w_in, D_FF, BF16, 1024, 512, b_halves=2)
    return _matmul(_ep_resid, act, w_out, D_MODEL, F32, 512, 512, tile_extra=h)


def _compress_kernel(r_ref, w1_ref, w2_ref, pos_ref, o_ref):
    r = r_ref[0, 0]
    half = CMP_STRIDE * HEAD_DIM
    top = _dot(r, w1_ref[0, :half, :])
    bot = _dot(r, w1_ref[0, half:, :])
    posb = _dot(pos_ref[0], w1_ref[0])[0:1]
    ncp = r.shape[0]
    hid = top + pltpu.roll(bot, ncp - 1, 0) + posb
    o_ref[0, 0] = _dot(jax.nn.gelu(hid).astype(BF16), w2_ref[0]).astype(o_ref.dtype)


def _compress(r, w1, w2, pos):
    _, g, ncp, width = r.shape
    return pl.pallas_call(
        _compress_kernel,
        out_shape=jax.ShapeDtypeStruct((2, g, ncp, HEAD_DIM), BF16),
        grid=(2, g),
        in_specs=[
            pl.BlockSpec((1, 1, ncp, width), lambda kv, gi: (kv, gi, 0, 0)),
            pl.BlockSpec((1, 2 * width, CMP_HIDDEN), lambda kv, gi: (kv, 0, 0)),
            pl.BlockSpec((1, CMP_HIDDEN, HEAD_DIM), lambda kv, gi: (kv, 0, 0)),
            pl.BlockSpec((1, SUBLANES, 2 * width), lambda kv, gi: (kv, 0, 0)),
        ],
        out_specs=pl.BlockSpec((1, 1, ncp, HEAD_DIM), lambda kv, gi: (kv, gi, 0, 0)),
        compiler_params=_params("parallel", "parallel"),
    )(r, w1, w2, pos)


def _cmp_attn_kernel(q_ref, kcc_ref, vcct_ref, band_ref, mt_ref, oc_ref, nm_ref, s_scr, *, tq, nb):
    i = pl.program_id(1)
    band_start = pl.multiple_of(i * (tq // CMP_STRIDE), SUBLANES)
    kcc = kcc_ref[0]
    vcct = vcct_ref[0]
    n_rows = kcc.shape[0]
    rows = lax.broadcasted_iota(jnp.int32, (n_rows, tq), 0)
    row_bias = jnp.where(rows >= CMP_PAD, jnp.where(rows < band_start + nb, 0.0, NEG), NEG)
    imp = jnp.zeros((n_rows, tq), F32)
    for h in range(HPG):
        qh = q_ref[:, h * HEAD_DIM:(h + 1) * HEAD_DIM]
        s_scr[...] = _dot_nt(kcc, qh) + row_bias
        s_scr[pl.ds(band_start, nb), :] += band_ref[h]
        s = s_scr[...]
        m = jnp.max(s, axis=0, keepdims=True)
        e = jnp.where(s > 0.5 * NEG, jnp.exp2(s - m), 0.0)
        l = jnp.sum(e, axis=0, keepdims=True)
        pn = e * (1.0 / jnp.where(l > 0.0, l, 1.0))
        imp = imp + pn
        oct_h = _dot(vcct, pn.astype(BF16))
        oc_ref[:, h * HEAD_DIM:(h + 1) * HEAD_DIM] = oct_h.T
    slc = jnp.dot(mt_ref[...], imp, precision=lax.Precision.HIGHEST, preferred_element_type=F32)
    n_blk = slc.shape[0]
    jj = lax.broadcasted_iota(jnp.int32, (n_blk, tq), 0)
    t_blk = (i * tq + lax.broadcasted_iota(jnp.int32, (n_blk, tq), 1)) // SEL_BLOCK
    forced = jnp.where(jj == 0, 1, jnp.where(jj == t_blk, 1, jnp.where(jj == t_blk - 1, 1, 0)))
    score = jnp.where(forced == 1, FORCE_SCORE, slc)
    score = jnp.where(jj <= t_blk, score, NEG)
    picked = jnp.zeros((n_blk, tq), F32)
    for _ in range(SEL_TOPK):
        best = jnp.max(score, axis=0, keepdims=True)
        first = jnp.min(jnp.where(score == best, jj, n_blk), axis=0, keepdims=True)
        hit = jj == first
        picked = jnp.where(hit, 1.0, picked)
        score = jnp.where(hit, -jnp.inf, score)
    neg_mask = jnp.where(picked > 0.5, jnp.where(jj <= t_blk, 0.0, NEG), NEG)
    nm_ref[...] = neg_mask.T.astype(nm_ref.dtype)


def _cmp_attn(proj, kcc, vcct, band, mt, tq):
    s = proj.shape[0]
    n_rows = kcc.shape[1]
    nb = band.shape[1]
    kern = functools.partial(_cmp_attn_kernel, tq=tq, nb=nb)
    return pl.pallas_call(
        kern,
        out_shape=(jax.ShapeDtypeStruct((s, Q_WIDTH), F32),
                   jax.ShapeDtypeStruct((s, N_KV_GROUPS * LANES), BF16)),
        grid=(N_KV_GROUPS, s // tq),
        in_specs=[
            pl.BlockSpec((tq, HPG * HEAD_DIM), lambda g, i: (i, g)),
            pl.BlockSpec((1, n_rows, HEAD_DIM), lambda g, i: (g, 0, 0)),
            pl.BlockSpec((1, HEAD_DIM, n_rows), lambda g, i: (g, 0, 0)),
            pl.BlockSpec((HPG, nb, tq), lambda g, i: (g, 0, 0)),
            pl.BlockSpec((LANES, n_rows), lambda g, i: (0, 0)),
        ],
        out_specs=(pl.BlockSpec((tq, HPG * HEAD_DIM), lambda g, i: (i, g)),
                   pl.BlockSpec((tq, LANES), lambda g, i: (i, g))),
        scratch_shapes=[pltpu.VMEM((n_rows, tq), F32)],
        compiler_params=_params("parallel", "parallel"),
    )(proj, kcc, vcct, band, mt)


def _lane_tile(x, width):
    return jnp.concatenate([x] * (width // x.shape[1]), axis=1)


def _softmax_update(s, v_aug, m_ref, acc_ref, rows):
    m_prev = m_ref[rows, :]
    m_next = jnp.maximum(m_prev, jnp.max(s, axis=1, keepdims=True))
    alpha = jnp.exp2(m_prev - m_next)
    p = jnp.exp2(s - _lane_tile(m_next, s.shape[1]))
    acc_ref[rows, :] = _lane_tile(alpha, acc_ref.shape[1]) * acc_ref[rows, :] + _dot(p.astype(BF16), v_aug)
    m_ref[rows, :] = m_next


def _softmax_once(s, v_aug):
    m = jnp.broadcast_to(jnp.max(s, axis=1, keepdims=True), (s.shape[0], LANES))
    p = jnp.exp2(s - _lane_tile(m, s.shape[1]))
    return _dot(p.astype(BF16), v_aug)


def _sel_win_kernel(q_ref, nm_ref, ks_ref, vs_ref, kw_ref, vw_ref, blk_ref, bias_ref, g_ref, oc_ref, o_ref,
                    q4, m_s, a_s, a_w, *, tile):
    i = pl.program_id(1)
    t = tile
    for h in range(HPG):
        q4[h * t:(h + 1) * t, :HEAD_DIM] = q_ref[:, h * HEAD_DIM:(h + 1) * HEAD_DIM]
        q4[h * t:(h + 1) * t, HEAD_DIM:] = nm_ref[...]
    m_s[...] = jnp.full(m_s.shape, -3e38, F32)
    a_s[...] = jnp.zeros(a_s.shape, F32)

    def sel_step(row0, n, first):
        keys = pl.ds(row0, n)
        k_aug = jnp.concatenate([ks_ref[keys, :], blk_ref[keys, :]], axis=1)
        v_aug = jnp.concatenate([vs_ref[keys, :], jnp.ones((n, LANES), BF16)], axis=1)
        for h in range(HPG):
            rows = slice(h * t, (h + 1) * t)
            s = _dot_nt(q4[rows, :], k_aug)
            if first is not None:
                s = s + bias_ref[h, :, first * t:]
            _softmax_update(s, v_aug, m_s, a_s, rows)

    def win_once(row0, n, first):
        keys = pl.ds(row0, n)
        v_aug = jnp.concatenate([vw_ref[keys, :], jnp.ones((n, LANES), BF16)], axis=1)
        for h in range(HPG):
            rows = slice(h * t, (h + 1) * t)
            s = _dot_nt(q4[rows, :HEAD_DIM], kw_ref[keys, :]) + bias_ref[h, :, first * t:]
            a_w[rows, :] = _softmax_once(s, v_aug)

    n_far = jnp.maximum(i - 1, 0)

    def far_chunk(c, carry):
        sel_step(pl.multiple_of(c * FAR_TILES * t, FAR_TILES * t), FAR_TILES * t, None)
        return carry

    lax.fori_loop(0, n_far // FAR_TILES, far_chunk, 0)
    done = n_far // FAR_TILES * FAR_TILES
    width = FAR_TILES // 2
    while width >= 1:
        @pl.when((n_far & width) != 0)
        def _(done=done, width=width):
            sel_step(pl.multiple_of(done * t, t), width * t, None)
        done = done + (n_far & width)
        width //= 2

    @pl.when(i == 0)
    def _():
        sel_step(0, t, 2)
        win_once(0, t, 2)

    @pl.when(i >= 1)
    def _():
        sel_step(pl.multiple_of((i - 1) * t, t), 2 * t, 1)

    @pl.when(i == 1)
    def _():
        win_once(0, 2 * t, 1)

    @pl.when(i >= 2)
    def _():
        win_once(pl.multiple_of((i - 2) * t, t), 3 * t, 0)

    acc_s = a_s[...]
    acc_w = a_w[...]
    o_s = acc_s[:, :HEAD_DIM] * (1.0 / acc_s[:, HEAD_DIM:])
    o_w = acc_w[:, :HEAD_DIM] * (1.0 / acc_w[:, HEAD_DIM:])
    gates = g_ref[...]
    for h in range(HPG):
        cols = slice(h * HEAD_DIM, (h + 1) * HEAD_DIM)
        o = (gates[:, h:h + 1] * oc_ref[:, cols]
             + gates[:, HPG + h:HPG + h + 1] * o_s[h * t:(h + 1) * t]
             + gates[:, 2 * HPG + h:2 * HPG + h + 1] * o_w[h * t:(h + 1) * t])
        o_ref[:, cols] = o.astype(o_ref.dtype)


def _sel_win(proj, neg_mask, blk_onehot, bias, gates, o_c, tile):
    s = proj.shape[0]
    ks_blk = (Q_WIDTH + 2 * KV_WIDTH) // HEAD_DIM
    vs_blk = (Q_WIDTH + 3 * KV_WIDTH) // HEAD_DIM
    kw_blk = (Q_WIDTH + 4 * KV_WIDTH) // HEAD_DIM
    vw_blk = (Q_WIDTH + 5 * KV_WIDTH) // HEAD_DIM
    once = pl.Buffered(1)
    rows4 = HPG * tile
    kern = functools.partial(_sel_win_kernel, tile=tile)
    return pl.pallas_call(
        kern,
        out_shape=jax.ShapeDtypeStruct((s, Q_WIDTH), BF16),
        grid=(N_KV_GROUPS, s // tile),
        in_specs=[
            pl.BlockSpec((tile, HPG * HEAD_DIM), lambda g, i: (i, g)),
            pl.BlockSpec((tile, LANES), lambda g, i: (i, g)),
            pl.BlockSpec((s, HEAD_DIM), lambda g, i: (0, ks_blk + g), pipeline_mode=once),
            pl.BlockSpec((s, HEAD_DIM), lambda g, i: (0, vs_blk + g), pipeline_mode=once),
            pl.BlockSpec((s, HEAD_DIM), lambda g, i: (0, kw_blk + g), pipeline_mode=once),
            pl.BlockSpec((s, HEAD_DIM), lambda g, i: (0, vw_blk + g), pipeline_mode=once),
            pl.BlockSpec((s, LANES), lambda g, i: (0, 0), pipeline_mode=once),
            pl.BlockSpec((HPG, tile, 3 * tile), lambda g, i: (g, 0, 0), pipeline_mode=once),
            pl.BlockSpec((tile, LANES), lambda g, i: (i, g)),
            pl.BlockSpec((tile, HPG * HEAD_DIM), lambda g, i: (i, g)),
        ],
        out_specs=pl.BlockSpec((tile, HPG * HEAD_DIM), lambda g, i: (i, g)),
        scratch_shapes=[
            pltpu.VMEM((rows4, 2 * HEAD_DIM), BF16),
            pltpu.VMEM((rows4, LANES), F32),
            pltpu.VMEM((rows4, 2 * HEAD_DIM), F32),
            pltpu.VMEM((rows4, 2 * HEAD_DIM), F32),
        ],
        compiler_params=_params("parallel", "arbitrary"),
    )(proj, neg_mask, proj, proj, proj, proj, blk_onehot, bias, gates, o_c)


def _rel_bucket_np(dist):
    n = np.maximum(dist, 0)
    max_exact = REL_BUCKETS // 2
    large = max_exact + (np.log(np.maximum(n, 1).astype(np.float32) / np.float32(max_exact))
                         / np.float32(math.log(REL_MAX_DIST / max_exact))
                         * np.float32(REL_BUCKETS - max_exact)).astype(np.int32)
    large = np.minimum(large, REL_BUCKETS - 1)
    return np.where(n < max_exact, n, large).astype(np.int32)


def _far_distance():
    d = np.arange(4 * REL_MAX_DIST)
    b = _rel_bucket_np(d)
    assert b[-1] == REL_BUCKETS - 1
    return int(np.max(np.nonzero(b != REL_BUCKETS - 1)[0])) + 1


def _bias_table(rel_bias, dist):
    bucket = jnp.asarray(_rel_bucket_np(dist))
    shifted = (rel_bias[bucket] - rel_bias[REL_BUCKETS - 1]) * LOG2E
    tab = jnp.where(jnp.asarray(dist >= 0)[..., None], shifted, NEG)
    return jnp.moveaxis(tab, -1, 0).astype(F32)


def _toeplitz_bias(rel_bias, offset, t):
    x = np.arange(2 * t)
    period = _bias_table(rel_bias, np.where(x < t, offset - x, offset + 2 * t - x))
    tiled = jnp.tile(period, (1, t))[:, :t * (2 * t - 1)]
    return tiled.reshape(-1, t, 2 * t - 1)[:, :, :t]


def _nsa_mixer(h, hn, rel_bias, w_in, w_out, pos_k, w1_k, w2_k, pos_v, w1_v, w2_v):
    s = h.shape[0]
    n_sel = s // SEL_BLOCK
    ncp = s // CMP_STRIDE
    tile = ATT_TILE
    far = _far_distance()
    assert n_sel <= LANES and s % CMP_TQ == 0 and s % tile == 0
    assert far <= tile and far <= CMP_STRIDE * CMP_PAD - CMP_BLOCK + 1 + CMP_STRIDE

    col_scale = jnp.where(jnp.arange(PROJ_MAIN) < Q_WIDTH, HEAD_DIM ** -0.5 * LOG2E, 1.0).astype(F32)[None, :]
    proj = _matmul(_ep_scale, hn, w_in, PROJ_MAIN, BF16, 1024, 512, row_extra=col_scale)
    w_gate = w_in[:, PROJ_MAIN:].reshape(D_MODEL, N_GATES, N_KV_GROUPS, HPG).transpose(0, 2, 1, 3)
    w_gate = w_gate.reshape(D_MODEL, N_KV_GROUPS, N_GATES * HPG)
    w_gate = jnp.pad(w_gate, ((0, 0), (0, 0), (0, LANES - N_GATES * HPG))).reshape(D_MODEL, N_KV_GROUPS * LANES)
    gates = _matmul(jax.nn.sigmoid, hn, w_gate, N_KV_GROUPS * LANES, F32, 1024, 512)

    def stride_rows(col0):
        a = proj[:, col0:col0 + KV_WIDTH].reshape(ncp, CMP_STRIDE, N_KV_GROUPS, HEAD_DIM)
        return a.transpose(2, 0, 1, 3).reshape(N_KV_GROUPS, ncp, CMP_STRIDE * HEAD_DIM)

    r = jnp.stack([stride_rows(Q_WIDTH), stride_rows(Q_WIDTH + KV_WIDTH)])
    w1 = jnp.stack([w1_k, w1_v]).astype(BF16)
    w2 = jnp.stack([w2_k, w2_v]).astype(BF16)
    pos = jnp.stack([pos_k.reshape(1, -1), pos_v.reshape(1, -1)]).astype(BF16)
    pos = jnp.broadcast_to(pos, (2, SUBLANES, CMP_BLOCK * HEAD_DIM))
    cc = _compress(r, w1, w2, pos)
    kcc = jnp.pad(cc[0], ((0, 0), (CMP_PAD, 0), (0, 0)))
    vcct = jnp.pad(cc[1], ((0, 0), (CMP_PAD, 0), (0, 0))).transpose(0, 2, 1)
    n_rows = ncp + CMP_PAD

    nb = CMP_TQ // CMP_STRIDE + CMP_PAD
    b_idx = np.arange(nb)[:, None]
    t_idx = np.arange(CMP_TQ)[None, :]
    band = _bias_table(rel_bias, t_idx - CMP_STRIDE * (b_idx - CMP_PAD) - (CMP_BLOCK - 1))
    ratio = SEL_BLOCK // CMP_STRIDE
    lo = CMP_BLOCK // CMP_STRIDE - 1
    c_of_row = np.arange(n_rows)[None, :] - CMP_PAD
    j_of = np.arange(LANES)[:, None]
    mt = ((c_of_row >= ratio * j_of - lo) & (c_of_row <= ratio * j_of + ratio - 1)
          & (c_of_row >= 0) & (c_of_row <= ncp - 2) & (j_of < n_sel)).astype(np.float32)
    o_c, neg_mask = _cmp_attn(proj, kcc, vcct, band, jnp.asarray(mt), CMP_TQ)

    onehot = (np.arange(s)[:, None] // SEL_BLOCK == np.arange(LANES)[None, :]).astype(np.float32)
    tl = np.arange(tile)[:, None]
    kl = np.arange(tile)[None, :]
    edge = jnp.broadcast_to(jnp.asarray(np.where(tl < kl, 0.0, NEG), F32), (N_HEADS, tile, tile))
    bias = jnp.concatenate([edge, _toeplitz_bias(rel_bias, tile, tile), _toeplitz_bias(rel_bias, 0, tile)], axis=-1)
    o = _sel_win(proj, neg_mask, jnp.asarray(onehot, BF16), bias, gates, o_c, tile)
    return _matmul(_ep_resid, o, w_out, D_MODEL, F32, 1024, 512, tile_extra=h)


def _s5_kernel(x_ref, lam_ref, bt_ref, c_ref, d_ref, y_ref, vt, ys, tmask, *, chunk):
    L = chunk
    half = L // 2
    width = L * SSM_GROUP
    n_chunks = x_ref.shape[0] // L
    for tau in range(L):
        vt[tau] = x_ref[pl.ds(tau, n_chunks, stride=L), :].T

    @pl.when(pl.program_id(0) == 0)
    def _():
        dst = lax.broadcasted_iota(jnp.int32, (width, width), 0) // SSM_GROUP
        src = lax.broadcasted_iota(jnp.int32, (width, width), 1) // SSM_GROUP
        tmask[...] = jnp.where(src <= dst, 1.0, 0.0)

    def cmul(xr, xi, yr, yi):
        return xr * yr - xi * yi, xr * yi + xi * yr

    sub = lax.broadcasted_iota(jnp.int32, (SUBLANES, 1), 0)
    consts = jnp.where(sub == 0, 1.0, jnp.where(sub == 1, float(half + 1), jnp.where(sub == 2, float(half - 1), float(L))))
    tau_col = lax.broadcasted_iota(jnp.int32, (L, 1), 0).astype(F32)
    row = lax.broadcasted_iota(jnp.int32, (n_chunks, SSM_STATE), 0)

    def group(g, carry):
        lam = lam_ref[g]
        a_re, a_im = lam[0:1], lam[1:2]
        dt = jnp.exp(lam[2:3])
        log_re, log_im = a_re * dt, a_im * dt

        def cpow(e):
            mag = jnp.exp(e * log_re)
            return mag * jnp.cos(e * log_im), mag * jnp.sin(e * log_im)

        kr, ki = cpow(consts)
        pr, pi = kr[0:1] - 1.0, ki[0:1]
        den = a_re * a_re + a_im * a_im
        cf_r = (pr * a_re + pi * a_im) / den
        cf_i = (pi * a_re - pr * a_im) / den
        bb_r, bb_i = cmul(cf_r, cf_i, bt_ref[g, 0], bt_ref[g, 1])
        c_r, c_i = c_ref[g, 0], c_ref[g, 1]

        e1r, e1i = cpow(tau_col - half)
        e2r, e2i = cpow(half - tau_col)
        e3r, e3i = cmul(e1r, e1i, kr[1:2], ki[1:2])
        e4r, e4i = cmul(e2r, e2i, kr[2:3], ki[2:3])

        def outer(er, ei, wr, wi):
            xr, xi = cmul(er[:, None, :], ei[:, None, :], wr[None, :, :], wi[None, :, :])
            return xr.reshape(width, SSM_STATE), xi.reshape(width, SSM_STATE)

        qm_r, qm_i = outer(e1r, e1i, c_r, c_i)
        km_r, km_i = outer(e2r, e2i, bb_r, bb_i)
        qc_r, qc_i = outer(e3r, e3i, c_r, c_i)
        wz_r, wz_i = outer(e4r, e4i, bb_r, bb_i)

        chans = pl.ds(pl.multiple_of(g * SSM_GROUP, SSM_GROUP), SSM_GROUP)
        u = vt[:, chans, :].reshape(width, n_chunks)
        ub = u.astype(BF16)

        tt = _dot_nt(qm_r.astype(BF16), km_r.astype(BF16)) - _dot_nt(qm_i.astype(BF16), km_i.astype(BF16))
        y = _dot((tt * tmask[...]).astype(BF16), ub)

        wz_t = jnp.concatenate([wz_r, wz_i], axis=1).T
        z = _dot(wz_t.astype(BF16), ub).T
        x_r, x_i = z[:, :SSM_STATE], z[:, SSM_STATE:]
        p_r, p_i = kr[3:4], ki[3:4]
        d = 1
        while d < n_chunks:
            s_r = jnp.where(row >= d, pltpu.roll(x_r, d, 0), 0.0)
            s_i = jnp.where(row >= d, pltpu.roll(x_i, d, 0), 0.0)
            inc_r, inc_i = cmul(p_r, p_i, s_r, s_i)
            x_r, x_i = x_r + inc_r, x_i + inc_i
            p_r, p_i = cmul(p_r, p_i, p_r, p_i)
            d *= 2
        prev_r = jnp.where(row >= 1, pltpu.roll(x_r, 1, 0), 0.0)
        prev_i = jnp.where(row >= 1, pltpu.roll(x_i, 1, 0), 0.0)
        y = y + _dot_nt(qc_r.astype(BF16), prev_r.astype(BF16)) - _dot_nt(qc_i.astype(BF16), prev_i.astype(BF16))
        y = y + d_ref[g] * u
        vt[:, chans, :] = jax.nn.gelu(y).reshape(L, SSM_GROUP, n_chunks)
        return carry

    lax.fori_loop(0, LANES // SSM_GROUP, group, 0)
    for tau in range(L):
        ys[pl.ds(tau, n_chunks, stride=L), :] = vt[tau].T
    y_ref[...] = ys[...].astype(y_ref.dtype)


def _s5_mixer(h, hn, a_re, a_im, log_dt, b_re, b_im, c_re, c_im, d_skip, w_glu):
    s = h.shape[0]
    L = S5_CHUNK
    n_chunks = s // L
    width = L * SSM_GROUP
    gpb = LANES // SSM_GROUP
    lam = jnp.stack([a_re, a_im, jnp.broadcast_to(log_dt[:, None], a_re.shape)], axis=1).astype(F32)
    bt = jnp.stack([b_re.transpose(0, 2, 1), b_im.transpose(0, 2, 1)], axis=1).astype(F32)
    cc = jnp.stack([c_re, c_im], axis=1).astype(F32)
    dd = jnp.tile(d_skip.astype(F32).reshape(SSM_GROUPS, 1, SSM_GROUP), (1, L, 1)).reshape(SSM_GROUPS, width, 1)
    y = pl.pallas_call(
        functools.partial(_s5_kernel, chunk=L),
        out_shape=jax.ShapeDtypeStruct((s, D_MODEL), BF16),
        grid=(SSM_GROUPS // gpb,),
        in_specs=[
            pl.BlockSpec((s, LANES), lambda b: (0, b)),
            pl.BlockSpec((gpb, 3, SSM_STATE), lambda b: (b, 0, 0)),
            pl.BlockSpec((gpb, 2, SSM_GROUP, SSM_STATE), lambda b: (b, 0, 0, 0)),
            pl.BlockSpec((gpb, 2, SSM_GROUP, SSM_STATE), lambda b: (b, 0, 0, 0)),
            pl.BlockSpec((gpb, width, 1), lambda b: (b, 0, 0)),
        ],
        out_specs=pl.BlockSpec((s, LANES), lambda b: (0, b)),
        scratch_shapes=[
            pltpu.VMEM((L, LANES, n_chunks), F32),
            pltpu.VMEM((s, LANES), F32),
            pltpu.VMEM((width, width), F32),
        ],
        compiler_params=_params("arbitrary"),
    )(hn, lam, bt, cc, dd)
    return _matmul(_ep_glu_resid, y, w_glu, D_MODEL, F32, 1024, 512, b_halves=2, tile_extra=h)


def kernel(x, rel_bias, mix_norm_g, ffn_norm_g, final_norm_g, nsa_w_in, nsa_w_out, cmp_pos_k, cmp_w1_k, cmp_w2_k, cmp_pos_v, cmp_w1_v, cmp_w2_v, s5_A_re, s5_A_im, s5_log_dt, s5_B_re, s5_B_im, s5_C_re, s5_C_im, s5_D, s5_w_glu, ffn_w_in, ffn_w_out):
    assert x.shape[0] == 1
    h = x[0]
    hn = _rmsnorm(h, mix_norm_g[0], BF16)
    h = _nsa_mixer(h, hn, rel_bias, nsa_w_in[0], nsa_w_out[0], cmp_pos_k[0], cmp_w1_k[0], cmp_w2_k[0],
                   cmp_pos_v[0], cmp_w1_v[0], cmp_w2_v[0])
    h = _swiglu_ffn(h, ffn_norm_g[0], ffn_w_in[0], ffn_w_out[0])
    hn = _rmsnorm(h, mix_norm_g[1], F32)
    h = _s5_mixer(h, hn, s5_A_re[0], s5_A_im[0], s5_log_dt[0], s5_B_re[0], s5_B_im[0], s5_C_re[0], s5_C_im[0],
                  s5_D[0], s5_w_glu[0])
    h = _swiglu_ffn(h, ffn_norm_g[1], ffn_w_in[1], ffn_w_out[1])
    return _rmsnorm(h, final_norm_g, x.dtype)[None]
```

```python
import functools
import math

import numpy as np
import jax
import jax.numpy as jnp
from jax import lax
from jax.experimental import pallas as pl
from jax.experimental.pallas import tpu as pltpu

D_MODEL = 2048
N_HEADS = 16
HEAD_DIM = 128
N_KV_GROUPS = 4
HPG = N_HEADS // N_KV_GROUPS
CMP_BLOCK = 32
CMP_STRIDE = 16
CMP_HIDDEN = 2 * HEAD_DIM
SEL_BLOCK = 64
SEL_TOPK = 16
WINDOW = 512
N_GATES = 3
KV_WIDTH = N_KV_GROUPS * HEAD_DIM
Q_WIDTH = N_HEADS * HEAD_DIM
PROJ_MAIN = Q_WIDTH + 6 * KV_WIDTH
REL_BUCKETS = 32
REL_MAX_DIST = 128
SSM_GROUP = 16
SSM_GROUPS = D_MODEL // SSM_GROUP
SSM_STATE = 64
D_FF = ((8 * D_MODEL + 2) // 3 + 255) // 256 * 256
RMS_EPS = 1e-6
NEG = -1e30
FORCE_SCORE = 1e9
LOG2E = math.log2(math.e)

LANES = 128
SUBLANES = 8
VMEM_LIMIT = 52 * 1024 * 1024

ATT_TILE = WINDOW // 2
FAR_TILES = 4
CMP_TQ = 256
CMP_PAD = 8
S5_CHUNK = 32

BF16 = jnp.bfloat16
F32 = jnp.float32


def _dot(a, b):
    return jnp.dot(a, b, preferred_element_type=F32)


def _dot_nt(a, b):
    return lax.dot_general(a, b, (((1,), (1,)), ((), ())), preferred_element_type=F32)


def _params(*sem):
    return pltpu.CompilerParams(dimension_semantics=sem, vmem_limit_bytes=VMEM_LIMIT)


def _rmsnorm_kernel(x_ref, g_ref, o_ref):
    x = x_ref[...]
    ms = jnp.mean(x * x, axis=-1, keepdims=True)
    o_ref[...] = (x * lax.rsqrt(ms + RMS_EPS) * g_ref[...]).astype(o_ref.dtype)


def _rmsnorm(x, g, out_dtype, tm=512):
    s, d = x.shape
    return pl.pallas_call(
        _rmsnorm_kernel,
        out_shape=jax.ShapeDtypeStruct((s, d), out_dtype),
        grid=(s // tm,),
        in_specs=[pl.BlockSpec((tm, d), lambda i: (i, 0)), pl.BlockSpec((1, d), lambda i: (0, 0))],
        out_specs=pl.BlockSpec((tm, d), lambda i: (i, 0)),
        compiler_params=_params("parallel"),
    )(x, g.reshape(1, d))


def _mm_kernel(*refs, n_b, n_extra, epilogue, w_rows_are_outputs):
    a_ref = refs[0]
    b_refs = refs[1:1 + n_b]
    extra_refs = refs[1 + n_b:1 + n_b + n_extra]
    o_ref = refs[1 + n_b + n_extra]
    w_scr = refs[2 + n_b + n_extra:]

    @pl.when(pl.program_id(1) == 0)
    def _():
        for b_ref, w in zip(b_refs, w_scr):
            w[...] = b_ref[...].astype(BF16)

    a = a_ref[...]
    z = [(_dot_nt if w_rows_are_outputs else _dot)(a, w[...]) for w in w_scr]
    o_ref[...] = epilogue(*z, *[e[...] for e in extra_refs]).astype(o_ref.dtype)


def _ep_scale(z, scale):
    return z * scale


def _ep_resid(z, resid):
    return resid + z


def _ep_swiglu(za, zb):
    return jax.nn.silu(za) * zb


def _ep_glu_resid(za, zb, resid):
    return resid + za * jax.nn.sigmoid(zb)


def _matmul(epilogue, a, b, layer, n_out, out_dtype, tm, tn, *, b_halves=1, row_extra=None, tile_extra=None,
            w_rows_are_outputs=False):
    m, k = a.shape
    nj = n_out // tn
    b_mode = pl.Buffered(1) if 2 * b_halves * k * tn * 4 > VMEM_LIMIT // 3 else None
    in_specs = [pl.BlockSpec((tm, k), lambda j, i: (i, 0))]
    args = [a]
    for half in range(b_halves):
        if w_rows_are_outputs:
            spec = pl.BlockSpec((None, tn, k), functools.partial(lambda j, i, o: (layer, j + o, 0), o=half * nj),
                                pipeline_mode=b_mode)
        else:
            spec = pl.BlockSpec((None, k, tn), functools.partial(lambda j, i, o: (layer, 0, j + o), o=half * nj),
                                pipeline_mode=b_mode)
        in_specs.append(spec)
        args.append(b)
    extras = []
    if row_extra is not None:
        in_specs.append(pl.BlockSpec((1, tn), lambda j, i: (0, j)))
        extras.append(row_extra)
    if tile_extra is not None:
        in_specs.append(pl.BlockSpec((tm, tn), lambda j, i: (i, j)))
        extras.append(tile_extra)
    kern = functools.partial(_mm_kernel, n_b=b_halves, n_extra=len(extras), epilogue=epilogue,
                             w_rows_are_outputs=w_rows_are_outputs)
    return pl.pallas_call(
        kern,
        out_shape=jax.ShapeDtypeStruct((m, n_out), out_dtype),
        grid=(nj, m // tm),
        in_specs=in_specs,
        out_specs=pl.BlockSpec((tm, tn), lambda j, i: (i, j)),
        scratch_shapes=[pltpu.VMEM((tn, k) if w_rows_are_outputs else (k, tn), BF16) for _ in range(b_halves)],
        compiler_params=_params("arbitrary", "arbitrary"),
    )(*args, *extras)


def _swiglu_ffn(h, g, w_in, w_out, layer):
    hn = _rmsnorm(h, g, BF16)
    act = _matmul(_ep_swiglu, hn, w_in, layer, D_FF, BF16, 1024, 512, b_halves=2)
    return _matmul(_ep_resid, act, w_out, layer, D_MODEL, F32, 512, 512, tile_extra=h)


def _compress_kernel(r_ref, w1_ref, w2_ref, pos_ref, o_ref):
    r = r_ref[0, 0]
    half = CMP_STRIDE * HEAD_DIM
    top = _dot(r, w1_ref[0, :half, :])
    bot = _dot(r, w1_ref[0, half:, :])
    posb = _dot(pos_ref[0], w1_ref[0])[0:1]
    ncp = r.shape[0]
    hid = top + pltpu.roll(bot, ncp - 1, 0) + posb
    o_ref[0, 0] = _dot(jax.nn.gelu(hid).astype(BF16), w2_ref[0]).astype(o_ref.dtype)


def _compress(r, w1, w2, pos):
    _, g, ncp, width = r.shape
    return pl.pallas_call(
        _compress_kernel,
        out_shape=jax.ShapeDtypeStruct((2, g, ncp, HEAD_DIM), BF16),
        grid=(2, g),
        in_specs=[
            pl.BlockSpec((1, 1, ncp, width), lambda kv, gi: (kv, gi, 0, 0)),
            pl.BlockSpec((1, 2 * width, CMP_HIDDEN), lambda kv, gi: (kv, 0, 0)),
            pl.BlockSpec((1, CMP_HIDDEN, HEAD_DIM), lambda kv, gi: (kv, 0, 0)),
            pl.BlockSpec((1, SUBLANES, 2 * width), lambda kv, gi: (kv, 0, 0)),
        ],
        out_specs=pl.BlockSpec((1, 1, ncp, HEAD_DIM), lambda kv, gi: (kv, gi, 0, 0)),
        compiler_params=_params("parallel", "parallel"),
    )(r, w1, w2, pos)


def _cmp_attn_kernel(q_ref, kcc_ref, vcct_ref, band_ref, mt_ref, oc_ref, nm_ref, s_scr, *, tq, nb):
    i = pl.program_id(1)
    band_start = pl.multiple_of(i * (tq // CMP_STRIDE), SUBLANES)
    kcc = kcc_ref[0]
    vcct = vcct_ref[0]
    n_rows = kcc.shape[0]
    rows = lax.broadcasted_iota(jnp.int32, (n_rows, tq), 0)
    row_bias = jnp.where(rows >= CMP_PAD, jnp.where(rows < band_start + nb, 0.0, NEG), NEG)
    imp = jnp.zeros((n_rows, tq), F32)
    for h in range(HPG):
        qh = q_ref[:, h * HEAD_DIM:(h + 1) * HEAD_DIM]
        s_scr[...] = _dot_nt(kcc, qh) + row_bias
        s_scr[pl.ds(band_start, nb), :] += band_ref[h]
        s = s_scr[...]
        m = jnp.max(s, axis=0, keepdims=True)
        e = jnp.where(s > 0.5 * NEG, jnp.exp2(s - m), 0.0)
        l = jnp.sum(e, axis=0, keepdims=True)
        pn = e * (1.0 / jnp.where(l > 0.0, l, 1.0))
        imp = imp + pn
        oct_h = _dot(vcct, pn.astype(BF16))
        oc_ref[:, h * HEAD_DIM:(h + 1) * HEAD_DIM] = oct_h.T
    slc = jnp.dot(mt_ref[...], imp, precision=lax.Precision.HIGHEST, preferred_element_type=F32)
    n_blk = slc.shape[0]
    jj = lax.broadcasted_iota(jnp.int32, (n_blk, tq), 0)
    t_blk = (i * tq + lax.broadcasted_iota(jnp.int32, (n_blk, tq), 1)) // SEL_BLOCK
    forced = jnp.where(jj == 0, 1, jnp.where(jj == t_blk, 1, jnp.where(jj == t_blk - 1, 1, 0)))
    score = jnp.where(forced == 1, FORCE_SCORE, slc)
    score = jnp.where(jj <= t_blk, score, NEG)
    picked = jnp.zeros((n_blk, tq), F32)
    for _ in range(SEL_TOPK):
        best = jnp.max(score, axis=0, keepdims=True)
        first = jnp.min(jnp.where(score == best, jj, n_blk), axis=0, keepdims=True)
        hit = jj == first
        picked = jnp.where(hit, 1.0, picked)
        score = jnp.where(hit, -jnp.inf, score)
    neg_mask = jnp.where(picked > 0.5, jnp.where(jj <= t_blk, 0.0, NEG), NEG)
    nm_ref[...] = neg_mask.T.astype(nm_ref.dtype)


def _cmp_attn(proj, kcc, vcct, band, mt, tq):
    s = proj.shape[0]
    n_rows = kcc.shape[1]
    nb = band.shape[1]
    kern = functools.partial(_cmp_attn_kernel, tq=tq, nb=nb)
    return pl.pallas_call(
        kern,
        out_shape=(jax.ShapeDtypeStruct((s, Q_WIDTH), F32),
                   jax.ShapeDtypeStruct((s, N_KV_GROUPS * LANES), BF16)),
        grid=(N_KV_GROUPS, s // tq),
        in_specs=[
            pl.BlockSpec((tq, HPG * HEAD_DIM), lambda g, i: (i, g)),
            pl.BlockSpec((1, n_rows, HEAD_DIM), lambda g, i: (g, 0, 0)),
            pl.BlockSpec((1, HEAD_DIM, n_rows), lambda g, i: (g, 0, 0)),
            pl.BlockSpec((HPG, nb, tq), lambda g, i: (g, 0, 0)),
            pl.BlockSpec((LANES, n_rows), lambda g, i: (0, 0)),
        ],
        out_specs=(pl.BlockSpec((tq, HPG * HEAD_DIM), lambda g, i: (i, g)),
                   pl.BlockSpec((tq, LANES), lambda g, i: (i, g))),
        scratch_shapes=[pltpu.VMEM((n_rows, tq), F32)],
        compiler_params=_params("parallel", "parallel"),
    )(proj, kcc, vcct, band, mt)


def _lane_tile(x, width):
    return jnp.concatenate([x] * (width // x.shape[1]), axis=1)


def _softmax_update(s, v_aug, m_ref, acc_ref, rows):
    m_prev = m_ref[rows, :]
    m_next = jnp.maximum(m_prev, jnp.max(s, axis=1, keepdims=True))
    alpha = jnp.exp2(m_prev - m_next)
    p = jnp.exp2(s - _lane_tile(m_next, s.shape[1]))
    acc_ref[rows, :] = _lane_tile(alpha, acc_ref.shape[1]) * acc_ref[rows, :] + _dot(p.astype(BF16), v_aug)
    m_ref[rows, :] = m_next


def _softmax_once(s, v_aug):
    m = jnp.broadcast_to(jnp.max(s, axis=1, keepdims=True), (s.shape[0], LANES))
    p = jnp.exp2(s - _lane_tile(m, s.shape[1]))
    return _dot(p.astype(BF16), v_aug)


def _sel_win_kernel(q_ref, nm_ref, ks_ref, vs_ref, kw_ref, vw_ref, blk_ref, per_ref, g_ref, oc_ref, o_ref,
                    q4, m_s, a_s, a_w, bias_ref, *, tile):
    i = pl.program_id(1)
    t = tile

    @pl.when(i == 0)
    def _():
        tl = lax.broadcasted_iota(jnp.int32, (t, t), 0)
        kl = lax.broadcasted_iota(jnp.int32, (t, t), 1)
        edge = jnp.where(tl < kl, 0.0, NEG)
        for h in range(HPG):
            bias_ref[h, :, :t] = edge
            for which in range(2):
                rows = jnp.broadcast_to(per_ref[h, which:which + 1, :], (t, 2 * t))
                table = pltpu.roll(rows, 0, 1, stride=1, stride_axis=0)
                bias_ref[h, :, (which + 1) * t:(which + 2) * t] = table[:, :t]
    for h in range(HPG):
        q4[h * t:(h + 1) * t, :HEAD_DIM] = q_ref[:, h * HEAD_DIM:(h + 1) * HEAD_DIM]
        q4[h * t:(h + 1) * t, HEAD_DIM:] = nm_ref[...]
    m_s[...] = jnp.full(m_s.shape, -3e38, F32)
    a_s[...] = jnp.zeros(a_s.shape, F32)

    def sel_step(row0, n_tiles, n_biased):
        keys = pl.ds(row0, n_tiles * t)
        k_aug = jnp.concatenate([ks_ref[keys, :], blk_ref[keys, :]], axis=1)
        v_aug = jnp.concatenate([vs_ref[keys, :], jnp.ones((n_tiles * t, LANES), BF16)], axis=1)
        plain = (n_tiles - n_biased) * t
        for h in range(HPG):
            rows = slice(h * t, (h + 1) * t)
            s = _dot_nt(q4[rows, :], k_aug)
            if n_biased:
                near = s[:, plain:] + bias_ref[h, :, (3 - n_biased) * t:]
                s = jnp.concatenate([s[:, :plain], near], axis=1) if plain else near
            _softmax_update(s, v_aug, m_s, a_s, rows)

    def win_once(row0, n_tiles):
        keys = pl.ds(row0, n_tiles * t)
        v_aug = jnp.concatenate([vw_ref[keys, :], jnp.ones((n_tiles * t, LANES), BF16)], axis=1)
        for h in range(HPG):
            rows = slice(h * t, (h + 1) * t)
            s = _dot_nt(q4[rows, :HEAD_DIM], kw_ref[keys, :]) + bias_ref[h, :, (3 - n_tiles) * t:]
            a_w[rows, :] = _softmax_once(s, v_aug)

    n_far = jnp.maximum(i - 1, 0)

    n_chunks = n_far // FAR_TILES
    chunk = FAR_TILES * t

    def far_pair(c, carry):
        row0 = pl.multiple_of(c * 2 * chunk, 2 * chunk)
        sel_step(row0, FAR_TILES, 0)
        sel_step(row0 + chunk, FAR_TILES, 0)
        return carry

    lax.fori_loop(0, n_chunks // 2, far_pair, 0)

    @pl.when(n_chunks % 2 == 1)
    def _():
        sel_step(pl.multiple_of((n_chunks - 1) * chunk, chunk), FAR_TILES, 0)

    for left in range(FAR_TILES):
        @pl.when(jnp.logical_and(i >= 2, n_far % FAR_TILES == left))
        def _(left=left):
            sel_step(pl.multiple_of((i - 1 - left) * t, t), left + 2, 2)
            win_once(pl.multiple_of((i - 2) * t, t), 3)

    @pl.when(i == 1)
    def _():
        sel_step(0, 2, 2)
        win_once(0, 2)

    @pl.when(i == 0)
    def _():
        sel_step(0, 1, 1)
        win_once(0, 1)

    acc_s = a_s[...]
    acc_w = a_w[...]
    o_s = acc_s[:, :HEAD_DIM] * (1.0 / acc_s[:, HEAD_DIM:])
    o_w = acc_w[:, :HEAD_DIM] * (1.0 / acc_w[:, HEAD_DIM:])
    gates = g_ref[...]
    for h in range(HPG):
        cols = slice(h * HEAD_DIM, (h + 1) * HEAD_DIM)
        o = (gates[:, h:h + 1] * oc_ref[:, cols]
             + gates[:, HPG + h:HPG + h + 1] * o_s[h * t:(h + 1) * t]
             + gates[:, 2 * HPG + h:2 * HPG + h + 1] * o_w[h * t:(h + 1) * t])
        o_ref[:, cols] = o.astype(o_ref.dtype)


def _sel_win(proj, neg_mask, blk_onehot, periods, gates, o_c, tile):
    s = proj.shape[0]
    ks_blk = (Q_WIDTH + 2 * KV_WIDTH) // HEAD_DIM
    vs_blk = (Q_WIDTH + 3 * KV_WIDTH) // HEAD_DIM
    kw_blk = (Q_WIDTH + 4 * KV_WIDTH) // HEAD_DIM
    vw_blk = (Q_WIDTH + 5 * KV_WIDTH) // HEAD_DIM
    once = pl.Buffered(1)
    rows4 = HPG * tile
    kern = functools.partial(_sel_win_kernel, tile=tile)
    return pl.pallas_call(
        kern,
        out_shape=jax.ShapeDtypeStruct((s, Q_WIDTH), BF16),
        grid=(N_KV_GROUPS, s // tile),
        in_specs=[
            pl.BlockSpec((tile, HPG * HEAD_DIM), lambda g, i: (i, g)),
            pl.BlockSpec((tile, LANES), lambda g, i: (i, g)),
            pl.BlockSpec((s, HEAD_DIM), lambda g, i: (0, ks_blk + g), pipeline_mode=once),
            pl.BlockSpec((s, HEAD_DIM), lambda g, i: (0, vs_blk + g), pipeline_mode=once),
            pl.BlockSpec((s, HEAD_DIM), lambda g, i: (0, kw_blk + g), pipeline_mode=once),
            pl.BlockSpec((s, HEAD_DIM), lambda g, i: (0, vw_blk + g), pipeline_mode=once),
            pl.BlockSpec((s, LANES), lambda g, i: (0, 0), pipeline_mode=once),
            pl.BlockSpec((HPG, 2, 2 * tile), lambda g, i: (g, 0, 0)),
            pl.BlockSpec((tile, LANES), lambda g, i: (i, g)),
            pl.BlockSpec((tile, HPG * HEAD_DIM), lambda g, i: (i, g)),
        ],
        out_specs=pl.BlockSpec((tile, HPG * HEAD_DIM), lambda g, i: (i, g)),
        scratch_shapes=[
            pltpu.VMEM((rows4, 2 * HEAD_DIM), BF16),
            pltpu.VMEM((rows4, LANES), F32),
            pltpu.VMEM((rows4, 2 * HEAD_DIM), F32),
            pltpu.VMEM((rows4, 2 * HEAD_DIM), F32),
            pltpu.VMEM((HPG, tile, 3 * tile), F32),
        ],
        compiler_params=_params("arbitrary", "arbitrary"),
    )(proj, neg_mask, proj, proj, proj, proj, blk_onehot, periods, gates, o_c)


def _rel_bucket_np(dist):
    n = np.maximum(dist, 0)
    max_exact = REL_BUCKETS // 2
    large = max_exact + (np.log(np.maximum(n, 1).astype(np.float32) / np.float32(max_exact))
                         / np.float32(math.log(REL_MAX_DIST / max_exact))
                         * np.float32(REL_BUCKETS - max_exact)).astype(np.int32)
    large = np.minimum(large, REL_BUCKETS - 1)
    return np.where(n < max_exact, n, large).astype(np.int32)


def _far_distance():
    d = np.arange(4 * REL_MAX_DIST)
    b = _rel_bucket_np(d)
    assert b[-1] == REL_BUCKETS - 1
    return int(np.max(np.nonzero(b != REL_BUCKETS - 1)[0])) + 1


def _bias_table(rel_bias, dist):
    bucket = jnp.asarray(_rel_bucket_np(dist))
    shifted = (rel_bias[bucket] - rel_bias[REL_BUCKETS - 1]) * LOG2E
    tab = jnp.where(jnp.asarray(dist >= 0)[..., None], shifted, NEG)
    return jnp.moveaxis(tab, -1, 0).astype(F32)


def _bias_period(rel_bias, offset, t):
    x = np.arange(2 * t)
    return _bias_table(rel_bias, np.where(x < t, offset - x, offset + 2 * t - x))


def _nsa_mixer(h, hn, rel_bias, w_in, w_out, layer, pos_k, w1_k, w2_k, pos_v, w1_v, w2_v):
    s = h.shape[0]
    n_sel = s // SEL_BLOCK
    ncp = s // CMP_STRIDE
    tile = ATT_TILE
    far = _far_distance()
    assert n_sel <= LANES and s % CMP_TQ == 0 and s % tile == 0
    assert far <= tile and far <= CMP_STRIDE * CMP_PAD - CMP_BLOCK + 1 + CMP_STRIDE

    col_scale = jnp.where(jnp.arange(PROJ_MAIN) < Q_WIDTH, HEAD_DIM ** -0.5 * LOG2E, 1.0).astype(F32)[None, :]
    w_in_t = jnp.swapaxes(w_in, 1, 2)
    proj = _matmul(_ep_scale, hn, w_in_t, layer, PROJ_MAIN, BF16, 1024, 512, row_extra=col_scale,
                   w_rows_are_outputs=True)
    w_gate = w_in_t[layer, PROJ_MAIN:].reshape(N_GATES, N_KV_GROUPS, HPG, D_MODEL).transpose(1, 0, 2, 3)
    w_gate = w_gate.reshape(N_KV_GROUPS, N_GATES * HPG, D_MODEL)
    w_gate = jnp.pad(w_gate, ((0, 0), (0, LANES - N_GATES * HPG), (0, 0))).reshape(1, N_KV_GROUPS * LANES, D_MODEL)
    gates = _matmul(jax.nn.sigmoid, hn, w_gate, 0, N_KV_GROUPS * LANES, F32, 1024, 512, w_rows_are_outputs=True)

    def stride_rows(col0):
        a = proj[:, col0:col0 + KV_WIDTH].reshape(ncp, CMP_STRIDE, N_KV_GROUPS, HEAD_DIM)
        return a.transpose(2, 0, 1, 3).reshape(N_KV_GROUPS, ncp, CMP_STRIDE * HEAD_DIM)

    r = jnp.stack([stride_rows(Q_WIDTH), stride_rows(Q_WIDTH + KV_WIDTH)])
    w1 = jnp.stack([w1_k, w1_v]).astype(BF16)
    w2 = jnp.stack([w2_k, w2_v]).astype(BF16)
    pos = jnp.stack([pos_k.reshape(1, -1), pos_v.reshape(1, -1)]).astype(BF16)
    pos = jnp.broadcast_to(pos, (2, SUBLANES, CMP_BLOCK * HEAD_DIM))
    cc = _compress(r, w1, w2, pos)
    kcc = jnp.pad(cc[0], ((0, 0), (CMP_PAD, 0), (0, 0)))
    vcct = jnp.pad(cc[1], ((0, 0), (CMP_PAD, 0), (0, 0))).transpose(0, 2, 1)
    n_rows = ncp + CMP_PAD

    nb = CMP_TQ // CMP_STRIDE + CMP_PAD
    b_idx = np.arange(nb)[:, None]
    t_idx = np.arange(CMP_TQ)[None, :]
    band = _bias_table(rel_bias, t_idx - CMP_STRIDE * (b_idx - CMP_PAD) - (CMP_BLOCK - 1))
    ratio = SEL_BLOCK // CMP_STRIDE
    lo = CMP_BLOCK // CMP_STRIDE - 1
    c_of_row = np.arange(n_rows)[None, :] - CMP_PAD
    j_of = np.arange(LANES)[:, None]
    mt = ((c_of_row >= ratio * j_of - lo) & (c_of_row <= ratio * j_of + ratio - 1)
          & (c_of_row >= 0) & (c_of_row <= ncp - 2) & (j_of < n_sel)).astype(np.float32)
    o_c, neg_mask = _cmp_attn(proj, kcc, vcct, band, jnp.asarray(mt), CMP_TQ)

    onehot = (np.arange(s)[:, None] // SEL_BLOCK == np.arange(LANES)[None, :]).astype(np.float32)
    periods = jnp.stack([_bias_period(rel_bias, tile, tile), _bias_period(rel_bias, 0, tile)], axis=1)
    o = _sel_win(proj, neg_mask, jnp.asarray(onehot, BF16), periods, gates, o_c, tile)
    return _matmul(_ep_resid, o, w_out, layer, D_MODEL, F32, 1024, 512, tile_extra=h)


def _s5_kernel(x_ref, lam_ref, bt_ref, c_ref, d_ref, y_ref, vt, ys, tmask, *, chunk):
    L = chunk
    half = L // 2
    width = L * SSM_GROUP
    n_chunks = x_ref.shape[0] // L
    for tau in range(L):
        vt[tau] = x_ref[pl.ds(tau, n_chunks, stride=L), :].T

    @pl.when(pl.program_id(0) == 0)
    def _():
        dst = lax.broadcasted_iota(jnp.int32, (width, width), 0) // SSM_GROUP
        src = lax.broadcasted_iota(jnp.int32, (width, width), 1) // SSM_GROUP
        tmask[...] = jnp.where(src <= dst, 1.0, 0.0)

    def cmul(xr, xi, yr, yi):
        return xr * yr - xi * yi, xr * yi + xi * yr

    n2 = 2 * SSM_STATE
    sub = lax.broadcasted_iota(jnp.int32, (SUBLANES, 1), 0)
    consts = jnp.where(sub == 0, 1.0, jnp.where(sub == 1, float(half + 1), jnp.where(sub == 2, float(half - 1), float(L))))
    tau_col = lax.broadcasted_iota(jnp.int32, (L, 1), 0).astype(F32)
    row = lax.broadcasted_iota(jnp.int32, (n_chunks, n2), 0)
    conj = jnp.where(lax.broadcasted_iota(jnp.int32, (1, n2), 1) < SSM_STATE, 1.0, -1.0)

    def group(g, carry):
        lam = lam_ref[g]
        a_re, a_im = lam[0:1], lam[1:2]
        dt = jnp.exp(lam[2:3])
        log_re, log_im = a_re * dt, a_im * dt

        def cpow(e):
            mag = jnp.exp(e * log_re)
            return mag * jnp.cos(e * log_im), mag * jnp.sin(e * log_im)

        kr, ki = cpow(consts)
        pr, pi = kr[0:1] - 1.0, ki[0:1]
        den = a_re * a_re + a_im * a_im
        cf_r = (pr * a_re + pi * a_im) / den
        cf_i = (pi * a_re - pr * a_im) / den
        bt_c, bt_s = bt_ref[g, 0], bt_ref[g, 1]
        bb_c = cf_r * bt_c + cf_i * bt_s
        bb_s = cf_r * bt_s - cf_i * bt_c
        c_c, c_s = c_ref[g, 0], c_ref[g, 1]

        e1r, e1i = cpow(tau_col - half)
        e2r, e2i = cpow(half - tau_col)
        e3r, e3i = cmul(e1r, e1i, kr[1:2], ki[1:2])
        e4r, e4i = cmul(e2r, e2i, kr[2:3], ki[2:3])

        def outer(er, ei, w_c, w_s):
            return (er[:, None, :] * w_c[None, :, :] + ei[:, None, :] * w_s[None, :, :]).reshape(width, n2)

        qm = outer(e1r, e1i, c_c, c_s)
        km_conj = outer(e2r, e2i, bb_c * conj, bb_s * conj)
        qc = outer(e3r, e3i, c_c, c_s)
        wz = outer(e4r, e4i, bb_c, bb_s)

        chans = pl.ds(pl.multiple_of(g * SSM_GROUP, SSM_GROUP), SSM_GROUP)
        ub = vt[:, chans, :].reshape(width, n_chunks).astype(BF16)

        tt = _dot_nt(qm.astype(BF16), km_conj.astype(BF16))
        y = _dot((tt * tmask[...]).astype(BF16), ub)

        x = _dot(wz.T.astype(BF16), ub).T
        p_r, p_i = kr[3:4], ki[3:4]
        d = 1
        while d < n_chunks:
            s = jnp.where(row >= d, pltpu.roll(x, d, 0), 0.0)
            x = x + s * p_r - pltpu.roll(s, SSM_STATE, 1) * (p_i * conj)
            p_r, p_i = cmul(p_r, p_i, p_r, p_i)
            d *= 2
        prev_conj = jnp.where(row >= 1, pltpu.roll(x, 1, 0), 0.0) * conj
        y = y + _dot_nt(qc.astype(BF16), prev_conj.astype(BF16))
        vt[:, chans, :] = y.reshape(L, SSM_GROUP, n_chunks)
        return carry

    lax.fori_loop(0, LANES // SSM_GROUP, group, 0)
    for tau in range(L):
        ys[pl.ds(tau, n_chunks, stride=L), :] = vt[tau].T
    y_ref[...] = jax.nn.gelu(ys[...] + d_ref[...] * x_ref[...]).astype(y_ref.dtype)


def _s5_mixer(h, hn, a_re, a_im, log_dt, b_re, b_im, c_re, c_im, d_skip, w_glu, layer):
    s = h.shape[0]
    L = S5_CHUNK
    n_chunks = s // L
    width = L * SSM_GROUP
    gpb = LANES // SSM_GROUP
    def packed(re, im):
        return jnp.stack([jnp.concatenate([re, im], -1), jnp.concatenate([-im, re], -1)], axis=1).astype(F32)

    lam = jnp.stack([a_re, a_im, jnp.broadcast_to(log_dt[:, None], a_re.shape)], axis=1).astype(F32)
    lam = jnp.concatenate([lam, lam], axis=-1)
    bt = packed(b_re.transpose(0, 2, 1), b_im.transpose(0, 2, 1))
    cc = packed(c_re, c_im)
    dd = d_skip.astype(F32).reshape(1, D_MODEL)
    y = pl.pallas_call(
        functools.partial(_s5_kernel, chunk=L),
        out_shape=jax.ShapeDtypeStruct((s, D_MODEL), BF16),
        grid=(SSM_GROUPS // gpb,),
        in_specs=[
            pl.BlockSpec((s, LANES), lambda b: (0, b)),
            pl.BlockSpec((gpb, 3, 2 * SSM_STATE), lambda b: (b, 0, 0)),
            pl.BlockSpec((gpb, 2, SSM_GROUP, 2 * SSM_STATE), lambda b: (b, 0, 0, 0)),
            pl.BlockSpec((gpb, 2, SSM_GROUP, 2 * SSM_STATE), lambda b: (b, 0, 0, 0)),
            pl.BlockSpec((1, LANES), lambda b: (0, b)),
        ],
        out_specs=pl.BlockSpec((s, LANES), lambda b: (0, b)),
        scratch_shapes=[
            pltpu.VMEM((L, LANES, n_chunks), F32),
            pltpu.VMEM((s, LANES), F32),
            pltpu.VMEM((width, width), F32),
        ],
        compiler_params=_params("arbitrary"),
    )(hn, lam, bt, cc, dd)
    return _matmul(_ep_glu_resid, y, w_glu, layer, D_MODEL, F32, 1024, 512, b_halves=2, tile_extra=h)


def kernel(x, rel_bias, mix_norm_g, ffn_norm_g, final_norm_g, nsa_w_in, nsa_w_out, cmp_pos_k, cmp_w1_k, cmp_w2_k, cmp_pos_v, cmp_w1_v, cmp_w2_v, s5_A_re, s5_A_im, s5_log_dt, s5_B_re, s5_B_im, s5_C_re, s5_C_im, s5_D, s5_w_glu, ffn_w_in, ffn_w_out):
    assert x.shape[0] == 1
    h = x[0]
    hn = _rmsnorm(h, mix_norm_g[0], BF16)
    h = _nsa_mixer(h, hn, rel_bias, nsa_w_in, nsa_w_out, 0, cmp_pos_k[0], cmp_w1_k[0], cmp_w2_k[0],
                   cmp_pos_v[0], cmp_w1_v[0], cmp_w2_v[0])
    h = _swiglu_ffn(h, ffn_norm_g[0], ffn_w_in, ffn_w_out, 0)
    hn = _rmsnorm(h, mix_norm_g[1], F32)
    h = _s5_mixer(h, hn, s5_A_re[0], s5_A_im[0], s5_log_dt[0], s5_B_re[0], s5_B_im[0], s5_C_re[0], s5_C_im[0],
                  s5_D[0], s5_w_glu, 0)
    h = _swiglu_ffn(h, ffn_norm_g[1], ffn_w_in, ffn_w_out, 1)
    return _rmsnorm(h, final_norm_g, x.dtype)[None]
```

```python
import functools
import math

import numpy as np
import jax
import jax.numpy as jnp
from jax import lax
from jax.experimental import pallas as pl
from jax.experimental.pallas import tpu as pltpu

D_MODEL = 2048
N_HEADS = 16
HEAD_DIM = 128
N_KV_GROUPS = 4
HPG = N_HEADS // N_KV_GROUPS
CMP_BLOCK = 32
CMP_STRIDE = 16
CMP_HIDDEN = 2 * HEAD_DIM
SEL_BLOCK = 64
SEL_TOPK = 16
N_FORCED = 3
WINDOW = 512
N_GATES = 3
KV_WIDTH = N_KV_GROUPS * HEAD_DIM
Q_WIDTH = N_HEADS * HEAD_DIM
PROJ_MAIN = Q_WIDTH + 6 * KV_WIDTH
PROJ_ATT = Q_WIDTH + 4 * KV_WIDTH
REL_BUCKETS = 32
REL_MAX_DIST = 128
SSM_GROUP = 16
SSM_GROUPS = D_MODEL // SSM_GROUP
SSM_STATE = 64
D_FF = ((8 * D_MODEL + 2) // 3 + 255) // 256 * 256
RMS_EPS = 1e-6
NEG = -1e30
FORCE_SCORE = 1e9
LOG2E = math.log2(math.e)

LANES = 128
SUBLANES = 8
VMEM_LIMIT = 52 * 1024 * 1024

ATT_TILE = WINDOW // 2
FAR_TILES = 4
CMP_TQ = 256
CMP_PAD = 8
S5_CHUNK = 32

BF16 = jnp.bfloat16
F32 = jnp.float32


def _dot(a, b):
    return jnp.dot(a, b, preferred_element_type=F32)


def _dot_nt(a, b):
    return lax.dot_general(a, b, (((1,), (1,)), ((), ())), preferred_element_type=F32)


def _params(*sem):
    return pltpu.CompilerParams(dimension_semantics=sem, vmem_limit_bytes=VMEM_LIMIT)


def _rmsnorm_kernel(x_ref, g_ref, o_ref):
    x = x_ref[...]
    ms = jnp.mean(x * x, axis=-1, keepdims=True)
    o_ref[...] = (x * lax.rsqrt(ms + RMS_EPS) * g_ref[...]).astype(o_ref.dtype)


def _rmsnorm(x, g, out_dtype, tm=512):
    s, d = x.shape
    return pl.pallas_call(
        _rmsnorm_kernel,
        out_shape=jax.ShapeDtypeStruct((s, d), out_dtype),
        grid=(s // tm,),
        in_specs=[pl.BlockSpec((tm, d), lambda i: (i, 0)), pl.BlockSpec((1, d), lambda i: (0, 0))],
        out_specs=pl.BlockSpec((tm, d), lambda i: (i, 0)),
        compiler_params=_params("parallel"),
    )(x, g.reshape(1, d))


def _mm_kernel(*refs, n_b, n_extra, epilogue, w_rows_are_outputs):
    a_ref = refs[0]
    b_refs = refs[1:1 + n_b]
    extra_refs = refs[1 + n_b:1 + n_b + n_extra]
    o_ref = refs[1 + n_b + n_extra]
    w_scr = refs[2 + n_b + n_extra:]

    @pl.when(pl.program_id(1) == 0)
    def _():
        for b_ref, w in zip(b_refs, w_scr):
            w[...] = b_ref[...].astype(BF16)

    a = a_ref[...]
    z = [(_dot_nt if w_rows_are_outputs else _dot)(a, w[...]) for w in w_scr]
    o_ref[...] = epilogue(*z, *[e[...] for e in extra_refs]).astype(o_ref.dtype)


def _ep_scale(z, scale):
    return z * scale


def _ep_resid(z, resid):
    return resid + z


def _ep_swiglu(za, zb):
    return jax.nn.silu(za) * zb


def _ep_glu_resid(za, zb, resid):
    return resid + za * jax.nn.sigmoid(zb)


def _matmul(epilogue, a, b, layer, n_out, out_dtype, tm, tn, *, b_halves=1, row_extra=None, tile_extra=None,
            w_rows_are_outputs=False, src_block=lambda j: j):
    m, k = a.shape
    nj = n_out // tn
    b_mode = pl.Buffered(1) if 2 * b_halves * k * tn * 4 > VMEM_LIMIT // 3 else None
    in_specs = [pl.BlockSpec((tm, k), lambda j, i: (i, 0))]
    args = [a]
    for half in range(b_halves):
        if w_rows_are_outputs:
            spec = pl.BlockSpec((None, tn, k), functools.partial(lambda j, i, o: (layer, src_block(j) + o, 0), o=half * nj),
                                pipeline_mode=b_mode)
        else:
            spec = pl.BlockSpec((None, k, tn), functools.partial(lambda j, i, o: (layer, 0, src_block(j) + o), o=half * nj),
                                pipeline_mode=b_mode)
        in_specs.append(spec)
        args.append(b)
    extras = []
    if row_extra is not None:
        in_specs.append(pl.BlockSpec((1, tn), lambda j, i: (0, j)))
        extras.append(row_extra)
    if tile_extra is not None:
        in_specs.append(pl.BlockSpec((tm, tn), lambda j, i: (i, j)))
        extras.append(tile_extra)
    kern = functools.partial(_mm_kernel, n_b=b_halves, n_extra=len(extras), epilogue=epilogue,
                             w_rows_are_outputs=w_rows_are_outputs)
    return pl.pallas_call(
        kern,
        out_shape=jax.ShapeDtypeStruct((m, n_out), out_dtype),
        grid=(nj, m // tm),
        in_specs=in_specs,
        out_specs=pl.BlockSpec((tm, tn), lambda j, i: (i, j)),
        scratch_shapes=[pltpu.VMEM((tn, k) if w_rows_are_outputs else (k, tn), BF16) for _ in range(b_halves)],
        compiler_params=_params("arbitrary", "arbitrary"),
    )(*args, *extras)


def _swiglu_ffn(h, g, w_in, w_out, layer):
    hn = _rmsnorm(h, g, BF16)
    act = _matmul(_ep_swiglu, hn, w_in, layer, D_FF, BF16, 1024, 512, b_halves=2)
    return _matmul(_ep_resid, act, w_out, layer, D_MODEL, F32, 512, 512, tile_extra=h)


def _compress_kernel(x_ref, w1_ref, w2_ref, pos_ref, o_ref):
    ncp = x_ref.shape[0] // CMP_STRIDE
    top = jnp.zeros((ncp, CMP_HIDDEN), F32)
    bot = jnp.zeros((ncp, CMP_HIDDEN), F32)
    for r in range(CMP_STRIDE):
        x_r = x_ref[pl.ds(r, ncp, stride=CMP_STRIDE), :].astype(BF16)
        top = top + _dot(x_r, w1_ref[0, r * HEAD_DIM:(r + 1) * HEAD_DIM, :])
        bot = bot + _dot(x_r, w1_ref[0, (CMP_STRIDE + r) * HEAD_DIM:(CMP_STRIDE + r + 1) * HEAD_DIM, :])
    posb = _dot(pos_ref[0], w1_ref[0])[0:1]
    hid = top + pltpu.roll(bot, ncp - 1, 0) + posb
    o_ref[0, 0] = _dot(jax.nn.gelu(hid).astype(BF16), w2_ref[0]).astype(o_ref.dtype)


def _compress(kcvc, w1, w2, pos):
    s = kcvc.shape[0]
    ncp = s // CMP_STRIDE
    return pl.pallas_call(
        _compress_kernel,
        out_shape=jax.ShapeDtypeStruct((2, N_KV_GROUPS, ncp, HEAD_DIM), BF16),
        grid=(2, N_KV_GROUPS),
        in_specs=[
            pl.BlockSpec((s, HEAD_DIM), lambda kv, gi: (0, kv * N_KV_GROUPS + gi)),
            pl.BlockSpec((1, CMP_BLOCK * HEAD_DIM, CMP_HIDDEN), lambda kv, gi: (kv, 0, 0)),
            pl.BlockSpec((1, CMP_HIDDEN, HEAD_DIM), lambda kv, gi: (kv, 0, 0)),
            pl.BlockSpec((1, SUBLANES, CMP_BLOCK * HEAD_DIM), lambda kv, gi: (kv, 0, 0)),
        ],
        out_specs=pl.BlockSpec((1, 1, ncp, HEAD_DIM), lambda kv, gi: (kv, gi, 0, 0)),
        compiler_params=_params("parallel", "parallel"),
    )(kcvc, w1, w2, pos)


def _cmp_attn_kernel(q_ref, kcc_ref, vcct_ref, band_ref, mt_ref, oc_ref, nm_ref, s_scr, *, tq, nb):
    i = pl.program_id(1)
    band_start = pl.multiple_of(i * (tq // CMP_STRIDE), SUBLANES)
    n_rows = kcc_ref.shape[1]
    n_blk = mt_ref.shape[0]

    def body(n_r, n_b):
        kcc = kcc_ref[0, :n_r, :]
        vcct = vcct_ref[0, :, :n_r]
        rows = lax.broadcasted_iota(jnp.int32, (n_r, tq), 0)
        row_bias = jnp.where(rows >= CMP_PAD, jnp.where(rows < band_start + nb, 0.0, NEG), NEG)
        imp = jnp.zeros((n_r, tq), F32)
        for h in range(HPG):
            qh = q_ref[:, h * HEAD_DIM:(h + 1) * HEAD_DIM]
            s_scr[:n_r, :] = _dot_nt(kcc, qh) + row_bias
            s_scr[pl.ds(band_start, nb), :] += band_ref[h]
            s = s_scr[:n_r, :]
            m = jnp.max(s, axis=0, keepdims=True)
            e = jnp.where(s > 0.5 * NEG, jnp.exp2(s - m), 0.0)
            l = jnp.sum(e, axis=0, keepdims=True)
            pn = e * (1.0 / jnp.where(l > 0.0, l, 1.0))
            imp = imp + pn
            oct_h = _dot(vcct, pn.astype(BF16))
            oc_ref[:, h * HEAD_DIM:(h + 1) * HEAD_DIM] = oct_h.T
        mt = mt_ref[:n_b, :n_r]
        hi = imp.astype(BF16)
        rest = imp - hi.astype(F32)
        mid = rest.astype(BF16)
        lo = (rest - mid.astype(F32)).astype(BF16)
        slc = _dot(mt, hi) + _dot(mt, mid) + _dot(mt, lo)
        jj = lax.broadcasted_iota(jnp.int32, (n_b, tq), 0)
        t_blk = (i * tq + lax.broadcasted_iota(jnp.int32, (n_b, tq), 1)) // SEL_BLOCK
        forced = jnp.where(jj == 0, 1, jnp.where(jj == t_blk, 1, jnp.where(jj == t_blk - 1, 1, 0)))
        picked = jnp.where(forced == 1, 1.0, 0.0)
        score = jnp.where(forced == 1, -jnp.inf, jnp.where(jj <= t_blk, slc, NEG))
        for _ in range(SEL_TOPK - N_FORCED):
            best = jnp.max(score, axis=0, keepdims=True)
            first = jnp.min(jnp.where(score == best, jj, n_blk), axis=0, keepdims=True)
            hit = jj == first
            picked = jnp.where(hit, 1.0, picked)
            score = jnp.where(hit, -jnp.inf, score)
        neg_mask = jnp.where(picked > 0.5, jnp.where(jj <= t_blk, 0.0, NEG), NEG)
        if n_b < n_blk:
            neg_mask = jnp.concatenate([neg_mask, jnp.full((n_blk - n_b, tq), NEG, F32)], axis=0)
        nm_ref[...] = neg_mask.T.astype(nm_ref.dtype)

    step = LANES
    extents = list(range(step, n_rows, step)) + [n_rows]
    need_rows = band_start + nb
    for v, n_r in enumerate(extents):
        lower = extents[v - 1] if v else 0
        n_b = min(n_blk, -(-(n_r * CMP_STRIDE // SEL_BLOCK) // SUBLANES) * SUBLANES)
        pl.when(jnp.logical_and(need_rows > lower, need_rows <= n_r))(functools.partial(body, n_r, n_b))


def _cmp_attn(proj, kcc, vcct, band, mt, tq):
    s = proj.shape[0]
    n_rows = kcc.shape[1]
    nb = band.shape[1]
    kern = functools.partial(_cmp_attn_kernel, tq=tq, nb=nb)
    return pl.pallas_call(
        kern,
        out_shape=(jax.ShapeDtypeStruct((s, Q_WIDTH), F32),
                   jax.ShapeDtypeStruct((s, N_KV_GROUPS * LANES), BF16)),
        grid=(N_KV_GROUPS, s // tq),
        in_specs=[
            pl.BlockSpec((tq, HPG * HEAD_DIM), lambda g, i: (i, g)),
            pl.BlockSpec((1, n_rows, HEAD_DIM), lambda g, i: (g, 0, 0)),
            pl.BlockSpec((1, HEAD_DIM, n_rows), lambda g, i: (g, 0, 0)),
            pl.BlockSpec((HPG, nb, tq), lambda g, i: (g, 0, 0)),
            pl.BlockSpec((LANES, n_rows), lambda g, i: (0, 0)),
        ],
        out_specs=(pl.BlockSpec((tq, HPG * HEAD_DIM), lambda g, i: (i, g)),
                   pl.BlockSpec((tq, LANES), lambda g, i: (i, g))),
        scratch_shapes=[pltpu.VMEM((n_rows, tq), F32)],
        compiler_params=_params("parallel", "parallel"),
    )(proj, kcc, vcct, band, mt)


def _lane_tile(x, width):
    return jnp.concatenate([x] * (width // x.shape[1]), axis=1)


def _softmax_update(s, v_aug, m_ref, acc_ref, rows):
    m_prev = m_ref[rows, :]
    m_next = jnp.maximum(m_prev, jnp.max(s, axis=1, keepdims=True))
    alpha = jnp.exp2(m_prev - m_next)
    p = jnp.exp2(s - _lane_tile(m_next, s.shape[1]))
    acc_ref[rows, :] = _lane_tile(alpha, acc_ref.shape[1]) * acc_ref[rows, :] + _dot(p.astype(BF16), v_aug)
    m_ref[rows, :] = m_next


def _softmax_once(s, v_aug):
    m = jnp.broadcast_to(jnp.max(s, axis=1, keepdims=True), (s.shape[0], LANES))
    p = jnp.exp2(s - _lane_tile(m, s.shape[1]))
    return _dot(p.astype(BF16), v_aug)


def _sel_win_kernel(q_ref, nm_ref, ks_ref, vs_ref, kw_ref, vw_ref, blk_ref, per_ref, g_ref, oc_ref, o_ref,
                    q4, m_s, a_s, a_w, bias_ref, *, tile):
    i = pl.program_id(1)
    t = tile

    @pl.when(i == 0)
    def _():
        tl = lax.broadcasted_iota(jnp.int32, (t, t), 0)
        kl = lax.broadcasted_iota(jnp.int32, (t, t), 1)
        edge = jnp.where(tl < kl, 0.0, NEG)
        for h in range(HPG):
            bias_ref[h, :, :t] = edge
            for which in range(2):
                rows = jnp.broadcast_to(per_ref[h, which:which + 1, :], (t, 2 * t))
                table = pltpu.roll(rows, 0, 1, stride=1, stride_axis=0)
                bias_ref[h, :, (which + 1) * t:(which + 2) * t] = table[:, :t]
    for h in range(HPG):
        q4[h * t:(h + 1) * t, :HEAD_DIM] = q_ref[:, h * HEAD_DIM:(h + 1) * HEAD_DIM]
        q4[h * t:(h + 1) * t, HEAD_DIM:] = nm_ref[...]
    m_s[...] = jnp.full(m_s.shape, -3e38, F32)
    a_s[...] = jnp.zeros(a_s.shape, F32)

    def sel_step(row0, n_tiles, n_biased):
        keys = pl.ds(row0, n_tiles * t)
        k_aug = jnp.concatenate([ks_ref[keys, :], blk_ref[keys, :]], axis=1)
        v_aug = jnp.concatenate([vs_ref[keys, :], jnp.ones((n_tiles * t, LANES), BF16)], axis=1)
        plain = (n_tiles - n_biased) * t
        for h in range(HPG):
            rows = slice(h * t, (h + 1) * t)
            s = _dot_nt(q4[rows, :], k_aug)
            if n_biased:
                near = s[:, plain:] + bias_ref[h, :, (3 - n_biased) * t:]
                s = jnp.concatenate([s[:, :plain], near], axis=1) if plain else near
            _softmax_update(s, v_aug, m_s, a_s, rows)

    def win_once(row0, n_tiles):
        keys = pl.ds(row0, n_tiles * t)
        v_aug = jnp.concatenate([vw_ref[keys, :], jnp.ones((n_tiles * t, LANES), BF16)], axis=1)
        for h in range(HPG):
            rows = slice(h * t, (h + 1) * t)
            s = _dot_nt(q4[rows, :HEAD_DIM], kw_ref[keys, :]) + bias_ref[h, :, (3 - n_tiles) * t:]
            a_w[rows, :] = _softmax_once(s, v_aug)

    n_far = jnp.maximum(i - 1, 0)

    n_chunks = n_far // FAR_TILES
    chunk = FAR_TILES * t

    def far_pair(c, carry):
        row0 = pl.multiple_of(c * 2 * chunk, 2 * chunk)
        sel_step(row0, FAR_TILES, 0)
        sel_step(row0 + chunk, FAR_TILES, 0)
        return carry

    lax.fori_loop(0, n_chunks // 2, far_pair, 0)

    @pl.when(n_chunks % 2 == 1)
    def _():
        sel_step(pl.multiple_of((n_chunks - 1) * chunk, chunk), FAR_TILES, 0)

    for left in range(FAR_TILES):
        @pl.when(jnp.logical_and(i >= 2, n_far % FAR_TILES == left))
        def _(left=left):
            sel_step(pl.multiple_of((i - 1 - left) * t, t), left + 2, 2)
            win_once(pl.multiple_of((i - 2) * t, t), 3)

    @pl.when(i == 1)
    def _():
        sel_step(0, 2, 2)
        win_once(0, 2)

    @pl.when(i == 0)
    def _():
        sel_step(0, 1, 1)
        win_once(0, 1)

    acc_s = a_s[...]
    acc_w = a_w[...]
    o_s = acc_s[:, :HEAD_DIM] * (1.0 / acc_s[:, HEAD_DIM:])
    o_w = acc_w[:, :HEAD_DIM] * (1.0 / acc_w[:, HEAD_DIM:])
    gates = g_ref[...]
    for h in range(HPG):
        cols = slice(h * HEAD_DIM, (h + 1) * HEAD_DIM)
        o = (gates[:, h:h + 1] * oc_ref[:, cols]
             + gates[:, HPG + h:HPG + h + 1] * o_s[h * t:(h + 1) * t]
             + gates[:, 2 * HPG + h:2 * HPG + h + 1] * o_w[h * t:(h + 1) * t])
        o_ref[:, cols] = o.astype(o_ref.dtype)


def _sel_win(proj, neg_mask, blk_onehot, periods, gates, o_c, tile):
    s = proj.shape[0]
    ks_blk = Q_WIDTH // HEAD_DIM
    vs_blk = (Q_WIDTH + KV_WIDTH) // HEAD_DIM
    kw_blk = (Q_WIDTH + 2 * KV_WIDTH) // HEAD_DIM
    vw_blk = (Q_WIDTH + 3 * KV_WIDTH) // HEAD_DIM
    once = pl.Buffered(1)
    rows4 = HPG * tile
    kern = functools.partial(_sel_win_kernel, tile=tile)
    return pl.pallas_call(
        kern,
        out_shape=jax.ShapeDtypeStruct((s, Q_WIDTH), BF16),
        grid=(N_KV_GROUPS, s // tile),
        in_specs=[
            pl.BlockSpec((tile, HPG * HEAD_DIM), lambda g, i: (i, g)),
            pl.BlockSpec((tile, LANES), lambda g, i: (i, g)),
            pl.BlockSpec((s, HEAD_DIM), lambda g, i: (0, ks_blk + g), pipeline_mode=once),
            pl.BlockSpec((s, HEAD_DIM), lambda g, i: (0, vs_blk + g), pipeline_mode=once),
            pl.BlockSpec((s, HEAD_DIM), lambda g, i: (0, kw_blk + g), pipeline_mode=once),
            pl.BlockSpec((s, HEAD_DIM), lambda g, i: (0, vw_blk + g), pipeline_mode=once),
            pl.BlockSpec((s, LANES), lambda g, i: (0, 0), pipeline_mode=once),
            pl.BlockSpec((HPG, 2, 2 * tile), lambda g, i: (g, 0, 0)),
            pl.BlockSpec((tile, LANES), lambda g, i: (i, g)),
            pl.BlockSpec((tile, HPG * HEAD_DIM), lambda g, i: (i, g)),
        ],
        out_specs=pl.BlockSpec((tile, HPG * HEAD_DIM), lambda g, i: (i, g)),
        scratch_shapes=[
            pltpu.VMEM((rows4, 2 * HEAD_DIM), BF16),
            pltpu.VMEM((rows4, LANES), F32),
            pltpu.VMEM((rows4, 2 * HEAD_DIM), F32),
            pltpu.VMEM((rows4, 2 * HEAD_DIM), F32),
            pltpu.VMEM((HPG, tile, 3 * tile), F32),
        ],
        compiler_params=_params("arbitrary", "arbitrary"),
    )(proj, neg_mask, proj, proj, proj, proj, blk_onehot, periods, gates, o_c)


def _rel_bucket_np(dist):
    n = np.maximum(dist, 0)
    max_exact = REL_BUCKETS // 2
    large = max_exact + (np.log(np.maximum(n, 1).astype(np.float32) / np.float32(max_exact))
                         / np.float32(math.log(REL_MAX_DIST / max_exact))
                         * np.float32(REL_BUCKETS - max_exact)).astype(np.int32)
    large = np.minimum(large, REL_BUCKETS - 1)
    return np.where(n < max_exact, n, large).astype(np.int32)


def _far_distance():
    d = np.arange(4 * REL_MAX_DIST)
    b = _rel_bucket_np(d)
    assert b[-1] == REL_BUCKETS - 1
    return int(np.max(np.nonzero(b != REL_BUCKETS - 1)[0])) + 1


def _bias_table(rel_bias, dist):
    bucket = jnp.asarray(_rel_bucket_np(dist))
    shifted = (rel_bias[bucket] - rel_bias[REL_BUCKETS - 1]) * LOG2E
    tab = jnp.where(jnp.asarray(dist >= 0)[..., None], shifted, NEG)
    return jnp.moveaxis(tab, -1, 0).astype(F32)


def _bias_period(rel_bias, offset, t):
    x = np.arange(2 * t)
    return _bias_table(rel_bias, np.where(x < t, offset - x, offset + 2 * t - x))


def _nsa_mixer(h, hn, rel_bias, w_in, w_out, layer, pos_k, w1_k, w2_k, pos_v, w1_v, w2_v):
    s = h.shape[0]
    n_sel = s // SEL_BLOCK
    ncp = s // CMP_STRIDE
    tile = ATT_TILE
    far = _far_distance()
    assert n_sel <= LANES and s % CMP_TQ == 0 and s % tile == 0
    assert far <= tile and far <= CMP_STRIDE * CMP_PAD - CMP_BLOCK + 1 + CMP_STRIDE

    col_scale = jnp.where(jnp.arange(PROJ_ATT) < Q_WIDTH, HEAD_DIM ** -0.5 * LOG2E, 1.0).astype(F32)[None, :]
    w_in_t = jnp.swapaxes(w_in, 1, 2)
    tn = KV_WIDTH
    q_blocks, cmp_blocks = Q_WIDTH // tn, 2 * KV_WIDTH // tn
    proj = _matmul(_ep_scale, hn, w_in_t, layer, PROJ_ATT, BF16, 1024, tn, row_extra=col_scale,
                   w_rows_are_outputs=True, src_block=lambda j: jnp.where(j < q_blocks, j, j + cmp_blocks))
    kcvc = _matmul(lambda z: z, hn, w_in_t, layer, 2 * KV_WIDTH, F32, 1024, tn,
                   w_rows_are_outputs=True, src_block=lambda j: j + q_blocks)
    w_gate = w_in_t[layer, PROJ_MAIN:].reshape(N_GATES, N_KV_GROUPS, HPG, D_MODEL).transpose(1, 0, 2, 3)
    w_gate = w_gate.reshape(N_KV_GROUPS, N_GATES * HPG, D_MODEL)
    w_gate = jnp.pad(w_gate, ((0, 0), (0, LANES - N_GATES * HPG), (0, 0))).reshape(1, N_KV_GROUPS * LANES, D_MODEL)
    gates = _matmul(jax.nn.sigmoid, hn, w_gate, 0, N_KV_GROUPS * LANES, F32, 1024, 512, w_rows_are_outputs=True)

    w1 = jnp.stack([w1_k, w1_v]).astype(BF16)
    w2 = jnp.stack([w2_k, w2_v]).astype(BF16)
    pos = jnp.stack([pos_k.reshape(1, -1), pos_v.reshape(1, -1)]).astype(BF16)
    pos = jnp.broadcast_to(pos, (2, SUBLANES, CMP_BLOCK * HEAD_DIM))
    cc = _compress(kcvc, w1, w2, pos)
    kcc = jnp.pad(cc[0], ((0, 0), (CMP_PAD, 0), (0, 0)))
    vcct = jnp.pad(cc[1], ((0, 0), (CMP_PAD, 0), (0, 0))).transpose(0, 2, 1)
    n_rows = ncp + CMP_PAD

    nb = CMP_TQ // CMP_STRIDE + CMP_PAD
    first_dist = -CMP_STRIDE * (nb - 1 - CMP_PAD) - (CMP_BLOCK - 1)
    by_dist = _bias_table(rel_bias, first_dist + np.arange(CMP_TQ + CMP_STRIDE * (nb - 1)))
    band = jnp.stack([by_dist[:, CMP_STRIDE * (nb - 1 - b):CMP_STRIDE * (nb - 1 - b) + CMP_TQ] for b in range(nb)], axis=1)
    ratio = SEL_BLOCK // CMP_STRIDE
    lo = CMP_BLOCK // CMP_STRIDE - 1
    c_of_row = np.arange(n_rows)[None, :] - CMP_PAD
    j_of = np.arange(LANES)[:, None]
    mt = ((c_of_row >= ratio * j_of - lo) & (c_of_row <= ratio * j_of + ratio - 1)
          & (c_of_row >= 0) & (c_of_row <= ncp - 2) & (j_of < n_sel)).astype(np.float32)
    o_c, neg_mask = _cmp_attn(proj, kcc, vcct, band, jnp.asarray(mt, BF16), CMP_TQ)

    onehot = (np.arange(s)[:, None] // SEL_BLOCK == np.arange(LANES)[None, :]).astype(np.float32)
    periods = jnp.stack([_bias_period(rel_bias, tile, tile), _bias_period(rel_bias, 0, tile)], axis=1)
    o = _sel_win(proj, neg_mask, jnp.asarray(onehot, BF16), periods, gates, o_c, tile)
    return _matmul(_ep_resid, o, w_out, layer, D_MODEL, F32, 1024, 512, tile_extra=h)


def _s5_kernel(x_ref, lam_ref, bt_ref, c_ref, d_ref, y_ref, vt, ys, tmask, *, chunk):
    L = chunk
    half = L // 2
    width = L * SSM_GROUP
    n_chunks = x_ref.shape[0] // L
    for tau in range(L):
        vt[tau] = x_ref[pl.ds(tau, n_chunks, stride=L), :].T

    @pl.when(pl.program_id(0) == 0)
    def _():
        dst = lax.broadcasted_iota(jnp.int32, (width, width), 0) // SSM_GROUP
        src = lax.broadcasted_iota(jnp.int32, (width, width), 1) // SSM_GROUP
        tmask[...] = jnp.where(src <= dst, 1.0, 0.0)

    def cmul(xr, xi, yr, yi):
        return xr * yr - xi * yi, xr * yi + xi * yr

    n2 = 2 * SSM_STATE
    sub = lax.broadcasted_iota(jnp.int32, (SUBLANES, 1), 0)
    consts = jnp.where(sub == 0, 1.0, jnp.where(sub == 1, float(half + 1), jnp.where(sub == 2, float(half - 1), float(L))))
    tau_col = lax.broadcasted_iota(jnp.int32, (L, 1), 0).astype(F32)
    row = lax.broadcasted_iota(jnp.int32, (n_chunks, n2), 0)
    conj = jnp.where(lax.broadcasted_iota(jnp.int32, (1, n2), 1) < SSM_STATE, 1.0, -1.0)

    def group(g, carry):
        lam = lam_ref[g]
        a_re, a_im = lam[0:1], lam[1:2]
        dt = jnp.exp(lam[2:3])
        log_re, log_im = a_re * dt, a_im * dt

        def cpow(e):
            mag = jnp.exp(e * log_re)
            return mag * jnp.cos(e * log_im), mag * jnp.sin(e * log_im)

        kr, ki = cpow(consts)
        pr, pi = kr[0:1] - 1.0, ki[0:1]
        den = a_re * a_re + a_im * a_im
        cf_r = (pr * a_re + pi * a_im) / den
        cf_i = (pi * a_re - pr * a_im) / den
        bt_c, bt_s = bt_ref[g, 0], bt_ref[g, 1]
        bb_c = cf_r * bt_c + cf_i * bt_s
        bb_s = cf_r * bt_s - cf_i * bt_c
        c_c, c_s = c_ref[g, 0], c_ref[g, 1]

        e1r, e1i = cpow(tau_col - half)
        e2r, e2i = cpow(half - tau_col)
        e3r, e3i = cmul(e1r, e1i, kr[1:2], ki[1:2])
        e4r, e4i = cmul(e2r, e2i, kr[2:3], ki[2:3])

        def outer(er, ei, w_c, w_s):
            return (er[:, None, :] * w_c[None, :, :] + ei[:, None, :] * w_s[None, :, :]).reshape(width, n2)

        qm = outer(e1r, e1i, c_c, c_s)
        km_conj = outer(e2r, e2i, bb_c * conj, bb_s * conj)
        qc = outer(e3r, e3i, c_c, c_s)
        wz = outer(e4r, e4i, bb_c, bb_s)

        chans = pl.ds(pl.multiple_of(g * SSM_GROUP, SSM_GROUP), SSM_GROUP)
        ub = vt[:, chans, :].reshape(width, n_chunks).astype(BF16)

        tt = _dot_nt(qm.astype(BF16), km_conj.astype(BF16))
        y = _dot((tt * tmask[...]).astype(BF16), ub)

        x = _dot(wz.T.astype(BF16), ub).T
        p_r, p_i = kr[3:4], ki[3:4]
        d = 1
        while d < n_chunks:
            s = jnp.where(row >= d, pltpu.roll(x, d, 0), 0.0)
            x = x + s * p_r - pltpu.roll(s, SSM_STATE, 1) * (p_i * conj)
            p_r, p_i = cmul(p_r, p_i, p_r, p_i)
            d *= 2
        prev_conj = jnp.where(row >= 1, pltpu.roll(x, 1, 0), 0.0) * conj
        y = y + _dot_nt(qc.astype(BF16), prev_conj.astype(BF16))
        vt[:, chans, :] = y.reshape(L, SSM_GROUP, n_chunks)
        return carry

    lax.fori_loop(0, LANES // SSM_GROUP, group, 0)
    for tau in range(L):
        ys[pl.ds(tau, n_chunks, stride=L), :] = vt[tau].T
    y_ref[...] = jax.nn.gelu(ys[...] + d_ref[...] * x_ref[...]).astype(y_ref.dtype)


def _s5_mixer(h, hn, a_re, a_im, log_dt, b_re, b_im, c_re, c_im, d_skip, w_glu, layer):
    s = h.shape[0]
    L = S5_CHUNK
    n_chunks = s // L
    width = L * SSM_GROUP
    gpb = LANES // SSM_GROUP
    def packed(re, im):
        return jnp.stack([jnp.concatenate([re, im], -1), jnp.concatenate([-im, re], -1)], axis=1).astype(F32)

    lam = jnp.stack([a_re, a_im, jnp.broadcast_to(log_dt[:, None], a_re.shape)], axis=1).astype(F32)
    lam = jnp.concatenate([lam, lam], axis=-1)
    bt = packed(b_re.transpose(0, 2, 1), b_im.transpose(0, 2, 1))
    cc = packed(c_re, c_im)
    dd = d_skip.astype(F32).reshape(1, D_MODEL)
    y = pl.pallas_call(
        functools.partial(_s5_kernel, chunk=L),
        out_shape=jax.ShapeDtypeStruct((s, D_MODEL), BF16),
        grid=(SSM_GROUPS // gpb,),
        in_specs=[
            pl.BlockSpec((s, LANES), lambda b: (0, b)),
            pl.BlockSpec((gpb, 3, 2 * SSM_STATE), lambda b: (b, 0, 0)),
            pl.BlockSpec((gpb, 2, SSM_GROUP, 2 * SSM_STATE), lambda b: (b, 0, 0, 0)),
            pl.BlockSpec((gpb, 2, SSM_GROUP, 2 * SSM_STATE), lambda b: (b, 0, 0, 0)),
            pl.BlockSpec((1, LANES), lambda b: (0, b)),
        ],
        out_specs=pl.BlockSpec((s, LANES), lambda b: (0, b)),
        scratch_shapes=[
            pltpu.VMEM((L, LANES, n_chunks), F32),
            pltpu.VMEM((s, LANES), F32),
            pltpu.VMEM((width, width), F32),
        ],
        compiler_params=_params("arbitrary"),
    )(hn, lam, bt, cc, dd)
    return _matmul(_ep_glu_resid, y, w_glu, layer, D_MODEL, F32, 1024, 512, b_halves=2, tile_extra=h)


def kernel(x, rel_bias, mix_norm_g, ffn_norm_g, final_norm_g, nsa_w_in, nsa_w_out, cmp_pos_k, cmp_w1_k, cmp_w2_k, cmp_pos_v, cmp_w1_v, cmp_w2_v, s5_A_re, s5_A_im, s5_log_dt, s5_B_re, s5_B_im, s5_C_re, s5_C_im, s5_D, s5_w_glu, ffn_w_in, ffn_w_out):
    assert x.shape[0] == 1
    h = x[0]
    hn = _rmsnorm(h, mix_norm_g[0], BF16)
    h = _nsa_mixer(h, hn, rel_bias, nsa_w_in, nsa_w_out, 0, cmp_pos_k[0], cmp_w1_k[0], cmp_w2_k[0],
                   cmp_pos_v[0], cmp_w1_v[0], cmp_w2_v[0])
    h = _swiglu_ffn(h, ffn_norm_g[0], ffn_w_in, ffn_w_out, 0)
    hn = _rmsnorm(h, mix_norm_g[1], F32)
    h = _s5_mixer(h, hn, s5_A_re[0], s5_A_im[0], s5_log_dt[0], s5_B_re[0], s5_B_im[0], s5_C_re[0], s5_C_im[0],
                  s5_D[0], s5_w_glu, 0)
    h = _swiglu_ffn(h, ffn_norm_g[1], ffn_w_in, ffn_w_out, 1)
    return _rmsnorm(h, final_norm_g, x.dtype)[None]
```

```python
import functools
import math

import numpy as np
import jax
import jax.numpy as jnp
from jax import lax
from jax.experimental import pallas as pl
from jax.experimental.pallas import tpu as pltpu

D_MODEL = 2048
N_HEADS = 16
HEAD_DIM = 128
N_KV_GROUPS = 4
HPG = N_HEADS // N_KV_GROUPS
CMP_BLOCK = 32
CMP_STRIDE = 16
CMP_HIDDEN = 2 * HEAD_DIM
SEL_BLOCK = 64
SEL_TOPK = 16
N_FORCED = 3
WINDOW = 512
N_GATES = 3
KV_WIDTH = N_KV_GROUPS * HEAD_DIM
Q_WIDTH = N_HEADS * HEAD_DIM
PROJ_MAIN = Q_WIDTH + 6 * KV_WIDTH
PROJ_ATT = Q_WIDTH + 4 * KV_WIDTH
REL_BUCKETS = 32
REL_MAX_DIST = 128
SSM_GROUP = 16
SSM_GROUPS = D_MODEL // SSM_GROUP
SSM_STATE = 64
D_FF = ((8 * D_MODEL + 2) // 3 + 255) // 256 * 256
RMS_EPS = 1e-6
NEG = -1e30
FORCE_SCORE = 1e9
LOG2E = math.log2(math.e)

LANES = 128
SUBLANES = 8
VMEM_LIMIT = 52 * 1024 * 1024

ATT_TILE = WINDOW // 2
FAR_TILES = 4
CMP_TQ = 256
CMP_PAD = 8
S5_CHUNK = 32
S5_GROUPS_PER_TRIP = 4

BF16 = jnp.bfloat16
F32 = jnp.float32


def _dot(a, b):
    return jnp.dot(a, b, preferred_element_type=F32)


def _dot_nt(a, b):
    return lax.dot_general(a, b, (((1,), (1,)), ((), ())), preferred_element_type=F32)


def _params(*sem):
    return pltpu.CompilerParams(dimension_semantics=sem, vmem_limit_bytes=VMEM_LIMIT)


def _rmsnorm_kernel(x_ref, g_ref, o_ref):
    x = x_ref[...]
    ms = jnp.mean(x * x, axis=-1, keepdims=True)
    o_ref[...] = (x * lax.rsqrt(ms + RMS_EPS) * g_ref[...]).astype(o_ref.dtype)


def _rmsnorm(x, g, out_dtype, tm=512):
    s, d = x.shape
    return pl.pallas_call(
        _rmsnorm_kernel,
        out_shape=jax.ShapeDtypeStruct((s, d), out_dtype),
        grid=(s // tm,),
        in_specs=[pl.BlockSpec((tm, d), lambda i: (i, 0)), pl.BlockSpec((1, d), lambda i: (0, 0))],
        out_specs=pl.BlockSpec((tm, d), lambda i: (i, 0)),
        compiler_params=_params("parallel"),
    )(x, g.reshape(1, d))


def _mm_kernel(*refs, n_b, n_extra, epilogue, w_rows_are_outputs):
    a_ref = refs[0]
    b_refs = refs[1:1 + n_b]
    extra_refs = refs[1 + n_b:1 + n_b + n_extra]
    o_ref = refs[1 + n_b + n_extra]
    w_scr = refs[2 + n_b + n_extra:]

    @pl.when(pl.program_id(1) == 0)
    def _():
        for b_ref, w in zip(b_refs, w_scr):
            w[...] = b_ref[...].astype(BF16)

    a = a_ref[...]
    z = [(_dot_nt if w_rows_are_outputs else _dot)(a, w[...]) for w in w_scr]
    o_ref[...] = epilogue(*z, *[e[...] for e in extra_refs]).astype(o_ref.dtype)


def _ep_scale(z, scale):
    return z * scale


def _ep_resid(z, resid):
    return resid + z


def _ep_swiglu(za, zb):
    return jax.nn.silu(za) * zb


def _ep_glu_resid(za, zb, resid):
    return resid + za * jax.nn.sigmoid(zb)


def _matmul(epilogue, a, b, layer, n_out, out_dtype, tm, tn, *, b_halves=1, row_extra=None, tile_extra=None,
            w_rows_are_outputs=False, src_block=lambda j: j):
    m, k = a.shape
    nj = n_out // tn
    b_mode = pl.Buffered(1) if 2 * b_halves * k * tn * 4 > VMEM_LIMIT // 3 else None
    in_specs = [pl.BlockSpec((tm, k), lambda j, i: (i, 0))]
    args = [a]
    for half in range(b_halves):
        if w_rows_are_outputs:
            spec = pl.BlockSpec((None, tn, k), functools.partial(lambda j, i, o: (layer, src_block(j) + o, 0), o=half * nj),
                                pipeline_mode=b_mode)
        else:
            spec = pl.BlockSpec((None, k, tn), functools.partial(lambda j, i, o: (layer, 0, src_block(j) + o), o=half * nj),
                                pipeline_mode=b_mode)
        in_specs.append(spec)
        args.append(b)
    extras = []
    if row_extra is not None:
        in_specs.append(pl.BlockSpec((1, tn), lambda j, i: (0, j)))
        extras.append(row_extra)
    if tile_extra is not None:
        in_specs.append(pl.BlockSpec((tm, tn), lambda j, i: (i, j)))
        extras.append(tile_extra)
    kern = functools.partial(_mm_kernel, n_b=b_halves, n_extra=len(extras), epilogue=epilogue,
                             w_rows_are_outputs=w_rows_are_outputs)
    return pl.pallas_call(
        kern,
        out_shape=jax.ShapeDtypeStruct((m, n_out), out_dtype),
        grid=(nj, m // tm),
        in_specs=in_specs,
        out_specs=pl.BlockSpec((tm, tn), lambda j, i: (i, j)),
        scratch_shapes=[pltpu.VMEM((tn, k) if w_rows_are_outputs else (k, tn), BF16) for _ in range(b_halves)],
        compiler_params=_params("arbitrary", "arbitrary"),
    )(*args, *extras)


def _swiglu_ffn(h, g, w_in, w_out, layer):
    hn = _rmsnorm(h, g, BF16)
    act = _matmul(_ep_swiglu, hn, w_in, layer, D_FF, BF16, 1024, 512, b_halves=2)
    return _matmul(_ep_resid, act, w_out, layer, D_MODEL, F32, 512, 512, tile_extra=h)


def _compress_kernel(x_ref, w1_ref, w2_ref, pos_ref, o_ref):
    ncp = x_ref.shape[0] // CMP_STRIDE
    top = jnp.zeros((ncp, CMP_HIDDEN), F32)
    bot = jnp.zeros((ncp, CMP_HIDDEN), F32)
    for r in range(CMP_STRIDE):
        x_r = x_ref[pl.ds(r, ncp, stride=CMP_STRIDE), :].astype(BF16)
        top = top + _dot(x_r, w1_ref[0, r * HEAD_DIM:(r + 1) * HEAD_DIM, :])
        bot = bot + _dot(x_r, w1_ref[0, (CMP_STRIDE + r) * HEAD_DIM:(CMP_STRIDE + r + 1) * HEAD_DIM, :])
    posb = _dot(pos_ref[0], w1_ref[0])[0:1]
    hid = top + pltpu.roll(bot, ncp - 1, 0) + posb
    o_ref[0, 0] = _dot(jax.nn.gelu(hid).astype(BF16), w2_ref[0]).astype(o_ref.dtype)


def _compress(kcvc, w1, w2, pos):
    s = kcvc.shape[0]
    ncp = s // CMP_STRIDE
    return pl.pallas_call(
        _compress_kernel,
        out_shape=jax.ShapeDtypeStruct((2, N_KV_GROUPS, ncp, HEAD_DIM), BF16),
        grid=(2, N_KV_GROUPS),
        in_specs=[
            pl.BlockSpec((s, HEAD_DIM), lambda kv, gi: (0, kv * N_KV_GROUPS + gi)),
            pl.BlockSpec((1, CMP_BLOCK * HEAD_DIM, CMP_HIDDEN), lambda kv, gi: (kv, 0, 0)),
            pl.BlockSpec((1, CMP_HIDDEN, HEAD_DIM), lambda kv, gi: (kv, 0, 0)),
            pl.BlockSpec((1, SUBLANES, CMP_BLOCK * HEAD_DIM), lambda kv, gi: (kv, 0, 0)),
        ],
        out_specs=pl.BlockSpec((1, 1, ncp, HEAD_DIM), lambda kv, gi: (kv, gi, 0, 0)),
        compiler_params=_params("parallel", "parallel"),
    )(kcvc, w1, w2, pos)


def _cmp_attn_kernel(q_ref, kcc_ref, vcct_ref, band_ref, mt_ref, oc_ref, nm_ref, s_scr, *, tq, nb):
    i = pl.program_id(1)
    band_start = pl.multiple_of(i * (tq // CMP_STRIDE), SUBLANES)
    n_rows = kcc_ref.shape[1]
    n_blk = mt_ref.shape[0]

    def body(n_r, n_b):
        kcc = kcc_ref[0, :n_r, :]
        vcct = vcct_ref[0, :, :n_r]
        rows = lax.broadcasted_iota(jnp.int32, (n_r, tq), 0)
        row_bias = jnp.where(rows >= CMP_PAD, jnp.where(rows < band_start + nb, 0.0, NEG), NEG)
        imp = jnp.zeros((n_r, tq), F32)
        for h in range(HPG):
            qh = q_ref[:, h * HEAD_DIM:(h + 1) * HEAD_DIM]
            s_scr[:n_r, :] = _dot_nt(kcc, qh) + row_bias
            s_scr[pl.ds(band_start, nb), :] += band_ref[h]
            s = s_scr[:n_r, :]
            m = jnp.max(s, axis=0, keepdims=True)
            e = jnp.where(s > 0.5 * NEG, jnp.exp2(s - m), 0.0)
            l = jnp.sum(e, axis=0, keepdims=True)
            pn = e * (1.0 / jnp.where(l > 0.0, l, 1.0))
            imp = imp + pn
            oct_h = _dot(vcct, pn.astype(BF16))
            oc_ref[:, h * HEAD_DIM:(h + 1) * HEAD_DIM] = oct_h.T
        mt = mt_ref[:n_b, :n_r]
        hi = imp.astype(BF16)
        rest = imp - hi.astype(F32)
        mid = rest.astype(BF16)
        lo = (rest - mid.astype(F32)).astype(BF16)
        slc = _dot(mt, hi) + _dot(mt, mid) + _dot(mt, lo)
        jj = lax.broadcasted_iota(jnp.int32, (n_b, tq), 0)
        t_blk = (i * tq + lax.broadcasted_iota(jnp.int32, (n_b, tq), 1)) // SEL_BLOCK
        forced = jnp.where(jj == 0, 1, jnp.where(jj == t_blk, 1, jnp.where(jj == t_blk - 1, 1, 0)))
        picked = jnp.where(forced == 1, 1.0, 0.0)
        score = jnp.where(forced == 1, -jnp.inf, jnp.where(jj <= t_blk, slc, NEG))
        for _ in range(SEL_TOPK - N_FORCED):
            best = jnp.max(score, axis=0, keepdims=True)
            first = jnp.min(jnp.where(score == best, jj, n_blk), axis=0, keepdims=True)
            hit = jj == first
            picked = jnp.where(hit, 1.0, picked)
            score = jnp.where(hit, -jnp.inf, score)
        neg_mask = jnp.where(picked > 0.5, jnp.where(jj <= t_blk, 0.0, NEG), NEG)
        if n_b < n_blk:
            neg_mask = jnp.concatenate([neg_mask, jnp.full((n_blk - n_b, tq), NEG, F32)], axis=0)
        nm_ref[...] = neg_mask.T.astype(nm_ref.dtype)

    step = LANES
    extents = list(range(step, n_rows, step)) + [n_rows]
    need_rows = band_start + nb
    for v, n_r in enumerate(extents):
        lower = extents[v - 1] if v else 0
        n_b = min(n_blk, -(-(n_r * CMP_STRIDE // SEL_BLOCK) // SUBLANES) * SUBLANES)
        pl.when(jnp.logical_and(need_rows > lower, need_rows <= n_r))(functools.partial(body, n_r, n_b))


def _cmp_attn(proj, kcc, vcct, band, mt, tq):
    s = proj.shape[0]
    n_rows = kcc.shape[1]
    nb = band.shape[1]
    kern = functools.partial(_cmp_attn_kernel, tq=tq, nb=nb)
    return pl.pallas_call(
        kern,
        out_shape=(jax.ShapeDtypeStruct((s, Q_WIDTH), F32),
                   jax.ShapeDtypeStruct((s, N_KV_GROUPS * LANES), BF16)),
        grid=(N_KV_GROUPS, s // tq),
        in_specs=[
            pl.BlockSpec((tq, HPG * HEAD_DIM), lambda g, i: (i, g)),
            pl.BlockSpec((1, n_rows, HEAD_DIM), lambda g, i: (g, 0, 0)),
            pl.BlockSpec((1, HEAD_DIM, n_rows), lambda g, i: (g, 0, 0)),
            pl.BlockSpec((HPG, nb, tq), lambda g, i: (g, 0, 0)),
            pl.BlockSpec((LANES, n_rows), lambda g, i: (0, 0)),
        ],
        out_specs=(pl.BlockSpec((tq, HPG * HEAD_DIM), lambda g, i: (i, g)),
                   pl.BlockSpec((tq, LANES), lambda g, i: (i, g))),
        scratch_shapes=[pltpu.VMEM((n_rows, tq), F32)],
        compiler_params=_params("parallel", "parallel"),
    )(proj, kcc, vcct, band, mt)


def _lane_tile(x, width):
    return jnp.concatenate([x] * (width // x.shape[1]), axis=1)


def _softmax_update(s, v_aug, m_ref, acc_ref, rows):
    m_prev = m_ref[rows, :]
    m_next = jnp.maximum(m_prev, jnp.max(s, axis=1, keepdims=True))
    alpha = jnp.exp2(m_prev - m_next)
    p = jnp.exp2(s - _lane_tile(m_next, s.shape[1]))
    acc_ref[rows, :] = _lane_tile(alpha, acc_ref.shape[1]) * acc_ref[rows, :] + _dot(p.astype(BF16), v_aug)
    m_ref[rows, :] = m_next


def _softmax_once(s, v_aug):
    m = jnp.broadcast_to(jnp.max(s, axis=1, keepdims=True), (s.shape[0], LANES))
    p = jnp.exp2(s - _lane_tile(m, s.shape[1]))
    return _dot(p.astype(BF16), v_aug)


def _sel_win_kernel(q_ref, nm_ref, ks_ref, vs_ref, kw_ref, vw_ref, blk_ref, per_ref, g_ref, oc_ref, o_ref,
                    q4, m_s, a_s, a_w, bias_ref, *, tile):
    i = pl.program_id(1)
    t = tile

    @pl.when(i == 0)
    def _():
        tl = lax.broadcasted_iota(jnp.int32, (t, t), 0)
        kl = lax.broadcasted_iota(jnp.int32, (t, t), 1)
        edge = jnp.where(tl < kl, 0.0, NEG)
        for h in range(HPG):
            bias_ref[h, :, :t] = edge
            for which in range(2):
                rows = jnp.broadcast_to(per_ref[h, which:which + 1, :], (t, 2 * t))
                table = pltpu.roll(rows, 0, 1, stride=1, stride_axis=0)
                bias_ref[h, :, (which + 1) * t:(which + 2) * t] = table[:, :t]
    for h in range(HPG):
        q4[h * t:(h + 1) * t, :HEAD_DIM] = q_ref[:, h * HEAD_DIM:(h + 1) * HEAD_DIM]
        q4[h * t:(h + 1) * t, HEAD_DIM:] = nm_ref[...]
    m_s[...] = jnp.full(m_s.shape, -3e38, F32)
    a_s[...] = jnp.zeros(a_s.shape, F32)

    def sel_step(row0, n_tiles, n_biased):
        keys = pl.ds(row0, n_tiles * t)
        k_aug = jnp.concatenate([ks_ref[keys, :], blk_ref[keys, :]], axis=1)
        v_aug = jnp.concatenate([vs_ref[keys, :], jnp.ones((n_tiles * t, LANES), BF16)], axis=1)
        plain = (n_tiles - n_biased) * t
        for h in range(HPG):
            rows = slice(h * t, (h + 1) * t)
            s = _dot_nt(q4[rows, :], k_aug)
            if n_biased:
                near = s[:, plain:] + bias_ref[h, :, (3 - n_biased) * t:]
                s = jnp.concatenate([s[:, :plain], near], axis=1) if plain else near
            _softmax_update(s, v_aug, m_s, a_s, rows)

    def win_once(row0, n_tiles):
        keys = pl.ds(row0, n_tiles * t)
        v_aug = jnp.concatenate([vw_ref[keys, :], jnp.ones((n_tiles * t, LANES), BF16)], axis=1)
        for h in range(HPG):
            rows = slice(h * t, (h + 1) * t)
            s = _dot_nt(q4[rows, :HEAD_DIM], kw_ref[keys, :]) + bias_ref[h, :, (3 - n_tiles) * t:]
            a_w[rows, :] = _softmax_once(s, v_aug)

    n_far = jnp.maximum(i - 1, 0)

    n_chunks = n_far // FAR_TILES
    chunk = FAR_TILES * t

    def far_pair(c, carry):
        row0 = pl.multiple_of(c * 2 * chunk, 2 * chunk)
        sel_step(row0, FAR_TILES, 0)
        sel_step(row0 + chunk, FAR_TILES, 0)
        return carry

    lax.fori_loop(0, n_chunks // 2, far_pair, 0)

    @pl.when(n_chunks % 2 == 1)
    def _():
        sel_step(pl.multiple_of((n_chunks - 1) * chunk, chunk), FAR_TILES, 0)

    for left in range(FAR_TILES):
        @pl.when(jnp.logical_and(i >= 2, n_far % FAR_TILES == left))
        def _(left=left):
            sel_step(pl.multiple_of((i - 1 - left) * t, t), left + 2, 2)
            win_once(pl.multiple_of((i - 2) * t, t), 3)

    @pl.when(i == 1)
    def _():
        sel_step(0, 2, 2)
        win_once(0, 2)

    @pl.when(i == 0)
    def _():
        sel_step(0, 1, 1)
        win_once(0, 1)

    acc_s = a_s[...]
    acc_w = a_w[...]
    o_s = acc_s[:, :HEAD_DIM] * (1.0 / acc_s[:, HEAD_DIM:])
    o_w = acc_w[:, :HEAD_DIM] * (1.0 / acc_w[:, HEAD_DIM:])
    gates = g_ref[...]
    for h in range(HPG):
        cols = slice(h * HEAD_DIM, (h + 1) * HEAD_DIM)
        o = (gates[:, h:h + 1] * oc_ref[:, cols]
             + gates[:, HPG + h:HPG + h + 1] * o_s[h * t:(h + 1) * t]
             + gates[:, 2 * HPG + h:2 * HPG + h + 1] * o_w[h * t:(h + 1) * t])
        o_ref[:, cols] = o.astype(o_ref.dtype)


def _sel_win(proj, neg_mask, blk_onehot, periods, gates, o_c, tile):
    s = proj.shape[0]
    ks_blk = Q_WIDTH // HEAD_DIM
    vs_blk = (Q_WIDTH + KV_WIDTH) // HEAD_DIM
    kw_blk = (Q_WIDTH + 2 * KV_WIDTH) // HEAD_DIM
    vw_blk = (Q_WIDTH + 3 * KV_WIDTH) // HEAD_DIM
    once = pl.Buffered(1)
    rows4 = HPG * tile
    kern = functools.partial(_sel_win_kernel, tile=tile)
    return pl.pallas_call(
        kern,
        out_shape=jax.ShapeDtypeStruct((s, Q_WIDTH), BF16),
        grid=(N_KV_GROUPS, s // tile),
        in_specs=[
            pl.BlockSpec((tile, HPG * HEAD_DIM), lambda g, i: (i, g)),
            pl.BlockSpec((tile, LANES), lambda g, i: (i, g)),
            pl.BlockSpec((s, HEAD_DIM), lambda g, i: (0, ks_blk + g), pipeline_mode=once),
            pl.BlockSpec((s, HEAD_DIM), lambda g, i: (0, vs_blk + g), pipeline_mode=once),
            pl.BlockSpec((s, HEAD_DIM), lambda g, i: (0, kw_blk + g), pipeline_mode=once),
            pl.BlockSpec((s, HEAD_DIM), lambda g, i: (0, vw_blk + g), pipeline_mode=once),
            pl.BlockSpec((s, LANES), lambda g, i: (0, 0), pipeline_mode=once),
            pl.BlockSpec((HPG, 2, 2 * tile), lambda g, i: (g, 0, 0)),
            pl.BlockSpec((tile, LANES), lambda g, i: (i, g)),
            pl.BlockSpec((tile, HPG * HEAD_DIM), lambda g, i: (i, g)),
        ],
        out_specs=pl.BlockSpec((tile, HPG * HEAD_DIM), lambda g, i: (i, g)),
        scratch_shapes=[
            pltpu.VMEM((rows4, 2 * HEAD_DIM), BF16),
            pltpu.VMEM((rows4, LANES), F32),
            pltpu.VMEM((rows4, 2 * HEAD_DIM), F32),
            pltpu.VMEM((rows4, 2 * HEAD_DIM), F32),
            pltpu.VMEM((HPG, tile, 3 * tile), F32),
        ],
        compiler_params=_params("arbitrary", "arbitrary"),
    )(proj, neg_mask, proj, proj, proj, proj, blk_onehot, periods, gates, o_c)


def _rel_bucket_np(dist):
    n = np.maximum(dist, 0)
    max_exact = REL_BUCKETS // 2
    large = max_exact + (np.log(np.maximum(n, 1).astype(np.float32) / np.float32(max_exact))
                         / np.float32(math.log(REL_MAX_DIST / max_exact))
                         * np.float32(REL_BUCKETS - max_exact)).astype(np.int32)
    large = np.minimum(large, REL_BUCKETS - 1)
    return np.where(n < max_exact, n, large).astype(np.int32)


def _far_distance():
    d = np.arange(4 * REL_MAX_DIST)
    b = _rel_bucket_np(d)
    assert b[-1] == REL_BUCKETS - 1
    return int(np.max(np.nonzero(b != REL_BUCKETS - 1)[0])) + 1


def _bias_table(rel_bias, dist):
    bucket = jnp.asarray(_rel_bucket_np(dist))
    shifted = (rel_bias[bucket] - rel_bias[REL_BUCKETS - 1]) * LOG2E
    tab = jnp.where(jnp.asarray(dist >= 0)[..., None], shifted, NEG)
    return jnp.moveaxis(tab, -1, 0).astype(F32)


def _bias_period(rel_bias, offset, t):
    x = np.arange(2 * t)
    return _bias_table(rel_bias, np.where(x < t, offset - x, offset + 2 * t - x))


def _nsa_mixer(h, hn, rel_bias, w_in, w_out, layer, pos_k, w1_k, w2_k, pos_v, w1_v, w2_v):
    s = h.shape[0]
    n_sel = s // SEL_BLOCK
    ncp = s // CMP_STRIDE
    tile = ATT_TILE
    far = _far_distance()
    assert n_sel <= LANES and s % CMP_TQ == 0 and s % tile == 0
    assert far <= tile and far <= CMP_STRIDE * CMP_PAD - CMP_BLOCK + 1 + CMP_STRIDE

    col_scale = jnp.where(jnp.arange(PROJ_ATT) < Q_WIDTH, HEAD_DIM ** -0.5 * LOG2E, 1.0).astype(F32)[None, :]
    w_in_t = jnp.swapaxes(w_in, 1, 2)
    tn = KV_WIDTH
    q_blocks, cmp_blocks = Q_WIDTH // tn, 2 * KV_WIDTH // tn
    proj = _matmul(_ep_scale, hn, w_in_t, layer, PROJ_ATT, BF16, 1024, tn, row_extra=col_scale,
                   w_rows_are_outputs=True, src_block=lambda j: jnp.where(j < q_blocks, j, j + cmp_blocks))
    kcvc = _matmul(lambda z: z, hn, w_in_t, layer, 2 * KV_WIDTH, F32, 1024, tn,
                   w_rows_are_outputs=True, src_block=lambda j: j + q_blocks)
    w_gate = w_in_t[layer, PROJ_MAIN:].reshape(N_GATES, N_KV_GROUPS, HPG, D_MODEL).transpose(1, 0, 2, 3)
    w_gate = w_gate.reshape(N_KV_GROUPS, N_GATES * HPG, D_MODEL)
    w_gate = jnp.pad(w_gate, ((0, 0), (0, LANES - N_GATES * HPG), (0, 0))).reshape(1, N_KV_GROUPS * LANES, D_MODEL)
    gates = _matmul(jax.nn.sigmoid, hn, w_gate, 0, N_KV_GROUPS * LANES, F32, 1024, 512, w_rows_are_outputs=True)

    w1 = jnp.stack([w1_k, w1_v]).astype(BF16)
    w2 = jnp.stack([w2_k, w2_v]).astype(BF16)
    pos = jnp.stack([pos_k.reshape(1, -1), pos_v.reshape(1, -1)]).astype(BF16)
    pos = jnp.broadcast_to(pos, (2, SUBLANES, CMP_BLOCK * HEAD_DIM))
    cc = _compress(kcvc, w1, w2, pos)
    kcc = jnp.pad(cc[0], ((0, 0), (CMP_PAD, 0), (0, 0)))
    vcct = jnp.pad(cc[1], ((0, 0), (CMP_PAD, 0), (0, 0))).transpose(0, 2, 1)
    n_rows = ncp + CMP_PAD

    nb = CMP_TQ // CMP_STRIDE + CMP_PAD
    first_dist = -CMP_STRIDE * (nb - 1 - CMP_PAD) - (CMP_BLOCK - 1)
    by_dist = _bias_table(rel_bias, first_dist + np.arange(CMP_TQ + CMP_STRIDE * (nb - 1)))
    band = jnp.stack([by_dist[:, CMP_STRIDE * (nb - 1 - b):CMP_STRIDE * (nb - 1 - b) + CMP_TQ] for b in range(nb)], axis=1)
    ratio = SEL_BLOCK // CMP_STRIDE
    lo = CMP_BLOCK // CMP_STRIDE - 1
    c_of_row = np.arange(n_rows)[None, :] - CMP_PAD
    j_of = np.arange(LANES)[:, None]
    mt = ((c_of_row >= ratio * j_of - lo) & (c_of_row <= ratio * j_of + ratio - 1)
          & (c_of_row >= 0) & (c_of_row <= ncp - 2) & (j_of < n_sel)).astype(np.float32)
    o_c, neg_mask = _cmp_attn(proj, kcc, vcct, band, jnp.asarray(mt, BF16), CMP_TQ)

    onehot = (np.arange(s)[:, None] // SEL_BLOCK == np.arange(LANES)[None, :]).astype(np.float32)
    periods = jnp.stack([_bias_period(rel_bias, tile, tile), _bias_period(rel_bias, 0, tile)], axis=1)
    o = _sel_win(proj, neg_mask, jnp.asarray(onehot, BF16), periods, gates, o_c, tile)
    return _matmul(_ep_resid, o, w_out, layer, D_MODEL, F32, 1024, 512, tile_extra=h)


def _s5_kernel(*refs, chunk):
    L = chunk
    x_refs = refs[:L]
    lam_ref, bt_ref, c_ref, d_ref, y_ref, vt, yt, ys, tmask = refs[L:]
    half = L // 2
    width = L * SSM_GROUP
    n_chunks = x_refs[0].shape[0]
    for tau in range(L):
        vt[tau] = x_refs[tau][...].T

    @pl.when(pl.program_id(0) == 0)
    def _():
        dst = lax.broadcasted_iota(jnp.int32, (width, width), 0) // SSM_GROUP
        src = lax.broadcasted_iota(jnp.int32, (width, width), 1) // SSM_GROUP
        tmask[...] = jnp.where(src <= dst, 1.0, 0.0)

    def cmul(xr, xi, yr, yi):
        return xr * yr - xi * yi, xr * yi + xi * yr

    n2 = 2 * SSM_STATE
    sub = lax.broadcasted_iota(jnp.int32, (SUBLANES, 1), 0)
    consts = jnp.where(sub == 0, 1.0, jnp.where(sub == 1, float(half + 1), jnp.where(sub == 2, float(half - 1), float(L))))
    tau_col = lax.broadcasted_iota(jnp.int32, (L, 1), 0).astype(F32)
    row = lax.broadcasted_iota(jnp.int32, (n_chunks, n2), 0)
    conj = jnp.where(lax.broadcasted_iota(jnp.int32, (1, n2), 1) < SSM_STATE, 1.0, -1.0)

    def group(g, carry):
        lam = lam_ref[g]
        a_re, a_im = lam[0:1], lam[1:2]
        dt = jnp.exp(lam[2:3])
        log_re, log_im = a_re * dt, a_im * dt

        def cpow(e):
            mag, cos, sin = jnp.exp(e * log_re), jnp.cos(e * log_im), jnp.sin(e * log_im)
            inv = 1.0 / mag
            return mag * cos, mag * sin, inv * cos, -(inv * sin)

        kr, ki, _, _ = cpow(consts)
        pr, pi = kr[0:1] - 1.0, ki[0:1]
        den = a_re * a_re + a_im * a_im
        cf_r = (pr * a_re + pi * a_im) / den
        cf_i = (pi * a_re - pr * a_im) / den
        bt_c, bt_s = bt_ref[g, 0], bt_ref[g, 1]
        bb_c = cf_r * bt_c + cf_i * bt_s
        bb_s = cf_r * bt_s - cf_i * bt_c
        c_c, c_s = c_ref[g, 0], c_ref[g, 1]

        e1r, e1i, e2r, e2i = cpow(tau_col - half)
        e3r, e3i = cmul(e1r, e1i, kr[1:2], ki[1:2])
        e4r, e4i = cmul(e2r, e2i, kr[2:3], ki[2:3])

        def outer(er, ei, w_c, w_s):
            return (er[:, None, :] * w_c[None, :, :] + ei[:, None, :] * w_s[None, :, :]).reshape(width, n2)

        qm = outer(e1r, e1i, c_c, c_s)
        km_conj = outer(e2r, e2i, bb_c * conj, bb_s * conj)
        qc = outer(e3r, e3i, c_c, c_s)
        wz = outer(e4r, e4i, bb_c, bb_s)

        chans = pl.ds(pl.multiple_of(g * SSM_GROUP, SSM_GROUP), SSM_GROUP)
        ub = vt[:, chans, :].reshape(width, n_chunks).astype(BF16)

        tt = _dot_nt(qm.astype(BF16), km_conj.astype(BF16))
        y = _dot((tt * tmask[...]).astype(BF16), ub)

        x = _dot(wz.T.astype(BF16), ub).T
        p_r, p_i = kr[3:4], ki[3:4]
        d = 1
        while d < n_chunks:
            s = jnp.where(row >= d, pltpu.roll(x, d, 0), 0.0)
            x = x + s * p_r - pltpu.roll(s, SSM_STATE, 1) * (p_i * conj)
            p_r, p_i = cmul(p_r, p_i, p_r, p_i)
            d *= 2
        prev_conj = jnp.where(row >= 1, pltpu.roll(x, 1, 0), 0.0) * conj
        y = y + _dot_nt(qc.astype(BF16), prev_conj.astype(BF16))
        yt[:, chans, :] = y.reshape(L, SSM_GROUP, n_chunks)

    def group_batch(k, carry):
        for j in range(S5_GROUPS_PER_TRIP):
            group(S5_GROUPS_PER_TRIP * k + j, carry)
        return carry

    lax.fori_loop(0, LANES // SSM_GROUP // S5_GROUPS_PER_TRIP, group_batch, 0)
    for tau in range(L):
        ys[pl.ds(tau, n_chunks, stride=L), :] = yt[tau].T + d_ref[...] * x_refs[tau][...]
    y_ref[...] = jax.nn.gelu(ys[...]).astype(y_ref.dtype)


def _s5_mixer(h, hn, a_re, a_im, log_dt, b_re, b_im, c_re, c_im, d_skip, w_glu, layer):
    s = h.shape[0]
    L = S5_CHUNK
    n_chunks = s // L
    width = L * SSM_GROUP
    gpb = LANES // SSM_GROUP
    lane_blocks = D_MODEL // LANES
    def packed(re, im):
        return jnp.stack([jnp.concatenate([re, im], -1), jnp.concatenate([-im, re], -1)], axis=1).astype(F32)

    lam = jnp.stack([a_re, a_im, jnp.broadcast_to(log_dt[:, None], a_re.shape)], axis=1).astype(F32)
    lam = jnp.concatenate([lam, lam], axis=-1)
    bt = packed(b_re.transpose(0, 2, 1), b_im.transpose(0, 2, 1))
    cc = packed(c_re, c_im)
    dd = d_skip.astype(F32).reshape(1, D_MODEL)
    y = pl.pallas_call(
        functools.partial(_s5_kernel, chunk=L),
        out_shape=jax.ShapeDtypeStruct((s, D_MODEL), BF16),
        grid=(SSM_GROUPS // gpb,),
        in_specs=[pl.BlockSpec((n_chunks, LANES), functools.partial(lambda b, tau: (0, tau * lane_blocks + b), tau=tau))
                  for tau in range(L)] + [
            pl.BlockSpec((gpb, 3, 2 * SSM_STATE), lambda b: (b, 0, 0)),
            pl.BlockSpec((gpb, 2, SSM_GROUP, 2 * SSM_STATE), lambda b: (b, 0, 0, 0)),
            pl.BlockSpec((gpb, 2, SSM_GROUP, 2 * SSM_STATE), lambda b: (b, 0, 0, 0)),
            pl.BlockSpec((1, LANES), lambda b: (0, b)),
        ],
        out_specs=pl.BlockSpec((s, LANES), lambda b: (0, b)),
        scratch_shapes=[
            pltpu.VMEM((L, LANES, n_chunks), F32),
            pltpu.VMEM((L, LANES, n_chunks), F32),
            pltpu.VMEM((s, LANES), F32),
            pltpu.VMEM((width, width), F32),
        ],
        compiler_params=_params("arbitrary"),
    )(*([hn.reshape(n_chunks, L * D_MODEL)] * L), lam, bt, cc, dd)
    return _matmul(_ep_glu_resid, y, w_glu, layer, D_MODEL, F32, 1024, 512, b_halves=2, tile_extra=h)


def kernel(x, rel_bias, mix_norm_g, ffn_norm_g, final_norm_g, nsa_w_in, nsa_w_out, cmp_pos_k, cmp_w1_k, cmp_w2_k, cmp_pos_v, cmp_w1_v, cmp_w2_v, s5_A_re, s5_A_im, s5_log_dt, s5_B_re, s5_B_im, s5_C_re, s5_C_im, s5_D, s5_w_glu, ffn_w_in, ffn_w_out):
    assert x.shape[0] == 1
    h = x[0]
    hn = _rmsnorm(h, mix_norm_g[0], BF16)
    h = _nsa_mixer(h, hn, rel_bias, nsa_w_in, nsa_w_out, 0, cmp_pos_k[0], cmp_w1_k[0], cmp_w2_k[0],
                   cmp_pos_v[0], cmp_w1_v[0], cmp_w2_v[0])
    h = _swiglu_ffn(h, ffn_norm_g[0], ffn_w_in, ffn_w_out, 0)
    hn = _rmsnorm(h, mix_norm_g[1], F32)
    h = _s5_mixer(h, hn, s5_A_re[0], s5_A_im[0], s5_log_dt[0], s5_B_re[0], s5_B_im[0], s5_C_re[0], s5_C_im[0],
                  s5_D[0], s5_w_glu, 0)
    h = _swiglu_ffn(h, ffn_norm_g[1], ffn_w_in, ffn_w_out, 1)
    return _rmsnorm(h, final_norm_g, x.dtype)[None]
```

```python
import functools
import math

import numpy as np
import jax
import jax.numpy as jnp
from jax import lax
from jax.experimental import pallas as pl
from jax.experimental.pallas import tpu as pltpu

D_MODEL = 2048
N_HEADS = 16
HEAD_DIM = 128
N_KV_GROUPS = 4
HPG = N_HEADS // N_KV_GROUPS
CMP_BLOCK = 32
CMP_STRIDE = 16
CMP_HIDDEN = 2 * HEAD_DIM
SEL_BLOCK = 64
SEL_TOPK = 16
N_FORCED = 3
WINDOW = 512
N_GATES = 3
KV_WIDTH = N_KV_GROUPS * HEAD_DIM
Q_WIDTH = N_HEADS * HEAD_DIM
PROJ_MAIN = Q_WIDTH + 6 * KV_WIDTH
PROJ_ATT = Q_WIDTH + 4 * KV_WIDTH
REL_BUCKETS = 32
REL_MAX_DIST = 128
SSM_GROUP = 16
SSM_GROUPS = D_MODEL // SSM_GROUP
SSM_STATE = 64
D_FF = ((8 * D_MODEL + 2) // 3 + 255) // 256 * 256
RMS_EPS = 1e-6
NEG = -1e30
FORCE_SCORE = 1e9
LOG2E = math.log2(math.e)

LANES = 128
SUBLANES = 8
VMEM_LIMIT = 52 * 1024 * 1024

ATT_TILE = WINDOW // 2
FAR_TILES = 4
CMP_TQ = 256
CMP_PAD = 8
S5_CHUNK = 32
S5_GROUPS_PER_TRIP = 4

BF16 = jnp.bfloat16
F32 = jnp.float32


def _dot(a, b):
    return jnp.dot(a, b, preferred_element_type=F32)


def _dot_nt(a, b):
    return lax.dot_general(a, b, (((1,), (1,)), ((), ())), preferred_element_type=F32)


def _params(*sem):
    return pltpu.CompilerParams(dimension_semantics=sem, vmem_limit_bytes=VMEM_LIMIT)


def _rmsnorm_kernel(x_ref, g_ref, o_ref):
    x = x_ref[...]
    ms = jnp.mean(x * x, axis=-1, keepdims=True)
    o_ref[...] = (x * lax.rsqrt(ms + RMS_EPS) * g_ref[...]).astype(o_ref.dtype)


def _rmsnorm(x, g, out_dtype, tm=512):
    s, d = x.shape
    return pl.pallas_call(
        _rmsnorm_kernel,
        out_shape=jax.ShapeDtypeStruct((s, d), out_dtype),
        grid=(s // tm,),
        in_specs=[pl.BlockSpec((tm, d), lambda i: (i, 0)), pl.BlockSpec((1, d), lambda i: (0, 0))],
        out_specs=pl.BlockSpec((tm, d), lambda i: (i, 0)),
        compiler_params=_params("parallel"),
    )(x, g.reshape(1, d))


def _mm_kernel(*refs, n_b, n_extra, epilogue, w_rows_are_outputs):
    a_ref = refs[0]
    b_refs = refs[1:1 + n_b]
    extra_refs = refs[1 + n_b:1 + n_b + n_extra]
    o_ref = refs[1 + n_b + n_extra]
    w_scr = refs[2 + n_b + n_extra:]

    @pl.when(pl.program_id(1) == 0)
    def _():
        for b_ref, w in zip(b_refs, w_scr):
            w[...] = b_ref[...].astype(BF16)

    a = a_ref[...]
    z = [(_dot_nt if w_rows_are_outputs else _dot)(a, w[...]) for w in w_scr]
    o_ref[...] = epilogue(*z, *[e[...] for e in extra_refs]).astype(o_ref.dtype)


def _ep_scale(z, scale):
    return z * scale


def _ep_resid(z, resid):
    return resid + z


def _ep_swiglu(za, zb):
    return jax.nn.silu(za) * zb


def _ep_glu_resid(za, zb, resid):
    return resid + za * jax.nn.sigmoid(zb)


def _matmul(epilogue, a, b, layer, n_out, out_dtype, tm, tn, *, b_halves=1, row_extra=None, tile_extra=None,
            w_rows_are_outputs=False, src_block=lambda j: j):
    m, k = a.shape
    nj = n_out // tn
    b_mode = pl.Buffered(1) if 2 * b_halves * k * tn * 4 > VMEM_LIMIT // 3 else None
    in_specs = [pl.BlockSpec((tm, k), lambda j, i: (i, 0))]
    args = [a]
    for half in range(b_halves):
        if w_rows_are_outputs:
            spec = pl.BlockSpec((None, tn, k), functools.partial(lambda j, i, o: (layer, src_block(j) + o, 0), o=half * nj),
                                pipeline_mode=b_mode)
        else:
            spec = pl.BlockSpec((None, k, tn), functools.partial(lambda j, i, o: (layer, 0, src_block(j) + o), o=half * nj),
                                pipeline_mode=b_mode)
        in_specs.append(spec)
        args.append(b)
    extras = []
    if row_extra is not None:
        in_specs.append(pl.BlockSpec((1, tn), lambda j, i: (0, j)))
        extras.append(row_extra)
    if tile_extra is not None:
        in_specs.append(pl.BlockSpec((tm, tn), lambda j, i: (i, j)))
        extras.append(tile_extra)
    kern = functools.partial(_mm_kernel, n_b=b_halves, n_extra=len(extras), epilogue=epilogue,
                             w_rows_are_outputs=w_rows_are_outputs)
    return pl.pallas_call(
        kern,
        out_shape=jax.ShapeDtypeStruct((m, n_out), out_dtype),
        grid=(nj, m // tm),
        in_specs=in_specs,
        out_specs=pl.BlockSpec((tm, tn), lambda j, i: (i, j)),
        scratch_shapes=[pltpu.VMEM((tn, k) if w_rows_are_outputs else (k, tn), BF16) for _ in range(b_halves)],
        compiler_params=_params("arbitrary", "arbitrary"),
    )(*args, *extras)


def _swiglu_ffn(h, g, w_in, w_out, layer):
    hn = _rmsnorm(h, g, BF16)
    act = _matmul(_ep_swiglu, hn, w_in, layer, D_FF, BF16, 1024, 512, b_halves=2)
    return _matmul(_ep_resid, act, w_out, layer, D_MODEL, F32, 512, 512, tile_extra=h)


def _compress_kernel(x_ref, w1_ref, w2_ref, pos_ref, o_ref):
    ncp = x_ref.shape[0] // CMP_STRIDE
    top = jnp.zeros((ncp, CMP_HIDDEN), F32)
    bot = jnp.zeros((ncp, CMP_HIDDEN), F32)
    for r in range(CMP_STRIDE):
        x_r = x_ref[pl.ds(r, ncp, stride=CMP_STRIDE), :].astype(BF16)
        top = top + _dot(x_r, w1_ref[0, r * HEAD_DIM:(r + 1) * HEAD_DIM, :])
        bot = bot + _dot(x_r, w1_ref[0, (CMP_STRIDE + r) * HEAD_DIM:(CMP_STRIDE + r + 1) * HEAD_DIM, :])
    posb = _dot(pos_ref[0], w1_ref[0])[0:1]
    hid = top + pltpu.roll(bot, ncp - 1, 0) + posb
    o_ref[0, 0] = _dot(jax.nn.gelu(hid).astype(BF16), w2_ref[0]).astype(o_ref.dtype)


def _compress(kcvc, w1, w2, pos):
    s = kcvc.shape[0]
    ncp = s // CMP_STRIDE
    return pl.pallas_call(
        _compress_kernel,
        out_shape=jax.ShapeDtypeStruct((2, N_KV_GROUPS, ncp, HEAD_DIM), BF16),
        grid=(2, N_KV_GROUPS),
        in_specs=[
            pl.BlockSpec((s, HEAD_DIM), lambda kv, gi: (0, kv * N_KV_GROUPS + gi)),
            pl.BlockSpec((1, CMP_BLOCK * HEAD_DIM, CMP_HIDDEN), lambda kv, gi: (kv, 0, 0)),
            pl.BlockSpec((1, CMP_HIDDEN, HEAD_DIM), lambda kv, gi: (kv, 0, 0)),
            pl.BlockSpec((1, SUBLANES, CMP_BLOCK * HEAD_DIM), lambda kv, gi: (kv, 0, 0)),
        ],
        out_specs=pl.BlockSpec((1, 1, ncp, HEAD_DIM), lambda kv, gi: (kv, gi, 0, 0)),
        compiler_params=_params("parallel", "parallel"),
    )(kcvc, w1, w2, pos)


def _cmp_attn_kernel(q_ref, kcc_ref, vcct_ref, band_ref, mt_ref, oc_ref, nm_ref, s_scr, *, tq, nb):
    i = pl.program_id(1)
    band_start = pl.multiple_of(i * (tq // CMP_STRIDE), SUBLANES)
    n_rows = kcc_ref.shape[1]
    n_blk = mt_ref.shape[0]

    def body(n_r, n_b):
        kcc = kcc_ref[0, :n_r, :]
        vcct = vcct_ref[0, :, :n_r]
        rows = lax.broadcasted_iota(jnp.int32, (n_r, tq), 0)
        row_bias = jnp.where(rows >= CMP_PAD, jnp.where(rows < band_start + nb, 0.0, NEG), NEG)
        imp = jnp.zeros((n_r, tq), F32)
        for h in range(HPG):
            qh = q_ref[:, h * HEAD_DIM:(h + 1) * HEAD_DIM]
            s_scr[:n_r, :] = _dot_nt(kcc, qh) + row_bias
            s_scr[pl.ds(band_start, nb), :] += band_ref[h]
            s = s_scr[:n_r, :]
            m = jnp.maximum(jnp.max(s, axis=0, keepdims=True), 0.5 * NEG)
            e = jnp.exp2(s - m)
            l = jnp.sum(e, axis=0, keepdims=True)
            pn = e * (1.0 / jnp.where(l > 0.0, l, 1.0))
            imp = imp + pn
            oct_h = _dot(vcct, pn.astype(BF16))
            oc_ref[:, h * HEAD_DIM:(h + 1) * HEAD_DIM] = oct_h.T
        mt = mt_ref[:n_b, :n_r]
        hi = imp.astype(BF16)
        rest = imp - hi.astype(F32)
        mid = rest.astype(BF16)
        lo = (rest - mid.astype(F32)).astype(BF16)
        slc = _dot(mt, hi) + _dot(mt, mid) + _dot(mt, lo)
        jj = lax.broadcasted_iota(jnp.int32, (n_b, tq), 0)
        t_blk = (i * tq + lax.broadcasted_iota(jnp.int32, (n_b, tq), 1)) // SEL_BLOCK
        forced = jnp.where(jj == 0, 1, jnp.where(jj == t_blk, 1, jnp.where(jj == t_blk - 1, 1, 0)))
        picked = jnp.where(forced == 1, 1.0, 0.0)
        score = jnp.where(forced == 1, -jnp.inf, jnp.where(jj <= t_blk, slc, NEG))
        for _ in range(SEL_TOPK - N_FORCED):
            best = jnp.max(score, axis=0, keepdims=True)
            first = jnp.min(jnp.where(score == best, jj, n_blk), axis=0, keepdims=True)
            hit = jj == first
            picked = jnp.where(hit, 1.0, picked)
            score = jnp.where(hit, -jnp.inf, score)
        neg_mask = jnp.where(picked > 0.5, jnp.where(jj <= t_blk, 0.0, NEG), NEG)
        if n_b < n_blk:
            neg_mask = jnp.concatenate([neg_mask, jnp.full((n_blk - n_b, tq), NEG, F32)], axis=0)
        nm_ref[...] = neg_mask.T.astype(nm_ref.dtype)

    step = LANES
    extents = list(range(step, n_rows, step)) + [n_rows]
    need_rows = band_start + nb
    for v, n_r in enumerate(extents):
        lower = extents[v - 1] if v else 0
        n_b = min(n_blk, -(-(n_r * CMP_STRIDE // SEL_BLOCK) // SUBLANES) * SUBLANES)
        pl.when(jnp.logical_and(need_rows > lower, need_rows <= n_r))(functools.partial(body, n_r, n_b))


def _cmp_attn(proj, kcc, vcct, band, mt, tq):
    s = proj.shape[0]
    n_rows = kcc.shape[1]
    nb = band.shape[1]
    kern = functools.partial(_cmp_attn_kernel, tq=tq, nb=nb)
    return pl.pallas_call(
        kern,
        out_shape=(jax.ShapeDtypeStruct((s, Q_WIDTH), F32),
                   jax.ShapeDtypeStruct((s, N_KV_GROUPS * LANES), BF16)),
        grid=(N_KV_GROUPS, s // tq),
        in_specs=[
            pl.BlockSpec((tq, HPG * HEAD_DIM), lambda g, i: (i, g)),
            pl.BlockSpec((1, n_rows, HEAD_DIM), lambda g, i: (g, 0, 0)),
            pl.BlockSpec((1, HEAD_DIM, n_rows), lambda g, i: (g, 0, 0)),
            pl.BlockSpec((HPG, nb, tq), lambda g, i: (g, 0, 0)),
            pl.BlockSpec((LANES, n_rows), lambda g, i: (0, 0)),
        ],
        out_specs=(pl.BlockSpec((tq, HPG * HEAD_DIM), lambda g, i: (i, g)),
                   pl.BlockSpec((tq, LANES), lambda g, i: (i, g))),
        scratch_shapes=[pltpu.VMEM((n_rows, tq), F32)],
        compiler_params=_params("parallel", "parallel"),
    )(proj, kcc, vcct, band, mt)


def _lane_tile(x, width):
    return jnp.concatenate([x] * (width // x.shape[1]), axis=1)


def _softmax_update(s, v_aug, m_ref, acc_ref, rows):
    m_prev = m_ref[rows, :]
    m_next = jnp.maximum(m_prev, jnp.max(s, axis=1, keepdims=True))
    alpha = jnp.exp2(m_prev - m_next)
    p = jnp.exp2(s - _lane_tile(m_next, s.shape[1]))
    acc_ref[rows, :] = _lane_tile(alpha, acc_ref.shape[1]) * acc_ref[rows, :] + _dot(p.astype(BF16), v_aug)
    m_ref[rows, :] = m_next


def _softmax_once(s, v_aug):
    m = jnp.broadcast_to(jnp.max(s, axis=1, keepdims=True), (s.shape[0], LANES))
    p = jnp.exp2(s - _lane_tile(m, s.shape[1]))
    return _dot(p.astype(BF16), v_aug)


def _sel_win_kernel(q_ref, nm_ref, ks_ref, vs_ref, kw_ref, vw_ref, blk_ref, per_ref, g_ref, oc_ref, o_ref,
                    q4, m_s, a_s, a_w, bias_ref, *, tile):
    i = pl.program_id(1)
    t = tile

    @pl.when(i == 0)
    def _():
        tl = lax.broadcasted_iota(jnp.int32, (t, t), 0)
        kl = lax.broadcasted_iota(jnp.int32, (t, t), 1)
        edge = jnp.where(tl < kl, 0.0, NEG)
        for h in range(HPG):
            bias_ref[h, :, :t] = edge
            for which in range(2):
                rows = jnp.broadcast_to(per_ref[h, which:which + 1, :], (t, 2 * t))
                table = pltpu.roll(rows, 0, 1, stride=1, stride_axis=0)
                bias_ref[h, :, (which + 1) * t:(which + 2) * t] = table[:, :t]
    for h in range(HPG):
        q4[h * t:(h + 1) * t, :HEAD_DIM] = q_ref[:, h * HEAD_DIM:(h + 1) * HEAD_DIM]
        q4[h * t:(h + 1) * t, HEAD_DIM:] = nm_ref[...]
    m_s[...] = jnp.full(m_s.shape, -3e38, F32)
    a_s[...] = jnp.zeros(a_s.shape, F32)

    def sel_step(row0, n_tiles, n_biased):
        keys = pl.ds(row0, n_tiles * t)
        k_aug = jnp.concatenate([ks_ref[keys, :], blk_ref[keys, :]], axis=1)
        v_aug = jnp.concatenate([vs_ref[keys, :], jnp.ones((n_tiles * t, LANES), BF16)], axis=1)
        plain = (n_tiles - n_biased) * t
        for h in range(HPG):
            rows = slice(h * t, (h + 1) * t)
            s = _dot_nt(q4[rows, :], k_aug)
            if n_biased:
                near = s[:, plain:] + bias_ref[h, :, (3 - n_biased) * t:]
                s = jnp.concatenate([s[:, :plain], near], axis=1) if plain else near
            _softmax_update(s, v_aug, m_s, a_s, rows)

    def win_once(row0, n_tiles):
        keys = pl.ds(row0, n_tiles * t)
        v_aug = jnp.concatenate([vw_ref[keys, :], jnp.ones((n_tiles * t, LANES), BF16)], axis=1)
        for h in range(HPG):
            rows = slice(h * t, (h + 1) * t)
            s = _dot_nt(q4[rows, :HEAD_DIM], kw_ref[keys, :]) + bias_ref[h, :, (3 - n_tiles) * t:]
            a_w[rows, :] = _softmax_once(s, v_aug)

    n_far = jnp.maximum(i - 1, 0)

    n_chunks = n_far // FAR_TILES
    chunk = FAR_TILES * t

    def far_pair(c, carry):
        row0 = pl.multiple_of(c * 2 * chunk, 2 * chunk)
        sel_step(row0, FAR_TILES, 0)
        sel_step(row0 + chunk, FAR_TILES, 0)
        return carry

    lax.fori_loop(0, n_chunks // 2, far_pair, 0)

    @pl.when(n_chunks % 2 == 1)
    def _():
        sel_step(pl.multiple_of((n_chunks - 1) * chunk, chunk), FAR_TILES, 0)

    for left in range(FAR_TILES):
        @pl.when(jnp.logical_and(i >= 2, n_far % FAR_TILES == left))
        def _(left=left):
            sel_step(pl.multiple_of((i - 1 - left) * t, t), left + 2, 2)
            win_once(pl.multiple_of((i - 2) * t, t), 3)

    @pl.when(i == 1)
    def _():
        sel_step(0, 2, 2)
        win_once(0, 2)

    @pl.when(i == 0)
    def _():
        sel_step(0, 1, 1)
        win_once(0, 1)

    acc_s = a_s[...]
    acc_w = a_w[...]
    o_s = acc_s[:, :HEAD_DIM] * (1.0 / acc_s[:, HEAD_DIM:])
    o_w = acc_w[:, :HEAD_DIM] * (1.0 / acc_w[:, HEAD_DIM:])
    gates = g_ref[...]

    def merge(group):
        for h in range(HPG):
            cols = slice(h * HEAD_DIM, (h + 1) * HEAD_DIM)
            lane = group * HPG + h
            o = (gates[:, lane:lane + 1] * oc_ref[:, cols]
                 + gates[:, N_HEADS + lane:N_HEADS + lane + 1] * o_s[h * t:(h + 1) * t]
                 + gates[:, 2 * N_HEADS + lane:2 * N_HEADS + lane + 1] * o_w[h * t:(h + 1) * t])
            o_ref[:, cols] = o.astype(o_ref.dtype)

    for group in range(N_KV_GROUPS):
        pl.when(pl.program_id(0) == group)(functools.partial(merge, group))


def _sel_win(proj, neg_mask, blk_onehot, periods, gates, o_c, tile):
    s = proj.shape[0]
    ks_blk = Q_WIDTH // HEAD_DIM
    vs_blk = (Q_WIDTH + KV_WIDTH) // HEAD_DIM
    kw_blk = (Q_WIDTH + 2 * KV_WIDTH) // HEAD_DIM
    vw_blk = (Q_WIDTH + 3 * KV_WIDTH) // HEAD_DIM
    once = pl.Buffered(1)
    rows4 = HPG * tile
    kern = functools.partial(_sel_win_kernel, tile=tile)
    return pl.pallas_call(
        kern,
        out_shape=jax.ShapeDtypeStruct((s, Q_WIDTH), BF16),
        grid=(N_KV_GROUPS, s // tile),
        in_specs=[
            pl.BlockSpec((tile, HPG * HEAD_DIM), lambda g, i: (i, g)),
            pl.BlockSpec((tile, LANES), lambda g, i: (i, g)),
            pl.BlockSpec((s, HEAD_DIM), lambda g, i: (0, ks_blk + g), pipeline_mode=once),
            pl.BlockSpec((s, HEAD_DIM), lambda g, i: (0, vs_blk + g), pipeline_mode=once),
            pl.BlockSpec((s, HEAD_DIM), lambda g, i: (0, kw_blk + g), pipeline_mode=once),
            pl.BlockSpec((s, HEAD_DIM), lambda g, i: (0, vw_blk + g), pipeline_mode=once),
            pl.BlockSpec((s, LANES), lambda g, i: (0, 0), pipeline_mode=once),
            pl.BlockSpec((HPG, 2, 2 * tile), lambda g, i: (g, 0, 0)),
            pl.BlockSpec((tile, LANES), lambda g, i: (i, 0)),
            pl.BlockSpec((tile, HPG * HEAD_DIM), lambda g, i: (i, g)),
        ],
        out_specs=pl.BlockSpec((tile, HPG * HEAD_DIM), lambda g, i: (i, g)),
        scratch_shapes=[
            pltpu.VMEM((rows4, 2 * HEAD_DIM), BF16),
            pltpu.VMEM((rows4, LANES), F32),
            pltpu.VMEM((rows4, 2 * HEAD_DIM), F32),
            pltpu.VMEM((rows4, 2 * HEAD_DIM), F32),
            pltpu.VMEM((HPG, tile, 3 * tile), F32),
        ],
        compiler_params=_params("arbitrary", "arbitrary"),
    )(proj, neg_mask, proj, proj, proj, proj, blk_onehot, periods, gates, o_c)


def _rel_bucket_np(dist):
    n = np.maximum(dist, 0)
    max_exact = REL_BUCKETS // 2
    large = max_exact + (np.log(np.maximum(n, 1).astype(np.float32) / np.float32(max_exact))
                         / np.float32(math.log(REL_MAX_DIST / max_exact))
                         * np.float32(REL_BUCKETS - max_exact)).astype(np.int32)
    large = np.minimum(large, REL_BUCKETS - 1)
    return np.where(n < max_exact, n, large).astype(np.int32)


def _far_distance():
    d = np.arange(4 * REL_MAX_DIST)
    b = _rel_bucket_np(d)
    assert b[-1] == REL_BUCKETS - 1
    return int(np.max(np.nonzero(b != REL_BUCKETS - 1)[0])) + 1


def _bias_table(rel_bias, dist):
    bucket = jnp.asarray(_rel_bucket_np(dist))
    shifted = (rel_bias[bucket] - rel_bias[REL_BUCKETS - 1]) * LOG2E
    tab = jnp.where(jnp.asarray(dist >= 0)[..., None], shifted, NEG)
    return jnp.moveaxis(tab, -1, 0).astype(F32)


def _bias_period(rel_bias, offset, t):
    x = np.arange(2 * t)
    return _bias_table(rel_bias, np.where(x < t, offset - x, offset + 2 * t - x))


def _nsa_mixer(h, hn, rel_bias, w_in, w_out, layer, pos_k, w1_k, w2_k, pos_v, w1_v, w2_v):
    s = h.shape[0]
    n_sel = s // SEL_BLOCK
    ncp = s // CMP_STRIDE
    tile = ATT_TILE
    far = _far_distance()
    assert n_sel <= LANES and s % CMP_TQ == 0 and s % tile == 0
    assert far <= tile and far <= CMP_STRIDE * CMP_PAD - CMP_BLOCK + 1 + CMP_STRIDE

    col_scale = jnp.where(jnp.arange(PROJ_ATT) < Q_WIDTH, HEAD_DIM ** -0.5 * LOG2E, 1.0).astype(F32)[None, :]
    w_in_t = jnp.swapaxes(w_in, 1, 2)
    tn = KV_WIDTH
    q_blocks, cmp_blocks = Q_WIDTH // tn, 2 * KV_WIDTH // tn
    proj = _matmul(_ep_scale, hn, w_in_t, layer, PROJ_ATT, BF16, 1024, tn, row_extra=col_scale,
                   w_rows_are_outputs=True, src_block=lambda j: jnp.where(j < q_blocks, j, j + cmp_blocks))
    kcvc = _matmul(lambda z: z, hn, w_in_t, layer, 2 * KV_WIDTH, F32, 1024, tn,
                   w_rows_are_outputs=True, src_block=lambda j: j + q_blocks)
    w_gate = jnp.pad(w_in_t[layer, PROJ_MAIN:], ((0, LANES - N_GATES * N_HEADS), (0, 0)))[None]
    gates = _matmul(jax.nn.sigmoid, hn, w_gate, 0, LANES, F32, 1024, LANES, w_rows_are_outputs=True)

    w1 = jnp.stack([w1_k, w1_v]).astype(BF16)
    w2 = jnp.stack([w2_k, w2_v]).astype(BF16)
    pos = jnp.stack([pos_k.reshape(1, -1), pos_v.reshape(1, -1)]).astype(BF16)
    pos = jnp.broadcast_to(pos, (2, SUBLANES, CMP_BLOCK * HEAD_DIM))
    cc = _compress(kcvc, w1, w2, pos)
    kcc = jnp.pad(cc[0], ((0, 0), (CMP_PAD, 0), (0, 0)))
    vcct = jnp.pad(cc[1], ((0, 0), (CMP_PAD, 0), (0, 0))).transpose(0, 2, 1)
    n_rows = ncp + CMP_PAD

    nb = CMP_TQ // CMP_STRIDE + CMP_PAD
    first_dist = -CMP_STRIDE * (nb - 1 - CMP_PAD) - (CMP_BLOCK - 1)
    by_dist = _bias_table(rel_bias, first_dist + np.arange(CMP_TQ + CMP_STRIDE * (nb - 1)))
    band = jnp.stack([by_dist[:, CMP_STRIDE * (nb - 1 - b):CMP_STRIDE * (nb - 1 - b) + CMP_TQ] for b in range(nb)], axis=1)
    ratio = SEL_BLOCK // CMP_STRIDE
    lo = CMP_BLOCK // CMP_STRIDE - 1
    c_of_row = np.arange(n_rows)[None, :] - CMP_PAD
    j_of = np.arange(LANES)[:, None]
    mt = ((c_of_row >= ratio * j_of - lo) & (c_of_row <= ratio * j_of + ratio - 1)
          & (c_of_row >= 0) & (c_of_row <= ncp - 2) & (j_of < n_sel)).astype(np.float32)
    o_c, neg_mask = _cmp_attn(proj, kcc, vcct, band, jnp.asarray(mt, BF16), CMP_TQ)

    onehot = (np.arange(s)[:, None] // SEL_BLOCK == np.arange(LANES)[None, :]).astype(np.float32)
    periods = jnp.stack([_bias_period(rel_bias, tile, tile), _bias_period(rel_bias, 0, tile)], axis=1)
    o = _sel_win(proj, neg_mask, jnp.asarray(onehot, BF16), periods, gates, o_c, tile)
    return _matmul(_ep_resid, o, w_out, layer, D_MODEL, F32, 512, 1024, tile_extra=h)


def _s5_kernel(x_ref, lam_ref, bt_ref, c_ref, d_ref, y_ref, vt, yt, ys, tmask, *, chunk):
    L = chunk
    half = L // 2
    width = L * SSM_GROUP
    n_chunks = x_ref.shape[0] // L
    for tau in range(L):
        vt[tau] = x_ref[pl.ds(tau, n_chunks, stride=L), :].T

    @pl.when(pl.program_id(0) == 0)
    def _():
        dst = lax.broadcasted_iota(jnp.int32, (width, width), 0) // SSM_GROUP
        src = lax.broadcasted_iota(jnp.int32, (width, width), 1) // SSM_GROUP
        tmask[...] = jnp.where(src <= dst, 1.0, 0.0)

    def cmul(xr, xi, yr, yi):
        return xr * yr - xi * yi, xr * yi + xi * yr

    n2 = 2 * SSM_STATE
    sub = lax.broadcasted_iota(jnp.int32, (SUBLANES, 1), 0)
    consts = jnp.where(sub == 0, 1.0, jnp.where(sub == 1, float(half + 1), jnp.where(sub == 2, float(half - 1), float(L))))
    tau_col = lax.broadcasted_iota(jnp.int32, (L, 1), 0).astype(F32)
    row = lax.broadcasted_iota(jnp.int32, (n_chunks, n2), 0)
    conj = jnp.where(lax.broadcasted_iota(jnp.int32, (1, n2), 1) < SSM_STATE, 1.0, -1.0)

    def group(g, carry):
        lam = lam_ref[g]
        a_re, a_im = lam[0:1], lam[1:2]
        dt = jnp.exp(lam[2:3])
        log_re, log_im = a_re * dt, a_im * dt

        def cpow(e):
            mag, cos, sin = jnp.exp(e * log_re), jnp.cos(e * log_im), jnp.sin(e * log_im)
            inv = 1.0 / mag
            return mag * cos, mag * sin, inv * cos, -(inv * sin)

        kr, ki, _, _ = cpow(consts)
        pr, pi = kr[0:1] - 1.0, ki[0:1]
        den = a_re * a_re + a_im * a_im
        cf_r = (pr * a_re + pi * a_im) / den
        cf_i = (pi * a_re - pr * a_im) / den
        bt_c, bt_s = bt_ref[g, 0], bt_ref[g, 1]
        bb_c = cf_r * bt_c + cf_i * bt_s
        bb_s = cf_r * bt_s - cf_i * bt_c
        c_c, c_s = c_ref[g, 0], c_ref[g, 1]

        e1r, e1i, e2r, e2i = cpow(tau_col - half)
        e3r, e3i = cmul(e1r, e1i, kr[1:2], ki[1:2])
        e4r, e4i = cmul(e2r, e2i, kr[2:3], ki[2:3])

        def outer(er, ei, w_c, w_s):
            return (er[:, None, :] * w_c[None, :, :] + ei[:, None, :] * w_s[None, :, :]).reshape(width, n2)

        qm = outer(e1r, e1i, c_c, c_s)
        km_conj = outer(e2r, e2i, bb_c * conj, bb_s * conj)
        qc = outer(e3r, e3i, c_c, c_s)
        wz = outer(e4r, e4i, bb_c, bb_s)

        chans = pl.ds(pl.multiple_of(g * SSM_GROUP, SSM_GROUP), SSM_GROUP)
        ub = vt[:, chans, :].reshape(width, n_chunks).astype(BF16)

        tt = _dot_nt(qm.astype(BF16), km_conj.astype(BF16))
        y = _dot((tt * tmask[...]).astype(BF16), ub)

        x = _dot(wz.T.astype(BF16), ub).T
        p_r, p_i = kr[3:4], ki[3:4]
        d = 1
        while d < n_chunks:
            s = jnp.where(row >= d, pltpu.roll(x, d, 0), 0.0)
            x = x + s * p_r - pltpu.roll(s, SSM_STATE, 1) * (p_i * conj)
            p_r, p_i = cmul(p_r, p_i, p_r, p_i)
            d *= 2
        prev_conj = jnp.where(row >= 1, pltpu.roll(x, 1, 0), 0.0) * conj
        y = y + _dot_nt(qc.astype(BF16), prev_conj.astype(BF16))
        yt[:, chans, :] = y.reshape(L, SSM_GROUP, n_chunks)

    def group_batch(k, carry):
        for j in range(S5_GROUPS_PER_TRIP):
            group(S5_GROUPS_PER_TRIP * k + j, carry)
        return carry

    lax.fori_loop(0, LANES // SSM_GROUP // S5_GROUPS_PER_TRIP, group_batch, 0)
    for tau in range(L):
        ys[pl.ds(tau, n_chunks, stride=L), :] = yt[tau].T
    y_ref[...] = jax.nn.gelu(ys[...] + d_ref[...] * x_ref[...]).astype(y_ref.dtype)


def _s5_mixer(h, hn, a_re, a_im, log_dt, b_re, b_im, c_re, c_im, d_skip, w_glu, layer):
    s = h.shape[0]
    L = S5_CHUNK
    n_chunks = s // L
    width = L * SSM_GROUP
    gpb = LANES // SSM_GROUP
    def packed(re, im):
        return jnp.stack([jnp.concatenate([re, im], -1), jnp.concatenate([-im, re], -1)], axis=1).astype(F32)

    lam = jnp.stack([a_re, a_im, jnp.broadcast_to(log_dt[:, None], a_re.shape)], axis=1).astype(F32)
    lam = jnp.concatenate([lam, lam], axis=-1)
    bt = packed(b_re.transpose(0, 2, 1), b_im.transpose(0, 2, 1))
    cc = packed(c_re, c_im)
    dd = d_skip.astype(F32).reshape(1, D_MODEL)
    y = pl.pallas_call(
        functools.partial(_s5_kernel, chunk=L),
        out_shape=jax.ShapeDtypeStruct((s, D_MODEL), BF16),
        grid=(SSM_GROUPS // gpb,),
        in_specs=[
            pl.BlockSpec((s, LANES), lambda b: (0, b)),
            pl.BlockSpec((gpb, 3, 2 * SSM_STATE), lambda b: (b, 0, 0)),
            pl.BlockSpec((gpb, 2, SSM_GROUP, 2 * SSM_STATE), lambda b: (b, 0, 0, 0)),
            pl.BlockSpec((gpb, 2, SSM_GROUP, 2 * SSM_STATE), lambda b: (b, 0, 0, 0)),
            pl.BlockSpec((1, LANES), lambda b: (0, b)),
        ],
        out_specs=pl.BlockSpec((s, LANES), lambda b: (0, b)),
        scratch_shapes=[
            pltpu.VMEM((L, LANES, n_chunks), F32),
            pltpu.VMEM((L, LANES, n_chunks), F32),
            pltpu.VMEM((s, LANES), F32),
            pltpu.VMEM((width, width), F32),
        ],
        compiler_params=_params("arbitrary"),
    )(hn, lam, bt, cc, dd)
    return _matmul(_ep_glu_resid, y, w_glu, layer, D_MODEL, F32, 1024, 512, b_halves=2, tile_extra=h)


def kernel(x, rel_bias, mix_norm_g, ffn_norm_g, final_norm_g, nsa_w_in, nsa_w_out, cmp_pos_k, cmp_w1_k, cmp_w2_k, cmp_pos_v, cmp_w1_v, cmp_w2_v, s5_A_re, s5_A_im, s5_log_dt, s5_B_re, s5_B_im, s5_C_re, s5_C_im, s5_D, s5_w_glu, ffn_w_in, ffn_w_out):
    assert x.shape[0] == 1
    h = x[0]
    hn = _rmsnorm(h, mix_norm_g[0], BF16)
    h = _nsa_mixer(h, hn, rel_bias, nsa_w_in, nsa_w_out, 0, cmp_pos_k[0], cmp_w1_k[0], cmp_w2_k[0],
                   cmp_pos_v[0], cmp_w1_v[0], cmp_w2_v[0])
    h = _swiglu_ffn(h, ffn_norm_g[0], ffn_w_in, ffn_w_out, 0)
    hn = _rmsnorm(h, mix_norm_g[1], F32)
    h = _s5_mixer(h, hn, s5_A_re[0], s5_A_im[0], s5_log_dt[0], s5_B_re[0], s5_B_im[0], s5_C_re[0], s5_C_im[0],
                  s5_D[0], s5_w_glu, 0)
    h = _swiglu_ffn(h, ffn_norm_g[1], ffn_w_in, ffn_w_out, 1)
    return _rmsnorm(h, final_norm_g, x.dtype)[None]
```

```python
import functools
import math

import numpy as np
import jax
import jax.numpy as jnp
from jax import lax
from jax.experimental import pallas as pl
from jax.experimental.pallas import tpu as pltpu

D_MODEL = 2048
N_HEADS = 16
HEAD_DIM = 128
N_KV_GROUPS = 4
HPG = N_HEADS // N_KV_GROUPS
CMP_BLOCK = 32
CMP_STRIDE = 16
CMP_HIDDEN = 2 * HEAD_DIM
SEL_BLOCK = 64
SEL_TOPK = 16
N_FORCED = 3
WINDOW = 512
N_GATES = 3
KV_WIDTH = N_KV_GROUPS * HEAD_DIM
Q_WIDTH = N_HEADS * HEAD_DIM
PROJ_MAIN = Q_WIDTH + 6 * KV_WIDTH
PROJ_ATT = Q_WIDTH + 4 * KV_WIDTH
REL_BUCKETS = 32
REL_MAX_DIST = 128
SSM_GROUP = 16
SSM_GROUPS = D_MODEL // SSM_GROUP
SSM_STATE = 64
D_FF = ((8 * D_MODEL + 2) // 3 + 255) // 256 * 256
RMS_EPS = 1e-6
NEG = -1e30
FORCE_SCORE = 1e9
LOG2E = math.log2(math.e)

LANES = 128
SUBLANES = 8
VMEM_LIMIT = 52 * 1024 * 1024

ATT_TILE = WINDOW // 2
FAR_TILES = 4
CMP_TQ = 256
CMP_PAD = 8
S5_CHUNK = 32
S5_GROUPS_PER_TRIP = 4

BF16 = jnp.bfloat16
F32 = jnp.float32


def _dot(a, b):
    return jnp.dot(a, b, preferred_element_type=F32)


def _dot_nt(a, b):
    return lax.dot_general(a, b, (((1,), (1,)), ((), ())), preferred_element_type=F32)


def _params(*sem):
    return pltpu.CompilerParams(dimension_semantics=sem, vmem_limit_bytes=VMEM_LIMIT)


def _rmsnorm_kernel(x_ref, g_ref, o_ref):
    x = x_ref[...]
    ms = jnp.mean(x * x, axis=-1, keepdims=True)
    o_ref[...] = (x * lax.rsqrt(ms + RMS_EPS) * g_ref[...]).astype(o_ref.dtype)


def _rmsnorm(x, g, out_dtype, tm=512):
    s, d = x.shape
    return pl.pallas_call(
        _rmsnorm_kernel,
        out_shape=jax.ShapeDtypeStruct((s, d), out_dtype),
        grid=(s // tm,),
        in_specs=[pl.BlockSpec((tm, d), lambda i: (i, 0)), pl.BlockSpec((1, d), lambda i: (0, 0))],
        out_specs=pl.BlockSpec((tm, d), lambda i: (i, 0)),
        compiler_params=_params("parallel"),
    )(x, g.reshape(1, d))


def _mm_kernel(*refs, n_b, n_extra, epilogue, w_rows_are_outputs):
    a_ref = refs[0]
    b_refs = refs[1:1 + n_b]
    extra_refs = refs[1 + n_b:1 + n_b + n_extra]
    o_ref = refs[1 + n_b + n_extra]
    w_scr = refs[2 + n_b + n_extra:]

    @pl.when(pl.program_id(1) == 0)
    def _():
        for b_ref, w in zip(b_refs, w_scr):
            w[...] = b_ref[...].astype(BF16)

    a = a_ref[...]
    z = [(_dot_nt if w_rows_are_outputs else _dot)(a, w[...]) for w in w_scr]
    o_ref[...] = epilogue(*z, *[e[...] for e in extra_refs]).astype(o_ref.dtype)


def _ep_scale(z, scale):
    return z * scale


def _ep_resid(z, resid):
    return resid + z


def _ep_swiglu(za, zb):
    return jax.nn.silu(za) * zb


def _ep_glu_resid(za, zb, resid):
    return resid + za * jax.nn.sigmoid(zb)


def _matmul(epilogue, a, b, layer, n_out, out_dtype, tm, tn, *, b_halves=1, row_extra=None, tile_extra=None,
            w_rows_are_outputs=False, src_block=lambda j: j):
    m, k = a.shape
    nj = n_out // tn
    b_mode = pl.Buffered(1) if 2 * b_halves * k * tn * 4 > VMEM_LIMIT // 3 else None
    in_specs = [pl.BlockSpec((tm, k), lambda j, i: (i, 0))]
    args = [a]
    for half in range(b_halves):
        if w_rows_are_outputs:
            spec = pl.BlockSpec((None, tn, k), functools.partial(lambda j, i, o: (layer, src_block(j) + o, 0), o=half * nj),
                                pipeline_mode=b_mode)
        else:
            spec = pl.BlockSpec((None, k, tn), functools.partial(lambda j, i, o: (layer, 0, src_block(j) + o), o=half * nj),
                                pipeline_mode=b_mode)
        in_specs.append(spec)
        args.append(b)
    extras = []
    if row_extra is not None:
        in_specs.append(pl.BlockSpec((1, tn), lambda j, i: (0, j)))
        extras.append(row_extra)
    if tile_extra is not None:
        in_specs.append(pl.BlockSpec((tm, tn), lambda j, i: (i, j)))
        extras.append(tile_extra)
    kern = functools.partial(_mm_kernel, n_b=b_halves, n_extra=len(extras), epilogue=epilogue,
                             w_rows_are_outputs=w_rows_are_outputs)
    return pl.pallas_call(
        kern,
        out_shape=jax.ShapeDtypeStruct((m, n_out), out_dtype),
        grid=(nj, m // tm),
        in_specs=in_specs,
        out_specs=pl.BlockSpec((tm, tn), lambda j, i: (i, j)),
        scratch_shapes=[pltpu.VMEM((tn, k) if w_rows_are_outputs else (k, tn), BF16) for _ in range(b_halves)],
        compiler_params=_params("arbitrary", "arbitrary"),
    )(*args, *extras)


def _swiglu_ffn(h, g, w_in, w_out, layer):
    hn = _rmsnorm(h, g, BF16)
    act = _matmul(_ep_swiglu, hn, w_in, layer, D_FF, BF16, 1024, 512, b_halves=2)
    return _matmul(_ep_resid, act, w_out, layer, D_MODEL, F32, 512, 512, tile_extra=h)


def _compress_kernel(x_ref, w1_ref, w2_ref, pos_ref, o_ref):
    ncp = x_ref.shape[0] // CMP_STRIDE
    top = jnp.zeros((ncp, CMP_HIDDEN), F32)
    bot = jnp.zeros((ncp, CMP_HIDDEN), F32)
    for r in range(CMP_STRIDE):
        x_r = x_ref[pl.ds(r, ncp, stride=CMP_STRIDE), :].astype(BF16)
        top = top + _dot(x_r, w1_ref[0, r * HEAD_DIM:(r + 1) * HEAD_DIM, :])
        bot = bot + _dot(x_r, w1_ref[0, (CMP_STRIDE + r) * HEAD_DIM:(CMP_STRIDE + r + 1) * HEAD_DIM, :])
    posb = _dot(pos_ref[0], w1_ref[0])[0:1]
    hid = top + pltpu.roll(bot, ncp - 1, 0) + posb
    o_ref[0, 0] = _dot(jax.nn.gelu(hid).astype(BF16), w2_ref[0]).astype(o_ref.dtype)


def _compress(kcvc, w1, w2, pos):
    s = kcvc.shape[0]
    ncp = s // CMP_STRIDE
    return pl.pallas_call(
        _compress_kernel,
        out_shape=jax.ShapeDtypeStruct((2, N_KV_GROUPS, ncp, HEAD_DIM), BF16),
        grid=(2, N_KV_GROUPS),
        in_specs=[
            pl.BlockSpec((s, HEAD_DIM), lambda kv, gi: (0, kv * N_KV_GROUPS + gi)),
            pl.BlockSpec((1, CMP_BLOCK * HEAD_DIM, CMP_HIDDEN), lambda kv, gi: (kv, 0, 0)),
            pl.BlockSpec((1, CMP_HIDDEN, HEAD_DIM), lambda kv, gi: (kv, 0, 0)),
            pl.BlockSpec((1, SUBLANES, CMP_BLOCK * HEAD_DIM), lambda kv, gi: (kv, 0, 0)),
        ],
        out_specs=pl.BlockSpec((1, 1, ncp, HEAD_DIM), lambda kv, gi: (kv, gi, 0, 0)),
        compiler_params=_params("parallel", "parallel"),
    )(kcvc, w1, w2, pos)


def _cmp_attn_kernel(q_ref, kcc_ref, vcct_ref, band_ref, mt_ref, oc_ref, nm_ref, s_scr, *, tq, nb):
    i = pl.program_id(1)
    band_start = pl.multiple_of(i * (tq // CMP_STRIDE), SUBLANES)
    n_rows = kcc_ref.shape[1]
    n_blk = mt_ref.shape[0]

    def body(n_r, n_b):
        kcc = kcc_ref[0, :n_r, :]
        vcct = vcct_ref[0, :, :n_r]
        rows = lax.broadcasted_iota(jnp.int32, (n_r, tq), 0)
        row_bias = jnp.where(rows >= CMP_PAD, jnp.where(rows < band_start + nb, 0.0, NEG), NEG)
        imp = jnp.zeros((n_r, tq), F32)
        for h in range(HPG):
            qh = q_ref[:, h * HEAD_DIM:(h + 1) * HEAD_DIM]
            s_scr[:n_r, :] = _dot_nt(kcc, qh) + row_bias
            s_scr[pl.ds(band_start, nb), :] += band_ref[h]
            s = s_scr[:n_r, :]
            m = jnp.maximum(jnp.max(s, axis=0, keepdims=True), 0.5 * NEG)
            e = jnp.exp2(s - m)
            l = jnp.sum(e, axis=0, keepdims=True)
            pn = e * (1.0 / jnp.where(l > 0.0, l, 1.0))
            imp = imp + pn
            oct_h = _dot(vcct, pn.astype(BF16))
            oc_ref[:, h * HEAD_DIM:(h + 1) * HEAD_DIM] = oct_h.T
        mt = mt_ref[:n_b, :n_r]
        hi = imp.astype(BF16)
        rest = imp - hi.astype(F32)
        mid = rest.astype(BF16)
        lo = (rest - mid.astype(F32)).astype(BF16)
        slc = _dot(mt, hi) + _dot(mt, mid) + _dot(mt, lo)
        jj = lax.broadcasted_iota(jnp.int32, (n_b, tq), 0)
        t_blk = (i * tq + lax.broadcasted_iota(jnp.int32, (n_b, tq), 1)) // SEL_BLOCK
        forced = jnp.where(jj == 0, 1, jnp.where(jj == t_blk, 1, jnp.where(jj == t_blk - 1, 1, 0)))
        picked = jnp.where(forced == 1, 1.0, 0.0)
        score = jnp.where(forced == 1, -jnp.inf, jnp.where(jj <= t_blk, slc, NEG))
        for _ in range(SEL_TOPK - N_FORCED):
            best = jnp.max(score, axis=0, keepdims=True)
            first = jnp.min(jnp.where(score == best, jj, n_blk), axis=0, keepdims=True)
            hit = jj == first
            picked = jnp.where(hit, 1.0, picked)
            score = jnp.where(hit, -jnp.inf, score)
        neg_mask = jnp.where(picked > 0.5, jnp.where(jj <= t_blk, 0.0, NEG), NEG)
        if n_b < n_blk:
            neg_mask = jnp.concatenate([neg_mask, jnp.full((n_blk - n_b, tq), NEG, F32)], axis=0)
        nm_ref[...] = neg_mask.T.astype(nm_ref.dtype)

    step = LANES
    extents = list(range(step, n_rows, step)) + [n_rows]
    need_rows = band_start + nb
    for v, n_r in enumerate(extents):
        lower = extents[v - 1] if v else 0
        n_b = min(n_blk, -(-(n_r * CMP_STRIDE // SEL_BLOCK) // SUBLANES) * SUBLANES)
        pl.when(jnp.logical_and(need_rows > lower, need_rows <= n_r))(functools.partial(body, n_r, n_b))


def _cmp_attn(proj, kcc, vcct, band, mt, tq):
    s = proj.shape[0]
    n_rows = kcc.shape[1]
    nb = band.shape[1]
    kern = functools.partial(_cmp_attn_kernel, tq=tq, nb=nb)
    return pl.pallas_call(
        kern,
        out_shape=(jax.ShapeDtypeStruct((s, Q_WIDTH), F32),
                   jax.ShapeDtypeStruct((s, N_KV_GROUPS * LANES), BF16)),
        grid=(N_KV_GROUPS, s // tq),
        in_specs=[
            pl.BlockSpec((tq, HPG * HEAD_DIM), lambda g, i: (i, g)),
            pl.BlockSpec((1, n_rows, HEAD_DIM), lambda g, i: (g, 0, 0)),
            pl.BlockSpec((1, HEAD_DIM, n_rows), lambda g, i: (g, 0, 0)),
            pl.BlockSpec((HPG, nb, tq), lambda g, i: (g, 0, 0)),
            pl.BlockSpec((LANES, n_rows), lambda g, i: (0, 0)),
        ],
        out_specs=(pl.BlockSpec((tq, HPG * HEAD_DIM), lambda g, i: (i, g)),
                   pl.BlockSpec((tq, LANES), lambda g, i: (i, g))),
        scratch_shapes=[pltpu.VMEM((n_rows, tq), F32)],
        compiler_params=_params("parallel", "parallel"),
    )(proj, kcc, vcct, band, mt)


def _lane_tile(x, width):
    return jnp.concatenate([x] * (width // x.shape[1]), axis=1)


def _softmax_update(s, v_aug, m_ref, acc_ref, rows):
    m_prev = m_ref[rows, :]
    m_next = jnp.maximum(m_prev, jnp.max(s, axis=1, keepdims=True))
    alpha = jnp.exp2(m_prev - m_next)
    p = jnp.exp2(s - _lane_tile(m_next, s.shape[1]))
    acc_ref[rows, :] = _lane_tile(alpha, acc_ref.shape[1]) * acc_ref[rows, :] + _dot(p.astype(BF16), v_aug)
    m_ref[rows, :] = m_next


def _softmax_once(s, v_aug):
    m = jnp.broadcast_to(jnp.max(s, axis=1, keepdims=True), (s.shape[0], LANES))
    p = jnp.exp2(s - _lane_tile(m, s.shape[1]))
    return _dot(p.astype(BF16), v_aug)


def _sel_win_kernel(q_ref, nm_ref, ks_ref, vs_ref, kw_ref, vw_ref, blk_ref, per_ref, g_ref, oc_ref, o_ref,
                    q4, m_s, a_s, a_w, bias_ref, *, tile):
    i = pl.program_id(1)
    t = tile

    @pl.when(i == 0)
    def _():
        tl = lax.broadcasted_iota(jnp.int32, (t, t), 0)
        kl = lax.broadcasted_iota(jnp.int32, (t, t), 1)
        edge = jnp.where(tl < kl, 0.0, NEG)
        for h in range(HPG):
            bias_ref[h, :, :t] = edge
            for which in range(2):
                rows = jnp.broadcast_to(per_ref[h, which:which + 1, :], (t, 2 * t))
                table = pltpu.roll(rows, 0, 1, stride=1, stride_axis=0)
                bias_ref[h, :, (which + 1) * t:(which + 2) * t] = table[:, :t]
    for h in range(HPG):
        q4[h * t:(h + 1) * t, :HEAD_DIM] = q_ref[:, h * HEAD_DIM:(h + 1) * HEAD_DIM]
        q4[h * t:(h + 1) * t, HEAD_DIM:] = nm_ref[...]
    m_s[...] = jnp.full(m_s.shape, -3e38, F32)
    a_s[...] = jnp.zeros(a_s.shape, F32)

    def sel_operands(row0, n_tiles):
        keys = pl.ds(row0, n_tiles * t)
        k_aug = jnp.concatenate([ks_ref[keys, :], blk_ref[keys, :]], axis=1)
        v_aug = jnp.concatenate([vs_ref[keys, :], jnp.ones((n_tiles * t, LANES), BF16)], axis=1)
        return k_aug, v_aug

    def sel_scores(h, k_aug, n_biased):
        s = _dot_nt(q4[h * t:(h + 1) * t, :], k_aug)
        plain = s.shape[1] - n_biased * t
        if n_biased:
            near = s[:, plain:] + bias_ref[h, :, (3 - n_biased) * t:]
            s = jnp.concatenate([s[:, :plain], near], axis=1) if plain else near
        return s

    def sel_pass(row0, n_tiles, n_biased):
        k_aug, v_aug = sel_operands(row0, n_tiles)

        def consume(h, s):
            _softmax_update(s, v_aug, m_s, a_s, slice(h * t, (h + 1) * t))

        return functools.partial(sel_scores, k_aug=k_aug, n_biased=n_biased), consume

    def win_pass(row0, n_tiles):
        keys = pl.ds(row0, n_tiles * t)
        v_aug = jnp.concatenate([vw_ref[keys, :], jnp.ones((n_tiles * t, LANES), BF16)], axis=1)

        def score(h):
            return _dot_nt(q4[h * t:(h + 1) * t, :HEAD_DIM], kw_ref[keys, :]) + bias_ref[h, :, (3 - n_tiles) * t:]

        def consume(h, s):
            a_w[h * t:(h + 1) * t, :] = _softmax_once(s, v_aug)

        return score, consume

    def run(passes):
        work = [(score, consume, h0) for score, consume in passes for h0 in range(0, HPG, 2)]
        ahead = [work[0][0](h) for h in (0, 1)]
        for n, (_, consume, h0) in enumerate(work):
            now = ahead
            if n + 1 < len(work):
                nxt_score, _, nxt_h0 = work[n + 1]
                ahead = [nxt_score(h) for h in (nxt_h0, nxt_h0 + 1)]
            for h, s in zip((h0, h0 + 1), now):
                consume(h, s)

    n_far = jnp.maximum(i - 1, 0)

    n_chunks = n_far // FAR_TILES
    chunk = FAR_TILES * t

    def far_pair(c, carry):
        row0 = pl.multiple_of(c * 2 * chunk, 2 * chunk)
        run([sel_pass(row0, FAR_TILES, 0), sel_pass(row0 + chunk, FAR_TILES, 0)])
        return carry

    lax.fori_loop(0, n_chunks // 2, far_pair, 0)

    @pl.when(n_chunks % 2 == 1)
    def _():
        run([sel_pass(pl.multiple_of((n_chunks - 1) * chunk, chunk), FAR_TILES, 0)])

    for left in range(FAR_TILES):
        @pl.when(jnp.logical_and(i >= 2, n_far % FAR_TILES == left))
        def _(left=left):
            run([sel_pass(pl.multiple_of((i - 1 - left) * t, t), left + 2, 2),
                 win_pass(pl.multiple_of((i - 2) * t, t), 3)])

    @pl.when(i == 1)
    def _():
        run([sel_pass(0, 2, 2), win_pass(0, 2)])

    @pl.when(i == 0)
    def _():
        run([sel_pass(0, 1, 1), win_pass(0, 1)])

    acc_s = a_s[...]
    acc_w = a_w[...]
    o_s = acc_s[:, :HEAD_DIM] * (1.0 / acc_s[:, HEAD_DIM:])
    o_w = acc_w[:, :HEAD_DIM] * (1.0 / acc_w[:, HEAD_DIM:])
    gates = g_ref[...]
    for h in range(HPG):
        cols = slice(h * HEAD_DIM, (h + 1) * HEAD_DIM)
        o = (gates[:, h:h + 1] * oc_ref[:, cols]
             + gates[:, HPG + h:HPG + h + 1] * o_s[h * t:(h + 1) * t]
             + gates[:, 2 * HPG + h:2 * HPG + h + 1] * o_w[h * t:(h + 1) * t])
        o_ref[:, cols] = o.astype(o_ref.dtype)


def _sel_win(proj, neg_mask, blk_onehot, periods, gates, o_c, tile):
    s = proj.shape[0]
    ks_blk = Q_WIDTH // HEAD_DIM
    vs_blk = (Q_WIDTH + KV_WIDTH) // HEAD_DIM
    kw_blk = (Q_WIDTH + 2 * KV_WIDTH) // HEAD_DIM
    vw_blk = (Q_WIDTH + 3 * KV_WIDTH) // HEAD_DIM
    once = pl.Buffered(1)
    rows4 = HPG * tile
    kern = functools.partial(_sel_win_kernel, tile=tile)
    return pl.pallas_call(
        kern,
        out_shape=jax.ShapeDtypeStruct((s, Q_WIDTH), BF16),
        grid=(N_KV_GROUPS, s // tile),
        in_specs=[
            pl.BlockSpec((tile, HPG * HEAD_DIM), lambda g, i: (i, g)),
            pl.BlockSpec((tile, LANES), lambda g, i: (i, g)),
            pl.BlockSpec((s, HEAD_DIM), lambda g, i: (0, ks_blk + g), pipeline_mode=once),
            pl.BlockSpec((s, HEAD_DIM), lambda g, i: (0, vs_blk + g), pipeline_mode=once),
            pl.BlockSpec((s, HEAD_DIM), lambda g, i: (0, kw_blk + g), pipeline_mode=once),
            pl.BlockSpec((s, HEAD_DIM), lambda g, i: (0, vw_blk + g), pipeline_mode=once),
            pl.BlockSpec((s, LANES), lambda g, i: (0, 0), pipeline_mode=once),
            pl.BlockSpec((HPG, 2, 2 * tile), lambda g, i: (g, 0, 0)),
            pl.BlockSpec((tile, LANES), lambda g, i: (i, g)),
            pl.BlockSpec((tile, HPG * HEAD_DIM), lambda g, i: (i, g)),
        ],
        out_specs=pl.BlockSpec((tile, HPG * HEAD_DIM), lambda g, i: (i, g)),
        scratch_shapes=[
            pltpu.VMEM((rows4, 2 * HEAD_DIM), BF16),
            pltpu.VMEM((rows4, LANES), F32),
            pltpu.VMEM((rows4, 2 * HEAD_DIM), F32),
            pltpu.VMEM((rows4, 2 * HEAD_DIM), F32),
            pltpu.VMEM((HPG, tile, 3 * tile), F32),
        ],
        compiler_params=_params("arbitrary", "arbitrary"),
    )(proj, neg_mask, proj, proj, proj, proj, blk_onehot, periods, gates, o_c)


def _rel_bucket_np(dist):
    n = np.maximum(dist, 0)
    max_exact = REL_BUCKETS // 2
    large = max_exact + (np.log(np.maximum(n, 1).astype(np.float32) / np.float32(max_exact))
                         / np.float32(math.log(REL_MAX_DIST / max_exact))
                         * np.float32(REL_BUCKETS - max_exact)).astype(np.int32)
    large = np.minimum(large, REL_BUCKETS - 1)
    return np.where(n < max_exact, n, large).astype(np.int32)


def _far_distance():
    d = np.arange(4 * REL_MAX_DIST)
    b = _rel_bucket_np(d)
    assert b[-1] == REL_BUCKETS - 1
    return int(np.max(np.nonzero(b != REL_BUCKETS - 1)[0])) + 1


def _bias_table(rel_bias, dist):
    bucket = jnp.asarray(_rel_bucket_np(dist))
    shifted = (rel_bias[bucket] - rel_bias[REL_BUCKETS - 1]) * LOG2E
    tab = jnp.where(jnp.asarray(dist >= 0)[..., None], shifted, NEG)
    return jnp.moveaxis(tab, -1, 0).astype(F32)


def _bias_period(rel_bias, offset, t):
    x = np.arange(2 * t)
    return _bias_table(rel_bias, np.where(x < t, offset - x, offset + 2 * t - x))


def _nsa_mixer(h, hn, rel_bias, w_in, w_out, layer, pos_k, w1_k, w2_k, pos_v, w1_v, w2_v):
    s = h.shape[0]
    n_sel = s // SEL_BLOCK
    ncp = s // CMP_STRIDE
    tile = ATT_TILE
    far = _far_distance()
    assert n_sel <= LANES and s % CMP_TQ == 0 and s % tile == 0
    assert far <= tile and far <= CMP_STRIDE * CMP_PAD - CMP_BLOCK + 1 + CMP_STRIDE

    col_scale = jnp.where(jnp.arange(PROJ_ATT) < Q_WIDTH, HEAD_DIM ** -0.5 * LOG2E, 1.0).astype(F32)[None, :]
    w_in_t = jnp.swapaxes(w_in, 1, 2)
    tn = KV_WIDTH
    q_blocks, cmp_blocks = Q_WIDTH // tn, 2 * KV_WIDTH // tn
    proj = _matmul(_ep_scale, hn, w_in_t, layer, PROJ_ATT, BF16, 1024, tn, row_extra=col_scale,
                   w_rows_are_outputs=True, src_block=lambda j: jnp.where(j < q_blocks, j, j + cmp_blocks))
    kcvc = _matmul(lambda z: z, hn, w_in_t, layer, 2 * KV_WIDTH, F32, 1024, tn,
                   w_rows_are_outputs=True, src_block=lambda j: j + q_blocks)
    w_gate = w_in_t[layer, PROJ_MAIN:].reshape(N_GATES, N_KV_GROUPS, HPG, D_MODEL).transpose(1, 0, 2, 3)
    w_gate = w_gate.reshape(N_KV_GROUPS, N_GATES * HPG, D_MODEL)
    w_gate = jnp.pad(w_gate, ((0, 0), (0, LANES - N_GATES * HPG), (0, 0))).reshape(1, N_KV_GROUPS * LANES, D_MODEL)
    gates = _matmul(jax.nn.sigmoid, hn, w_gate, 0, N_KV_GROUPS * LANES, F32, 1024, 512, w_rows_are_outputs=True)

    w1 = jnp.stack([w1_k, w1_v]).astype(BF16)
    w2 = jnp.stack([w2_k, w2_v]).astype(BF16)
    pos = jnp.stack([pos_k.reshape(1, -1), pos_v.reshape(1, -1)]).astype(BF16)
    pos = jnp.broadcast_to(pos, (2, SUBLANES, CMP_BLOCK * HEAD_DIM))
    cc = _compress(kcvc, w1, w2, pos)
    kcc = jnp.pad(cc[0], ((0, 0), (CMP_PAD, 0), (0, 0)))
    vcct = jnp.pad(cc[1], ((0, 0), (CMP_PAD, 0), (0, 0))).transpose(0, 2, 1)
    n_rows = ncp + CMP_PAD

    nb = CMP_TQ // CMP_STRIDE + CMP_PAD
    first_dist = -CMP_STRIDE * (nb - 1 - CMP_PAD) - (CMP_BLOCK - 1)
    by_dist = _bias_table(rel_bias, first_dist + np.arange(CMP_TQ + CMP_STRIDE * (nb - 1)))
    band = jnp.stack([by_dist[:, CMP_STRIDE * (nb - 1 - b):CMP_STRIDE * (nb - 1 - b) + CMP_TQ] for b in range(nb)], axis=1)
    ratio = SEL_BLOCK // CMP_STRIDE
    lo = CMP_BLOCK // CMP_STRIDE - 1
    c_of_row = np.arange(n_rows)[None, :] - CMP_PAD
    j_of = np.arange(LANES)[:, None]
    mt = ((c_of_row >= ratio * j_of - lo) & (c_of_row <= ratio * j_of + ratio - 1)
          & (c_of_row >= 0) & (c_of_row <= ncp - 2) & (j_of < n_sel)).astype(np.float32)
    o_c, neg_mask = _cmp_attn(proj, kcc, vcct, band, jnp.asarray(mt, BF16), CMP_TQ)

    onehot = (np.arange(s)[:, None] // SEL_BLOCK == np.arange(LANES)[None, :]).astype(np.float32)
    periods = jnp.stack([_bias_period(rel_bias, tile, tile), _bias_period(rel_bias, 0, tile)], axis=1)
    o = _sel_win(proj, neg_mask, jnp.asarray(onehot, BF16), periods, gates, o_c, tile)
    return _matmul(_ep_resid, o, w_out, layer, D_MODEL, F32, 512, 1024, tile_extra=h)


def _s5_kernel(x_ref, lam_ref, bt_ref, c_ref, d_ref, y_ref, vt, yt, ys, tmask, *, chunk):
    L = chunk
    half = L // 2
    width = L * SSM_GROUP
    n_chunks = x_ref.shape[0] // L
    for tau in range(L):
        vt[tau] = x_ref[pl.ds(tau, n_chunks, stride=L), :].T

    @pl.when(pl.program_id(0) == 0)
    def _():
        dst = lax.broadcasted_iota(jnp.int32, (width, width), 0) // SSM_GROUP
        src = lax.broadcasted_iota(jnp.int32, (width, width), 1) // SSM_GROUP
        tmask[...] = jnp.where(src <= dst, 1.0, 0.0)

    def cmul(xr, xi, yr, yi):
        return xr * yr - xi * yi, xr * yi + xi * yr

    n2 = 2 * SSM_STATE
    sub = lax.broadcasted_iota(jnp.int32, (SUBLANES, 1), 0)
    consts = jnp.where(sub == 0, 1.0, jnp.where(sub == 1, float(half + 1), jnp.where(sub == 2, float(half - 1), float(L))))
    tau_col = lax.broadcasted_iota(jnp.int32, (L, 1), 0).astype(F32)
    row = lax.broadcasted_iota(jnp.int32, (n_chunks, n2), 0)
    conj = jnp.where(lax.broadcasted_iota(jnp.int32, (1, n2), 1) < SSM_STATE, 1.0, -1.0)

    def group(g, carry):
        lam = lam_ref[g]
        a_re, a_im = lam[0:1], lam[1:2]
        dt = jnp.exp(lam[2:3])
        log_re, log_im = a_re * dt, a_im * dt

        def cpow(e):
            mag, cos, sin = jnp.exp(e * log_re), jnp.cos(e * log_im), jnp.sin(e * log_im)
            inv = 1.0 / mag
            return mag * cos, mag * sin, inv * cos, -(inv * sin)

        kr, ki, _, _ = cpow(consts)
        pr, pi = kr[0:1] - 1.0, ki[0:1]
        den = a_re * a_re + a_im * a_im
        cf_r = (pr * a_re + pi * a_im) / den
        cf_i = (pi * a_re - pr * a_im) / den
        bt_c, bt_s = bt_ref[g, 0], bt_ref[g, 1]
        bb_c = cf_r * bt_c + cf_i * bt_s
        bb_s = cf_r * bt_s - cf_i * bt_c
        c_c, c_s = c_ref[g, 0], c_ref[g, 1]

        e1r, e1i, e2r, e2i = cpow(tau_col - half)
        e3r, e3i = cmul(e1r, e1i, kr[1:2], ki[1:2])
        e4r, e4i = cmul(e2r, e2i, kr[2:3], ki[2:3])

        def outer(er, ei, w_c, w_s):
            return (er[:, None, :] * w_c[None, :, :] + ei[:, None, :] * w_s[None, :, :]).reshape(width, n2)

        qm = outer(e1r, e1i, c_c, c_s)
        km_conj = outer(e2r, e2i, bb_c * conj, bb_s * conj)
        qc = outer(e3r, e3i, c_c, c_s)
        wz = outer(e4r, e4i, bb_c, bb_s)

        chans = pl.ds(pl.multiple_of(g * SSM_GROUP, SSM_GROUP), SSM_GROUP)
        ub = vt[:, chans, :].reshape(width, n_chunks).astype(BF16)

        tt = _dot_nt(qm.astype(BF16), km_conj.astype(BF16))
        y = _dot((tt * tmask[...]).astype(BF16), ub)

        x = _dot(wz.T.astype(BF16), ub).T
        p_r, p_i = kr[3:4], ki[3:4]
        d = 1
        while d < n_chunks:
            s = jnp.where(row >= d, pltpu.roll(x, d, 0), 0.0)
            x = x + s * p_r - pltpu.roll(s, SSM_STATE, 1) * (p_i * conj)
            p_r, p_i = cmul(p_r, p_i, p_r, p_i)
            d *= 2
        prev_conj = jnp.where(row >= 1, pltpu.roll(x, 1, 0), 0.0) * conj
        y = y + _dot_nt(qc.astype(BF16), prev_conj.astype(BF16))
        yt[:, chans, :] = y.reshape(L, SSM_GROUP, n_chunks)

    def group_batch(k, carry):
        for j in range(S5_GROUPS_PER_TRIP):
            group(S5_GROUPS_PER_TRIP * k + j, carry)
        return carry

    lax.fori_loop(0, LANES // SSM_GROUP // S5_GROUPS_PER_TRIP, group_batch, 0)
    for tau in range(L):
        ys[pl.ds(tau, n_chunks, stride=L), :] = yt[tau].T
    y_ref[...] = jax.nn.gelu(ys[...] + d_ref[...] * x_ref[...]).astype(y_ref.dtype)


def _s5_mixer(h, hn, a_re, a_im, log_dt, b_re, b_im, c_re, c_im, d_skip, w_glu, layer):
    s = h.shape[0]
    L = S5_CHUNK
    n_chunks = s // L
    width = L * SSM_GROUP
    gpb = LANES // SSM_GROUP
    def packed(re, im):
        return jnp.stack([jnp.concatenate([re, im], -1), jnp.concatenate([-im, re], -1)], axis=1).astype(F32)

    lam = jnp.stack([a_re, a_im, jnp.broadcast_to(log_dt[:, None], a_re.shape)], axis=1).astype(F32)
    lam = jnp.concatenate([lam, lam], axis=-1)
    bt = packed(b_re.transpose(0, 2, 1), b_im.transpose(0, 2, 1))
    cc = packed(c_re, c_im)
    dd = d_skip.astype(F32).reshape(1, D_MODEL)
    y = pl.pallas_call(
        functools.partial(_s5_kernel, chunk=L),
        out_shape=jax.ShapeDtypeStruct((s, D_MODEL), BF16),
        grid=(SSM_GROUPS // gpb,),
        in_specs=[
            pl.BlockSpec((s, LANES), lambda b: (0, b)),
            pl.BlockSpec((gpb, 3, 2 * SSM_STATE), lambda b: (b, 0, 0)),
            pl.BlockSpec((gpb, 2, SSM_GROUP, 2 * SSM_STATE), lambda b: (b, 0, 0, 0)),
            pl.BlockSpec((gpb, 2, SSM_GROUP, 2 * SSM_STATE), lambda b: (b, 0, 0, 0)),
            pl.BlockSpec((1, LANES), lambda b: (0, b)),
        ],
        out_specs=pl.BlockSpec((s, LANES), lambda b: (0, b)),
        scratch_shapes=[
            pltpu.VMEM((L, LANES, n_chunks), F32),
            pltpu.VMEM((L, LANES, n_chunks), F32),
            pltpu.VMEM((s, LANES), F32),
            pltpu.VMEM((width, width), F32),
        ],
        compiler_params=_params("arbitrary"),
    )(hn, lam, bt, cc, dd)
    return _matmul(_ep_glu_resid, y, w_glu, layer, D_MODEL, F32, 1024, 512, b_halves=2, tile_extra=h)


def kernel(x, rel_bias, mix_norm_g, ffn_norm_g, final_norm_g, nsa_w_in, nsa_w_out, cmp_pos_k, cmp_w1_k, cmp_w2_k, cmp_pos_v, cmp_w1_v, cmp_w2_v, s5_A_re, s5_A_im, s5_log_dt, s5_B_re, s5_B_im, s5_C_re, s5_C_im, s5_D, s5_w_glu, ffn_w_in, ffn_w_out):
    assert x.shape[0] == 1
    h = x[0]
    hn = _rmsnorm(h, mix_norm_g[0], BF16)
    h = _nsa_mixer(h, hn, rel_bias, nsa_w_in, nsa_w_out, 0, cmp_pos_k[0], cmp_w1_k[0], cmp_w2_k[0],
                   cmp_pos_v[0], cmp_w1_v[0], cmp_w2_v[0])
    h = _swiglu_ffn(h, ffn_norm_g[0], ffn_w_in, ffn_w_out, 0)
    hn = _rmsnorm(h, mix_norm_g[1], F32)
    h = _s5_mixer(h, hn, s5_A_re[0], s5_A_im[0], s5_log_dt[0], s5_B_re[0], s5_B_im[0], s5_C_re[0], s5_C_im[0],
                  s5_D[0], s5_w_glu, 0)
    h = _swiglu_ffn(h, ffn_norm_g[1], ffn_w_in, ffn_w_out, 1)
    return _rmsnorm(h, final_norm_g, x.dtype)[None]
```

```python
import functools
import math

import numpy as np
import jax
import jax.numpy as jnp
from jax import lax
from jax.experimental import pallas as pl
from jax.experimental.pallas import tpu as pltpu

D_MODEL = 2048
N_HEADS = 16
HEAD_DIM = 128
N_KV_GROUPS = 4
HPG = N_HEADS // N_KV_GROUPS
CMP_BLOCK = 32
CMP_STRIDE = 16
CMP_HIDDEN = 2 * HEAD_DIM
SEL_BLOCK = 64
SEL_TOPK = 16
N_FORCED = 3
WINDOW = 512
N_GATES = 3
KV_WIDTH = N_KV_GROUPS * HEAD_DIM
Q_WIDTH = N_HEADS * HEAD_DIM
PROJ_MAIN = Q_WIDTH + 6 * KV_WIDTH
PROJ_ATT = Q_WIDTH + 4 * KV_WIDTH
REL_BUCKETS = 32
REL_MAX_DIST = 128
SSM_GROUP = 16
SSM_GROUPS = D_MODEL // SSM_GROUP
SSM_STATE = 64
D_FF = ((8 * D_MODEL + 2) // 3 + 255) // 256 * 256
RMS_EPS = 1e-6
NEG = -1e30
FORCE_SCORE = 1e9
LOG2E = math.log2(math.e)

LANES = 128
SUBLANES = 8
VMEM_LIMIT = 52 * 1024 * 1024

ATT_TILE = WINDOW // 2
FAR_TILES = 4
CMP_TQ = 256
CMP_PAD = 8
S5_CHUNK = 32
S5_GROUPS_PER_TRIP = 4

BF16 = jnp.bfloat16
F32 = jnp.float32


def _dot(a, b):
    return jnp.dot(a, b, preferred_element_type=F32)


def _dot_nt(a, b):
    return lax.dot_general(a, b, (((1,), (1,)), ((), ())), preferred_element_type=F32)


def _params(*sem):
    return pltpu.CompilerParams(dimension_semantics=sem, vmem_limit_bytes=VMEM_LIMIT)


def _rmsnorm_kernel(x_ref, g_ref, o_ref):
    x = x_ref[...]
    ms = jnp.mean(x * x, axis=-1, keepdims=True)
    o_ref[...] = (x * lax.rsqrt(ms + RMS_EPS) * g_ref[...]).astype(o_ref.dtype)


def _rmsnorm(x, g, out_dtype, tm=512):
    s, d = x.shape
    return pl.pallas_call(
        _rmsnorm_kernel,
        out_shape=jax.ShapeDtypeStruct((s, d), out_dtype),
        grid=(s // tm,),
        in_specs=[pl.BlockSpec((tm, d), lambda i: (i, 0)), pl.BlockSpec((1, d), lambda i: (0, 0))],
        out_specs=pl.BlockSpec((tm, d), lambda i: (i, 0)),
        compiler_params=_params("parallel"),
    )(x, g.reshape(1, d))


def _mm_kernel(*refs, n_b, n_extra, epilogue, w_rows_are_outputs):
    a_ref = refs[0]
    b_refs = refs[1:1 + n_b]
    extra_refs = refs[1 + n_b:1 + n_b + n_extra]
    o_ref = refs[1 + n_b + n_extra]
    w_scr = refs[2 + n_b + n_extra:]

    @pl.when(pl.program_id(1) == 0)
    def _():
        for b_ref, w in zip(b_refs, w_scr):
            w[...] = b_ref[...].astype(BF16)

    a = a_ref[...]
    z = [(_dot_nt if w_rows_are_outputs else _dot)(a, w[...]) for w in w_scr]
    o_ref[...] = epilogue(*z, *[e[...] for e in extra_refs]).astype(o_ref.dtype)


def _ep_scale(z, scale):
    return z * scale


def _ep_resid(z, resid):
    return resid + z


def _ep_swiglu(za, zb):
    return jax.nn.silu(za) * zb


def _ep_glu_resid(za, zb, resid):
    return resid + za * jax.nn.sigmoid(zb)


def _matmul(epilogue, a, b, layer, n_out, out_dtype, tm, tn, *, b_halves=1, row_extra=None, tile_extra=None,
            w_rows_are_outputs=False, src_block=lambda j: j):
    m, k = a.shape
    nj = n_out // tn
    b_mode = pl.Buffered(1) if 2 * b_halves * k * tn * 4 > VMEM_LIMIT // 3 else None
    in_specs = [pl.BlockSpec((tm, k), lambda j, i: (i, 0))]
    args = [a]
    for half in range(b_halves):
        if w_rows_are_outputs:
            spec = pl.BlockSpec((None, tn, k), functools.partial(lambda j, i, o: (layer, src_block(j) + o, 0), o=half * nj),
                                pipeline_mode=b_mode)
        else:
            spec = pl.BlockSpec((None, k, tn), functools.partial(lambda j, i, o: (layer, 0, src_block(j) + o), o=half * nj),
                                pipeline_mode=b_mode)
        in_specs.append(spec)
        args.append(b)
    extras = []
    if row_extra is not None:
        in_specs.append(pl.BlockSpec((1, tn), lambda j, i: (0, j)))
        extras.append(row_extra)
    if tile_extra is not None:
        in_specs.append(pl.BlockSpec((tm, tn), lambda j, i: (i, j)))
        extras.append(tile_extra)
    kern = functools.partial(_mm_kernel, n_b=b_halves, n_extra=len(extras), epilogue=epilogue,
                             w_rows_are_outputs=w_rows_are_outputs)
    return pl.pallas_call(
        kern,
        out_shape=jax.ShapeDtypeStruct((m, n_out), out_dtype),
        grid=(nj, m // tm),
        in_specs=in_specs,
        out_specs=pl.BlockSpec((tm, tn), lambda j, i: (i, j)),
        scratch_shapes=[pltpu.VMEM((tn, k) if w_rows_are_outputs else (k, tn), BF16) for _ in range(b_halves)],
        compiler_params=_params("arbitrary", "arbitrary"),
    )(*args, *extras)


def _swiglu_ffn(h, g, w_in, w_out, layer):
    hn = _rmsnorm(h, g, BF16)
    act = _matmul(_ep_swiglu, hn, w_in, layer, D_FF, BF16, 1024, 512, b_halves=2)
    return _matmul(_ep_resid, act, w_out, layer, D_MODEL, F32, 512, 512, tile_extra=h)


def _compress_kernel(x_ref, w1_ref, w2_ref, pos_ref, o_ref):
    ncp = x_ref.shape[0] // CMP_STRIDE
    top = jnp.zeros((ncp, CMP_HIDDEN), F32)
    bot = jnp.zeros((ncp, CMP_HIDDEN), F32)
    for r in range(CMP_STRIDE):
        x_r = x_ref[pl.ds(r, ncp, stride=CMP_STRIDE), :].astype(BF16)
        top = top + _dot(x_r, w1_ref[0, r * HEAD_DIM:(r + 1) * HEAD_DIM, :])
        bot = bot + _dot(x_r, w1_ref[0, (CMP_STRIDE + r) * HEAD_DIM:(CMP_STRIDE + r + 1) * HEAD_DIM, :])
    posb = _dot(pos_ref[0], w1_ref[0])[0:1]
    hid = top + pltpu.roll(bot, ncp - 1, 0) + posb
    o_ref[0, 0] = _dot(jax.nn.gelu(hid).astype(BF16), w2_ref[0]).astype(o_ref.dtype)


def _compress(kcvc, w1, w2, pos):
    s = kcvc.shape[0]
    ncp = s // CMP_STRIDE
    return pl.pallas_call(
        _compress_kernel,
        out_shape=jax.ShapeDtypeStruct((2, N_KV_GROUPS, ncp, HEAD_DIM), BF16),
        grid=(2, N_KV_GROUPS),
        in_specs=[
            pl.BlockSpec((s, HEAD_DIM), lambda kv, gi: (0, kv * N_KV_GROUPS + gi)),
            pl.BlockSpec((1, CMP_BLOCK * HEAD_DIM, CMP_HIDDEN), lambda kv, gi: (kv, 0, 0)),
            pl.BlockSpec((1, CMP_HIDDEN, HEAD_DIM), lambda kv, gi: (kv, 0, 0)),
            pl.BlockSpec((1, SUBLANES, CMP_BLOCK * HEAD_DIM), lambda kv, gi: (kv, 0, 0)),
        ],
        out_specs=pl.BlockSpec((1, 1, ncp, HEAD_DIM), lambda kv, gi: (kv, gi, 0, 0)),
        compiler_params=_params("parallel", "parallel"),
    )(kcvc, w1, w2, pos)


def _cmp_attn_kernel(q_ref, kcc_ref, vcct_ref, band_ref, mt_ref, oc_ref, nm_ref, s_scr, *, tq, nb):
    i = pl.program_id(1)
    band_start = pl.multiple_of(i * (tq // CMP_STRIDE), SUBLANES)
    n_rows = kcc_ref.shape[1]
    n_blk = mt_ref.shape[0]

    def body(n_r, n_b):
        kcc = kcc_ref[0, :n_r, :]
        vcct = vcct_ref[0, :, :n_r]
        rows = lax.broadcasted_iota(jnp.int32, (n_r, tq), 0)
        row_bias = jnp.where(rows >= CMP_PAD, jnp.where(rows < band_start + nb, 0.0, NEG), NEG)
        imp = jnp.zeros((n_r, tq), F32)

        def scores(h):
            qh = q_ref[:, h * HEAD_DIM:(h + 1) * HEAD_DIM]
            s_scr[h, :n_r, :] = _dot_nt(kcc, qh) + row_bias
            s_scr[h, pl.ds(band_start, nb), :] += band_ref[h]

        scores(0)
        for h in range(HPG):
            if h + 1 < HPG:
                scores(h + 1)
            s = s_scr[h, :n_r, :]
            m = jnp.maximum(jnp.max(s, axis=0, keepdims=True), 0.5 * NEG)
            e = jnp.exp2(s - m)
            l = jnp.sum(e, axis=0, keepdims=True)
            pn = e * (1.0 / jnp.where(l > 0.0, l, 1.0))
            imp = imp + pn
            oct_h = _dot(vcct, pn.astype(BF16))
            oc_ref[:, h * HEAD_DIM:(h + 1) * HEAD_DIM] = oct_h.T
        mt = mt_ref[:n_b, :n_r]
        hi = imp.astype(BF16)
        rest = imp - hi.astype(F32)
        mid = rest.astype(BF16)
        lo = (rest - mid.astype(F32)).astype(BF16)
        slc = _dot(mt, hi) + _dot(mt, mid) + _dot(mt, lo)
        jj = lax.broadcasted_iota(jnp.int32, (n_b, tq), 0)
        t_blk = (i * tq + lax.broadcasted_iota(jnp.int32, (n_b, tq), 1)) // SEL_BLOCK
        forced = jnp.where(jj == 0, 1, jnp.where(jj == t_blk, 1, jnp.where(jj == t_blk - 1, 1, 0)))
        picked = jnp.where(forced == 1, 1.0, 0.0)
        score = jnp.where(forced == 1, -jnp.inf, jnp.where(jj <= t_blk, slc, NEG))
        for _ in range(SEL_TOPK - N_FORCED):
            best = jnp.max(score, axis=0, keepdims=True)
            first = jnp.min(jnp.where(score == best, jj, n_blk), axis=0, keepdims=True)
            hit = jj == first
            picked = jnp.where(hit, 1.0, picked)
            score = jnp.where(hit, -jnp.inf, score)
        neg_mask = jnp.where(picked > 0.5, jnp.where(jj <= t_blk, 0.0, NEG), NEG)
        if n_b < n_blk:
            neg_mask = jnp.concatenate([neg_mask, jnp.full((n_blk - n_b, tq), NEG, F32)], axis=0)
        nm_ref[...] = neg_mask.T.astype(nm_ref.dtype)

    step = LANES
    extents = list(range(step, n_rows, step)) + [n_rows]
    need_rows = band_start + nb
    for v, n_r in enumerate(extents):
        lower = extents[v - 1] if v else 0
        n_b = min(n_blk, -(-(n_r * CMP_STRIDE // SEL_BLOCK) // SUBLANES) * SUBLANES)
        pl.when(jnp.logical_and(need_rows > lower, need_rows <= n_r))(functools.partial(body, n_r, n_b))


def _cmp_attn(proj, kcc, vcct, band, mt, tq):
    s = proj.shape[0]
    n_rows = kcc.shape[1]
    nb = band.shape[1]
    kern = functools.partial(_cmp_attn_kernel, tq=tq, nb=nb)
    return pl.pallas_call(
        kern,
        out_shape=(jax.ShapeDtypeStruct((s, Q_WIDTH), F32),
                   jax.ShapeDtypeStruct((s, N_KV_GROUPS * LANES), BF16)),
        grid=(N_KV_GROUPS, s // tq),
        in_specs=[
            pl.BlockSpec((tq, HPG * HEAD_DIM), lambda g, i: (i, g)),
            pl.BlockSpec((1, n_rows, HEAD_DIM), lambda g, i: (g, 0, 0)),
            pl.BlockSpec((1, HEAD_DIM, n_rows), lambda g, i: (g, 0, 0)),
            pl.BlockSpec((HPG, nb, tq), lambda g, i: (g, 0, 0)),
            pl.BlockSpec((LANES, n_rows), lambda g, i: (0, 0)),
        ],
        out_specs=(pl.BlockSpec((tq, HPG * HEAD_DIM), lambda g, i: (i, g)),
                   pl.BlockSpec((tq, LANES), lambda g, i: (i, g))),
        scratch_shapes=[pltpu.VMEM((HPG, n_rows, tq), F32)],
        compiler_params=_params("parallel", "parallel"),
    )(proj, kcc, vcct, band, mt)


def _lane_tile(x, width):
    return jnp.concatenate([x] * (width // x.shape[1]), axis=1)


def _softmax_update(s, v_aug, m_ref, acc_ref, rows):
    m_prev = m_ref[rows, :]
    m_next = jnp.maximum(m_prev, jnp.max(s, axis=1, keepdims=True))
    alpha = jnp.exp2(m_prev - m_next)
    p = jnp.exp2(s - _lane_tile(m_next, s.shape[1]))
    acc_ref[rows, :] = _lane_tile(alpha, acc_ref.shape[1]) * acc_ref[rows, :] + _dot(p.astype(BF16), v_aug)
    m_ref[rows, :] = m_next


def _softmax_once(s, v_aug):
    m = jnp.broadcast_to(jnp.max(s, axis=1, keepdims=True), (s.shape[0], LANES))
    p = jnp.exp2(s - _lane_tile(m, s.shape[1]))
    return _dot(p.astype(BF16), v_aug)


def _sel_win_kernel(q_ref, nm_ref, ks_ref, vs_ref, kw_ref, vw_ref, blk_ref, per_ref, g_ref, oc_ref, o_ref,
                    q4, m_s, a_s, a_w, bias_ref, *, tile):
    i = pl.program_id(1)
    t = tile

    @pl.when(i == 0)
    def _():
        tl = lax.broadcasted_iota(jnp.int32, (t, t), 0)
        kl = lax.broadcasted_iota(jnp.int32, (t, t), 1)
        edge = jnp.where(tl < kl, 0.0, NEG)
        for h in range(HPG):
            bias_ref[h, :, :t] = edge
            for which in range(2):
                rows = jnp.broadcast_to(per_ref[h, which:which + 1, :], (t, 2 * t))
                table = pltpu.roll(rows, 0, 1, stride=1, stride_axis=0)
                bias_ref[h, :, (which + 1) * t:(which + 2) * t] = table[:, :t]
    for h in range(HPG):
        q4[h * t:(h + 1) * t, :HEAD_DIM] = q_ref[:, h * HEAD_DIM:(h + 1) * HEAD_DIM]
        q4[h * t:(h + 1) * t, HEAD_DIM:] = nm_ref[...]
    m_s[...] = jnp.full(m_s.shape, -3e38, F32)
    a_s[...] = jnp.zeros(a_s.shape, F32)

    def sel_operands(row0, n_tiles):
        keys = pl.ds(row0, n_tiles * t)
        k_aug = jnp.concatenate([ks_ref[keys, :], blk_ref[keys, :]], axis=1)
        v_aug = jnp.concatenate([vs_ref[keys, :], jnp.ones((n_tiles * t, LANES), BF16)], axis=1)
        return k_aug, v_aug

    def sel_scores(h, k_aug, n_biased):
        s = _dot_nt(q4[h * t:(h + 1) * t, :], k_aug)
        plain = s.shape[1] - n_biased * t
        if n_biased:
            near = s[:, plain:] + bias_ref[h, :, (3 - n_biased) * t:]
            s = jnp.concatenate([s[:, :plain], near], axis=1) if plain else near
        return s

    def sel_pass(row0, n_tiles, n_biased):
        k_aug, v_aug = sel_operands(row0, n_tiles)

        def consume(h, s):
            _softmax_update(s, v_aug, m_s, a_s, slice(h * t, (h + 1) * t))

        return functools.partial(sel_scores, k_aug=k_aug, n_biased=n_biased), consume

    def win_pass(row0, n_tiles):
        keys = pl.ds(row0, n_tiles * t)
        v_aug = jnp.concatenate([vw_ref[keys, :], jnp.ones((n_tiles * t, LANES), BF16)], axis=1)

        def score(h):
            return _dot_nt(q4[h * t:(h + 1) * t, :HEAD_DIM], kw_ref[keys, :]) + bias_ref[h, :, (3 - n_tiles) * t:]

        def consume(h, s):
            a_w[h * t:(h + 1) * t, :] = _softmax_once(s, v_aug)

        return score, consume

    def run(passes):
        work = [(score, consume, h0) for score, consume in passes for h0 in range(0, HPG, 2)]
        ahead = [work[0][0](h) for h in (0, 1)]
        for n, (_, consume, h0) in enumerate(work):
            now = ahead
            if n + 1 < len(work):
                nxt_score, _, nxt_h0 = work[n + 1]
                ahead = [nxt_score(h) for h in (nxt_h0, nxt_h0 + 1)]
            for h, s in zip((h0, h0 + 1), now):
                consume(h, s)

    n_far = jnp.maximum(i - 1, 0)

    n_chunks = n_far // FAR_TILES
    chunk = FAR_TILES * t

    def far_pair(c, carry):
        row0 = pl.multiple_of(c * 2 * chunk, 2 * chunk)
        run([sel_pass(row0, FAR_TILES, 0), sel_pass(row0 + chunk, FAR_TILES, 0)])
        return carry

    lax.fori_loop(0, n_chunks // 2, far_pair, 0)

    @pl.when(n_chunks % 2 == 1)
    def _():
        run([sel_pass(pl.multiple_of((n_chunks - 1) * chunk, chunk), FAR_TILES, 0)])

    for left in range(FAR_TILES):
        @pl.when(jnp.logical_and(i >= 2, n_far % FAR_TILES == left))
        def _(left=left):
            run([sel_pass(pl.multiple_of((i - 1 - left) * t, t), left + 2, 2),
                 win_pass(pl.multiple_of((i - 2) * t, t), 3)])

    @pl.when(i == 1)
    def _():
        run([sel_pass(0, 2, 2), win_pass(0, 2)])

    @pl.when(i == 0)
    def _():
        run([sel_pass(0, 1, 1), win_pass(0, 1)])

    acc_s = a_s[...]
    acc_w = a_w[...]
    o_s = acc_s[:, :HEAD_DIM] * (1.0 / acc_s[:, HEAD_DIM:])
    o_w = acc_w[:, :HEAD_DIM] * (1.0 / acc_w[:, HEAD_DIM:])
    gates = g_ref[...]
    for h in range(HPG):
        cols = slice(h * HEAD_DIM, (h + 1) * HEAD_DIM)
        o = (gates[:, h:h + 1] * oc_ref[:, cols]
             + gates[:, HPG + h:HPG + h + 1] * o_s[h * t:(h + 1) * t]
             + gates[:, 2 * HPG + h:2 * HPG + h + 1] * o_w[h * t:(h + 1) * t])
        o_ref[:, cols] = o.astype(o_ref.dtype)


def _sel_win(proj, neg_mask, blk_onehot, periods, gates, o_c, tile):
    s = proj.shape[0]
    ks_blk = Q_WIDTH // HEAD_DIM
    vs_blk = (Q_WIDTH + KV_WIDTH) // HEAD_DIM
    kw_blk = (Q_WIDTH + 2 * KV_WIDTH) // HEAD_DIM
    vw_blk = (Q_WIDTH + 3 * KV_WIDTH) // HEAD_DIM
    once = pl.Buffered(1)
    rows4 = HPG * tile
    kern = functools.partial(_sel_win_kernel, tile=tile)
    return pl.pallas_call(
        kern,
        out_shape=jax.ShapeDtypeStruct((s, Q_WIDTH), BF16),
        grid=(N_KV_GROUPS, s // tile),
        in_specs=[
            pl.BlockSpec((tile, HPG * HEAD_DIM), lambda g, i: (i, g)),
            pl.BlockSpec((tile, LANES), lambda g, i: (i, g)),
            pl.BlockSpec((s, HEAD_DIM), lambda g, i: (0, ks_blk + g), pipeline_mode=once),
            pl.BlockSpec((s, HEAD_DIM), lambda g, i: (0, vs_blk + g), pipeline_mode=once),
            pl.BlockSpec((s, HEAD_DIM), lambda g, i: (0, kw_blk + g), pipeline_mode=once),
            pl.BlockSpec((s, HEAD_DIM), lambda g, i: (0, vw_blk + g), pipeline_mode=once),
            pl.BlockSpec((s, LANES), lambda g, i: (0, 0), pipeline_mode=once),
            pl.BlockSpec((HPG, 2, 2 * tile), lambda g, i: (g, 0, 0)),
            pl.BlockSpec((tile, LANES), lambda g, i: (i, g)),
            pl.BlockSpec((tile, HPG * HEAD_DIM), lambda g, i: (i, g)),
        ],
        out_specs=pl.BlockSpec((tile, HPG * HEAD_DIM), lambda g, i: (i, g)),
        scratch_shapes=[
            pltpu.VMEM((rows4, 2 * HEAD_DIM), BF16),
            pltpu.VMEM((rows4, LANES), F32),
            pltpu.VMEM((rows4, 2 * HEAD_DIM), F32),
            pltpu.VMEM((rows4, 2 * HEAD_DIM), F32),
            pltpu.VMEM((HPG, tile, 3 * tile), F32),
        ],
        compiler_params=_params("arbitrary", "arbitrary"),
    )(proj, neg_mask, proj, proj, proj, proj, blk_onehot, periods, gates, o_c)


def _rel_bucket_np(dist):
    n = np.maximum(dist, 0)
    max_exact = REL_BUCKETS // 2
    large = max_exact + (np.log(np.maximum(n, 1).astype(np.float32) / np.float32(max_exact))
                         / np.float32(math.log(REL_MAX_DIST / max_exact))
                         * np.float32(REL_BUCKETS - max_exact)).astype(np.int32)
    large = np.minimum(large, REL_BUCKETS - 1)
    return np.where(n < max_exact, n, large).astype(np.int32)


def _far_distance():
    d = np.arange(4 * REL_MAX_DIST)
    b = _rel_bucket_np(d)
    assert b[-1] == REL_BUCKETS - 1
    return int(np.max(np.nonzero(b != REL_BUCKETS - 1)[0])) + 1


def _bias_table(rel_bias, dist):
    bucket = jnp.asarray(_rel_bucket_np(dist))
    shifted = (rel_bias[bucket] - rel_bias[REL_BUCKETS - 1]) * LOG2E
    tab = jnp.where(jnp.asarray(dist >= 0)[..., None], shifted, NEG)
    return jnp.moveaxis(tab, -1, 0).astype(F32)


def _bias_period(rel_bias, offset, t):
    x = np.arange(2 * t)
    return _bias_table(rel_bias, np.where(x < t, offset - x, offset + 2 * t - x))


def _nsa_mixer(h, hn, rel_bias, w_in, w_out, layer, pos_k, w1_k, w2_k, pos_v, w1_v, w2_v):
    s = h.shape[0]
    n_sel = s // SEL_BLOCK
    ncp = s // CMP_STRIDE
    tile = ATT_TILE
    far = _far_distance()
    assert n_sel <= LANES and s % CMP_TQ == 0 and s % tile == 0
    assert far <= tile and far <= CMP_STRIDE * CMP_PAD - CMP_BLOCK + 1 + CMP_STRIDE

    col_scale = jnp.where(jnp.arange(PROJ_ATT) < Q_WIDTH, HEAD_DIM ** -0.5 * LOG2E, 1.0).astype(F32)[None, :]
    w_in_t = jnp.swapaxes(w_in, 1, 2)
    tn = KV_WIDTH
    q_blocks, cmp_blocks = Q_WIDTH // tn, 2 * KV_WIDTH // tn
    proj = _matmul(_ep_scale, hn, w_in_t, layer, PROJ_ATT, BF16, 1024, tn, row_extra=col_scale,
                   w_rows_are_outputs=True, src_block=lambda j: jnp.where(j < q_blocks, j, j + cmp_blocks))
    kcvc = _matmul(lambda z: z, hn, w_in_t, layer, 2 * KV_WIDTH, F32, 1024, tn,
                   w_rows_are_outputs=True, src_block=lambda j: j + q_blocks)
    w_gate = w_in_t[layer, PROJ_MAIN:].reshape(N_GATES, N_KV_GROUPS, HPG, D_MODEL).transpose(1, 0, 2, 3)
    w_gate = w_gate.reshape(N_KV_GROUPS, N_GATES * HPG, D_MODEL)
    w_gate = jnp.pad(w_gate, ((0, 0), (0, LANES - N_GATES * HPG), (0, 0))).reshape(1, N_KV_GROUPS * LANES, D_MODEL)
    gates = _matmul(jax.nn.sigmoid, hn, w_gate, 0, N_KV_GROUPS * LANES, F32, 1024, 512, w_rows_are_outputs=True)

    w1 = jnp.stack([w1_k, w1_v]).astype(BF16)
    w2 = jnp.stack([w2_k, w2_v]).astype(BF16)
    pos = jnp.stack([pos_k.reshape(1, -1), pos_v.reshape(1, -1)]).astype(BF16)
    pos = jnp.broadcast_to(pos, (2, SUBLANES, CMP_BLOCK * HEAD_DIM))
    cc = _compress(kcvc, w1, w2, pos)
    kcc = jnp.pad(cc[0], ((0, 0), (CMP_PAD, 0), (0, 0)))
    vcct = jnp.pad(cc[1], ((0, 0), (CMP_PAD, 0), (0, 0))).transpose(0, 2, 1)
    n_rows = ncp + CMP_PAD

    nb = CMP_TQ // CMP_STRIDE + CMP_PAD
    first_dist = -CMP_STRIDE * (nb - 1 - CMP_PAD) - (CMP_BLOCK - 1)
    by_dist = _bias_table(rel_bias, first_dist + np.arange(CMP_TQ + CMP_STRIDE * (nb - 1)))
    band = jnp.stack([by_dist[:, CMP_STRIDE * (nb - 1 - b):CMP_STRIDE * (nb - 1 - b) + CMP_TQ] for b in range(nb)], axis=1)
    ratio = SEL_BLOCK // CMP_STRIDE
    lo = CMP_BLOCK // CMP_STRIDE - 1
    c_of_row = np.arange(n_rows)[None, :] - CMP_PAD
    j_of = np.arange(LANES)[:, None]
    mt = ((c_of_row >= ratio * j_of - lo) & (c_of_row <= ratio * j_of + ratio - 1)
          & (c_of_row >= 0) & (c_of_row <= ncp - 2) & (j_of < n_sel)).astype(np.float32)
    o_c, neg_mask = _cmp_attn(proj, kcc, vcct, band, jnp.asarray(mt, BF16), CMP_TQ)

    onehot = (np.arange(s)[:, None] // SEL_BLOCK == np.arange(LANES)[None, :]).astype(np.float32)
    periods = jnp.stack([_bias_period(rel_bias, tile, tile), _bias_period(rel_bias, 0, tile)], axis=1)
    o = _sel_win(proj, neg_mask, jnp.asarray(onehot, BF16), periods, gates, o_c, tile)
    return _matmul(_ep_resid, o, w_out, layer, D_MODEL, F32, 512, 1024, tile_extra=h)


def _s5_kernel(x_ref, lam_ref, bt_ref, c_ref, d_ref, y_ref, vt, yt, ys, tmask, *, chunk):
    L = chunk
    half = L // 2
    width = L * SSM_GROUP
    n_chunks = x_ref.shape[0] // L
    for tau in range(L):
        vt[tau] = x_ref[pl.ds(tau, n_chunks, stride=L), :].T

    @pl.when(pl.program_id(0) == 0)
    def _():
        dst = lax.broadcasted_iota(jnp.int32, (width, width), 0) // SSM_GROUP
        src = lax.broadcasted_iota(jnp.int32, (width, width), 1) // SSM_GROUP
        tmask[...] = jnp.where(src <= dst, 1.0, 0.0)

    def cmul(xr, xi, yr, yi):
        return xr * yr - xi * yi, xr * yi + xi * yr

    n2 = 2 * SSM_STATE
    sub = lax.broadcasted_iota(jnp.int32, (SUBLANES, 1), 0)
    consts = jnp.where(sub == 0, 1.0, jnp.where(sub == 1, float(half + 1), jnp.where(sub == 2, float(half - 1), float(L))))
    tau_col = lax.broadcasted_iota(jnp.int32, (L, 1), 0).astype(F32)
    row = lax.broadcasted_iota(jnp.int32, (n_chunks, n2), 0)
    conj = jnp.where(lax.broadcasted_iota(jnp.int32, (1, n2), 1) < SSM_STATE, 1.0, -1.0)

    def group(g):
        lam = lam_ref[g]
        a_re, a_im = lam[0:1], lam[1:2]
        dt = jnp.exp(lam[2:3])
        log_re, log_im = a_re * dt, a_im * dt

        def cpow(e):
            mag, cos, sin = jnp.exp(e * log_re), jnp.cos(e * log_im), jnp.sin(e * log_im)
            inv = 1.0 / mag
            return mag * cos, mag * sin, inv * cos, -(inv * sin)

        kr, ki, _, _ = cpow(consts)
        pr, pi = kr[0:1] - 1.0, ki[0:1]
        den = a_re * a_re + a_im * a_im
        cf_r = (pr * a_re + pi * a_im) / den
        cf_i = (pi * a_re - pr * a_im) / den
        bt_c, bt_s = bt_ref[g, 0], bt_ref[g, 1]
        bb_c = cf_r * bt_c + cf_i * bt_s
        bb_s = cf_r * bt_s - cf_i * bt_c
        c_c, c_s = c_ref[g, 0], c_ref[g, 1]

        e1r, e1i, e2r, e2i = cpow(tau_col - half)
        e3r, e3i = cmul(e1r, e1i, kr[1:2], ki[1:2])
        e4r, e4i = cmul(e2r, e2i, kr[2:3], ki[2:3])

        def outer(er, ei, w_c, w_s):
            return (er[:, None, :] * w_c[None, :, :] + ei[:, None, :] * w_s[None, :, :]).reshape(width, n2)

        qm = outer(e1r, e1i, c_c, c_s)
        km_conj = outer(e2r, e2i, bb_c * conj, bb_s * conj)
        qc = outer(e3r, e3i, c_c, c_s)
        wz = outer(e4r, e4i, bb_c, bb_s)

        chans = pl.ds(pl.multiple_of(g * SSM_GROUP, SSM_GROUP), SSM_GROUP)
        ub = vt[:, chans, :].reshape(width, n_chunks).astype(BF16)
        yield

        tt = _dot_nt(qm.astype(BF16), km_conj.astype(BF16))
        x = _dot(wz.T.astype(BF16), ub).T
        yield
        y = _dot((tt * tmask[...]).astype(BF16), ub)
        p_r, p_i = kr[3:4], ki[3:4]
        d = 1
        while d < n_chunks:
            s = jnp.where(row >= d, pltpu.roll(x, d, 0), 0.0)
            x = x + s * p_r - pltpu.roll(s, SSM_STATE, 1) * (p_i * conj)
            p_r, p_i = cmul(p_r, p_i, p_r, p_i)
            d *= 2
        prev_conj = jnp.where(row >= 1, pltpu.roll(x, 1, 0), 0.0) * conj
        yield
        y = y + _dot_nt(qc.astype(BF16), prev_conj.astype(BF16))
        yt[:, chans, :] = y.reshape(L, SSM_GROUP, n_chunks)

    def group_batch(k, carry):
        running = [group(S5_GROUPS_PER_TRIP * k + j) for j in range(S5_GROUPS_PER_TRIP)]
        while running:
            running = [gen for gen in running if next(gen, running) is not running]
        return carry

    lax.fori_loop(0, LANES // SSM_GROUP // S5_GROUPS_PER_TRIP, group_batch, 0)
    for tau in range(L):
        ys[pl.ds(tau, n_chunks, stride=L), :] = yt[tau].T
    y_ref[...] = jax.nn.gelu(ys[...] + d_ref[...] * x_ref[...]).astype(y_ref.dtype)


def _s5_mixer(h, hn, a_re, a_im, log_dt, b_re, b_im, c_re, c_im, d_skip, w_glu, layer):
    s = h.shape[0]
    L = S5_CHUNK
    n_chunks = s // L
    width = L * SSM_GROUP
    gpb = LANES // SSM_GROUP
    def packed(re, im):
        return jnp.stack([jnp.concatenate([re, im], -1), jnp.concatenate([-im, re], -1)], axis=1).astype(F32)

    lam = jnp.stack([a_re, a_im, jnp.broadcast_to(log_dt[:, None], a_re.shape)], axis=1).astype(F32)
    lam = jnp.concatenate([lam, lam], axis=-1)
    bt = packed(b_re.transpose(0, 2, 1), b_im.transpose(0, 2, 1))
    cc = packed(c_re, c_im)
    dd = d_skip.astype(F32).reshape(1, D_MODEL)
    y = pl.pallas_call(
        functools.partial(_s5_kernel, chunk=L),
        out_shape=jax.ShapeDtypeStruct((s, D_MODEL), BF16),
        grid=(SSM_GROUPS // gpb,),
        in_specs=[
            pl.BlockSpec((s, LANES), lambda b: (0, b)),
            pl.BlockSpec((gpb, 3, 2 * SSM_STATE), lambda b: (b, 0, 0)),
            pl.BlockSpec((gpb, 2, SSM_GROUP, 2 * SSM_STATE), lambda b: (b, 0, 0, 0)),
            pl.BlockSpec((gpb, 2, SSM_GROUP, 2 * SSM_STATE), lambda b: (b, 0, 0, 0)),
            pl.BlockSpec((1, LANES), lambda b: (0, b)),
        ],
        out_specs=pl.BlockSpec((s, LANES), lambda b: (0, b)),
        scratch_shapes=[
            pltpu.VMEM((L, LANES, n_chunks), F32),
            pltpu.VMEM((L, LANES, n_chunks), F32),
            pltpu.VMEM((s, LANES), F32),
            pltpu.VMEM((width, width), F32),
        ],
        compiler_params=_params("arbitrary"),
    )(hn, lam, bt, cc, dd)
    return _matmul(_ep_glu_resid, y, w_glu, layer, D_MODEL, F32, 1024, 512, b_halves=2, tile_extra=h)


def kernel(x, rel_bias, mix_norm_g, ffn_norm_g, final_norm_g, nsa_w_in, nsa_w_out, cmp_pos_k, cmp_w1_k, cmp_w2_k, cmp_pos_v, cmp_w1_v, cmp_w2_v, s5_A_re, s5_A_im, s5_log_dt, s5_B_re, s5_B_im, s5_C_re, s5_C_im, s5_D, s5_w_glu, ffn_w_in, ffn_w_out):
    assert x.shape[0] == 1
    h = x[0]
    hn = _rmsnorm(h, mix_norm_g[0], BF16)
    h = _nsa_mixer(h, hn, rel_bias, nsa_w_in, nsa_w_out, 0, cmp_pos_k[0], cmp_w1_k[0], cmp_w2_k[0],
                   cmp_pos_v[0], cmp_w1_v[0], cmp_w2_v[0])
    h = _swiglu_ffn(h, ffn_norm_g[0], ffn_w_in, ffn_w_out, 0)
    hn = _rmsnorm(h, mix_norm_g[1], F32)
    h = _s5_mixer(h, hn, s5_A_re[0], s5_A_im[0], s5_log_dt[0], s5_B_re[0], s5_B_im[0], s5_C_re[0], s5_C_im[0],
                  s5_D[0], s5_w_glu, 0)
    h = _swiglu_ffn(h, ffn_norm_g[1], ffn_w_in, ffn_w_out, 1)
    return _rmsnorm(h, final_norm_g, x.dtype)[None]
```

```python
import functools
import math

import numpy as np
import jax
import jax.numpy as jnp
from jax import lax
from jax.experimental import pallas as pl
from jax.experimental.pallas import tpu as pltpu

D_MODEL = 2048
N_HEADS = 16
HEAD_DIM = 128
N_KV_GROUPS = 4
HPG = N_HEADS // N_KV_GROUPS
CMP_BLOCK = 32
CMP_STRIDE = 16
CMP_HIDDEN = 2 * HEAD_DIM
SEL_BLOCK = 64
SEL_TOPK = 16
N_FORCED = 3
WINDOW = 512
N_GATES = 3
KV_WIDTH = N_KV_GROUPS * HEAD_DIM
Q_WIDTH = N_HEADS * HEAD_DIM
PROJ_MAIN = Q_WIDTH + 6 * KV_WIDTH
PROJ_ATT = Q_WIDTH + 4 * KV_WIDTH
REL_BUCKETS = 32
REL_MAX_DIST = 128
SSM_GROUP = 16
SSM_GROUPS = D_MODEL // SSM_GROUP
SSM_STATE = 64
D_FF = ((8 * D_MODEL + 2) // 3 + 255) // 256 * 256
RMS_EPS = 1e-6
NEG = -1e30
FORCE_SCORE = 1e9
LOG2E = math.log2(math.e)

LANES = 128
SUBLANES = 8
VMEM_LIMIT = 52 * 1024 * 1024

ATT_TILE = WINDOW // 2
FAR_TILES = 4
CMP_TQ = 256
CMP_PAD = 8
S5_CHUNK = 32
S5_GROUPS_PER_TRIP = 4

BF16 = jnp.bfloat16
F32 = jnp.float32


def _dot(a, b):
    return jnp.dot(a, b, preferred_element_type=F32)


def _dot_nt(a, b):
    return lax.dot_general(a, b, (((1,), (1,)), ((), ())), preferred_element_type=F32)


def _params(*sem):
    return pltpu.CompilerParams(dimension_semantics=sem, vmem_limit_bytes=VMEM_LIMIT)


def _rmsnorm_kernel(x_ref, g_ref, o_ref):
    x = x_ref[...]
    ms = jnp.mean(x * x, axis=-1, keepdims=True)
    o_ref[...] = (x * lax.rsqrt(ms + RMS_EPS) * g_ref[...]).astype(o_ref.dtype)


def _rmsnorm(x, g, out_dtype, tm=512):
    s, d = x.shape
    return pl.pallas_call(
        _rmsnorm_kernel,
        out_shape=jax.ShapeDtypeStruct((s, d), out_dtype),
        grid=(s // tm,),
        in_specs=[pl.BlockSpec((tm, d), lambda i: (i, 0)), pl.BlockSpec((1, d), lambda i: (0, 0))],
        out_specs=pl.BlockSpec((tm, d), lambda i: (i, 0)),
        compiler_params=_params("parallel"),
    )(x, g.reshape(1, d))


def _mm_kernel(*refs, n_b, n_extra, epilogue, w_rows_are_outputs):
    a_ref = refs[0]
    b_refs = refs[1:1 + n_b]
    extra_refs = refs[1 + n_b:1 + n_b + n_extra]
    o_ref = refs[1 + n_b + n_extra]
    w_scr = refs[2 + n_b + n_extra:]

    @pl.when(pl.program_id(1) == 0)
    def _():
        for b_ref, w in zip(b_refs, w_scr):
            w[...] = b_ref[...].astype(BF16)

    a = a_ref[...]
    z = [(_dot_nt if w_rows_are_outputs else _dot)(a, w[...]) for w in w_scr]
    o_ref[...] = epilogue(*z, *[e[...] for e in extra_refs]).astype(o_ref.dtype)


def _ep_scale(z, scale):
    return z * scale


def _ep_resid(z, resid):
    return resid + z


def _ep_swiglu(za, zb):
    return jax.nn.silu(za) * zb


def _ep_glu_resid(za, zb, resid):
    return resid + za * jax.nn.sigmoid(zb)


def _matmul(epilogue, a, b, layer, n_out, out_dtype, tm, tn, *, b_halves=1, row_extra=None, tile_extra=None,
            w_rows_are_outputs=False, src_block=lambda j: j):
    m, k = a.shape
    nj = n_out // tn
    b_mode = pl.Buffered(1) if 2 * b_halves * k * tn * 4 > VMEM_LIMIT // 3 else None
    in_specs = [pl.BlockSpec((tm, k), lambda j, i: (i, 0))]
    args = [a]
    for half in range(b_halves):
        if w_rows_are_outputs:
            spec = pl.BlockSpec((None, tn, k), functools.partial(lambda j, i, o: (layer, src_block(j) + o, 0), o=half * nj),
                                pipeline_mode=b_mode)
        else:
            spec = pl.BlockSpec((None, k, tn), functools.partial(lambda j, i, o: (layer, 0, src_block(j) + o), o=half * nj),
                                pipeline_mode=b_mode)
        in_specs.append(spec)
        args.append(b)
    extras = []
    if row_extra is not None:
        in_specs.append(pl.BlockSpec((1, tn), lambda j, i: (0, j)))
        extras.append(row_extra)
    if tile_extra is not None:
        in_specs.append(pl.BlockSpec((tm, tn), lambda j, i: (i, j)))
        extras.append(tile_extra)
    kern = functools.partial(_mm_kernel, n_b=b_halves, n_extra=len(extras), epilogue=epilogue,
                             w_rows_are_outputs=w_rows_are_outputs)
    return pl.pallas_call(
        kern,
        out_shape=jax.ShapeDtypeStruct((m, n_out), out_dtype),
        grid=(nj, m // tm),
        in_specs=in_specs,
        out_specs=pl.BlockSpec((tm, tn), lambda j, i: (i, j)),
        scratch_shapes=[pltpu.VMEM((tn, k) if w_rows_are_outputs else (k, tn), BF16) for _ in range(b_halves)],
        compiler_params=_params("arbitrary", "arbitrary"),
    )(*args, *extras)


def _swiglu_ffn(h, g, w_in, w_out, layer):
    hn = _rmsnorm(h, g, BF16)
    act = _matmul(_ep_swiglu, hn, w_in, layer, D_FF, BF16, 1024, 512, b_halves=2)
    return _matmul(_ep_resid, act, w_out, layer, D_MODEL, F32, 512, 512, tile_extra=h)


def _compress_kernel(x_ref, w1_ref, w2_ref, pos_ref, o_ref):
    ncp = x_ref.shape[0] // CMP_STRIDE
    top = jnp.zeros((ncp, CMP_HIDDEN), F32)
    bot = jnp.zeros((ncp, CMP_HIDDEN), F32)
    for r in range(CMP_STRIDE):
        x_r = x_ref[pl.ds(r, ncp, stride=CMP_STRIDE), :].astype(BF16)
        top = top + _dot(x_r, w1_ref[0, r * HEAD_DIM:(r + 1) * HEAD_DIM, :])
        bot = bot + _dot(x_r, w1_ref[0, (CMP_STRIDE + r) * HEAD_DIM:(CMP_STRIDE + r + 1) * HEAD_DIM, :])
    posb = _dot(pos_ref[0], w1_ref[0])[0:1]
    hid = top + pltpu.roll(bot, ncp - 1, 0) + posb
    o_ref[0, 0] = _dot(jax.nn.gelu(hid).astype(BF16), w2_ref[0]).astype(o_ref.dtype)


def _compress(kcvc, w1, w2, pos):
    s = kcvc.shape[0]
    ncp = s // CMP_STRIDE
    return pl.pallas_call(
        _compress_kernel,
        out_shape=jax.ShapeDtypeStruct((2, N_KV_GROUPS, ncp, HEAD_DIM), BF16),
        grid=(2, N_KV_GROUPS),
        in_specs=[
            pl.BlockSpec((s, HEAD_DIM), lambda kv, gi: (0, kv * N_KV_GROUPS + gi)),
            pl.BlockSpec((1, CMP_BLOCK * HEAD_DIM, CMP_HIDDEN), lambda kv, gi: (kv, 0, 0)),
            pl.BlockSpec((1, CMP_HIDDEN, HEAD_DIM), lambda kv, gi: (kv, 0, 0)),
            pl.BlockSpec((1, SUBLANES, CMP_BLOCK * HEAD_DIM), lambda kv, gi: (kv, 0, 0)),
        ],
        out_specs=pl.BlockSpec((1, 1, ncp, HEAD_DIM), lambda kv, gi: (kv, gi, 0, 0)),
        compiler_params=_params("parallel", "parallel"),
    )(kcvc, w1, w2, pos)


def _cmp_attn_kernel(q_ref, kcc_ref, vcct_ref, band_ref, mt_ref, oc_ref, nm_ref, s_scr, *, tq, nb):
    i = pl.program_id(1)
    band_start = pl.multiple_of(i * (tq // CMP_STRIDE), SUBLANES)
    n_rows = kcc_ref.shape[1]
    n_blk = mt_ref.shape[0]

    def body(n_r, n_b):
        kcc = kcc_ref[0, :n_r, :]
        vcct = vcct_ref[0, :, :n_r]
        rows = lax.broadcasted_iota(jnp.int32, (n_r, tq), 0)
        row_bias = jnp.where(rows >= CMP_PAD, jnp.where(rows < band_start + nb, 0.0, NEG), NEG)
        imp = jnp.zeros((n_r, tq), F32)

        def scores(h):
            qh = q_ref[:, h * HEAD_DIM:(h + 1) * HEAD_DIM]
            s_scr[h, :n_r, :] = _dot_nt(kcc, qh) + row_bias
            s_scr[h, pl.ds(band_start, nb), :] += band_ref[h]

        scores(0)
        for h in range(HPG):
            if h + 1 < HPG:
                scores(h + 1)
            s = s_scr[h, :n_r, :]
            m = jnp.maximum(jnp.max(s, axis=0, keepdims=True), 0.5 * NEG)
            e = jnp.exp2(s - m)
            l = jnp.sum(e, axis=0, keepdims=True)
            pn = e * (1.0 / jnp.where(l > 0.0, l, 1.0))
            imp = imp + pn
            oct_h = _dot(vcct, pn.astype(BF16))
            oc_ref[:, h * HEAD_DIM:(h + 1) * HEAD_DIM] = oct_h.T
        mt = mt_ref[:n_b, :n_r]
        hi = imp.astype(BF16)
        rest = imp - hi.astype(F32)
        mid = rest.astype(BF16)
        lo = (rest - mid.astype(F32)).astype(BF16)
        slc = _dot(mt, hi) + _dot(mt, mid) + _dot(mt, lo)
        jj = lax.broadcasted_iota(jnp.int32, (n_b, tq), 0)
        t_blk = (i * tq + lax.broadcasted_iota(jnp.int32, (n_b, tq), 1)) // SEL_BLOCK
        forced = jnp.where(jj == 0, 1, jnp.where(jj == t_blk, 1, jnp.where(jj == t_blk - 1, 1, 0)))
        picked = jnp.where(forced == 1, 1.0, 0.0)
        score = jnp.where(forced == 1, -jnp.inf, jnp.where(jj <= t_blk, slc, NEG))
        for _ in range(SEL_TOPK - N_FORCED):
            best = jnp.max(score, axis=0, keepdims=True)
            first = jnp.min(jnp.where(score == best, jj, n_blk), axis=0, keepdims=True)
            hit = jj == first
            picked = jnp.where(hit, 1.0, picked)
            score = jnp.where(hit, -jnp.inf, score)
        neg_mask = jnp.where(picked > 0.5, jnp.where(jj <= t_blk, 0.0, NEG), NEG)
        if n_b < n_blk:
            neg_mask = jnp.concatenate([neg_mask, jnp.full((n_blk - n_b, tq), NEG, F32)], axis=0)
        nm_ref[...] = neg_mask.T.astype(nm_ref.dtype)

    step = LANES
    extents = list(range(step, n_rows, step)) + [n_rows]
    need_rows = band_start + nb
    for v, n_r in enumerate(extents):
        lower = extents[v - 1] if v else 0
        n_b = min(n_blk, -(-(n_r * CMP_STRIDE // SEL_BLOCK) // SUBLANES) * SUBLANES)
        pl.when(jnp.logical_and(need_rows > lower, need_rows <= n_r))(functools.partial(body, n_r, n_b))


def _cmp_attn(proj, kcc, vcct, band, mt, tq):
    s = proj.shape[0]
    n_rows = kcc.shape[1]
    nb = band.shape[1]
    kern = functools.partial(_cmp_attn_kernel, tq=tq, nb=nb)
    return pl.pallas_call(
        kern,
        out_shape=(jax.ShapeDtypeStruct((s, Q_WIDTH), F32),
                   jax.ShapeDtypeStruct((s, N_KV_GROUPS * LANES), BF16)),
        grid=(N_KV_GROUPS, s // tq),
        in_specs=[
            pl.BlockSpec((tq, HPG * HEAD_DIM), lambda g, i: (i, g)),
            pl.BlockSpec((1, n_rows, HEAD_DIM), lambda g, i: (g, 0, 0)),
            pl.BlockSpec((1, HEAD_DIM, n_rows), lambda g, i: (g, 0, 0)),
            pl.BlockSpec((HPG, nb, tq), lambda g, i: (g, 0, 0)),
            pl.BlockSpec((LANES, n_rows), lambda g, i: (0, 0)),
        ],
        out_specs=(pl.BlockSpec((tq, HPG * HEAD_DIM), lambda g, i: (i, g)),
                   pl.BlockSpec((tq, LANES), lambda g, i: (i, g))),
        scratch_shapes=[pltpu.VMEM((HPG, n_rows, tq), F32)],
        compiler_params=_params("parallel", "parallel"),
    )(proj, kcc, vcct, band, mt)


def _lane_tile(x, width):
    return jnp.concatenate([x] * (width // x.shape[1]), axis=1)


def _softmax_update(s, v_aug, m_ref, acc_ref, rows):
    m_prev = m_ref[rows, :]
    m_next = jnp.maximum(m_prev, jnp.max(s, axis=1, keepdims=True))
    alpha = jnp.exp2(m_prev - m_next)
    p = jnp.exp2(s - _lane_tile(m_next, s.shape[1]))
    acc_ref[rows, :] = _lane_tile(alpha, acc_ref.shape[1]) * acc_ref[rows, :] + _dot(p.astype(BF16), v_aug)
    m_ref[rows, :] = m_next


def _softmax_once(s, v_aug):
    m = jnp.broadcast_to(jnp.max(s, axis=1, keepdims=True), (s.shape[0], LANES))
    p = jnp.exp2(s - _lane_tile(m, s.shape[1]))
    return _dot(p.astype(BF16), v_aug)


def _sel_win_kernel(q_ref, nm_ref, ks_ref, vs_ref, kw_ref, vw_ref, blk_ref, per_ref, g_ref, oc_ref, o_ref,
                    q4, m_s, a_s, bias_ref, *, tile):
    i = pl.program_id(1)
    t = tile

    @pl.when(i == 0)
    def _():
        tl = lax.broadcasted_iota(jnp.int32, (t, t), 0)
        kl = lax.broadcasted_iota(jnp.int32, (t, t), 1)
        edge = jnp.where(tl < kl, 0.0, NEG)
        for h in range(HPG):
            bias_ref[h, :, :t] = edge
            for which in range(2):
                rows = jnp.broadcast_to(per_ref[h, which:which + 1, :], (t, 2 * t))
                table = pltpu.roll(rows, 0, 1, stride=1, stride_axis=0)
                bias_ref[h, :, (which + 1) * t:(which + 2) * t] = table[:, :t]
    for h in range(HPG):
        q4[h * t:(h + 1) * t, :HEAD_DIM] = q_ref[:, h * HEAD_DIM:(h + 1) * HEAD_DIM]
        q4[h * t:(h + 1) * t, HEAD_DIM:] = nm_ref[...]
    m_s[...] = jnp.full(m_s.shape, -3e38, F32)
    a_s[...] = jnp.zeros(a_s.shape, F32)

    def sel_operands(row0, n_tiles):
        keys = pl.ds(row0, n_tiles * t)
        k_aug = jnp.concatenate([ks_ref[keys, :], blk_ref[keys, :]], axis=1)
        v_aug = jnp.concatenate([vs_ref[keys, :], jnp.ones((n_tiles * t, LANES), BF16)], axis=1)
        return k_aug, v_aug

    def sel_scores(h, k_aug, n_biased):
        s = _dot_nt(q4[h * t:(h + 1) * t, :], k_aug)
        plain = s.shape[1] - n_biased * t
        if n_biased:
            near = s[:, plain:] + bias_ref[h, :, (3 - n_biased) * t:]
            s = jnp.concatenate([s[:, :plain], near], axis=1) if plain else near
        return s

    def sel_pass(row0, n_tiles, n_biased):
        k_aug, v_aug = sel_operands(row0, n_tiles)

        def consume(h, s):
            _softmax_update(s, v_aug, m_s, a_s, slice(h * t, (h + 1) * t))

        return functools.partial(sel_scores, k_aug=k_aug, n_biased=n_biased), consume

    def win_pass(row0, n_tiles):
        keys = pl.ds(row0, n_tiles * t)
        v_aug = jnp.concatenate([vw_ref[keys, :], jnp.ones((n_tiles * t, LANES), BF16)], axis=1)

        def score(h):
            return _dot_nt(q4[h * t:(h + 1) * t, :HEAD_DIM], kw_ref[keys, :]) + bias_ref[h, :, (3 - n_tiles) * t:]

        def consume(h, s):
            acc_w = _softmax_once(s, v_aug)
            acc_s = a_s[h * t:(h + 1) * t, :]
            o_w = acc_w[:, :HEAD_DIM] * (1.0 / acc_w[:, HEAD_DIM:])
            o_s = acc_s[:, :HEAD_DIM] * (1.0 / acc_s[:, HEAD_DIM:])
            cols = slice(h * HEAD_DIM, (h + 1) * HEAD_DIM)
            gates = g_ref[...]
            o = (gates[:, h:h + 1] * oc_ref[:, cols] + gates[:, HPG + h:HPG + h + 1] * o_s
                 + gates[:, 2 * HPG + h:2 * HPG + h + 1] * o_w)
            o_ref[:, cols] = o.astype(o_ref.dtype)

        return score, consume

    def run(passes):
        work = [(score, consume, h0) for score, consume in passes for h0 in range(0, HPG, 2)]
        ahead = [work[0][0](h) for h in (0, 1)]
        for n, (_, consume, h0) in enumerate(work):
            now = ahead
            if n + 1 < len(work):
                nxt_score, _, nxt_h0 = work[n + 1]
                ahead = [nxt_score(h) for h in (nxt_h0, nxt_h0 + 1)]
            for h, s in zip((h0, h0 + 1), now):
                consume(h, s)

    n_far = jnp.maximum(i - 1, 0)

    n_chunks = n_far // FAR_TILES
    chunk = FAR_TILES * t

    def far_pair(c, carry):
        row0 = pl.multiple_of(c * 2 * chunk, 2 * chunk)
        run([sel_pass(row0, FAR_TILES, 0), sel_pass(row0 + chunk, FAR_TILES, 0)])
        return carry

    lax.fori_loop(0, n_chunks // 2, far_pair, 0)

    @pl.when(n_chunks % 2 == 1)
    def _():
        run([sel_pass(pl.multiple_of((n_chunks - 1) * chunk, chunk), FAR_TILES, 0)])

    for left in range(FAR_TILES):
        @pl.when(jnp.logical_and(i >= 2, n_far % FAR_TILES == left))
        def _(left=left):
            run([sel_pass(pl.multiple_of((i - 1 - left) * t, t), left + 2, 2),
                 win_pass(pl.multiple_of((i - 2) * t, t), 3)])

    @pl.when(i == 1)
    def _():
        run([sel_pass(0, 2, 2), win_pass(0, 2)])

    @pl.when(i == 0)
    def _():
        run([sel_pass(0, 1, 1), win_pass(0, 1)])


def _sel_win(proj, neg_mask, blk_onehot, periods, gates, o_c, tile):
    s = proj.shape[0]
    ks_blk = Q_WIDTH // HEAD_DIM
    vs_blk = (Q_WIDTH + KV_WIDTH) // HEAD_DIM
    kw_blk = (Q_WIDTH + 2 * KV_WIDTH) // HEAD_DIM
    vw_blk = (Q_WIDTH + 3 * KV_WIDTH) // HEAD_DIM
    once = pl.Buffered(1)
    rows4 = HPG * tile
    kern = functools.partial(_sel_win_kernel, tile=tile)
    return pl.pallas_call(
        kern,
        out_shape=jax.ShapeDtypeStruct((s, Q_WIDTH), BF16),
        grid=(N_KV_GROUPS, s // tile),
        in_specs=[
            pl.BlockSpec((tile, HPG * HEAD_DIM), lambda g, i: (i, g)),
            pl.BlockSpec((tile, LANES), lambda g, i: (i, g)),
            pl.BlockSpec((s, HEAD_DIM), lambda g, i: (0, ks_blk + g), pipeline_mode=once),
            pl.BlockSpec((s, HEAD_DIM), lambda g, i: (0, vs_blk + g), pipeline_mode=once),
            pl.BlockSpec((s, HEAD_DIM), lambda g, i: (0, kw_blk + g), pipeline_mode=once),
            pl.BlockSpec((s, HEAD_DIM), lambda g, i: (0, vw_blk + g), pipeline_mode=once),
            pl.BlockSpec((s, LANES), lambda g, i: (0, 0), pipeline_mode=once),
            pl.BlockSpec((HPG, 2, 2 * tile), lambda g, i: (g, 0, 0)),
            pl.BlockSpec((tile, LANES), lambda g, i: (i, g)),
            pl.BlockSpec((tile, HPG * HEAD_DIM), lambda g, i: (i, g)),
        ],
        out_specs=pl.BlockSpec((tile, HPG * HEAD_DIM), lambda g, i: (i, g)),
        scratch_shapes=[
            pltpu.VMEM((rows4, 2 * HEAD_DIM), BF16),
            pltpu.VMEM((rows4, LANES), F32),
            pltpu.VMEM((rows4, 2 * HEAD_DIM), F32),
            pltpu.VMEM((HPG, tile, 3 * tile), F32),
        ],
        compiler_params=_params("arbitrary", "arbitrary"),
    )(proj, neg_mask, proj, proj, proj, proj, blk_onehot, periods, gates, o_c)


def _rel_bucket_np(dist):
    n = np.maximum(dist, 0)
    max_exact = REL_BUCKETS // 2
    large = max_exact + (np.log(np.maximum(n, 1).astype(np.float32) / np.float32(max_exact))
                         / np.float32(math.log(REL_MAX_DIST / max_exact))
                         * np.float32(REL_BUCKETS - max_exact)).astype(np.int32)
    large = np.minimum(large, REL_BUCKETS - 1)
    return np.where(n < max_exact, n, large).astype(np.int32)


def _far_distance():
    d = np.arange(4 * REL_MAX_DIST)
    b = _rel_bucket_np(d)
    assert b[-1] == REL_BUCKETS - 1
    return int(np.max(np.nonzero(b != REL_BUCKETS - 1)[0])) + 1


def _bias_table(rel_bias, dist):
    bucket = jnp.asarray(_rel_bucket_np(dist))
    shifted = (rel_bias[bucket] - rel_bias[REL_BUCKETS - 1]) * LOG2E
    tab = jnp.where(jnp.asarray(dist >= 0)[..., None], shifted, NEG)
    return jnp.moveaxis(tab, -1, 0).astype(F32)


def _bias_period(rel_bias, offset, t):
    x = np.arange(2 * t)
    return _bias_table(rel_bias, np.where(x < t, offset - x, offset + 2 * t - x))


def _nsa_mixer(h, hn, rel_bias, w_in, w_out, layer, pos_k, w1_k, w2_k, pos_v, w1_v, w2_v):
    s = h.shape[0]
    n_sel = s // SEL_BLOCK
    ncp = s // CMP_STRIDE
    tile = ATT_TILE
    far = _far_distance()
    assert n_sel <= LANES and s % CMP_TQ == 0 and s % tile == 0
    assert far <= tile and far <= CMP_STRIDE * CMP_PAD - CMP_BLOCK + 1 + CMP_STRIDE

    col_scale = jnp.where(jnp.arange(PROJ_ATT) < Q_WIDTH, HEAD_DIM ** -0.5 * LOG2E, 1.0).astype(F32)[None, :]
    w_in_t = jnp.swapaxes(w_in, 1, 2)
    tn = KV_WIDTH
    q_blocks, cmp_blocks = Q_WIDTH // tn, 2 * KV_WIDTH // tn
    proj = _matmul(_ep_scale, hn, w_in_t, layer, PROJ_ATT, BF16, 1024, tn, row_extra=col_scale,
                   w_rows_are_outputs=True, src_block=lambda j: jnp.where(j < q_blocks, j, j + cmp_blocks))
    kcvc = _matmul(lambda z: z, hn, w_in_t, layer, 2 * KV_WIDTH, F32, 1024, tn,
                   w_rows_are_outputs=True, src_block=lambda j: j + q_blocks)
    w_gate = w_in_t[layer, PROJ_MAIN:].reshape(N_GATES, N_KV_GROUPS, HPG, D_MODEL).transpose(1, 0, 2, 3)
    w_gate = w_gate.reshape(N_KV_GROUPS, N_GATES * HPG, D_MODEL)
    w_gate = jnp.pad(w_gate, ((0, 0), (0, LANES - N_GATES * HPG), (0, 0))).reshape(1, N_KV_GROUPS * LANES, D_MODEL)
    gates = _matmul(jax.nn.sigmoid, hn, w_gate, 0, N_KV_GROUPS * LANES, F32, 1024, 512, w_rows_are_outputs=True)

    w1 = jnp.stack([w1_k, w1_v]).astype(BF16)
    w2 = jnp.stack([w2_k, w2_v]).astype(BF16)
    pos = jnp.stack([pos_k.reshape(1, -1), pos_v.reshape(1, -1)]).astype(BF16)
    pos = jnp.broadcast_to(pos, (2, SUBLANES, CMP_BLOCK * HEAD_DIM))
    cc = _compress(kcvc, w1, w2, pos)
    kcc = jnp.pad(cc[0], ((0, 0), (CMP_PAD, 0), (0, 0)))
    vcct = jnp.pad(cc[1], ((0, 0), (CMP_PAD, 0), (0, 0))).transpose(0, 2, 1)
    n_rows = ncp + CMP_PAD

    nb = CMP_TQ // CMP_STRIDE + CMP_PAD
    first_dist = -CMP_STRIDE * (nb - 1 - CMP_PAD) - (CMP_BLOCK - 1)
    by_dist = _bias_table(rel_bias, first_dist + np.arange(CMP_TQ + CMP_STRIDE * (nb - 1)))
    band = jnp.stack([by_dist[:, CMP_STRIDE * (nb - 1 - b):CMP_STRIDE * (nb - 1 - b) + CMP_TQ] for b in range(nb)], axis=1)
    ratio = SEL_BLOCK // CMP_STRIDE
    lo = CMP_BLOCK // CMP_STRIDE - 1
    c_of_row = np.arange(n_rows)[None, :] - CMP_PAD
    j_of = np.arange(LANES)[:, None]
    mt = ((c_of_row >= ratio * j_of - lo) & (c_of_row <= ratio * j_of + ratio - 1)
          & (c_of_row >= 0) & (c_of_row <= ncp - 2) & (j_of < n_sel)).astype(np.float32)
    o_c, neg_mask = _cmp_attn(proj, kcc, vcct, band, jnp.asarray(mt, BF16), CMP_TQ)

    onehot = (np.arange(s)[:, None] // SEL_BLOCK == np.arange(LANES)[None, :]).astype(np.float32)
    periods = jnp.stack([_bias_period(rel_bias, tile, tile), _bias_period(rel_bias, 0, tile)], axis=1)
    o = _sel_win(proj, neg_mask, jnp.asarray(onehot, BF16), periods, gates, o_c, tile)
    return _matmul(_ep_resid, o, w_out, layer, D_MODEL, F32, 512, 1024, tile_extra=h)


def _s5_kernel(x_ref, lam_ref, bt_ref, c_ref, d_ref, y_ref, vt, yt, ys, tmask, *, chunk):
    L = chunk
    half = L // 2
    width = L * SSM_GROUP
    n_chunks = x_ref.shape[0] // L
    for tau in range(L):
        vt[tau] = x_ref[pl.ds(tau, n_chunks, stride=L), :].T

    @pl.when(pl.program_id(0) == 0)
    def _():
        dst = lax.broadcasted_iota(jnp.int32, (width, width), 0) // SSM_GROUP
        src = lax.broadcasted_iota(jnp.int32, (width, width), 1) // SSM_GROUP
        tmask[...] = jnp.where(src <= dst, 1.0, 0.0)

    def cmul(xr, xi, yr, yi):
        return xr * yr - xi * yi, xr * yi + xi * yr

    n2 = 2 * SSM_STATE
    sub = lax.broadcasted_iota(jnp.int32, (SUBLANES, 1), 0)
    consts = jnp.where(sub == 0, 1.0, jnp.where(sub == 1, float(half + 1), jnp.where(sub == 2, float(half - 1), float(L))))
    tau_col = lax.broadcasted_iota(jnp.int32, (L, 1), 0).astype(F32)
    row = lax.broadcasted_iota(jnp.int32, (n_chunks, n2), 0)
    conj = jnp.where(lax.broadcasted_iota(jnp.int32, (1, n2), 1) < SSM_STATE, 1.0, -1.0)

    def group(g):
        lam = lam_ref[g]
        a_re, a_im = lam[0:1], lam[1:2]
        dt = jnp.exp(lam[2:3])
        log_re, log_im = a_re * dt, a_im * dt

        def cpow(e):
            mag, cos, sin = jnp.exp(e * log_re), jnp.cos(e * log_im), jnp.sin(e * log_im)
            inv = 1.0 / mag
            return mag * cos, mag * sin, inv * cos, -(inv * sin)

        kr, ki, _, _ = cpow(consts)
        pr, pi = kr[0:1] - 1.0, ki[0:1]
        den = a_re * a_re + a_im * a_im
        cf_r = (pr * a_re + pi * a_im) / den
        cf_i = (pi * a_re - pr * a_im) / den
        bt_c, bt_s = bt_ref[g, 0], bt_ref[g, 1]
        bb_c = cf_r * bt_c + cf_i * bt_s
        bb_s = cf_r * bt_s - cf_i * bt_c
        c_c, c_s = c_ref[g, 0], c_ref[g, 1]

        e1r, e1i, e2r, e2i = cpow(tau_col - half)
        e3r, e3i = cmul(e1r, e1i, kr[1:2], ki[1:2])
        e4r, e4i = cmul(e2r, e2i, kr[2:3], ki[2:3])

        def outer(er, ei, w_c, w_s):
            return (er[:, None, :] * w_c[None, :, :] + ei[:, None, :] * w_s[None, :, :]).reshape(width, n2)

        qm = outer(e1r, e1i, c_c, c_s)
        km_conj = outer(e2r, e2i, bb_c * conj, bb_s * conj)
        qc = outer(e3r, e3i, c_c, c_s)
        wz = outer(e4r, e4i, bb_c, bb_s)

        chans = pl.ds(pl.multiple_of(g * SSM_GROUP, SSM_GROUP), SSM_GROUP)
        ub = vt[:, chans, :].reshape(width, n_chunks).astype(BF16)
        yield

        tt = _dot_nt(qm.astype(BF16), km_conj.astype(BF16))
        x = _dot(wz.T.astype(BF16), ub).T
        yield
        y = _dot((tt * tmask[...]).astype(BF16), ub)
        p_r, p_i = kr[3:4], ki[3:4]
        d = 1
        while d < n_chunks:
            s = jnp.where(row >= d, pltpu.roll(x, d, 0), 0.0)
            x = x + s * p_r - pltpu.roll(s, SSM_STATE, 1) * (p_i * conj)
            p_r, p_i = cmul(p_r, p_i, p_r, p_i)
            d *= 2
        prev_conj = jnp.where(row >= 1, pltpu.roll(x, 1, 0), 0.0) * conj
        yield
        y = y + _dot_nt(qc.astype(BF16), prev_conj.astype(BF16))
        yt[:, chans, :] = y.reshape(L, SSM_GROUP, n_chunks)

    def group_batch(k, carry):
        running = [group(S5_GROUPS_PER_TRIP * k + j) for j in range(S5_GROUPS_PER_TRIP)]
        while running:
            running = [gen for gen in running if next(gen, running) is not running]
        return carry

    lax.fori_loop(0, LANES // SSM_GROUP // S5_GROUPS_PER_TRIP, group_batch, 0)
    for tau in range(L):
        ys[pl.ds(tau, n_chunks, stride=L), :] = yt[tau].T
    y_ref[...] = jax.nn.gelu(ys[...] + d_ref[...] * x_ref[...]).astype(y_ref.dtype)


def _s5_mixer(h, hn, a_re, a_im, log_dt, b_re, b_im, c_re, c_im, d_skip, w_glu, layer):
    s = h.shape[0]
    L = S5_CHUNK
    n_chunks = s // L
    width = L * SSM_GROUP
    gpb = LANES // SSM_GROUP
    def packed(re, im):
        return jnp.stack([jnp.concatenate([re, im], -1), jnp.concatenate([-im, re], -1)], axis=1).astype(F32)

    lam = jnp.stack([a_re, a_im, jnp.broadcast_to(log_dt[:, None], a_re.shape)], axis=1).astype(F32)
    lam = jnp.concatenate([lam, lam], axis=-1)
    bt = packed(b_re.transpose(0, 2, 1), b_im.transpose(0, 2, 1))
    cc = packed(c_re, c_im)
    dd = d_skip.astype(F32).reshape(1, D_MODEL)
    y = pl.pallas_call(
        functools.partial(_s5_kernel, chunk=L),
        out_shape=jax.ShapeDtypeStruct((s, D_MODEL), BF16),
        grid=(SSM_GROUPS // gpb,),
        in_specs=[
            pl.BlockSpec((s, LANES), lambda b: (0, b)),
            pl.BlockSpec((gpb, 3, 2 * SSM_STATE), lambda b: (b, 0, 0)),
            pl.BlockSpec((gpb, 2, SSM_GROUP, 2 * SSM_STATE), lambda b: (b, 0, 0, 0)),
            pl.BlockSpec((gpb, 2, SSM_GROUP, 2 * SSM_STATE), lambda b: (b, 0, 0, 0)),
            pl.BlockSpec((1, LANES), lambda b: (0, b)),
        ],
        out_specs=pl.BlockSpec((s, LANES), lambda b: (0, b)),
        scratch_shapes=[
            pltpu.VMEM((L, LANES, n_chunks), F32),
            pltpu.VMEM((L, LANES, n_chunks), F32),
            pltpu.VMEM((s, LANES), F32),
            pltpu.VMEM((width, width), F32),
        ],
        compiler_params=_params("arbitrary"),
    )(hn, lam, bt, cc, dd)
    return _matmul(_ep_glu_resid, y, w_glu, layer, D_MODEL, F32, 1024, 512, b_halves=2, tile_extra=h)


def kernel(x, rel_bias, mix_norm_g, ffn_norm_g, final_norm_g, nsa_w_in, nsa_w_out, cmp_pos_k, cmp_w1_k, cmp_w2_k, cmp_pos_v, cmp_w1_v, cmp_w2_v, s5_A_re, s5_A_im, s5_log_dt, s5_B_re, s5_B_im, s5_C_re, s5_C_im, s5_D, s5_w_glu, ffn_w_in, ffn_w_out):
    assert x.shape[0] == 1
    h = x[0]
    hn = _rmsnorm(h, mix_norm_g[0], BF16)
    h = _nsa_mixer(h, hn, rel_bias, nsa_w_in, nsa_w_out, 0, cmp_pos_k[0], cmp_w1_k[0], cmp_w2_k[0],
                   cmp_pos_v[0], cmp_w1_v[0], cmp_w2_v[0])
    h = _swiglu_ffn(h, ffn_norm_g[0], ffn_w_in, ffn_w_out, 0)
    hn = _rmsnorm(h, mix_norm_g[1], F32)
    h = _s5_mixer(h, hn, s5_A_re[0], s5_A_im[0], s5_log_dt[0], s5_B_re[0], s5_B_im[0], s5_C_re[0], s5_C_im[0],
                  s5_D[0], s5_w_glu, 0)
    h = _swiglu_ffn(h, ffn_norm_g[1], ffn_w_in, ffn_w_out, 1)
    return _rmsnorm(h, final_norm_g, x.dtype)[None]
```

```python
import functools
import math

import numpy as np
import jax
import jax.numpy as jnp
from jax import lax
from jax.experimental import pallas as pl
from jax.experimental.pallas import tpu as pltpu

D_MODEL = 2048
N_HEADS = 16
HEAD_DIM = 128
N_KV_GROUPS = 4
HPG = N_HEADS // N_KV_GROUPS
CMP_BLOCK = 32
CMP_STRIDE = 16
CMP_HIDDEN = 2 * HEAD_DIM
SEL_BLOCK = 64
SEL_TOPK = 16
N_FORCED = 3
WINDOW = 512
N_GATES = 3
KV_WIDTH = N_KV_GROUPS * HEAD_DIM
Q_WIDTH = N_HEADS * HEAD_DIM
PROJ_MAIN = Q_WIDTH + 6 * KV_WIDTH
PROJ_ATT = Q_WIDTH + 4 * KV_WIDTH
REL_BUCKETS = 32
REL_MAX_DIST = 128
SSM_GROUP = 16
SSM_GROUPS = D_MODEL // SSM_GROUP
SSM_STATE = 64
D_FF = ((8 * D_MODEL + 2) // 3 + 255) // 256 * 256
RMS_EPS = 1e-6
NEG = -1e30
FORCE_SCORE = 1e9
LOG2E = math.log2(math.e)

LANES = 128
SUBLANES = 8
VMEM_LIMIT = 52 * 1024 * 1024

ATT_TILE = WINDOW // 2
FAR_TILES = 4
Q_TILES = 2
CMP_TQ = 256
CMP_PAD = 8
S5_CHUNK = 32
S5_GROUPS_PER_TRIP = 4

BF16 = jnp.bfloat16
F32 = jnp.float32


def _dot(a, b):
    return jnp.dot(a, b, preferred_element_type=F32)


def _dot_nt(a, b):
    return lax.dot_general(a, b, (((1,), (1,)), ((), ())), preferred_element_type=F32)


def _params(*sem):
    return pltpu.CompilerParams(dimension_semantics=sem, vmem_limit_bytes=VMEM_LIMIT)


def _rmsnorm_kernel(x_ref, g_ref, o_ref):
    x = x_ref[...]
    ms = jnp.mean(x * x, axis=-1, keepdims=True)
    o_ref[...] = (x * lax.rsqrt(ms + RMS_EPS) * g_ref[...]).astype(o_ref.dtype)


def _rmsnorm(x, g, out_dtype, tm=512):
    s, d = x.shape
    return pl.pallas_call(
        _rmsnorm_kernel,
        out_shape=jax.ShapeDtypeStruct((s, d), out_dtype),
        grid=(s // tm,),
        in_specs=[pl.BlockSpec((tm, d), lambda i: (i, 0)), pl.BlockSpec((1, d), lambda i: (0, 0))],
        out_specs=pl.BlockSpec((tm, d), lambda i: (i, 0)),
        compiler_params=_params("parallel"),
    )(x, g.reshape(1, d))


def _mm_kernel(*refs, n_b, n_extra, epilogue, w_rows_are_outputs):
    a_ref = refs[0]
    b_refs = refs[1:1 + n_b]
    extra_refs = refs[1 + n_b:1 + n_b + n_extra]
    o_ref = refs[1 + n_b + n_extra]
    w_scr = refs[2 + n_b + n_extra:]

    @pl.when(pl.program_id(1) == 0)
    def _():
        for b_ref, w in zip(b_refs, w_scr):
            w[...] = b_ref[...].astype(BF16)

    a = a_ref[...]
    z = [(_dot_nt if w_rows_are_outputs else _dot)(a, w[...]) for w in w_scr]
    o_ref[...] = epilogue(*z, *[e[...] for e in extra_refs]).astype(o_ref.dtype)


def _ep_scale(z, scale):
    return z * scale


def _ep_resid(z, resid):
    return resid + z


def _ep_swiglu(za, zb):
    return jax.nn.silu(za) * zb


def _ep_glu_resid(za, zb, resid):
    return resid + za * jax.nn.sigmoid(zb)


def _matmul(epilogue, a, b, layer, n_out, out_dtype, tm, tn, *, b_halves=1, row_extra=None, tile_extra=None,
            w_rows_are_outputs=False, src_block=lambda j: j):
    m, k = a.shape
    nj = n_out // tn
    estimate = (b_halves * k * tn * (2 * 4 + 2) + 2 * tm * k * 2 + 4 * tm * tn * 4 + b_halves * tm * tn * 4)
    b_mode = pl.Buffered(1) if estimate > VMEM_LIMIT else None
    in_specs = [pl.BlockSpec((tm, k), lambda j, i: (i, 0))]
    args = [a]
    for half in range(b_halves):
        if w_rows_are_outputs:
            spec = pl.BlockSpec((None, tn, k), functools.partial(lambda j, i, o: (layer, src_block(j) + o, 0), o=half * nj),
                                pipeline_mode=b_mode)
        else:
            spec = pl.BlockSpec((None, k, tn), functools.partial(lambda j, i, o: (layer, 0, src_block(j) + o), o=half * nj),
                                pipeline_mode=b_mode)
        in_specs.append(spec)
        args.append(b)
    extras = []
    if row_extra is not None:
        in_specs.append(pl.BlockSpec((1, tn), lambda j, i: (0, j)))
        extras.append(row_extra)
    if tile_extra is not None:
        in_specs.append(pl.BlockSpec((tm, tn), lambda j, i: (i, j)))
        extras.append(tile_extra)
    kern = functools.partial(_mm_kernel, n_b=b_halves, n_extra=len(extras), epilogue=epilogue,
                             w_rows_are_outputs=w_rows_are_outputs)
    return pl.pallas_call(
        kern,
        out_shape=jax.ShapeDtypeStruct((m, n_out), out_dtype),
        grid=(nj, m // tm),
        in_specs=in_specs,
        out_specs=pl.BlockSpec((tm, tn), lambda j, i: (i, j)),
        scratch_shapes=[pltpu.VMEM((tn, k) if w_rows_are_outputs else (k, tn), BF16) for _ in range(b_halves)],
        compiler_params=_params("arbitrary", "arbitrary"),
    )(*args, *extras)


def _swiglu_ffn(h, g, w_in, w_out, layer):
    hn = _rmsnorm(h, g, BF16)
    act = _matmul(_ep_swiglu, hn, w_in, layer, D_FF, BF16, 1024, 512, b_halves=2)
    return _matmul(_ep_resid, act, w_out, layer, D_MODEL, F32, 512, 512, tile_extra=h)


def _compress_kernel(x_ref, w1_ref, w2_ref, pos_ref, o_ref):
    ncp = x_ref.shape[0] // CMP_STRIDE
    top = jnp.zeros((ncp, CMP_HIDDEN), F32)
    bot = jnp.zeros((ncp, CMP_HIDDEN), F32)
    for r in range(CMP_STRIDE):
        x_r = x_ref[pl.ds(r, ncp, stride=CMP_STRIDE), :].astype(BF16)
        top = top + _dot(x_r, w1_ref[0, r * HEAD_DIM:(r + 1) * HEAD_DIM, :])
        bot = bot + _dot(x_r, w1_ref[0, (CMP_STRIDE + r) * HEAD_DIM:(CMP_STRIDE + r + 1) * HEAD_DIM, :])
    posb = _dot(pos_ref[0], w1_ref[0])[0:1]
    hid = top + pltpu.roll(bot, ncp - 1, 0) + posb
    o_ref[0, 0] = _dot(jax.nn.gelu(hid).astype(BF16), w2_ref[0]).astype(o_ref.dtype)


def _compress(kcvc, w1, w2, pos):
    s = kcvc.shape[0]
    ncp = s // CMP_STRIDE
    return pl.pallas_call(
        _compress_kernel,
        out_shape=jax.ShapeDtypeStruct((2, N_KV_GROUPS, ncp, HEAD_DIM), BF16),
        grid=(2, N_KV_GROUPS),
        in_specs=[
            pl.BlockSpec((s, HEAD_DIM), lambda kv, gi: (0, kv * N_KV_GROUPS + gi)),
            pl.BlockSpec((1, CMP_BLOCK * HEAD_DIM, CMP_HIDDEN), lambda kv, gi: (kv, 0, 0)),
            pl.BlockSpec((1, CMP_HIDDEN, HEAD_DIM), lambda kv, gi: (kv, 0, 0)),
            pl.BlockSpec((1, SUBLANES, CMP_BLOCK * HEAD_DIM), lambda kv, gi: (kv, 0, 0)),
        ],
        out_specs=pl.BlockSpec((1, 1, ncp, HEAD_DIM), lambda kv, gi: (kv, gi, 0, 0)),
        compiler_params=_params("parallel", "parallel"),
    )(kcvc, w1, w2, pos)


def _cmp_attn_kernel(q_ref, kcc_ref, vcct_ref, band_ref, mt_ref, oc_ref, nm_ref, s_scr, *, tq, nb):
    i = pl.program_id(1)
    band_start = pl.multiple_of(i * (tq // CMP_STRIDE), SUBLANES)
    n_rows = kcc_ref.shape[1]
    n_blk = mt_ref.shape[0]

    def body(n_r, n_b):
        kcc = kcc_ref[0, :n_r, :]
        vcct = vcct_ref[0, :, :n_r]
        rows = lax.broadcasted_iota(jnp.int32, (n_r, tq), 0)
        row_bias = jnp.where(rows >= CMP_PAD, jnp.where(rows < band_start + nb, 0.0, NEG), NEG)
        imp = jnp.zeros((n_r, tq), F32)

        def scores(h):
            qh = q_ref[:, h * HEAD_DIM:(h + 1) * HEAD_DIM]
            s_scr[h, :n_r, :] = _dot_nt(kcc, qh) + row_bias
            s_scr[h, pl.ds(band_start, nb), :] += band_ref[h]

        scores(0)
        for h in range(HPG):
            if h + 1 < HPG:
                scores(h + 1)
            s = s_scr[h, :n_r, :]
            m = jnp.maximum(jnp.max(s, axis=0, keepdims=True), 0.5 * NEG)
            e = jnp.exp2(s - m)
            l = jnp.sum(e, axis=0, keepdims=True)
            pn = e * (1.0 / jnp.where(l > 0.0, l, 1.0))
            imp = imp + pn
            oct_h = _dot(vcct, pn.astype(BF16))
            oc_ref[:, h * HEAD_DIM:(h + 1) * HEAD_DIM] = oct_h.T
        mt = mt_ref[:n_b, :n_r]
        hi = imp.astype(BF16)
        rest = imp - hi.astype(F32)
        mid = rest.astype(BF16)
        lo = (rest - mid.astype(F32)).astype(BF16)
        slc = _dot(mt, hi) + _dot(mt, mid) + _dot(mt, lo)
        jj = lax.broadcasted_iota(jnp.int32, (n_b, tq), 0)
        t_blk = (i * tq + lax.broadcasted_iota(jnp.int32, (n_b, tq), 1)) // SEL_BLOCK
        forced = jnp.where(jj == 0, 1, jnp.where(jj == t_blk, 1, jnp.where(jj == t_blk - 1, 1, 0)))
        picked = jnp.where(forced == 1, 1.0, 0.0)
        score = jnp.where(forced == 1, -jnp.inf, jnp.where(jj <= t_blk, slc, NEG))
        for _ in range(SEL_TOPK - N_FORCED):
            best = jnp.max(score, axis=0, keepdims=True)
            first = jnp.min(jnp.where(score == best, jj, n_blk), axis=0, keepdims=True)
            hit = jj == first
            picked = jnp.where(hit, 1.0, picked)
            score = jnp.where(hit, -jnp.inf, score)
        neg_mask = jnp.where(picked > 0.5, jnp.where(jj <= t_blk, 0.0, NEG), NEG)
        if n_b < n_blk:
            neg_mask = jnp.concatenate([neg_mask, jnp.full((n_blk - n_b, tq), NEG, F32)], axis=0)
        nm_ref[...] = neg_mask.T.astype(nm_ref.dtype)

    step = LANES
    extents = list(range(step, n_rows, step)) + [n_rows]
    need_rows = band_start + nb
    for v, n_r in enumerate(extents):
        lower = extents[v - 1] if v else 0
        n_b = min(n_blk, -(-(n_r * CMP_STRIDE // SEL_BLOCK) // SUBLANES) * SUBLANES)
        pl.when(jnp.logical_and(need_rows > lower, need_rows <= n_r))(functools.partial(body, n_r, n_b))


def _cmp_attn(proj, kcc, vcct, band, mt, tq):
    s = proj.shape[0]
    n_rows = kcc.shape[1]
    nb = band.shape[1]
    kern = functools.partial(_cmp_attn_kernel, tq=tq, nb=nb)
    return pl.pallas_call(
        kern,
        out_shape=(jax.ShapeDtypeStruct((s, Q_WIDTH), F32),
                   jax.ShapeDtypeStruct((s, N_KV_GROUPS * LANES), BF16)),
        grid=(N_KV_GROUPS, s // tq),
        in_specs=[
            pl.BlockSpec((tq, HPG * HEAD_DIM), lambda g, i: (i, g)),
            pl.BlockSpec((1, n_rows, HEAD_DIM), lambda g, i: (g, 0, 0)),
            pl.BlockSpec((1, HEAD_DIM, n_rows), lambda g, i: (g, 0, 0)),
            pl.BlockSpec((HPG, nb, tq), lambda g, i: (g, 0, 0)),
            pl.BlockSpec((LANES, n_rows), lambda g, i: (0, 0)),
        ],
        out_specs=(pl.BlockSpec((tq, HPG * HEAD_DIM), lambda g, i: (i, g)),
                   pl.BlockSpec((tq, LANES), lambda g, i: (i, g))),
        scratch_shapes=[pltpu.VMEM((HPG, n_rows, tq), F32)],
        compiler_params=_params("parallel", "parallel"),
    )(proj, kcc, vcct, band, mt)


def _lane_tile(x, width):
    return jnp.concatenate([x] * (width // x.shape[1]), axis=1)


def _softmax_update(s, v_aug, m_ref, acc_ref, rows):
    m_prev = m_ref[rows, :]
    m_next = jnp.maximum(m_prev, jnp.max(s, axis=1, keepdims=True))
    alpha = jnp.exp2(m_prev - m_next)
    p = jnp.exp2(s - _lane_tile(m_next, s.shape[1]))
    acc_ref[rows, :] = _lane_tile(alpha, acc_ref.shape[1]) * acc_ref[rows, :] + _dot(p.astype(BF16), v_aug)
    m_ref[rows, :] = m_next


def _softmax_once(s, v_aug):
    m = jnp.broadcast_to(jnp.max(s, axis=1, keepdims=True), (s.shape[0], LANES))
    p = jnp.exp2(s - _lane_tile(m, s.shape[1]))
    return _dot(p.astype(BF16), v_aug)


def _sel_win_kernel(q_ref, nm_ref, ks_ref, vs_ref, kw_ref, vw_ref, blk_ref, per_ref, g_ref, oc_ref, o_ref,
                    q4, m_s, a_s, bias_ref, *, tile):
    step = pl.program_id(1)
    t = tile
    first = step * Q_TILES

    @pl.when(step == 0)
    def _():
        tl = lax.broadcasted_iota(jnp.int32, (t, t), 0)
        kl = lax.broadcasted_iota(jnp.int32, (t, t), 1)
        edge = jnp.where(tl < kl, 0.0, NEG)
        for h in range(HPG):
            bias_ref[h, :, :t] = edge
            for which in range(2):
                rows = jnp.broadcast_to(per_ref[h, which:which + 1, :], (t, 2 * t))
                table = pltpu.roll(rows, 0, 1, stride=1, stride_axis=0)
                bias_ref[h, :, (which + 1) * t:(which + 2) * t] = table[:, :t]
    def stacked(qt, h):
        return slice((qt * HPG + h) * t, (qt * HPG + h + 1) * t)

    def tile_rows(qt):
        return slice(qt * t, (qt + 1) * t)

    for qt in range(Q_TILES):
        for h in range(HPG):
            q4[stacked(qt, h), :HEAD_DIM] = q_ref[tile_rows(qt), h * HEAD_DIM:(h + 1) * HEAD_DIM]
            q4[stacked(qt, h), HEAD_DIM:] = nm_ref[tile_rows(qt), :]
    m_s[...] = jnp.full(m_s.shape, -3e38, F32)
    a_s[...] = jnp.zeros(a_s.shape, F32)

    def sel_operands(row0, n_tiles):
        keys = pl.ds(row0, n_tiles * t)
        k_aug = jnp.concatenate([ks_ref[keys, :], blk_ref[keys, :]], axis=1)
        v_aug = jnp.concatenate([vs_ref[keys, :], jnp.ones((n_tiles * t, LANES), BF16)], axis=1)
        return k_aug, v_aug

    def sel_pass(qt, operands, n_biased):
        k_aug, v_aug = operands

        def score(h):
            s = _dot_nt(q4[stacked(qt, h), :], k_aug)
            plain = s.shape[1] - n_biased * t
            if n_biased:
                near = s[:, plain:] + bias_ref[h, :, (3 - n_biased) * t:]
                s = jnp.concatenate([s[:, :plain], near], axis=1) if plain else near
            return s

        def consume(h, s):
            _softmax_update(s, v_aug, m_s, a_s, stacked(qt, h))

        return score, consume

    def win_pass(qt, row0, n_tiles):
        keys = pl.ds(row0, n_tiles * t)
        v_aug = jnp.concatenate([vw_ref[keys, :], jnp.ones((n_tiles * t, LANES), BF16)], axis=1)

        def score(h):
            return _dot_nt(q4[stacked(qt, h), :HEAD_DIM], kw_ref[keys, :]) + bias_ref[h, :, (3 - n_tiles) * t:]

        def consume(h, s):
            acc_w = _softmax_once(s, v_aug)
            acc_s = a_s[stacked(qt, h), :]
            o_w = acc_w[:, :HEAD_DIM] * (1.0 / acc_w[:, HEAD_DIM:])
            o_s = acc_s[:, :HEAD_DIM] * (1.0 / acc_s[:, HEAD_DIM:])
            cols = slice(h * HEAD_DIM, (h + 1) * HEAD_DIM)
            gates = g_ref[tile_rows(qt), :]
            o = (gates[:, h:h + 1] * oc_ref[tile_rows(qt), cols] + gates[:, HPG + h:HPG + h + 1] * o_s
                 + gates[:, 2 * HPG + h:2 * HPG + h + 1] * o_w)
            o_ref[tile_rows(qt), cols] = o.astype(o_ref.dtype)

        return score, consume

    def run(passes):
        work = [(score, consume, h0) for score, consume in passes for h0 in range(0, HPG, 2)]
        ahead = [work[0][0](h) for h in (0, 1)]
        for n, (_, consume, h0) in enumerate(work):
            now = ahead
            if n + 1 < len(work):
                nxt_score, _, nxt_h0 = work[n + 1]
                ahead = [nxt_score(h) for h in (nxt_h0, nxt_h0 + 1)]
            for h, s in zip((h0, h0 + 1), now):
                consume(h, s)

    def far_passes(row0):
        operands = sel_operands(row0, FAR_TILES)
        return [sel_pass(qt, operands, 0) for qt in range(Q_TILES)]

    n_far = jnp.maximum(first - 1, 0)
    n_chunks = n_far // FAR_TILES
    chunk = FAR_TILES * t

    def far_pair(c, carry):
        row0 = pl.multiple_of(c * 2 * chunk, 2 * chunk)
        run(far_passes(row0) + far_passes(row0 + chunk))
        return carry

    lax.fori_loop(0, n_chunks // 2, far_pair, 0)

    @pl.when(n_chunks % 2 == 1)
    def _():
        run(far_passes(pl.multiple_of((n_chunks - 1) * chunk, chunk)))

    def tail(left):
        row0 = pl.multiple_of((n_far - left) * t, t)
        passes = []
        for qt in range(Q_TILES):
            n_sel = left + 2 + qt
            passes.append(sel_pass(qt, sel_operands(row0, n_sel), 2))
            passes.append(win_pass(qt, pl.multiple_of((first + qt - 2) * t, t), 3))
        run(passes)

    for left in range(FAR_TILES):
        if (left + 1) % Q_TILES == 0:
            pl.when(jnp.logical_and(step >= 1, n_far % FAR_TILES == left))(functools.partial(tail, left))

    @pl.when(step == 0)
    def _():
        passes = []
        for qt in range(Q_TILES):
            n_keys = min(qt + 1, 3)
            passes.append(sel_pass(qt, sel_operands(0, qt + 1), min(qt + 1, 2)))
            passes.append(win_pass(qt, (qt + 1 - n_keys) * t, n_keys))
        run(passes)


def _sel_win(proj, neg_mask, blk_onehot, periods, gates, o_c, tile):
    s = proj.shape[0]
    ks_blk = Q_WIDTH // HEAD_DIM
    vs_blk = (Q_WIDTH + KV_WIDTH) // HEAD_DIM
    kw_blk = (Q_WIDTH + 2 * KV_WIDTH) // HEAD_DIM
    vw_blk = (Q_WIDTH + 3 * KV_WIDTH) // HEAD_DIM
    once = pl.Buffered(1)
    rows = Q_TILES * tile
    stacked_rows = HPG * rows
    kern = functools.partial(_sel_win_kernel, tile=tile)
    return pl.pallas_call(
        kern,
        out_shape=jax.ShapeDtypeStruct((s, Q_WIDTH), BF16),
        grid=(N_KV_GROUPS, s // rows),
        in_specs=[
            pl.BlockSpec((rows, HPG * HEAD_DIM), lambda g, i: (i, g)),
            pl.BlockSpec((rows, LANES), lambda g, i: (i, g)),
            pl.BlockSpec((s, HEAD_DIM), lambda g, i: (0, ks_blk + g), pipeline_mode=once),
            pl.BlockSpec((s, HEAD_DIM), lambda g, i: (0, vs_blk + g), pipeline_mode=once),
            pl.BlockSpec((s, HEAD_DIM), lambda g, i: (0, kw_blk + g), pipeline_mode=once),
            pl.BlockSpec((s, HEAD_DIM), lambda g, i: (0, vw_blk + g), pipeline_mode=once),
            pl.BlockSpec((s, LANES), lambda g, i: (0, 0), pipeline_mode=once),
            pl.BlockSpec((HPG, 2, 2 * tile), lambda g, i: (g, 0, 0)),
            pl.BlockSpec((rows, LANES), lambda g, i: (i, g)),
            pl.BlockSpec((rows, HPG * HEAD_DIM), lambda g, i: (i, g)),
        ],
        out_specs=pl.BlockSpec((rows, HPG * HEAD_DIM), lambda g, i: (i, g)),
        scratch_shapes=[
            pltpu.VMEM((stacked_rows, 2 * HEAD_DIM), BF16),
            pltpu.VMEM((stacked_rows, LANES), F32),
            pltpu.VMEM((stacked_rows, 2 * HEAD_DIM), F32),
            pltpu.VMEM((HPG, tile, 3 * tile), F32),
        ],
        compiler_params=_params("arbitrary", "arbitrary"),
    )(proj, neg_mask, proj, proj, proj, proj, blk_onehot, periods, gates, o_c)


def _rel_bucket_np(dist):
    n = np.maximum(dist, 0)
    max_exact = REL_BUCKETS // 2
    large = max_exact + (np.log(np.maximum(n, 1).astype(np.float32) / np.float32(max_exact))
                         / np.float32(math.log(REL_MAX_DIST / max_exact))
                         * np.float32(REL_BUCKETS - max_exact)).astype(np.int32)
    large = np.minimum(large, REL_BUCKETS - 1)
    return np.where(n < max_exact, n, large).astype(np.int32)


def _far_distance():
    d = np.arange(4 * REL_MAX_DIST)
    b = _rel_bucket_np(d)
    assert b[-1] == REL_BUCKETS - 1
    return int(np.max(np.nonzero(b != REL_BUCKETS - 1)[0])) + 1


def _bias_table(rel_bias, dist):
    bucket = jnp.asarray(_rel_bucket_np(dist))
    shifted = (rel_bias[bucket] - rel_bias[REL_BUCKETS - 1]) * LOG2E
    tab = jnp.where(jnp.asarray(dist >= 0)[..., None], shifted, NEG)
    return jnp.moveaxis(tab, -1, 0).astype(F32)


def _bias_period(rel_bias, offset, t):
    x = np.arange(2 * t)
    return _bias_table(rel_bias, np.where(x < t, offset - x, offset + 2 * t - x))


def _nsa_mixer(h, hn, rel_bias, w_in, w_out, layer, pos_k, w1_k, w2_k, pos_v, w1_v, w2_v):
    s = h.shape[0]
    n_sel = s // SEL_BLOCK
    ncp = s // CMP_STRIDE
    tile = ATT_TILE
    far = _far_distance()
    assert n_sel <= LANES and s % CMP_TQ == 0 and s % (Q_TILES * tile) == 0 and Q_TILES <= 2
    assert far <= tile and far <= CMP_STRIDE * CMP_PAD - CMP_BLOCK + 1 + CMP_STRIDE

    col_scale = jnp.where(jnp.arange(PROJ_ATT) < Q_WIDTH, HEAD_DIM ** -0.5 * LOG2E, 1.0).astype(F32)[None, :]
    w_in_t = jnp.swapaxes(w_in, 1, 2)
    tn = KV_WIDTH
    q_blocks, cmp_blocks = Q_WIDTH // tn, 2 * KV_WIDTH // tn
    proj = _matmul(_ep_scale, hn, w_in_t, layer, PROJ_ATT, BF16, 1024, tn, row_extra=col_scale,
                   w_rows_are_outputs=True, src_block=lambda j: jnp.where(j < q_blocks, j, j + cmp_blocks))
    kcvc = _matmul(lambda z: z, hn, w_in_t, layer, 2 * KV_WIDTH, F32, 1024, tn,
                   w_rows_are_outputs=True, src_block=lambda j: j + q_blocks)
    w_gate = w_in_t[layer, PROJ_MAIN:].reshape(N_GATES, N_KV_GROUPS, HPG, D_MODEL).transpose(1, 0, 2, 3)
    w_gate = w_gate.reshape(N_KV_GROUPS, N_GATES * HPG, D_MODEL)
    w_gate = jnp.pad(w_gate, ((0, 0), (0, LANES - N_GATES * HPG), (0, 0))).reshape(1, N_KV_GROUPS * LANES, D_MODEL)
    gates = _matmul(jax.nn.sigmoid, hn, w_gate, 0, N_KV_GROUPS * LANES, F32, 1024, 512, w_rows_are_outputs=True)

    w1 = jnp.stack([w1_k, w1_v]).astype(BF16)
    w2 = jnp.stack([w2_k, w2_v]).astype(BF16)
    pos = jnp.stack([pos_k.reshape(1, -1), pos_v.reshape(1, -1)]).astype(BF16)
    pos = jnp.broadcast_to(pos, (2, SUBLANES, CMP_BLOCK * HEAD_DIM))
    cc = _compress(kcvc, w1, w2, pos)
    kcc = jnp.pad(cc[0], ((0, 0), (CMP_PAD, 0), (0, 0)))
    vcct = jnp.pad(cc[1], ((0, 0), (CMP_PAD, 0), (0, 0))).transpose(0, 2, 1)
    n_rows = ncp + CMP_PAD

    nb = CMP_TQ // CMP_STRIDE + CMP_PAD
    first_dist = -CMP_STRIDE * (nb - 1 - CMP_PAD) - (CMP_BLOCK - 1)
    by_dist = _bias_table(rel_bias, first_dist + np.arange(CMP_TQ + CMP_STRIDE * (nb - 1)))
    band = jnp.stack([by_dist[:, CMP_STRIDE * (nb - 1 - b):CMP_STRIDE * (nb - 1 - b) + CMP_TQ] for b in range(nb)], axis=1)
    ratio = SEL_BLOCK // CMP_STRIDE
    lo = CMP_BLOCK // CMP_STRIDE - 1
    c_of_row = np.arange(n_rows)[None, :] - CMP_PAD
    j_of = np.arange(LANES)[:, None]
    mt = ((c_of_row >= ratio * j_of - lo) & (c_of_row <= ratio * j_of + ratio - 1)
          & (c_of_row >= 0) & (c_of_row <= ncp - 2) & (j_of < n_sel)).astype(np.float32)
    o_c, neg_mask = _cmp_attn(proj, kcc, vcct, band, jnp.asarray(mt, BF16), CMP_TQ)

    onehot = (np.arange(s)[:, None] // SEL_BLOCK == np.arange(LANES)[None, :]).astype(np.float32)
    periods = jnp.stack([_bias_period(rel_bias, tile, tile), _bias_period(rel_bias, 0, tile)], axis=1)
    o = _sel_win(proj, neg_mask, jnp.asarray(onehot, BF16), periods, gates, o_c, tile)
    return _matmul(_ep_resid, o, w_out, layer, D_MODEL, F32, 512, 1024, tile_extra=h)


def _s5_kernel(x_ref, lam_ref, bt_ref, c_ref, d_ref, y_ref, vt, yt, ys, tmask, *, chunk):
    L = chunk
    half = L // 2
    width = L * SSM_GROUP
    n_chunks = x_ref.shape[0] // L
    for tau in range(L):
        vt[tau] = x_ref[pl.ds(tau, n_chunks, stride=L), :].T

    @pl.when(pl.program_id(0) == 0)
    def _():
        dst = lax.broadcasted_iota(jnp.int32, (width, width), 0) // SSM_GROUP
        src = lax.broadcasted_iota(jnp.int32, (width, width), 1) // SSM_GROUP
        tmask[...] = jnp.where(src <= dst, 1.0, 0.0)

    def cmul(xr, xi, yr, yi):
        return xr * yr - xi * yi, xr * yi + xi * yr

    n2 = 2 * SSM_STATE
    sub = lax.broadcasted_iota(jnp.int32, (SUBLANES, 1), 0)
    consts = jnp.where(sub == 0, 1.0, jnp.where(sub == 1, float(half + 1), jnp.where(sub == 2, float(half - 1), float(L))))
    tau_col = lax.broadcasted_iota(jnp.int32, (L, 1), 0).astype(F32)
    row = lax.broadcasted_iota(jnp.int32, (n_chunks, n2), 0)
    conj = jnp.where(lax.broadcasted_iota(jnp.int32, (1, n2), 1) < SSM_STATE, 1.0, -1.0)

    def group(g):
        lam = lam_ref[g]
        a_re, a_im = lam[0:1], lam[1:2]
        dt = jnp.exp(lam[2:3])
        log_re, log_im = a_re * dt, a_im * dt

        def cpow(e):
            mag, cos, sin = jnp.exp(e * log_re), jnp.cos(e * log_im), jnp.sin(e * log_im)
            inv = 1.0 / mag
            return mag * cos, mag * sin, inv * cos, -(inv * sin)

        kr, ki, _, _ = cpow(consts)
        pr, pi = kr[0:1] - 1.0, ki[0:1]
        den = a_re * a_re + a_im * a_im
        cf_r = (pr * a_re + pi * a_im) / den
        cf_i = (pi * a_re - pr * a_im) / den
        bt_c, bt_s = bt_ref[g, 0], bt_ref[g, 1]
        bb_c = cf_r * bt_c + cf_i * bt_s
        bb_s = cf_r * bt_s - cf_i * bt_c
        c_c, c_s = c_ref[g, 0], c_ref[g, 1]

        e1r, e1i, e2r, e2i = cpow(tau_col - half)
        e3r, e3i = cmul(e1r, e1i, kr[1:2], ki[1:2])
        e4r, e4i = cmul(e2r, e2i, kr[2:3], ki[2:3])

        def outer(er, ei, w_c, w_s):
            return (er[:, None, :] * w_c[None, :, :] + ei[:, None, :] * w_s[None, :, :]).reshape(width, n2)

        qm = outer(e1r, e1i, c_c, c_s)
        km_conj = outer(e2r, e2i, bb_c * conj, bb_s * conj)
        qc = outer(e3r, e3i, c_c, c_s)
        wz = outer(e4r, e4i, bb_c, bb_s)

        chans = pl.ds(pl.multiple_of(g * SSM_GROUP, SSM_GROUP), SSM_GROUP)
        ub = vt[:, chans, :].reshape(width, n_chunks).astype(BF16)
        yield

        tt = _dot_nt(qm.astype(BF16), km_conj.astype(BF16))
        x = _dot(wz.T.astype(BF16), ub).T
        yield
        y = _dot((tt * tmask[...]).astype(BF16), ub)
        p_r, p_i = kr[3:4], ki[3:4]
        d = 1
        while d < n_chunks:
            s = jnp.where(row >= d, pltpu.roll(x, d, 0), 0.0)
            x = x + s * p_r - pltpu.roll(s, SSM_STATE, 1) * (p_i * conj)
            p_r, p_i = cmul(p_r, p_i, p_r, p_i)
            d *= 2
        prev_conj = jnp.where(row >= 1, pltpu.roll(x, 1, 0), 0.0) * conj
        yield
        y = y + _dot_nt(qc.astype(BF16), prev_conj.astype(BF16))
        yt[:, chans, :] = y.reshape(L, SSM_GROUP, n_chunks)

    def group_batch(k, carry):
        running = [group(S5_GROUPS_PER_TRIP * k + j) for j in range(S5_GROUPS_PER_TRIP)]
        while running:
            running = [gen for gen in running if next(gen, running) is not running]
        return carry

    lax.fori_loop(0, LANES // SSM_GROUP // S5_GROUPS_PER_TRIP, group_batch, 0)
    for tau in range(L):
        ys[pl.ds(tau, n_chunks, stride=L), :] = yt[tau].T
    y_ref[...] = jax.nn.gelu(ys[...] + d_ref[...] * x_ref[...]).astype(y_ref.dtype)


def _s5_mixer(h, hn, a_re, a_im, log_dt, b_re, b_im, c_re, c_im, d_skip, w_glu, layer):
    s = h.shape[0]
    L = S5_CHUNK
    n_chunks = s // L
    width = L * SSM_GROUP
    gpb = LANES // SSM_GROUP
    def packed(re, im):
        return jnp.stack([jnp.concatenate([re, im], -1), jnp.concatenate([-im, re], -1)], axis=1).astype(F32)

    lam = jnp.stack([a_re, a_im, jnp.broadcast_to(log_dt[:, None], a_re.shape)], axis=1).astype(F32)
    lam = jnp.concatenate([lam, lam], axis=-1)
    bt = packed(b_re.transpose(0, 2, 1), b_im.transpose(0, 2, 1))
    cc = packed(c_re, c_im)
    dd = d_skip.astype(F32).reshape(1, D_MODEL)
    y = pl.pallas_call(
        functools.partial(_s5_kernel, chunk=L),
        out_shape=jax.ShapeDtypeStruct((s, D_MODEL), BF16),
        grid=(SSM_GROUPS // gpb,),
        in_specs=[
            pl.BlockSpec((s, LANES), lambda b: (0, b)),
            pl.BlockSpec((gpb, 3, 2 * SSM_STATE), lambda b: (b, 0, 0)),
            pl.BlockSpec((gpb, 2, SSM_GROUP, 2 * SSM_STATE), lambda b: (b, 0, 0, 0)),
            pl.BlockSpec((gpb, 2, SSM_GROUP, 2 * SSM_STATE), lambda b: (b, 0, 0, 0)),
            pl.BlockSpec((1, LANES), lambda b: (0, b)),
        ],
        out_specs=pl.BlockSpec((s, LANES), lambda b: (0, b)),
        scratch_shapes=[
            pltpu.VMEM((L, LANES, n_chunks), F32),
            pltpu.VMEM((L, LANES, n_chunks), F32),
            pltpu.VMEM((s, LANES), F32),
            pltpu.VMEM((width, width), F32),
        ],
        compiler_params=_params("arbitrary"),
    )(hn, lam, bt, cc, dd)
    return _matmul(_ep_glu_resid, y, w_glu, layer, D_MODEL, F32, 1024, 512, b_halves=2, tile_extra=h)


def kernel(x, rel_bias, mix_norm_g, ffn_norm_g, final_norm_g, nsa_w_in, nsa_w_out, cmp_pos_k, cmp_w1_k, cmp_w2_k, cmp_pos_v, cmp_w1_v, cmp_w2_v, s5_A_re, s5_A_im, s5_log_dt, s5_B_re, s5_B_im, s5_C_re, s5_C_im, s5_D, s5_w_glu, ffn_w_in, ffn_w_out):
    assert x.shape[0] == 1
    h = x[0]
    hn = _rmsnorm(h, mix_norm_g[0], BF16)
    h = _nsa_mixer(h, hn, rel_bias, nsa_w_in, nsa_w_out, 0, cmp_pos_k[0], cmp_w1_k[0], cmp_w2_k[0],
                   cmp_pos_v[0], cmp_w1_v[0], cmp_w2_v[0])
    h = _swiglu_ffn(h, ffn_norm_g[0], ffn_w_in, ffn_w_out, 0)
    hn = _rmsnorm(h, mix_norm_g[1], F32)
    h = _s5_mixer(h, hn, s5_A_re[0], s5_A_im[0], s5_log_dt[0], s5_B_re[0], s5_B_im[0], s5_C_re[0], s5_C_im[0],
                  s5_D[0], s5_w_glu, 0)
    h = _swiglu_ffn(h, ffn_norm_g[1], ffn_w_in, ffn_w_out, 1)
    return _rmsnorm(h, final_norm_g, x.dtype)[None]
```

```python
import functools
import math

import numpy as np
import jax
import jax.numpy as jnp
from jax import lax
from jax.experimental import pallas as pl
from jax.experimental.pallas import tpu as pltpu

D_MODEL = 2048
N_HEADS = 16
HEAD_DIM = 128
N_KV_GROUPS = 4
HPG = N_HEADS // N_KV_GROUPS
CMP_BLOCK = 32
CMP_STRIDE = 16
CMP_HIDDEN = 2 * HEAD_DIM
SEL_BLOCK = 64
SEL_TOPK = 16
N_FORCED = 3
WINDOW = 512
N_GATES = 3
KV_WIDTH = N_KV_GROUPS * HEAD_DIM
Q_WIDTH = N_HEADS * HEAD_DIM
PROJ_MAIN = Q_WIDTH + 6 * KV_WIDTH
PROJ_ATT = Q_WIDTH + 4 * KV_WIDTH
REL_BUCKETS = 32
REL_MAX_DIST = 128
SSM_GROUP = 16
SSM_GROUPS = D_MODEL // SSM_GROUP
SSM_STATE = 64
D_FF = ((8 * D_MODEL + 2) // 3 + 255) // 256 * 256
RMS_EPS = 1e-6
NEG = -1e30
FORCE_SCORE = 1e9
LOG2E = math.log2(math.e)

LANES = 128
SUBLANES = 8
VMEM_LIMIT = 52 * 1024 * 1024

ATT_TILE = WINDOW // 2
FAR_TILES = 4
Q_TILES = 2
CMP_TQ = 256
CMP_PAD = 8
S5_CHUNK = 32
S5_GROUPS_PER_TRIP = 4

BF16 = jnp.bfloat16
F32 = jnp.float32


def _dot(a, b):
    return jnp.dot(a, b, preferred_element_type=F32)


def _dot_nt(a, b):
    return lax.dot_general(a, b, (((1,), (1,)), ((), ())), preferred_element_type=F32)


def _params(*sem):
    return pltpu.CompilerParams(dimension_semantics=sem, vmem_limit_bytes=VMEM_LIMIT)


def _rmsnorm_kernel(x_ref, g_ref, o_ref):
    x = x_ref[...]
    ms = jnp.mean(x * x, axis=-1, keepdims=True)
    o_ref[...] = (x * lax.rsqrt(ms + RMS_EPS) * g_ref[...]).astype(o_ref.dtype)


def _rmsnorm(x, g, out_dtype, tm=512):
    s, d = x.shape
    return pl.pallas_call(
        _rmsnorm_kernel,
        out_shape=jax.ShapeDtypeStruct((s, d), out_dtype),
        grid=(s // tm,),
        in_specs=[pl.BlockSpec((tm, d), lambda i: (i, 0)), pl.BlockSpec((1, d), lambda i: (0, 0))],
        out_specs=pl.BlockSpec((tm, d), lambda i: (i, 0)),
        compiler_params=_params("parallel"),
    )(x, g.reshape(1, d))


def _mm_kernel(*refs, n_b, n_extra, epilogue, w_rows_are_outputs):
    a_ref = refs[0]
    b_refs = refs[1:1 + n_b]
    extra_refs = refs[1 + n_b:1 + n_b + n_extra]
    o_ref = refs[1 + n_b + n_extra]
    w_scr = refs[2 + n_b + n_extra:]

    @pl.when(pl.program_id(1) == 0)
    def _():
        for b_ref, w in zip(b_refs, w_scr):
            w[...] = b_ref[...].astype(BF16)

    a = a_ref[...]
    z = [(_dot_nt if w_rows_are_outputs else _dot)(a, w[...]) for w in w_scr]
    o_ref[...] = epilogue(*z, *[e[...] for e in extra_refs]).astype(o_ref.dtype)


def _ep_scale(z, scale):
    return z * scale


def _ep_resid(z, resid):
    return resid + z


def _ep_swiglu(za, zb):
    return jax.nn.silu(za) * zb


def _ep_glu_resid(za, zb, resid):
    return resid + za * jax.nn.sigmoid(zb)


def _matmul(epilogue, a, b, layer, n_out, out_dtype, tm, tn, *, b_halves=1, row_extra=None, tile_extra=None,
            w_rows_are_outputs=False, src_block=lambda j: j):
    m, k = a.shape
    nj = n_out // tn
    estimate = (b_halves * k * tn * (2 * 4 + 2) + 2 * tm * k * 2 + 4 * tm * tn * 4 + b_halves * tm * tn * 4)
    b_mode = pl.Buffered(1) if estimate > VMEM_LIMIT else None
    in_specs = [pl.BlockSpec((tm, k), lambda j, i: (i, 0))]
    args = [a]
    for half in range(b_halves):
        if w_rows_are_outputs:
            spec = pl.BlockSpec((None, tn, k), functools.partial(lambda j, i, o: (layer, src_block(j) + o, 0), o=half * nj),
                                pipeline_mode=b_mode)
        else:
            spec = pl.BlockSpec((None, k, tn), functools.partial(lambda j, i, o: (layer, 0, src_block(j) + o), o=half * nj),
                                pipeline_mode=b_mode)
        in_specs.append(spec)
        args.append(b)
    extras = []
    if row_extra is not None:
        in_specs.append(pl.BlockSpec((1, tn), lambda j, i: (0, j)))
        extras.append(row_extra)
    if tile_extra is not None:
        in_specs.append(pl.BlockSpec((tm, tn), lambda j, i: (i, j)))
        extras.append(tile_extra)
    kern = functools.partial(_mm_kernel, n_b=b_halves, n_extra=len(extras), epilogue=epilogue,
                             w_rows_are_outputs=w_rows_are_outputs)
    return pl.pallas_call(
        kern,
        out_shape=jax.ShapeDtypeStruct((m, n_out), out_dtype),
        grid=(nj, m // tm),
        in_specs=in_specs,
        out_specs=pl.BlockSpec((tm, tn), lambda j, i: (i, j)),
        scratch_shapes=[pltpu.VMEM((tn, k) if w_rows_are_outputs else (k, tn), BF16) for _ in range(b_halves)],
        compiler_params=_params("arbitrary", "arbitrary"),
    )(*args, *extras)


def _swiglu_ffn(h, g, w_in, w_out, layer):
    hn = _rmsnorm(h, g, BF16)
    act = _matmul(_ep_swiglu, hn, w_in, layer, D_FF, BF16, 1024, 512, b_halves=2)
    return _matmul(_ep_resid, act, w_out, layer, D_MODEL, F32, 512, 512, tile_extra=h)


def _compress_kernel(x_ref, w1_ref, w2_ref, pos_ref, o_ref):
    ncp = x_ref.shape[0] // CMP_STRIDE
    top = jnp.zeros((ncp, CMP_HIDDEN), F32)
    bot = jnp.zeros((ncp, CMP_HIDDEN), F32)
    for r in range(CMP_STRIDE):
        x_r = x_ref[pl.ds(r, ncp, stride=CMP_STRIDE), :].astype(BF16)
        top = top + _dot(x_r, w1_ref[0, r * HEAD_DIM:(r + 1) * HEAD_DIM, :])
        bot = bot + _dot(x_r, w1_ref[0, (CMP_STRIDE + r) * HEAD_DIM:(CMP_STRIDE + r + 1) * HEAD_DIM, :])
    posb = _dot(pos_ref[0], w1_ref[0])[0:1]
    hid = top + pltpu.roll(bot, ncp - 1, 0) + posb
    o_ref[0, 0] = _dot(jax.nn.gelu(hid).astype(BF16), w2_ref[0]).astype(o_ref.dtype)


def _compress(kcvc, w1, w2, pos):
    s = kcvc.shape[0]
    ncp = s // CMP_STRIDE
    return pl.pallas_call(
        _compress_kernel,
        out_shape=jax.ShapeDtypeStruct((2, N_KV_GROUPS, ncp, HEAD_DIM), BF16),
        grid=(2, N_KV_GROUPS),
        in_specs=[
            pl.BlockSpec((s, HEAD_DIM), lambda kv, gi: (0, kv * N_KV_GROUPS + gi)),
            pl.BlockSpec((1, CMP_BLOCK * HEAD_DIM, CMP_HIDDEN), lambda kv, gi: (kv, 0, 0)),
            pl.BlockSpec((1, CMP_HIDDEN, HEAD_DIM), lambda kv, gi: (kv, 0, 0)),
            pl.BlockSpec((1, SUBLANES, CMP_BLOCK * HEAD_DIM), lambda kv, gi: (kv, 0, 0)),
        ],
        out_specs=pl.BlockSpec((1, 1, ncp, HEAD_DIM), lambda kv, gi: (kv, gi, 0, 0)),
        compiler_params=_params("parallel", "parallel"),
    )(kcvc, w1, w2, pos)


def _cmp_attn_kernel(q_ref, kcc_ref, vcct_ref, band_ref, mt_ref, oc_ref, nm_ref, s_scr, *, tq, nb):
    i = pl.program_id(1)
    band_start = pl.multiple_of(i * (tq // CMP_STRIDE), SUBLANES)
    n_rows = kcc_ref.shape[1]
    n_blk = mt_ref.shape[0]

    def body(n_r, n_b):
        kcc = kcc_ref[0, :n_r, :]
        vcct = vcct_ref[0, :, :n_r]
        rows = lax.broadcasted_iota(jnp.int32, (n_r, tq), 0)
        row_bias = jnp.where(rows >= CMP_PAD, jnp.where(rows < band_start + nb, 0.0, NEG), NEG)
        imp = jnp.zeros((n_r, tq), F32)

        def scores(h):
            qh = q_ref[:, h * HEAD_DIM:(h + 1) * HEAD_DIM]
            s_scr[h, :n_r, :] = _dot_nt(kcc, qh) + row_bias
            s_scr[h, pl.ds(band_start, nb), :] += band_ref[h]

        scores(0)
        for h in range(HPG):
            if h + 1 < HPG:
                scores(h + 1)
            s = s_scr[h, :n_r, :]
            m = jnp.maximum(jnp.max(s, axis=0, keepdims=True), 0.5 * NEG)
            e = jnp.exp2(s - m)
            l = jnp.sum(e, axis=0, keepdims=True)
            pn = e * (1.0 / jnp.where(l > 0.0, l, 1.0))
            imp = imp + pn
            oct_h = _dot(vcct, pn.astype(BF16))
            oc_ref[:, h * HEAD_DIM:(h + 1) * HEAD_DIM] = oct_h.T
        mt = mt_ref[:n_b, :n_r]
        hi = imp.astype(BF16)
        rest = imp - hi.astype(F32)
        mid = rest.astype(BF16)
        lo = (rest - mid.astype(F32)).astype(BF16)
        slc = _dot(mt, hi) + _dot(mt, mid) + _dot(mt, lo)
        jj = lax.broadcasted_iota(jnp.int32, (n_b, tq), 0)
        t_blk = (i * tq + lax.broadcasted_iota(jnp.int32, (n_b, tq), 1)) // SEL_BLOCK
        forced = jnp.where(jj == 0, 1, jnp.where(jj == t_blk, 1, jnp.where(jj == t_blk - 1, 1, 0)))
        picked = jnp.where(forced == 1, 1.0, 0.0)
        score = jnp.where(forced == 1, -jnp.inf, jnp.where(jj <= t_blk, slc, NEG))
        for _ in range(SEL_TOPK - N_FORCED):
            best = jnp.max(score, axis=0, keepdims=True)
            first = jnp.min(jnp.where(score == best, jj, n_blk), axis=0, keepdims=True)
            hit = jj == first
            picked = jnp.where(hit, 1.0, picked)
            score = jnp.where(hit, -jnp.inf, score)
        neg_mask = jnp.where(picked > 0.5, jnp.where(jj <= t_blk, 0.0, NEG), NEG)
        if n_b < n_blk:
            neg_mask = jnp.concatenate([neg_mask, jnp.full((n_blk - n_b, tq), NEG, F32)], axis=0)
        nm_ref[...] = neg_mask.T.astype(nm_ref.dtype)

    step = LANES
    extents = list(range(step, n_rows, step)) + [n_rows]
    need_rows = band_start + nb
    for v, n_r in enumerate(extents):
        lower = extents[v - 1] if v else 0
        n_b = min(n_blk, -(-(n_r * CMP_STRIDE // SEL_BLOCK) // SUBLANES) * SUBLANES)
        pl.when(jnp.logical_and(need_rows > lower, need_rows <= n_r))(functools.partial(body, n_r, n_b))


def _cmp_attn(proj, kcc, vcct, band, mt, tq):
    s = proj.shape[0]
    n_rows = kcc.shape[1]
    nb = band.shape[1]
    kern = functools.partial(_cmp_attn_kernel, tq=tq, nb=nb)
    return pl.pallas_call(
        kern,
        out_shape=(jax.ShapeDtypeStruct((s, Q_WIDTH), F32),
                   jax.ShapeDtypeStruct((s, N_KV_GROUPS * LANES), BF16)),
        grid=(N_KV_GROUPS, s // tq),
        in_specs=[
            pl.BlockSpec((tq, HPG * HEAD_DIM), lambda g, i: (i, g)),
            pl.BlockSpec((1, n_rows, HEAD_DIM), lambda g, i: (g, 0, 0)),
            pl.BlockSpec((1, HEAD_DIM, n_rows), lambda g, i: (g, 0, 0)),
            pl.BlockSpec((HPG, nb, tq), lambda g, i: (g, 0, 0)),
            pl.BlockSpec((LANES, n_rows), lambda g, i: (0, 0)),
        ],
        out_specs=(pl.BlockSpec((tq, HPG * HEAD_DIM), lambda g, i: (i, g)),
                   pl.BlockSpec((tq, LANES), lambda g, i: (i, g))),
        scratch_shapes=[pltpu.VMEM((HPG, n_rows, tq), F32)],
        compiler_params=_params("parallel", "parallel"),
    )(proj, kcc, vcct, band, mt)


def _lane_tile(x, width):
    return jnp.concatenate([x] * (width // x.shape[1]), axis=1)


def _softmax_update(s, v_aug, m_ref, acc_ref, rows):
    m_prev = m_ref[rows, :]
    m_next = jnp.maximum(m_prev, jnp.max(s, axis=1, keepdims=True))
    alpha = jnp.exp2(m_prev - m_next)
    p = jnp.exp2(s - _lane_tile(m_next, s.shape[1]))
    acc_ref[rows, :] = _lane_tile(alpha, acc_ref.shape[1]) * acc_ref[rows, :] + _dot(p.astype(BF16), v_aug)
    m_ref[rows, :] = m_next


def _softmax_once(s, v_aug):
    m = jnp.broadcast_to(jnp.max(s, axis=1, keepdims=True), (s.shape[0], LANES))
    p = jnp.exp2(s - _lane_tile(m, s.shape[1]))
    return _dot(p.astype(BF16), v_aug)


def _sel_win_kernel(q_ref, nm_ref, ks_ref, vs_ref, kw_ref, vw_ref, blk_ref, per_ref, g_ref, oc_ref, o_ref,
                    q4, m_s, a_s, bias_ref, *, tile):
    step = pl.program_id(1)
    t = tile
    first = step * Q_TILES

    @pl.when(step == 0)
    def _():
        tl = lax.broadcasted_iota(jnp.int32, (t, t), 0)
        kl = lax.broadcasted_iota(jnp.int32, (t, t), 1)
        edge = jnp.where(tl < kl, 0.0, NEG)
        for h in range(HPG):
            bias_ref[h, :, :t] = edge
            for which in range(2):
                rows = jnp.broadcast_to(per_ref[h, which:which + 1, :], (t, 2 * t))
                table = pltpu.roll(rows, 0, 1, stride=1, stride_axis=0)
                bias_ref[h, :, (which + 1) * t:(which + 2) * t] = table[:, :t]
    def stacked(qt, h):
        return slice((qt * HPG + h) * t, (qt * HPG + h + 1) * t)

    def tile_rows(qt):
        return slice(qt * t, (qt + 1) * t)

    for qt in range(Q_TILES):
        for h in range(HPG):
            q4[stacked(qt, h), :HEAD_DIM] = q_ref[tile_rows(qt), h * HEAD_DIM:(h + 1) * HEAD_DIM]
            q4[stacked(qt, h), HEAD_DIM:] = nm_ref[tile_rows(qt), :]
    m_s[...] = jnp.full(m_s.shape, -3e38, F32)
    a_s[...] = jnp.zeros(a_s.shape, F32)

    def sel_operands(row0, n_tiles):
        keys = pl.ds(row0, n_tiles * t)
        k_aug = jnp.concatenate([ks_ref[keys, :], blk_ref[keys, :]], axis=1)
        v_aug = jnp.concatenate([vs_ref[keys, :], jnp.ones((n_tiles * t, LANES), BF16)], axis=1)
        return k_aug, v_aug

    def sel_pass(qt, operands, n_biased):
        k_aug, v_aug = operands

        def score(h):
            s = _dot_nt(q4[stacked(qt, h), :], k_aug)
            plain = s.shape[1] - n_biased * t
            if n_biased:
                near = s[:, plain:] + bias_ref[h, :, (3 - n_biased) * t:]
                s = jnp.concatenate([s[:, :plain], near], axis=1) if plain else near
            return s

        def consume(h, s):
            _softmax_update(s, v_aug, m_s, a_s, stacked(qt, h))

        return score, consume

    def win_pass(qt, row0, n_tiles):
        keys = pl.ds(row0, n_tiles * t)
        v_aug = jnp.concatenate([vw_ref[keys, :], jnp.ones((n_tiles * t, LANES), BF16)], axis=1)

        def score(h):
            return _dot_nt(q4[stacked(qt, h), :HEAD_DIM], kw_ref[keys, :]) + bias_ref[h, :, (3 - n_tiles) * t:]

        def consume(h, s):
            acc_w = _softmax_once(s, v_aug)
            acc_s = a_s[stacked(qt, h), :]
            o_w = acc_w[:, :HEAD_DIM] * (1.0 / acc_w[:, HEAD_DIM:])
            o_s = acc_s[:, :HEAD_DIM] * (1.0 / acc_s[:, HEAD_DIM:])
            cols = slice(h * HEAD_DIM, (h + 1) * HEAD_DIM)
            gates = g_ref[tile_rows(qt), :]
            o = (gates[:, h:h + 1] * oc_ref[tile_rows(qt), cols] + gates[:, HPG + h:HPG + h + 1] * o_s
                 + gates[:, 2 * HPG + h:2 * HPG + h + 1] * o_w)
            o_ref[tile_rows(qt), cols] = o.astype(o_ref.dtype)

        return score, consume

    def run(passes):
        work = [(score, consume, h0) for score, consume in passes for h0 in range(0, HPG, 2)]
        ahead = [work[0][0](h) for h in (0, 1)]
        for n, (_, consume, h0) in enumerate(work):
            now = ahead
            if n + 1 < len(work):
                nxt_score, _, nxt_h0 = work[n + 1]
                ahead = [nxt_score(h) for h in (nxt_h0, nxt_h0 + 1)]
            for h, s in zip((h0, h0 + 1), now):
                consume(h, s)

    def far_passes(row0):
        operands = sel_operands(row0, FAR_TILES)
        return [sel_pass(qt, operands, 0) for qt in range(Q_TILES)]

    n_far = jnp.maximum(first - 1, 0)
    n_chunks = n_far // FAR_TILES
    chunk = FAR_TILES * t

    def far_pair(c, carry):
        row0 = pl.multiple_of(c * 2 * chunk, 2 * chunk)
        run(far_passes(row0) + far_passes(row0 + chunk))
        return carry

    lax.fori_loop(0, n_chunks // 2, far_pair, 0)

    @pl.when(n_chunks % 2 == 1)
    def _():
        run(far_passes(pl.multiple_of((n_chunks - 1) * chunk, chunk)))

    def tail(left):
        row0 = pl.multiple_of((n_far - left) * t, t)
        passes = []
        for qt in range(Q_TILES):
            n_sel = left + 2 + qt
            passes.append(sel_pass(qt, sel_operands(row0, n_sel), 2))
            passes.append(win_pass(qt, pl.multiple_of((first + qt - 2) * t, t), 3))
        run(passes)

    for left in range(FAR_TILES):
        if (left + 1) % Q_TILES == 0:
            pl.when(jnp.logical_and(step >= 1, n_far % FAR_TILES == left))(functools.partial(tail, left))

    @pl.when(step == 0)
    def _():
        passes = []
        for qt in range(Q_TILES):
            n_keys = min(qt + 1, 3)
            passes.append(sel_pass(qt, sel_operands(0, qt + 1), min(qt + 1, 2)))
            passes.append(win_pass(qt, (qt + 1 - n_keys) * t, n_keys))
        run(passes)


def _sel_win(proj, neg_mask, blk_onehot, periods, gates, o_c, tile):
    s = proj.shape[0]
    ks_blk = Q_WIDTH // HEAD_DIM
    vs_blk = (Q_WIDTH + KV_WIDTH) // HEAD_DIM
    kw_blk = (Q_WIDTH + 2 * KV_WIDTH) // HEAD_DIM
    vw_blk = (Q_WIDTH + 3 * KV_WIDTH) // HEAD_DIM
    once = pl.Buffered(1)
    rows = Q_TILES * tile
    stacked_rows = HPG * rows
    kern = functools.partial(_sel_win_kernel, tile=tile)
    return pl.pallas_call(
        kern,
        out_shape=jax.ShapeDtypeStruct((s, Q_WIDTH), BF16),
        grid=(N_KV_GROUPS, s // rows),
        in_specs=[
            pl.BlockSpec((rows, HPG * HEAD_DIM), lambda g, i: (i, g)),
            pl.BlockSpec((rows, LANES), lambda g, i: (i, g)),
            pl.BlockSpec((s, HEAD_DIM), lambda g, i: (0, ks_blk + g), pipeline_mode=once),
            pl.BlockSpec((s, HEAD_DIM), lambda g, i: (0, vs_blk + g), pipeline_mode=once),
            pl.BlockSpec((s, HEAD_DIM), lambda g, i: (0, kw_blk + g), pipeline_mode=once),
            pl.BlockSpec((s, HEAD_DIM), lambda g, i: (0, vw_blk + g), pipeline_mode=once),
            pl.BlockSpec((s, LANES), lambda g, i: (0, 0), pipeline_mode=once),
            pl.BlockSpec((HPG, 2, 2 * tile), lambda g, i: (g, 0, 0)),
            pl.BlockSpec((rows, LANES), lambda g, i: (i, g)),
            pl.BlockSpec((rows, HPG * HEAD_DIM), lambda g, i: (i, g)),
        ],
        out_specs=pl.BlockSpec((rows, HPG * HEAD_DIM), lambda g, i: (i, g)),
        scratch_shapes=[
            pltpu.VMEM((stacked_rows, 2 * HEAD_DIM), BF16),
            pltpu.VMEM((stacked_rows, LANES), F32),
            pltpu.VMEM((stacked_rows, 2 * HEAD_DIM), F32),
            pltpu.VMEM((HPG, tile, 3 * tile), F32),
        ],
        compiler_params=_params("arbitrary", "arbitrary"),
    )(proj, neg_mask, proj, proj, proj, proj, blk_onehot, periods, gates, o_c)


def _rel_bucket_np(dist):
    n = np.maximum(dist, 0)
    max_exact = REL_BUCKETS // 2
    large = max_exact + (np.log(np.maximum(n, 1).astype(np.float32) / np.float32(max_exact))
                         / np.float32(math.log(REL_MAX_DIST / max_exact))
                         * np.float32(REL_BUCKETS - max_exact)).astype(np.int32)
    large = np.minimum(large, REL_BUCKETS - 1)
    return np.where(n < max_exact, n, large).astype(np.int32)


def _far_distance():
    d = np.arange(4 * REL_MAX_DIST)
    b = _rel_bucket_np(d)
    assert b[-1] == REL_BUCKETS - 1
    return int(np.max(np.nonzero(b != REL_BUCKETS - 1)[0])) + 1


def _bias_table(rel_bias, dist):
    bucket = jnp.asarray(_rel_bucket_np(dist))
    shifted = (rel_bias[bucket] - rel_bias[REL_BUCKETS - 1]) * LOG2E
    tab = jnp.where(jnp.asarray(dist >= 0)[..., None], shifted, NEG)
    return jnp.moveaxis(tab, -1, 0).astype(F32)


def _bias_period(rel_bias, offset, t):
    x = np.arange(2 * t)
    return _bias_table(rel_bias, np.where(x < t, offset - x, offset + 2 * t - x))


def _nsa_mixer(h, hn, rel_bias, w_in, w_out, layer, pos_k, w1_k, w2_k, pos_v, w1_v, w2_v):
    s = h.shape[0]
    n_sel = s // SEL_BLOCK
    ncp = s // CMP_STRIDE
    tile = ATT_TILE
    far = _far_distance()
    assert n_sel <= LANES and s % CMP_TQ == 0 and s % (Q_TILES * tile) == 0 and FAR_TILES % Q_TILES == 0
    assert far <= tile and far <= CMP_STRIDE * CMP_PAD - CMP_BLOCK + 1 + CMP_STRIDE

    col_scale = jnp.where(jnp.arange(PROJ_ATT) < Q_WIDTH, HEAD_DIM ** -0.5 * LOG2E, 1.0).astype(F32)[None, :]
    w_in_t = jnp.swapaxes(w_in, 1, 2)
    tn = KV_WIDTH
    q_blocks, cmp_blocks = Q_WIDTH // tn, 2 * KV_WIDTH // tn
    proj = _matmul(_ep_scale, hn, w_in_t, layer, PROJ_ATT, BF16, 1024, tn, row_extra=col_scale,
                   w_rows_are_outputs=True, src_block=lambda j: jnp.where(j < q_blocks, j, j + cmp_blocks))
    kcvc = _matmul(lambda z: z, hn, w_in_t, layer, 2 * KV_WIDTH, F32, 1024, tn,
                   w_rows_are_outputs=True, src_block=lambda j: j + q_blocks)
    w_gate = w_in_t[layer, PROJ_MAIN:].reshape(N_GATES, N_KV_GROUPS, HPG, D_MODEL).transpose(1, 0, 2, 3)
    w_gate = w_gate.reshape(N_KV_GROUPS, N_GATES * HPG, D_MODEL)
    w_gate = jnp.pad(w_gate, ((0, 0), (0, LANES - N_GATES * HPG), (0, 0))).reshape(1, N_KV_GROUPS * LANES, D_MODEL)
    gates = _matmul(jax.nn.sigmoid, hn, w_gate, 0, N_KV_GROUPS * LANES, F32, 1024, 512, w_rows_are_outputs=True)

    w1 = jnp.stack([w1_k, w1_v]).astype(BF16)
    w2 = jnp.stack([w2_k, w2_v]).astype(BF16)
    pos = jnp.stack([pos_k.reshape(1, -1), pos_v.reshape(1, -1)]).astype(BF16)
    pos = jnp.broadcast_to(pos, (2, SUBLANES, CMP_BLOCK * HEAD_DIM))
    cc = _compress(kcvc, w1, w2, pos)
    kcc = jnp.pad(cc[0], ((0, 0), (CMP_PAD, 0), (0, 0)))
    vcct = jnp.pad(cc[1], ((0, 0), (CMP_PAD, 0), (0, 0))).transpose(0, 2, 1)
    n_rows = ncp + CMP_PAD

    nb = CMP_TQ // CMP_STRIDE + CMP_PAD
    first_dist = -CMP_STRIDE * (nb - 1 - CMP_PAD) - (CMP_BLOCK - 1)
    by_dist = _bias_table(rel_bias, first_dist + np.arange(CMP_TQ + CMP_STRIDE * (nb - 1)))
    band = jnp.stack([by_dist[:, CMP_STRIDE * (nb - 1 - b):CMP_STRIDE * (nb - 1 - b) + CMP_TQ] for b in range(nb)], axis=1)
    ratio = SEL_BLOCK // CMP_STRIDE
    lo = CMP_BLOCK // CMP_STRIDE - 1
    c_of_row = np.arange(n_rows)[None, :] - CMP_PAD
    j_of = np.arange(LANES)[:, None]
    mt = ((c_of_row >= ratio * j_of - lo) & (c_of_row <= ratio * j_of + ratio - 1)
          & (c_of_row >= 0) & (c_of_row <= ncp - 2) & (j_of < n_sel)).astype(np.float32)
    o_c, neg_mask = _cmp_attn(proj, kcc, vcct, band, jnp.asarray(mt, BF16), CMP_TQ)

    onehot = (np.arange(s)[:, None] // SEL_BLOCK == np.arange(LANES)[None, :]).astype(np.float32)
    periods = jnp.stack([_bias_period(rel_bias, tile, tile), _bias_period(rel_bias, 0, tile)], axis=1)
    o = _sel_win(proj, neg_mask, jnp.asarray(onehot, BF16), periods, gates, o_c, tile)
    return _matmul(_ep_resid, o, w_out, layer, D_MODEL, F32, 512, 1024, tile_extra=h)


def _s5_kernel(x_ref, lam_ref, bt_ref, c_ref, d_ref, y_ref, vt, yt, tmask, *, chunk):
    L = chunk
    half = L // 2
    width = L * SSM_GROUP
    n_chunks = x_ref.shape[0] // L
    by_step = jnp.swapaxes(x_ref[...].reshape(n_chunks, L, LANES), 0, 1)
    for tau in range(L):
        vt[tau] = by_step[tau].T

    @pl.when(pl.program_id(0) == 0)
    def _():
        dst = lax.broadcasted_iota(jnp.int32, (width, width), 0) // SSM_GROUP
        src = lax.broadcasted_iota(jnp.int32, (width, width), 1) // SSM_GROUP
        tmask[...] = jnp.where(src <= dst, 1.0, 0.0)

    def cmul(xr, xi, yr, yi):
        return xr * yr - xi * yi, xr * yi + xi * yr

    n2 = 2 * SSM_STATE
    sub = lax.broadcasted_iota(jnp.int32, (SUBLANES, 1), 0)
    consts = jnp.where(sub == 0, 1.0, jnp.where(sub == 1, float(half + 1), jnp.where(sub == 2, float(half - 1), float(L))))
    tau_col = lax.broadcasted_iota(jnp.int32, (L, 1), 0).astype(F32)
    row = lax.broadcasted_iota(jnp.int32, (n_chunks, n2), 0)
    conj = jnp.where(lax.broadcasted_iota(jnp.int32, (1, n2), 1) < SSM_STATE, 1.0, -1.0)

    def group(g):
        lam = lam_ref[g]
        a_re, a_im = lam[0:1], lam[1:2]
        dt = jnp.exp(lam[2:3])
        log_re, log_im = a_re * dt, a_im * dt

        def cpow(e):
            mag, cos, sin = jnp.exp(e * log_re), jnp.cos(e * log_im), jnp.sin(e * log_im)
            inv = 1.0 / mag
            return mag * cos, mag * sin, inv * cos, -(inv * sin)

        kr, ki, _, _ = cpow(consts)
        pr, pi = kr[0:1] - 1.0, ki[0:1]
        den = a_re * a_re + a_im * a_im
        cf_r = (pr * a_re + pi * a_im) / den
        cf_i = (pi * a_re - pr * a_im) / den
        bt_c, bt_s = bt_ref[g, 0], bt_ref[g, 1]
        bb_c = cf_r * bt_c + cf_i * bt_s
        bb_s = cf_r * bt_s - cf_i * bt_c
        c_c, c_s = c_ref[g, 0], c_ref[g, 1]

        e1r, e1i, e2r, e2i = cpow(tau_col - half)
        e3r, e3i = cmul(e1r, e1i, kr[1:2], ki[1:2])
        e4r, e4i = cmul(e2r, e2i, kr[2:3], ki[2:3])

        def outer(er, ei, w_c, w_s):
            return (er[:, None, :] * w_c[None, :, :] + ei[:, None, :] * w_s[None, :, :]).reshape(width, n2)

        qm = outer(e1r, e1i, c_c, c_s)
        km_conj = outer(e2r, e2i, bb_c * conj, bb_s * conj)
        qc = outer(e3r, e3i, c_c, c_s)
        wz = outer(e4r, e4i, bb_c, bb_s)

        chans = pl.ds(pl.multiple_of(g * SSM_GROUP, SSM_GROUP), SSM_GROUP)
        ub = vt[:, chans, :].reshape(width, n_chunks).astype(BF16)
        yield

        tt = _dot_nt(qm.astype(BF16), km_conj.astype(BF16))
        x = _dot(wz.T.astype(BF16), ub).T
        yield
        y = _dot((tt * tmask[...]).astype(BF16), ub)
        p_r, p_i = kr[3:4], ki[3:4]
        d = 1
        while d < n_chunks:
            s = jnp.where(row >= d, pltpu.roll(x, d, 0), 0.0)
            x = x + s * p_r - pltpu.roll(s, SSM_STATE, 1) * (p_i * conj)
            p_r, p_i = cmul(p_r, p_i, p_r, p_i)
            d *= 2
        prev_conj = jnp.where(row >= 1, pltpu.roll(x, 1, 0), 0.0) * conj
        yield
        y = y + _dot_nt(qc.astype(BF16), prev_conj.astype(BF16))
        yt[:, chans, :] = y.reshape(L, SSM_GROUP, n_chunks)

    def group_batch(k, carry):
        running = [group(S5_GROUPS_PER_TRIP * k + j) for j in range(S5_GROUPS_PER_TRIP)]
        while running:
            running = [gen for gen in running if next(gen, running) is not running]
        return carry

    lax.fori_loop(0, LANES // SSM_GROUP // S5_GROUPS_PER_TRIP, group_batch, 0)
    by_chunk = jnp.swapaxes(jnp.stack([yt[tau].T for tau in range(L)], axis=0), 0, 1)
    y = by_chunk.reshape(n_chunks * L, LANES) + d_ref[...] * x_ref[...]
    y_ref[...] = jax.nn.gelu(y).astype(y_ref.dtype)


def _s5_mixer(h, hn, a_re, a_im, log_dt, b_re, b_im, c_re, c_im, d_skip, w_glu, layer):
    s = h.shape[0]
    L = S5_CHUNK
    n_chunks = s // L
    width = L * SSM_GROUP
    gpb = LANES // SSM_GROUP
    def packed(re, im):
        return jnp.stack([jnp.concatenate([re, im], -1), jnp.concatenate([-im, re], -1)], axis=1).astype(F32)

    lam = jnp.stack([a_re, a_im, jnp.broadcast_to(log_dt[:, None], a_re.shape)], axis=1).astype(F32)
    lam = jnp.concatenate([lam, lam], axis=-1)
    bt = packed(b_re.transpose(0, 2, 1), b_im.transpose(0, 2, 1))
    cc = packed(c_re, c_im)
    dd = d_skip.astype(F32).reshape(1, D_MODEL)
    y = pl.pallas_call(
        functools.partial(_s5_kernel, chunk=L),
        out_shape=jax.ShapeDtypeStruct((s, D_MODEL), BF16),
        grid=(SSM_GROUPS // gpb,),
        in_specs=[
            pl.BlockSpec((s, LANES), lambda b: (0, b)),
            pl.BlockSpec((gpb, 3, 2 * SSM_STATE), lambda b: (b, 0, 0)),
            pl.BlockSpec((gpb, 2, SSM_GROUP, 2 * SSM_STATE), lambda b: (b, 0, 0, 0)),
            pl.BlockSpec((gpb, 2, SSM_GROUP, 2 * SSM_STATE), lambda b: (b, 0, 0, 0)),
            pl.BlockSpec((1, LANES), lambda b: (0, b)),
        ],
        out_specs=pl.BlockSpec((s, LANES), lambda b: (0, b)),
        scratch_shapes=[
            pltpu.VMEM((L, LANES, n_chunks), F32),
            pltpu.VMEM((L, LANES, n_chunks), F32),
            pltpu.VMEM((width, width), F32),
        ],
        compiler_params=_params("arbitrary"),
    )(hn, lam, bt, cc, dd)
    return _matmul(_ep_glu_resid, y, w_glu, layer, D_MODEL, F32, 1024, 512, b_halves=2, tile_extra=h)


def kernel(x, rel_bias, mix_norm_g, ffn_norm_g, final_norm_g, nsa_w_in, nsa_w_out, cmp_pos_k, cmp_w1_k, cmp_w2_k, cmp_pos_v, cmp_w1_v, cmp_w2_v, s5_A_re, s5_A_im, s5_log_dt, s5_B_re, s5_B_im, s5_C_re, s5_C_im, s5_D, s5_w_glu, ffn_w_in, ffn_w_out):
    assert x.shape[0] == 1
    h = x[0]
    hn = _rmsnorm(h, mix_norm_g[0], BF16)
    h = _nsa_mixer(h, hn, rel_bias, nsa_w_in, nsa_w_out, 0, cmp_pos_k[0], cmp_w1_k[0], cmp_w2_k[0],
                   cmp_pos_v[0], cmp_w1_v[0], cmp_w2_v[0])
    h = _swiglu_ffn(h, ffn_norm_g[0], ffn_w_in, ffn_w_out, 0)
    hn = _rmsnorm(h, mix_norm_g[1], F32)
    h = _s5_mixer(h, hn, s5_A_re[0], s5_A_im[0], s5_log_dt[0], s5_B_re[0], s5_B_im[0], s5_C_re[0], s5_C_im[0],
                  s5_D[0], s5_w_glu, 0)
    h = _swiglu_ffn(h, ffn_norm_g[1], ffn_w_in, ffn_w_out, 1)
    return _rmsnorm(h, final_norm_g, x.dtype)[None]
```

```python
import functools
import math

import numpy as np
import jax
import jax.numpy as jnp
from jax import lax
from jax.experimental import pallas as pl
from jax.experimental.pallas import tpu as pltpu

D_MODEL = 2048
N_HEADS = 16
HEAD_DIM = 128
N_KV_GROUPS = 4
HPG = N_HEADS // N_KV_GROUPS
CMP_BLOCK = 32
CMP_STRIDE = 16
CMP_HIDDEN = 2 * HEAD_DIM
SEL_BLOCK = 64
SEL_TOPK = 16
N_FORCED = 3
WINDOW = 512
N_GATES = 3
KV_WIDTH = N_KV_GROUPS * HEAD_DIM
Q_WIDTH = N_HEADS * HEAD_DIM
PROJ_MAIN = Q_WIDTH + 6 * KV_WIDTH
REL_BUCKETS = 32
REL_MAX_DIST = 128
SSM_GROUP = 16
SSM_GROUPS = D_MODEL // SSM_GROUP
SSM_STATE = 64
D_FF = ((8 * D_MODEL + 2) // 3 + 255) // 256 * 256
RMS_EPS = 1e-6
NEG = -1e30
FORCE_SCORE = 1e9
LOG2E = math.log2(math.e)

LANES = 128
SUBLANES = 8
VMEM_LIMIT = 52 * 1024 * 1024

ATT_TILE = WINDOW // 2
FAR_TILES = 4
Q_TILES = 2
CMP_TQ = 256
CMP_PAD = 8
S5_CHUNK = 32
S5_GROUPS_PER_TRIP = 4

BF16 = jnp.bfloat16
F32 = jnp.float32


def _dot(a, b):
    return jnp.dot(a, b, preferred_element_type=F32)


def _dot_nt(a, b):
    return lax.dot_general(a, b, (((1,), (1,)), ((), ())), preferred_element_type=F32)


def _params(*sem):
    return pltpu.CompilerParams(dimension_semantics=sem, vmem_limit_bytes=VMEM_LIMIT)


def _rmsnorm_kernel(x_ref, g_ref, o_ref):
    x = x_ref[...]
    ms = jnp.mean(x * x, axis=-1, keepdims=True)
    o_ref[...] = (x * lax.rsqrt(ms + RMS_EPS) * g_ref[...]).astype(o_ref.dtype)


def _rmsnorm(x, g, out_dtype, tm=512):
    s, d = x.shape
    return pl.pallas_call(
        _rmsnorm_kernel,
        out_shape=jax.ShapeDtypeStruct((s, d), out_dtype),
        grid=(s // tm,),
        in_specs=[pl.BlockSpec((tm, d), lambda i: (i, 0)), pl.BlockSpec((1, d), lambda i: (0, 0))],
        out_specs=pl.BlockSpec((tm, d), lambda i: (i, 0)),
        compiler_params=_params("parallel"),
    )(x, g.reshape(1, d))


def _mm_kernel(*refs, n_b, n_extra, epilogue, w_rows_are_outputs):
    a_ref = refs[0]
    b_refs = refs[1:1 + n_b]
    extra_refs = refs[1 + n_b:1 + n_b + n_extra]
    o_ref = refs[1 + n_b + n_extra]
    w_scr = refs[2 + n_b + n_extra:]

    @pl.when(pl.program_id(1) == 0)
    def _():
        for b_ref, w in zip(b_refs, w_scr):
            w[...] = b_ref[...].astype(BF16)

    a = a_ref[...]
    z = [(_dot_nt if w_rows_are_outputs else _dot)(a, w[...]) for w in w_scr]
    o_ref[...] = epilogue(*z, *[e[...] for e in extra_refs]).astype(o_ref.dtype)


def _ep_scale(z, scale):
    return z * scale


def _ep_resid(z, resid):
    return resid + z


def _ep_swiglu(za, zb):
    return jax.nn.silu(za) * zb


def _ep_glu_resid(za, zb, resid):
    return resid + za * jax.nn.sigmoid(zb)


def _matmul(epilogue, a, b, layer, n_out, out_dtype, tm, tn, *, b_halves=1, row_extra=None, tile_extra=None,
            w_rows_are_outputs=False):
    m, k = a.shape
    nj = n_out // tn
    estimate = (b_halves * k * tn * (2 * 4 + 2) + 2 * tm * k * 2 + 4 * tm * tn * 4 + b_halves * tm * tn * 4)
    b_mode = pl.Buffered(1) if estimate > VMEM_LIMIT else None
    in_specs = [pl.BlockSpec((tm, k), lambda j, i: (i, 0))]
    args = [a]
    for half in range(b_halves):
        if w_rows_are_outputs:
            spec = pl.BlockSpec((None, tn, k), functools.partial(lambda j, i, o: (layer, j + o, 0), o=half * nj),
                                pipeline_mode=b_mode)
        else:
            spec = pl.BlockSpec((None, k, tn), functools.partial(lambda j, i, o: (layer, 0, j + o), o=half * nj),
                                pipeline_mode=b_mode)
        in_specs.append(spec)
        args.append(b)
    extras = []
    if row_extra is not None:
        in_specs.append(pl.BlockSpec((1, tn), lambda j, i: (0, j)))
        extras.append(row_extra)
    if tile_extra is not None:
        in_specs.append(pl.BlockSpec((tm, tn), lambda j, i: (i, j)))
        extras.append(tile_extra)
    kern = functools.partial(_mm_kernel, n_b=b_halves, n_extra=len(extras), epilogue=epilogue,
                             w_rows_are_outputs=w_rows_are_outputs)
    return pl.pallas_call(
        kern,
        out_shape=jax.ShapeDtypeStruct((m, n_out), out_dtype),
        grid=(nj, m // tm),
        in_specs=in_specs,
        out_specs=pl.BlockSpec((tm, tn), lambda j, i: (i, j)),
        scratch_shapes=[pltpu.VMEM((tn, k) if w_rows_are_outputs else (k, tn), BF16) for _ in range(b_halves)],
        compiler_params=_params("arbitrary", "arbitrary"),
    )(*args, *extras)


def _swiglu_ffn(h, g, w_in, w_out, layer):
    hn = _rmsnorm(h, g, BF16)
    act = _matmul(_ep_swiglu, hn, w_in, layer, D_FF, BF16, 1024, 512, b_halves=2)
    return _matmul(_ep_resid, act, w_out, layer, D_MODEL, F32, 512, 512, tile_extra=h)


def _compress_kernel(x_ref, w1_ref, w2_ref, pos_ref, o_ref):
    ncp = x_ref.shape[0] // CMP_STRIDE
    top = jnp.zeros((ncp, CMP_HIDDEN), F32)
    bot = jnp.zeros((ncp, CMP_HIDDEN), F32)
    by_offset = jnp.swapaxes(x_ref[...].reshape(ncp, CMP_STRIDE, HEAD_DIM), 0, 1)
    for r in range(CMP_STRIDE):
        x_r = by_offset[r]
        top = top + _dot(x_r, w1_ref[0, r * HEAD_DIM:(r + 1) * HEAD_DIM, :])
        bot = bot + _dot(x_r, w1_ref[0, (CMP_STRIDE + r) * HEAD_DIM:(CMP_STRIDE + r + 1) * HEAD_DIM, :])
    posb = _dot(pos_ref[0], w1_ref[0])[0:1]
    hid = top + pltpu.roll(bot, ncp - 1, 0) + posb
    o_ref[0, 0] = _dot(jax.nn.gelu(hid).astype(BF16), w2_ref[0]).astype(o_ref.dtype)


def _compress(proj, w1, w2, pos):
    s = proj.shape[0]
    ncp = s // CMP_STRIDE
    kc_blk = Q_WIDTH // HEAD_DIM
    return pl.pallas_call(
        _compress_kernel,
        out_shape=jax.ShapeDtypeStruct((2, N_KV_GROUPS, ncp, HEAD_DIM), BF16),
        grid=(2, N_KV_GROUPS),
        in_specs=[
            pl.BlockSpec((s, HEAD_DIM), lambda kv, gi: (0, kc_blk + kv * N_KV_GROUPS + gi)),
            pl.BlockSpec((1, CMP_BLOCK * HEAD_DIM, CMP_HIDDEN), lambda kv, gi: (kv, 0, 0)),
            pl.BlockSpec((1, CMP_HIDDEN, HEAD_DIM), lambda kv, gi: (kv, 0, 0)),
            pl.BlockSpec((1, SUBLANES, CMP_BLOCK * HEAD_DIM), lambda kv, gi: (kv, 0, 0)),
        ],
        out_specs=pl.BlockSpec((1, 1, ncp, HEAD_DIM), lambda kv, gi: (kv, gi, 0, 0)),
        compiler_params=_params("parallel", "parallel"),
    )(proj, w1, w2, pos)


def _cmp_attn_kernel(q_ref, kcc_ref, vcct_ref, band_ref, mt_ref, oc_ref, nm_ref, s_scr, *, tq, nb):
    i = pl.program_id(1)
    band_start = pl.multiple_of(i * (tq // CMP_STRIDE), SUBLANES)
    n_rows = kcc_ref.shape[1]
    n_blk = mt_ref.shape[0]

    def body(n_r, n_b):
        kcc = kcc_ref[0, :n_r, :]
        vcct = vcct_ref[0, :, :n_r]
        rows = lax.broadcasted_iota(jnp.int32, (n_r, tq), 0)
        row_bias = jnp.where(rows >= CMP_PAD, jnp.where(rows < band_start + nb, 0.0, NEG), NEG)
        imp = jnp.zeros((n_r, tq), F32)

        def scores(h):
            qh = q_ref[:, h * HEAD_DIM:(h + 1) * HEAD_DIM]
            s_scr[h, :n_r, :] = _dot_nt(kcc, qh) + row_bias
            s_scr[h, pl.ds(band_start, nb), :] += band_ref[h]

        scores(0)
        for h in range(HPG):
            if h + 1 < HPG:
                scores(h + 1)
            s = s_scr[h, :n_r, :]
            m = jnp.maximum(jnp.max(s, axis=0, keepdims=True), 0.5 * NEG)
            e = jnp.exp2(s - m)
            l = jnp.sum(e, axis=0, keepdims=True)
            pn = e * (1.0 / jnp.where(l > 0.0, l, 1.0))
            imp = imp + pn
            oct_h = _dot(vcct, pn.astype(BF16))
            oc_ref[:, h * HEAD_DIM:(h + 1) * HEAD_DIM] = oct_h.T
        mt = mt_ref[:n_b, :n_r]
        hi = imp.astype(BF16)
        rest = imp - hi.astype(F32)
        mid = rest.astype(BF16)
        lo = (rest - mid.astype(F32)).astype(BF16)
        slc = _dot(mt, hi) + _dot(mt, mid) + _dot(mt, lo)
        jj = lax.broadcasted_iota(jnp.int32, (n_b, tq), 0)
        t_blk = (i * tq + lax.broadcasted_iota(jnp.int32, (n_b, tq), 1)) // SEL_BLOCK
        forced = jnp.where(jj == 0, 1, jnp.where(jj == t_blk, 1, jnp.where(jj == t_blk - 1, 1, 0)))
        picked = jnp.where(forced == 1, 1.0, 0.0)
        score = jnp.where(forced == 1, -jnp.inf, jnp.where(jj <= t_blk, slc, NEG))
        for _ in range(SEL_TOPK - N_FORCED):
            best = jnp.max(score, axis=0, keepdims=True)
            first = jnp.min(jnp.where(score == best, jj, n_blk), axis=0, keepdims=True)
            hit = jj == first
            picked = jnp.where(hit, 1.0, picked)
            score = jnp.where(hit, -jnp.inf, score)
        neg_mask = jnp.where(picked > 0.5, jnp.where(jj <= t_blk, 0.0, NEG), NEG)
        if n_b < n_blk:
            neg_mask = jnp.concatenate([neg_mask, jnp.full((n_blk - n_b, tq), NEG, F32)], axis=0)
        nm_ref[...] = neg_mask.T.astype(nm_ref.dtype)

    step = LANES
    extents = list(range(step, n_rows, step)) + [n_rows]
    need_rows = band_start + nb
    for v, n_r in enumerate(extents):
        lower = extents[v - 1] if v else 0
        n_b = min(n_blk, -(-(n_r * CMP_STRIDE // SEL_BLOCK) // SUBLANES) * SUBLANES)
        pl.when(jnp.logical_and(need_rows > lower, need_rows <= n_r))(functools.partial(body, n_r, n_b))


def _cmp_attn(proj, kcc, vcct, band, mt, tq):
    s = proj.shape[0]
    n_rows = kcc.shape[1]
    nb = band.shape[1]
    kern = functools.partial(_cmp_attn_kernel, tq=tq, nb=nb)
    return pl.pallas_call(
        kern,
        out_shape=(jax.ShapeDtypeStruct((s, Q_WIDTH), F32),
                   jax.ShapeDtypeStruct((s, N_KV_GROUPS * LANES), BF16)),
        grid=(N_KV_GROUPS, s // tq),
        in_specs=[
            pl.BlockSpec((tq, HPG * HEAD_DIM), lambda g, i: (i, g)),
            pl.BlockSpec((1, n_rows, HEAD_DIM), lambda g, i: (g, 0, 0)),
            pl.BlockSpec((1, HEAD_DIM, n_rows), lambda g, i: (g, 0, 0)),
            pl.BlockSpec((HPG, nb, tq), lambda g, i: (g, 0, 0)),
            pl.BlockSpec((LANES, n_rows), lambda g, i: (0, 0)),
        ],
        out_specs=(pl.BlockSpec((tq, HPG * HEAD_DIM), lambda g, i: (i, g)),
                   pl.BlockSpec((tq, LANES), lambda g, i: (i, g))),
        scratch_shapes=[pltpu.VMEM((HPG, n_rows, tq), F32)],
        compiler_params=_params("parallel", "parallel"),
    )(proj, kcc, vcct, band, mt)


def _lane_tile(x, width):
    return jnp.concatenate([x] * (width // x.shape[1]), axis=1)


def _softmax_update(s, v_aug, m_ref, acc_ref, rows):
    m_prev = m_ref[rows, :]
    m_next = jnp.maximum(m_prev, jnp.max(s, axis=1, keepdims=True))
    alpha = jnp.exp2(m_prev - m_next)
    p = jnp.exp2(s - _lane_tile(m_next, s.shape[1]))
    acc_ref[rows, :] = _lane_tile(alpha, acc_ref.shape[1]) * acc_ref[rows, :] + _dot(p.astype(BF16), v_aug)
    m_ref[rows, :] = m_next


def _softmax_once(s, v_aug):
    m = jnp.broadcast_to(jnp.max(s, axis=1, keepdims=True), (s.shape[0], LANES))
    p = jnp.exp2(s - _lane_tile(m, s.shape[1]))
    return _dot(p.astype(BF16), v_aug)


def _sel_win_kernel(q_ref, nm_ref, ks_ref, vs_ref, kw_ref, vw_ref, blk_ref, per_ref, g_ref, oc_ref, o_ref,
                    q4, m_s, a_s, bias_ref, *, tile):
    step = pl.program_id(1)
    t = tile
    first = step * Q_TILES

    @pl.when(step == 0)
    def _():
        tl = lax.broadcasted_iota(jnp.int32, (t, t), 0)
        kl = lax.broadcasted_iota(jnp.int32, (t, t), 1)
        edge = jnp.where(tl < kl, 0.0, NEG)
        for h in range(HPG):
            bias_ref[h, :, :t] = edge
            for which in range(2):
                rows = jnp.broadcast_to(per_ref[h, which:which + 1, :], (t, 2 * t))
                table = pltpu.roll(rows, 0, 1, stride=1, stride_axis=0)
                bias_ref[h, :, (which + 1) * t:(which + 2) * t] = table[:, :t]
    def stacked(qt, h):
        return slice((qt * HPG + h) * t, (qt * HPG + h + 1) * t)

    def tile_rows(qt):
        return slice(qt * t, (qt + 1) * t)

    for qt in range(Q_TILES):
        for h in range(HPG):
            q4[stacked(qt, h), :HEAD_DIM] = q_ref[tile_rows(qt), h * HEAD_DIM:(h + 1) * HEAD_DIM]
            q4[stacked(qt, h), HEAD_DIM:] = nm_ref[tile_rows(qt), :]
    m_s[...] = jnp.full(m_s.shape, -3e38, F32)
    a_s[...] = jnp.zeros(a_s.shape, F32)

    def sel_operands(row0, n_tiles):
        keys = pl.ds(row0, n_tiles * t)
        k_aug = jnp.concatenate([ks_ref[keys, :], blk_ref[keys, :]], axis=1)
        v_aug = jnp.concatenate([vs_ref[keys, :], jnp.ones((n_tiles * t, LANES), BF16)], axis=1)
        return k_aug, v_aug

    def sel_pass(qt, operands, n_biased):
        k_aug, v_aug = operands

        def score(h):
            s = _dot_nt(q4[stacked(qt, h), :], k_aug)
            plain = s.shape[1] - n_biased * t
            if n_biased:
                near = s[:, plain:] + bias_ref[h, :, (3 - n_biased) * t:]
                s = jnp.concatenate([s[:, :plain], near], axis=1) if plain else near
            return s

        def consume(h, s):
            _softmax_update(s, v_aug, m_s, a_s, stacked(qt, h))

        return score, consume

    def win_pass(qt, row0, n_tiles):
        keys = pl.ds(row0, n_tiles * t)
        v_aug = jnp.concatenate([vw_ref[keys, :], jnp.ones((n_tiles * t, LANES), BF16)], axis=1)

        def score(h):
            return _dot_nt(q4[stacked(qt, h), :HEAD_DIM], kw_ref[keys, :]) + bias_ref[h, :, (3 - n_tiles) * t:]

        def consume(h, s):
            acc_w = _softmax_once(s, v_aug)
            acc_s = a_s[stacked(qt, h), :]
            o_w = acc_w[:, :HEAD_DIM] * (1.0 / acc_w[:, HEAD_DIM:])
            o_s = acc_s[:, :HEAD_DIM] * (1.0 / acc_s[:, HEAD_DIM:])
            cols = slice(h * HEAD_DIM, (h + 1) * HEAD_DIM)
            gates = g_ref[tile_rows(qt), :]
            o = (gates[:, h:h + 1] * oc_ref[tile_rows(qt), cols] + gates[:, HPG + h:HPG + h + 1] * o_s
                 + gates[:, 2 * HPG + h:2 * HPG + h + 1] * o_w)
            o_ref[tile_rows(qt), cols] = o.astype(o_ref.dtype)

        return score, consume

    def run(passes):
        work = [(score, consume, h0) for score, consume in passes for h0 in range(0, HPG, 2)]
        ahead = [work[0][0](h) for h in (0, 1)]
        for n, (_, consume, h0) in enumerate(work):
            now = ahead
            if n + 1 < len(work):
                nxt_score, _, nxt_h0 = work[n + 1]
                ahead = [nxt_score(h) for h in (nxt_h0, nxt_h0 + 1)]
            for h, s in zip((h0, h0 + 1), now):
                consume(h, s)

    def far_passes(row0):
        operands = sel_operands(row0, FAR_TILES)
        return [sel_pass(qt, operands, 0) for qt in range(Q_TILES)]

    n_far = jnp.maximum(first - 1, 0)
    n_chunks = n_far // FAR_TILES
    chunk = FAR_TILES * t

    def far_pair(c, carry):
        row0 = pl.multiple_of(c * 2 * chunk, 2 * chunk)
        run(far_passes(row0) + far_passes(row0 + chunk))
        return carry

    lax.fori_loop(0, n_chunks // 2, far_pair, 0)

    @pl.when(n_chunks % 2 == 1)
    def _():
        run(far_passes(pl.multiple_of((n_chunks - 1) * chunk, chunk)))

    def tail(left):
        row0 = pl.multiple_of((n_far - left) * t, t)
        passes = []
        for qt in range(Q_TILES):
            n_sel = left + 2 + qt
            passes.append(sel_pass(qt, sel_operands(row0, n_sel), 2))
            passes.append(win_pass(qt, pl.multiple_of((first + qt - 2) * t, t), 3))
        run(passes)

    for left in range(FAR_TILES):
        if (left + 1) % Q_TILES == 0:
            pl.when(jnp.logical_and(step >= 1, n_far % FAR_TILES == left))(functools.partial(tail, left))

    @pl.when(step == 0)
    def _():
        passes = []
        for qt in range(Q_TILES):
            n_keys = min(qt + 1, 3)
            passes.append(sel_pass(qt, sel_operands(0, qt + 1), min(qt + 1, 2)))
            passes.append(win_pass(qt, (qt + 1 - n_keys) * t, n_keys))
        run(passes)


def _sel_win(proj, neg_mask, blk_onehot, periods, gates, o_c, tile):
    s = proj.shape[0]
    ks_blk = (Q_WIDTH + 2 * KV_WIDTH) // HEAD_DIM
    vs_blk = (Q_WIDTH + 3 * KV_WIDTH) // HEAD_DIM
    kw_blk = (Q_WIDTH + 4 * KV_WIDTH) // HEAD_DIM
    vw_blk = (Q_WIDTH + 5 * KV_WIDTH) // HEAD_DIM
    once = pl.Buffered(1)
    rows = Q_TILES * tile
    stacked_rows = HPG * rows
    kern = functools.partial(_sel_win_kernel, tile=tile)
    return pl.pallas_call(
        kern,
        out_shape=jax.ShapeDtypeStruct((s, Q_WIDTH), BF16),
        grid=(N_KV_GROUPS, s // rows),
        in_specs=[
            pl.BlockSpec((rows, HPG * HEAD_DIM), lambda g, i: (i, g)),
            pl.BlockSpec((rows, LANES), lambda g, i: (i, g)),
            pl.BlockSpec((s, HEAD_DIM), lambda g, i: (0, ks_blk + g), pipeline_mode=once),
            pl.BlockSpec((s, HEAD_DIM), lambda g, i: (0, vs_blk + g), pipeline_mode=once),
            pl.BlockSpec((s, HEAD_DIM), lambda g, i: (0, kw_blk + g), pipeline_mode=once),
            pl.BlockSpec((s, HEAD_DIM), lambda g, i: (0, vw_blk + g), pipeline_mode=once),
            pl.BlockSpec((s, LANES), lambda g, i: (0, 0), pipeline_mode=once),
            pl.BlockSpec((HPG, 2, 2 * tile), lambda g, i: (g, 0, 0)),
            pl.BlockSpec((rows, LANES), lambda g, i: (i, g)),
            pl.BlockSpec((rows, HPG * HEAD_DIM), lambda g, i: (i, g)),
        ],
        out_specs=pl.BlockSpec((rows, HPG * HEAD_DIM), lambda g, i: (i, g)),
        scratch_shapes=[
            pltpu.VMEM((stacked_rows, 2 * HEAD_DIM), BF16),
            pltpu.VMEM((stacked_rows, LANES), F32),
            pltpu.VMEM((stacked_rows, 2 * HEAD_DIM), F32),
            pltpu.VMEM((HPG, tile, 3 * tile), F32),
        ],
        compiler_params=_params("arbitrary", "arbitrary"),
    )(proj, neg_mask, proj, proj, proj, proj, blk_onehot, periods, gates, o_c)


def _rel_bucket_np(dist):
    n = np.maximum(dist, 0)
    max_exact = REL_BUCKETS // 2
    large = max_exact + (np.log(np.maximum(n, 1).astype(np.float32) / np.float32(max_exact))
                         / np.float32(math.log(REL_MAX_DIST / max_exact))
                         * np.float32(REL_BUCKETS - max_exact)).astype(np.int32)
    large = np.minimum(large, REL_BUCKETS - 1)
    return np.where(n < max_exact, n, large).astype(np.int32)


def _far_distance():
    d = np.arange(4 * REL_MAX_DIST)
    b = _rel_bucket_np(d)
    assert b[-1] == REL_BUCKETS - 1
    return int(np.max(np.nonzero(b != REL_BUCKETS - 1)[0])) + 1


def _bias_table(rel_bias, dist):
    bucket = jnp.asarray(_rel_bucket_np(dist))
    shifted = (rel_bias[bucket] - rel_bias[REL_BUCKETS - 1]) * LOG2E
    tab = jnp.where(jnp.asarray(dist >= 0)[..., None], shifted, NEG)
    return jnp.moveaxis(tab, -1, 0).astype(F32)


def _bias_period(rel_bias, offset, t):
    x = np.arange(2 * t)
    return _bias_table(rel_bias, np.where(x < t, offset - x, offset + 2 * t - x))


def _nsa_mixer(h, hn, rel_bias, w_in, w_out, layer, pos_k, w1_k, w2_k, pos_v, w1_v, w2_v):
    s = h.shape[0]
    n_sel = s // SEL_BLOCK
    ncp = s // CMP_STRIDE
    tile = ATT_TILE
    far = _far_distance()
    assert n_sel <= LANES and s % CMP_TQ == 0 and s % (Q_TILES * tile) == 0 and FAR_TILES % Q_TILES == 0
    assert far <= tile and far <= CMP_STRIDE * CMP_PAD - CMP_BLOCK + 1 + CMP_STRIDE

    col_scale = jnp.where(jnp.arange(PROJ_MAIN) < Q_WIDTH, HEAD_DIM ** -0.5 * LOG2E, 1.0).astype(F32)[None, :]
    w_in_t = jnp.swapaxes(w_in, 1, 2)
    proj = _matmul(_ep_scale, hn, w_in_t, layer, PROJ_MAIN, BF16, 2048, 512, row_extra=col_scale,
                   w_rows_are_outputs=True)
    w_gate = w_in_t[layer, PROJ_MAIN:].reshape(N_GATES, N_KV_GROUPS, HPG, D_MODEL).transpose(1, 0, 2, 3)
    w_gate = w_gate.reshape(N_KV_GROUPS, N_GATES * HPG, D_MODEL)
    w_gate = jnp.pad(w_gate, ((0, 0), (0, LANES - N_GATES * HPG), (0, 0))).reshape(1, N_KV_GROUPS * LANES, D_MODEL)
    gates = _matmul(jax.nn.sigmoid, hn, w_gate, 0, N_KV_GROUPS * LANES, F32, 2048, 512, w_rows_are_outputs=True)

    w1 = jnp.stack([w1_k, w1_v]).astype(BF16)
    w2 = jnp.stack([w2_k, w2_v]).astype(BF16)
    pos = jnp.stack([pos_k.reshape(1, -1), pos_v.reshape(1, -1)]).astype(BF16)
    pos = jnp.broadcast_to(pos, (2, SUBLANES, CMP_BLOCK * HEAD_DIM))
    cc = _compress(proj, w1, w2, pos)
    kcc = jnp.pad(cc[0], ((0, 0), (CMP_PAD, 0), (0, 0)))
    vcct = jnp.pad(cc[1], ((0, 0), (CMP_PAD, 0), (0, 0))).transpose(0, 2, 1)
    n_rows = ncp + CMP_PAD

    nb = CMP_TQ // CMP_STRIDE + CMP_PAD
    first_dist = -CMP_STRIDE * (nb - 1 - CMP_PAD) - (CMP_BLOCK - 1)
    by_dist = _bias_table(rel_bias, first_dist + np.arange(CMP_TQ + CMP_STRIDE * (nb - 1)))
    band = jnp.stack([by_dist[:, CMP_STRIDE * (nb - 1 - b):CMP_STRIDE * (nb - 1 - b) + CMP_TQ] for b in range(nb)], axis=1)
    ratio = SEL_BLOCK // CMP_STRIDE
    lo = CMP_BLOCK // CMP_STRIDE - 1
    c_of_row = np.arange(n_rows)[None, :] - CMP_PAD
    j_of = np.arange(LANES)[:, None]
    mt = ((c_of_row >= ratio * j_of - lo) & (c_of_row <= ratio * j_of + ratio - 1)
          & (c_of_row >= 0) & (c_of_row <= ncp - 2) & (j_of < n_sel)).astype(np.float32)
    o_c, neg_mask = _cmp_attn(proj, kcc, vcct, band, jnp.asarray(mt, BF16), CMP_TQ)

    onehot = (np.arange(s)[:, None] // SEL_BLOCK == np.arange(LANES)[None, :]).astype(np.float32)
    periods = jnp.stack([_bias_period(rel_bias, tile, tile), _bias_period(rel_bias, 0, tile)], axis=1)
    o = _sel_win(proj, neg_mask, jnp.asarray(onehot, BF16), periods, gates, o_c, tile)
    return _matmul(_ep_resid, o, w_out, layer, D_MODEL, F32, 512, 1024, tile_extra=h)


def _s5_kernel(x_ref, lam_ref, bt_ref, c_ref, d_ref, y_ref, vt, yt, tmask, *, chunk):
    L = chunk
    half = L // 2
    width = L * SSM_GROUP
    n_chunks = x_ref.shape[0] // L
    by_step = jnp.swapaxes(x_ref[...].reshape(n_chunks, L, LANES), 0, 1)
    for tau in range(L):
        vt[tau] = by_step[tau].T

    @pl.when(pl.program_id(0) == 0)
    def _():
        dst = lax.broadcasted_iota(jnp.int32, (width, width), 0) // SSM_GROUP
        src = lax.broadcasted_iota(jnp.int32, (width, width), 1) // SSM_GROUP
        tmask[...] = jnp.where(src <= dst, 1.0, 0.0)

    def cmul(xr, xi, yr, yi):
        return xr * yr - xi * yi, xr * yi + xi * yr

    n2 = 2 * SSM_STATE
    sub = lax.broadcasted_iota(jnp.int32, (SUBLANES, 1), 0)
    consts = jnp.where(sub == 0, 1.0, jnp.where(sub == 1, float(half + 1), jnp.where(sub == 2, float(half - 1), float(L))))
    tau_col = lax.broadcasted_iota(jnp.int32, (L, 1), 0).astype(F32)
    row = lax.broadcasted_iota(jnp.int32, (n_chunks, n2), 0)
    conj = jnp.where(lax.broadcasted_iota(jnp.int32, (1, n2), 1) < SSM_STATE, 1.0, -1.0)

    def group(g):
        lam = lam_ref[g]
        a_re, a_im = lam[0:1], lam[1:2]
        dt = jnp.exp(lam[2:3])
        log_re, log_im = a_re * dt, a_im * dt

        def cpow(e):
            mag, cos, sin = jnp.exp(e * log_re), jnp.cos(e * log_im), jnp.sin(e * log_im)
            inv = 1.0 / mag
            return mag * cos, mag * sin, inv * cos, -(inv * sin)

        kr, ki, _, _ = cpow(consts)
        pr, pi = kr[0:1] - 1.0, ki[0:1]
        den = a_re * a_re + a_im * a_im
        cf_r = (pr * a_re + pi * a_im) / den
        cf_i = (pi * a_re - pr * a_im) / den
        bt_c, bt_s = bt_ref[g, 0], bt_ref[g, 1]
        bb_c = cf_r * bt_c + cf_i * bt_s
        bb_s = cf_r * bt_s - cf_i * bt_c
        c_c, c_s = c_ref[g, 0], c_ref[g, 1]

        e1r, e1i, e2r, e2i = cpow(tau_col - half)
        e3r, e3i = cmul(e1r, e1i, kr[1:2], ki[1:2])
        e4r, e4i = cmul(e2r, e2i, kr[2:3], ki[2:3])

        def outer(er, ei, w_c, w_s):
            return (er[:, None, :] * w_c[None, :, :] + ei[:, None, :] * w_s[None, :, :]).reshape(width, n2)

        qm = outer(e1r, e1i, c_c, c_s)
        km_conj = outer(e2r, e2i, bb_c * conj, bb_s * conj)
        qc = outer(e3r, e3i, c_c, c_s)
        wz = outer(e4r, e4i, bb_c, bb_s)

        chans = pl.ds(pl.multiple_of(g * SSM_GROUP, SSM_GROUP), SSM_GROUP)
        ub = vt[:, chans, :].reshape(width, n_chunks).astype(BF16)
        yield

        tt = _dot_nt(qm.astype(BF16), km_conj.astype(BF16))
        x = _dot(wz.T.astype(BF16), ub).T
        yield
        y = _dot((tt * tmask[...]).astype(BF16), ub)
        p_r, p_i = kr[3:4], ki[3:4]
        d = 1
        while d < n_chunks:
            s = jnp.where(row >= d, pltpu.roll(x, d, 0), 0.0)
            x = x + s * p_r - pltpu.roll(s, SSM_STATE, 1) * (p_i * conj)
            p_r, p_i = cmul(p_r, p_i, p_r, p_i)
            d *= 2
        prev_conj = jnp.where(row >= 1, pltpu.roll(x, 1, 0), 0.0) * conj
        yield
        y = y + _dot_nt(qc.astype(BF16), prev_conj.astype(BF16))
        yt[:, chans, :] = y.reshape(L, SSM_GROUP, n_chunks)

    def group_batch(k, carry):
        running = [group(S5_GROUPS_PER_TRIP * k + j) for j in range(S5_GROUPS_PER_TRIP)]
        while running:
            running = [gen for gen in running if next(gen, running) is not running]
        return carry

    lax.fori_loop(0, LANES // SSM_GROUP // S5_GROUPS_PER_TRIP, group_batch, 0)
    by_chunk = jnp.swapaxes(jnp.stack([yt[tau].T for tau in range(L)], axis=0), 0, 1)
    y = by_chunk.reshape(n_chunks * L, LANES) + d_ref[...] * x_ref[...]
    y_ref[...] = jax.nn.gelu(y).astype(y_ref.dtype)


def _s5_mixer(h, hn, a_re, a_im, log_dt, b_re, b_im, c_re, c_im, d_skip, w_glu, layer):
    s = h.shape[0]
    L = S5_CHUNK
    n_chunks = s // L
    width = L * SSM_GROUP
    gpb = LANES // SSM_GROUP
    def packed(re, im):
        return jnp.stack([jnp.concatenate([re, im], -1), jnp.concatenate([-im, re], -1)], axis=1).astype(F32)

    lam = jnp.stack([a_re, a_im, jnp.broadcast_to(log_dt[:, None], a_re.shape)], axis=1).astype(F32)
    lam = jnp.concatenate([lam, lam], axis=-1)
    bt = packed(b_re.transpose(0, 2, 1), b_im.transpose(0, 2, 1))
    cc = packed(c_re, c_im)
    dd = d_skip.astype(F32).reshape(1, D_MODEL)
    y = pl.pallas_call(
        functools.partial(_s5_kernel, chunk=L),
        out_shape=jax.ShapeDtypeStruct((s, D_MODEL), BF16),
        grid=(SSM_GROUPS // gpb,),
        in_specs=[
            pl.BlockSpec((s, LANES), lambda b: (0, b)),
            pl.BlockSpec((gpb, 3, 2 * SSM_STATE), lambda b: (b, 0, 0)),
            pl.BlockSpec((gpb, 2, SSM_GROUP, 2 * SSM_STATE), lambda b: (b, 0, 0, 0)),
            pl.BlockSpec((gpb, 2, SSM_GROUP, 2 * SSM_STATE), lambda b: (b, 0, 0, 0)),
            pl.BlockSpec((1, LANES), lambda b: (0, b)),
        ],
        out_specs=pl.BlockSpec((s, LANES), lambda b: (0, b)),
        scratch_shapes=[
            pltpu.VMEM((L, LANES, n_chunks), F32),
            pltpu.VMEM((L, LANES, n_chunks), F32),
            pltpu.VMEM((width, width), F32),
        ],
        compiler_params=_params("arbitrary"),
    )(hn, lam, bt, cc, dd)
    return _matmul(_ep_glu_resid, y, w_glu, layer, D_MODEL, F32, 1024, 512, b_halves=2, tile_extra=h)


def kernel(x, rel_bias, mix_norm_g, ffn_norm_g, final_norm_g, nsa_w_in, nsa_w_out, cmp_pos_k, cmp_w1_k, cmp_w2_k, cmp_pos_v, cmp_w1_v, cmp_w2_v, s5_A_re, s5_A_im, s5_log_dt, s5_B_re, s5_B_im, s5_C_re, s5_C_im, s5_D, s5_w_glu, ffn_w_in, ffn_w_out):
    assert x.shape[0] == 1
    h = x[0]
    hn = _rmsnorm(h, mix_norm_g[0], BF16)
    h = _nsa_mixer(h, hn, rel_bias, nsa_w_in, nsa_w_out, 0, cmp_pos_k[0], cmp_w1_k[0], cmp_w2_k[0],
                   cmp_pos_v[0], cmp_w1_v[0], cmp_w2_v[0])
    h = _swiglu_ffn(h, ffn_norm_g[0], ffn_w_in, ffn_w_out, 0)
    hn = _rmsnorm(h, mix_norm_g[1], F32)
    h = _s5_mixer(h, hn, s5_A_re[0], s5_A_im[0], s5_log_dt[0], s5_B_re[0], s5_B_im[0], s5_C_re[0], s5_C_im[0],
                  s5_D[0], s5_w_glu, 0)
    h = _swiglu_ffn(h, ffn_norm_g[1], ffn_w_in, ffn_w_out, 1)
    return _rmsnorm(h, final_norm_g, x.dtype)[None]
```

```python
import functools
import math

import numpy as np
import jax
import jax.numpy as jnp
from jax import lax
from jax.experimental import pallas as pl
from jax.experimental.pallas import tpu as pltpu

D_MODEL = 2048
N_HEADS = 16
HEAD_DIM = 128
N_KV_GROUPS = 4
HPG = N_HEADS // N_KV_GROUPS
CMP_BLOCK = 32
CMP_STRIDE = 16
CMP_HIDDEN = 2 * HEAD_DIM
SEL_BLOCK = 64
SEL_TOPK = 16
N_FORCED = 3
WINDOW = 512
N_GATES = 3
KV_WIDTH = N_KV_GROUPS * HEAD_DIM
Q_WIDTH = N_HEADS * HEAD_DIM
PROJ_MAIN = Q_WIDTH + 6 * KV_WIDTH
REL_BUCKETS = 32
REL_MAX_DIST = 128
SSM_GROUP = 16
SSM_GROUPS = D_MODEL // SSM_GROUP
SSM_STATE = 64
D_FF = ((8 * D_MODEL + 2) // 3 + 255) // 256 * 256
RMS_EPS = 1e-6
NEG = -1e30
LOG2E = math.log2(math.e)

LANES = 128
SUBLANES = 8
VMEM_LIMIT = 52 * 1024 * 1024

ATT_TILE = WINDOW // 2
FAR_TILES = 4
Q_TILES = 2
CMP_TQ = 512
CMP_PAD = 8
S5_CHUNK = 32
S5_GROUPS_PER_TRIP = 4

BF16 = jnp.bfloat16
F32 = jnp.float32


def _dot(a, b):
    return jnp.dot(a, b, preferred_element_type=F32)


def _dot_nt(a, b):
    return lax.dot_general(a, b, (((1,), (1,)), ((), ())), preferred_element_type=F32)


def _params(*sem):
    return pltpu.CompilerParams(dimension_semantics=sem, vmem_limit_bytes=VMEM_LIMIT)


def _rmsnorm_kernel(x_ref, g_ref, o_ref):
    x = x_ref[...]
    ms = jnp.mean(x * x, axis=-1, keepdims=True)
    o_ref[...] = (x * lax.rsqrt(ms + RMS_EPS) * g_ref[...]).astype(o_ref.dtype)


def _rmsnorm(x, g, out_dtype, tm=512):
    s, d = x.shape
    return pl.pallas_call(
        _rmsnorm_kernel,
        out_shape=jax.ShapeDtypeStruct((s, d), out_dtype),
        grid=(s // tm,),
        in_specs=[pl.BlockSpec((tm, d), lambda i: (i, 0)), pl.BlockSpec((1, d), lambda i: (0, 0))],
        out_specs=pl.BlockSpec((tm, d), lambda i: (i, 0)),
        compiler_params=_params("parallel"),
    )(x, g.reshape(1, d))


def _mm_kernel(*refs, n_b, n_extra, epilogue, w_rows_are_outputs):
    a_ref = refs[0]
    b_refs = refs[1:1 + n_b]
    extra_refs = refs[1 + n_b:1 + n_b + n_extra]
    o_ref = refs[1 + n_b + n_extra]
    w_scr = refs[2 + n_b + n_extra:]

    @pl.when(pl.program_id(1) == 0)
    def _():
        for b_ref, w in zip(b_refs, w_scr):
            w[...] = b_ref[...].astype(BF16)

    a = a_ref[...]
    z = [(_dot_nt if w_rows_are_outputs else _dot)(a, w[...]) for w in w_scr]
    o_ref[...] = epilogue(*z, *[e[...] for e in extra_refs]).astype(o_ref.dtype)


def _ep_scale(z, scale):
    return z * scale


def _ep_resid(z, resid):
    return resid + z


def _ep_swiglu(za, zb):
    return jax.nn.silu(za) * zb


def _ep_glu_resid(za, zb, resid):
    return resid + za * jax.nn.sigmoid(zb)


def _matmul(epilogue, a, b, layer, n_out, out_dtype, tm, tn, *, b_halves=1, row_extra=None, tile_extra=None,
            w_rows_are_outputs=False):
    m, k = a.shape
    nj = n_out // tn
    estimate = (b_halves * k * tn * (2 * 4 + 2) + 2 * tm * k * 2 + 4 * tm * tn * 4 + b_halves * tm * tn * 4)
    b_mode = pl.Buffered(1) if estimate > VMEM_LIMIT else None
    in_specs = [pl.BlockSpec((tm, k), lambda j, i: (i, 0))]
    args = [a]
    for half in range(b_halves):
        if w_rows_are_outputs:
            spec = pl.BlockSpec((None, tn, k), functools.partial(lambda j, i, o: (layer, j + o, 0), o=half * nj),
                                pipeline_mode=b_mode)
        else:
            spec = pl.BlockSpec((None, k, tn), functools.partial(lambda j, i, o: (layer, 0, j + o), o=half * nj),
                                pipeline_mode=b_mode)
        in_specs.append(spec)
        args.append(b)
    extras = []
    if row_extra is not None:
        in_specs.append(pl.BlockSpec((1, tn), lambda j, i: (0, j)))
        extras.append(row_extra)
    if tile_extra is not None:
        in_specs.append(pl.BlockSpec((tm, tn), lambda j, i: (i, j)))
        extras.append(tile_extra)
    kern = functools.partial(_mm_kernel, n_b=b_halves, n_extra=len(extras), epilogue=epilogue,
                             w_rows_are_outputs=w_rows_are_outputs)
    return pl.pallas_call(
        kern,
        out_shape=jax.ShapeDtypeStruct((m, n_out), out_dtype),
        grid=(nj, m // tm),
        in_specs=in_specs,
        out_specs=pl.BlockSpec((tm, tn), lambda j, i: (i, j)),
        scratch_shapes=[pltpu.VMEM((tn, k) if w_rows_are_outputs else (k, tn), BF16) for _ in range(b_halves)],
        compiler_params=_params("arbitrary", "arbitrary"),
    )(*args, *extras)


def _mm_resid_norm_kernel(a_ref, b_ref, r_ref, g_ref, h_ref, hn_ref, w_scr):
    @pl.when(pl.program_id(0) == 0)
    def _():
        w_scr[...] = b_ref[...].astype(BF16)

    h = r_ref[...] + _dot(a_ref[...], w_scr[...])
    h_ref[...] = h
    ms = jnp.mean(h * h, axis=-1, keepdims=True)
    hn_ref[...] = (h * lax.rsqrt(ms + RMS_EPS) * g_ref[...]).astype(hn_ref.dtype)


def _matmul_resid_norm(a, b, layer, resid, g, tm):
    m, k = a.shape
    n = b.shape[2]
    return pl.pallas_call(
        _mm_resid_norm_kernel,
        out_shape=(jax.ShapeDtypeStruct((m, n), F32), jax.ShapeDtypeStruct((m, n), BF16)),
        grid=(m // tm,),
        in_specs=[
            pl.BlockSpec((tm, k), lambda i: (i, 0)),
            pl.BlockSpec((None, k, n), lambda i: (layer, 0, 0), pipeline_mode=pl.Buffered(1)),
            pl.BlockSpec((tm, n), lambda i: (i, 0)),
            pl.BlockSpec((1, n), lambda i: (0, 0)),
        ],
        out_specs=(pl.BlockSpec((tm, n), lambda i: (i, 0)), pl.BlockSpec((tm, n), lambda i: (i, 0))),
        scratch_shapes=[pltpu.VMEM((k, n), BF16)],
        compiler_params=_params("arbitrary"),
    )(a, b, resid, g.reshape(1, n))


def _swiglu_ffn(h, hn, w_in, w_out, layer):
    act = _matmul(_ep_swiglu, hn, w_in, layer, D_FF, BF16, 1024, 512, b_halves=2)
    return _matmul(_ep_resid, act, w_out, layer, D_MODEL, F32, 512, 512, tile_extra=h)


def _compress_kernel(x_ref, w1_ref, w2_ref, pos_ref, o_ref):
    ncp = x_ref.shape[0] // CMP_STRIDE
    top = jnp.zeros((ncp, CMP_HIDDEN), F32)
    bot = jnp.zeros((ncp, CMP_HIDDEN), F32)
    by_offset = jnp.swapaxes(x_ref[...].reshape(ncp, CMP_STRIDE, HEAD_DIM), 0, 1)
    for r in range(CMP_STRIDE):
        x_r = by_offset[r]
        top = top + _dot(x_r, w1_ref[0, r * HEAD_DIM:(r + 1) * HEAD_DIM, :])
        bot = bot + _dot(x_r, w1_ref[0, (CMP_STRIDE + r) * HEAD_DIM:(CMP_STRIDE + r + 1) * HEAD_DIM, :])
    posb = _dot(pos_ref[0], w1_ref[0])[0:1]
    hid = top + pltpu.roll(bot, ncp - 1, 0) + posb
    o_ref[0, 0] = _dot(jax.nn.gelu(hid).astype(BF16), w2_ref[0]).astype(o_ref.dtype)


def _compress(proj, w1, w2, pos):
    s = proj.shape[0]
    ncp = s // CMP_STRIDE
    kc_blk = Q_WIDTH // HEAD_DIM
    return pl.pallas_call(
        _compress_kernel,
        out_shape=jax.ShapeDtypeStruct((2, N_KV_GROUPS, ncp, HEAD_DIM), BF16),
        grid=(2, N_KV_GROUPS),
        in_specs=[
            pl.BlockSpec((s, HEAD_DIM), lambda kv, gi: (0, kc_blk + kv * N_KV_GROUPS + gi)),
            pl.BlockSpec((1, CMP_BLOCK * HEAD_DIM, CMP_HIDDEN), lambda kv, gi: (kv, 0, 0)),
            pl.BlockSpec((1, CMP_HIDDEN, HEAD_DIM), lambda kv, gi: (kv, 0, 0)),
            pl.BlockSpec((1, SUBLANES, CMP_BLOCK * HEAD_DIM), lambda kv, gi: (kv, 0, 0)),
        ],
        out_specs=pl.BlockSpec((1, 1, ncp, HEAD_DIM), lambda kv, gi: (kv, gi, 0, 0)),
        compiler_params=_params("parallel", "parallel"),
    )(proj, w1, w2, pos)


def _cmp_attn_kernel(q_ref, kcc_ref, vcct_ref, band_ref, mt_ref, oc_ref, nm_ref, s_scr, *, tq, nb):
    i = pl.program_id(1)
    band_start = pl.multiple_of(i * (tq // CMP_STRIDE), SUBLANES)
    n_rows = kcc_ref.shape[1]
    n_blk = mt_ref.shape[0]

    def body(n_r, n_b):
        kcc = kcc_ref[0, :n_r, :]
        vcct = vcct_ref[0, :, :n_r]
        rows = lax.broadcasted_iota(jnp.int32, (n_r, tq), 0)
        row_bias = jnp.where(rows >= CMP_PAD, jnp.where(rows < band_start + nb, 0.0, NEG), NEG)
        imp = jnp.zeros((n_r, tq), F32)

        def scores(h):
            qh = q_ref[:, h * HEAD_DIM:(h + 1) * HEAD_DIM]
            s_scr[h, :n_r, :] = _dot_nt(kcc, qh) + row_bias
            s_scr[h, pl.ds(band_start, nb), :] += band_ref[h]

        scores(0)
        for h in range(HPG):
            if h + 1 < HPG:
                scores(h + 1)
            s = s_scr[h, :n_r, :]
            m = jnp.maximum(jnp.max(s, axis=0, keepdims=True), 0.5 * NEG)
            e = jnp.exp2(s - m)
            l = jnp.sum(e, axis=0, keepdims=True)
            pn = e * (1.0 / jnp.where(l > 0.0, l, 1.0))
            imp = imp + pn
            oct_h = _dot(vcct, pn.astype(BF16))
            oc_ref[:, h * HEAD_DIM:(h + 1) * HEAD_DIM] = oct_h.T
        mt = mt_ref[:n_b, :n_r]
        hi = imp.astype(BF16)
        rest = imp - hi.astype(F32)
        mid = rest.astype(BF16)
        lo = (rest - mid.astype(F32)).astype(BF16)
        slc = _dot(mt, hi) + _dot(mt, mid) + _dot(mt, lo)
        jj = lax.broadcasted_iota(jnp.int32, (n_b, tq), 0)
        t_blk = (i * tq + lax.broadcasted_iota(jnp.int32, (n_b, tq), 1)) // SEL_BLOCK
        forced = jnp.where(jj == 0, 1, jnp.where(jj == t_blk, 1, jnp.where(jj == t_blk - 1, 1, 0)))
        picked = jnp.where(forced == 1, 1.0, 0.0)
        score = jnp.where(forced == 1, -jnp.inf, jnp.where(jj <= t_blk, slc, NEG))
        for _ in range(SEL_TOPK - N_FORCED):
            best = jnp.max(score, axis=0, keepdims=True)
            first = jnp.min(jnp.where(score == best, jj, n_blk), axis=0, keepdims=True)
            hit = jj == first
            picked = jnp.where(hit, 1.0, picked)
            score = jnp.where(hit, -jnp.inf, score)
        neg_mask = jnp.where(picked > 0.5, jnp.where(jj <= t_blk, 0.0, NEG), NEG)
        if n_b < n_blk:
            neg_mask = jnp.concatenate([neg_mask, jnp.full((n_blk - n_b, tq), NEG, F32)], axis=0)
        nm_ref[...] = neg_mask.T.astype(nm_ref.dtype)

    step = LANES
    extents = list(range(step, n_rows, step)) + [n_rows]
    need_rows = band_start + nb
    for v, n_r in enumerate(extents):
        lower = extents[v - 1] if v else 0
        n_b = min(n_blk, -(-(n_r * CMP_STRIDE // SEL_BLOCK) // SUBLANES) * SUBLANES)
        pl.when(jnp.logical_and(need_rows > lower, need_rows <= n_r))(functools.partial(body, n_r, n_b))


def _cmp_attn(proj, kcc, vcct, band, mt, tq):
    s = proj.shape[0]
    n_rows = kcc.shape[1]
    nb = band.shape[1]
    kern = functools.partial(_cmp_attn_kernel, tq=tq, nb=nb)
    return pl.pallas_call(
        kern,
        out_shape=(jax.ShapeDtypeStruct((s, Q_WIDTH), F32),
                   jax.ShapeDtypeStruct((s, N_KV_GROUPS * LANES), BF16)),
        grid=(N_KV_GROUPS, s // tq),
        in_specs=[
            pl.BlockSpec((tq, HPG * HEAD_DIM), lambda g, i: (i, g)),
            pl.BlockSpec((1, n_rows, HEAD_DIM), lambda g, i: (g, 0, 0)),
            pl.BlockSpec((1, HEAD_DIM, n_rows), lambda g, i: (g, 0, 0)),
            pl.BlockSpec((HPG, nb, tq), lambda g, i: (g, 0, 0)),
            pl.BlockSpec((LANES, n_rows), lambda g, i: (0, 0)),
        ],
        out_specs=(pl.BlockSpec((tq, HPG * HEAD_DIM), lambda g, i: (i, g)),
                   pl.BlockSpec((tq, LANES), lambda g, i: (i, g))),
        scratch_shapes=[pltpu.VMEM((HPG, n_rows, tq), F32)],
        compiler_params=_params("parallel", "parallel"),
    )(proj, kcc, vcct, band, mt)


def _lane_tile(x, width):
    return jnp.concatenate([x] * (width // x.shape[1]), axis=1)


def _softmax_update(s, v_aug, m_ref, acc_ref, rows):
    m_prev = m_ref[rows, :]
    m_next = jnp.maximum(m_prev, jnp.max(s, axis=1, keepdims=True))
    alpha = jnp.exp2(m_prev - m_next)
    p = jnp.exp2(s - _lane_tile(m_next, s.shape[1]))
    acc_ref[rows, :] = _lane_tile(alpha, acc_ref.shape[1]) * acc_ref[rows, :] + _dot(p.astype(BF16), v_aug)
    m_ref[rows, :] = m_next


def _softmax_once(s, v_aug):
    m = jnp.broadcast_to(jnp.max(s, axis=1, keepdims=True), (s.shape[0], LANES))
    p = jnp.exp2(s - _lane_tile(m, s.shape[1]))
    return _dot(p.astype(BF16), v_aug)


def _sel_win_kernel(q_ref, nm_ref, ks_ref, vs_ref, kw_ref, vw_ref, blk_ref, per_ref, g_ref, oc_ref, o_ref,
                    q4, m_s, a_s, bias_ref, *, tile):
    step = pl.program_id(1)
    t = tile
    first = step * Q_TILES

    @pl.when(step == 0)
    def _():
        tl = lax.broadcasted_iota(jnp.int32, (t, t), 0)
        kl = lax.broadcasted_iota(jnp.int32, (t, t), 1)
        edge = jnp.where(tl < kl, 0.0, NEG)
        for h in range(HPG):
            bias_ref[h, :, :t] = edge
            for which in range(2):
                rows = jnp.broadcast_to(per_ref[h, which:which + 1, :], (t, 2 * t))
                table = pltpu.roll(rows, 0, 1, stride=1, stride_axis=0)
                bias_ref[h, :, (which + 1) * t:(which + 2) * t] = table[:, :t]
    def stacked(qt, h):
        return slice((qt * HPG + h) * t, (qt * HPG + h + 1) * t)

    def tile_rows(qt):
        return slice(qt * t, (qt + 1) * t)

    for qt in range(Q_TILES):
        for h in range(HPG):
            q4[stacked(qt, h), :HEAD_DIM] = q_ref[tile_rows(qt), h * HEAD_DIM:(h + 1) * HEAD_DIM]
            q4[stacked(qt, h), HEAD_DIM:] = nm_ref[tile_rows(qt), :]
    m_s[...] = jnp.full(m_s.shape, -3e38, F32)
    a_s[...] = jnp.zeros(a_s.shape, F32)

    def sel_operands(row0, n_tiles):
        keys = pl.ds(row0, n_tiles * t)
        k_aug = jnp.concatenate([ks_ref[keys, :], blk_ref[keys, :]], axis=1)
        v_aug = jnp.concatenate([vs_ref[keys, :], jnp.ones((n_tiles * t, LANES), BF16)], axis=1)
        return k_aug, v_aug

    def sel_pass(qt, operands, n_biased):
        k_aug, v_aug = operands

        def score(h):
            s = _dot_nt(q4[stacked(qt, h), :], k_aug)
            plain = s.shape[1] - n_biased * t
            if n_biased:
                near = s[:, plain:] + bias_ref[h, :, (3 - n_biased) * t:]
                s = jnp.concatenate([s[:, :plain], near], axis=1) if plain else near
            return s

        def consume(h, s):
            _softmax_update(s, v_aug, m_s, a_s, stacked(qt, h))

        return score, consume

    def win_pass(qt, row0, n_tiles):
        keys = pl.ds(row0, n_tiles * t)
        v_aug = jnp.concatenate([vw_ref[keys, :], jnp.ones((n_tiles * t, LANES), BF16)], axis=1)

        def score(h):
            return _dot_nt(q4[stacked(qt, h), :HEAD_DIM], kw_ref[keys, :]) + bias_ref[h, :, (3 - n_tiles) * t:]

        def consume(h, s):
            acc_w = _softmax_once(s, v_aug)
            acc_s = a_s[stacked(qt, h), :]
            o_w = acc_w[:, :HEAD_DIM] * (1.0 / acc_w[:, HEAD_DIM:])
            o_s = acc_s[:, :HEAD_DIM] * (1.0 / acc_s[:, HEAD_DIM:])
            cols = slice(h * HEAD_DIM, (h + 1) * HEAD_DIM)
            gates = g_ref[tile_rows(qt), :]
            o = (gates[:, h:h + 1] * oc_ref[tile_rows(qt), cols] + gates[:, HPG + h:HPG + h + 1] * o_s
                 + gates[:, 2 * HPG + h:2 * HPG + h + 1] * o_w)
            o_ref[tile_rows(qt), cols] = o.astype(o_ref.dtype)

        return score, consume

    def run(passes):
        work = [(score, consume, h0) for score, consume in passes for h0 in range(0, HPG, 2)]
        ahead = [work[0][0](h) for h in (0, 1)]
        for n, (_, consume, h0) in enumerate(work):
            now = ahead
            if n + 1 < len(work):
                nxt_score, _, nxt_h0 = work[n + 1]
                ahead = [nxt_score(h) for h in (nxt_h0, nxt_h0 + 1)]
            for h, s in zip((h0, h0 + 1), now):
                consume(h, s)

    def far_passes(row0):
        operands = sel_operands(row0, FAR_TILES)
        return [sel_pass(qt, operands, 0) for qt in range(Q_TILES)]

    n_far = jnp.maximum(first - 1, 0)
    n_chunks = n_far // FAR_TILES
    chunk = FAR_TILES * t

    def far_pair(c, carry):
        row0 = pl.multiple_of(c * 2 * chunk, 2 * chunk)
        run(far_passes(row0) + far_passes(row0 + chunk))
        return carry

    lax.fori_loop(0, n_chunks // 2, far_pair, 0)

    @pl.when(n_chunks % 2 == 1)
    def _():
        run(far_passes(pl.multiple_of((n_chunks - 1) * chunk, chunk)))

    def tail(left):
        row0 = pl.multiple_of((n_far - left) * t, t)
        passes = [sel_pass(qt, sel_operands(row0, left + 2 + qt), 2) for qt in range(Q_TILES)]
        passes += [win_pass(qt, pl.multiple_of((first + qt - 2) * t, t), 3) for qt in range(Q_TILES)]
        run(passes)

    for left in range(FAR_TILES):
        if (left + 1) % Q_TILES == 0:
            pl.when(jnp.logical_and(step >= 1, n_far % FAR_TILES == left))(functools.partial(tail, left))

    @pl.when(step == 0)
    def _():
        passes = []
        for qt in range(Q_TILES):
            n_keys = min(qt + 1, 3)
            passes.append(sel_pass(qt, sel_operands(0, qt + 1), min(qt + 1, 2)))
            passes.append(win_pass(qt, (qt + 1 - n_keys) * t, n_keys))
        run(passes)


def _sel_win(proj, neg_mask, blk_onehot, periods, gates, o_c, tile):
    s = proj.shape[0]
    ks_blk = (Q_WIDTH + 2 * KV_WIDTH) // HEAD_DIM
    vs_blk = (Q_WIDTH + 3 * KV_WIDTH) // HEAD_DIM
    kw_blk = (Q_WIDTH + 4 * KV_WIDTH) // HEAD_DIM
    vw_blk = (Q_WIDTH + 5 * KV_WIDTH) // HEAD_DIM
    once = pl.Buffered(1)
    rows = Q_TILES * tile
    stacked_rows = HPG * rows
    kern = functools.partial(_sel_win_kernel, tile=tile)
    return pl.pallas_call(
        kern,
        out_shape=jax.ShapeDtypeStruct((s, Q_WIDTH), BF16),
        grid=(N_KV_GROUPS, s // rows),
        in_specs=[
            pl.BlockSpec((rows, HPG * HEAD_DIM), lambda g, i: (i, g)),
            pl.BlockSpec((rows, LANES), lambda g, i: (i, g)),
            pl.BlockSpec((s, HEAD_DIM), lambda g, i: (0, ks_blk + g), pipeline_mode=once),
            pl.BlockSpec((s, HEAD_DIM), lambda g, i: (0, vs_blk + g), pipeline_mode=once),
            pl.BlockSpec((s, HEAD_DIM), lambda g, i: (0, kw_blk + g), pipeline_mode=once),
            pl.BlockSpec((s, HEAD_DIM), lambda g, i: (0, vw_blk + g), pipeline_mode=once),
            pl.BlockSpec((s, LANES), lambda g, i: (0, 0), pipeline_mode=once),
            pl.BlockSpec((HPG, 2, 2 * tile), lambda g, i: (g, 0, 0)),
            pl.BlockSpec((rows, LANES), lambda g, i: (i, g)),
            pl.BlockSpec((rows, HPG * HEAD_DIM), lambda g, i: (i, g)),
        ],
        out_specs=pl.BlockSpec((rows, HPG * HEAD_DIM), lambda g, i: (i, g)),
        scratch_shapes=[
            pltpu.VMEM((stacked_rows, 2 * HEAD_DIM), BF16),
            pltpu.VMEM((stacked_rows, LANES), F32),
            pltpu.VMEM((stacked_rows, 2 * HEAD_DIM), F32),
            pltpu.VMEM((HPG, tile, 3 * tile), F32),
        ],
        compiler_params=_params("arbitrary", "arbitrary"),
    )(proj, neg_mask, proj, proj, proj, proj, blk_onehot, periods, gates, o_c)


def _rel_bucket_np(dist):
    n = np.maximum(dist, 0)
    max_exact = REL_BUCKETS // 2
    large = max_exact + (np.log(np.maximum(n, 1).astype(np.float32) / np.float32(max_exact))
                         / np.float32(math.log(REL_MAX_DIST / max_exact))
                         * np.float32(REL_BUCKETS - max_exact)).astype(np.int32)
    large = np.minimum(large, REL_BUCKETS - 1)
    return np.where(n < max_exact, n, large).astype(np.int32)


def _far_distance():
    d = np.arange(4 * REL_MAX_DIST)
    b = _rel_bucket_np(d)
    assert b[-1] == REL_BUCKETS - 1
    return int(np.max(np.nonzero(b != REL_BUCKETS - 1)[0])) + 1


def _bias_table(rel_bias, dist):
    bucket = jnp.asarray(_rel_bucket_np(dist))
    shifted = (rel_bias[bucket] - rel_bias[REL_BUCKETS - 1]) * LOG2E
    tab = jnp.where(jnp.asarray(dist >= 0)[..., None], shifted, NEG)
    return jnp.moveaxis(tab, -1, 0).astype(F32)


def _bias_period(rel_bias, offset, t):
    x = np.arange(2 * t)
    return _bias_table(rel_bias, np.where(x < t, offset - x, offset + 2 * t - x))


def _nsa_mixer(h, hn, rel_bias, w_in, w_out, layer, pos_k, w1_k, w2_k, pos_v, w1_v, w2_v, next_norm_g):
    s = h.shape[0]
    n_sel = s // SEL_BLOCK
    ncp = s // CMP_STRIDE
    tile = ATT_TILE
    far = _far_distance()
    assert n_sel <= LANES and s % CMP_TQ == 0 and s % (Q_TILES * tile) == 0 and FAR_TILES % Q_TILES == 0
    assert far <= tile and far <= CMP_STRIDE * CMP_PAD - CMP_BLOCK + 1 + CMP_STRIDE

    col_scale = jnp.where(jnp.arange(PROJ_MAIN) < Q_WIDTH, HEAD_DIM ** -0.5 * LOG2E, 1.0).astype(F32)[None, :]
    w_in_t = jnp.swapaxes(w_in, 1, 2)
    proj = _matmul(_ep_scale, hn, w_in_t, layer, PROJ_MAIN, BF16, 2048, 512, row_extra=col_scale,
                   w_rows_are_outputs=True)
    w_gate = w_in_t[layer, PROJ_MAIN:].reshape(N_GATES, N_KV_GROUPS, HPG, D_MODEL).transpose(1, 0, 2, 3)
    w_gate = w_gate.reshape(N_KV_GROUPS, N_GATES * HPG, D_MODEL)
    w_gate = jnp.pad(w_gate, ((0, 0), (0, LANES - N_GATES * HPG), (0, 0))).reshape(1, N_KV_GROUPS * LANES, D_MODEL)
    gates = _matmul(jax.nn.sigmoid, hn, w_gate, 0, N_KV_GROUPS * LANES, F32, 2048, 512, w_rows_are_outputs=True)

    w1 = jnp.stack([w1_k, w1_v]).astype(BF16)
    w2 = jnp.stack([w2_k, w2_v]).astype(BF16)
    pos = jnp.stack([pos_k.reshape(1, -1), pos_v.reshape(1, -1)]).astype(BF16)
    pos = jnp.broadcast_to(pos, (2, SUBLANES, CMP_BLOCK * HEAD_DIM))
    cc = _compress(proj, w1, w2, pos)
    kcc = jnp.pad(cc[0], ((0, 0), (CMP_PAD, 0), (0, 0)))
    vcct = jnp.pad(cc[1], ((0, 0), (CMP_PAD, 0), (0, 0))).transpose(0, 2, 1)
    n_rows = ncp + CMP_PAD

    nb = CMP_TQ // CMP_STRIDE + CMP_PAD
    first_dist = -CMP_STRIDE * (nb - 1 - CMP_PAD) - (CMP_BLOCK - 1)
    by_dist = _bias_table(rel_bias, first_dist + np.arange(CMP_TQ + CMP_STRIDE * (nb - 1)))
    band = jnp.stack([by_dist[:, CMP_STRIDE * (nb - 1 - b):CMP_STRIDE * (nb - 1 - b) + CMP_TQ] for b in range(nb)], axis=1)
    ratio = SEL_BLOCK // CMP_STRIDE
    lo = CMP_BLOCK // CMP_STRIDE - 1
    c_of_row = np.arange(n_rows)[None, :] - CMP_PAD
    j_of = np.arange(LANES)[:, None]
    mt = ((c_of_row >= ratio * j_of - lo) & (c_of_row <= ratio * j_of + ratio - 1)
          & (c_of_row >= 0) & (c_of_row <= ncp - 2) & (j_of < n_sel)).astype(np.float32)
    o_c, neg_mask = _cmp_attn(proj, kcc, vcct, band, jnp.asarray(mt, BF16), CMP_TQ)

    onehot = (np.arange(s)[:, None] // SEL_BLOCK == np.arange(LANES)[None, :]).astype(np.float32)
    periods = jnp.stack([_bias_period(rel_bias, tile, tile), _bias_period(rel_bias, 0, tile)], axis=1)
    o = _sel_win(proj, neg_mask, jnp.asarray(onehot, BF16), periods, gates, o_c, tile)
    return _matmul_resid_norm(o, w_out, layer, h, next_norm_g, 256)


def _s5_kernel(x_ref, lam_ref, bt_ref, c_ref, d_ref, y_ref, vt, yt, tmask, *, chunk):
    L = chunk
    half = L // 2
    width = L * SSM_GROUP
    n_chunks = x_ref.shape[0] // L
    by_step = jnp.swapaxes(x_ref[...].reshape(n_chunks, L, LANES), 0, 1)
    for tau in range(L):
        vt[tau] = by_step[tau].T

    @pl.when(pl.program_id(0) == 0)
    def _():
        dst = lax.broadcasted_iota(jnp.int32, (width, width), 0) // SSM_GROUP
        src = lax.broadcasted_iota(jnp.int32, (width, width), 1) // SSM_GROUP
        tmask[...] = jnp.where(src <= dst, 1.0, 0.0)

    def cmul(xr, xi, yr, yi):
        return xr * yr - xi * yi, xr * yi + xi * yr

    n2 = 2 * SSM_STATE
    sub = lax.broadcasted_iota(jnp.int32, (SUBLANES, 1), 0)
    consts = jnp.where(sub == 0, 1.0, jnp.where(sub == 1, float(half + 1), jnp.where(sub == 2, float(half - 1), float(L))))
    tau_col = lax.broadcasted_iota(jnp.int32, (L, 1), 0).astype(F32)
    row = lax.broadcasted_iota(jnp.int32, (n_chunks, n2), 0)
    conj = jnp.where(lax.broadcasted_iota(jnp.int32, (1, n2), 1) < SSM_STATE, 1.0, -1.0)

    def group(g):
        lam = lam_ref[g]
        a_re, a_im = lam[0:1], lam[1:2]
        dt = jnp.exp(lam[2:3])
        log_re, log_im = a_re * dt, a_im * dt

        def cpow(e):
            mag, cos, sin = jnp.exp(e * log_re), jnp.cos(e * log_im), jnp.sin(e * log_im)
            inv = 1.0 / mag
            return mag * cos, mag * sin, inv * cos, -(inv * sin)

        kr, ki, _, _ = cpow(consts)
        pr, pi = kr[0:1] - 1.0, ki[0:1]
        den = a_re * a_re + a_im * a_im
        cf_r = (pr * a_re + pi * a_im) / den
        cf_i = (pi * a_re - pr * a_im) / den
        bt_c, bt_s = bt_ref[g, 0], bt_ref[g, 1]
        bb_c = cf_r * bt_c + cf_i * bt_s
        bb_s = cf_r * bt_s - cf_i * bt_c
        c_c, c_s = c_ref[g, 0], c_ref[g, 1]

        e1r, e1i, e2r, e2i = cpow(tau_col - half)
        e3r, e3i = cmul(e1r, e1i, kr[1:2], ki[1:2])
        e4r, e4i = cmul(e2r, e2i, kr[2:3], ki[2:3])

        def outer(er, ei, w_c, w_s):
            return (er[:, None, :] * w_c[None, :, :] + ei[:, None, :] * w_s[None, :, :]).reshape(width, n2)

        qm = outer(e1r, e1i, c_c, c_s)
        km_conj = outer(e2r, e2i, bb_c * conj, bb_s * conj)
        qc = outer(e3r, e3i, c_c, c_s)
        wz = outer(e4r, e4i, bb_c, bb_s)

        chans = pl.ds(pl.multiple_of(g * SSM_GROUP, SSM_GROUP), SSM_GROUP)
        ub = vt[:, chans, :].reshape(width, n_chunks).astype(BF16)
        yield

        tt = _dot_nt(qm.astype(BF16), km_conj.astype(BF16))
        x = _dot(wz.T.astype(BF16), ub).T
        yield
        y = _dot((tt * tmask[...]).astype(BF16), ub)
        p_r, p_i = kr[3:4], ki[3:4]
        d = 1
        while d < n_chunks:
            s = jnp.where(row >= d, pltpu.roll(x, d, 0), 0.0)
            x = x + s * p_r - pltpu.roll(s, SSM_STATE, 1) * (p_i * conj)
            p_r, p_i = cmul(p_r, p_i, p_r, p_i)
            d *= 2
        prev_conj = jnp.where(row >= 1, pltpu.roll(x, 1, 0), 0.0) * conj
        yield
        y = y + _dot_nt(qc.astype(BF16), prev_conj.astype(BF16))
        yt[:, chans, :] = y.reshape(L, SSM_GROUP, n_chunks)

    def group_batch(k, carry):
        running = [group(S5_GROUPS_PER_TRIP * k + j) for j in range(S5_GROUPS_PER_TRIP)]
        while running:
            running = [gen for gen in running if next(gen, running) is not running]
        return carry

    lax.fori_loop(0, LANES // SSM_GROUP // S5_GROUPS_PER_TRIP, group_batch, 0)
    by_chunk = jnp.swapaxes(jnp.stack([yt[tau].T for tau in range(L)], axis=0), 0, 1)
    y = by_chunk.reshape(n_chunks * L, LANES) + d_ref[...] * x_ref[...]
    y_ref[...] = jax.nn.gelu(y).astype(y_ref.dtype)


def _s5_mixer(h, hn, a_re, a_im, log_dt, b_re, b_im, c_re, c_im, d_skip, w_glu, layer):
    s = h.shape[0]
    L = S5_CHUNK
    n_chunks = s // L
    width = L * SSM_GROUP
    gpb = LANES // SSM_GROUP
    def packed(re, im):
        return jnp.stack([jnp.concatenate([re, im], -1), jnp.concatenate([-im, re], -1)], axis=1).astype(F32)

    lam = jnp.stack([a_re, a_im, jnp.broadcast_to(log_dt[:, None], a_re.shape)], axis=1).astype(F32)
    lam = jnp.concatenate([lam, lam], axis=-1)
    bt = packed(b_re.transpose(0, 2, 1), b_im.transpose(0, 2, 1))
    cc = packed(c_re, c_im)
    dd = d_skip.astype(F32).reshape(1, D_MODEL)
    y = pl.pallas_call(
        functools.partial(_s5_kernel, chunk=L),
        out_shape=jax.ShapeDtypeStruct((s, D_MODEL), BF16),
        grid=(SSM_GROUPS // gpb,),
        in_specs=[
            pl.BlockSpec((s, LANES), lambda b: (0, b)),
            pl.BlockSpec((gpb, 3, 2 * SSM_STATE), lambda b: (b, 0, 0)),
            pl.BlockSpec((gpb, 2, SSM_GROUP, 2 * SSM_STATE), lambda b: (b, 0, 0, 0)),
            pl.BlockSpec((gpb, 2, SSM_GROUP, 2 * SSM_STATE), lambda b: (b, 0, 0, 0)),
            pl.BlockSpec((1, LANES), lambda b: (0, b)),
        ],
        out_specs=pl.BlockSpec((s, LANES), lambda b: (0, b)),
        scratch_shapes=[
            pltpu.VMEM((L, LANES, n_chunks), F32),
            pltpu.VMEM((L, LANES, n_chunks), F32),
            pltpu.VMEM((width, width), F32),
        ],
        compiler_params=_params("arbitrary"),
    )(hn, lam, bt, cc, dd)
    return _matmul(_ep_glu_resid, y, w_glu, layer, D_MODEL, F32, 1024, 512, b_halves=2, tile_extra=h)


def kernel(x, rel_bias, mix_norm_g, ffn_norm_g, final_norm_g, nsa_w_in, nsa_w_out, cmp_pos_k, cmp_w1_k, cmp_w2_k, cmp_pos_v, cmp_w1_v, cmp_w2_v, s5_A_re, s5_A_im, s5_log_dt, s5_B_re, s5_B_im, s5_C_re, s5_C_im, s5_D, s5_w_glu, ffn_w_in, ffn_w_out):
    assert x.shape[0] == 1
    h = x[0]
    hn = _rmsnorm(h, mix_norm_g[0], BF16)
    h, hn = _nsa_mixer(h, hn, rel_bias, nsa_w_in, nsa_w_out, 0, cmp_pos_k[0], cmp_w1_k[0], cmp_w2_k[0],
                       cmp_pos_v[0], cmp_w1_v[0], cmp_w2_v[0], ffn_norm_g[0])
    h = _swiglu_ffn(h, hn, ffn_w_in, ffn_w_out, 0)
    hn = _rmsnorm(h, mix_norm_g[1], F32)
    h = _s5_mixer(h, hn, s5_A_re[0], s5_A_im[0], s5_log_dt[0], s5_B_re[0], s5_B_im[0], s5_C_re[0], s5_C_im[0],
                  s5_D[0], s5_w_glu, 0)
    h = _swiglu_ffn(h, _rmsnorm(h, ffn_norm_g[1], BF16), ffn_w_in, ffn_w_out, 1)
    return _rmsnorm(h, final_norm_g, x.dtype)[None]
```

```python
import functools
import math

import numpy as np
import jax
import jax.numpy as jnp
from jax import lax
from jax.experimental import pallas as pl
from jax.experimental.pallas import tpu as pltpu

D_MODEL = 2048
N_HEADS = 16
HEAD_DIM = 128
N_KV_GROUPS = 4
HPG = N_HEADS // N_KV_GROUPS
CMP_BLOCK = 32
CMP_STRIDE = 16
CMP_HIDDEN = 2 * HEAD_DIM
SEL_BLOCK = 64
SEL_TOPK = 16
N_FORCED = 3
WINDOW = 512
N_GATES = 3
KV_WIDTH = N_KV_GROUPS * HEAD_DIM
Q_WIDTH = N_HEADS * HEAD_DIM
PROJ_MAIN = Q_WIDTH + 6 * KV_WIDTH
REL_BUCKETS = 32
REL_MAX_DIST = 128
SSM_GROUP = 16
SSM_GROUPS = D_MODEL // SSM_GROUP
SSM_STATE = 64
D_FF = ((8 * D_MODEL + 2) // 3 + 255) // 256 * 256
RMS_EPS = 1e-6
NEG = -1e30
LOG2E = math.log2(math.e)

LANES = 128
SUBLANES = 8
VMEM_LIMIT = 52 * 1024 * 1024

ATT_TILE = WINDOW // 2
FAR_TILES = 4
Q_TILES = 2
CMP_TQ = 512
CMP_PAD = 8
S5_CHUNK = 32
S5_GROUPS_PER_TRIP = 4

BF16 = jnp.bfloat16
F32 = jnp.float32


def _dot(a, b):
    return jnp.dot(a, b, preferred_element_type=F32)


def _dot_nt(a, b):
    return lax.dot_general(a, b, (((1,), (1,)), ((), ())), preferred_element_type=F32)


def _params(*sem):
    return pltpu.CompilerParams(dimension_semantics=sem, vmem_limit_bytes=VMEM_LIMIT)


def _rmsnorm_kernel(x_ref, g_ref, o_ref):
    x = x_ref[...]
    ms = jnp.mean(x * x, axis=-1, keepdims=True)
    o_ref[...] = (x * lax.rsqrt(ms + RMS_EPS) * g_ref[...]).astype(o_ref.dtype)


def _rmsnorm(x, g, out_dtype, tm=512):
    s, d = x.shape
    return pl.pallas_call(
        _rmsnorm_kernel,
        out_shape=jax.ShapeDtypeStruct((s, d), out_dtype),
        grid=(s // tm,),
        in_specs=[pl.BlockSpec((tm, d), lambda i: (i, 0)), pl.BlockSpec((1, d), lambda i: (0, 0))],
        out_specs=pl.BlockSpec((tm, d), lambda i: (i, 0)),
        compiler_params=_params("parallel"),
    )(x, g.reshape(1, d))


def _mm_kernel(*refs, n_b, n_extra, epilogue, w_rows_are_outputs):
    a_ref = refs[0]
    b_refs = refs[1:1 + n_b]
    extra_refs = refs[1 + n_b:1 + n_b + n_extra]
    o_ref = refs[1 + n_b + n_extra]
    w_scr = refs[2 + n_b + n_extra:]

    @pl.when(pl.program_id(1) == 0)
    def _():
        for b_ref, w in zip(b_refs, w_scr):
            w[...] = b_ref[...].astype(BF16)

    a = a_ref[...]
    z = [(_dot_nt if w_rows_are_outputs else _dot)(a, w[...]) for w in w_scr]
    o_ref[...] = epilogue(*z, *[e[...] for e in extra_refs]).astype(o_ref.dtype)


def _ep_scale(z, scale):
    return z * scale


def _ep_resid(z, resid):
    return resid + z


def _ep_swiglu(za, zb):
    return jax.nn.silu(za) * zb


def _ep_glu_resid(za, zb, resid):
    return resid + za * jax.nn.sigmoid(zb)


def _matmul(epilogue, a, b, layer, n_out, out_dtype, tm, tn, *, b_halves=1, row_extra=None, tile_extra=None,
            w_rows_are_outputs=False):
    m, k = a.shape
    nj = n_out // tn
    estimate = (b_halves * k * tn * (2 * 4 + 2) + 2 * tm * k * 2 + 4 * tm * tn * 4 + b_halves * tm * tn * 4)
    b_mode = pl.Buffered(1) if estimate > VMEM_LIMIT else None
    in_specs = [pl.BlockSpec((tm, k), lambda j, i: (i, 0))]
    args = [a]
    for half in range(b_halves):
        if w_rows_are_outputs:
            spec = pl.BlockSpec((None, tn, k), functools.partial(lambda j, i, o: (layer, j + o, 0), o=half * nj),
                                pipeline_mode=b_mode)
        else:
            spec = pl.BlockSpec((None, k, tn), functools.partial(lambda j, i, o: (layer, 0, j + o), o=half * nj),
                                pipeline_mode=b_mode)
        in_specs.append(spec)
        args.append(b)
    extras = []
    if row_extra is not None:
        in_specs.append(pl.BlockSpec((1, tn), lambda j, i: (0, j)))
        extras.append(row_extra)
    if tile_extra is not None:
        in_specs.append(pl.BlockSpec((tm, tn), lambda j, i: (i, j)))
        extras.append(tile_extra)
    kern = functools.partial(_mm_kernel, n_b=b_halves, n_extra=len(extras), epilogue=epilogue,
                             w_rows_are_outputs=w_rows_are_outputs)
    return pl.pallas_call(
        kern,
        out_shape=jax.ShapeDtypeStruct((m, n_out), out_dtype),
        grid=(nj, m // tm),
        in_specs=in_specs,
        out_specs=pl.BlockSpec((tm, tn), lambda j, i: (i, j)),
        scratch_shapes=[pltpu.VMEM((tn, k) if w_rows_are_outputs else (k, tn), BF16) for _ in range(b_halves)],
        compiler_params=_params("arbitrary", "arbitrary"),
    )(*args, *extras)


def _mm_resid_norm_kernel(a_ref, b_ref, r_ref, g_ref, h_ref, hn_ref, w_scr):
    @pl.when(pl.program_id(0) == 0)
    def _():
        w_scr[...] = b_ref[...].astype(BF16)

    h = r_ref[...] + _dot(a_ref[...], w_scr[...])
    h_ref[...] = h
    ms = jnp.mean(h * h, axis=-1, keepdims=True)
    hn_ref[...] = (h * lax.rsqrt(ms + RMS_EPS) * g_ref[...]).astype(hn_ref.dtype)


def _matmul_resid_norm(a, b, layer, resid, g, tm):
    m, k = a.shape
    n = b.shape[2]
    return pl.pallas_call(
        _mm_resid_norm_kernel,
        out_shape=(jax.ShapeDtypeStruct((m, n), F32), jax.ShapeDtypeStruct((m, n), BF16)),
        grid=(m // tm,),
        in_specs=[
            pl.BlockSpec((tm, k), lambda i: (i, 0)),
            pl.BlockSpec((None, k, n), lambda i: (layer, 0, 0), pipeline_mode=pl.Buffered(1)),
            pl.BlockSpec((tm, n), lambda i: (i, 0)),
            pl.BlockSpec((1, n), lambda i: (0, 0)),
        ],
        out_specs=(pl.BlockSpec((tm, n), lambda i: (i, 0)), pl.BlockSpec((tm, n), lambda i: (i, 0))),
        scratch_shapes=[pltpu.VMEM((k, n), BF16)],
        compiler_params=_params("arbitrary"),
    )(a, b, resid, g.reshape(1, n))


def _swiglu_ffn(h, hn, w_in, w_out, layer):
    act = _matmul(_ep_swiglu, hn, w_in, layer, D_FF, BF16, 1024, 512, b_halves=2)
    return _matmul(_ep_resid, act, w_out, layer, D_MODEL, F32, 512, 512, tile_extra=h)


def _compress_kernel(x_ref, w1_ref, w2_ref, pos_ref, o_ref):
    ncp = x_ref.shape[0] // CMP_STRIDE
    top = jnp.zeros((ncp, CMP_HIDDEN), F32)
    bot = jnp.zeros((ncp, CMP_HIDDEN), F32)
    by_offset = jnp.swapaxes(x_ref[...].reshape(ncp, CMP_STRIDE, HEAD_DIM), 0, 1)
    for r in range(CMP_STRIDE):
        x_r = by_offset[r]
        top = top + _dot(x_r, w1_ref[0, r * HEAD_DIM:(r + 1) * HEAD_DIM, :])
        bot = bot + _dot(x_r, w1_ref[0, (CMP_STRIDE + r) * HEAD_DIM:(CMP_STRIDE + r + 1) * HEAD_DIM, :])
    posb = _dot(pos_ref[0], w1_ref[0])[0:1]
    hid = top + pltpu.roll(bot, ncp - 1, 0) + posb
    o_ref[0, 0] = _dot(jax.nn.gelu(hid).astype(BF16), w2_ref[0]).astype(o_ref.dtype)


def _compress(proj, w1, w2, pos):
    s = proj.shape[0]
    ncp = s // CMP_STRIDE
    kc_blk = Q_WIDTH // HEAD_DIM
    return pl.pallas_call(
        _compress_kernel,
        out_shape=jax.ShapeDtypeStruct((2, N_KV_GROUPS, ncp, HEAD_DIM), BF16),
        grid=(2, N_KV_GROUPS),
        in_specs=[
            pl.BlockSpec((s, HEAD_DIM), lambda kv, gi: (0, kc_blk + kv * N_KV_GROUPS + gi)),
            pl.BlockSpec((1, CMP_BLOCK * HEAD_DIM, CMP_HIDDEN), lambda kv, gi: (kv, 0, 0)),
            pl.BlockSpec((1, CMP_HIDDEN, HEAD_DIM), lambda kv, gi: (kv, 0, 0)),
            pl.BlockSpec((1, SUBLANES, CMP_BLOCK * HEAD_DIM), lambda kv, gi: (kv, 0, 0)),
        ],
        out_specs=pl.BlockSpec((1, 1, ncp, HEAD_DIM), lambda kv, gi: (kv, gi, 0, 0)),
        compiler_params=_params("parallel", "parallel"),
    )(proj, w1, w2, pos)


def _cmp_attn_kernel(q_ref, kcc_ref, vcct_ref, band_ref, mt_ref, oc_ref, nm_ref, s_scr, *, tq, nb):
    i = pl.program_id(1)
    band_start = pl.multiple_of(i * (tq // CMP_STRIDE), SUBLANES)
    n_rows = kcc_ref.shape[1]
    n_blk = mt_ref.shape[0]

    def body(n_r, n_b):
        kcc = kcc_ref[0, :n_r, :]
        vcct = vcct_ref[0, :, :n_r]
        rows = lax.broadcasted_iota(jnp.int32, (n_r, tq), 0)
        row_bias = jnp.where(rows >= CMP_PAD, jnp.where(rows < band_start + nb, 0.0, NEG), NEG)
        imp = jnp.zeros((n_r, tq), F32)

        def scores(h):
            qh = q_ref[:, h * HEAD_DIM:(h + 1) * HEAD_DIM]
            s_scr[h, :n_r, :] = _dot_nt(kcc, qh) + row_bias
            s_scr[h, pl.ds(band_start, nb), :] += band_ref[h]

        scores(0)
        for h in range(HPG):
            if h + 1 < HPG:
                scores(h + 1)
            s = s_scr[h, :n_r, :]
            m = jnp.maximum(jnp.max(s, axis=0, keepdims=True), 0.5 * NEG)
            e = jnp.exp2(s - m)
            l = jnp.sum(e, axis=0, keepdims=True)
            pn = e * (1.0 / jnp.where(l > 0.0, l, 1.0))
            imp = imp + pn
            oct_h = _dot(vcct, pn.astype(BF16))
            oc_ref[:, h * HEAD_DIM:(h + 1) * HEAD_DIM] = oct_h.T
        mt = mt_ref[:n_b, :n_r]
        hi = imp.astype(BF16)
        rest = imp - hi.astype(F32)
        mid = rest.astype(BF16)
        lo = (rest - mid.astype(F32)).astype(BF16)
        slc = _dot(mt, hi) + _dot(mt, mid) + _dot(mt, lo)
        jj = lax.broadcasted_iota(jnp.int32, (n_b, tq), 0)
        t_blk = (i * tq + lax.broadcasted_iota(jnp.int32, (n_b, tq), 1)) // SEL_BLOCK
        forced = jnp.where(jj == 0, 1, jnp.where(jj == t_blk, 1, jnp.where(jj == t_blk - 1, 1, 0)))
        picked = jnp.where(forced == 1, 1.0, 0.0)
        score = jnp.where(forced == 1, -jnp.inf, jnp.where(jj <= t_blk, slc, NEG))
        for _ in range(SEL_TOPK - N_FORCED):
            best = jnp.max(score, axis=0, keepdims=True)
            first = jnp.min(jnp.where(score == best, jj, n_blk), axis=0, keepdims=True)
            hit = jj == first
            picked = jnp.where(hit, 1.0, picked)
            score = jnp.where(hit, -jnp.inf, score)
        neg_mask = jnp.where(picked > 0.5, jnp.where(jj <= t_blk, 0.0, NEG), NEG)
        if n_b < n_blk:
            neg_mask = jnp.concatenate([neg_mask, jnp.full((n_blk - n_b, tq), NEG, F32)], axis=0)
        nm_ref[...] = neg_mask.T.astype(nm_ref.dtype)

    step = LANES
    extents = list(range(step, n_rows, step)) + [n_rows]
    need_rows = band_start + nb
    for v, n_r in enumerate(extents):
        lower = extents[v - 1] if v else 0
        n_b = min(n_blk, -(-(n_r * CMP_STRIDE // SEL_BLOCK) // SUBLANES) * SUBLANES)
        pl.when(jnp.logical_and(need_rows > lower, need_rows <= n_r))(functools.partial(body, n_r, n_b))


def _cmp_attn(proj, kcc, vcct, band, mt, tq):
    s = proj.shape[0]
    n_rows = kcc.shape[1]
    nb = band.shape[1]
    kern = functools.partial(_cmp_attn_kernel, tq=tq, nb=nb)
    return pl.pallas_call(
        kern,
        out_shape=(jax.ShapeDtypeStruct((s, Q_WIDTH), F32),
                   jax.ShapeDtypeStruct((s, N_KV_GROUPS * LANES), BF16)),
        grid=(N_KV_GROUPS, s // tq),
        in_specs=[
            pl.BlockSpec((tq, HPG * HEAD_DIM), lambda g, i: (i, g)),
            pl.BlockSpec((1, n_rows, HEAD_DIM), lambda g, i: (g, 0, 0)),
            pl.BlockSpec((1, HEAD_DIM, n_rows), lambda g, i: (g, 0, 0)),
            pl.BlockSpec((HPG, nb, tq), lambda g, i: (g, 0, 0)),
            pl.BlockSpec((LANES, n_rows), lambda g, i: (0, 0)),
        ],
        out_specs=(pl.BlockSpec((tq, HPG * HEAD_DIM), lambda g, i: (i, g)),
                   pl.BlockSpec((tq, LANES), lambda g, i: (i, g))),
        scratch_shapes=[pltpu.VMEM((HPG, n_rows, tq), F32)],
        compiler_params=_params("parallel", "parallel"),
    )(proj, kcc, vcct, band, mt)


def _lane_tile(x, width):
    return jnp.concatenate([x] * (width // x.shape[1]), axis=1)


def _softmax_update(s, v_aug, m_ref, acc_ref, rows):
    m_prev = m_ref[rows, :]
    m_next = jnp.maximum(m_prev, jnp.max(s, axis=1, keepdims=True))
    alpha = jnp.exp2(m_prev - m_next)
    p = jnp.exp2(s - _lane_tile(m_next, s.shape[1]))
    acc_ref[rows, :] = _lane_tile(alpha, acc_ref.shape[1]) * acc_ref[rows, :] + _dot(p.astype(BF16), v_aug)
    m_ref[rows, :] = m_next


def _softmax_once(s, v_aug):
    m = jnp.broadcast_to(jnp.max(s, axis=1, keepdims=True), (s.shape[0], LANES))
    p = jnp.exp2(s - _lane_tile(m, s.shape[1]))
    return _dot(p.astype(BF16), v_aug)


def _sel_win_kernel(q_ref, nm_ref, ks_ref, vs_ref, kw_ref, vw_ref, blk_ref, per_ref, g_ref, oc_ref, o_ref,
                    q4, m_s, a_s, bias_ref, *, tile):
    step = pl.program_id(1)
    t = tile
    first = step * Q_TILES

    @pl.when(step == 0)
    def _():
        tl = lax.broadcasted_iota(jnp.int32, (t, t), 0)
        kl = lax.broadcasted_iota(jnp.int32, (t, t), 1)
        edge = jnp.where(tl < kl, 0.0, NEG)
        for h in range(HPG):
            bias_ref[h, :, :t] = edge
            for which in range(2):
                rows = jnp.broadcast_to(per_ref[h, which:which + 1, :], (t, 2 * t))
                table = pltpu.roll(rows, 0, 1, stride=1, stride_axis=0)
                bias_ref[h, :, (which + 1) * t:(which + 2) * t] = table[:, :t]
    def stacked(qt, h):
        return slice((qt * HPG + h) * t, (qt * HPG + h + 1) * t)

    def tile_rows(qt):
        return slice(qt * t, (qt + 1) * t)

    for qt in range(Q_TILES):
        for h in range(HPG):
            q4[stacked(qt, h), :HEAD_DIM] = q_ref[tile_rows(qt), h * HEAD_DIM:(h + 1) * HEAD_DIM]
            q4[stacked(qt, h), HEAD_DIM:] = nm_ref[tile_rows(qt), :]
    m_s[...] = jnp.full(m_s.shape, -3e38, F32)
    a_s[...] = jnp.zeros(a_s.shape, F32)

    def sel_operands(row0, n_tiles):
        keys = pl.ds(row0, n_tiles * t)
        k_aug = jnp.concatenate([ks_ref[keys, :], blk_ref[keys, :]], axis=1)
        v_aug = jnp.concatenate([vs_ref[keys, :], jnp.ones((n_tiles * t, LANES), BF16)], axis=1)
        return k_aug, v_aug

    def sel_pass(qt, operands, n_biased):
        k_aug, v_aug = operands

        def score(h):
            s = _dot_nt(q4[stacked(qt, h), :], k_aug)
            plain = s.shape[1] - n_biased * t
            if n_biased:
                near = s[:, plain:] + bias_ref[h, :, (3 - n_biased) * t:]
                s = jnp.concatenate([s[:, :plain], near], axis=1) if plain else near
            return s

        def consume(h, s):
            _softmax_update(s, v_aug, m_s, a_s, stacked(qt, h))

        return score, consume

    def win_pass(qt, row0, n_tiles):
        keys = pl.ds(row0, n_tiles * t)
        v_aug = jnp.concatenate([vw_ref[keys, :], jnp.ones((n_tiles * t, LANES), BF16)], axis=1)

        def score(h):
            return _dot_nt(q4[stacked(qt, h), :HEAD_DIM], kw_ref[keys, :]) + bias_ref[h, :, (3 - n_tiles) * t:]

        def consume(h, s):
            acc_w = _softmax_once(s, v_aug)
            acc_s = a_s[stacked(qt, h), :]
            o_w = acc_w[:, :HEAD_DIM] * (1.0 / acc_w[:, HEAD_DIM:])
            o_s = acc_s[:, :HEAD_DIM] * (1.0 / acc_s[:, HEAD_DIM:])
            cols = slice(h * HEAD_DIM, (h + 1) * HEAD_DIM)
            gates = g_ref[tile_rows(qt), :]
            o = (gates[:, h:h + 1] * oc_ref[tile_rows(qt), cols] + gates[:, HPG + h:HPG + h + 1] * o_s
                 + gates[:, 2 * HPG + h:2 * HPG + h + 1] * o_w)
            o_ref[tile_rows(qt), cols] = o.astype(o_ref.dtype)

        return score, consume

    def run(passes):
        work = [(score, consume, h0) for score, consume in passes for h0 in range(0, HPG, 2)]
        ahead = [work[0][0](h) for h in (0, 1)]
        for n, (_, consume, h0) in enumerate(work):
            now = ahead
            if n + 1 < len(work):
                nxt_score, _, nxt_h0 = work[n + 1]
                ahead = [nxt_score(h) for h in (nxt_h0, nxt_h0 + 1)]
            for h, s in zip((h0, h0 + 1), now):
                consume(h, s)

    def far_passes(row0):
        operands = sel_operands(row0, FAR_TILES)
        return [sel_pass(qt, operands, 0) for qt in range(Q_TILES)]

    n_far = jnp.maximum(first - 1, 0)
    n_chunks = n_far // FAR_TILES
    chunk = FAR_TILES * t

    def far_pair(c, carry):
        row0 = pl.multiple_of(c * 2 * chunk, 2 * chunk)
        run(far_passes(row0) + far_passes(row0 + chunk))
        return carry

    lax.fori_loop(0, n_chunks // 2, far_pair, 0)

    @pl.when(n_chunks % 2 == 1)
    def _():
        run(far_passes(pl.multiple_of((n_chunks - 1) * chunk, chunk)))

    def tail(left):
        row0 = pl.multiple_of((n_far - left) * t, t)
        passes = []
        for qt in range(Q_TILES):
            n_sel = left + 2 + qt
            passes.append(sel_pass(qt, sel_operands(row0, n_sel), 2))
            passes.append(win_pass(qt, pl.multiple_of((first + qt - 2) * t, t), 3))
        run(passes)

    for left in range(FAR_TILES):
        if (left + 1) % Q_TILES == 0:
            pl.when(jnp.logical_and(step >= 1, n_far % FAR_TILES == left))(functools.partial(tail, left))

    @pl.when(step == 0)
    def _():
        passes = []
        for qt in range(Q_TILES):
            n_keys = min(qt + 1, 3)
            passes.append(sel_pass(qt, sel_operands(0, qt + 1), min(qt + 1, 2)))
            passes.append(win_pass(qt, (qt + 1 - n_keys) * t, n_keys))
        run(passes)


def _sel_win(proj, neg_mask, blk_onehot, periods, gates, o_c, tile):
    s = proj.shape[0]
    ks_blk = (Q_WIDTH + 2 * KV_WIDTH) // HEAD_DIM
    vs_blk = (Q_WIDTH + 3 * KV_WIDTH) // HEAD_DIM
    kw_blk = (Q_WIDTH + 4 * KV_WIDTH) // HEAD_DIM
    vw_blk = (Q_WIDTH + 5 * KV_WIDTH) // HEAD_DIM
    once = pl.Buffered(1)
    rows = Q_TILES * tile
    stacked_rows = HPG * rows
    kern = functools.partial(_sel_win_kernel, tile=tile)
    return pl.pallas_call(
        kern,
        out_shape=jax.ShapeDtypeStruct((s, Q_WIDTH), BF16),
        grid=(N_KV_GROUPS, s // rows),
        in_specs=[
            pl.BlockSpec((rows, HPG * HEAD_DIM), lambda g, i: (i, g)),
            pl.BlockSpec((rows, LANES), lambda g, i: (i, g)),
            pl.BlockSpec((s, HEAD_DIM), lambda g, i: (0, ks_blk + g), pipeline_mode=once),
            pl.BlockSpec((s, HEAD_DIM), lambda g, i: (0, vs_blk + g), pipeline_mode=once),
            pl.BlockSpec((s, HEAD_DIM), lambda g, i: (0, kw_blk + g), pipeline_mode=once),
            pl.BlockSpec((s, HEAD_DIM), lambda g, i: (0, vw_blk + g), pipeline_mode=once),
            pl.BlockSpec((s, LANES), lambda g, i: (0, 0), pipeline_mode=once),
            pl.BlockSpec((HPG, 2, 2 * tile), lambda g, i: (g, 0, 0)),
            pl.BlockSpec((rows, LANES), lambda g, i: (i, g)),
            pl.BlockSpec((rows, HPG * HEAD_DIM), lambda g, i: (i, g)),
        ],
        out_specs=pl.BlockSpec((rows, HPG * HEAD_DIM), lambda g, i: (i, g)),
        scratch_shapes=[
            pltpu.VMEM((stacked_rows, 2 * HEAD_DIM), BF16),
            pltpu.VMEM((stacked_rows, LANES), F32),
            pltpu.VMEM((stacked_rows, 2 * HEAD_DIM), F32),
            pltpu.VMEM((HPG, tile, 3 * tile), F32),
        ],
        compiler_params=_params("arbitrary", "arbitrary"),
    )(proj, neg_mask, proj, proj, proj, proj, blk_onehot, periods, gates, o_c)


def _rel_bucket_np(dist):
    n = np.maximum(dist, 0)
    max_exact = REL_BUCKETS // 2
    large = max_exact + (np.log(np.maximum(n, 1).astype(np.float32) / np.float32(max_exact))
                         / np.float32(math.log(REL_MAX_DIST / max_exact))
                         * np.float32(REL_BUCKETS - max_exact)).astype(np.int32)
    large = np.minimum(large, REL_BUCKETS - 1)
    return np.where(n < max_exact, n, large).astype(np.int32)


def _far_distance():
    d = np.arange(4 * REL_MAX_DIST)
    b = _rel_bucket_np(d)
    assert b[-1] == REL_BUCKETS - 1
    return int(np.max(np.nonzero(b != REL_BUCKETS - 1)[0])) + 1


def _bias_table(rel_bias, dist):
    bucket = jnp.asarray(_rel_bucket_np(dist))
    shifted = (rel_bias[bucket] - rel_bias[REL_BUCKETS - 1]) * LOG2E
    tab = jnp.where(jnp.asarray(dist >= 0)[..., None], shifted, NEG)
    return jnp.moveaxis(tab, -1, 0).astype(F32)


def _bias_period(rel_bias, offset, t):
    x = np.arange(2 * t)
    return _bias_table(rel_bias, np.where(x < t, offset - x, offset + 2 * t - x))


def _nsa_mixer(h, hn, rel_bias, w_in, w_out, layer, pos_k, w1_k, w2_k, pos_v, w1_v, w2_v, next_norm_g):
    s = h.shape[0]
    n_sel = s // SEL_BLOCK
    ncp = s // CMP_STRIDE
    tile = ATT_TILE
    far = _far_distance()
    assert n_sel <= LANES and s % CMP_TQ == 0 and s % (Q_TILES * tile) == 0 and FAR_TILES % Q_TILES == 0
    assert far <= tile and far <= CMP_STRIDE * CMP_PAD - CMP_BLOCK + 1 + CMP_STRIDE

    col_scale = jnp.where(jnp.arange(PROJ_MAIN) < Q_WIDTH, HEAD_DIM ** -0.5 * LOG2E, 1.0).astype(F32)[None, :]
    w_in_t = jnp.swapaxes(w_in, 1, 2)
    proj = _matmul(_ep_scale, hn, w_in_t, layer, PROJ_MAIN, BF16, 2048, 512, row_extra=col_scale,
                   w_rows_are_outputs=True)
    w_gate = w_in_t[layer, PROJ_MAIN:].reshape(N_GATES, N_KV_GROUPS, HPG, D_MODEL).transpose(1, 0, 2, 3)
    w_gate = w_gate.reshape(N_KV_GROUPS, N_GATES * HPG, D_MODEL)
    w_gate = jnp.pad(w_gate, ((0, 0), (0, LANES - N_GATES * HPG), (0, 0))).reshape(1, N_KV_GROUPS * LANES, D_MODEL)
    gates = _matmul(jax.nn.sigmoid, hn, w_gate, 0, N_KV_GROUPS * LANES, F32, 2048, 512, w_rows_are_outputs=True)

    w1 = jnp.stack([w1_k, w1_v]).astype(BF16)
    w2 = jnp.stack([w2_k, w2_v]).astype(BF16)
    pos = jnp.stack([pos_k.reshape(1, -1), pos_v.reshape(1, -1)]).astype(BF16)
    pos = jnp.broadcast_to(pos, (2, SUBLANES, CMP_BLOCK * HEAD_DIM))
    cc = _compress(proj, w1, w2, pos)
    kcc = jnp.pad(cc[0], ((0, 0), (CMP_PAD, 0), (0, 0)))
    vcct = jnp.pad(cc[1], ((0, 0), (CMP_PAD, 0), (0, 0))).transpose(0, 2, 1)
    n_rows = ncp + CMP_PAD

    nb = CMP_TQ // CMP_STRIDE + CMP_PAD
    first_dist = -CMP_STRIDE * (nb - 1 - CMP_PAD) - (CMP_BLOCK - 1)
    by_dist = _bias_table(rel_bias, first_dist + np.arange(CMP_TQ + CMP_STRIDE * (nb - 1)))
    band = jnp.stack([by_dist[:, CMP_STRIDE * (nb - 1 - b):CMP_STRIDE * (nb - 1 - b) + CMP_TQ] for b in range(nb)], axis=1)
    ratio = SEL_BLOCK // CMP_STRIDE
    lo = CMP_BLOCK // CMP_STRIDE - 1
    c_of_row = np.arange(n_rows)[None, :] - CMP_PAD
    j_of = np.arange(LANES)[:, None]
    mt = ((c_of_row >= ratio * j_of - lo) & (c_of_row <= ratio * j_of + ratio - 1)
          & (c_of_row >= 0) & (c_of_row <= ncp - 2) & (j_of < n_sel)).astype(np.float32)
    o_c, neg_mask = _cmp_attn(proj, kcc, vcct, band, jnp.asarray(mt, BF16), CMP_TQ)

    onehot = (np.arange(s)[:, None] // SEL_BLOCK == np.arange(LANES)[None, :]).astype(np.float32)
    periods = jnp.stack([_bias_period(rel_bias, tile, tile), _bias_period(rel_bias, 0, tile)], axis=1)
    o = _sel_win(proj, neg_mask, jnp.asarray(onehot, BF16), periods, gates, o_c, tile)
    return _matmul_resid_norm(o, w_out, layer, h, next_norm_g, 256)


def _s5_kernel(x_ref, lam_ref, bt_ref, c_ref, d_ref, y_ref, vt, yt, tmask, *, chunk):
    L = chunk
    half = L // 2
    width = L * SSM_GROUP
    n_chunks = x_ref.shape[0] // L
    by_step = jnp.swapaxes(x_ref[...].reshape(n_chunks, L, LANES), 0, 1)
    for tau in range(L):
        vt[tau] = by_step[tau].T

    @pl.when(pl.program_id(0) == 0)
    def _():
        dst = lax.broadcasted_iota(jnp.int32, (width, width), 0) // SSM_GROUP
        src = lax.broadcasted_iota(jnp.int32, (width, width), 1) // SSM_GROUP
        tmask[...] = jnp.where(src <= dst, 1.0, 0.0)

    def cmul(xr, xi, yr, yi):
        return xr * yr - xi * yi, xr * yi + xi * yr

    n2 = 2 * SSM_STATE
    sub = lax.broadcasted_iota(jnp.int32, (SUBLANES, 1), 0)
    consts = jnp.where(sub == 0, 1.0, jnp.where(sub == 1, float(half + 1), jnp.where(sub == 2, float(half - 1), float(L))))
    tau_col = lax.broadcasted_iota(jnp.int32, (L, 1), 0).astype(F32)
    row = lax.broadcasted_iota(jnp.int32, (n_chunks, n2), 0)
    conj = jnp.where(lax.broadcasted_iota(jnp.int32, (1, n2), 1) < SSM_STATE, 1.0, -1.0)

    def group(g):
        lam = lam_ref[g]
        a_re, a_im = lam[0:1], lam[1:2]
        dt = jnp.exp(lam[2:3])
        log_re, log_im = a_re * dt, a_im * dt

        def cpow(e):
            mag, cos, sin = jnp.exp(e * log_re), jnp.cos(e * log_im), jnp.sin(e * log_im)
            inv = 1.0 / mag
            return mag * cos, mag * sin, inv * cos, -(inv * sin)

        kr, ki, _, _ = cpow(consts)
        pr, pi = kr[0:1] - 1.0, ki[0:1]
        den = a_re * a_re + a_im * a_im
        cf_r = (pr * a_re + pi * a_im) / den
        cf_i = (pi * a_re - pr * a_im) / den
        bt_c, bt_s = bt_ref[g, 0], bt_ref[g, 1]
        bb_c = cf_r * bt_c + cf_i * bt_s
        bb_s = cf_r * bt_s - cf_i * bt_c
        c_c, c_s = c_ref[g, 0], c_ref[g, 1]

        e1r, e1i, e2r, e2i = cpow(tau_col - half)
        e3r, e3i = cmul(e1r, e1i, kr[1:2], ki[1:2])
        e4r, e4i = cmul(e2r, e2i, kr[2:3], ki[2:3])

        def outer(er, ei, w_c, w_s):
            return (er[:, None, :] * w_c[None, :, :] + ei[:, None, :] * w_s[None, :, :]).reshape(width, n2)

        qm = outer(e1r, e1i, c_c, c_s)
        km_conj = outer(e2r, e2i, bb_c * conj, bb_s * conj)
        qc = outer(e3r, e3i, c_c, c_s)
        wz = outer(e4r, e4i, bb_c, bb_s)

        chans = pl.ds(pl.multiple_of(g * SSM_GROUP, SSM_GROUP), SSM_GROUP)
        ub = vt[:, chans, :].reshape(width, n_chunks).astype(BF16)
        yield

        tt = _dot_nt(qm.astype(BF16), km_conj.astype(BF16))
        x = _dot(wz.T.astype(BF16), ub).T
        yield
        y = _dot((tt * tmask[...]).astype(BF16), ub)
        p_r, p_i = kr[3:4], ki[3:4]
        d = 1
        while d < n_chunks:
            s = jnp.where(row >= d, pltpu.roll(x, d, 0), 0.0)
            x = x + s * p_r - pltpu.roll(s, SSM_STATE, 1) * (p_i * conj)
            p_r, p_i = cmul(p_r, p_i, p_r, p_i)
            d *= 2
        prev_conj = jnp.where(row >= 1, pltpu.roll(x, 1, 0), 0.0) * conj
        yield
        y = y + _dot_nt(qc.astype(BF16), prev_conj.astype(BF16))
        yt[:, chans, :] = y.reshape(L, SSM_GROUP, n_chunks)

    def group_batch(k, carry):
        running = [group(S5_GROUPS_PER_TRIP * k + j) for j in range(S5_GROUPS_PER_TRIP)]
        while running:
            running = [gen for gen in running if next(gen, running) is not running]
        return carry

    lax.fori_loop(0, LANES // SSM_GROUP // S5_GROUPS_PER_TRIP, group_batch, 0)
    by_chunk = jnp.swapaxes(jnp.stack([yt[tau].T for tau in range(L)], axis=0), 0, 1)
    y = by_chunk.reshape(n_chunks * L, LANES) + d_ref[...] * x_ref[...]
    y_ref[...] = jax.nn.gelu(y).astype(y_ref.dtype)


def _s5_mixer(h, hn, a_re, a_im, log_dt, b_re, b_im, c_re, c_im, d_skip, w_glu, layer):
    s = h.shape[0]
    L = S5_CHUNK
    n_chunks = s // L
    width = L * SSM_GROUP
    gpb = LANES // SSM_GROUP
    def packed(re, im):
        return jnp.stack([jnp.concatenate([re, im], -1), jnp.concatenate([-im, re], -1)], axis=1).astype(F32)

    lam = jnp.stack([a_re, a_im, jnp.broadcast_to(log_dt[:, None], a_re.shape)], axis=1).astype(F32)
    lam = jnp.concatenate([lam, lam], axis=-1)
    bt = packed(b_re.transpose(0, 2, 1), b_im.transpose(0, 2, 1))
    cc = packed(c_re, c_im)
    dd = d_skip.astype(F32).reshape(1, D_MODEL)
    y = pl.pallas_call(
        functools.partial(_s5_kernel, chunk=L),
        out_shape=jax.ShapeDtypeStruct((s, D_MODEL), BF16),
        grid=(SSM_GROUPS // gpb,),
        in_specs=[
            pl.BlockSpec((s, LANES), lambda b: (0, b)),
            pl.BlockSpec((gpb, 3, 2 * SSM_STATE), lambda b: (b, 0, 0)),
            pl.BlockSpec((gpb, 2, SSM_GROUP, 2 * SSM_STATE), lambda b: (b, 0, 0, 0)),
            pl.BlockSpec((gpb, 2, SSM_GROUP, 2 * SSM_STATE), lambda b: (b, 0, 0, 0)),
            pl.BlockSpec((1, LANES), lambda b: (0, b)),
        ],
        out_specs=pl.BlockSpec((s, LANES), lambda b: (0, b)),
        scratch_shapes=[
            pltpu.VMEM((L, LANES, n_chunks), F32),
            pltpu.VMEM((L, LANES, n_chunks), F32),
            pltpu.VMEM((width, width), F32),
        ],
        compiler_params=_params("arbitrary"),
    )(hn, lam, bt, cc, dd)
    return _matmul(_ep_glu_resid, y, w_glu, layer, D_MODEL, F32, 1024, 512, b_halves=2, tile_extra=h)


def kernel(x, rel_bias, mix_norm_g, ffn_norm_g, final_norm_g, nsa_w_in, nsa_w_out, cmp_pos_k, cmp_w1_k, cmp_w2_k, cmp_pos_v, cmp_w1_v, cmp_w2_v, s5_A_re, s5_A_im, s5_log_dt, s5_B_re, s5_B_im, s5_C_re, s5_C_im, s5_D, s5_w_glu, ffn_w_in, ffn_w_out):
    assert x.shape[0] == 1
    h = x[0]
    hn = _rmsnorm(h, mix_norm_g[0], BF16)
    h, hn = _nsa_mixer(h, hn, rel_bias, nsa_w_in, nsa_w_out, 0, cmp_pos_k[0], cmp_w1_k[0], cmp_w2_k[0],
                       cmp_pos_v[0], cmp_w1_v[0], cmp_w2_v[0], ffn_norm_g[0])
    h = _swiglu_ffn(h, hn, ffn_w_in, ffn_w_out, 0)
    hn = _rmsnorm(h, mix_norm_g[1], F32)
    h = _s5_mixer(h, hn, s5_A_re[0], s5_A_im[0], s5_log_dt[0], s5_B_re[0], s5_B_im[0], s5_C_re[0], s5_C_im[0],
                  s5_D[0], s5_w_glu, 0)
    h = _swiglu_ffn(h, _rmsnorm(h, ffn_norm_g[1], BF16), ffn_w_in, ffn_w_out, 1)
    return _rmsnorm(h, final_norm_g, x.dtype)[None]
```

```python
import functools
import math

import numpy as np
import jax
import jax.numpy as jnp
from jax import lax
from jax.experimental import pallas as pl
from jax.experimental.pallas import tpu as pltpu

D_MODEL = 2048
N_HEADS = 16
HEAD_DIM = 128
N_KV_GROUPS = 4
HPG = N_HEADS // N_KV_GROUPS
CMP_BLOCK = 32
CMP_STRIDE = 16
CMP_HIDDEN = 2 * HEAD_DIM
SEL_BLOCK = 64
SEL_TOPK = 16
N_FORCED = 3
WINDOW = 512
N_GATES = 3
KV_WIDTH = N_KV_GROUPS * HEAD_DIM
Q_WIDTH = N_HEADS * HEAD_DIM
PROJ_MAIN = Q_WIDTH + 6 * KV_WIDTH
REL_BUCKETS = 32
REL_MAX_DIST = 128
SSM_GROUP = 16
SSM_GROUPS = D_MODEL // SSM_GROUP
SSM_STATE = 64
D_FF = ((8 * D_MODEL + 2) // 3 + 255) // 256 * 256
RMS_EPS = 1e-6
NEG = -1e30
LOG2E = math.log2(math.e)

LANES = 128
SUBLANES = 8
VMEM_LIMIT = 52 * 1024 * 1024

ATT_TILE = WINDOW // 2
FAR_TILES = 4
Q_TILES = 2
CMP_TQ = 256
CMP_PAD = 8
S5_CHUNK = 32
S5_GROUPS_PER_TRIP = 4

BF16 = jnp.bfloat16
F32 = jnp.float32


def _dot(a, b):
    return jnp.dot(a, b, preferred_element_type=F32)


def _dot_nt(a, b):
    return lax.dot_general(a, b, (((1,), (1,)), ((), ())), preferred_element_type=F32)


def _params(*sem):
    return pltpu.CompilerParams(dimension_semantics=sem, vmem_limit_bytes=VMEM_LIMIT)


def _rmsnorm_kernel(x_ref, g_ref, o_ref):
    x = x_ref[...]
    ms = jnp.mean(x * x, axis=-1, keepdims=True)
    o_ref[...] = (x * lax.rsqrt(ms + RMS_EPS) * g_ref[...]).astype(o_ref.dtype)


def _rmsnorm(x, g, out_dtype, tm=512):
    s, d = x.shape
    return pl.pallas_call(
        _rmsnorm_kernel,
        out_shape=jax.ShapeDtypeStruct((s, d), out_dtype),
        grid=(s // tm,),
        in_specs=[pl.BlockSpec((tm, d), lambda i: (i, 0)), pl.BlockSpec((1, d), lambda i: (0, 0))],
        out_specs=pl.BlockSpec((tm, d), lambda i: (i, 0)),
        compiler_params=_params("parallel"),
    )(x, g.reshape(1, d))


def _mm_kernel(*refs, n_b, n_extra, epilogue, w_rows_are_outputs):
    a_ref = refs[0]
    b_refs = refs[1:1 + n_b]
    extra_refs = refs[1 + n_b:1 + n_b + n_extra]
    o_ref = refs[1 + n_b + n_extra]
    w_scr = refs[2 + n_b + n_extra:]

    @pl.when(pl.program_id(1) == 0)
    def _():
        for b_ref, w in zip(b_refs, w_scr):
            w[...] = b_ref[...].astype(BF16)

    a = a_ref[...]
    z = [(_dot_nt if w_rows_are_outputs else _dot)(a, w[...]) for w in w_scr]
    o_ref[...] = epilogue(*z, *[e[...] for e in extra_refs]).astype(o_ref.dtype)


def _ep_scale(z, scale):
    return z * scale


def _ep_resid(z, resid):
    return resid + z


def _ep_swiglu(za, zb):
    return jax.nn.silu(za) * zb


def _ep_glu_resid(za, zb, resid):
    return resid + za * jax.nn.sigmoid(zb)


def _matmul(epilogue, a, b, layer, n_out, out_dtype, tm, tn, *, b_halves=1, row_extra=None, tile_extra=None,
            w_rows_are_outputs=False):
    m, k = a.shape
    nj = n_out // tn
    estimate = (b_halves * k * tn * (2 * 4 + 2) + 2 * tm * k * 2 + 4 * tm * tn * 4 + b_halves * tm * tn * 4)
    b_mode = pl.Buffered(1) if estimate > VMEM_LIMIT else None
    in_specs = [pl.BlockSpec((tm, k), lambda j, i: (i, 0))]
    args = [a]
    for half in range(b_halves):
        if w_rows_are_outputs:
            spec = pl.BlockSpec((None, tn, k), functools.partial(lambda j, i, o: (layer, j + o, 0), o=half * nj),
                                pipeline_mode=b_mode)
        else:
            spec = pl.BlockSpec((None, k, tn), functools.partial(lambda j, i, o: (layer, 0, j + o), o=half * nj),
                                pipeline_mode=b_mode)
        in_specs.append(spec)
        args.append(b)
    extras = []
    if row_extra is not None:
        in_specs.append(pl.BlockSpec((1, tn), lambda j, i: (0, j)))
        extras.append(row_extra)
    if tile_extra is not None:
        in_specs.append(pl.BlockSpec((tm, tn), lambda j, i: (i, j)))
        extras.append(tile_extra)
    kern = functools.partial(_mm_kernel, n_b=b_halves, n_extra=len(extras), epilogue=epilogue,
                             w_rows_are_outputs=w_rows_are_outputs)
    return pl.pallas_call(
        kern,
        out_shape=jax.ShapeDtypeStruct((m, n_out), out_dtype),
        grid=(nj, m // tm),
        in_specs=in_specs,
        out_specs=pl.BlockSpec((tm, tn), lambda j, i: (i, j)),
        scratch_shapes=[pltpu.VMEM((tn, k) if w_rows_are_outputs else (k, tn), BF16) for _ in range(b_halves)],
        compiler_params=_params("arbitrary", "arbitrary"),
    )(*args, *extras)


def _mm_resid_norm_kernel(a_ref, b_ref, r_ref, g_ref, h_ref, hn_ref, w_scr):
    @pl.when(pl.program_id(0) == 0)
    def _():
        w_scr[...] = b_ref[...].astype(BF16)

    h = r_ref[...] + _dot(a_ref[...], w_scr[...])
    h_ref[...] = h
    ms = jnp.mean(h * h, axis=-1, keepdims=True)
    hn_ref[...] = (h * lax.rsqrt(ms + RMS_EPS) * g_ref[...]).astype(hn_ref.dtype)


def _matmul_resid_norm(a, b, layer, resid, g, tm):
    m, k = a.shape
    n = b.shape[2]
    return pl.pallas_call(
        _mm_resid_norm_kernel,
        out_shape=(jax.ShapeDtypeStruct((m, n), F32), jax.ShapeDtypeStruct((m, n), BF16)),
        grid=(m // tm,),
        in_specs=[
            pl.BlockSpec((tm, k), lambda i: (i, 0)),
            pl.BlockSpec((None, k, n), lambda i: (layer, 0, 0), pipeline_mode=pl.Buffered(1)),
            pl.BlockSpec((tm, n), lambda i: (i, 0)),
            pl.BlockSpec((1, n), lambda i: (0, 0)),
        ],
        out_specs=(pl.BlockSpec((tm, n), lambda i: (i, 0)), pl.BlockSpec((tm, n), lambda i: (i, 0))),
        scratch_shapes=[pltpu.VMEM((k, n), BF16)],
        compiler_params=_params("arbitrary"),
    )(a, b, resid, g.reshape(1, n))


def _swiglu_ffn(h, hn, w_in, w_out, layer):
    act = _matmul(_ep_swiglu, hn, w_in, layer, D_FF, BF16, 1024, 512, b_halves=2)
    return _matmul(_ep_resid, act, w_out, layer, D_MODEL, F32, 512, 512, tile_extra=h)


def _compress_kernel(x_ref, w1_ref, w2_ref, pos_ref, o_ref):
    ncp = x_ref.shape[0] // CMP_STRIDE
    top = jnp.zeros((ncp, CMP_HIDDEN), F32)
    bot = jnp.zeros((ncp, CMP_HIDDEN), F32)
    by_offset = jnp.swapaxes(x_ref[...].reshape(ncp, CMP_STRIDE, HEAD_DIM), 0, 1)
    for r in range(CMP_STRIDE):
        x_r = by_offset[r]
        top = top + _dot(x_r, w1_ref[0, r * HEAD_DIM:(r + 1) * HEAD_DIM, :])
        bot = bot + _dot(x_r, w1_ref[0, (CMP_STRIDE + r) * HEAD_DIM:(CMP_STRIDE + r + 1) * HEAD_DIM, :])
    posb = _dot(pos_ref[0], w1_ref[0])[0:1]
    hid = top + pltpu.roll(bot, ncp - 1, 0) + posb
    o_ref[0, 0] = _dot(jax.nn.gelu(hid).astype(BF16), w2_ref[0]).astype(o_ref.dtype)


def _compress(proj, w1, w2, pos):
    s = proj.shape[0]
    ncp = s // CMP_STRIDE
    kc_blk = Q_WIDTH // HEAD_DIM
    return pl.pallas_call(
        _compress_kernel,
        out_shape=jax.ShapeDtypeStruct((2, N_KV_GROUPS, ncp, HEAD_DIM), BF16),
        grid=(2, N_KV_GROUPS),
        in_specs=[
            pl.BlockSpec((s, HEAD_DIM), lambda kv, gi: (0, kc_blk + kv * N_KV_GROUPS + gi)),
            pl.BlockSpec((1, CMP_BLOCK * HEAD_DIM, CMP_HIDDEN), lambda kv, gi: (kv, 0, 0)),
            pl.BlockSpec((1, CMP_HIDDEN, HEAD_DIM), lambda kv, gi: (kv, 0, 0)),
            pl.BlockSpec((1, SUBLANES, CMP_BLOCK * HEAD_DIM), lambda kv, gi: (kv, 0, 0)),
        ],
        out_specs=pl.BlockSpec((1, 1, ncp, HEAD_DIM), lambda kv, gi: (kv, gi, 0, 0)),
        compiler_params=_params("parallel", "parallel"),
    )(proj, w1, w2, pos)


def _cmp_attn_kernel(q_ref, kcc_ref, vcct_ref, band_ref, mt_ref, oc_ref, nm_ref, s_scr, *, tq, nb):
    i = pl.program_id(1)
    band_start = pl.multiple_of(i * (tq // CMP_STRIDE), SUBLANES)
    n_rows = kcc_ref.shape[1]
    n_blk = mt_ref.shape[0]

    def body(n_r, n_b):
        kcc = kcc_ref[0, :n_r, :]
        vcct = vcct_ref[0, :, :n_r]
        rows = lax.broadcasted_iota(jnp.int32, (n_r, tq), 0)
        row_bias = jnp.where(rows >= CMP_PAD, jnp.where(rows < band_start + nb, 0.0, NEG), NEG)
        imp = jnp.zeros((n_r, tq), F32)

        def scores(h):
            qh = q_ref[:, h * HEAD_DIM:(h + 1) * HEAD_DIM]
            s_scr[h, :n_r, :] = _dot_nt(kcc, qh) + row_bias
            s_scr[h, pl.ds(band_start, nb), :] += band_ref[h]

        scores(0)
        for h in range(HPG):
            if h + 1 < HPG:
                scores(h + 1)
            s = s_scr[h, :n_r, :]
            m = jnp.maximum(jnp.max(s, axis=0, keepdims=True), 0.5 * NEG)
            e = jnp.exp2(s - m)
            l = jnp.sum(e, axis=0, keepdims=True)
            pn = e * (1.0 / jnp.where(l > 0.0, l, 1.0))
            imp = imp + pn
            oct_h = _dot(vcct, pn.astype(BF16))
            oc_ref[:, h * HEAD_DIM:(h + 1) * HEAD_DIM] = oct_h.T
        mt = mt_ref[:n_b, :n_r]
        hi = imp.astype(BF16)
        rest = imp - hi.astype(F32)
        mid = rest.astype(BF16)
        lo = (rest - mid.astype(F32)).astype(BF16)
        slc = _dot(mt, hi) + _dot(mt, mid) + _dot(mt, lo)
        jj = lax.broadcasted_iota(jnp.int32, (n_b, tq), 0)
        t_blk = (i * tq + lax.broadcasted_iota(jnp.int32, (n_b, tq), 1)) // SEL_BLOCK
        forced = jnp.where(jj == 0, 1, jnp.where(jj == t_blk, 1, jnp.where(jj == t_blk - 1, 1, 0)))
        picked = jnp.where(forced == 1, 1.0, 0.0)
        score = jnp.where(forced == 1, -jnp.inf, jnp.where(jj <= t_blk, slc, NEG))
        for _ in range(SEL_TOPK - N_FORCED):
            best = jnp.max(score, axis=0, keepdims=True)
            first = jnp.min(jnp.where(score == best, jj, n_blk), axis=0, keepdims=True)
            hit = jj == first
            picked = jnp.where(hit, 1.0, picked)
            score = jnp.where(hit, -jnp.inf, score)
        neg_mask = jnp.where(picked > 0.5, jnp.where(jj <= t_blk, 0.0, NEG), NEG)
        if n_b < n_blk:
            neg_mask = jnp.concatenate([neg_mask, jnp.full((n_blk - n_b, tq), NEG, F32)], axis=0)
        nm_ref[...] = neg_mask.T.astype(nm_ref.dtype)

    step = LANES
    extents = list(range(step, n_rows, step)) + [n_rows]
    need_rows = band_start + nb
    for v, n_r in enumerate(extents):
        lower = extents[v - 1] if v else 0
        n_b = min(n_blk, -(-(n_r * CMP_STRIDE // SEL_BLOCK) // SUBLANES) * SUBLANES)
        pl.when(jnp.logical_and(need_rows > lower, need_rows <= n_r))(functools.partial(body, n_r, n_b))


def _cmp_attn(proj, kcc, vcct, band, mt, tq):
    s = proj.shape[0]
    n_rows = kcc.shape[1]
    nb = band.shape[1]
    kern = functools.partial(_cmp_attn_kernel, tq=tq, nb=nb)
    return pl.pallas_call(
        kern,
        out_shape=(jax.ShapeDtypeStruct((s, Q_WIDTH), F32),
                   jax.ShapeDtypeStruct((s, N_KV_GROUPS * LANES), BF16)),
        grid=(N_KV_GROUPS, s // tq),
        in_specs=[
            pl.BlockSpec((tq, HPG * HEAD_DIM), lambda g, i: (i, g)),
            pl.BlockSpec((1, n_rows, HEAD_DIM), lambda g, i: (g, 0, 0)),
            pl.BlockSpec((1, HEAD_DIM, n_rows), lambda g, i: (g, 0, 0)),
            pl.BlockSpec((HPG, nb, tq), lambda g, i: (g, 0, 0)),
            pl.BlockSpec((LANES, n_rows), lambda g, i: (0, 0)),
        ],
        out_specs=(pl.BlockSpec((tq, HPG * HEAD_DIM), lambda g, i: (i, g)),
                   pl.BlockSpec((tq, LANES), lambda g, i: (i, g))),
        scratch_shapes=[pltpu.VMEM((HPG, n_rows, tq), F32)],
        compiler_params=_params("parallel", "parallel"),
    )(proj, kcc, vcct, band, mt)


def _lane_tile(x, width):
    return jnp.concatenate([x] * (width // x.shape[1]), axis=1)


def _softmax_update(s, v_aug, m_ref, acc_ref, rows):
    m_prev = m_ref[rows, :]
    m_next = jnp.maximum(m_prev, jnp.max(s, axis=1, keepdims=True))
    alpha = jnp.exp2(m_prev - m_next)
    p = jnp.exp2(s - _lane_tile(m_next, s.shape[1]))
    acc_ref[rows, :] = _lane_tile(alpha, acc_ref.shape[1]) * acc_ref[rows, :] + _dot(p.astype(BF16), v_aug)
    m_ref[rows, :] = m_next


def _softmax_once(s, v_aug):
    m = jnp.broadcast_to(jnp.max(s, axis=1, keepdims=True), (s.shape[0], LANES))
    p = jnp.exp2(s - _lane_tile(m, s.shape[1]))
    return _dot(p.astype(BF16), v_aug)


def _sel_win_kernel(q_ref, nm_ref, ks_ref, vs_ref, kw_ref, vw_ref, blk_ref, per_ref, g_ref, oc_ref, o_ref,
                    q4, m_s, a_s, bias_ref, *, tile):
    step = pl.program_id(1)
    t = tile
    first = step * Q_TILES

    @pl.when(step == 0)
    def _():
        tl = lax.broadcasted_iota(jnp.int32, (t, t), 0)
        kl = lax.broadcasted_iota(jnp.int32, (t, t), 1)
        edge = jnp.where(tl < kl, 0.0, NEG)
        for h in range(HPG):
            bias_ref[h, :, :t] = edge
            for which in range(2):
                rows = jnp.broadcast_to(per_ref[h, which:which + 1, :], (t, 2 * t))
                table = pltpu.roll(rows, 0, 1, stride=1, stride_axis=0)
                bias_ref[h, :, (which + 1) * t:(which + 2) * t] = table[:, :t]
    def stacked(qt, h):
        return slice((qt * HPG + h) * t, (qt * HPG + h + 1) * t)

    def tile_rows(qt):
        return slice(qt * t, (qt + 1) * t)

    for qt in range(Q_TILES):
        for h in range(HPG):
            q4[stacked(qt, h), :HEAD_DIM] = q_ref[tile_rows(qt), h * HEAD_DIM:(h + 1) * HEAD_DIM]
            q4[stacked(qt, h), HEAD_DIM:] = nm_ref[tile_rows(qt), :]
    m_s[...] = jnp.full(m_s.shape, -3e38, F32)
    a_s[...] = jnp.zeros(a_s.shape, F32)

    def sel_operands(row0, n_tiles):
        keys = pl.ds(row0, n_tiles * t)
        k_aug = jnp.concatenate([ks_ref[keys, :], blk_ref[keys, :]], axis=1)
        v_aug = jnp.concatenate([vs_ref[keys, :], jnp.ones((n_tiles * t, LANES), BF16)], axis=1)
        return k_aug, v_aug

    def sel_pass(qt, operands, n_biased):
        k_aug, v_aug = operands

        def score(h):
            s = _dot_nt(q4[stacked(qt, h), :], k_aug)
            plain = s.shape[1] - n_biased * t
            if n_biased:
                near = s[:, plain:] + bias_ref[h, :, (3 - n_biased) * t:]
                s = jnp.concatenate([s[:, :plain], near], axis=1) if plain else near
            return s

        def consume(h, s):
            _softmax_update(s, v_aug, m_s, a_s, stacked(qt, h))

        return score, consume

    def win_pass(qt, row0, n_tiles):
        keys = pl.ds(row0, n_tiles * t)
        v_aug = jnp.concatenate([vw_ref[keys, :], jnp.ones((n_tiles * t, LANES), BF16)], axis=1)

        def score(h):
            return _dot_nt(q4[stacked(qt, h), :HEAD_DIM], kw_ref[keys, :]) + bias_ref[h, :, (3 - n_tiles) * t:]

        def consume(h, s):
            acc_w = _softmax_once(s, v_aug)
            acc_s = a_s[stacked(qt, h), :]
            o_w = acc_w[:, :HEAD_DIM] * (1.0 / acc_w[:, HEAD_DIM:])
            o_s = acc_s[:, :HEAD_DIM] * (1.0 / acc_s[:, HEAD_DIM:])
            cols = slice(h * HEAD_DIM, (h + 1) * HEAD_DIM)
            gates = g_ref[tile_rows(qt), :]
            o = (gates[:, h:h + 1] * oc_ref[tile_rows(qt), cols] + gates[:, HPG + h:HPG + h + 1] * o_s
                 + gates[:, 2 * HPG + h:2 * HPG + h + 1] * o_w)
            o_ref[tile_rows(qt), cols] = o.astype(o_ref.dtype)

        return score, consume

    def run(passes):
        work = [(score, consume, h0) for score, consume in passes for h0 in range(0, HPG, 2)]
        ahead = [work[0][0](h) for h in (0, 1)]
        for n, (_, consume, h0) in enumerate(work):
            now = ahead
            if n + 1 < len(work):
                nxt_score, _, nxt_h0 = work[n + 1]
                ahead = [nxt_score(h) for h in (nxt_h0, nxt_h0 + 1)]
            for h, s in zip((h0, h0 + 1), now):
                consume(h, s)

    def far_passes(row0):
        operands = sel_operands(row0, FAR_TILES)
        return [sel_pass(qt, operands, 0) for qt in range(Q_TILES)]

    n_far = jnp.maximum(first - 1, 0)
    n_chunks = n_far // FAR_TILES
    chunk = FAR_TILES * t

    def far_pair(c, carry):
        row0 = pl.multiple_of(c * 2 * chunk, 2 * chunk)
        run(far_passes(row0) + far_passes(row0 + chunk))
        return carry

    lax.fori_loop(0, n_chunks // 2, far_pair, 0)

    @pl.when(n_chunks % 2 == 1)
    def _():
        run(far_passes(pl.multiple_of((n_chunks - 1) * chunk, chunk)))

    def tail(left):
        row0 = pl.multiple_of((n_far - left) * t, t)
        passes = []
        for qt in range(Q_TILES):
            n_sel = left + 2 + qt
            passes.append(sel_pass(qt, sel_operands(row0, n_sel), 2))
            passes.append(win_pass(qt, pl.multiple_of((first + qt - 2) * t, t), 3))
        run(passes)

    for left in range(FAR_TILES):
        if (left + 1) % Q_TILES == 0:
            pl.when(jnp.logical_and(step >= 1, n_far % FAR_TILES == left))(functools.partial(tail, left))

    @pl.when(step == 0)
    def _():
        passes = []
        for qt in range(Q_TILES):
            n_keys = min(qt + 1, 3)
            passes.append(sel_pass(qt, sel_operands(0, qt + 1), min(qt + 1, 2)))
            passes.append(win_pass(qt, (qt + 1 - n_keys) * t, n_keys))
        run(passes)


def _sel_win(proj, neg_mask, blk_onehot, periods, gates, o_c, tile):
    s = proj.shape[0]
    ks_blk = (Q_WIDTH + 2 * KV_WIDTH) // HEAD_DIM
    vs_blk = (Q_WIDTH + 3 * KV_WIDTH) // HEAD_DIM
    kw_blk = (Q_WIDTH + 4 * KV_WIDTH) // HEAD_DIM
    vw_blk = (Q_WIDTH + 5 * KV_WIDTH) // HEAD_DIM
    once = pl.Buffered(1)
    rows = Q_TILES * tile
    stacked_rows = HPG * rows
    kern = functools.partial(_sel_win_kernel, tile=tile)
    return pl.pallas_call(
        kern,
        out_shape=jax.ShapeDtypeStruct((s, Q_WIDTH), BF16),
        grid=(N_KV_GROUPS, s // rows),
        in_specs=[
            pl.BlockSpec((rows, HPG * HEAD_DIM), lambda g, i: (i, g)),
            pl.BlockSpec((rows, LANES), lambda g, i: (i, g)),
            pl.BlockSpec((s, HEAD_DIM), lambda g, i: (0, ks_blk + g), pipeline_mode=once),
            pl.BlockSpec((s, HEAD_DIM), lambda g, i: (0, vs_blk + g), pipeline_mode=once),
            pl.BlockSpec((s, HEAD_DIM), lambda g, i: (0, kw_blk + g), pipeline_mode=once),
            pl.BlockSpec((s, HEAD_DIM), lambda g, i: (0, vw_blk + g), pipeline_mode=once),
            pl.BlockSpec((s, LANES), lambda g, i: (0, 0), pipeline_mode=once),
            pl.BlockSpec((HPG, 2, 2 * tile), lambda g, i: (g, 0, 0)),
            pl.BlockSpec((rows, LANES), lambda g, i: (i, g)),
            pl.BlockSpec((rows, HPG * HEAD_DIM), lambda g, i: (i, g)),
        ],
        out_specs=pl.BlockSpec((rows, HPG * HEAD_DIM), lambda g, i: (i, g)),
        scratch_shapes=[
            pltpu.VMEM((stacked_rows, 2 * HEAD_DIM), BF16),
            pltpu.VMEM((stacked_rows, LANES), F32),
            pltpu.VMEM((stacked_rows, 2 * HEAD_DIM), F32),
            pltpu.VMEM((HPG, tile, 3 * tile), F32),
        ],
        compiler_params=_params("arbitrary", "arbitrary"),
    )(proj, neg_mask, proj, proj, proj, proj, blk_onehot, periods, gates, o_c)


def _rel_bucket_np(dist):
    n = np.maximum(dist, 0)
    max_exact = REL_BUCKETS // 2
    large = max_exact + (np.log(np.maximum(n, 1).astype(np.float32) / np.float32(max_exact))
                         / np.float32(math.log(REL_MAX_DIST / max_exact))
                         * np.float32(REL_BUCKETS - max_exact)).astype(np.int32)
    large = np.minimum(large, REL_BUCKETS - 1)
    return np.where(n < max_exact, n, large).astype(np.int32)


def _far_distance():
    d = np.arange(4 * REL_MAX_DIST)
    b = _rel_bucket_np(d)
    assert b[-1] == REL_BUCKETS - 1
    return int(np.max(np.nonzero(b != REL_BUCKETS - 1)[0])) + 1


def _bias_table(rel_bias, dist):
    bucket = jnp.asarray(_rel_bucket_np(dist))
    shifted = (rel_bias[bucket] - rel_bias[REL_BUCKETS - 1]) * LOG2E
    tab = jnp.where(jnp.asarray(dist >= 0)[..., None], shifted, NEG)
    return jnp.moveaxis(tab, -1, 0).astype(F32)


def _bias_period(rel_bias, offset, t):
    x = np.arange(2 * t)
    return _bias_table(rel_bias, np.where(x < t, offset - x, offset + 2 * t - x))


def _nsa_mixer(h, hn, rel_bias, w_in, w_out, layer, pos_k, w1_k, w2_k, pos_v, w1_v, w2_v, next_norm_g):
    s = h.shape[0]
    n_sel = s // SEL_BLOCK
    ncp = s // CMP_STRIDE
    tile = ATT_TILE
    far = _far_distance()
    assert n_sel <= LANES and s % CMP_TQ == 0 and s % (Q_TILES * tile) == 0 and FAR_TILES % Q_TILES == 0
    assert far <= tile and far <= CMP_STRIDE * CMP_PAD - CMP_BLOCK + 1 + CMP_STRIDE

    col_scale = jnp.where(jnp.arange(PROJ_MAIN) < Q_WIDTH, HEAD_DIM ** -0.5 * LOG2E, 1.0).astype(F32)[None, :]
    w_in_t = jnp.swapaxes(w_in, 1, 2)
    proj = _matmul(_ep_scale, hn, w_in_t, layer, PROJ_MAIN, BF16, 2048, 512, row_extra=col_scale,
                   w_rows_are_outputs=True)
    w_gate = w_in_t[layer, PROJ_MAIN:].reshape(N_GATES, N_KV_GROUPS, HPG, D_MODEL).transpose(1, 0, 2, 3)
    w_gate = w_gate.reshape(N_KV_GROUPS, N_GATES * HPG, D_MODEL)
    w_gate = jnp.pad(w_gate, ((0, 0), (0, LANES - N_GATES * HPG), (0, 0))).reshape(1, N_KV_GROUPS * LANES, D_MODEL)
    gates = _matmul(jax.nn.sigmoid, hn, w_gate, 0, N_KV_GROUPS * LANES, F32, 2048, 512, w_rows_are_outputs=True)

    w1 = jnp.stack([w1_k, w1_v]).astype(BF16)
    w2 = jnp.stack([w2_k, w2_v]).astype(BF16)
    pos = jnp.stack([pos_k.reshape(1, -1), pos_v.reshape(1, -1)]).astype(BF16)
    pos = jnp.broadcast_to(pos, (2, SUBLANES, CMP_BLOCK * HEAD_DIM))
    cc = _compress(proj, w1, w2, pos)
    kcc = jnp.pad(cc[0], ((0, 0), (CMP_PAD, 0), (0, 0)))
    vcct = jnp.pad(cc[1], ((0, 0), (CMP_PAD, 0), (0, 0))).transpose(0, 2, 1)
    n_rows = ncp + CMP_PAD

    nb = CMP_TQ // CMP_STRIDE + CMP_PAD
    first_dist = -CMP_STRIDE * (nb - 1 - CMP_PAD) - (CMP_BLOCK - 1)
    by_dist = _bias_table(rel_bias, first_dist + np.arange(CMP_TQ + CMP_STRIDE * (nb - 1)))
    band = jnp.stack([by_dist[:, CMP_STRIDE * (nb - 1 - b):CMP_STRIDE * (nb - 1 - b) + CMP_TQ] for b in range(nb)], axis=1)
    ratio = SEL_BLOCK // CMP_STRIDE
    lo = CMP_BLOCK // CMP_STRIDE - 1
    c_of_row = np.arange(n_rows)[None, :] - CMP_PAD
    j_of = np.arange(LANES)[:, None]
    mt = ((c_of_row >= ratio * j_of - lo) & (c_of_row <= ratio * j_of + ratio - 1)
          & (c_of_row >= 0) & (c_of_row <= ncp - 2) & (j_of < n_sel)).astype(np.float32)
    o_c, neg_mask = _cmp_attn(proj, kcc, vcct, band, jnp.asarray(mt, BF16), CMP_TQ)

    onehot = (np.arange(s)[:, None] // SEL_BLOCK == np.arange(LANES)[None, :]).astype(np.float32)
    periods = jnp.stack([_bias_period(rel_bias, tile, tile), _bias_period(rel_bias, 0, tile)], axis=1)
    o = _sel_win(proj, neg_mask, jnp.asarray(onehot, BF16), periods, gates, o_c, tile)
    return _matmul_resid_norm(o, w_out, layer, h, next_norm_g, 256)


def _s5_kernel(x_ref, lam_ref, bt_ref, c_ref, d_ref, y_ref, vt, yt, tmask, *, chunk):
    L = chunk
    half = L // 2
    width = L * SSM_GROUP
    n_chunks = x_ref.shape[0] // L
    by_step = jnp.swapaxes(x_ref[...].reshape(n_chunks, L, LANES), 0, 1)
    for tau in range(L):
        vt[tau] = by_step[tau].T

    @pl.when(pl.program_id(0) == 0)
    def _():
        dst = lax.broadcasted_iota(jnp.int32, (width, width), 0) // SSM_GROUP
        src = lax.broadcasted_iota(jnp.int32, (width, width), 1) // SSM_GROUP
        tmask[...] = jnp.where(src <= dst, 1.0, 0.0)

    def cmul(xr, xi, yr, yi):
        return xr * yr - xi * yi, xr * yi + xi * yr

    n2 = 2 * SSM_STATE
    sub = lax.broadcasted_iota(jnp.int32, (SUBLANES, 1), 0)
    consts = jnp.where(sub == 0, 1.0, jnp.where(sub == 1, float(half + 1), jnp.where(sub == 2, float(half - 1), float(L))))
    tau_col = lax.broadcasted_iota(jnp.int32, (L, 1), 0).astype(F32)
    row = lax.broadcasted_iota(jnp.int32, (n_chunks, n2), 0)
    conj = jnp.where(lax.broadcasted_iota(jnp.int32, (1, n2), 1) < SSM_STATE, 1.0, -1.0)

    def group(g):
        lam = lam_ref[g]
        a_re, a_im = lam[0:1], lam[1:2]
        dt = jnp.exp(lam[2:3])
        log_re, log_im = a_re * dt, a_im * dt

        def cpow(e):
            mag, cos, sin = jnp.exp(e * log_re), jnp.cos(e * log_im), jnp.sin(e * log_im)
            inv = 1.0 / mag
            return mag * cos, mag * sin, inv * cos, -(inv * sin)

        kr, ki, _, _ = cpow(consts)
        pr, pi = kr[0:1] - 1.0, ki[0:1]
        den = a_re * a_re + a_im * a_im
        cf_r = (pr * a_re + pi * a_im) / den
        cf_i = (pi * a_re - pr * a_im) / den
        bt_c, bt_s = bt_ref[g, 0], bt_ref[g, 1]
        bb_c = cf_r * bt_c + cf_i * bt_s
        bb_s = cf_r * bt_s - cf_i * bt_c
        c_c, c_s = c_ref[g, 0], c_ref[g, 1]

        e1r, e1i, e2r, e2i = cpow(tau_col - half)
        e3r, e3i = cmul(e1r, e1i, kr[1:2], ki[1:2])
        e4r, e4i = cmul(e2r, e2i, kr[2:3], ki[2:3])

        def outer(er, ei, w_c, w_s):
            return (er[:, None, :] * w_c[None, :, :] + ei[:, None, :] * w_s[None, :, :]).reshape(width, n2)

        qm = outer(e1r, e1i, c_c, c_s)
        km_conj = outer(e2r, e2i, bb_c * conj, bb_s * conj)
        qc = outer(e3r, e3i, c_c, c_s)
        wz = outer(e4r, e4i, bb_c, bb_s)

        chans = pl.ds(pl.multiple_of(g * SSM_GROUP, SSM_GROUP), SSM_GROUP)
        ub = vt[:, chans, :].reshape(width, n_chunks).astype(BF16)
        yield

        tt = _dot_nt(qm.astype(BF16), km_conj.astype(BF16))
        x = _dot(wz.T.astype(BF16), ub).T
        yield
        y = _dot((tt * tmask[...]).astype(BF16), ub)
        p_r, p_i = kr[3:4], ki[3:4]
        d = 1
        while d < n_chunks:
            s = jnp.where(row >= d, pltpu.roll(x, d, 0), 0.0)
            x = x + s * p_r - pltpu.roll(s, SSM_STATE, 1) * (p_i * conj)
            p_r, p_i = cmul(p_r, p_i, p_r, p_i)
            d *= 2
        prev_conj = jnp.where(row >= 1, pltpu.roll(x, 1, 0), 0.0) * conj
        yield
        y = y + _dot_nt(qc.astype(BF16), prev_conj.astype(BF16))
        yt[:, chans, :] = y.reshape(L, SSM_GROUP, n_chunks)

    def group_batch(k, carry):
        running = [group(S5_GROUPS_PER_TRIP * k + j) for j in range(S5_GROUPS_PER_TRIP)]
        while running:
            running = [gen for gen in running if next(gen, running) is not running]
        return carry

    lax.fori_loop(0, LANES // SSM_GROUP // S5_GROUPS_PER_TRIP, group_batch, 0)
    by_chunk = jnp.swapaxes(jnp.stack([yt[tau].T for tau in range(L)], axis=0), 0, 1)
    y = by_chunk.reshape(n_chunks * L, LANES) + d_ref[...] * x_ref[...]
    y_ref[...] = jax.nn.gelu(y).astype(y_ref.dtype)


def _s5_mixer(h, hn, a_re, a_im, log_dt, b_re, b_im, c_re, c_im, d_skip, w_glu, layer):
    s = h.shape[0]
    L = S5_CHUNK
    n_chunks = s // L
    width = L * SSM_GROUP
    gpb = LANES // SSM_GROUP
    def packed(re, im):
        return jnp.stack([jnp.concatenate([re, im], -1), jnp.concatenate([-im, re], -1)], axis=1).astype(F32)

    lam = jnp.stack([a_re, a_im, jnp.broadcast_to(log_dt[:, None], a_re.shape)], axis=1).astype(F32)
    lam = jnp.concatenate([lam, lam], axis=-1)
    bt = packed(b_re.transpose(0, 2, 1), b_im.transpose(0, 2, 1))
    cc = packed(c_re, c_im)
    dd = d_skip.astype(F32).reshape(1, D_MODEL)
    y = pl.pallas_call(
        functools.partial(_s5_kernel, chunk=L),
        out_shape=jax.ShapeDtypeStruct((s, D_MODEL), BF16),
        grid=(SSM_GROUPS // gpb,),
        in_specs=[
            pl.BlockSpec((s, LANES), lambda b: (0, b)),
            pl.BlockSpec((gpb, 3, 2 * SSM_STATE), lambda b: (b, 0, 0)),
            pl.BlockSpec((gpb, 2, SSM_GROUP, 2 * SSM_STATE), lambda b: (b, 0, 0, 0)),
            pl.BlockSpec((gpb, 2, SSM_GROUP, 2 * SSM_STATE), lambda b: (b, 0, 0, 0)),
            pl.BlockSpec((1, LANES), lambda b: (0, b)),
        ],
        out_specs=pl.BlockSpec((s, LANES), lambda b: (0, b)),
        scratch_shapes=[
            pltpu.VMEM((L, LANES, n_chunks), F32),
            pltpu.VMEM((L, LANES, n_chunks), F32),
            pltpu.VMEM((width, width), F32),
        ],
        compiler_params=_params("arbitrary"),
    )(hn, lam, bt, cc, dd)
    return _matmul(_ep_glu_resid, y, w_glu, layer, D_MODEL, F32, 1024, 512, b_halves=2, tile_extra=h)


def kernel(x, rel_bias, mix_norm_g, ffn_norm_g, final_norm_g, nsa_w_in, nsa_w_out, cmp_pos_k, cmp_w1_k, cmp_w2_k, cmp_pos_v, cmp_w1_v, cmp_w2_v, s5_A_re, s5_A_im, s5_log_dt, s5_B_re, s5_B_im, s5_C_re, s5_C_im, s5_D, s5_w_glu, ffn_w_in, ffn_w_out):
    assert x.shape[0] == 1
    h = x[0]
    hn = _rmsnorm(h, mix_norm_g[0], BF16)
    h, hn = _nsa_mixer(h, hn, rel_bias, nsa_w_in, nsa_w_out, 0, cmp_pos_k[0], cmp_w1_k[0], cmp_w2_k[0],
                       cmp_pos_v[0], cmp_w1_v[0], cmp_w2_v[0], ffn_norm_g[0])
    h = _swiglu_ffn(h, hn, ffn_w_in, ffn_w_out, 0)
    hn = _rmsnorm(h, mix_norm_g[1], F32)
    h = _s5_mixer(h, hn, s5_A_re[0], s5_A_im[0], s5_log_dt[0], s5_B_re[0], s5_B_im[0], s5_C_re[0], s5_C_im[0],
                  s5_D[0], s5_w_glu, 0)
    h = _swiglu_ffn(h, _rmsnorm(h, ffn_norm_g[1], BF16), ffn_w_in, ffn_w_out, 1)
    return _rmsnorm(h, final_norm_g, x.dtype)[None]
```

```python
import functools
import math

import numpy as np
import jax
import jax.numpy as jnp
from jax import lax
from jax.experimental import pallas as pl
from jax.experimental.pallas import tpu as pltpu

D_MODEL = 2048
N_HEADS = 16
HEAD_DIM = 128
N_KV_GROUPS = 4
HPG = N_HEADS // N_KV_GROUPS
CMP_BLOCK = 32
CMP_STRIDE = 16
CMP_HIDDEN = 2 * HEAD_DIM
SEL_BLOCK = 64
SEL_TOPK = 16
N_FORCED = 3
WINDOW = 512
N_GATES = 3
KV_WIDTH = N_KV_GROUPS * HEAD_DIM
Q_WIDTH = N_HEADS * HEAD_DIM
PROJ_MAIN = Q_WIDTH + 6 * KV_WIDTH
REL_BUCKETS = 32
REL_MAX_DIST = 128
SSM_GROUP = 16
SSM_GROUPS = D_MODEL // SSM_GROUP
SSM_STATE = 64
D_FF = ((8 * D_MODEL + 2) // 3 + 255) // 256 * 256
RMS_EPS = 1e-6
NEG = -1e30
LOG2E = math.log2(math.e)

LANES = 128
SUBLANES = 8
VMEM_LIMIT = 52 * 1024 * 1024

ATT_TILE = WINDOW // 2
FAR_TILES = 4
Q_TILES = 2
CMP_TQ = 256
CMP_PAD = 8
S5_CHUNK = 32
S5_GROUPS_PER_TRIP = 4

BF16 = jnp.bfloat16
F32 = jnp.float32


def _dot(a, b):
    return jnp.dot(a, b, preferred_element_type=F32)


def _dot_nt(a, b):
    return lax.dot_general(a, b, (((1,), (1,)), ((), ())), preferred_element_type=F32)


def _params(*sem):
    return pltpu.CompilerParams(dimension_semantics=sem, vmem_limit_bytes=VMEM_LIMIT)


def _rmsnorm_kernel(x_ref, g_ref, o_ref):
    x = x_ref[...]
    ms = jnp.mean(x * x, axis=-1, keepdims=True)
    o_ref[...] = (x * lax.rsqrt(ms + RMS_EPS) * g_ref[...]).astype(o_ref.dtype)


def _rmsnorm(x, g, out_dtype, tm=512):
    s, d = x.shape
    return pl.pallas_call(
        _rmsnorm_kernel,
        out_shape=jax.ShapeDtypeStruct((s, d), out_dtype),
        grid=(s // tm,),
        in_specs=[pl.BlockSpec((tm, d), lambda i: (i, 0)), pl.BlockSpec((1, d), lambda i: (0, 0))],
        out_specs=pl.BlockSpec((tm, d), lambda i: (i, 0)),
        compiler_params=_params("parallel"),
    )(x, g.reshape(1, d))


def _mm_kernel(*refs, n_b, n_extra, epilogue, w_rows_are_outputs):
    a_ref = refs[0]
    b_refs = refs[1:1 + n_b]
    extra_refs = refs[1 + n_b:1 + n_b + n_extra]
    o_ref = refs[1 + n_b + n_extra]
    w_scr = refs[2 + n_b + n_extra:]

    @pl.when(pl.program_id(1) == 0)
    def _():
        for b_ref, w in zip(b_refs, w_scr):
            w[...] = b_ref[...].astype(BF16)

    a = a_ref[...]
    z = [(_dot_nt if w_rows_are_outputs else _dot)(a, w[...]) for w in w_scr]
    o_ref[...] = epilogue(*z, *[e[...] for e in extra_refs]).astype(o_ref.dtype)


def _ep_scale(z, scale):
    return z * scale


def _ep_resid(z, resid):
    return resid + z


def _ep_swiglu(za, zb):
    return jax.nn.silu(za) * zb


def _ep_glu_resid(za, zb, resid):
    return resid + za * jax.nn.sigmoid(zb)


def _matmul(epilogue, a, b, layer, n_out, out_dtype, tm, tn, *, b_halves=1, row_extra=None, tile_extra=None,
            w_rows_are_outputs=False):
    m, k = a.shape
    nj = n_out // tn
    estimate = (b_halves * k * tn * (2 * 4 + 2) + 2 * tm * k * 2 + 4 * tm * tn * 4 + b_halves * tm * tn * 4)
    b_mode = pl.Buffered(1) if estimate > VMEM_LIMIT else None
    in_specs = [pl.BlockSpec((tm, k), lambda j, i: (i, 0))]
    args = [a]
    for half in range(b_halves):
        if w_rows_are_outputs:
            spec = pl.BlockSpec((None, tn, k), functools.partial(lambda j, i, o: (layer, j + o, 0), o=half * nj),
                                pipeline_mode=b_mode)
        else:
            spec = pl.BlockSpec((None, k, tn), functools.partial(lambda j, i, o: (layer, 0, j + o), o=half * nj),
                                pipeline_mode=b_mode)
        in_specs.append(spec)
        args.append(b)
    extras = []
    if row_extra is not None:
        in_specs.append(pl.BlockSpec((1, tn), lambda j, i: (0, j)))
        extras.append(row_extra)
    if tile_extra is not None:
        in_specs.append(pl.BlockSpec((tm, tn), lambda j, i: (i, j)))
        extras.append(tile_extra)
    kern = functools.partial(_mm_kernel, n_b=b_halves, n_extra=len(extras), epilogue=epilogue,
                             w_rows_are_outputs=w_rows_are_outputs)
    return pl.pallas_call(
        kern,
        out_shape=jax.ShapeDtypeStruct((m, n_out), out_dtype),
        grid=(nj, m // tm),
        in_specs=in_specs,
        out_specs=pl.BlockSpec((tm, tn), lambda j, i: (i, j)),
        scratch_shapes=[pltpu.VMEM((tn, k) if w_rows_are_outputs else (k, tn), BF16) for _ in range(b_halves)],
        compiler_params=_params("arbitrary", "arbitrary"),
    )(*args, *extras)


def _mm_resid_norm_kernel(a_ref, b_ref, r_ref, g_ref, h_ref, hn_ref, w_scr):
    @pl.when(pl.program_id(0) == 0)
    def _():
        w_scr[...] = b_ref[...].astype(BF16)

    h = r_ref[...] + _dot(a_ref[...], w_scr[...])
    h_ref[...] = h
    ms = jnp.mean(h * h, axis=-1, keepdims=True)
    hn_ref[...] = (h * lax.rsqrt(ms + RMS_EPS) * g_ref[...]).astype(hn_ref.dtype)


def _matmul_resid_norm(a, b, layer, resid, g, tm):
    m, k = a.shape
    n = b.shape[2]
    return pl.pallas_call(
        _mm_resid_norm_kernel,
        out_shape=(jax.ShapeDtypeStruct((m, n), F32), jax.ShapeDtypeStruct((m, n), BF16)),
        grid=(m // tm,),
        in_specs=[
            pl.BlockSpec((tm, k), lambda i: (i, 0)),
            pl.BlockSpec((None, k, n), lambda i: (layer, 0, 0), pipeline_mode=pl.Buffered(1)),
            pl.BlockSpec((tm, n), lambda i: (i, 0)),
            pl.BlockSpec((1, n), lambda i: (0, 0)),
        ],
        out_specs=(pl.BlockSpec((tm, n), lambda i: (i, 0)), pl.BlockSpec((tm, n), lambda i: (i, 0))),
        scratch_shapes=[pltpu.VMEM((k, n), BF16)],
        compiler_params=_params("arbitrary"),
    )(a, b, resid, g.reshape(1, n))


def _swiglu_ffn(h, hn, w_in, w_out, layer):
    act = _matmul(_ep_swiglu, hn, w_in, layer, D_FF, BF16, 1024, 512, b_halves=2)
    return _matmul(_ep_resid, act, w_out, layer, D_MODEL, F32, 512, 512, tile_extra=h)


def _compress_kernel(x_ref, w1_ref, w2_ref, pos_ref, o_ref):
    ncp = x_ref.shape[0] // CMP_STRIDE
    top = jnp.zeros((ncp, CMP_HIDDEN), F32)
    bot = jnp.zeros((ncp, CMP_HIDDEN), F32)
    by_offset = jnp.swapaxes(x_ref[...].reshape(ncp, CMP_STRIDE, HEAD_DIM), 0, 1)
    for r in range(CMP_STRIDE):
        x_r = by_offset[r]
        top = top + _dot(x_r, w1_ref[0, r * HEAD_DIM:(r + 1) * HEAD_DIM, :])
        bot = bot + _dot(x_r, w1_ref[0, (CMP_STRIDE + r) * HEAD_DIM:(CMP_STRIDE + r + 1) * HEAD_DIM, :])
    posb = _dot(pos_ref[0], w1_ref[0])[0:1]
    hid = top + pltpu.roll(bot, ncp - 1, 0) + posb
    o_ref[0, 0] = _dot(jax.nn.gelu(hid).astype(BF16), w2_ref[0]).astype(o_ref.dtype)


def _compress(proj, w1, w2, pos):
    s = proj.shape[0]
    ncp = s // CMP_STRIDE
    kc_blk = Q_WIDTH // HEAD_DIM
    return pl.pallas_call(
        _compress_kernel,
        out_shape=jax.ShapeDtypeStruct((2, N_KV_GROUPS, ncp, HEAD_DIM), BF16),
        grid=(2, N_KV_GROUPS),
        in_specs=[
            pl.BlockSpec((s, HEAD_DIM), lambda kv, gi: (0, kc_blk + kv * N_KV_GROUPS + gi)),
            pl.BlockSpec((1, CMP_BLOCK * HEAD_DIM, CMP_HIDDEN), lambda kv, gi: (kv, 0, 0)),
            pl.BlockSpec((1, CMP_HIDDEN, HEAD_DIM), lambda kv, gi: (kv, 0, 0)),
            pl.BlockSpec((1, SUBLANES, CMP_BLOCK * HEAD_DIM), lambda kv, gi: (kv, 0, 0)),
        ],
        out_specs=pl.BlockSpec((1, 1, ncp, HEAD_DIM), lambda kv, gi: (kv, gi, 0, 0)),
        compiler_params=_params("parallel", "parallel"),
    )(proj, w1, w2, pos)


def _cmp_attn_kernel(q_ref, kcc_ref, vcct_ref, band_ref, mt_ref, oc_ref, nm_ref, s_scr, *, tq, nb):
    i = pl.program_id(1)
    band_start = pl.multiple_of(i * (tq // CMP_STRIDE), SUBLANES)
    n_rows = kcc_ref.shape[1]
    n_blk = mt_ref.shape[0]

    def body(n_r, n_b):
        kcc = kcc_ref[0, :n_r, :]
        vcct = vcct_ref[0, :, :n_r]
        rows = lax.broadcasted_iota(jnp.int32, (n_r, tq), 0)
        row_bias = jnp.where(rows >= CMP_PAD, jnp.where(rows < band_start + nb, 0.0, NEG), NEG)
        imp = jnp.zeros((n_r, tq), F32)

        def scores(h):
            qh = q_ref[:, h * HEAD_DIM:(h + 1) * HEAD_DIM]
            s_scr[h, :n_r, :] = _dot_nt(kcc, qh) + row_bias
            s_scr[h, pl.ds(band_start, nb), :] += band_ref[h]

        scores(0)
        for h in range(HPG):
            if h + 1 < HPG:
                scores(h + 1)
            s = s_scr[h, :n_r, :]
            m = jnp.maximum(jnp.max(s, axis=0, keepdims=True), 0.5 * NEG)
            e = jnp.exp2(s - m)
            l = jnp.sum(e, axis=0, keepdims=True)
            pn = e * (1.0 / jnp.where(l > 0.0, l, 1.0))
            imp = imp + pn
            oct_h = _dot(vcct, pn.astype(BF16))
            oc_ref[:, h * HEAD_DIM:(h + 1) * HEAD_DIM] = oct_h.T.astype(oc_ref.dtype)
        mt = mt_ref[:n_b, :n_r]
        hi = imp.astype(BF16)
        rest = imp - hi.astype(F32)
        mid = rest.astype(BF16)
        lo = (rest - mid.astype(F32)).astype(BF16)
        slc = _dot(mt, hi) + _dot(mt, mid) + _dot(mt, lo)
        jj = lax.broadcasted_iota(jnp.int32, (n_b, tq), 0)
        t_blk = (i * tq + lax.broadcasted_iota(jnp.int32, (n_b, tq), 1)) // SEL_BLOCK
        forced = jnp.where(jj == 0, 1, jnp.where(jj == t_blk, 1, jnp.where(jj == t_blk - 1, 1, 0)))
        picked = jnp.where(forced == 1, 1.0, 0.0)
        score = jnp.where(forced == 1, -jnp.inf, jnp.where(jj <= t_blk, slc, NEG))
        for _ in range(SEL_TOPK - N_FORCED):
            best = jnp.max(score, axis=0, keepdims=True)
            first = jnp.min(jnp.where(score == best, jj, n_blk), axis=0, keepdims=True)
            hit = jj == first
            picked = jnp.where(hit, 1.0, picked)
            score = jnp.where(hit, -jnp.inf, score)
        neg_mask = jnp.where(picked > 0.5, jnp.where(jj <= t_blk, 0.0, NEG), NEG)
        if n_b < n_blk:
            neg_mask = jnp.concatenate([neg_mask, jnp.full((n_blk - n_b, tq), NEG, F32)], axis=0)
        nm_ref[...] = neg_mask.T.astype(nm_ref.dtype)

    step = LANES
    extents = list(range(step, n_rows, step)) + [n_rows]
    need_rows = band_start + nb
    for v, n_r in enumerate(extents):
        lower = extents[v - 1] if v else 0
        n_b = min(n_blk, -(-(n_r * CMP_STRIDE // SEL_BLOCK) // SUBLANES) * SUBLANES)
        pl.when(jnp.logical_and(need_rows > lower, need_rows <= n_r))(functools.partial(body, n_r, n_b))


def _cmp_attn(proj, kcc, vcct, band, mt, tq):
    s = proj.shape[0]
    n_rows = kcc.shape[1]
    nb = band.shape[1]
    kern = functools.partial(_cmp_attn_kernel, tq=tq, nb=nb)
    return pl.pallas_call(
        kern,
        out_shape=(jax.ShapeDtypeStruct((s, Q_WIDTH), BF16),
                   jax.ShapeDtypeStruct((s, N_KV_GROUPS * LANES), BF16)),
        grid=(N_KV_GROUPS, s // tq),
        in_specs=[
            pl.BlockSpec((tq, HPG * HEAD_DIM), lambda g, i: (i, g)),
            pl.BlockSpec((1, n_rows, HEAD_DIM), lambda g, i: (g, 0, 0)),
            pl.BlockSpec((1, HEAD_DIM, n_rows), lambda g, i: (g, 0, 0)),
            pl.BlockSpec((HPG, nb, tq), lambda g, i: (g, 0, 0)),
            pl.BlockSpec((LANES, n_rows), lambda g, i: (0, 0)),
        ],
        out_specs=(pl.BlockSpec((tq, HPG * HEAD_DIM), lambda g, i: (i, g)),
                   pl.BlockSpec((tq, LANES), lambda g, i: (i, g))),
        scratch_shapes=[pltpu.VMEM((HPG, n_rows, tq), F32)],
        compiler_params=_params("parallel", "parallel"),
    )(proj, kcc, vcct, band, mt)


def _lane_tile(x, width):
    return jnp.concatenate([x] * (width // x.shape[1]), axis=1)


def _softmax_update(s, v_aug, m_ref, acc_ref, rows):
    m_prev = m_ref[rows, :]
    m_next = jnp.maximum(m_prev, jnp.max(s, axis=1, keepdims=True))
    alpha = jnp.exp2(m_prev - m_next)
    p = jnp.exp2(s - _lane_tile(m_next, s.shape[1]))
    acc_ref[rows, :] = _lane_tile(alpha, acc_ref.shape[1]) * acc_ref[rows, :] + _dot(p.astype(BF16), v_aug)
    m_ref[rows, :] = m_next


def _softmax_once(s, v_aug):
    m = jnp.broadcast_to(jnp.max(s, axis=1, keepdims=True), (s.shape[0], LANES))
    p = jnp.exp2(s - _lane_tile(m, s.shape[1]))
    return _dot(p.astype(BF16), v_aug)


def _sel_win_kernel(q_ref, nm_ref, ks_ref, vs_ref, kw_ref, vw_ref, blk_ref, per_ref, g_ref, oc_ref, o_ref,
                    q4, m_s, a_s, bias_ref, *, tile):
    step = pl.program_id(1)
    t = tile
    first = step * Q_TILES

    @pl.when(step == 0)
    def _():
        tl = lax.broadcasted_iota(jnp.int32, (t, t), 0)
        kl = lax.broadcasted_iota(jnp.int32, (t, t), 1)
        edge = jnp.where(tl < kl, 0.0, NEG)
        for h in range(HPG):
            bias_ref[h, :, :t] = edge
            for which in range(2):
                rows = jnp.broadcast_to(per_ref[h, which:which + 1, :], (t, 2 * t))
                table = pltpu.roll(rows, 0, 1, stride=1, stride_axis=0)
                bias_ref[h, :, (which + 1) * t:(which + 2) * t] = table[:, :t]

    def stacked(qt, h):
        return slice((qt * HPG + h) * t, (qt * HPG + h + 1) * t)

    def tile_rows(qt):
        return slice(qt * t, (qt + 1) * t)

    for qt in range(Q_TILES):
        for h in range(HPG):
            q4[stacked(qt, h), :HEAD_DIM] = q_ref[tile_rows(qt), h * HEAD_DIM:(h + 1) * HEAD_DIM]
            q4[stacked(qt, h), HEAD_DIM:] = nm_ref[tile_rows(qt), :]
    m_s[...] = jnp.full(m_s.shape, -3e38, F32)
    a_s[...] = jnp.zeros(a_s.shape, F32)

    def sel_operands(row0, n_tiles):
        keys = pl.ds(row0, n_tiles * t)
        k_aug = jnp.concatenate([ks_ref[keys, :], blk_ref[keys, :]], axis=1)
        v_aug = jnp.concatenate([vs_ref[keys, :], jnp.ones((n_tiles * t, LANES), BF16)], axis=1)
        return k_aug, v_aug

    def sel_pass(qt, operands, n_biased):
        k_aug, v_aug = operands

        def score(h):
            s = _dot_nt(q4[stacked(qt, h), :], k_aug)
            plain = s.shape[1] - n_biased * t
            if n_biased:
                near = s[:, plain:] + bias_ref[h, :, (3 - n_biased) * t:]
                s = jnp.concatenate([s[:, :plain], near], axis=1) if plain else near
            return s

        def consume(h, s):
            _softmax_update(s, v_aug, m_s, a_s, stacked(qt, h))

        return score, consume

    def win_pass(qt, row0, n_tiles):
        keys = pl.ds(row0, n_tiles * t)
        v_aug = jnp.concatenate([vw_ref[keys, :], jnp.ones((n_tiles * t, LANES), BF16)], axis=1)

        def score(h):
            return _dot_nt(q4[stacked(qt, h), :HEAD_DIM], kw_ref[keys, :]) + bias_ref[h, :, (3 - n_tiles) * t:]

        def consume(h, s):
            acc_w = _softmax_once(s, v_aug)
            acc_s = a_s[stacked(qt, h), :]
            o_w = acc_w[:, :HEAD_DIM] * (1.0 / acc_w[:, HEAD_DIM:])
            o_s = acc_s[:, :HEAD_DIM] * (1.0 / acc_s[:, HEAD_DIM:])
            cols = slice(h * HEAD_DIM, (h + 1) * HEAD_DIM)
            gates = g_ref[tile_rows(qt), :]
            o = (gates[:, h:h + 1] * oc_ref[tile_rows(qt), cols] + gates[:, HPG + h:HPG + h + 1] * o_s
                 + gates[:, 2 * HPG + h:2 * HPG + h + 1] * o_w)
            o_ref[tile_rows(qt), cols] = o.astype(o_ref.dtype)

        return score, consume

    def run(passes):
        work = [(score, consume, h0) for score, consume in passes for h0 in range(0, HPG, 2)]
        ahead = [work[0][0](h) for h in (0, 1)]
        for n, (_, consume, h0) in enumerate(work):
            now = ahead
            if n + 1 < len(work):
                nxt_score, _, nxt_h0 = work[n + 1]
                ahead = [nxt_score(h) for h in (nxt_h0, nxt_h0 + 1)]
            for h, s in zip((h0, h0 + 1), now):
                consume(h, s)

    def far_passes(row0):
        operands = sel_operands(row0, FAR_TILES)
        return [sel_pass(qt, operands, 0) for qt in range(Q_TILES)]

    n_far = jnp.maximum(first - 1, 0)
    n_chunks = n_far // FAR_TILES
    chunk = FAR_TILES * t

    def far_pair(c, carry):
        row0 = pl.multiple_of(c * 2 * chunk, 2 * chunk)
        run(far_passes(row0) + far_passes(row0 + chunk))
        return carry

    lax.fori_loop(0, n_chunks // 2, far_pair, 0)

    @pl.when(n_chunks % 2 == 1)
    def _():
        run(far_passes(pl.multiple_of((n_chunks - 1) * chunk, chunk)))

    def tail(left):
        row0 = pl.multiple_of((n_far - left) * t, t)
        passes = []
        for qt in range(Q_TILES):
            n_sel = left + 2 + qt
            passes.append(sel_pass(qt, sel_operands(row0, n_sel), 2))
            passes.append(win_pass(qt, pl.multiple_of((first + qt - 2) * t, t), 3))
        run(passes)

    for left in range(FAR_TILES):
        if (left + 1) % Q_TILES == 0:
            pl.when(jnp.logical_and(step >= 1, n_far % FAR_TILES == left))(functools.partial(tail, left))

    @pl.when(step == 0)
    def _():
        passes = []
        for qt in range(Q_TILES):
            n_keys = min(qt + 1, 3)
            passes.append(sel_pass(qt, sel_operands(0, qt + 1), min(qt + 1, 2)))
            passes.append(win_pass(qt, (qt + 1 - n_keys) * t, n_keys))
        run(passes)


def _sel_win(proj, neg_mask, blk_onehot, periods, gates, o_c, tile):
    s = proj.shape[0]
    ks_blk = (Q_WIDTH + 2 * KV_WIDTH) // HEAD_DIM
    vs_blk = (Q_WIDTH + 3 * KV_WIDTH) // HEAD_DIM
    kw_blk = (Q_WIDTH + 4 * KV_WIDTH) // HEAD_DIM
    vw_blk = (Q_WIDTH + 5 * KV_WIDTH) // HEAD_DIM
    once = pl.Buffered(1)
    rows = Q_TILES * tile
    stacked_rows = HPG * rows
    kern = functools.partial(_sel_win_kernel, tile=tile)
    return pl.pallas_call(
        kern,
        out_shape=jax.ShapeDtypeStruct((s, Q_WIDTH), BF16),
        grid=(N_KV_GROUPS, s // rows),
        in_specs=[
            pl.BlockSpec((rows, HPG * HEAD_DIM), lambda g, i: (i, g)),
            pl.BlockSpec((rows, LANES), lambda g, i: (i, g)),
            pl.BlockSpec((s, HEAD_DIM), lambda g, i: (0, ks_blk + g), pipeline_mode=once),
            pl.BlockSpec((s, HEAD_DIM), lambda g, i: (0, vs_blk + g), pipeline_mode=once),
            pl.BlockSpec((s, HEAD_DIM), lambda g, i: (0, kw_blk + g), pipeline_mode=once),
            pl.BlockSpec((s, HEAD_DIM), lambda g, i: (0, vw_blk + g), pipeline_mode=once),
            pl.BlockSpec((s, LANES), lambda g, i: (0, 0), pipeline_mode=once),
            pl.BlockSpec((HPG, 2, 2 * tile), lambda g, i: (g, 0, 0)),
            pl.BlockSpec((rows, LANES), lambda g, i: (i, g)),
            pl.BlockSpec((rows, HPG * HEAD_DIM), lambda g, i: (i, g)),
        ],
        out_specs=pl.BlockSpec((rows, HPG * HEAD_DIM), lambda g, i: (i, g)),
        scratch_shapes=[
            pltpu.VMEM((stacked_rows, 2 * HEAD_DIM), BF16),
            pltpu.VMEM((stacked_rows, LANES), F32),
            pltpu.VMEM((stacked_rows, 2 * HEAD_DIM), F32),
            pltpu.VMEM((HPG, tile, 3 * tile), F32),
        ],
        compiler_params=_params("arbitrary", "arbitrary"),
    )(proj, neg_mask, proj, proj, proj, proj, blk_onehot, periods, gates, o_c)


def _rel_bucket_np(dist):
    n = np.maximum(dist, 0)
    max_exact = REL_BUCKETS // 2
    large = max_exact + (np.log(np.maximum(n, 1).astype(np.float32) / np.float32(max_exact))
                         / np.float32(math.log(REL_MAX_DIST / max_exact))
                         * np.float32(REL_BUCKETS - max_exact)).astype(np.int32)
    large = np.minimum(large, REL_BUCKETS - 1)
    return np.where(n < max_exact, n, large).astype(np.int32)


def _far_distance():
    d = np.arange(4 * REL_MAX_DIST)
    b = _rel_bucket_np(d)
    assert b[-1] == REL_BUCKETS - 1
    return int(np.max(np.nonzero(b != REL_BUCKETS - 1)[0])) + 1


def _bias_table(rel_bias, dist):
    bucket = jnp.asarray(_rel_bucket_np(dist))
    shifted = (rel_bias[bucket] - rel_bias[REL_BUCKETS - 1]) * LOG2E
    tab = jnp.where(jnp.asarray(dist >= 0)[..., None], shifted, NEG)
    return jnp.moveaxis(tab, -1, 0).astype(F32)


def _bias_period(rel_bias, offset, t):
    x = np.arange(2 * t)
    return _bias_table(rel_bias, np.where(x < t, offset - x, offset + 2 * t - x))


def _nsa_mixer(h, hn, rel_bias, w_in, w_out, layer, pos_k, w1_k, w2_k, pos_v, w1_v, w2_v, next_norm_g):
    s = h.shape[0]
    n_sel = s // SEL_BLOCK
    ncp = s // CMP_STRIDE
    tile = ATT_TILE
    far = _far_distance()
    assert n_sel <= LANES and s % CMP_TQ == 0 and s % (Q_TILES * tile) == 0 and FAR_TILES % Q_TILES == 0
    assert far <= tile and far <= CMP_STRIDE * CMP_PAD - CMP_BLOCK + 1 + CMP_STRIDE

    col_scale = jnp.where(jnp.arange(PROJ_MAIN) < Q_WIDTH, HEAD_DIM ** -0.5 * LOG2E, 1.0).astype(F32)[None, :]
    w_in_t = jnp.swapaxes(w_in, 1, 2)
    proj = _matmul(_ep_scale, hn, w_in_t, layer, PROJ_MAIN, BF16, 2048, 512, row_extra=col_scale,
                   w_rows_are_outputs=True)
    w_gate = w_in_t[layer, PROJ_MAIN:].reshape(N_GATES, N_KV_GROUPS, HPG, D_MODEL).transpose(1, 0, 2, 3)
    w_gate = w_gate.reshape(N_KV_GROUPS, N_GATES * HPG, D_MODEL)
    w_gate = jnp.pad(w_gate, ((0, 0), (0, LANES - N_GATES * HPG), (0, 0))).reshape(1, N_KV_GROUPS * LANES, D_MODEL)
    gates = _matmul(jax.nn.sigmoid, hn, w_gate, 0, N_KV_GROUPS * LANES, F32, 2048, 512, w_rows_are_outputs=True)

    w1 = jnp.stack([w1_k, w1_v]).astype(BF16)
    w2 = jnp.stack([w2_k, w2_v]).astype(BF16)
    pos = jnp.stack([pos_k.reshape(1, -1), pos_v.reshape(1, -1)]).astype(BF16)
    pos = jnp.broadcast_to(pos, (2, SUBLANES, CMP_BLOCK * HEAD_DIM))
    cc = _compress(proj, w1, w2, pos)
    kcc = jnp.pad(cc[0], ((0, 0), (CMP_PAD, 0), (0, 0)))
    vcct = jnp.pad(cc[1], ((0, 0), (CMP_PAD, 0), (0, 0))).transpose(0, 2, 1)
    n_rows = ncp + CMP_PAD

    nb = CMP_TQ // CMP_STRIDE + CMP_PAD
    first_dist = -CMP_STRIDE * (nb - 1 - CMP_PAD) - (CMP_BLOCK - 1)
    by_dist = _bias_table(rel_bias, first_dist + np.arange(CMP_TQ + CMP_STRIDE * (nb - 1)))
    band = jnp.stack([by_dist[:, CMP_STRIDE * (nb - 1 - b):CMP_STRIDE * (nb - 1 - b) + CMP_TQ] for b in range(nb)], axis=1)
    ratio = SEL_BLOCK // CMP_STRIDE
    lo = CMP_BLOCK // CMP_STRIDE - 1
    c_of_row = np.arange(n_rows)[None, :] - CMP_PAD
    j_of = np.arange(LANES)[:, None]
    mt = ((c_of_row >= ratio * j_of - lo) & (c_of_row <= ratio * j_of + ratio - 1)
          & (c_of_row >= 0) & (c_of_row <= ncp - 2) & (j_of < n_sel)).astype(np.float32)
    o_c, neg_mask = _cmp_attn(proj, kcc, vcct, band, jnp.asarray(mt, BF16), CMP_TQ)

    onehot = (np.arange(s)[:, None] // SEL_BLOCK == np.arange(LANES)[None, :]).astype(np.float32)
    periods = jnp.stack([_bias_period(rel_bias, tile, tile), _bias_period(rel_bias, 0, tile)], axis=1)
    o = _sel_win(proj, neg_mask, jnp.asarray(onehot, BF16), periods, gates, o_c, tile)
    return _matmul_resid_norm(o, w_out, layer, h, next_norm_g, 256)


def _s5_kernel(x_ref, lam_ref, bt_ref, c_ref, d_ref, y_ref, vt, yt, tmask, *, chunk):
    L = chunk
    half = L // 2
    width = L * SSM_GROUP
    n_chunks = x_ref.shape[0] // L
    by_step = jnp.swapaxes(x_ref[...].reshape(n_chunks, L, LANES), 0, 1)
    for tau in range(L):
        vt[tau] = by_step[tau].T

    @pl.when(pl.program_id(0) == 0)
    def _():
        dst = lax.broadcasted_iota(jnp.int32, (width, width), 0) // SSM_GROUP
        src = lax.broadcasted_iota(jnp.int32, (width, width), 1) // SSM_GROUP
        tmask[...] = jnp.where(src <= dst, 1.0, 0.0)

    def cmul(xr, xi, yr, yi):
        return xr * yr - xi * yi, xr * yi + xi * yr

    n2 = 2 * SSM_STATE
    sub = lax.broadcasted_iota(jnp.int32, (SUBLANES, 1), 0)
    consts = jnp.where(sub == 0, 1.0, jnp.where(sub == 1, float(half + 1), jnp.where(sub == 2, float(half - 1), float(L))))
    tau_col = lax.broadcasted_iota(jnp.int32, (L, 1), 0).astype(F32)
    row = lax.broadcasted_iota(jnp.int32, (n_chunks, n2), 0)
    conj = jnp.where(lax.broadcasted_iota(jnp.int32, (1, n2), 1) < SSM_STATE, 1.0, -1.0)

    def group(g):
        lam = lam_ref[g]
        a_re, a_im = lam[0:1], lam[1:2]
        dt = jnp.exp(lam[2:3])
        log_re, log_im = a_re * dt, a_im * dt

        def cpow(e):
            mag, cos, sin = jnp.exp(e * log_re), jnp.cos(e * log_im), jnp.sin(e * log_im)
            inv = 1.0 / mag
            return mag * cos, mag * sin, inv * cos, -(inv * sin)

        kr, ki, _, _ = cpow(consts)
        pr, pi = kr[0:1] - 1.0, ki[0:1]
        den = a_re * a_re + a_im * a_im
        cf_r = (pr * a_re + pi * a_im) / den
        cf_i = (pi * a_re - pr * a_im) / den
        bt_c, bt_s = bt_ref[g, 0], bt_ref[g, 1]
        bb_c = cf_r * bt_c + cf_i * bt_s
        bb_s = cf_r * bt_s - cf_i * bt_c
        c_c, c_s = c_ref[g, 0], c_ref[g, 1]

        e1r, e1i, e2r, e2i = cpow(tau_col - half)
        e3r, e3i = cmul(e1r, e1i, kr[1:2], ki[1:2])
        e4r, e4i = cmul(e2r, e2i, kr[2:3], ki[2:3])

        def outer(er, ei, w_c, w_s):
            return (er[:, None, :] * w_c[None, :, :] + ei[:, None, :] * w_s[None, :, :]).reshape(width, n2)

        qm = outer(e1r, e1i, c_c, c_s)
        km_conj = outer(e2r, e2i, bb_c * conj, bb_s * conj)
        qc = outer(e3r, e3i, c_c, c_s)
        wz = outer(e4r, e4i, bb_c, bb_s)

        chans = pl.ds(pl.multiple_of(g * SSM_GROUP, SSM_GROUP), SSM_GROUP)
        ub = vt[:, chans, :].reshape(width, n_chunks).astype(BF16)
        yield

        tt = _dot_nt(qm.astype(BF16), km_conj.astype(BF16))
        x = _dot(wz.T.astype(BF16), ub).T
        yield
        y = _dot((tt * tmask[...]).astype(BF16), ub)
        p_r, p_i = kr[3:4], ki[3:4]
        d = 1
        while d < n_chunks:
            s = jnp.where(row >= d, pltpu.roll(x, d, 0), 0.0)
            x = x + s * p_r - pltpu.roll(s, SSM_STATE, 1) * (p_i * conj)
            p_r, p_i = cmul(p_r, p_i, p_r, p_i)
            d *= 2
        prev_conj = jnp.where(row >= 1, pltpu.roll(x, 1, 0), 0.0) * conj
        yield
        y = y + _dot_nt(qc.astype(BF16), prev_conj.astype(BF16))
        yt[:, chans, :] = y.reshape(L, SSM_GROUP, n_chunks)

    def group_batch(k, carry):
        running = [group(S5_GROUPS_PER_TRIP * k + j) for j in range(S5_GROUPS_PER_TRIP)]
        finished = object()
        while running:
            running = [gen for gen in running if next(gen, finished) is not finished]
        return carry

    lax.fori_loop(0, LANES // SSM_GROUP // S5_GROUPS_PER_TRIP, group_batch, 0)
    by_chunk = jnp.swapaxes(jnp.stack([yt[tau].T for tau in range(L)], axis=0), 0, 1)
    y = by_chunk.reshape(n_chunks * L, LANES) + d_ref[...] * x_ref[...]
    y_ref[...] = jax.nn.gelu(y).astype(y_ref.dtype)


def _s5_mixer(h, hn, a_re, a_im, log_dt, b_re, b_im, c_re, c_im, d_skip, w_glu, layer):
    s = h.shape[0]
    L = S5_CHUNK
    n_chunks = s // L
    width = L * SSM_GROUP
    gpb = LANES // SSM_GROUP
    def packed(re, im):
        return jnp.stack([jnp.concatenate([re, im], -1), jnp.concatenate([-im, re], -1)], axis=1).astype(F32)

    lam = jnp.stack([a_re, a_im, jnp.broadcast_to(log_dt[:, None], a_re.shape)], axis=1).astype(F32)
    lam = jnp.concatenate([lam, lam], axis=-1)
    bt = packed(b_re.transpose(0, 2, 1), b_im.transpose(0, 2, 1))
    cc = packed(c_re, c_im)
    dd = d_skip.astype(F32).reshape(1, D_MODEL)
    y = pl.pallas_call(
        functools.partial(_s5_kernel, chunk=L),
        out_shape=jax.ShapeDtypeStruct((s, D_MODEL), BF16),
        grid=(SSM_GROUPS // gpb,),
        in_specs=[
            pl.BlockSpec((s, LANES), lambda b: (0, b)),
            pl.BlockSpec((gpb, 3, 2 * SSM_STATE), lambda b: (b, 0, 0)),
            pl.BlockSpec((gpb, 2, SSM_GROUP, 2 * SSM_STATE), lambda b: (b, 0, 0, 0)),
            pl.BlockSpec((gpb, 2, SSM_GROUP, 2 * SSM_STATE), lambda b: (b, 0, 0, 0)),
            pl.BlockSpec((1, LANES), lambda b: (0, b)),
        ],
        out_specs=pl.BlockSpec((s, LANES), lambda b: (0, b)),
        scratch_shapes=[
            pltpu.VMEM((L, LANES, n_chunks), F32),
            pltpu.VMEM((L, LANES, n_chunks), F32),
            pltpu.VMEM((width, width), F32),
        ],
        compiler_params=_params("arbitrary"),
    )(hn, lam, bt, cc, dd)
    return _matmul(_ep_glu_resid, y, w_glu, layer, D_MODEL, F32, 1024, 512, b_halves=2, tile_extra=h)


def kernel(x, rel_bias, mix_norm_g, ffn_norm_g, final_norm_g, nsa_w_in, nsa_w_out, cmp_pos_k, cmp_w1_k, cmp_w2_k, cmp_pos_v, cmp_w1_v, cmp_w2_v, s5_A_re, s5_A_im, s5_log_dt, s5_B_re, s5_B_im, s5_C_re, s5_C_im, s5_D, s5_w_glu, ffn_w_in, ffn_w_out):
    assert x.shape[0] == 1
    h = x[0]
    hn = _rmsnorm(h, mix_norm_g[0], BF16)
    h, hn = _nsa_mixer(h, hn, rel_bias, nsa_w_in, nsa_w_out, 0, cmp_pos_k[0], cmp_w1_k[0], cmp_w2_k[0],
                       cmp_pos_v[0], cmp_w1_v[0], cmp_w2_v[0], ffn_norm_g[0])
    h = _swiglu_ffn(h, hn, ffn_w_in, ffn_w_out, 0)
    hn = _rmsnorm(h, mix_norm_g[1], F32)
    h = _s5_mixer(h, hn, s5_A_re[0], s5_A_im[0], s5_log_dt[0], s5_B_re[0], s5_B_im[0], s5_C_re[0], s5_C_im[0],
                  s5_D[0], s5_w_glu, 0)
    h = _swiglu_ffn(h, _rmsnorm(h, ffn_norm_g[1], BF16), ffn_w_in, ffn_w_out, 1)
    return _rmsnorm(h, final_norm_g, x.dtype)[None]
```

```python
import functools
import math

import numpy as np
import jax
import jax.numpy as jnp
from jax import lax
from jax.experimental import pallas as pl
from jax.experimental.pallas import tpu as pltpu

D_MODEL = 2048
N_HEADS = 16
HEAD_DIM = 128
N_KV_GROUPS = 4
HPG = N_HEADS // N_KV_GROUPS
CMP_BLOCK = 32
CMP_STRIDE = 16
CMP_HIDDEN = 2 * HEAD_DIM
SEL_BLOCK = 64
SEL_TOPK = 16
N_FORCED = 3
WINDOW = 512
N_GATES = 3
KV_WIDTH = N_KV_GROUPS * HEAD_DIM
Q_WIDTH = N_HEADS * HEAD_DIM
PROJ_MAIN = Q_WIDTH + 6 * KV_WIDTH
REL_BUCKETS = 32
REL_MAX_DIST = 128
SSM_GROUP = 16
SSM_GROUPS = D_MODEL // SSM_GROUP
SSM_STATE = 64
D_FF = ((8 * D_MODEL + 2) // 3 + 255) // 256 * 256
RMS_EPS = 1e-6
NEG = -1e30
LOG2E = math.log2(math.e)

LANES = 128
SUBLANES = 8
VMEM_LIMIT = 52 * 1024 * 1024

ATT_TILE = WINDOW // 2
FAR_TILES = 4
Q_TILES = 2
CMP_TQ = 256
CMP_PAD = 8
S5_CHUNK = 32
S5_GROUPS_PER_TRIP = 4

BF16 = jnp.bfloat16
F32 = jnp.float32


def _dot(a, b):
    return jnp.dot(a, b, preferred_element_type=F32)


def _dot_nt(a, b):
    return lax.dot_general(a, b, (((1,), (1,)), ((), ())), preferred_element_type=F32)


def _params(*sem):
    return pltpu.CompilerParams(dimension_semantics=sem, vmem_limit_bytes=VMEM_LIMIT)


def _rmsnorm_kernel(x_ref, g_ref, o_ref):
    x = x_ref[...]
    ms = jnp.mean(x * x, axis=-1, keepdims=True)
    o_ref[...] = (x * lax.rsqrt(ms + RMS_EPS) * g_ref[...]).astype(o_ref.dtype)


def _rmsnorm(x, g, out_dtype, tm=512):
    s, d = x.shape
    return pl.pallas_call(
        _rmsnorm_kernel,
        out_shape=jax.ShapeDtypeStruct((s, d), out_dtype),
        grid=(s // tm,),
        in_specs=[pl.BlockSpec((tm, d), lambda i: (i, 0)), pl.BlockSpec((1, d), lambda i: (0, 0))],
        out_specs=pl.BlockSpec((tm, d), lambda i: (i, 0)),
        compiler_params=_params("parallel"),
    )(x, g.reshape(1, d))


def _mm_kernel(*refs, n_b, n_extra, epilogue, w_rows_are_outputs):
    a_ref = refs[0]
    b_refs = refs[1:1 + n_b]
    extra_refs = refs[1 + n_b:1 + n_b + n_extra]
    o_ref = refs[1 + n_b + n_extra]
    w_scr = refs[2 + n_b + n_extra:]

    @pl.when(pl.program_id(1) == 0)
    def _():
        for b_ref, w in zip(b_refs, w_scr):
            w[...] = b_ref[...].astype(BF16)

    a = a_ref[...]
    z = [(_dot_nt if w_rows_are_outputs else _dot)(a, w[...]) for w in w_scr]
    o_ref[...] = epilogue(*z, *[e[...] for e in extra_refs]).astype(o_ref.dtype)


def _ep_scale(z, scale):
    return z * scale


def _ep_resid(z, resid):
    return resid + z


def _ep_swiglu(za, zb):
    return jax.nn.silu(za) * zb


def _ep_glu_resid(za, zb, resid):
    return resid + za * jax.nn.sigmoid(zb)


def _matmul(epilogue, a, b, layer, n_out, out_dtype, tm, tn, *, b_halves=1, row_extra=None, tile_extra=None,
            w_rows_are_outputs=False):
    m, k = a.shape
    nj = n_out // tn
    estimate = (b_halves * k * tn * (2 * 4 + 2) + 2 * tm * k * 2 + 4 * tm * tn * 4 + b_halves * tm * tn * 4)
    b_mode = pl.Buffered(1) if estimate > VMEM_LIMIT else None
    in_specs = [pl.BlockSpec((tm, k), lambda j, i: (i, 0))]
    args = [a]
    for half in range(b_halves):
        if w_rows_are_outputs:
            spec = pl.BlockSpec((None, tn, k), functools.partial(lambda j, i, o: (layer, j + o, 0), o=half * nj),
                                pipeline_mode=b_mode)
        else:
            spec = pl.BlockSpec((None, k, tn), functools.partial(lambda j, i, o: (layer, 0, j + o), o=half * nj),
                                pipeline_mode=b_mode)
        in_specs.append(spec)
        args.append(b)
    extras = []
    if row_extra is not None:
        in_specs.append(pl.BlockSpec((1, tn), lambda j, i: (0, j)))
        extras.append(row_extra)
    if tile_extra is not None:
        in_specs.append(pl.BlockSpec((tm, tn), lambda j, i: (i, j)))
        extras.append(tile_extra)
    kern = functools.partial(_mm_kernel, n_b=b_halves, n_extra=len(extras), epilogue=epilogue,
                             w_rows_are_outputs=w_rows_are_outputs)
    return pl.pallas_call(
        kern,
        out_shape=jax.ShapeDtypeStruct((m, n_out), out_dtype),
        grid=(nj, m // tm),
        in_specs=in_specs,
        out_specs=pl.BlockSpec((tm, tn), lambda j, i: (i, j)),
        scratch_shapes=[pltpu.VMEM((tn, k) if w_rows_are_outputs else (k, tn), BF16) for _ in range(b_halves)],
        compiler_params=_params("arbitrary", "arbitrary"),
    )(*args, *extras)


def _mm_resid_norm_kernel(a_ref, b_ref, r_ref, g_ref, h_ref, hn_ref, w_scr):
    @pl.when(pl.program_id(0) == 0)
    def _():
        w_scr[...] = b_ref[...].astype(BF16)

    h = r_ref[...] + _dot(a_ref[...], w_scr[...])
    h_ref[...] = h
    ms = jnp.mean(h * h, axis=-1, keepdims=True)
    hn_ref[...] = (h * lax.rsqrt(ms + RMS_EPS) * g_ref[...]).astype(hn_ref.dtype)


def _matmul_resid_norm(a, b, layer, resid, g, tm):
    m, k = a.shape
    n = b.shape[2]
    return pl.pallas_call(
        _mm_resid_norm_kernel,
        out_shape=(jax.ShapeDtypeStruct((m, n), F32), jax.ShapeDtypeStruct((m, n), BF16)),
        grid=(m // tm,),
        in_specs=[
            pl.BlockSpec((tm, k), lambda i: (i, 0)),
            pl.BlockSpec((None, k, n), lambda i: (layer, 0, 0), pipeline_mode=pl.Buffered(1)),
            pl.BlockSpec((tm, n), lambda i: (i, 0)),
            pl.BlockSpec((1, n), lambda i: (0, 0)),
        ],
        out_specs=(pl.BlockSpec((tm, n), lambda i: (i, 0)), pl.BlockSpec((tm, n), lambda i: (i, 0))),
        scratch_shapes=[pltpu.VMEM((k, n), BF16)],
        compiler_params=_params("arbitrary"),
    )(a, b, resid, g.reshape(1, n))


def _swiglu_ffn(h, hn, w_in, w_out, layer):
    act = _matmul(_ep_swiglu, hn, w_in, layer, D_FF, BF16, 1024, 512, b_halves=2)
    return _matmul(_ep_resid, act, w_out, layer, D_MODEL, F32, 512, 512, tile_extra=h)


def _compress_kernel(x_ref, w1_ref, w2_ref, pos_ref, o_ref):
    ncp = x_ref.shape[0] // CMP_STRIDE
    top = jnp.zeros((ncp, CMP_HIDDEN), F32)
    bot = jnp.zeros((ncp, CMP_HIDDEN), F32)
    by_offset = jnp.swapaxes(x_ref[...].reshape(ncp, CMP_STRIDE, HEAD_DIM), 0, 1)
    for r in range(CMP_STRIDE):
        x_r = by_offset[r]
        top = top + _dot(x_r, w1_ref[0, r * HEAD_DIM:(r + 1) * HEAD_DIM, :])
        bot = bot + _dot(x_r, w1_ref[0, (CMP_STRIDE + r) * HEAD_DIM:(CMP_STRIDE + r + 1) * HEAD_DIM, :])
    posb = _dot(pos_ref[0], w1_ref[0])[0:1]
    hid = top + pltpu.roll(bot, ncp - 1, 0) + posb
    o_ref[0, 0] = _dot(jax.nn.gelu(hid).astype(BF16), w2_ref[0]).astype(o_ref.dtype)


def _compress(proj, w1, w2, pos):
    s = proj.shape[0]
    ncp = s // CMP_STRIDE
    kc_blk = Q_WIDTH // HEAD_DIM
    return pl.pallas_call(
        _compress_kernel,
        out_shape=jax.ShapeDtypeStruct((2, N_KV_GROUPS, ncp, HEAD_DIM), BF16),
        grid=(2, N_KV_GROUPS),
        in_specs=[
            pl.BlockSpec((s, HEAD_DIM), lambda kv, gi: (0, kc_blk + kv * N_KV_GROUPS + gi)),
            pl.BlockSpec((1, CMP_BLOCK * HEAD_DIM, CMP_HIDDEN), lambda kv, gi: (kv, 0, 0)),
            pl.BlockSpec((1, CMP_HIDDEN, HEAD_DIM), lambda kv, gi: (kv, 0, 0)),
            pl.BlockSpec((1, SUBLANES, CMP_BLOCK * HEAD_DIM), lambda kv, gi: (kv, 0, 0)),
        ],
        out_specs=pl.BlockSpec((1, 1, ncp, HEAD_DIM), lambda kv, gi: (kv, gi, 0, 0)),
        compiler_params=_params("parallel", "parallel"),
    )(proj, w1, w2, pos)


def _cmp_attn_kernel(q_ref, kcc_ref, vcct_ref, band_ref, mt_ref, oc_ref, nm_ref, s_scr, *, tq, nb):
    i = pl.program_id(1)
    band_start = pl.multiple_of(i * (tq // CMP_STRIDE), SUBLANES)
    n_rows = kcc_ref.shape[1]
    n_blk = mt_ref.shape[0]

    def body(n_r, n_b):
        kcc = kcc_ref[0, :n_r, :]
        vcct = vcct_ref[0, :, :n_r]
        rows = lax.broadcasted_iota(jnp.int32, (n_r, tq), 0)
        row_bias = jnp.where(rows >= CMP_PAD, jnp.where(rows < band_start + nb, 0.0, NEG), NEG)
        imp = jnp.zeros((n_r, tq), F32)

        def scores(h):
            qh = q_ref[:, h * HEAD_DIM:(h + 1) * HEAD_DIM]
            s_scr[h, :n_r, :] = _dot_nt(kcc, qh) + row_bias
            s_scr[h, pl.ds(band_start, nb), :] += band_ref[h]

        scores(0)
        for h in range(HPG):
            if h + 1 < HPG:
                scores(h + 1)
            s = s_scr[h, :n_r, :]
            m = jnp.maximum(jnp.max(s, axis=0, keepdims=True), 0.5 * NEG)
            e = jnp.exp2(s - m)
            l = jnp.sum(e, axis=0, keepdims=True)
            pn = e * (1.0 / jnp.where(l > 0.0, l, 1.0))
            imp = imp + pn
            oct_h = _dot(vcct, pn.astype(BF16))
            oc_ref[:, h * HEAD_DIM:(h + 1) * HEAD_DIM] = oct_h.T
        mt = mt_ref[:n_b, :n_r]
        hi = imp.astype(BF16)
        rest = imp - hi.astype(F32)
        mid = rest.astype(BF16)
        lo = (rest - mid.astype(F32)).astype(BF16)
        slc = _dot(mt, hi) + _dot(mt, mid) + _dot(mt, lo)
        jj = lax.broadcasted_iota(jnp.int32, (n_b, tq), 0)
        t_blk = (i * tq + lax.broadcasted_iota(jnp.int32, (n_b, tq), 1)) // SEL_BLOCK
        forced = jnp.where(jj == 0, 1, jnp.where(jj == t_blk, 1, jnp.where(jj == t_blk - 1, 1, 0)))
        picked = jnp.where(forced == 1, 1.0, 0.0)
        score = jnp.where(forced == 1, -jnp.inf, jnp.where(jj <= t_blk, slc, NEG))
        for _ in range(SEL_TOPK - N_FORCED):
            best = jnp.max(score, axis=0, keepdims=True)
            first = jnp.min(jnp.where(score == best, jj, n_blk), axis=0, keepdims=True)
            hit = jj == first
            picked = jnp.where(hit, 1.0, picked)
            score = jnp.where(hit, -jnp.inf, score)
        neg_mask = jnp.where(picked > 0.5, jnp.where(jj <= t_blk, 0.0, NEG), NEG)
        if n_b < n_blk:
            neg_mask = jnp.concatenate([neg_mask, jnp.full((n_blk - n_b, tq), NEG, F32)], axis=0)
        nm_ref[...] = neg_mask.T.astype(nm_ref.dtype)

    step = LANES
    extents = list(range(step, n_rows, step)) + [n_rows]
    need_rows = band_start + nb
    for v, n_r in enumerate(extents):
        lower = extents[v - 1] if v else 0
        n_b = min(n_blk, -(-(n_r * CMP_STRIDE // SEL_BLOCK) // SUBLANES) * SUBLANES)
        pl.when(jnp.logical_and(need_rows > lower, need_rows <= n_r))(functools.partial(body, n_r, n_b))


def _cmp_attn(proj, kcc, vcct, band, mt, tq):
    s = proj.shape[0]
    n_rows = kcc.shape[1]
    nb = band.shape[1]
    kern = functools.partial(_cmp_attn_kernel, tq=tq, nb=nb)
    return pl.pallas_call(
        kern,
        out_shape=(jax.ShapeDtypeStruct((s, Q_WIDTH), F32),
                   jax.ShapeDtypeStruct((s, N_KV_GROUPS * LANES), BF16)),
        grid=(N_KV_GROUPS, s // tq),
        in_specs=[
            pl.BlockSpec((tq, HPG * HEAD_DIM), lambda g, i: (i, g)),
            pl.BlockSpec((1, n_rows, HEAD_DIM), lambda g, i: (g, 0, 0)),
            pl.BlockSpec((1, HEAD_DIM, n_rows), lambda g, i: (g, 0, 0)),
            pl.BlockSpec((HPG, nb, tq), lambda g, i: (g, 0, 0)),
            pl.BlockSpec((LANES, n_rows), lambda g, i: (0, 0)),
        ],
        out_specs=(pl.BlockSpec((tq, HPG * HEAD_DIM), lambda g, i: (i, g)),
                   pl.BlockSpec((tq, LANES), lambda g, i: (i, g))),
        scratch_shapes=[pltpu.VMEM((HPG, n_rows, tq), F32)],
        compiler_params=_params("parallel", "parallel"),
    )(proj, kcc, vcct, band, mt)


def _lane_tile(x, width):
    return jnp.concatenate([x] * (width // x.shape[1]), axis=1)


def _softmax_update(s, v_aug, m_ref, acc_ref, rows):
    m_prev = m_ref[rows, :]
    m_next = jnp.maximum(m_prev, jnp.max(s, axis=1, keepdims=True))
    alpha = jnp.exp2(m_prev - m_next)
    p = jnp.exp2(s - _lane_tile(m_next, s.shape[1]))
    acc_ref[rows, :] = _lane_tile(alpha, acc_ref.shape[1]) * acc_ref[rows, :] + _dot(p.astype(BF16), v_aug)
    m_ref[rows, :] = m_next


def _softmax_once(s, v_aug):
    m = jnp.broadcast_to(jnp.max(s, axis=1, keepdims=True), (s.shape[0], LANES))
    p = jnp.exp2(s - _lane_tile(m, s.shape[1]))
    return _dot(p.astype(BF16), v_aug)


def _sel_win_kernel(q_ref, nm_ref, ks_ref, vs_ref, kw_ref, vw_ref, blk_ref, per_ref, g_ref, oc_ref, o_ref,
                    q4, m_s, a_s, bias_ref, *, tile):
    step = pl.program_id(1)
    t = tile
    first = step * Q_TILES

    @pl.when(step == 0)
    def _():
        tl = lax.broadcasted_iota(jnp.int32, (t, t), 0)
        kl = lax.broadcasted_iota(jnp.int32, (t, t), 1)
        edge = jnp.where(tl < kl, 0.0, NEG)
        for h in range(HPG):
            bias_ref[h, :, :t] = edge
            for which in range(2):
                rows = jnp.broadcast_to(per_ref[h, which:which + 1, :], (t, 2 * t))
                table = pltpu.roll(rows, 0, 1, stride=1, stride_axis=0)
                bias_ref[h, :, (which + 1) * t:(which + 2) * t] = table[:, :t]

    def stacked(qt, h):
        return slice((qt * HPG + h) * t, (qt * HPG + h + 1) * t)

    def tile_rows(qt):
        return slice(qt * t, (qt + 1) * t)

    for qt in range(Q_TILES):
        for h in range(HPG):
            q4[stacked(qt, h), :HEAD_DIM] = q_ref[tile_rows(qt), h * HEAD_DIM:(h + 1) * HEAD_DIM]
            q4[stacked(qt, h), HEAD_DIM:] = nm_ref[tile_rows(qt), :]
    m_s[...] = jnp.full(m_s.shape, -3e38, F32)
    a_s[...] = jnp.zeros(a_s.shape, F32)

    def sel_operands(row0, n_tiles):
        keys = pl.ds(row0, n_tiles * t)
        k_aug = jnp.concatenate([ks_ref[keys, :], blk_ref[keys, :]], axis=1)
        v_aug = jnp.concatenate([vs_ref[keys, :], jnp.ones((n_tiles * t, LANES), BF16)], axis=1)
        return k_aug, v_aug

    def sel_pass(qt, operands, n_biased):
        k_aug, v_aug = operands

        def score(h):
            s = _dot_nt(q4[stacked(qt, h), :], k_aug)
            plain = s.shape[1] - n_biased * t
            if n_biased:
                near = s[:, plain:] + bias_ref[h, :, (3 - n_biased) * t:]
                s = jnp.concatenate([s[:, :plain], near], axis=1) if plain else near
            return s

        def consume(h, s):
            _softmax_update(s, v_aug, m_s, a_s, stacked(qt, h))

        return score, consume

    def win_pass(qt, row0, n_tiles):
        keys = pl.ds(row0, n_tiles * t)
        v_aug = jnp.concatenate([vw_ref[keys, :], jnp.ones((n_tiles * t, LANES), BF16)], axis=1)

        def score(h):
            return _dot_nt(q4[stacked(qt, h), :HEAD_DIM], kw_ref[keys, :]) + bias_ref[h, :, (3 - n_tiles) * t:]

        def consume(h, s):
            acc_w = _softmax_once(s, v_aug)
            acc_s = a_s[stacked(qt, h), :]
            o_w = acc_w[:, :HEAD_DIM] * (1.0 / acc_w[:, HEAD_DIM:])
            o_s = acc_s[:, :HEAD_DIM] * (1.0 / acc_s[:, HEAD_DIM:])
            cols = slice(h * HEAD_DIM, (h + 1) * HEAD_DIM)
            gates = g_ref[tile_rows(qt), :]
            o = (gates[:, h:h + 1] * oc_ref[tile_rows(qt), cols] + gates[:, HPG + h:HPG + h + 1] * o_s
                 + gates[:, 2 * HPG + h:2 * HPG + h + 1] * o_w)
            o_ref[tile_rows(qt), cols] = o.astype(o_ref.dtype)

        return score, consume

    def run(passes):
        work = [(score, consume, h0) for score, consume in passes for h0 in range(0, HPG, 2)]
        ahead = [work[0][0](h) for h in (0, 1)]
        for n, (_, consume, h0) in enumerate(work):
            now = ahead
            if n + 1 < len(work):
                nxt_score, _, nxt_h0 = work[n + 1]
                ahead = [nxt_score(h) for h in (nxt_h0, nxt_h0 + 1)]
            for h, s in zip((h0, h0 + 1), now):
                consume(h, s)

    def far_passes(row0):
        operands = sel_operands(row0, FAR_TILES)
        return [sel_pass(qt, operands, 0) for qt in range(Q_TILES)]

    n_far = jnp.maximum(first - 1, 0)
    n_chunks = n_far // FAR_TILES
    chunk = FAR_TILES * t

    def far_pair(c, carry):
        row0 = pl.multiple_of(c * 2 * chunk, 2 * chunk)
        run(far_passes(row0) + far_passes(row0 + chunk))
        return carry

    lax.fori_loop(0, n_chunks // 2, far_pair, 0)

    @pl.when(n_chunks % 2 == 1)
    def _():
        run(far_passes(pl.multiple_of((n_chunks - 1) * chunk, chunk)))

    def tail(left):
        row0 = pl.multiple_of((n_far - left) * t, t)
        passes = []
        for qt in range(Q_TILES):
            n_sel = left + 2 + qt
            passes.append(sel_pass(qt, sel_operands(row0, n_sel), 2))
            passes.append(win_pass(qt, pl.multiple_of((first + qt - 2) * t, t), 3))
        run(passes)

    for left in range(FAR_TILES):
        if (left + 1) % Q_TILES == 0:
            pl.when(jnp.logical_and(step >= 1, n_far % FAR_TILES == left))(functools.partial(tail, left))

    @pl.when(step == 0)
    def _():
        passes = []
        for qt in range(Q_TILES):
            n_keys = min(qt + 1, 3)
            passes.append(sel_pass(qt, sel_operands(0, qt + 1), min(qt + 1, 2)))
            passes.append(win_pass(qt, (qt + 1 - n_keys) * t, n_keys))
        run(passes)


def _sel_win(proj, neg_mask, blk_onehot, periods, gates, o_c, tile):
    s = proj.shape[0]
    ks_blk = (Q_WIDTH + 2 * KV_WIDTH) // HEAD_DIM
    vs_blk = (Q_WIDTH + 3 * KV_WIDTH) // HEAD_DIM
    kw_blk = (Q_WIDTH + 4 * KV_WIDTH) // HEAD_DIM
    vw_blk = (Q_WIDTH + 5 * KV_WIDTH) // HEAD_DIM
    once = pl.Buffered(1)
    rows = Q_TILES * tile
    stacked_rows = HPG * rows
    kern = functools.partial(_sel_win_kernel, tile=tile)
    return pl.pallas_call(
        kern,
        out_shape=jax.ShapeDtypeStruct((s, Q_WIDTH), BF16),
        grid=(N_KV_GROUPS, s // rows),
        in_specs=[
            pl.BlockSpec((rows, HPG * HEAD_DIM), lambda g, i: (i, g)),
            pl.BlockSpec((rows, LANES), lambda g, i: (i, g)),
            pl.BlockSpec((s, HEAD_DIM), lambda g, i: (0, ks_blk + g), pipeline_mode=once),
            pl.BlockSpec((s, HEAD_DIM), lambda g, i: (0, vs_blk + g), pipeline_mode=once),
            pl.BlockSpec((s, HEAD_DIM), lambda g, i: (0, kw_blk + g), pipeline_mode=once),
            pl.BlockSpec((s, HEAD_DIM), lambda g, i: (0, vw_blk + g), pipeline_mode=once),
            pl.BlockSpec((s, LANES), lambda g, i: (0, 0), pipeline_mode=once),
            pl.BlockSpec((HPG, 2, 2 * tile), lambda g, i: (g, 0, 0)),
            pl.BlockSpec((rows, LANES), lambda g, i: (i, g)),
            pl.BlockSpec((rows, HPG * HEAD_DIM), lambda g, i: (i, g)),
        ],
        out_specs=pl.BlockSpec((rows, HPG * HEAD_DIM), lambda g, i: (i, g)),
        scratch_shapes=[
            pltpu.VMEM((stacked_rows, 2 * HEAD_DIM), BF16),
            pltpu.VMEM((stacked_rows, LANES), F32),
            pltpu.VMEM((stacked_rows, 2 * HEAD_DIM), F32),
            pltpu.VMEM((HPG, tile, 3 * tile), F32),
        ],
        compiler_params=_params("arbitrary", "arbitrary"),
    )(proj, neg_mask, proj, proj, proj, proj, blk_onehot, periods, gates, o_c)


def _rel_bucket_np(dist):
    n = np.maximum(dist, 0)
    max_exact = REL_BUCKETS // 2
    large = max_exact + (np.log(np.maximum(n, 1).astype(np.float32) / np.float32(max_exact))
                         / np.float32(math.log(REL_MAX_DIST / max_exact))
                         * np.float32(REL_BUCKETS - max_exact)).astype(np.int32)
    large = np.minimum(large, REL_BUCKETS - 1)
    return np.where(n < max_exact, n, large).astype(np.int32)


def _far_distance():
    d = np.arange(4 * REL_MAX_DIST)
    b = _rel_bucket_np(d)
    assert b[-1] == REL_BUCKETS - 1
    return int(np.max(np.nonzero(b != REL_BUCKETS - 1)[0])) + 1


def _bias_table(rel_bias, dist):
    bucket = jnp.asarray(_rel_bucket_np(dist))
    shifted = (rel_bias[bucket] - rel_bias[REL_BUCKETS - 1]) * LOG2E
    tab = jnp.where(jnp.asarray(dist >= 0)[..., None], shifted, NEG)
    return jnp.moveaxis(tab, -1, 0).astype(F32)


def _bias_period(rel_bias, offset, t):
    x = np.arange(2 * t)
    return _bias_table(rel_bias, np.where(x < t, offset - x, offset + 2 * t - x))


def _nsa_mixer(h, hn, rel_bias, w_in, w_out, layer, pos_k, w1_k, w2_k, pos_v, w1_v, w2_v, next_norm_g):
    s = h.shape[0]
    n_sel = s // SEL_BLOCK
    ncp = s // CMP_STRIDE
    tile = ATT_TILE
    far = _far_distance()
    assert n_sel <= LANES and s % CMP_TQ == 0 and s % (Q_TILES * tile) == 0 and FAR_TILES % Q_TILES == 0
    assert far <= tile and far <= CMP_STRIDE * CMP_PAD - CMP_BLOCK + 1 + CMP_STRIDE

    col_scale = jnp.where(jnp.arange(PROJ_MAIN) < Q_WIDTH, HEAD_DIM ** -0.5 * LOG2E, 1.0).astype(F32)[None, :]
    w_in_t = jnp.swapaxes(w_in, 1, 2)
    proj = _matmul(_ep_scale, hn, w_in_t, layer, PROJ_MAIN, BF16, 2048, 512, row_extra=col_scale,
                   w_rows_are_outputs=True)
    w_gate = w_in_t[layer, PROJ_MAIN:].reshape(N_GATES, N_KV_GROUPS, HPG, D_MODEL).transpose(1, 0, 2, 3)
    w_gate = w_gate.reshape(N_KV_GROUPS, N_GATES * HPG, D_MODEL)
    w_gate = jnp.pad(w_gate, ((0, 0), (0, LANES - N_GATES * HPG), (0, 0))).reshape(1, N_KV_GROUPS * LANES, D_MODEL)
    gates = _matmul(jax.nn.sigmoid, hn, w_gate, 0, N_KV_GROUPS * LANES, F32, 2048, 512, w_rows_are_outputs=True)

    w1 = jnp.stack([w1_k, w1_v]).astype(BF16)
    w2 = jnp.stack([w2_k, w2_v]).astype(BF16)
    pos = jnp.stack([pos_k.reshape(1, -1), pos_v.reshape(1, -1)]).astype(BF16)
    pos = jnp.broadcast_to(pos, (2, SUBLANES, CMP_BLOCK * HEAD_DIM))
    cc = _compress(proj, w1, w2, pos)
    kcc = jnp.pad(cc[0], ((0, 0), (CMP_PAD, 0), (0, 0)))
    vcct = jnp.pad(cc[1], ((0, 0), (CMP_PAD, 0), (0, 0))).transpose(0, 2, 1)
    n_rows = ncp + CMP_PAD

    nb = CMP_TQ // CMP_STRIDE + CMP_PAD
    first_dist = -CMP_STRIDE * (nb - 1 - CMP_PAD) - (CMP_BLOCK - 1)
    by_dist = _bias_table(rel_bias, first_dist + np.arange(CMP_TQ + CMP_STRIDE * (nb - 1)))
    band = jnp.stack([by_dist[:, CMP_STRIDE * (nb - 1 - b):CMP_STRIDE * (nb - 1 - b) + CMP_TQ] for b in range(nb)], axis=1)
    ratio = SEL_BLOCK // CMP_STRIDE
    lo = CMP_BLOCK // CMP_STRIDE - 1
    c_of_row = np.arange(n_rows)[None, :] - CMP_PAD
    j_of = np.arange(LANES)[:, None]
    mt = ((c_of_row >= ratio * j_of - lo) & (c_of_row <= ratio * j_of + ratio - 1)
          & (c_of_row >= 0) & (c_of_row <= ncp - 2) & (j_of < n_sel)).astype(np.float32)
    o_c, neg_mask = _cmp_attn(proj, kcc, vcct, band, jnp.asarray(mt, BF16), CMP_TQ)

    onehot = (np.arange(s)[:, None] // SEL_BLOCK == np.arange(LANES)[None, :]).astype(np.float32)
    periods = jnp.stack([_bias_period(rel_bias, tile, tile), _bias_period(rel_bias, 0, tile)], axis=1)
    o = _sel_win(proj, neg_mask, jnp.asarray(onehot, BF16), periods, gates, o_c, tile)
    return _matmul_resid_norm(o, w_out, layer, h, next_norm_g, 256)


def _s5_kernel(x_ref, lam_ref, bt_ref, c_ref, d_ref, y_ref, vt, yt, tmask, *, chunk):
    L = chunk
    half = L // 2
    width = L * SSM_GROUP
    n_chunks = x_ref.shape[0] // L
    by_step = jnp.swapaxes(x_ref[...].reshape(n_chunks, L, LANES), 0, 1)
    for tau in range(L):
        vt[tau] = by_step[tau].T

    @pl.when(pl.program_id(0) == 0)
    def _():
        dst = lax.broadcasted_iota(jnp.int32, (width, width), 0) // SSM_GROUP
        src = lax.broadcasted_iota(jnp.int32, (width, width), 1) // SSM_GROUP
        tmask[...] = jnp.where(src <= dst, 1.0, 0.0)

    def cmul(xr, xi, yr, yi):
        return xr * yr - xi * yi, xr * yi + xi * yr

    n2 = 2 * SSM_STATE
    sub = lax.broadcasted_iota(jnp.int32, (SUBLANES, 1), 0)
    consts = jnp.where(sub == 0, 1.0, jnp.where(sub == 1, float(half + 1), jnp.where(sub == 2, float(half - 1), float(L))))
    tau_col = lax.broadcasted_iota(jnp.int32, (L, 1), 0).astype(F32)
    row = lax.broadcasted_iota(jnp.int32, (n_chunks, n2), 0)
    conj = jnp.where(lax.broadcasted_iota(jnp.int32, (1, n2), 1) < SSM_STATE, 1.0, -1.0)

    def group(g):
        lam = lam_ref[g]
        a_re, a_im = lam[0:1], lam[1:2]
        dt = jnp.exp(lam[2:3])
        log_re, log_im = a_re * dt, a_im * dt

        def cpow(e):
            mag, cos, sin = jnp.exp(e * log_re), jnp.cos(e * log_im), jnp.sin(e * log_im)
            inv = 1.0 / mag
            return mag * cos, mag * sin, inv * cos, -(inv * sin)

        kr, ki, _, _ = cpow(consts)
        pr, pi = kr[0:1] - 1.0, ki[0:1]
        den = a_re * a_re + a_im * a_im
        cf_r = (pr * a_re + pi * a_im) / den
        cf_i = (pi * a_re - pr * a_im) / den
        bt_c, bt_s = bt_ref[g, 0], bt_ref[g, 1]
        bb_c = cf_r * bt_c + cf_i * bt_s
        bb_s = cf_r * bt_s - cf_i * bt_c
        c_c, c_s = c_ref[g, 0], c_ref[g, 1]

        e1r, e1i, e2r, e2i = cpow(tau_col - half)
        e3r, e3i = cmul(e1r, e1i, kr[1:2], ki[1:2])
        e4r, e4i = cmul(e2r, e2i, kr[2:3], ki[2:3])

        def outer(er, ei, w_c, w_s):
            return (er[:, None, :] * w_c[None, :, :] + ei[:, None, :] * w_s[None, :, :]).reshape(width, n2)

        qm = outer(e1r, e1i, c_c, c_s)
        km_conj = outer(e2r, e2i, bb_c * conj, bb_s * conj)
        qc = outer(e3r, e3i, c_c, c_s)
        wz = outer(e4r, e4i, bb_c, bb_s)

        chans = pl.ds(pl.multiple_of(g * SSM_GROUP, SSM_GROUP), SSM_GROUP)
        ub = vt[:, chans, :].reshape(width, n_chunks).astype(BF16)
        yield

        tt = _dot_nt(qm.astype(BF16), km_conj.astype(BF16))
        x = _dot(wz.T.astype(BF16), ub).T
        yield
        y = _dot((tt * tmask[...]).astype(BF16), ub)
        p_r, p_i = kr[3:4], ki[3:4]
        d = 1
        while d < n_chunks:
            s = jnp.where(row >= d, pltpu.roll(x, d, 0), 0.0)
            x = x + s * p_r - pltpu.roll(s, SSM_STATE, 1) * (p_i * conj)
            p_r, p_i = cmul(p_r, p_i, p_r, p_i)
            d *= 2
        prev_conj = jnp.where(row >= 1, pltpu.roll(x, 1, 0), 0.0) * conj
        yield
        y = y + _dot_nt(qc.astype(BF16), prev_conj.astype(BF16))
        yt[:, chans, :] = y.reshape(L, SSM_GROUP, n_chunks)

    def group_batch(k, carry):
        running = [group(S5_GROUPS_PER_TRIP * k + j) for j in range(S5_GROUPS_PER_TRIP)]
        finished = object()
        while running:
            running = [gen for gen in running if next(gen, finished) is not finished]
        return carry

    lax.fori_loop(0, LANES // SSM_GROUP // S5_GROUPS_PER_TRIP, group_batch, 0)
    by_chunk = jnp.swapaxes(jnp.stack([yt[tau].T for tau in range(L)], axis=0), 0, 1)
    y = by_chunk.reshape(n_chunks * L, LANES) + d_ref[...] * x_ref[...]
    y_ref[...] = jax.nn.gelu(y).astype(y_ref.dtype)


def _s5_mixer(h, hn, a_re, a_im, log_dt, b_re, b_im, c_re, c_im, d_skip, w_glu, layer):
    s = h.shape[0]
    L = S5_CHUNK
    n_chunks = s // L
    width = L * SSM_GROUP
    gpb = LANES // SSM_GROUP
    def packed(re, im):
        return jnp.stack([jnp.concatenate([re, im], -1), jnp.concatenate([-im, re], -1)], axis=1).astype(F32)

    lam = jnp.stack([a_re, a_im, jnp.broadcast_to(log_dt[:, None], a_re.shape)], axis=1).astype(F32)
    lam = jnp.concatenate([lam, lam], axis=-1)
    bt = packed(b_re.transpose(0, 2, 1), b_im.transpose(0, 2, 1))
    cc = packed(c_re, c_im)
    dd = d_skip.astype(F32).reshape(1, D_MODEL)
    y = pl.pallas_call(
        functools.partial(_s5_kernel, chunk=L),
        out_shape=jax.ShapeDtypeStruct((s, D_MODEL), BF16),
        grid=(SSM_GROUPS // gpb,),
        in_specs=[
            pl.BlockSpec((s, LANES), lambda b: (0, b)),
            pl.BlockSpec((gpb, 3, 2 * SSM_STATE), lambda b: (b, 0, 0)),
            pl.BlockSpec((gpb, 2, SSM_GROUP, 2 * SSM_STATE), lambda b: (b, 0, 0, 0)),
            pl.BlockSpec((gpb, 2, SSM_GROUP, 2 * SSM_STATE), lambda b: (b, 0, 0, 0)),
            pl.BlockSpec((1, LANES), lambda b: (0, b)),
        ],
        out_specs=pl.BlockSpec((s, LANES), lambda b: (0, b)),
        scratch_shapes=[
            pltpu.VMEM((L, LANES, n_chunks), F32),
            pltpu.VMEM((L, LANES, n_chunks), F32),
            pltpu.VMEM((width, width), F32),
        ],
        compiler_params=_params("arbitrary"),
    )(hn, lam, bt, cc, dd)
    return _matmul(_ep_glu_resid, y, w_glu, layer, D_MODEL, F32, 1024, 512, b_halves=2, tile_extra=h)


def kernel(x, rel_bias, mix_norm_g, ffn_norm_g, final_norm_g, nsa_w_in, nsa_w_out, cmp_pos_k, cmp_w1_k, cmp_w2_k, cmp_pos_v, cmp_w1_v, cmp_w2_v, s5_A_re, s5_A_im, s5_log_dt, s5_B_re, s5_B_im, s5_C_re, s5_C_im, s5_D, s5_w_glu, ffn_w_in, ffn_w_out):
    assert x.shape[0] == 1
    h = x[0]
    hn = _rmsnorm(h, mix_norm_g[0], BF16)
    h, hn = _nsa_mixer(h, hn, rel_bias, nsa_w_in, nsa_w_out, 0, cmp_pos_k[0], cmp_w1_k[0], cmp_w2_k[0],
                       cmp_pos_v[0], cmp_w1_v[0], cmp_w2_v[0], ffn_norm_g[0])
    h = _swiglu_ffn(h, hn, ffn_w_in, ffn_w_out, 0)
    hn = _rmsnorm(h, mix_norm_g[1], F32)
    h = _s5_mixer(h, hn, s5_A_re[0], s5_A_im[0], s5_log_dt[0], s5_B_re[0], s5_B_im[0], s5_C_re[0], s5_C_im[0],
                  s5_D[0], s5_w_glu, 0)
    h = _swiglu_ffn(h, _rmsnorm(h, ffn_norm_g[1], BF16), ffn_w_in, ffn_w_out, 1)
    return _rmsnorm(h, final_norm_g, x.dtype)[None]
```

```python
import functools
import math

import numpy as np
import jax
import jax.numpy as jnp
from jax import lax
from jax.experimental import pallas as pl
from jax.experimental.pallas import tpu as pltpu

D_MODEL = 2048
N_HEADS = 16
HEAD_DIM = 128
N_KV_GROUPS = 4
HPG = N_HEADS // N_KV_GROUPS
CMP_BLOCK = 32
CMP_STRIDE = 16
CMP_HIDDEN = 2 * HEAD_DIM
SEL_BLOCK = 64
SEL_TOPK = 16
N_FORCED = 3
WINDOW = 512
N_GATES = 3
KV_WIDTH = N_KV_GROUPS * HEAD_DIM
Q_WIDTH = N_HEADS * HEAD_DIM
PROJ_MAIN = Q_WIDTH + 6 * KV_WIDTH
REL_BUCKETS = 32
REL_MAX_DIST = 128
SSM_GROUP = 16
SSM_GROUPS = D_MODEL // SSM_GROUP
SSM_STATE = 64
D_FF = ((8 * D_MODEL + 2) // 3 + 255) // 256 * 256
RMS_EPS = 1e-6
NEG = -1e30
LOG2E = math.log2(math.e)

LANES = 128
SUBLANES = 8
VMEM_LIMIT = 52 * 1024 * 1024

ATT_TILE = WINDOW // 2
FAR_TILES = 4
Q_TILES = 2
CMP_TQ = 256
CMP_PAD = 8
S5_CHUNK = 32
S5_GROUPS_PER_TRIP = 4

BF16 = jnp.bfloat16
F32 = jnp.float32


def _dot(a, b):
    return jnp.dot(a, b, preferred_element_type=F32)


def _dot_nt(a, b):
    return lax.dot_general(a, b, (((1,), (1,)), ((), ())), preferred_element_type=F32)


def _params(*sem):
    return pltpu.CompilerParams(dimension_semantics=sem, vmem_limit_bytes=VMEM_LIMIT)


def _rmsnorm_kernel(x_ref, g_ref, o_ref):
    x = x_ref[...]
    ms = jnp.mean(x * x, axis=-1, keepdims=True)
    o_ref[...] = (x * lax.rsqrt(ms + RMS_EPS) * g_ref[...]).astype(o_ref.dtype)


def _rmsnorm(x, g, out_dtype, tm=512):
    s, d = x.shape
    return pl.pallas_call(
        _rmsnorm_kernel,
        out_shape=jax.ShapeDtypeStruct((s, d), out_dtype),
        grid=(s // tm,),
        in_specs=[pl.BlockSpec((tm, d), lambda i: (i, 0)), pl.BlockSpec((1, d), lambda i: (0, 0))],
        out_specs=pl.BlockSpec((tm, d), lambda i: (i, 0)),
        compiler_params=_params("parallel"),
    )(x, g.reshape(1, d))


def _mm_kernel(*refs, n_b, n_extra, epilogue, w_rows_are_outputs):
    a_ref = refs[0]
    b_refs = refs[1:1 + n_b]
    extra_refs = refs[1 + n_b:1 + n_b + n_extra]
    o_ref = refs[1 + n_b + n_extra]
    w_scr = refs[2 + n_b + n_extra:]

    @pl.when(pl.program_id(1) == 0)
    def _():
        for b_ref, w in zip(b_refs, w_scr):
            w[...] = b_ref[...].astype(BF16)

    a = a_ref[...]
    z = [(_dot_nt if w_rows_are_outputs else _dot)(a, w[...]) for w in w_scr]
    o_ref[...] = epilogue(*z, *[e[...] for e in extra_refs]).astype(o_ref.dtype)


def _ep_scale(z, scale):
    return z * scale


def _ep_resid(z, resid):
    return resid + z


def _ep_swiglu(za, zb):
    return jax.nn.silu(za) * zb


def _ep_glu_resid(za, zb, resid):
    return resid + za * jax.nn.sigmoid(zb)


def _matmul(epilogue, a, b, layer, n_out, out_dtype, tm, tn, *, b_halves=1, row_extra=None, tile_extra=None,
            w_rows_are_outputs=False):
    m, k = a.shape
    nj = n_out // tn
    estimate = (b_halves * k * tn * (2 * 4 + 2) + 2 * tm * k * 2 + 4 * tm * tn * 4 + b_halves * tm * tn * 4)
    b_mode = pl.Buffered(1) if estimate > VMEM_LIMIT else None
    in_specs = [pl.BlockSpec((tm, k), lambda j, i: (i, 0))]
    args = [a]
    for half in range(b_halves):
        if w_rows_are_outputs:
            spec = pl.BlockSpec((None, tn, k), functools.partial(lambda j, i, o: (layer, j + o, 0), o=half * nj),
                                pipeline_mode=b_mode)
        else:
            spec = pl.BlockSpec((None, k, tn), functools.partial(lambda j, i, o: (layer, 0, j + o), o=half * nj),
                                pipeline_mode=b_mode)
        in_specs.append(spec)
        args.append(b)
    extras = []
    if row_extra is not None:
        in_specs.append(pl.BlockSpec((1, tn), lambda j, i: (0, j)))
        extras.append(row_extra)
    if tile_extra is not None:
        in_specs.append(pl.BlockSpec((tm, tn), lambda j, i: (i, j)))
        extras.append(tile_extra)
    kern = functools.partial(_mm_kernel, n_b=b_halves, n_extra=len(extras), epilogue=epilogue,
                             w_rows_are_outputs=w_rows_are_outputs)
    return pl.pallas_call(
        kern,
        out_shape=jax.ShapeDtypeStruct((m, n_out), out_dtype),
        grid=(nj, m // tm),
        in_specs=in_specs,
        out_specs=pl.BlockSpec((tm, tn), lambda j, i: (i, j)),
        scratch_shapes=[pltpu.VMEM((tn, k) if w_rows_are_outputs else (k, tn), BF16) for _ in range(b_halves)],
        compiler_params=_params("arbitrary", "arbitrary"),
    )(*args, *extras)


def _mm_resid_norm_kernel(a_ref, b_ref, r_ref, g_ref, h_ref, hn_ref, w_scr):
    @pl.when(pl.program_id(0) == 0)
    def _():
        w_scr[...] = b_ref[...].astype(BF16)

    h = r_ref[...] + _dot(a_ref[...], w_scr[...])
    h_ref[...] = h
    ms = jnp.mean(h * h, axis=-1, keepdims=True)
    hn_ref[...] = (h * lax.rsqrt(ms + RMS_EPS) * g_ref[...]).astype(hn_ref.dtype)


def _matmul_resid_norm(a, b, layer, resid, g, tm):
    m, k = a.shape
    n = b.shape[2]
    return pl.pallas_call(
        _mm_resid_norm_kernel,
        out_shape=(jax.ShapeDtypeStruct((m, n), F32), jax.ShapeDtypeStruct((m, n), BF16)),
        grid=(m // tm,),
        in_specs=[
            pl.BlockSpec((tm, k), lambda i: (i, 0)),
            pl.BlockSpec((None, k, n), lambda i: (layer, 0, 0), pipeline_mode=pl.Buffered(1)),
            pl.BlockSpec((tm, n), lambda i: (i, 0)),
            pl.BlockSpec((1, n), lambda i: (0, 0)),
        ],
        out_specs=(pl.BlockSpec((tm, n), lambda i: (i, 0)), pl.BlockSpec((tm, n), lambda i: (i, 0))),
        scratch_shapes=[pltpu.VMEM((k, n), BF16)],
        compiler_params=_params("arbitrary"),
    )(a, b, resid, g.reshape(1, n))


def _swiglu_ffn(h, hn, w_in, w_out, layer):
    act = _matmul(_ep_swiglu, hn, w_in, layer, D_FF, BF16, 1024, 512, b_halves=2)
    return _matmul(_ep_resid, act, w_out, layer, D_MODEL, F32, 512, 512, tile_extra=h)


def _compress_kernel(x_ref, w1_ref, w2_ref, pos_ref, o_ref):
    ncp = x_ref.shape[0] // CMP_STRIDE
    top = jnp.zeros((ncp, CMP_HIDDEN), F32)
    bot = jnp.zeros((ncp, CMP_HIDDEN), F32)
    by_offset = jnp.swapaxes(x_ref[...].reshape(ncp, CMP_STRIDE, HEAD_DIM), 0, 1)
    for r in range(CMP_STRIDE):
        x_r = by_offset[r]
        top = top + _dot(x_r, w1_ref[0, r * HEAD_DIM:(r + 1) * HEAD_DIM, :])
        bot = bot + _dot(x_r, w1_ref[0, (CMP_STRIDE + r) * HEAD_DIM:(CMP_STRIDE + r + 1) * HEAD_DIM, :])
    posb = _dot(pos_ref[0], w1_ref[0])[0:1]
    hid = top + pltpu.roll(bot, ncp - 1, 0) + posb
    o_ref[0, 0] = _dot(jax.nn.gelu(hid).astype(BF16), w2_ref[0]).astype(o_ref.dtype)


def _compress(proj, w1, w2, pos):
    s = proj.shape[0]
    ncp = s // CMP_STRIDE
    kc_blk = Q_WIDTH // HEAD_DIM
    return pl.pallas_call(
        _compress_kernel,
        out_shape=jax.ShapeDtypeStruct((2, N_KV_GROUPS, ncp, HEAD_DIM), BF16),
        grid=(2, N_KV_GROUPS),
        in_specs=[
            pl.BlockSpec((s, HEAD_DIM), lambda kv, gi: (0, kc_blk + kv * N_KV_GROUPS + gi)),
            pl.BlockSpec((1, CMP_BLOCK * HEAD_DIM, CMP_HIDDEN), lambda kv, gi: (kv, 0, 0)),
            pl.BlockSpec((1, CMP_HIDDEN, HEAD_DIM), lambda kv, gi: (kv, 0, 0)),
            pl.BlockSpec((1, SUBLANES, CMP_BLOCK * HEAD_DIM), lambda kv, gi: (kv, 0, 0)),
        ],
        out_specs=pl.BlockSpec((1, 1, ncp, HEAD_DIM), lambda kv, gi: (kv, gi, 0, 0)),
        compiler_params=_params("parallel", "parallel"),
    )(proj, w1, w2, pos)


def _cmp_attn_kernel(q_ref, kcc_ref, vcct_ref, band_ref, mt_ref, oc_ref, nm_ref, s_scr, *, tq, nb):
    i = pl.program_id(1)
    band_start = pl.multiple_of(i * (tq // CMP_STRIDE), SUBLANES)
    n_rows = kcc_ref.shape[1]
    n_blk = mt_ref.shape[0]

    def body(n_r, n_b):
        kcc = kcc_ref[0, :n_r, :]
        vcct = vcct_ref[0, :, :n_r]
        rows = lax.broadcasted_iota(jnp.int32, (n_r, tq), 0)
        row_bias = jnp.where(rows >= CMP_PAD, jnp.where(rows < band_start + nb, 0.0, NEG), NEG)
        imp = jnp.zeros((n_r, tq), F32)

        def scores(h):
            qh = q_ref[:, h * HEAD_DIM:(h + 1) * HEAD_DIM]
            s_scr[h, :n_r, :] = _dot_nt(kcc, qh) + row_bias
            s_scr[h, pl.ds(band_start, nb), :] += band_ref[h]

        scores(0)
        for h in range(HPG):
            if h + 1 < HPG:
                scores(h + 1)
            s = s_scr[h, :n_r, :]
            m = jnp.maximum(jnp.max(s, axis=0, keepdims=True), 0.5 * NEG)
            e = jnp.exp2(s - m)
            l = jnp.sum(e, axis=0, keepdims=True)
            pn = e * (1.0 / jnp.where(l > 0.0, l, 1.0))
            imp = imp + pn
            oct_h = _dot(vcct, pn.astype(BF16))
            oc_ref[:, h * HEAD_DIM:(h + 1) * HEAD_DIM] = oct_h.T
        mt = mt_ref[:n_b, :n_r]
        hi = imp.astype(BF16)
        rest = imp - hi.astype(F32)
        mid = rest.astype(BF16)
        lo = (rest - mid.astype(F32)).astype(BF16)
        slc = _dot(mt, hi) + _dot(mt, mid) + _dot(mt, lo)
        jj = lax.broadcasted_iota(jnp.int32, (n_b, tq), 0)
        t_blk = (i * tq + lax.broadcasted_iota(jnp.int32, (n_b, tq), 1)) // SEL_BLOCK
        forced = jnp.where(jj == 0, 1, jnp.where(jj == t_blk, 1, jnp.where(jj == t_blk - 1, 1, 0)))
        picked = jnp.where(forced == 1, 1.0, 0.0)
        score = jnp.where(forced == 1, -jnp.inf, jnp.where(jj <= t_blk, slc, NEG))
        for _ in range(SEL_TOPK - N_FORCED):
            best = jnp.max(score, axis=0, keepdims=True)
            first = jnp.min(jnp.where(score == best, jj, n_blk), axis=0, keepdims=True)
            hit = jj == first
            picked = jnp.where(hit, 1.0, picked)
            score = jnp.where(hit, -jnp.inf, score)
        neg_mask = jnp.where(picked > 0.5, jnp.where(jj <= t_blk, 0.0, NEG), NEG)
        if n_b < n_blk:
            neg_mask = jnp.concatenate([neg_mask, jnp.full((n_blk - n_b, tq), NEG, F32)], axis=0)
        nm_ref[...] = neg_mask.T.astype(nm_ref.dtype)

    step = LANES
    extents = list(range(step, n_rows, step)) + [n_rows]
    need_rows = band_start + nb
    for v, n_r in enumerate(extents):
        lower = extents[v - 1] if v else 0
        n_b = min(n_blk, -(-(n_r * CMP_STRIDE // SEL_BLOCK) // SUBLANES) * SUBLANES)
        pl.when(jnp.logical_and(need_rows > lower, need_rows <= n_r))(functools.partial(body, n_r, n_b))


def _cmp_attn(proj, kcc, vcct, band, mt, tq):
    s = proj.shape[0]
    n_rows = kcc.shape[1]
    nb = band.shape[1]
    kern = functools.partial(_cmp_attn_kernel, tq=tq, nb=nb)
    return pl.pallas_call(
        kern,
        out_shape=(jax.ShapeDtypeStruct((s, Q_WIDTH), F32),
                   jax.ShapeDtypeStruct((s, N_KV_GROUPS * LANES), BF16)),
        grid=(N_KV_GROUPS, s // tq),
        in_specs=[
            pl.BlockSpec((tq, HPG * HEAD_DIM), lambda g, i: (i, g)),
            pl.BlockSpec((1, n_rows, HEAD_DIM), lambda g, i: (g, 0, 0)),
            pl.BlockSpec((1, HEAD_DIM, n_rows), lambda g, i: (g, 0, 0)),
            pl.BlockSpec((HPG, nb, tq), lambda g, i: (g, 0, 0)),
            pl.BlockSpec((LANES, n_rows), lambda g, i: (0, 0)),
        ],
        out_specs=(pl.BlockSpec((tq, HPG * HEAD_DIM), lambda g, i: (i, g)),
                   pl.BlockSpec((tq, LANES), lambda g, i: (i, g))),
        scratch_shapes=[pltpu.VMEM((HPG, n_rows, tq), F32)],
        compiler_params=_params("parallel", "parallel"),
    )(proj, kcc, vcct, band, mt)


def _lane_tile(x, width):
    return jnp.concatenate([x] * (width // x.shape[1]), axis=1)


def _softmax_update(s, v_aug, m_ref, acc_ref, rows):
    m_prev = m_ref[rows, :]
    m_next = jnp.maximum(m_prev, jnp.max(s, axis=1, keepdims=True))
    alpha = jnp.exp2(m_prev - m_next)
    p = jnp.exp2(s - _lane_tile(m_next, s.shape[1]))
    acc_ref[rows, :] = _lane_tile(alpha, acc_ref.shape[1]) * acc_ref[rows, :] + _dot(p.astype(BF16), v_aug)
    m_ref[rows, :] = m_next


def _softmax_once(s, v_aug):
    m = jnp.broadcast_to(jnp.max(s, axis=1, keepdims=True), (s.shape[0], LANES))
    p = jnp.exp2(s - _lane_tile(m, s.shape[1]))
    return _dot(p.astype(BF16), v_aug)


def _sel_win_kernel(q_ref, nm_ref, ks_ref, vs_ref, kw_ref, vw_ref, blk_ref, per_ref, g_ref, oc_ref, o_ref,
                    q4, m_s, a_s, bias_ref, *, tile):
    step = pl.program_id(1)
    t = tile
    first = step * Q_TILES

    @pl.when(step == 0)
    def _():
        tl = lax.broadcasted_iota(jnp.int32, (t, t), 0)
        kl = lax.broadcasted_iota(jnp.int32, (t, t), 1)
        edge = jnp.where(tl < kl, 0.0, NEG)
        for h in range(HPG):
            bias_ref[h, :, :t] = edge
            for which in range(2):
                rows = jnp.broadcast_to(per_ref[h, which:which + 1, :], (t, 2 * t))
                table = pltpu.roll(rows, 0, 1, stride=1, stride_axis=0)
                bias_ref[h, :, (which + 1) * t:(which + 2) * t] = table[:, :t]

    def stacked(qt, h):
        return slice((qt * HPG + h) * t, (qt * HPG + h + 1) * t)

    def tile_rows(qt):
        return slice(qt * t, (qt + 1) * t)

    for qt in range(Q_TILES):
        for h in range(HPG):
            q4[stacked(qt, h), :HEAD_DIM] = q_ref[tile_rows(qt), h * HEAD_DIM:(h + 1) * HEAD_DIM]
            q4[stacked(qt, h), HEAD_DIM:] = nm_ref[tile_rows(qt), :]
    m_s[...] = jnp.full(m_s.shape, -3e38, F32)
    a_s[...] = jnp.zeros(a_s.shape, F32)

    def sel_operands(row0, n_tiles):
        keys = pl.ds(row0, n_tiles * t)
        k_aug = jnp.concatenate([ks_ref[keys, :], blk_ref[keys, :]], axis=1)
        v_aug = jnp.concatenate([vs_ref[keys, :], jnp.ones((n_tiles * t, LANES), BF16)], axis=1)
        return k_aug, v_aug

    def sel_pass(qt, operands, n_biased):
        k_aug, v_aug = operands

        def score(h):
            s = _dot_nt(q4[stacked(qt, h), :], k_aug)
            plain = s.shape[1] - n_biased * t
            if n_biased:
                near = s[:, plain:] + bias_ref[h, :, (3 - n_biased) * t:]
                s = jnp.concatenate([s[:, :plain], near], axis=1) if plain else near
            return s

        def consume(h, s):
            _softmax_update(s, v_aug, m_s, a_s, stacked(qt, h))

        return score, consume

    def win_pass(qt, row0, n_tiles):
        keys = pl.ds(row0, n_tiles * t)
        v_aug = jnp.concatenate([vw_ref[keys, :], jnp.ones((n_tiles * t, LANES), BF16)], axis=1)

        def score(h):
            return _dot_nt(q4[stacked(qt, h), :HEAD_DIM], kw_ref[keys, :]) + bias_ref[h, :, (3 - n_tiles) * t:]

        def consume(h, s):
            acc_w = _softmax_once(s, v_aug)
            acc_s = a_s[stacked(qt, h), :]
            o_w = acc_w[:, :HEAD_DIM] * (1.0 / acc_w[:, HEAD_DIM:])
            o_s = acc_s[:, :HEAD_DIM] * (1.0 / acc_s[:, HEAD_DIM:])
            cols = slice(h * HEAD_DIM, (h + 1) * HEAD_DIM)
            gates = g_ref[tile_rows(qt), :]
            o = (gates[:, h:h + 1] * oc_ref[tile_rows(qt), cols] + gates[:, HPG + h:HPG + h + 1] * o_s
                 + gates[:, 2 * HPG + h:2 * HPG + h + 1] * o_w)
            o_ref[tile_rows(qt), cols] = o.astype(o_ref.dtype)

        return score, consume

    def run(passes):
        work = [(score, consume, h0) for score, consume in passes for h0 in range(0, HPG, 2)]
        ahead = [work[0][0](h) for h in (0, 1)]
        for n, (_, consume, h0) in enumerate(work):
            now = ahead
            if n + 1 < len(work):
                nxt_score, _, nxt_h0 = work[n + 1]
                ahead = [nxt_score(h) for h in (nxt_h0, nxt_h0 + 1)]
            for h, s in zip((h0, h0 + 1), now):
                consume(h, s)

    def far_passes(row0):
        operands = sel_operands(row0, FAR_TILES)
        return [sel_pass(qt, operands, 0) for qt in range(Q_TILES)]

    n_far = jnp.maximum(first - 1, 0)
    n_chunks = n_far // FAR_TILES
    chunk = FAR_TILES * t

    def far_pair(c, carry):
        row0 = pl.multiple_of(c * 2 * chunk, 2 * chunk)
        run(far_passes(row0) + far_passes(row0 + chunk))
        return carry

    lax.fori_loop(0, n_chunks // 2, far_pair, 0)

    @pl.when(n_chunks % 2 == 1)
    def _():
        run(far_passes(pl.multiple_of((n_chunks - 1) * chunk, chunk)))

    def tail(left):
        row0 = pl.multiple_of((n_far - left) * t, t)
        passes = []
        for qt in range(Q_TILES):
            n_sel = left + 2 + qt
            passes.append(sel_pass(qt, sel_operands(row0, n_sel), 2))
            passes.append(win_pass(qt, pl.multiple_of((first + qt - 2) * t, t), 3))
        run(passes)

    for left in range(FAR_TILES):
        if (left + 1) % Q_TILES == 0:
            pl.when(jnp.logical_and(step >= 1, n_far % FAR_TILES == left))(functools.partial(tail, left))

    @pl.when(step == 0)
    def _():
        passes = []
        for qt in range(Q_TILES):
            n_keys = min(qt + 1, 3)
            passes.append(sel_pass(qt, sel_operands(0, qt + 1), min(qt + 1, 2)))
            passes.append(win_pass(qt, (qt + 1 - n_keys) * t, n_keys))
        run(passes)


def _sel_win(proj, neg_mask, blk_onehot, periods, gates, o_c, tile):
    s = proj.shape[0]
    ks_blk = (Q_WIDTH + 2 * KV_WIDTH) // HEAD_DIM
    vs_blk = (Q_WIDTH + 3 * KV_WIDTH) // HEAD_DIM
    kw_blk = (Q_WIDTH + 4 * KV_WIDTH) // HEAD_DIM
    vw_blk = (Q_WIDTH + 5 * KV_WIDTH) // HEAD_DIM
    once = pl.Buffered(1)
    rows = Q_TILES * tile
    stacked_rows = HPG * rows
    kern = functools.partial(_sel_win_kernel, tile=tile)
    return pl.pallas_call(
        kern,
        out_shape=jax.ShapeDtypeStruct((s, Q_WIDTH), BF16),
        grid=(N_KV_GROUPS, s // rows),
        in_specs=[
            pl.BlockSpec((rows, HPG * HEAD_DIM), lambda g, i: (i, g)),
            pl.BlockSpec((rows, LANES), lambda g, i: (i, g)),
            pl.BlockSpec((s, HEAD_DIM), lambda g, i: (0, ks_blk + g), pipeline_mode=once),
            pl.BlockSpec((s, HEAD_DIM), lambda g, i: (0, vs_blk + g), pipeline_mode=once),
            pl.BlockSpec((s, HEAD_DIM), lambda g, i: (0, kw_blk + g), pipeline_mode=once),
            pl.BlockSpec((s, HEAD_DIM), lambda g, i: (0, vw_blk + g), pipeline_mode=once),
            pl.BlockSpec((s, LANES), lambda g, i: (0, 0), pipeline_mode=once),
            pl.BlockSpec((HPG, 2, 2 * tile), lambda g, i: (g, 0, 0)),
            pl.BlockSpec((rows, LANES), lambda g, i: (i, g)),
            pl.BlockSpec((rows, HPG * HEAD_DIM), lambda g, i: (i, g)),
        ],
        out_specs=pl.BlockSpec((rows, HPG * HEAD_DIM), lambda g, i: (i, g)),
        scratch_shapes=[
            pltpu.VMEM((stacked_rows, 2 * HEAD_DIM), BF16),
            pltpu.VMEM((stacked_rows, LANES), F32),
            pltpu.VMEM((stacked_rows, 2 * HEAD_DIM), F32),
            pltpu.VMEM((HPG, tile, 3 * tile), F32),
        ],
        compiler_params=_params("arbitrary", "arbitrary"),
    )(proj, neg_mask, proj, proj, proj, proj, blk_onehot, periods, gates, o_c)


def _rel_bucket_np(dist):
    n = np.maximum(dist, 0)
    max_exact = REL_BUCKETS // 2
    large = max_exact + (np.log(np.maximum(n, 1).astype(np.float32) / np.float32(max_exact))
                         / np.float32(math.log(REL_MAX_DIST / max_exact))
                         * np.float32(REL_BUCKETS - max_exact)).astype(np.int32)
    large = np.minimum(large, REL_BUCKETS - 1)
    return np.where(n < max_exact, n, large).astype(np.int32)


def _far_distance():
    d = np.arange(4 * REL_MAX_DIST)
    b = _rel_bucket_np(d)
    assert b[-1] == REL_BUCKETS - 1
    return int(np.max(np.nonzero(b != REL_BUCKETS - 1)[0])) + 1


def _bias_table(rel_bias, dist):
    bucket = jnp.asarray(_rel_bucket_np(dist))
    shifted = (rel_bias[bucket] - rel_bias[REL_BUCKETS - 1]) * LOG2E
    tab = jnp.where(jnp.asarray(dist >= 0)[..., None], shifted, NEG)
    return jnp.moveaxis(tab, -1, 0).astype(F32)


def _bias_period(rel_bias, offset, t):
    x = np.arange(2 * t)
    return _bias_table(rel_bias, np.where(x < t, offset - x, offset + 2 * t - x))


def _nsa_mixer(h, hn, rel_bias, w_in, w_out, layer, pos_k, w1_k, w2_k, pos_v, w1_v, w2_v, next_norm_g):
    s = h.shape[0]
    n_sel = s // SEL_BLOCK
    ncp = s // CMP_STRIDE
    tile = ATT_TILE
    far = _far_distance()
    assert n_sel <= LANES and s % CMP_TQ == 0 and s % (Q_TILES * tile) == 0 and FAR_TILES % Q_TILES == 0
    assert far <= tile and far <= CMP_STRIDE * CMP_PAD - CMP_BLOCK + 1 + CMP_STRIDE

    col_scale = jnp.where(jnp.arange(PROJ_MAIN) < Q_WIDTH, HEAD_DIM ** -0.5 * LOG2E, 1.0).astype(F32)[None, :]
    w_in_t = jnp.swapaxes(w_in, 1, 2)
    proj = _matmul(_ep_scale, hn, w_in_t, layer, PROJ_MAIN, BF16, 2048, 512, row_extra=col_scale,
                   w_rows_are_outputs=True)
    w_gate = w_in_t[layer, PROJ_MAIN:].reshape(N_GATES, N_KV_GROUPS, HPG, D_MODEL).transpose(1, 0, 2, 3)
    w_gate = w_gate.reshape(N_KV_GROUPS, N_GATES * HPG, D_MODEL)
    w_gate = jnp.pad(w_gate, ((0, 0), (0, LANES - N_GATES * HPG), (0, 0))).reshape(1, N_KV_GROUPS * LANES, D_MODEL)
    gates = _matmul(jax.nn.sigmoid, hn, w_gate, 0, N_KV_GROUPS * LANES, F32, 2048, 512, w_rows_are_outputs=True)

    w1 = jnp.stack([w1_k, w1_v]).astype(BF16)
    w2 = jnp.stack([w2_k, w2_v]).astype(BF16)
    pos = jnp.stack([pos_k.reshape(1, -1), pos_v.reshape(1, -1)]).astype(BF16)
    pos = jnp.broadcast_to(pos, (2, SUBLANES, CMP_BLOCK * HEAD_DIM))
    cc = _compress(proj, w1, w2, pos)
    kcc = jnp.pad(cc[0], ((0, 0), (CMP_PAD, 0), (0, 0)))
    vcct = jnp.pad(cc[1], ((0, 0), (CMP_PAD, 0), (0, 0))).transpose(0, 2, 1)
    n_rows = ncp + CMP_PAD

    nb = CMP_TQ // CMP_STRIDE + CMP_PAD
    first_dist = -CMP_STRIDE * (nb - 1 - CMP_PAD) - (CMP_BLOCK - 1)
    by_dist = _bias_table(rel_bias, first_dist + np.arange(CMP_TQ + CMP_STRIDE * (nb - 1)))
    band = jnp.stack([by_dist[:, CMP_STRIDE * (nb - 1 - b):CMP_STRIDE * (nb - 1 - b) + CMP_TQ] for b in range(nb)], axis=1)
    ratio = SEL_BLOCK // CMP_STRIDE
    lo = CMP_BLOCK // CMP_STRIDE - 1
    c_of_row = np.arange(n_rows)[None, :] - CMP_PAD
    j_of = np.arange(LANES)[:, None]
    mt = ((c_of_row >= ratio * j_of - lo) & (c_of_row <= ratio * j_of + ratio - 1)
          & (c_of_row >= 0) & (c_of_row <= ncp - 2) & (j_of < n_sel)).astype(np.float32)
    o_c, neg_mask = _cmp_attn(proj, kcc, vcct, band, jnp.asarray(mt, BF16), CMP_TQ)

    onehot = (np.arange(s)[:, None] // SEL_BLOCK == np.arange(LANES)[None, :]).astype(np.float32)
    periods = jnp.stack([_bias_period(rel_bias, tile, tile), _bias_period(rel_bias, 0, tile)], axis=1)
    o = _sel_win(proj, neg_mask, jnp.asarray(onehot, BF16), periods, gates, o_c, tile)
    return _matmul_resid_norm(o, w_out, layer, h, next_norm_g, 256)


def _s5_kernel(x_ref, lam_ref, bt_ref, c_ref, d_ref, y_ref, vt, yt, tmask, *, chunk):
    L = chunk
    half = L // 2
    width = L * SSM_GROUP
    n_chunks = x_ref.shape[0] // L
    by_step = jnp.swapaxes(x_ref[...].reshape(n_chunks, L, LANES), 0, 1)
    for tau in range(L):
        vt[tau] = by_step[tau].T

    @pl.when(pl.program_id(0) == 0)
    def _():
        dst = lax.broadcasted_iota(jnp.int32, (width, width), 0) // SSM_GROUP
        src = lax.broadcasted_iota(jnp.int32, (width, width), 1) // SSM_GROUP
        tmask[...] = jnp.where(src <= dst, 1.0, 0.0)

    def cmul(xr, xi, yr, yi):
        return xr * yr - xi * yi, xr * yi + xi * yr

    n2 = 2 * SSM_STATE
    sub = lax.broadcasted_iota(jnp.int32, (SUBLANES, 1), 0)
    consts = jnp.where(sub == 0, 1.0, jnp.where(sub == 1, float(half + 1), jnp.where(sub == 2, float(half - 1), float(L))))
    tau_col = lax.broadcasted_iota(jnp.int32, (L, 1), 0).astype(F32)
    row = lax.broadcasted_iota(jnp.int32, (n_chunks, n2), 0)
    conj = jnp.where(lax.broadcasted_iota(jnp.int32, (1, n2), 1) < SSM_STATE, 1.0, -1.0)

    def group(g):
        lam = lam_ref[g]
        a_re, a_im = lam[0:1], lam[1:2]
        dt = jnp.exp(lam[2:3])
        log_re, log_im = a_re * dt, a_im * dt

        def cpow(e):
            mag, cos, sin = jnp.exp(e * log_re), jnp.cos(e * log_im), jnp.sin(e * log_im)
            inv = 1.0 / mag
            return mag * cos, mag * sin, inv * cos, -(inv * sin)

        kr, ki, _, _ = cpow(consts)
        pr, pi = kr[0:1] - 1.0, ki[0:1]
        den = a_re * a_re + a_im * a_im
        cf_r = (pr * a_re + pi * a_im) / den
        cf_i = (pi * a_re - pr * a_im) / den
        bt_c, bt_s = bt_ref[g, 0], bt_ref[g, 1]
        bb_c = cf_r * bt_c + cf_i * bt_s
        bb_s = cf_r * bt_s - cf_i * bt_c
        c_c, c_s = c_ref[g, 0], c_ref[g, 1]

        e1r, e1i, e2r, e2i = cpow(tau_col - half)
        e3r, e3i = cmul(e1r, e1i, kr[1:2], ki[1:2])
        e4r, e4i = cmul(e2r, e2i, kr[2:3], ki[2:3])

        def outer(er, ei, w_c, w_s):
            return (er[:, None, :] * w_c[None, :, :] + ei[:, None, :] * w_s[None, :, :]).reshape(width, n2)

        qm = outer(e1r, e1i, c_c, c_s)
        km_conj = outer(e2r, e2i, bb_c * conj, bb_s * conj)
        qc = outer(e3r, e3i, c_c, c_s)
        wz = outer(e4r, e4i, bb_c, bb_s)

        chans = pl.ds(pl.multiple_of(g * SSM_GROUP, SSM_GROUP), SSM_GROUP)
        ub = vt[:, chans, :].reshape(width, n_chunks).astype(BF16)
        yield

        tt = _dot_nt(qm.astype(BF16), km_conj.astype(BF16))
        x = _dot(wz.T.astype(BF16), ub).T
        yield
        y = _dot((tt * tmask[...]).astype(BF16), ub)
        p_r, p_i = kr[3:4], ki[3:4]
        d = 1
        while d < n_chunks:
            s = jnp.where(row >= d, pltpu.roll(x, d, 0), 0.0)
            x = x + s * p_r - pltpu.roll(s, SSM_STATE, 1) * (p_i * conj)
            p_r, p_i = cmul(p_r, p_i, p_r, p_i)
            d *= 2
        prev_conj = jnp.where(row >= 1, pltpu.roll(x, 1, 0), 0.0) * conj
        yield
        y = y + _dot_nt(qc.astype(BF16), prev_conj.astype(BF16))
        yt[:, chans, :] = y.reshape(L, SSM_GROUP, n_chunks)

    def group_batch(k, carry):
        running = [group(S5_GROUPS_PER_TRIP * k + j) for j in range(S5_GROUPS_PER_TRIP)]
        finished = object()
        while running:
            running = [gen for gen in running if next(gen, finished) is not finished]
        return carry

    lax.fori_loop(0, LANES // SSM_GROUP // S5_GROUPS_PER_TRIP, group_batch, 0)
    by_chunk = jnp.swapaxes(jnp.stack([yt[tau].T for tau in range(L)], axis=0), 0, 1)
    y = by_chunk.reshape(n_chunks * L, LANES) + d_ref[...] * x_ref[...]
    y_ref[...] = jax.nn.gelu(y).astype(y_ref.dtype)


def _s5_mixer(h, hn, a_re, a_im, log_dt, b_re, b_im, c_re, c_im, d_skip, w_glu, layer):
    s = h.shape[0]
    L = S5_CHUNK
    n_chunks = s // L
    width = L * SSM_GROUP
    gpb = LANES // SSM_GROUP
    def packed(re, im):
        return jnp.stack([jnp.concatenate([re, im], -1), jnp.concatenate([-im, re], -1)], axis=1).astype(F32)

    lam = jnp.stack([a_re, a_im, jnp.broadcast_to(log_dt[:, None], a_re.shape)], axis=1).astype(F32)
    lam = jnp.concatenate([lam, lam], axis=-1)
    bt = packed(b_re.transpose(0, 2, 1), b_im.transpose(0, 2, 1))
    cc = packed(c_re, c_im)
    dd = d_skip.astype(F32).reshape(1, D_MODEL)
    y = pl.pallas_call(
        functools.partial(_s5_kernel, chunk=L),
        out_shape=jax.ShapeDtypeStruct((s, D_MODEL), BF16),
        grid=(SSM_GROUPS // gpb,),
        in_specs=[
            pl.BlockSpec((s, LANES), lambda b: (0, b)),
            pl.BlockSpec((gpb, 3, 2 * SSM_STATE), lambda b: (b, 0, 0)),
            pl.BlockSpec((gpb, 2, SSM_GROUP, 2 * SSM_STATE), lambda b: (b, 0, 0, 0)),
            pl.BlockSpec((gpb, 2, SSM_GROUP, 2 * SSM_STATE), lambda b: (b, 0, 0, 0)),
            pl.BlockSpec((1, LANES), lambda b: (0, b)),
        ],
        out_specs=pl.BlockSpec((s, LANES), lambda b: (0, b)),
        scratch_shapes=[
            pltpu.VMEM((L, LANES, n_chunks), hn.dtype),
            pltpu.VMEM((L, LANES, n_chunks), F32),
            pltpu.VMEM((width, width), F32),
        ],
        compiler_params=_params("arbitrary"),
    )(hn, lam, bt, cc, dd)
    return _matmul(_ep_glu_resid, y, w_glu, layer, D_MODEL, F32, 1024, 512, b_halves=2, tile_extra=h)


def kernel(x, rel_bias, mix_norm_g, ffn_norm_g, final_norm_g, nsa_w_in, nsa_w_out, cmp_pos_k, cmp_w1_k, cmp_w2_k, cmp_pos_v, cmp_w1_v, cmp_w2_v, s5_A_re, s5_A_im, s5_log_dt, s5_B_re, s5_B_im, s5_C_re, s5_C_im, s5_D, s5_w_glu, ffn_w_in, ffn_w_out):
    assert x.shape[0] == 1
    h = x[0]
    hn = _rmsnorm(h, mix_norm_g[0], BF16)
    h, hn = _nsa_mixer(h, hn, rel_bias, nsa_w_in, nsa_w_out, 0, cmp_pos_k[0], cmp_w1_k[0], cmp_w2_k[0],
                       cmp_pos_v[0], cmp_w1_v[0], cmp_w2_v[0], ffn_norm_g[0])
    h = _swiglu_ffn(h, hn, ffn_w_in, ffn_w_out, 0)
    hn = _rmsnorm(h, mix_norm_g[1], BF16)
    h = _s5_mixer(h, hn, s5_A_re[0], s5_A_im[0], s5_log_dt[0], s5_B_re[0], s5_B_im[0], s5_C_re[0], s5_C_im[0],
                  s5_D[0], s5_w_glu, 0)
    h = _swiglu_ffn(h, _rmsnorm(h, ffn_norm_g[1], BF16), ffn_w_in, ffn_w_out, 1)
    return _rmsnorm(h, final_norm_g, x.dtype)[None]
```

```python
import functools
import math

import numpy as np
import jax
import jax.numpy as jnp
from jax import lax
from jax.experimental import pallas as pl
from jax.experimental.pallas import tpu as pltpu

D_MODEL = 2048
N_HEADS = 16
HEAD_DIM = 128
N_KV_GROUPS = 4
HPG = N_HEADS // N_KV_GROUPS
CMP_BLOCK = 32
CMP_STRIDE = 16
CMP_HIDDEN = 2 * HEAD_DIM
SEL_BLOCK = 64
SEL_TOPK = 16
N_FORCED = 3
WINDOW = 512
N_GATES = 3
KV_WIDTH = N_KV_GROUPS * HEAD_DIM
Q_WIDTH = N_HEADS * HEAD_DIM
PROJ_MAIN = Q_WIDTH + 6 * KV_WIDTH
REL_BUCKETS = 32
REL_MAX_DIST = 128
SSM_GROUP = 16
SSM_GROUPS = D_MODEL // SSM_GROUP
SSM_STATE = 64
D_FF = ((8 * D_MODEL + 2) // 3 + 255) // 256 * 256
RMS_EPS = 1e-6
NEG = -1e30
LOG2E = math.log2(math.e)

LANES = 128
SUBLANES = 8
VMEM_LIMIT = 52 * 1024 * 1024

ATT_TILE = WINDOW // 2
FAR_TILES = 4
Q_TILES = 2
CMP_TQ = 256
CMP_PAD = 8
S5_CHUNK = 32
S5_GROUPS_PER_TRIP = 4

BF16 = jnp.bfloat16
F32 = jnp.float32


def _dot(a, b):
    return jnp.dot(a, b, preferred_element_type=F32)


def _dot_nt(a, b):
    return lax.dot_general(a, b, (((1,), (1,)), ((), ())), preferred_element_type=F32)


def _params(*sem):
    return pltpu.CompilerParams(dimension_semantics=sem, vmem_limit_bytes=VMEM_LIMIT)


def _rmsnorm_kernel(x_ref, g_ref, o_ref):
    x = x_ref[...]
    ms = jnp.mean(x * x, axis=-1, keepdims=True)
    o_ref[...] = (x * lax.rsqrt(ms + RMS_EPS) * g_ref[...]).astype(o_ref.dtype)


def _rmsnorm(x, g, out_dtype, tm=512):
    s, d = x.shape
    return pl.pallas_call(
        _rmsnorm_kernel,
        out_shape=jax.ShapeDtypeStruct((s, d), out_dtype),
        grid=(s // tm,),
        in_specs=[pl.BlockSpec((tm, d), lambda i: (i, 0)), pl.BlockSpec((1, d), lambda i: (0, 0))],
        out_specs=pl.BlockSpec((tm, d), lambda i: (i, 0)),
        compiler_params=_params("parallel"),
    )(x, g.reshape(1, d))


def _mm_kernel(*refs, n_b, n_extra, epilogue, w_rows_are_outputs):
    a_ref = refs[0]
    b_refs = refs[1:1 + n_b]
    extra_refs = refs[1 + n_b:1 + n_b + n_extra]
    o_ref = refs[1 + n_b + n_extra]
    w_scr = refs[2 + n_b + n_extra:]

    @pl.when(pl.program_id(1) == 0)
    def _():
        for b_ref, w in zip(b_refs, w_scr):
            w[...] = b_ref[...].astype(BF16)

    a = a_ref[...]
    z = [(_dot_nt if w_rows_are_outputs else _dot)(a, w[...]) for w in w_scr]
    o_ref[...] = epilogue(*z, *[e[...] for e in extra_refs]).astype(o_ref.dtype)


def _ep_scale(z, scale):
    return z * scale


def _ep_resid(z, resid):
    return resid + z


def _ep_swiglu(za, zb):
    return jax.nn.silu(za) * zb


def _ep_glu_resid(za, zb, resid):
    return resid + za * jax.nn.sigmoid(zb)


def _matmul(epilogue, a, b, layer, n_out, out_dtype, tm, tn, *, b_halves=1, row_extra=None, tile_extra=None,
            w_rows_are_outputs=False):
    m, k = a.shape
    nj = n_out // tn
    estimate = (b_halves * k * tn * (2 * 4 + 2) + 2 * tm * k * 2 + 4 * tm * tn * 4 + b_halves * tm * tn * 4)
    b_mode = pl.Buffered(1) if estimate > VMEM_LIMIT else None
    in_specs = [pl.BlockSpec((tm, k), lambda j, i: (i, 0))]
    args = [a]
    for half in range(b_halves):
        if w_rows_are_outputs:
            spec = pl.BlockSpec((None, tn, k), functools.partial(lambda j, i, o: (layer, j + o, 0), o=half * nj),
                                pipeline_mode=b_mode)
        else:
            spec = pl.BlockSpec((None, k, tn), functools.partial(lambda j, i, o: (layer, 0, j + o), o=half * nj),
                                pipeline_mode=b_mode)
        in_specs.append(spec)
        args.append(b)
    extras = []
    if row_extra is not None:
        in_specs.append(pl.BlockSpec((1, tn), lambda j, i: (0, j)))
        extras.append(row_extra)
    if tile_extra is not None:
        in_specs.append(pl.BlockSpec((tm, tn), lambda j, i: (i, j)))
        extras.append(tile_extra)
    kern = functools.partial(_mm_kernel, n_b=b_halves, n_extra=len(extras), epilogue=epilogue,
                             w_rows_are_outputs=w_rows_are_outputs)
    return pl.pallas_call(
        kern,
        out_shape=jax.ShapeDtypeStruct((m, n_out), out_dtype),
        grid=(nj, m // tm),
        in_specs=in_specs,
        out_specs=pl.BlockSpec((tm, tn), lambda j, i: (i, j)),
        scratch_shapes=[pltpu.VMEM((tn, k) if w_rows_are_outputs else (k, tn), BF16) for _ in range(b_halves)],
        compiler_params=_params("arbitrary", "arbitrary"),
    )(*args, *extras)


def _mm_resid_norm_kernel(a_ref, b_ref, r_ref, g_ref, h_ref, hn_ref, w_scr):
    @pl.when(pl.program_id(0) == 0)
    def _():
        w_scr[...] = b_ref[...].astype(BF16)

    h = r_ref[...] + _dot(a_ref[...], w_scr[...])
    h_ref[...] = h
    ms = jnp.mean(h * h, axis=-1, keepdims=True)
    hn_ref[...] = (h * lax.rsqrt(ms + RMS_EPS) * g_ref[...]).astype(hn_ref.dtype)


def _matmul_resid_norm(a, b, layer, resid, g, tm):
    m, k = a.shape
    n = b.shape[2]
    return pl.pallas_call(
        _mm_resid_norm_kernel,
        out_shape=(jax.ShapeDtypeStruct((m, n), F32), jax.ShapeDtypeStruct((m, n), BF16)),
        grid=(m // tm,),
        in_specs=[
            pl.BlockSpec((tm, k), lambda i: (i, 0)),
            pl.BlockSpec((None, k, n), lambda i: (layer, 0, 0), pipeline_mode=pl.Buffered(1)),
            pl.BlockSpec((tm, n), lambda i: (i, 0)),
            pl.BlockSpec((1, n), lambda i: (0, 0)),
        ],
        out_specs=(pl.BlockSpec((tm, n), lambda i: (i, 0)), pl.BlockSpec((tm, n), lambda i: (i, 0))),
        scratch_shapes=[pltpu.VMEM((k, n), BF16)],
        compiler_params=_params("arbitrary"),
    )(a, b, resid, g.reshape(1, n))


def _swiglu_ffn(h, hn, w_in, w_out, layer):
    act = _matmul(_ep_swiglu, hn, w_in, layer, D_FF, BF16, 1024, 512, b_halves=2)
    return _matmul(_ep_resid, act, w_out, layer, D_MODEL, F32, 512, 512, tile_extra=h)


def _compress_kernel(x_ref, w1_ref, w2_ref, pos_ref, o_ref):
    ncp = x_ref.shape[0] // CMP_STRIDE
    top = jnp.zeros((ncp, CMP_HIDDEN), F32)
    bot = jnp.zeros((ncp, CMP_HIDDEN), F32)
    by_offset = jnp.swapaxes(x_ref[...].reshape(ncp, CMP_STRIDE, HEAD_DIM), 0, 1)
    for r in range(CMP_STRIDE):
        x_r = by_offset[r]
        top = top + _dot(x_r, w1_ref[0, r * HEAD_DIM:(r + 1) * HEAD_DIM, :])
        bot = bot + _dot(x_r, w1_ref[0, (CMP_STRIDE + r) * HEAD_DIM:(CMP_STRIDE + r + 1) * HEAD_DIM, :])
    posb = _dot(pos_ref[0], w1_ref[0])[0:1]
    hid = top + pltpu.roll(bot, ncp - 1, 0) + posb
    o_ref[0, 0] = _dot(jax.nn.gelu(hid).astype(BF16), w2_ref[0]).astype(o_ref.dtype)


def _compress(proj, w1, w2, pos):
    s = proj.shape[0]
    ncp = s // CMP_STRIDE
    kc_blk = Q_WIDTH // HEAD_DIM
    return pl.pallas_call(
        _compress_kernel,
        out_shape=jax.ShapeDtypeStruct((2, N_KV_GROUPS, ncp, HEAD_DIM), BF16),
        grid=(2, N_KV_GROUPS),
        in_specs=[
            pl.BlockSpec((s, HEAD_DIM), lambda kv, gi: (0, kc_blk + kv * N_KV_GROUPS + gi)),
            pl.BlockSpec((1, CMP_BLOCK * HEAD_DIM, CMP_HIDDEN), lambda kv, gi: (kv, 0, 0)),
            pl.BlockSpec((1, CMP_HIDDEN, HEAD_DIM), lambda kv, gi: (kv, 0, 0)),
            pl.BlockSpec((1, SUBLANES, CMP_BLOCK * HEAD_DIM), lambda kv, gi: (kv, 0, 0)),
        ],
        out_specs=pl.BlockSpec((1, 1, ncp, HEAD_DIM), lambda kv, gi: (kv, gi, 0, 0)),
        compiler_params=_params("parallel", "parallel"),
    )(proj, w1, w2, pos)


def _cmp_attn_kernel(q_ref, kcc_ref, vcct_ref, band_ref, mt_ref, oc_ref, nm_ref, s_scr, *, tq, nb):
    i = pl.program_id(1)
    band_start = pl.multiple_of(i * (tq // CMP_STRIDE), SUBLANES)
    n_rows = kcc_ref.shape[1]
    n_blk = mt_ref.shape[0]

    def body(n_r, n_b):
        kcc = kcc_ref[0, :n_r, :]
        vcct = vcct_ref[0, :, :n_r]
        rows = lax.broadcasted_iota(jnp.int32, (n_r, tq), 0)
        row_bias = jnp.where(rows >= CMP_PAD, jnp.where(rows < band_start + nb, 0.0, NEG), NEG)
        imp = jnp.zeros((n_r, tq), F32)

        def scores(h):
            qh = q_ref[:, h * HEAD_DIM:(h + 1) * HEAD_DIM]
            s_scr[h, :n_r, :] = _dot_nt(kcc, qh) + row_bias
            s_scr[h, pl.ds(band_start, nb), :] += band_ref[h]

        scores(0)
        probs = []
        for h in range(HPG):
            if h + 1 < HPG:
                scores(h + 1)
            s = s_scr[h, :n_r, :]
            m = jnp.maximum(jnp.max(s, axis=0, keepdims=True), 0.5 * NEG)
            e = jnp.exp2(s - m)
            l = jnp.sum(e, axis=0, keepdims=True)
            pn = e * (1.0 / jnp.where(l > 0.0, l, 1.0))
            imp = imp + pn
            probs.append(pn.astype(BF16))

        def outputs():
            for h in range(HPG):
                oct_h = _dot(vcct, probs[h])
                oc_ref[:, h * HEAD_DIM:(h + 1) * HEAD_DIM] = oct_h.T
                yield
        mt = mt_ref[:n_b, :n_r]
        hi = imp.astype(BF16)
        rest = imp - hi.astype(F32)
        mid = rest.astype(BF16)
        lo = (rest - mid.astype(F32)).astype(BF16)
        slc = _dot(mt, hi) + _dot(mt, mid) + _dot(mt, lo)
        jj = lax.broadcasted_iota(jnp.int32, (n_b, tq), 0)
        t_blk = (i * tq + lax.broadcasted_iota(jnp.int32, (n_b, tq), 1)) // SEL_BLOCK
        forced = jnp.where(jj == 0, 1, jnp.where(jj == t_blk, 1, jnp.where(jj == t_blk - 1, 1, 0)))
        picked = jnp.where(forced == 1, 1.0, 0.0)
        score = jnp.where(forced == 1, -jnp.inf, jnp.where(jj <= t_blk, slc, NEG))
        pending = outputs()
        for r in range(SEL_TOPK - N_FORCED):
            best = jnp.max(score, axis=0, keepdims=True)
            first = jnp.min(jnp.where(score == best, jj, n_blk), axis=0, keepdims=True)
            hit = jj == first
            picked = jnp.where(hit, 1.0, picked)
            score = jnp.where(hit, -jnp.inf, score)
            if r % 3 == 0:
                next(pending, None)
        for _ in pending:
            pass
        neg_mask = jnp.where(picked > 0.5, jnp.where(jj <= t_blk, 0.0, NEG), NEG)
        if n_b < n_blk:
            neg_mask = jnp.concatenate([neg_mask, jnp.full((n_blk - n_b, tq), NEG, F32)], axis=0)
        nm_ref[...] = neg_mask.T.astype(nm_ref.dtype)

    step = LANES
    extents = list(range(step, n_rows, step)) + [n_rows]
    need_rows = band_start + nb
    for v, n_r in enumerate(extents):
        lower = extents[v - 1] if v else 0
        n_b = min(n_blk, -(-(n_r * CMP_STRIDE // SEL_BLOCK) // SUBLANES) * SUBLANES)
        pl.when(jnp.logical_and(need_rows > lower, need_rows <= n_r))(functools.partial(body, n_r, n_b))


def _cmp_attn(proj, kcc, vcct, band, mt, tq):
    s = proj.shape[0]
    n_rows = kcc.shape[1]
    nb = band.shape[1]
    kern = functools.partial(_cmp_attn_kernel, tq=tq, nb=nb)
    return pl.pallas_call(
        kern,
        out_shape=(jax.ShapeDtypeStruct((s, Q_WIDTH), F32),
                   jax.ShapeDtypeStruct((s, N_KV_GROUPS * LANES), BF16)),
        grid=(N_KV_GROUPS, s // tq),
        in_specs=[
            pl.BlockSpec((tq, HPG * HEAD_DIM), lambda g, i: (i, g)),
            pl.BlockSpec((1, n_rows, HEAD_DIM), lambda g, i: (g, 0, 0)),
            pl.BlockSpec((1, HEAD_DIM, n_rows), lambda g, i: (g, 0, 0)),
            pl.BlockSpec((HPG, nb, tq), lambda g, i: (g, 0, 0)),
            pl.BlockSpec((LANES, n_rows), lambda g, i: (0, 0)),
        ],
        out_specs=(pl.BlockSpec((tq, HPG * HEAD_DIM), lambda g, i: (i, g)),
                   pl.BlockSpec((tq, LANES), lambda g, i: (i, g))),
        scratch_shapes=[pltpu.VMEM((HPG, n_rows, tq), F32)],
        compiler_params=_params("parallel", "parallel"),
    )(proj, kcc, vcct, band, mt)


def _lane_tile(x, width):
    return jnp.concatenate([x] * (width // x.shape[1]), axis=1)


def _softmax_update(s, v_aug, m_ref, acc_ref, rows):
    m_prev = m_ref[rows, :]
    m_next = jnp.maximum(m_prev, jnp.max(s, axis=1, keepdims=True))
    alpha = jnp.exp2(m_prev - m_next)
    p = jnp.exp2(s - _lane_tile(m_next, s.shape[1]))
    acc_ref[rows, :] = _lane_tile(alpha, acc_ref.shape[1]) * acc_ref[rows, :] + _dot(p.astype(BF16), v_aug)
    m_ref[rows, :] = m_next


def _softmax_once(s, v_aug):
    m = jnp.broadcast_to(jnp.max(s, axis=1, keepdims=True), (s.shape[0], LANES))
    p = jnp.exp2(s - _lane_tile(m, s.shape[1]))
    return _dot(p.astype(BF16), v_aug)


def _sel_win_kernel(q_ref, nm_ref, ks_ref, vs_ref, kw_ref, vw_ref, blk_ref, per_ref, g_ref, oc_ref, o_ref,
                    q4, m_s, a_s, bias_ref, *, tile):
    step = pl.program_id(1)
    t = tile
    first = step * Q_TILES

    @pl.when(step == 0)
    def _():
        tl = lax.broadcasted_iota(jnp.int32, (t, t), 0)
        kl = lax.broadcasted_iota(jnp.int32, (t, t), 1)
        edge = jnp.where(tl < kl, 0.0, NEG)
        for h in range(HPG):
            bias_ref[h, :, :t] = edge
            for which in range(2):
                rows = jnp.broadcast_to(per_ref[h, which:which + 1, :], (t, 2 * t))
                table = pltpu.roll(rows, 0, 1, stride=1, stride_axis=0)
                bias_ref[h, :, (which + 1) * t:(which + 2) * t] = table[:, :t]

    def stacked(qt, h):
        return slice((qt * HPG + h) * t, (qt * HPG + h + 1) * t)

    def tile_rows(qt):
        return slice(qt * t, (qt + 1) * t)

    for qt in range(Q_TILES):
        for h in range(HPG):
            q4[stacked(qt, h), :HEAD_DIM] = q_ref[tile_rows(qt), h * HEAD_DIM:(h + 1) * HEAD_DIM]
            q4[stacked(qt, h), HEAD_DIM:] = nm_ref[tile_rows(qt), :]
    m_s[...] = jnp.full(m_s.shape, -3e38, F32)
    a_s[...] = jnp.zeros(a_s.shape, F32)

    def sel_operands(row0, n_tiles):
        keys = pl.ds(row0, n_tiles * t)
        k_aug = jnp.concatenate([ks_ref[keys, :], blk_ref[keys, :]], axis=1)
        v_aug = jnp.concatenate([vs_ref[keys, :], jnp.ones((n_tiles * t, LANES), BF16)], axis=1)
        return k_aug, v_aug

    def sel_pass(qt, operands, n_biased):
        k_aug, v_aug = operands

        def score(h):
            s = _dot_nt(q4[stacked(qt, h), :], k_aug)
            plain = s.shape[1] - n_biased * t
            if n_biased:
                near = s[:, plain:] + bias_ref[h, :, (3 - n_biased) * t:]
                s = jnp.concatenate([s[:, :plain], near], axis=1) if plain else near
            return s

        def consume(h, s):
            _softmax_update(s, v_aug, m_s, a_s, stacked(qt, h))

        return score, consume

    def win_pass(qt, row0, n_tiles):
        keys = pl.ds(row0, n_tiles * t)
        v_aug = jnp.concatenate([vw_ref[keys, :], jnp.ones((n_tiles * t, LANES), BF16)], axis=1)

        def score(h):
            return _dot_nt(q4[stacked(qt, h), :HEAD_DIM], kw_ref[keys, :]) + bias_ref[h, :, (3 - n_tiles) * t:]

        def consume(h, s):
            acc_w = _softmax_once(s, v_aug)
            acc_s = a_s[stacked(qt, h), :]
            o_w = acc_w[:, :HEAD_DIM] * (1.0 / acc_w[:, HEAD_DIM:])
            o_s = acc_s[:, :HEAD_DIM] * (1.0 / acc_s[:, HEAD_DIM:])
            cols = slice(h * HEAD_DIM, (h + 1) * HEAD_DIM)
            gates = g_ref[tile_rows(qt), :]
            o = (gates[:, h:h + 1] * oc_ref[tile_rows(qt), cols] + gates[:, HPG + h:HPG + h + 1] * o_s
                 + gates[:, 2 * HPG + h:2 * HPG + h + 1] * o_w)
            o_ref[tile_rows(qt), cols] = o.astype(o_ref.dtype)

        return score, consume

    def run(passes):
        work = [(score, consume, h0) for score, consume in passes for h0 in range(0, HPG, 2)]
        ahead = [work[0][0](h) for h in (0, 1)]
        for n, (_, consume, h0) in enumerate(work):
            now = ahead
            if n + 1 < len(work):
                nxt_score, _, nxt_h0 = work[n + 1]
                ahead = [nxt_score(h) for h in (nxt_h0, nxt_h0 + 1)]
            for h, s in zip((h0, h0 + 1), now):
                consume(h, s)

    def far_passes(row0):
        operands = sel_operands(row0, FAR_TILES)
        return [sel_pass(qt, operands, 0) for qt in range(Q_TILES)]

    n_far = jnp.maximum(first - 1, 0)
    n_chunks = n_far // FAR_TILES
    chunk = FAR_TILES * t

    def far_pair(c, carry):
        row0 = pl.multiple_of(c * 2 * chunk, 2 * chunk)
        run(far_passes(row0) + far_passes(row0 + chunk))
        return carry

    lax.fori_loop(0, n_chunks // 2, far_pair, 0)

    @pl.when(n_chunks % 2 == 1)
    def _():
        run(far_passes(pl.multiple_of((n_chunks - 1) * chunk, chunk)))

    def tail(left):
        row0 = pl.multiple_of((n_far - left) * t, t)
        passes = []
        for qt in range(Q_TILES):
            n_sel = left + 2 + qt
            passes.append(sel_pass(qt, sel_operands(row0, n_sel), 2))
            passes.append(win_pass(qt, pl.multiple_of((first + qt - 2) * t, t), 3))
        run(passes)

    for left in range(FAR_TILES):
        if (left + 1) % Q_TILES == 0:
            pl.when(jnp.logical_and(step >= 1, n_far % FAR_TILES == left))(functools.partial(tail, left))

    @pl.when(step == 0)
    def _():
        passes = []
        for qt in range(Q_TILES):
            n_keys = min(qt + 1, 3)
            passes.append(sel_pass(qt, sel_operands(0, qt + 1), min(qt + 1, 2)))
            passes.append(win_pass(qt, (qt + 1 - n_keys) * t, n_keys))
        run(passes)


def _sel_win(proj, neg_mask, blk_onehot, periods, gates, o_c, tile):
    s = proj.shape[0]
    ks_blk = (Q_WIDTH + 2 * KV_WIDTH) // HEAD_DIM
    vs_blk = (Q_WIDTH + 3 * KV_WIDTH) // HEAD_DIM
    kw_blk = (Q_WIDTH + 4 * KV_WIDTH) // HEAD_DIM
    vw_blk = (Q_WIDTH + 5 * KV_WIDTH) // HEAD_DIM
    once = pl.Buffered(1)
    rows = Q_TILES * tile
    stacked_rows = HPG * rows
    kern = functools.partial(_sel_win_kernel, tile=tile)
    return pl.pallas_call(
        kern,
        out_shape=jax.ShapeDtypeStruct((s, Q_WIDTH), BF16),
        grid=(N_KV_GROUPS, s // rows),
        in_specs=[
            pl.BlockSpec((rows, HPG * HEAD_DIM), lambda g, i: (i, g)),
            pl.BlockSpec((rows, LANES), lambda g, i: (i, g)),
            pl.BlockSpec((s, HEAD_DIM), lambda g, i: (0, ks_blk + g), pipeline_mode=once),
            pl.BlockSpec((s, HEAD_DIM), lambda g, i: (0, vs_blk + g), pipeline_mode=once),
            pl.BlockSpec((s, HEAD_DIM), lambda g, i: (0, kw_blk + g), pipeline_mode=once),
            pl.BlockSpec((s, HEAD_DIM), lambda g, i: (0, vw_blk + g), pipeline_mode=once),
            pl.BlockSpec((s, LANES), lambda g, i: (0, 0), pipeline_mode=once),
            pl.BlockSpec((HPG, 2, 2 * tile), lambda g, i: (g, 0, 0)),
            pl.BlockSpec((rows, LANES), lambda g, i: (i, g)),
            pl.BlockSpec((rows, HPG * HEAD_DIM), lambda g, i: (i, g)),
        ],
        out_specs=pl.BlockSpec((rows, HPG * HEAD_DIM), lambda g, i: (i, g)),
        scratch_shapes=[
            pltpu.VMEM((stacked_rows, 2 * HEAD_DIM), BF16),
            pltpu.VMEM((stacked_rows, LANES), F32),
            pltpu.VMEM((stacked_rows, 2 * HEAD_DIM), F32),
            pltpu.VMEM((HPG, tile, 3 * tile), F32),
        ],
        compiler_params=_params("arbitrary", "arbitrary"),
    )(proj, neg_mask, proj, proj, proj, proj, blk_onehot, periods, gates, o_c)


def _rel_bucket_np(dist):
    n = np.maximum(dist, 0)
    max_exact = REL_BUCKETS // 2
    large = max_exact + (np.log(np.maximum(n, 1).astype(np.float32) / np.float32(max_exact))
                         / np.float32(math.log(REL_MAX_DIST / max_exact))
                         * np.float32(REL_BUCKETS - max_exact)).astype(np.int32)
    large = np.minimum(large, REL_BUCKETS - 1)
    return np.where(n < max_exact, n, large).astype(np.int32)


def _far_distance():
    d = np.arange(4 * REL_MAX_DIST)
    b = _rel_bucket_np(d)
    assert b[-1] == REL_BUCKETS - 1
    return int(np.max(np.nonzero(b != REL_BUCKETS - 1)[0])) + 1


def _bias_table(rel_bias, dist):
    bucket = jnp.asarray(_rel_bucket_np(dist))
    shifted = (rel_bias[bucket] - rel_bias[REL_BUCKETS - 1]) * LOG2E
    tab = jnp.where(jnp.asarray(dist >= 0)[..., None], shifted, NEG)
    return jnp.moveaxis(tab, -1, 0).astype(F32)


def _bias_period(rel_bias, offset, t):
    x = np.arange(2 * t)
    return _bias_table(rel_bias, np.where(x < t, offset - x, offset + 2 * t - x))


def _nsa_mixer(h, hn, rel_bias, w_in, w_out, layer, pos_k, w1_k, w2_k, pos_v, w1_v, w2_v, next_norm_g):
    s = h.shape[0]
    n_sel = s // SEL_BLOCK
    ncp = s // CMP_STRIDE
    tile = ATT_TILE
    far = _far_distance()
    assert n_sel <= LANES and s % CMP_TQ == 0 and s % (Q_TILES * tile) == 0 and FAR_TILES % Q_TILES == 0
    assert far <= tile and far <= CMP_STRIDE * CMP_PAD - CMP_BLOCK + 1 + CMP_STRIDE

    col_scale = jnp.where(jnp.arange(PROJ_MAIN) < Q_WIDTH, HEAD_DIM ** -0.5 * LOG2E, 1.0).astype(F32)[None, :]
    w_in_t = jnp.swapaxes(w_in, 1, 2)
    proj = _matmul(_ep_scale, hn, w_in_t, layer, PROJ_MAIN, BF16, 2048, 512, row_extra=col_scale,
                   w_rows_are_outputs=True)
    w_gate = w_in_t[layer, PROJ_MAIN:].reshape(N_GATES, N_KV_GROUPS, HPG, D_MODEL).transpose(1, 0, 2, 3)
    w_gate = w_gate.reshape(N_KV_GROUPS, N_GATES * HPG, D_MODEL)
    w_gate = jnp.pad(w_gate, ((0, 0), (0, LANES - N_GATES * HPG), (0, 0))).reshape(1, N_KV_GROUPS * LANES, D_MODEL)
    gates = _matmul(jax.nn.sigmoid, hn, w_gate, 0, N_KV_GROUPS * LANES, F32, 2048, 512, w_rows_are_outputs=True)

    w1 = jnp.stack([w1_k, w1_v]).astype(BF16)
    w2 = jnp.stack([w2_k, w2_v]).astype(BF16)
    pos = jnp.stack([pos_k.reshape(1, -1), pos_v.reshape(1, -1)]).astype(BF16)
    pos = jnp.broadcast_to(pos, (2, SUBLANES, CMP_BLOCK * HEAD_DIM))
    cc = _compress(proj, w1, w2, pos)
    kcc = jnp.pad(cc[0], ((0, 0), (CMP_PAD, 0), (0, 0)))
    vcct = jnp.pad(cc[1], ((0, 0), (CMP_PAD, 0), (0, 0))).transpose(0, 2, 1)
    n_rows = ncp + CMP_PAD

    nb = CMP_TQ // CMP_STRIDE + CMP_PAD
    first_dist = -CMP_STRIDE * (nb - 1 - CMP_PAD) - (CMP_BLOCK - 1)
    by_dist = _bias_table(rel_bias, first_dist + np.arange(CMP_TQ + CMP_STRIDE * (nb - 1)))
    band = jnp.stack([by_dist[:, CMP_STRIDE * (nb - 1 - b):CMP_STRIDE * (nb - 1 - b) + CMP_TQ] for b in range(nb)], axis=1)
    ratio = SEL_BLOCK // CMP_STRIDE
    lo = CMP_BLOCK // CMP_STRIDE - 1
    c_of_row = np.arange(n_rows)[None, :] - CMP_PAD
    j_of = np.arange(LANES)[:, None]
    mt = ((c_of_row >= ratio * j_of - lo) & (c_of_row <= ratio * j_of + ratio - 1)
          & (c_of_row >= 0) & (c_of_row <= ncp - 2) & (j_of < n_sel)).astype(np.float32)
    o_c, neg_mask = _cmp_attn(proj, kcc, vcct, band, jnp.asarray(mt, BF16), CMP_TQ)

    onehot = (np.arange(s)[:, None] // SEL_BLOCK == np.arange(LANES)[None, :]).astype(np.float32)
    periods = jnp.stack([_bias_period(rel_bias, tile, tile), _bias_period(rel_bias, 0, tile)], axis=1)
    o = _sel_win(proj, neg_mask, jnp.asarray(onehot, BF16), periods, gates, o_c, tile)
    return _matmul_resid_norm(o, w_out, layer, h, next_norm_g, 256)


def _s5_kernel(x_ref, lam_ref, bt_ref, c_ref, d_ref, y_ref, vt, yt, tmask, *, chunk):
    L = chunk
    half = L // 2
    width = L * SSM_GROUP
    n_chunks = x_ref.shape[0] // L
    by_step = jnp.swapaxes(x_ref[...].reshape(n_chunks, L, LANES), 0, 1)
    for tau in range(L):
        vt[tau] = by_step[tau].T

    @pl.when(pl.program_id(0) == 0)
    def _():
        dst = lax.broadcasted_iota(jnp.int32, (width, width), 0) // SSM_GROUP
        src = lax.broadcasted_iota(jnp.int32, (width, width), 1) // SSM_GROUP
        tmask[...] = jnp.where(src <= dst, 1.0, 0.0)

    def cmul(xr, xi, yr, yi):
        return xr * yr - xi * yi, xr * yi + xi * yr

    n2 = 2 * SSM_STATE
    sub = lax.broadcasted_iota(jnp.int32, (SUBLANES, 1), 0)
    consts = jnp.where(sub == 0, 1.0, jnp.where(sub == 1, float(half + 1), jnp.where(sub == 2, float(half - 1), float(L))))
    tau_col = lax.broadcasted_iota(jnp.int32, (L, 1), 0).astype(F32)
    row = lax.broadcasted_iota(jnp.int32, (n_chunks, n2), 0)
    conj = jnp.where(lax.broadcasted_iota(jnp.int32, (1, n2), 1) < SSM_STATE, 1.0, -1.0)

    def group(g):
        lam = lam_ref[g]
        a_re, a_im = lam[0:1], lam[1:2]
        dt = jnp.exp(lam[2:3])
        log_re, log_im = a_re * dt, a_im * dt

        def cpow(e):
            mag, cos, sin = jnp.exp(e * log_re), jnp.cos(e * log_im), jnp.sin(e * log_im)
            inv = 1.0 / mag
            return mag * cos, mag * sin, inv * cos, -(inv * sin)

        kr, ki, _, _ = cpow(consts)
        pr, pi = kr[0:1] - 1.0, ki[0:1]
        den = a_re * a_re + a_im * a_im
        cf_r = (pr * a_re + pi * a_im) / den
        cf_i = (pi * a_re - pr * a_im) / den
        bt_c, bt_s = bt_ref[g, 0], bt_ref[g, 1]
        bb_c = cf_r * bt_c + cf_i * bt_s
        bb_s = cf_r * bt_s - cf_i * bt_c
        c_c, c_s = c_ref[g, 0], c_ref[g, 1]

        e1r, e1i, e2r, e2i = cpow(tau_col - half)
        e3r, e3i = cmul(e1r, e1i, kr[1:2], ki[1:2])
        e4r, e4i = cmul(e2r, e2i, kr[2:3], ki[2:3])

        def outer(er, ei, w_c, w_s):
            return (er[:, None, :] * w_c[None, :, :] + ei[:, None, :] * w_s[None, :, :]).reshape(width, n2)

        qm = outer(e1r, e1i, c_c, c_s)
        km_conj = outer(e2r, e2i, bb_c * conj, bb_s * conj)
        qc = outer(e3r, e3i, c_c, c_s)
        wz = outer(e4r, e4i, bb_c, bb_s)

        chans = pl.ds(pl.multiple_of(g * SSM_GROUP, SSM_GROUP), SSM_GROUP)
        ub = vt[:, chans, :].reshape(width, n_chunks).astype(BF16)
        yield

        tt = _dot_nt(qm.astype(BF16), km_conj.astype(BF16))
        x = _dot(wz.T.astype(BF16), ub).T
        yield
        y = _dot((tt * tmask[...]).astype(BF16), ub)
        p_r, p_i = kr[3:4], ki[3:4]
        d = 1
        while d < n_chunks:
            s = jnp.where(row >= d, pltpu.roll(x, d, 0), 0.0)
            x = x + s * p_r - pltpu.roll(s, SSM_STATE, 1) * (p_i * conj)
            p_r, p_i = cmul(p_r, p_i, p_r, p_i)
            d *= 2
        prev_conj = jnp.where(row >= 1, pltpu.roll(x, 1, 0), 0.0) * conj
        yield
        y = y + _dot_nt(qc.astype(BF16), prev_conj.astype(BF16))
        yt[:, chans, :] = y.reshape(L, SSM_GROUP, n_chunks)

    def group_batch(k, carry):
        running = [group(S5_GROUPS_PER_TRIP * k + j) for j in range(S5_GROUPS_PER_TRIP)]
        finished = object()
        while running:
            running = [gen for gen in running if next(gen, finished) is not finished]
        return carry

    lax.fori_loop(0, LANES // SSM_GROUP // S5_GROUPS_PER_TRIP, group_batch, 0)
    by_chunk = jnp.swapaxes(jnp.stack([yt[tau].T for tau in range(L)], axis=0), 0, 1)
    y = by_chunk.reshape(n_chunks * L, LANES) + d_ref[...] * x_ref[...]
    y_ref[...] = jax.nn.gelu(y).astype(y_ref.dtype)


def _s5_mixer(h, hn, a_re, a_im, log_dt, b_re, b_im, c_re, c_im, d_skip, w_glu, layer):
    s = h.shape[0]
    L = S5_CHUNK
    n_chunks = s // L
    width = L * SSM_GROUP
    gpb = LANES // SSM_GROUP
    def packed(re, im):
        return jnp.stack([jnp.concatenate([re, im], -1), jnp.concatenate([-im, re], -1)], axis=1).astype(F32)

    lam = jnp.stack([a_re, a_im, jnp.broadcast_to(log_dt[:, None], a_re.shape)], axis=1).astype(F32)
    lam = jnp.concatenate([lam, lam], axis=-1)
    bt = packed(b_re.transpose(0, 2, 1), b_im.transpose(0, 2, 1))
    cc = packed(c_re, c_im)
    dd = d_skip.astype(F32).reshape(1, D_MODEL)
    y = pl.pallas_call(
        functools.partial(_s5_kernel, chunk=L),
        out_shape=jax.ShapeDtypeStruct((s, D_MODEL), BF16),
        grid=(SSM_GROUPS // gpb,),
        in_specs=[
            pl.BlockSpec((s, LANES), lambda b: (0, b)),
            pl.BlockSpec((gpb, 3, 2 * SSM_STATE), lambda b: (b, 0, 0)),
            pl.BlockSpec((gpb, 2, SSM_GROUP, 2 * SSM_STATE), lambda b: (b, 0, 0, 0)),
            pl.BlockSpec((gpb, 2, SSM_GROUP, 2 * SSM_STATE), lambda b: (b, 0, 0, 0)),
            pl.BlockSpec((1, LANES), lambda b: (0, b)),
        ],
        out_specs=pl.BlockSpec((s, LANES), lambda b: (0, b)),
        scratch_shapes=[
            pltpu.VMEM((L, LANES, n_chunks), hn.dtype),
            pltpu.VMEM((L, LANES, n_chunks), F32),
            pltpu.VMEM((width, width), F32),
        ],
        compiler_params=_params("arbitrary"),
    )(hn, lam, bt, cc, dd)
    return _matmul(_ep_glu_resid, y, w_glu, layer, D_MODEL, F32, 1024, 512, b_halves=2, tile_extra=h)


def kernel(x, rel_bias, mix_norm_g, ffn_norm_g, final_norm_g, nsa_w_in, nsa_w_out, cmp_pos_k, cmp_w1_k, cmp_w2_k, cmp_pos_v, cmp_w1_v, cmp_w2_v, s5_A_re, s5_A_im, s5_log_dt, s5_B_re, s5_B_im, s5_C_re, s5_C_im, s5_D, s5_w_glu, ffn_w_in, ffn_w_out):
    assert x.shape[0] == 1
    h = x[0]
    hn = _rmsnorm(h, mix_norm_g[0], BF16)
    h, hn = _nsa_mixer(h, hn, rel_bias, nsa_w_in, nsa_w_out, 0, cmp_pos_k[0], cmp_w1_k[0], cmp_w2_k[0],
                       cmp_pos_v[0], cmp_w1_v[0], cmp_w2_v[0], ffn_norm_g[0])
    h = _swiglu_ffn(h, hn, ffn_w_in, ffn_w_out, 0)
    hn = _rmsnorm(h, mix_norm_g[1], BF16)
    h = _s5_mixer(h, hn, s5_A_re[0], s5_A_im[0], s5_log_dt[0], s5_B_re[0], s5_B_im[0], s5_C_re[0], s5_C_im[0],
                  s5_D[0], s5_w_glu, 0)
    h = _swiglu_ffn(h, _rmsnorm(h, ffn_norm_g[1], BF16), ffn_w_in, ffn_w_out, 1)
    return _rmsnorm(h, final_norm_g, x.dtype)[None]
```

```python
import functools
import math

import numpy as np
import jax
import jax.numpy as jnp
from jax import lax
from jax.experimental import pallas as pl
from jax.experimental.pallas import tpu as pltpu

D_MODEL = 2048
N_HEADS = 16
HEAD_DIM = 128
N_KV_GROUPS = 4
HPG = N_HEADS // N_KV_GROUPS
CMP_BLOCK = 32
CMP_STRIDE = 16
CMP_HIDDEN = 2 * HEAD_DIM
SEL_BLOCK = 64
SEL_TOPK = 16
N_FORCED = 3
WINDOW = 512
N_GATES = 3
KV_WIDTH = N_KV_GROUPS * HEAD_DIM
Q_WIDTH = N_HEADS * HEAD_DIM
PROJ_MAIN = Q_WIDTH + 6 * KV_WIDTH
REL_BUCKETS = 32
REL_MAX_DIST = 128
SSM_GROUP = 16
SSM_GROUPS = D_MODEL // SSM_GROUP
SSM_STATE = 64
D_FF = ((8 * D_MODEL + 2) // 3 + 255) // 256 * 256
RMS_EPS = 1e-6
NEG = -1e30
LOG2E = math.log2(math.e)

LANES = 128
SUBLANES = 8
VMEM_LIMIT = 52 * 1024 * 1024

ATT_TILE = WINDOW // 2
FAR_TILES = 4
Q_TILES = 2
CMP_TQ = 256
CMP_PAD = 8
S5_CHUNK = 32
S5_GROUPS_PER_TRIP = 4

BF16 = jnp.bfloat16
F32 = jnp.float32


def _dot(a, b):
    return jnp.dot(a, b, preferred_element_type=F32)


def _dot_nt(a, b):
    return lax.dot_general(a, b, (((1,), (1,)), ((), ())), preferred_element_type=F32)


def _params(*sem):
    return pltpu.CompilerParams(dimension_semantics=sem, vmem_limit_bytes=VMEM_LIMIT)


def _rmsnorm_kernel(x_ref, g_ref, o_ref):
    x = x_ref[...]
    ms = jnp.mean(x * x, axis=-1, keepdims=True)
    o_ref[...] = (x * lax.rsqrt(ms + RMS_EPS) * g_ref[...]).astype(o_ref.dtype)


def _rmsnorm(x, g, out_dtype, tm=512):
    s, d = x.shape
    return pl.pallas_call(
        _rmsnorm_kernel,
        out_shape=jax.ShapeDtypeStruct((s, d), out_dtype),
        grid=(s // tm,),
        in_specs=[pl.BlockSpec((tm, d), lambda i: (i, 0)), pl.BlockSpec((1, d), lambda i: (0, 0))],
        out_specs=pl.BlockSpec((tm, d), lambda i: (i, 0)),
        compiler_params=_params("parallel"),
    )(x, g.reshape(1, d))


def _mm_kernel(*refs, n_b, n_extra, epilogue, w_rows_are_outputs):
    a_ref = refs[0]
    b_refs = refs[1:1 + n_b]
    extra_refs = refs[1 + n_b:1 + n_b + n_extra]
    o_ref = refs[1 + n_b + n_extra]
    w_scr = refs[2 + n_b + n_extra:]

    @pl.when(pl.program_id(1) == 0)
    def _():
        for b_ref, w in zip(b_refs, w_scr):
            w[...] = b_ref[...].astype(BF16)

    a = a_ref[...]
    z = [(_dot_nt if w_rows_are_outputs else _dot)(a, w[...]) for w in w_scr]
    o_ref[...] = epilogue(*z, *[e[...] for e in extra_refs]).astype(o_ref.dtype)


def _ep_scale(z, scale):
    return z * scale


def _ep_resid(z, resid):
    return resid + z


def _ep_swiglu(za, zb):
    return jax.nn.silu(za) * zb


def _ep_glu_resid(za, zb, resid):
    return resid + za * jax.nn.sigmoid(zb)


def _matmul(epilogue, a, b, layer, n_out, out_dtype, tm, tn, *, b_halves=1, row_extra=None, tile_extra=None,
            w_rows_are_outputs=False):
    m, k = a.shape
    nj = n_out // tn
    estimate = (b_halves * k * tn * (2 * 4 + 2) + 2 * tm * k * 2 + 4 * tm * tn * 4 + b_halves * tm * tn * 4)
    b_mode = pl.Buffered(1) if estimate > VMEM_LIMIT else None
    in_specs = [pl.BlockSpec((tm, k), lambda j, i: (i, 0))]
    args = [a]
    for half in range(b_halves):
        if w_rows_are_outputs:
            spec = pl.BlockSpec((None, tn, k), functools.partial(lambda j, i, o: (layer, j + o, 0), o=half * nj),
                                pipeline_mode=b_mode)
        else:
            spec = pl.BlockSpec((None, k, tn), functools.partial(lambda j, i, o: (layer, 0, j + o), o=half * nj),
                                pipeline_mode=b_mode)
        in_specs.append(spec)
        args.append(b)
    extras = []
    if row_extra is not None:
        in_specs.append(pl.BlockSpec((1, tn), lambda j, i: (0, j)))
        extras.append(row_extra)
    if tile_extra is not None:
        in_specs.append(pl.BlockSpec((tm, tn), lambda j, i: (i, j)))
        extras.append(tile_extra)
    kern = functools.partial(_mm_kernel, n_b=b_halves, n_extra=len(extras), epilogue=epilogue,
                             w_rows_are_outputs=w_rows_are_outputs)
    return pl.pallas_call(
        kern,
        out_shape=jax.ShapeDtypeStruct((m, n_out), out_dtype),
        grid=(nj, m // tm),
        in_specs=in_specs,
        out_specs=pl.BlockSpec((tm, tn), lambda j, i: (i, j)),
        scratch_shapes=[pltpu.VMEM((tn, k) if w_rows_are_outputs else (k, tn), BF16) for _ in range(b_halves)],
        compiler_params=_params("arbitrary", "arbitrary"),
    )(*args, *extras)


def _mm_resid_norm_kernel(a_ref, b_ref, r_ref, g_ref, h_ref, hn_ref, w_scr):
    @pl.when(pl.program_id(0) == 0)
    def _():
        w_scr[...] = b_ref[...].astype(BF16)

    h = r_ref[...] + _dot(a_ref[...], w_scr[...])
    h_ref[...] = h
    ms = jnp.mean(h * h, axis=-1, keepdims=True)
    hn_ref[...] = (h * lax.rsqrt(ms + RMS_EPS) * g_ref[...]).astype(hn_ref.dtype)


def _matmul_resid_norm(a, b, layer, resid, g, tm):
    m, k = a.shape
    n = b.shape[2]
    return pl.pallas_call(
        _mm_resid_norm_kernel,
        out_shape=(jax.ShapeDtypeStruct((m, n), F32), jax.ShapeDtypeStruct((m, n), BF16)),
        grid=(m // tm,),
        in_specs=[
            pl.BlockSpec((tm, k), lambda i: (i, 0)),
            pl.BlockSpec((None, k, n), lambda i: (layer, 0, 0), pipeline_mode=pl.Buffered(1)),
            pl.BlockSpec((tm, n), lambda i: (i, 0)),
            pl.BlockSpec((1, n), lambda i: (0, 0)),
        ],
        out_specs=(pl.BlockSpec((tm, n), lambda i: (i, 0)), pl.BlockSpec((tm, n), lambda i: (i, 0))),
        scratch_shapes=[pltpu.VMEM((k, n), BF16)],
        compiler_params=_params("arbitrary"),
    )(a, b, resid, g.reshape(1, n))


def _swiglu_ffn(h, hn, w_in, w_out, layer):
    act = _matmul(_ep_swiglu, hn, w_in, layer, D_FF, BF16, 1024, 512, b_halves=2)
    return _matmul(_ep_resid, act, w_out, layer, D_MODEL, F32, 512, 512, tile_extra=h)


def _compress_kernel(x_ref, w1_ref, w2_ref, pos_ref, o_ref):
    ncp = x_ref.shape[0] // CMP_STRIDE
    top = jnp.zeros((ncp, CMP_HIDDEN), F32)
    bot = jnp.zeros((ncp, CMP_HIDDEN), F32)
    by_offset = jnp.swapaxes(x_ref[...].reshape(ncp, CMP_STRIDE, HEAD_DIM), 0, 1)
    for r in range(CMP_STRIDE):
        x_r = by_offset[r]
        top = top + _dot(x_r, w1_ref[0, r * HEAD_DIM:(r + 1) * HEAD_DIM, :])
        bot = bot + _dot(x_r, w1_ref[0, (CMP_STRIDE + r) * HEAD_DIM:(CMP_STRIDE + r + 1) * HEAD_DIM, :])
    posb = _dot(pos_ref[0], w1_ref[0])[0:1]
    hid = top + pltpu.roll(bot, ncp - 1, 0) + posb
    o_ref[0, 0] = _dot(jax.nn.gelu(hid).astype(BF16), w2_ref[0]).astype(o_ref.dtype)


def _compress(proj, w1, w2, pos):
    s = proj.shape[0]
    ncp = s // CMP_STRIDE
    kc_blk = Q_WIDTH // HEAD_DIM
    return pl.pallas_call(
        _compress_kernel,
        out_shape=jax.ShapeDtypeStruct((2, N_KV_GROUPS, ncp, HEAD_DIM), BF16),
        grid=(2, N_KV_GROUPS),
        in_specs=[
            pl.BlockSpec((s, HEAD_DIM), lambda kv, gi: (0, kc_blk + kv * N_KV_GROUPS + gi)),
            pl.BlockSpec((1, CMP_BLOCK * HEAD_DIM, CMP_HIDDEN), lambda kv, gi: (kv, 0, 0)),
            pl.BlockSpec((1, CMP_HIDDEN, HEAD_DIM), lambda kv, gi: (kv, 0, 0)),
            pl.BlockSpec((1, SUBLANES, CMP_BLOCK * HEAD_DIM), lambda kv, gi: (kv, 0, 0)),
        ],
        out_specs=pl.BlockSpec((1, 1, ncp, HEAD_DIM), lambda kv, gi: (kv, gi, 0, 0)),
        compiler_params=_params("parallel", "parallel"),
    )(proj, w1, w2, pos)


def _first_argmax(score):
    n_rows, n = score.shape
    sub = lax.broadcasted_iota(jnp.int32, (SUBLANES, n), 0)
    pairs = [(score[r:r + SUBLANES], sub + r) for r in range(0, n_rows, SUBLANES)]
    while len(pairs) > 2:
        merged = []
        for k in range(0, len(pairs) - 1, 2):
            (va, ia), (vb, ib) = pairs[k], pairs[k + 1]
            take = vb > va
            merged.append((jnp.where(take, vb, va), jnp.where(take, ib, ia)))
        if len(pairs) % 2:
            merged.append(pairs[-1])
        pairs = merged
    val = jnp.concatenate([v for v, _ in pairs], axis=0)
    idx = jnp.concatenate([i for _, i in pairs], axis=0)
    best = jnp.max(val, axis=0, keepdims=True)
    return jnp.min(jnp.where(val == best, idx, n_rows), axis=0, keepdims=True)


def _cmp_attn_kernel(q_ref, kcc_ref, vcct_ref, band_ref, mt_ref, oc_ref, nm_ref, s_scr, *, tq, nb):
    i = pl.program_id(1)
    band_start = pl.multiple_of(i * (tq // CMP_STRIDE), SUBLANES)
    n_rows = kcc_ref.shape[1]
    n_blk = mt_ref.shape[0]

    def body(n_r, n_b):
        kcc = kcc_ref[0, :n_r, :]
        vcct = vcct_ref[0, :, :n_r]
        rows = lax.broadcasted_iota(jnp.int32, (n_r, tq), 0)
        row_bias = jnp.where(rows >= CMP_PAD, jnp.where(rows < band_start + nb, 0.0, NEG), NEG)
        imp = jnp.zeros((n_r, tq), F32)

        def scores(h):
            qh = q_ref[:, h * HEAD_DIM:(h + 1) * HEAD_DIM]
            s_scr[h, :n_r, :] = _dot_nt(kcc, qh) + row_bias
            s_scr[h, pl.ds(band_start, nb), :] += band_ref[h]

        scores(0)
        probs = []
        for h in range(HPG):
            if h + 1 < HPG:
                scores(h + 1)
            s = s_scr[h, :n_r, :]
            m = jnp.maximum(jnp.max(s, axis=0, keepdims=True), 0.5 * NEG)
            e = jnp.exp2(s - m)
            l = jnp.sum(e, axis=0, keepdims=True)
            pn = e * (1.0 / jnp.where(l > 0.0, l, 1.0))
            imp = imp + pn
            probs.append(pn.astype(BF16))

        def outputs():
            for h in range(HPG):
                oct_h = _dot(vcct, probs[h])
                oc_ref[:, h * HEAD_DIM:(h + 1) * HEAD_DIM] = oct_h.T
                yield
        mt = mt_ref[:n_b, :n_r]
        hi = imp.astype(BF16)
        rest = imp - hi.astype(F32)
        mid = rest.astype(BF16)
        lo = (rest - mid.astype(F32)).astype(BF16)
        slc = _dot(mt, hi) + _dot(mt, mid) + _dot(mt, lo)
        jj = lax.broadcasted_iota(jnp.int32, (n_b, tq), 0)
        t_blk = (i * tq + lax.broadcasted_iota(jnp.int32, (n_b, tq), 1)) // SEL_BLOCK
        forced = jnp.where(jj == 0, 1, jnp.where(jj == t_blk, 1, jnp.where(jj == t_blk - 1, 1, 0)))
        picked = jnp.where(forced == 1, 1.0, 0.0)
        score = jnp.where(forced == 1, -jnp.inf, jnp.where(jj <= t_blk, slc, NEG))
        pending = outputs()
        for r in range(SEL_TOPK - N_FORCED):
            hit = jj == _first_argmax(score)
            picked = jnp.where(hit, 1.0, picked)
            score = jnp.where(hit, -jnp.inf, score)
            if r % 3 == 0:
                next(pending, None)
        for _ in pending:
            pass
        neg_mask = jnp.where(picked > 0.5, jnp.where(jj <= t_blk, 0.0, NEG), NEG)
        if n_b < n_blk:
            neg_mask = jnp.concatenate([neg_mask, jnp.full((n_blk - n_b, tq), NEG, F32)], axis=0)
        nm_ref[...] = neg_mask.T.astype(nm_ref.dtype)

    step = LANES
    extents = list(range(step, n_rows, step)) + [n_rows]
    need_rows = band_start + nb
    for v, n_r in enumerate(extents):
        lower = extents[v - 1] if v else 0
        n_b = min(n_blk, -(-(n_r * CMP_STRIDE // SEL_BLOCK) // SUBLANES) * SUBLANES)
        pl.when(jnp.logical_and(need_rows > lower, need_rows <= n_r))(functools.partial(body, n_r, n_b))


def _cmp_attn(proj, kcc, vcct, band, mt, tq):
    s = proj.shape[0]
    n_rows = kcc.shape[1]
    nb = band.shape[1]
    kern = functools.partial(_cmp_attn_kernel, tq=tq, nb=nb)
    return pl.pallas_call(
        kern,
        out_shape=(jax.ShapeDtypeStruct((s, Q_WIDTH), F32),
                   jax.ShapeDtypeStruct((s, N_KV_GROUPS * LANES), BF16)),
        grid=(N_KV_GROUPS, s // tq),
        in_specs=[
            pl.BlockSpec((tq, HPG * HEAD_DIM), lambda g, i: (i, g)),
            pl.BlockSpec((1, n_rows, HEAD_DIM), lambda g, i: (g, 0, 0)),
            pl.BlockSpec((1, HEAD_DIM, n_rows), lambda g, i: (g, 0, 0)),
            pl.BlockSpec((HPG, nb, tq), lambda g, i: (g, 0, 0)),
            pl.BlockSpec((LANES, n_rows), lambda g, i: (0, 0)),
        ],
        out_specs=(pl.BlockSpec((tq, HPG * HEAD_DIM), lambda g, i: (i, g)),
                   pl.BlockSpec((tq, LANES), lambda g, i: (i, g))),
        scratch_shapes=[pltpu.VMEM((HPG, n_rows, tq), F32)],
        compiler_params=_params("parallel", "parallel"),
    )(proj, kcc, vcct, band, mt)


def _lane_tile(x, width):
    return jnp.concatenate([x] * (width // x.shape[1]), axis=1)


def _softmax_update(s, v_aug, m_ref, acc_ref, rows):
    m_prev = m_ref[rows, :]
    m_next = jnp.maximum(m_prev, jnp.max(s, axis=1, keepdims=True))
    alpha = jnp.exp2(m_prev - m_next)
    p = jnp.exp2(s - _lane_tile(m_next, s.shape[1]))
    acc_ref[rows, :] = _lane_tile(alpha, acc_ref.shape[1]) * acc_ref[rows, :] + _dot(p.astype(BF16), v_aug)
    m_ref[rows, :] = m_next


def _softmax_once(s, v_aug):
    m = jnp.broadcast_to(jnp.max(s, axis=1, keepdims=True), (s.shape[0], LANES))
    p = jnp.exp2(s - _lane_tile(m, s.shape[1]))
    return _dot(p.astype(BF16), v_aug)


def _sel_win_kernel(q_ref, nm_ref, ks_ref, vs_ref, kw_ref, vw_ref, blk_ref, per_ref, g_ref, oc_ref, o_ref,
                    q4, m_s, a_s, bias_ref, *, tile):
    step = pl.program_id(1)
    t = tile
    first = step * Q_TILES

    @pl.when(step == 0)
    def _():
        tl = lax.broadcasted_iota(jnp.int32, (t, t), 0)
        kl = lax.broadcasted_iota(jnp.int32, (t, t), 1)
        edge = jnp.where(tl < kl, 0.0, NEG)
        for h in range(HPG):
            bias_ref[h, :, :t] = edge
            for which in range(2):
                rows = jnp.broadcast_to(per_ref[h, which:which + 1, :], (t, 2 * t))
                table = pltpu.roll(rows, 0, 1, stride=1, stride_axis=0)
                bias_ref[h, :, (which + 1) * t:(which + 2) * t] = table[:, :t]

    def stacked(qt, h):
        return slice((qt * HPG + h) * t, (qt * HPG + h + 1) * t)

    def tile_rows(qt):
        return slice(qt * t, (qt + 1) * t)

    for qt in range(Q_TILES):
        for h in range(HPG):
            q4[stacked(qt, h), :HEAD_DIM] = q_ref[tile_rows(qt), h * HEAD_DIM:(h + 1) * HEAD_DIM]
            q4[stacked(qt, h), HEAD_DIM:] = nm_ref[tile_rows(qt), :]
    m_s[...] = jnp.full(m_s.shape, -3e38, F32)
    a_s[...] = jnp.zeros(a_s.shape, F32)

    def sel_operands(row0, n_tiles):
        keys = pl.ds(row0, n_tiles * t)
        k_aug = jnp.concatenate([ks_ref[keys, :], blk_ref[keys, :]], axis=1)
        v_aug = jnp.concatenate([vs_ref[keys, :], jnp.ones((n_tiles * t, LANES), BF16)], axis=1)
        return k_aug, v_aug

    def sel_pass(qt, operands, n_biased):
        k_aug, v_aug = operands

        def score(h):
            s = _dot_nt(q4[stacked(qt, h), :], k_aug)
            plain = s.shape[1] - n_biased * t
            if n_biased:
                near = s[:, plain:] + bias_ref[h, :, (3 - n_biased) * t:]
                s = jnp.concatenate([s[:, :plain], near], axis=1) if plain else near
            return s

        def consume(h, s):
            _softmax_update(s, v_aug, m_s, a_s, stacked(qt, h))

        return score, consume

    def win_pass(qt, row0, n_tiles):
        keys = pl.ds(row0, n_tiles * t)
        v_aug = jnp.concatenate([vw_ref[keys, :], jnp.ones((n_tiles * t, LANES), BF16)], axis=1)

        def score(h):
            return _dot_nt(q4[stacked(qt, h), :HEAD_DIM], kw_ref[keys, :]) + bias_ref[h, :, (3 - n_tiles) * t:]

        def consume(h, s):
            acc_w = _softmax_once(s, v_aug)
            acc_s = a_s[stacked(qt, h), :]
            o_w = acc_w[:, :HEAD_DIM] * (1.0 / acc_w[:, HEAD_DIM:])
            o_s = acc_s[:, :HEAD_DIM] * (1.0 / acc_s[:, HEAD_DIM:])
            cols = slice(h * HEAD_DIM, (h + 1) * HEAD_DIM)
            gates = g_ref[tile_rows(qt), :]
            o = (gates[:, h:h + 1] * oc_ref[tile_rows(qt), cols] + gates[:, HPG + h:HPG + h + 1] * o_s
                 + gates[:, 2 * HPG + h:2 * HPG + h + 1] * o_w)
            o_ref[tile_rows(qt), cols] = o.astype(o_ref.dtype)

        return score, consume

    def run(passes):
        work = [(score, consume, h0) for score, consume in passes for h0 in range(0, HPG, 2)]
        ahead = [work[0][0](h) for h in (0, 1)]
        for n, (_, consume, h0) in enumerate(work):
            now = ahead
            if n + 1 < len(work):
                nxt_score, _, nxt_h0 = work[n + 1]
                ahead = [nxt_score(h) for h in (nxt_h0, nxt_h0 + 1)]
            for h, s in zip((h0, h0 + 1), now):
                consume(h, s)

    def far_passes(row0):
        operands = sel_operands(row0, FAR_TILES)
        return [sel_pass(qt, operands, 0) for qt in range(Q_TILES)]

    n_far = jnp.maximum(first - 1, 0)
    n_chunks = n_far // FAR_TILES
    chunk = FAR_TILES * t

    def far_pair(c, carry):
        row0 = pl.multiple_of(c * 2 * chunk, 2 * chunk)
        run(far_passes(row0) + far_passes(row0 + chunk))
        return carry

    lax.fori_loop(0, n_chunks // 2, far_pair, 0)

    @pl.when(n_chunks % 2 == 1)
    def _():
        run(far_passes(pl.multiple_of((n_chunks - 1) * chunk, chunk)))

    def tail(left):
        row0 = pl.multiple_of((n_far - left) * t, t)
        passes = []
        for qt in range(Q_TILES):
            n_sel = left + 2 + qt
            passes.append(sel_pass(qt, sel_operands(row0, n_sel), 2))
            passes.append(win_pass(qt, pl.multiple_of((first + qt - 2) * t, t), 3))
        run(passes)

    for left in range(FAR_TILES):
        if (left + 1) % Q_TILES == 0:
            pl.when(jnp.logical_and(step >= 1, n_far % FAR_TILES == left))(functools.partial(tail, left))

    @pl.when(step == 0)
    def _():
        passes = []
        for qt in range(Q_TILES):
            n_keys = min(qt + 1, 3)
            passes.append(sel_pass(qt, sel_operands(0, qt + 1), min(qt + 1, 2)))
            passes.append(win_pass(qt, (qt + 1 - n_keys) * t, n_keys))
        run(passes)


def _sel_win(proj, neg_mask, blk_onehot, periods, gates, o_c, tile):
    s = proj.shape[0]
    ks_blk = (Q_WIDTH + 2 * KV_WIDTH) // HEAD_DIM
    vs_blk = (Q_WIDTH + 3 * KV_WIDTH) // HEAD_DIM
    kw_blk = (Q_WIDTH + 4 * KV_WIDTH) // HEAD_DIM
    vw_blk = (Q_WIDTH + 5 * KV_WIDTH) // HEAD_DIM
    once = pl.Buffered(1)
    rows = Q_TILES * tile
    stacked_rows = HPG * rows
    kern = functools.partial(_sel_win_kernel, tile=tile)
    return pl.pallas_call(
        kern,
        out_shape=jax.ShapeDtypeStruct((s, Q_WIDTH), BF16),
        grid=(N_KV_GROUPS, s // rows),
        in_specs=[
            pl.BlockSpec((rows, HPG * HEAD_DIM), lambda g, i: (i, g)),
            pl.BlockSpec((rows, LANES), lambda g, i: (i, g)),
            pl.BlockSpec((s, HEAD_DIM), lambda g, i: (0, ks_blk + g), pipeline_mode=once),
            pl.BlockSpec((s, HEAD_DIM), lambda g, i: (0, vs_blk + g), pipeline_mode=once),
            pl.BlockSpec((s, HEAD_DIM), lambda g, i: (0, kw_blk + g), pipeline_mode=once),
            pl.BlockSpec((s, HEAD_DIM), lambda g, i: (0, vw_blk + g), pipeline_mode=once),
            pl.BlockSpec((s, LANES), lambda g, i: (0, 0), pipeline_mode=once),
            pl.BlockSpec((HPG, 2, 2 * tile), lambda g, i: (g, 0, 0)),
            pl.BlockSpec((rows, LANES), lambda g, i: (i, g)),
            pl.BlockSpec((rows, HPG * HEAD_DIM), lambda g, i: (i, g)),
        ],
        out_specs=pl.BlockSpec((rows, HPG * HEAD_DIM), lambda g, i: (i, g)),
        scratch_shapes=[
            pltpu.VMEM((stacked_rows, 2 * HEAD_DIM), BF16),
            pltpu.VMEM((stacked_rows, LANES), F32),
            pltpu.VMEM((stacked_rows, 2 * HEAD_DIM), F32),
            pltpu.VMEM((HPG, tile, 3 * tile), F32),
        ],
        compiler_params=_params("arbitrary", "arbitrary"),
    )(proj, neg_mask, proj, proj, proj, proj, blk_onehot, periods, gates, o_c)


def _rel_bucket_np(dist):
    n = np.maximum(dist, 0)
    max_exact = REL_BUCKETS // 2
    large = max_exact + (np.log(np.maximum(n, 1).astype(np.float32) / np.float32(max_exact))
                         / np.float32(math.log(REL_MAX_DIST / max_exact))
                         * np.float32(REL_BUCKETS - max_exact)).astype(np.int32)
    large = np.minimum(large, REL_BUCKETS - 1)
    return np.where(n < max_exact, n, large).astype(np.int32)


def _far_distance():
    d = np.arange(4 * REL_MAX_DIST)
    b = _rel_bucket_np(d)
    assert b[-1] == REL_BUCKETS - 1
    return int(np.max(np.nonzero(b != REL_BUCKETS - 1)[0])) + 1


def _bias_table(rel_bias, dist):
    bucket = jnp.asarray(_rel_bucket_np(dist))
    shifted = (rel_bias[bucket] - rel_bias[REL_BUCKETS - 1]) * LOG2E
    tab = jnp.where(jnp.asarray(dist >= 0)[..., None], shifted, NEG)
    return jnp.moveaxis(tab, -1, 0).astype(F32)


def _bias_period(rel_bias, offset, t):
    x = np.arange(2 * t)
    return _bias_table(rel_bias, np.where(x < t, offset - x, offset + 2 * t - x))


def _nsa_mixer(h, hn, rel_bias, w_in, w_out, layer, pos_k, w1_k, w2_k, pos_v, w1_v, w2_v, next_norm_g):
    s = h.shape[0]
    n_sel = s // SEL_BLOCK
    ncp = s // CMP_STRIDE
    tile = ATT_TILE
    far = _far_distance()
    assert n_sel <= LANES and s % CMP_TQ == 0 and s % (Q_TILES * tile) == 0 and FAR_TILES % Q_TILES == 0
    assert far <= tile and far <= CMP_STRIDE * CMP_PAD - CMP_BLOCK + 1 + CMP_STRIDE

    col_scale = jnp.where(jnp.arange(PROJ_MAIN) < Q_WIDTH, HEAD_DIM ** -0.5 * LOG2E, 1.0).astype(F32)[None, :]
    w_in_t = jnp.swapaxes(w_in, 1, 2)
    proj = _matmul(_ep_scale, hn, w_in_t, layer, PROJ_MAIN, BF16, 2048, 512, row_extra=col_scale,
                   w_rows_are_outputs=True)
    w_gate = w_in_t[layer, PROJ_MAIN:].reshape(N_GATES, N_KV_GROUPS, HPG, D_MODEL).transpose(1, 0, 2, 3)
    w_gate = w_gate.reshape(N_KV_GROUPS, N_GATES * HPG, D_MODEL)
    w_gate = jnp.pad(w_gate, ((0, 0), (0, LANES - N_GATES * HPG), (0, 0))).reshape(1, N_KV_GROUPS * LANES, D_MODEL)
    gates = _matmul(jax.nn.sigmoid, hn, w_gate, 0, N_KV_GROUPS * LANES, F32, 2048, 512, w_rows_are_outputs=True)

    w1 = jnp.stack([w1_k, w1_v]).astype(BF16)
    w2 = jnp.stack([w2_k, w2_v]).astype(BF16)
    pos = jnp.stack([pos_k.reshape(1, -1), pos_v.reshape(1, -1)]).astype(BF16)
    pos = jnp.broadcast_to(pos, (2, SUBLANES, CMP_BLOCK * HEAD_DIM))
    cc = _compress(proj, w1, w2, pos)
    kcc = jnp.pad(cc[0], ((0, 0), (CMP_PAD, 0), (0, 0)))
    vcct = jnp.pad(cc[1], ((0, 0), (CMP_PAD, 0), (0, 0))).transpose(0, 2, 1)
    n_rows = ncp + CMP_PAD

    nb = CMP_TQ // CMP_STRIDE + CMP_PAD
    first_dist = -CMP_STRIDE * (nb - 1 - CMP_PAD) - (CMP_BLOCK - 1)
    by_dist = _bias_table(rel_bias, first_dist + np.arange(CMP_TQ + CMP_STRIDE * (nb - 1)))
    band = jnp.stack([by_dist[:, CMP_STRIDE * (nb - 1 - b):CMP_STRIDE * (nb - 1 - b) + CMP_TQ] for b in range(nb)], axis=1)
    ratio = SEL_BLOCK // CMP_STRIDE
    lo = CMP_BLOCK // CMP_STRIDE - 1
    c_of_row = np.arange(n_rows)[None, :] - CMP_PAD
    j_of = np.arange(LANES)[:, None]
    mt = ((c_of_row >= ratio * j_of - lo) & (c_of_row <= ratio * j_of + ratio - 1)
          & (c_of_row >= 0) & (c_of_row <= ncp - 2) & (j_of < n_sel)).astype(np.float32)
    o_c, neg_mask = _cmp_attn(proj, kcc, vcct, band, jnp.asarray(mt, BF16), CMP_TQ)

    onehot = (np.arange(s)[:, None] // SEL_BLOCK == np.arange(LANES)[None, :]).astype(np.float32)
    periods = jnp.stack([_bias_period(rel_bias, tile, tile), _bias_period(rel_bias, 0, tile)], axis=1)
    o = _sel_win(proj, neg_mask, jnp.asarray(onehot, BF16), periods, gates, o_c, tile)
    return _matmul_resid_norm(o, w_out, layer, h, next_norm_g, 256)


def _s5_kernel(x_ref, lam_ref, bt_ref, c_ref, d_ref, y_ref, vt, yt, tmask, *, chunk):
    L = chunk
    half = L // 2
    width = L * SSM_GROUP
    n_chunks = x_ref.shape[0] // L
    by_step = jnp.swapaxes(x_ref[...].reshape(n_chunks, L, LANES), 0, 1)
    for tau in range(L):
        vt[tau] = by_step[tau].T

    @pl.when(pl.program_id(0) == 0)
    def _():
        dst = lax.broadcasted_iota(jnp.int32, (width, width), 0) // SSM_GROUP
        src = lax.broadcasted_iota(jnp.int32, (width, width), 1) // SSM_GROUP
        tmask[...] = jnp.where(src <= dst, 1.0, 0.0)

    def cmul(xr, xi, yr, yi):
        return xr * yr - xi * yi, xr * yi + xi * yr

    n2 = 2 * SSM_STATE
    sub = lax.broadcasted_iota(jnp.int32, (SUBLANES, 1), 0)
    consts = jnp.where(sub == 0, 1.0, jnp.where(sub == 1, float(half + 1), jnp.where(sub == 2, float(half - 1), float(L))))
    tau_col = lax.broadcasted_iota(jnp.int32, (L, 1), 0).astype(F32)
    row = lax.broadcasted_iota(jnp.int32, (n_chunks, n2), 0)
    conj = jnp.where(lax.broadcasted_iota(jnp.int32, (1, n2), 1) < SSM_STATE, 1.0, -1.0)

    def group(g):
        lam = lam_ref[g]
        a_re, a_im = lam[0:1], lam[1:2]
        dt = jnp.exp(lam[2:3])
        log_re, log_im = a_re * dt, a_im * dt

        def cpow(e):
            mag, cos, sin = jnp.exp(e * log_re), jnp.cos(e * log_im), jnp.sin(e * log_im)
            inv = 1.0 / mag
            return mag * cos, mag * sin, inv * cos, -(inv * sin)

        kr, ki, _, _ = cpow(consts)
        pr, pi = kr[0:1] - 1.0, ki[0:1]
        den = a_re * a_re + a_im * a_im
        cf_r = (pr * a_re + pi * a_im) / den
        cf_i = (pi * a_re - pr * a_im) / den
        bt_c, bt_s = bt_ref[g, 0], bt_ref[g, 1]
        bb_c = cf_r * bt_c + cf_i * bt_s
        bb_s = cf_r * bt_s - cf_i * bt_c
        c_c, c_s = c_ref[g, 0], c_ref[g, 1]

        e1r, e1i, e2r, e2i = cpow(tau_col - half)
        e3r, e3i = cmul(e1r, e1i, kr[1:2], ki[1:2])
        e4r, e4i = cmul(e2r, e2i, kr[2:3], ki[2:3])

        def outer(er, ei, w_c, w_s):
            return (er[:, None, :] * w_c[None, :, :] + ei[:, None, :] * w_s[None, :, :]).reshape(width, n2)

        qm = outer(e1r, e1i, c_c, c_s)
        km_conj = outer(e2r, e2i, bb_c * conj, bb_s * conj)
        qc = outer(e3r, e3i, c_c, c_s)
        wz = outer(e4r, e4i, bb_c, bb_s)

        chans = pl.ds(pl.multiple_of(g * SSM_GROUP, SSM_GROUP), SSM_GROUP)
        ub = vt[:, chans, :].reshape(width, n_chunks).astype(BF16)
        yield

        tt = _dot_nt(qm.astype(BF16), km_conj.astype(BF16))
        x = _dot(wz.T.astype(BF16), ub).T
        yield
        y = _dot((tt * tmask[...]).astype(BF16), ub)
        p_r, p_i = kr[3:4], ki[3:4]
        d = 1
        while d < n_chunks:
            s = jnp.where(row >= d, pltpu.roll(x, d, 0), 0.0)
            x = x + s * p_r - pltpu.roll(s, SSM_STATE, 1) * (p_i * conj)
            p_r, p_i = cmul(p_r, p_i, p_r, p_i)
            d *= 2
        prev_conj = jnp.where(row >= 1, pltpu.roll(x, 1, 0), 0.0) * conj
        yield
        y = y + _dot_nt(qc.astype(BF16), prev_conj.astype(BF16))
        yt[:, chans, :] = y.reshape(L, SSM_GROUP, n_chunks)

    def group_batch(k, carry):
        running = [group(S5_GROUPS_PER_TRIP * k + j) for j in range(S5_GROUPS_PER_TRIP)]
        finished = object()
        while running:
            running = [gen for gen in running if next(gen, finished) is not finished]
        return carry

    lax.fori_loop(0, LANES // SSM_GROUP // S5_GROUPS_PER_TRIP, group_batch, 0)
    by_chunk = jnp.swapaxes(jnp.stack([yt[tau].T for tau in range(L)], axis=0), 0, 1)
    y = by_chunk.reshape(n_chunks * L, LANES) + d_ref[...] * x_ref[...]
    y_ref[...] = jax.nn.gelu(y).astype(y_ref.dtype)


def _s5_mixer(h, hn, a_re, a_im, log_dt, b_re, b_im, c_re, c_im, d_skip, w_glu, layer):
    s = h.shape[0]
    L = S5_CHUNK
    n_chunks = s // L
    width = L * SSM_GROUP
    gpb = LANES // SSM_GROUP
    def packed(re, im):
        return jnp.stack([jnp.concatenate([re, im], -1), jnp.concatenate([-im, re], -1)], axis=1).astype(F32)

    lam = jnp.stack([a_re, a_im, jnp.broadcast_to(log_dt[:, None], a_re.shape)], axis=1).astype(F32)
    lam = jnp.concatenate([lam, lam], axis=-1)
    bt = packed(b_re.transpose(0, 2, 1), b_im.transpose(0, 2, 1))
    cc = packed(c_re, c_im)
    dd = d_skip.astype(F32).reshape(1, D_MODEL)
    y = pl.pallas_call(
        functools.partial(_s5_kernel, chunk=L),
        out_shape=jax.ShapeDtypeStruct((s, D_MODEL), BF16),
        grid=(SSM_GROUPS // gpb,),
        in_specs=[
            pl.BlockSpec((s, LANES), lambda b: (0, b)),
            pl.BlockSpec((gpb, 3, 2 * SSM_STATE), lambda b: (b, 0, 0)),
            pl.BlockSpec((gpb, 2, SSM_GROUP, 2 * SSM_STATE), lambda b: (b, 0, 0, 0)),
            pl.BlockSpec((gpb, 2, SSM_GROUP, 2 * SSM_STATE), lambda b: (b, 0, 0, 0)),
            pl.BlockSpec((1, LANES), lambda b: (0, b)),
        ],
        out_specs=pl.BlockSpec((s, LANES), lambda b: (0, b)),
        scratch_shapes=[
            pltpu.VMEM((L, LANES, n_chunks), hn.dtype),
            pltpu.VMEM((L, LANES, n_chunks), F32),
            pltpu.VMEM((width, width), F32),
        ],
        compiler_params=_params("arbitrary"),
    )(hn, lam, bt, cc, dd)
    return _matmul(_ep_glu_resid, y, w_glu, layer, D_MODEL, F32, 1024, 512, b_halves=2, tile_extra=h)


def kernel(x, rel_bias, mix_norm_g, ffn_norm_g, final_norm_g, nsa_w_in, nsa_w_out, cmp_pos_k, cmp_w1_k, cmp_w2_k, cmp_pos_v, cmp_w1_v, cmp_w2_v, s5_A_re, s5_A_im, s5_log_dt, s5_B_re, s5_B_im, s5_C_re, s5_C_im, s5_D, s5_w_glu, ffn_w_in, ffn_w_out):
    assert x.shape[0] == 1
    h = x[0]
    hn = _rmsnorm(h, mix_norm_g[0], BF16)
    h, hn = _nsa_mixer(h, hn, rel_bias, nsa_w_in, nsa_w_out, 0, cmp_pos_k[0], cmp_w1_k[0], cmp_w2_k[0],
                       cmp_pos_v[0], cmp_w1_v[0], cmp_w2_v[0], ffn_norm_g[0])
    h = _swiglu_ffn(h, hn, ffn_w_in, ffn_w_out, 0)
    hn = _rmsnorm(h, mix_norm_g[1], BF16)
    h = _s5_mixer(h, hn, s5_A_re[0], s5_A_im[0], s5_log_dt[0], s5_B_re[0], s5_B_im[0], s5_C_re[0], s5_C_im[0],
                  s5_D[0], s5_w_glu, 0)
    h = _swiglu_ffn(h, _rmsnorm(h, ffn_norm_g[1], BF16), ffn_w_in, ffn_w_out, 1)
    return _rmsnorm(h, final_norm_g, x.dtype)[None]
```

```python
import functools
import math

import numpy as np
import jax
import jax.numpy as jnp
from jax import lax
from jax.experimental import pallas as pl
from jax.experimental.pallas import tpu as pltpu

D_MODEL = 2048
N_HEADS = 16
HEAD_DIM = 128
N_KV_GROUPS = 4
HPG = N_HEADS // N_KV_GROUPS
CMP_BLOCK = 32
CMP_STRIDE = 16
CMP_HIDDEN = 2 * HEAD_DIM
SEL_BLOCK = 64
SEL_TOPK = 16
N_FORCED = 3
WINDOW = 512
N_GATES = 3
KV_WIDTH = N_KV_GROUPS * HEAD_DIM
Q_WIDTH = N_HEADS * HEAD_DIM
PROJ_MAIN = Q_WIDTH + 6 * KV_WIDTH
REL_BUCKETS = 32
REL_MAX_DIST = 128
SSM_GROUP = 16
SSM_GROUPS = D_MODEL // SSM_GROUP
SSM_STATE = 64
D_FF = ((8 * D_MODEL + 2) // 3 + 255) // 256 * 256
RMS_EPS = 1e-6
NEG = -1e30
LOG2E = math.log2(math.e)

LANES = 128
SUBLANES = 8
VMEM_LIMIT = 52 * 1024 * 1024

ATT_TILE = WINDOW // 2
FAR_TILES = 4
Q_TILES = 2
CMP_TQ = 256
CMP_PAD = 8
S5_CHUNK = 32
S5_GROUPS_PER_TRIP = 4

BF16 = jnp.bfloat16
F32 = jnp.float32


def _dot(a, b):
    return jnp.dot(a, b, preferred_element_type=F32)


def _dot_nt(a, b):
    return lax.dot_general(a, b, (((1,), (1,)), ((), ())), preferred_element_type=F32)


def _params(*sem):
    return pltpu.CompilerParams(dimension_semantics=sem, vmem_limit_bytes=VMEM_LIMIT)


def _rmsnorm_kernel(x_ref, g_ref, o_ref):
    x = x_ref[...]
    ms = jnp.mean(x * x, axis=-1, keepdims=True)
    o_ref[...] = (x * lax.rsqrt(ms + RMS_EPS) * g_ref[...]).astype(o_ref.dtype)


def _rmsnorm(x, g, out_dtype, tm=1024):
    s, d = x.shape
    return pl.pallas_call(
        _rmsnorm_kernel,
        out_shape=jax.ShapeDtypeStruct((s, d), out_dtype),
        grid=(s // tm,),
        in_specs=[pl.BlockSpec((tm, d), lambda i: (i, 0)), pl.BlockSpec((1, d), lambda i: (0, 0))],
        out_specs=pl.BlockSpec((tm, d), lambda i: (i, 0)),
        compiler_params=_params("parallel"),
    )(x, g.reshape(1, d))


def _mm_kernel(*refs, n_b, n_extra, epilogue, w_rows_are_outputs):
    a_ref = refs[0]
    b_refs = refs[1:1 + n_b]
    extra_refs = refs[1 + n_b:1 + n_b + n_extra]
    o_ref = refs[1 + n_b + n_extra]
    w_scr = refs[2 + n_b + n_extra:]

    @pl.when(pl.program_id(1) == 0)
    def _():
        for b_ref, w in zip(b_refs, w_scr):
            w[...] = b_ref[...].astype(BF16)

    a = a_ref[...]
    z = [(_dot_nt if w_rows_are_outputs else _dot)(a, w[...]) for w in w_scr]
    o_ref[...] = epilogue(*z, *[e[...] for e in extra_refs]).astype(o_ref.dtype)


def _ep_scale(z, scale):
    return z * scale


def _ep_resid(z, resid):
    return resid + z


def _ep_swiglu(za, zb):
    return jax.nn.silu(za) * zb


def _ep_glu_resid(za, zb, resid):
    return resid + za * jax.nn.sigmoid(zb)


def _matmul(epilogue, a, b, layer, n_out, out_dtype, tm, tn, *, b_halves=1, row_extra=None, tile_extra=None,
            w_rows_are_outputs=False):
    m, k = a.shape
    nj = n_out // tn
    estimate = (b_halves * k * tn * (2 * 4 + 2) + 2 * tm * k * 2 + 4 * tm * tn * 4 + b_halves * tm * tn * 4)
    b_mode = pl.Buffered(1) if estimate > VMEM_LIMIT else None
    in_specs = [pl.BlockSpec((tm, k), lambda j, i: (i, 0))]
    args = [a]
    for half in range(b_halves):
        if w_rows_are_outputs:
            spec = pl.BlockSpec((None, tn, k), functools.partial(lambda j, i, o: (layer, j + o, 0), o=half * nj),
                                pipeline_mode=b_mode)
        else:
            spec = pl.BlockSpec((None, k, tn), functools.partial(lambda j, i, o: (layer, 0, j + o), o=half * nj),
                                pipeline_mode=b_mode)
        in_specs.append(spec)
        args.append(b)
    extras = []
    if row_extra is not None:
        in_specs.append(pl.BlockSpec((1, tn), lambda j, i: (0, j)))
        extras.append(row_extra)
    if tile_extra is not None:
        in_specs.append(pl.BlockSpec((tm, tn), lambda j, i: (i, j)))
        extras.append(tile_extra)
    kern = functools.partial(_mm_kernel, n_b=b_halves, n_extra=len(extras), epilogue=epilogue,
                             w_rows_are_outputs=w_rows_are_outputs)
    return pl.pallas_call(
        kern,
        out_shape=jax.ShapeDtypeStruct((m, n_out), out_dtype),
        grid=(nj, m // tm),
        in_specs=in_specs,
        out_specs=pl.BlockSpec((tm, tn), lambda j, i: (i, j)),
        scratch_shapes=[pltpu.VMEM((tn, k) if w_rows_are_outputs else (k, tn), BF16) for _ in range(b_halves)],
        compiler_params=_params("arbitrary", "arbitrary"),
    )(*args, *extras)


def _mm_resid_norm_kernel(a_ref, b_ref, r_ref, g_ref, h_ref, hn_ref, w_scr):
    @pl.when(pl.program_id(0) == 0)
    def _():
        w_scr[...] = b_ref[...].astype(BF16)

    h = r_ref[...] + _dot(a_ref[...], w_scr[...])
    h_ref[...] = h
    ms = jnp.mean(h * h, axis=-1, keepdims=True)
    hn_ref[...] = (h * lax.rsqrt(ms + RMS_EPS) * g_ref[...]).astype(hn_ref.dtype)


def _matmul_resid_norm(a, b, layer, resid, g, tm):
    m, k = a.shape
    n = b.shape[2]
    return pl.pallas_call(
        _mm_resid_norm_kernel,
        out_shape=(jax.ShapeDtypeStruct((m, n), F32), jax.ShapeDtypeStruct((m, n), BF16)),
        grid=(m // tm,),
        in_specs=[
            pl.BlockSpec((tm, k), lambda i: (i, 0)),
            pl.BlockSpec((None, k, n), lambda i: (layer, 0, 0), pipeline_mode=pl.Buffered(1)),
            pl.BlockSpec((tm, n), lambda i: (i, 0)),
            pl.BlockSpec((1, n), lambda i: (0, 0)),
        ],
        out_specs=(pl.BlockSpec((tm, n), lambda i: (i, 0)), pl.BlockSpec((tm, n), lambda i: (i, 0))),
        scratch_shapes=[pltpu.VMEM((k, n), BF16)],
        compiler_params=_params("arbitrary"),
    )(a, b, resid, g.reshape(1, n))


def _swiglu_ffn(h, hn, w_in, w_out, layer):
    act = _matmul(_ep_swiglu, hn, w_in, layer, D_FF, BF16, 1024, 512, b_halves=2)
    return _matmul(_ep_resid, act, w_out, layer, D_MODEL, F32, 512, 512, tile_extra=h)


def _compress_kernel(x_ref, w1_ref, w2_ref, pos_ref, o_ref):
    ncp = x_ref.shape[0] // CMP_STRIDE
    top = jnp.zeros((ncp, CMP_HIDDEN), F32)
    bot = jnp.zeros((ncp, CMP_HIDDEN), F32)
    by_offset = jnp.swapaxes(x_ref[...].reshape(ncp, CMP_STRIDE, HEAD_DIM), 0, 1)
    for r in range(CMP_STRIDE):
        x_r = by_offset[r]
        top = top + _dot(x_r, w1_ref[0, r * HEAD_DIM:(r + 1) * HEAD_DIM, :])
        bot = bot + _dot(x_r, w1_ref[0, (CMP_STRIDE + r) * HEAD_DIM:(CMP_STRIDE + r + 1) * HEAD_DIM, :])
    posb = _dot(pos_ref[0], w1_ref[0])[0:1]
    hid = top + pltpu.roll(bot, ncp - 1, 0) + posb
    o_ref[0, 0] = _dot(jax.nn.gelu(hid).astype(BF16), w2_ref[0]).astype(o_ref.dtype)


def _compress(proj, w1, w2, pos):
    s = proj.shape[0]
    ncp = s // CMP_STRIDE
    kc_blk = Q_WIDTH // HEAD_DIM
    return pl.pallas_call(
        _compress_kernel,
        out_shape=jax.ShapeDtypeStruct((2, N_KV_GROUPS, ncp, HEAD_DIM), BF16),
        grid=(2, N_KV_GROUPS),
        in_specs=[
            pl.BlockSpec((s, HEAD_DIM), lambda kv, gi: (0, kc_blk + kv * N_KV_GROUPS + gi)),
            pl.BlockSpec((1, CMP_BLOCK * HEAD_DIM, CMP_HIDDEN), lambda kv, gi: (kv, 0, 0)),
            pl.BlockSpec((1, CMP_HIDDEN, HEAD_DIM), lambda kv, gi: (kv, 0, 0)),
            pl.BlockSpec((1, SUBLANES, CMP_BLOCK * HEAD_DIM), lambda kv, gi: (kv, 0, 0)),
        ],
        out_specs=pl.BlockSpec((1, 1, ncp, HEAD_DIM), lambda kv, gi: (kv, gi, 0, 0)),
        compiler_params=_params("parallel", "parallel"),
    )(proj, w1, w2, pos)


def _first_argmax(score):
    n_rows, n = score.shape
    sub = lax.broadcasted_iota(jnp.int32, (SUBLANES, n), 0)
    pairs = [(score[r:r + SUBLANES], sub + r) for r in range(0, n_rows, SUBLANES)]
    while len(pairs) > 2:
        merged = []
        for k in range(0, len(pairs) - 1, 2):
            (va, ia), (vb, ib) = pairs[k], pairs[k + 1]
            take = vb > va
            merged.append((jnp.where(take, vb, va), jnp.where(take, ib, ia)))
        if len(pairs) % 2:
            merged.append(pairs[-1])
        pairs = merged
    val = jnp.concatenate([v for v, _ in pairs], axis=0)
    idx = jnp.concatenate([i for _, i in pairs], axis=0)
    best = jnp.max(val, axis=0, keepdims=True)
    return jnp.min(jnp.where(val == best, idx, n_rows), axis=0, keepdims=True)


def _cmp_attn_kernel(q_ref, kcc_ref, vcct_ref, band_ref, mt_ref, oc_ref, nm_ref, s_scr, *, tq, nb):
    i = pl.program_id(1)
    band_start = pl.multiple_of(i * (tq // CMP_STRIDE), SUBLANES)
    n_rows = kcc_ref.shape[1]
    n_blk = mt_ref.shape[0]

    def body(n_r, n_b):
        kcc = kcc_ref[0, :n_r, :]
        vcct = vcct_ref[0, :, :n_r]
        rows = lax.broadcasted_iota(jnp.int32, (n_r, tq), 0)
        row_bias = jnp.where(rows >= CMP_PAD, jnp.where(rows < band_start + nb, 0.0, NEG), NEG)
        imp = jnp.zeros((n_r, tq), F32)

        def scores(h):
            qh = q_ref[:, h * HEAD_DIM:(h + 1) * HEAD_DIM]
            s_scr[h, :n_r, :] = _dot_nt(kcc, qh) + row_bias
            s_scr[h, pl.ds(band_start, nb), :] += band_ref[h]

        scores(0)
        probs = []
        for h in range(HPG):
            if h + 1 < HPG:
                scores(h + 1)
            s = s_scr[h, :n_r, :]
            m = jnp.maximum(jnp.max(s, axis=0, keepdims=True), 0.5 * NEG)
            e = jnp.exp2(s - m)
            l = jnp.sum(e, axis=0, keepdims=True)
            pn = e * (1.0 / jnp.where(l > 0.0, l, 1.0))
            imp = imp + pn
            probs.append(pn.astype(BF16))

        def outputs():
            for h in range(HPG):
                oct_h = _dot(vcct, probs[h])
                oc_ref[:, h * HEAD_DIM:(h + 1) * HEAD_DIM] = oct_h.T
                yield
        mt = mt_ref[:n_b, :n_r]
        hi = imp.astype(BF16)
        rest = imp - hi.astype(F32)
        mid = rest.astype(BF16)
        lo = (rest - mid.astype(F32)).astype(BF16)
        slc = _dot(mt, hi) + _dot(mt, mid) + _dot(mt, lo)
        jj = lax.broadcasted_iota(jnp.int32, (n_b, tq), 0)
        t_blk = (i * tq + lax.broadcasted_iota(jnp.int32, (n_b, tq), 1)) // SEL_BLOCK
        forced = jnp.where(jj == 0, 1, jnp.where(jj == t_blk, 1, jnp.where(jj == t_blk - 1, 1, 0)))
        picked = jnp.where(forced == 1, 1.0, 0.0)
        score = jnp.where(forced == 1, -jnp.inf, jnp.where(jj <= t_blk, slc, NEG))
        pending = outputs()
        for r in range(SEL_TOPK - N_FORCED):
            hit = jj == _first_argmax(score)
            picked = jnp.where(hit, 1.0, picked)
            score = jnp.where(hit, -jnp.inf, score)
            if r % 3 == 0:
                next(pending, None)
        for _ in pending:
            pass
        neg_mask = jnp.where(picked > 0.5, jnp.where(jj <= t_blk, 0.0, NEG), NEG)
        if n_b < n_blk:
            neg_mask = jnp.concatenate([neg_mask, jnp.full((n_blk - n_b, tq), NEG, F32)], axis=0)
        nm_ref[...] = neg_mask.T.astype(nm_ref.dtype)

    step = LANES
    extents = list(range(step, n_rows, step)) + [n_rows]
    need_rows = band_start + nb
    for v, n_r in enumerate(extents):
        lower = extents[v - 1] if v else 0
        n_b = min(n_blk, -(-(n_r * CMP_STRIDE // SEL_BLOCK) // SUBLANES) * SUBLANES)
        pl.when(jnp.logical_and(need_rows > lower, need_rows <= n_r))(functools.partial(body, n_r, n_b))


def _cmp_attn(proj, kcc, vcct, band, mt, tq):
    s = proj.shape[0]
    n_rows = kcc.shape[1]
    nb = band.shape[1]
    kern = functools.partial(_cmp_attn_kernel, tq=tq, nb=nb)
    return pl.pallas_call(
        kern,
        out_shape=(jax.ShapeDtypeStruct((s, Q_WIDTH), F32),
                   jax.ShapeDtypeStruct((s, N_KV_GROUPS * LANES), BF16)),
        grid=(N_KV_GROUPS, s // tq),
        in_specs=[
            pl.BlockSpec((tq, HPG * HEAD_DIM), lambda g, i: (i, g)),
            pl.BlockSpec((1, n_rows, HEAD_DIM), lambda g, i: (g, 0, 0)),
            pl.BlockSpec((1, HEAD_DIM, n_rows), lambda g, i: (g, 0, 0)),
            pl.BlockSpec((HPG, nb, tq), lambda g, i: (g, 0, 0)),
            pl.BlockSpec((LANES, n_rows), lambda g, i: (0, 0)),
        ],
        out_specs=(pl.BlockSpec((tq, HPG * HEAD_DIM), lambda g, i: (i, g)),
                   pl.BlockSpec((tq, LANES), lambda g, i: (i, g))),
        scratch_shapes=[pltpu.VMEM((HPG, n_rows, tq), F32)],
        compiler_params=_params("parallel", "parallel"),
    )(proj, kcc, vcct, band, mt)


def _lane_tile(x, width):
    return jnp.concatenate([x] * (width // x.shape[1]), axis=1)


def _softmax_update(s, v_aug, m_ref, acc_ref, rows):
    m_prev = m_ref[rows, :]
    m_next = jnp.maximum(m_prev, jnp.max(s, axis=1, keepdims=True))
    alpha = jnp.exp2(m_prev - m_next)
    p = jnp.exp2(s - _lane_tile(m_next, s.shape[1]))
    acc_ref[rows, :] = _lane_tile(alpha, acc_ref.shape[1]) * acc_ref[rows, :] + _dot(p.astype(BF16), v_aug)
    m_ref[rows, :] = m_next


def _softmax_once(s, v_aug):
    m = jnp.broadcast_to(jnp.max(s, axis=1, keepdims=True), (s.shape[0], LANES))
    p = jnp.exp2(s - _lane_tile(m, s.shape[1]))
    return _dot(p.astype(BF16), v_aug)


def _sel_win_kernel(q_ref, nm_ref, ks_ref, vs_ref, kw_ref, vw_ref, blk_ref, per_ref, g_ref, oc_ref, o_ref,
                    q4, m_s, a_s, bias_ref, *, tile):
    step = pl.program_id(1)
    t = tile
    first = step * Q_TILES

    @pl.when(step == 0)
    def _():
        tl = lax.broadcasted_iota(jnp.int32, (t, t), 0)
        kl = lax.broadcasted_iota(jnp.int32, (t, t), 1)
        edge = jnp.where(tl < kl, 0.0, NEG)
        for h in range(HPG):
            bias_ref[h, :, :t] = edge
            for which in range(2):
                rows = jnp.broadcast_to(per_ref[h, which:which + 1, :], (t, 2 * t))
                table = pltpu.roll(rows, 0, 1, stride=1, stride_axis=0)
                bias_ref[h, :, (which + 1) * t:(which + 2) * t] = table[:, :t]

    def stacked(qt, h):
        return slice((qt * HPG + h) * t, (qt * HPG + h + 1) * t)

    def tile_rows(qt):
        return slice(qt * t, (qt + 1) * t)

    for qt in range(Q_TILES):
        for h in range(HPG):
            q4[stacked(qt, h), :HEAD_DIM] = q_ref[tile_rows(qt), h * HEAD_DIM:(h + 1) * HEAD_DIM]
            q4[stacked(qt, h), HEAD_DIM:] = nm_ref[tile_rows(qt), :]
    m_s[...] = jnp.full(m_s.shape, -3e38, F32)
    a_s[...] = jnp.zeros(a_s.shape, F32)

    def sel_operands(row0, n_tiles):
        keys = pl.ds(row0, n_tiles * t)
        k_aug = jnp.concatenate([ks_ref[keys, :], blk_ref[keys, :]], axis=1)
        v_aug = jnp.concatenate([vs_ref[keys, :], jnp.ones((n_tiles * t, LANES), BF16)], axis=1)
        return k_aug, v_aug

    def sel_pass(qt, operands, n_biased):
        k_aug, v_aug = operands

        def score(h):
            s = _dot_nt(q4[stacked(qt, h), :], k_aug)
            plain = s.shape[1] - n_biased * t
            if n_biased:
                near = s[:, plain:] + bias_ref[h, :, (3 - n_biased) * t:]
                s = jnp.concatenate([s[:, :plain], near], axis=1) if plain else near
            return s

        def consume(h, s):
            _softmax_update(s, v_aug, m_s, a_s, stacked(qt, h))

        return score, consume

    def win_pass(qt, row0, n_tiles):
        keys = pl.ds(row0, n_tiles * t)
        v_aug = jnp.concatenate([vw_ref[keys, :], jnp.ones((n_tiles * t, LANES), BF16)], axis=1)

        def score(h):
            return _dot_nt(q4[stacked(qt, h), :HEAD_DIM], kw_ref[keys, :]) + bias_ref[h, :, (3 - n_tiles) * t:]

        def consume(h, s):
            acc_w = _softmax_once(s, v_aug)
            acc_s = a_s[stacked(qt, h), :]
            o_w = acc_w[:, :HEAD_DIM] * (1.0 / acc_w[:, HEAD_DIM:])
            o_s = acc_s[:, :HEAD_DIM] * (1.0 / acc_s[:, HEAD_DIM:])
            cols = slice(h * HEAD_DIM, (h + 1) * HEAD_DIM)
            gates = g_ref[tile_rows(qt), :]
            o = (gates[:, h:h + 1] * oc_ref[tile_rows(qt), cols] + gates[:, HPG + h:HPG + h + 1] * o_s
                 + gates[:, 2 * HPG + h:2 * HPG + h + 1] * o_w)
            o_ref[tile_rows(qt), cols] = o.astype(o_ref.dtype)

        return score, consume

    def run(passes):
        work = [(score, consume, h0) for score, consume in passes for h0 in range(0, HPG, 2)]
        ahead = [work[0][0](h) for h in (0, 1)]
        for n, (_, consume, h0) in enumerate(work):
            now = ahead
            if n + 1 < len(work):
                nxt_score, _, nxt_h0 = work[n + 1]
                ahead = [nxt_score(h) for h in (nxt_h0, nxt_h0 + 1)]
            for h, s in zip((h0, h0 + 1), now):
                consume(h, s)

    def far_passes(row0):
        operands = sel_operands(row0, FAR_TILES)
        return [sel_pass(qt, operands, 0) for qt in range(Q_TILES)]

    n_far = jnp.maximum(first - 1, 0)
    n_chunks = n_far // FAR_TILES
    chunk = FAR_TILES * t

    def far_pair(c, carry):
        row0 = pl.multiple_of(c * 2 * chunk, 2 * chunk)
        run(far_passes(row0) + far_passes(row0 + chunk))
        return carry

    lax.fori_loop(0, n_chunks // 2, far_pair, 0)

    @pl.when(n_chunks % 2 == 1)
    def _():
        run(far_passes(pl.multiple_of((n_chunks - 1) * chunk, chunk)))

    def tail(left):
        row0 = pl.multiple_of((n_far - left) * t, t)
        passes = []
        for qt in range(Q_TILES):
            n_sel = left + 2 + qt
            passes.append(sel_pass(qt, sel_operands(row0, n_sel), 2))
            passes.append(win_pass(qt, pl.multiple_of((first + qt - 2) * t, t), 3))
        run(passes)

    for left in range(FAR_TILES):
        if (left + 1) % Q_TILES == 0:
            pl.when(jnp.logical_and(step >= 1, n_far % FAR_TILES == left))(functools.partial(tail, left))

    @pl.when(step == 0)
    def _():
        passes = []
        for qt in range(Q_TILES):
            n_keys = min(qt + 1, 3)
            passes.append(sel_pass(qt, sel_operands(0, qt + 1), min(qt + 1, 2)))
            passes.append(win_pass(qt, (qt + 1 - n_keys) * t, n_keys))
        run(passes)


def _sel_win(proj, neg_mask, blk_onehot, periods, gates, o_c, tile):
    s = proj.shape[0]
    ks_blk = (Q_WIDTH + 2 * KV_WIDTH) // HEAD_DIM
    vs_blk = (Q_WIDTH + 3 * KV_WIDTH) // HEAD_DIM
    kw_blk = (Q_WIDTH + 4 * KV_WIDTH) // HEAD_DIM
    vw_blk = (Q_WIDTH + 5 * KV_WIDTH) // HEAD_DIM
    once = pl.Buffered(1)
    rows = Q_TILES * tile
    stacked_rows = HPG * rows
    kern = functools.partial(_sel_win_kernel, tile=tile)
    return pl.pallas_call(
        kern,
        out_shape=jax.ShapeDtypeStruct((s, Q_WIDTH), BF16),
        grid=(N_KV_GROUPS, s // rows),
        in_specs=[
            pl.BlockSpec((rows, HPG * HEAD_DIM), lambda g, i: (i, g)),
            pl.BlockSpec((rows, LANES), lambda g, i: (i, g)),
            pl.BlockSpec((s, HEAD_DIM), lambda g, i: (0, ks_blk + g), pipeline_mode=once),
            pl.BlockSpec((s, HEAD_DIM), lambda g, i: (0, vs_blk + g), pipeline_mode=once),
            pl.BlockSpec((s, HEAD_DIM), lambda g, i: (0, kw_blk + g), pipeline_mode=once),
            pl.BlockSpec((s, HEAD_DIM), lambda g, i: (0, vw_blk + g), pipeline_mode=once),
            pl.BlockSpec((s, LANES), lambda g, i: (0, 0), pipeline_mode=once),
            pl.BlockSpec((HPG, 2, 2 * tile), lambda g, i: (g, 0, 0)),
            pl.BlockSpec((rows, LANES), lambda g, i: (i, g)),
            pl.BlockSpec((rows, HPG * HEAD_DIM), lambda g, i: (i, g)),
        ],
        out_specs=pl.BlockSpec((rows, HPG * HEAD_DIM), lambda g, i: (i, g)),
        scratch_shapes=[
            pltpu.VMEM((stacked_rows, 2 * HEAD_DIM), BF16),
            pltpu.VMEM((stacked_rows, LANES), F32),
            pltpu.VMEM((stacked_rows, 2 * HEAD_DIM), F32),
            pltpu.VMEM((HPG, tile, 3 * tile), F32),
        ],
        compiler_params=_params("arbitrary", "arbitrary"),
    )(proj, neg_mask, proj, proj, proj, proj, blk_onehot, periods, gates, o_c)


def _rel_bucket_np(dist):
    n = np.maximum(dist, 0)
    max_exact = REL_BUCKETS // 2
    large = max_exact + (np.log(np.maximum(n, 1).astype(np.float32) / np.float32(max_exact))
                         / np.float32(math.log(REL_MAX_DIST / max_exact))
                         * np.float32(REL_BUCKETS - max_exact)).astype(np.int32)
    large = np.minimum(large, REL_BUCKETS - 1)
    return np.where(n < max_exact, n, large).astype(np.int32)


def _far_distance():
    d = np.arange(4 * REL_MAX_DIST)
    b = _rel_bucket_np(d)
    assert b[-1] == REL_BUCKETS - 1
    return int(np.max(np.nonzero(b != REL_BUCKETS - 1)[0])) + 1


def _bias_table(rel_bias, dist):
    bucket = jnp.asarray(_rel_bucket_np(dist))
    shifted = (rel_bias[bucket] - rel_bias[REL_BUCKETS - 1]) * LOG2E
    tab = jnp.where(jnp.asarray(dist >= 0)[..., None], shifted, NEG)
    return jnp.moveaxis(tab, -1, 0).astype(F32)


def _bias_period(rel_bias, offset, t):
    x = np.arange(2 * t)
    return _bias_table(rel_bias, np.where(x < t, offset - x, offset + 2 * t - x))


def _nsa_mixer(h, hn, rel_bias, w_in, w_out, layer, pos_k, w1_k, w2_k, pos_v, w1_v, w2_v, next_norm_g):
    s = h.shape[0]
    n_sel = s // SEL_BLOCK
    ncp = s // CMP_STRIDE
    tile = ATT_TILE
    far = _far_distance()
    assert n_sel <= LANES and s % CMP_TQ == 0 and s % (Q_TILES * tile) == 0 and FAR_TILES % Q_TILES == 0
    assert far <= tile and far <= CMP_STRIDE * CMP_PAD - CMP_BLOCK + 1 + CMP_STRIDE

    col_scale = jnp.where(jnp.arange(PROJ_MAIN) < Q_WIDTH, HEAD_DIM ** -0.5 * LOG2E, 1.0).astype(F32)[None, :]
    w_in_t = jnp.swapaxes(w_in, 1, 2)
    proj = _matmul(_ep_scale, hn, w_in_t, layer, PROJ_MAIN, BF16, 2048, 512, row_extra=col_scale,
                   w_rows_are_outputs=True)
    w_gate = w_in_t[layer, PROJ_MAIN:].reshape(N_GATES, N_KV_GROUPS, HPG, D_MODEL).transpose(1, 0, 2, 3)
    w_gate = w_gate.reshape(N_KV_GROUPS, N_GATES * HPG, D_MODEL)
    w_gate = jnp.pad(w_gate, ((0, 0), (0, LANES - N_GATES * HPG), (0, 0))).reshape(1, N_KV_GROUPS * LANES, D_MODEL)
    gates = _matmul(jax.nn.sigmoid, hn, w_gate, 0, N_KV_GROUPS * LANES, F32, 2048, 512, w_rows_are_outputs=True)

    w1 = jnp.stack([w1_k, w1_v]).astype(BF16)
    w2 = jnp.stack([w2_k, w2_v]).astype(BF16)
    pos = jnp.stack([pos_k.reshape(1, -1), pos_v.reshape(1, -1)]).astype(BF16)
    pos = jnp.broadcast_to(pos, (2, SUBLANES, CMP_BLOCK * HEAD_DIM))
    cc = _compress(proj, w1, w2, pos)
    kcc = jnp.pad(cc[0], ((0, 0), (CMP_PAD, 0), (0, 0)))
    vcct = jnp.pad(cc[1], ((0, 0), (CMP_PAD, 0), (0, 0))).transpose(0, 2, 1)
    n_rows = ncp + CMP_PAD

    nb = CMP_TQ // CMP_STRIDE + CMP_PAD
    first_dist = -CMP_STRIDE * (nb - 1 - CMP_PAD) - (CMP_BLOCK - 1)
    by_dist = _bias_table(rel_bias, first_dist + np.arange(CMP_TQ + CMP_STRIDE * (nb - 1)))
    band = jnp.stack([by_dist[:, CMP_STRIDE * (nb - 1 - b):CMP_STRIDE * (nb - 1 - b) + CMP_TQ] for b in range(nb)], axis=1)
    ratio = SEL_BLOCK // CMP_STRIDE
    lo = CMP_BLOCK // CMP_STRIDE - 1
    c_of_row = np.arange(n_rows)[None, :] - CMP_PAD
    j_of = np.arange(LANES)[:, None]
    mt = ((c_of_row >= ratio * j_of - lo) & (c_of_row <= ratio * j_of + ratio - 1)
          & (c_of_row >= 0) & (c_of_row <= ncp - 2) & (j_of < n_sel)).astype(np.float32)
    o_c, neg_mask = _cmp_attn(proj, kcc, vcct, band, jnp.asarray(mt, BF16), CMP_TQ)

    onehot = (np.arange(s)[:, None] // SEL_BLOCK == np.arange(LANES)[None, :]).astype(np.float32)
    periods = jnp.stack([_bias_period(rel_bias, tile, tile), _bias_period(rel_bias, 0, tile)], axis=1)
    o = _sel_win(proj, neg_mask, jnp.asarray(onehot, BF16), periods, gates, o_c, tile)
    return _matmul_resid_norm(o, w_out, layer, h, next_norm_g, 256)


def _s5_kernel(x_ref, lam_ref, bt_ref, c_ref, d_ref, y_ref, vt, yt, tmask, *, chunk):
    L = chunk
    half = L // 2
    width = L * SSM_GROUP
    n_chunks = x_ref.shape[0] // L
    by_step = jnp.swapaxes(x_ref[...].reshape(n_chunks, L, LANES), 0, 1)
    for tau in range(L):
        vt[tau] = by_step[tau].T

    @pl.when(pl.program_id(0) == 0)
    def _():
        dst = lax.broadcasted_iota(jnp.int32, (width, width), 0) // SSM_GROUP
        src = lax.broadcasted_iota(jnp.int32, (width, width), 1) // SSM_GROUP
        tmask[...] = jnp.where(src <= dst, 1.0, 0.0)

    def cmul(xr, xi, yr, yi):
        return xr * yr - xi * yi, xr * yi + xi * yr

    n2 = 2 * SSM_STATE
    sub = lax.broadcasted_iota(jnp.int32, (SUBLANES, 1), 0)
    consts = jnp.where(sub == 0, 1.0, jnp.where(sub == 1, float(half + 1), jnp.where(sub == 2, float(half - 1), float(L))))
    tau_col = lax.broadcasted_iota(jnp.int32, (L, 1), 0).astype(F32)
    row = lax.broadcasted_iota(jnp.int32, (n_chunks, n2), 0)
    conj = jnp.where(lax.broadcasted_iota(jnp.int32, (1, n2), 1) < SSM_STATE, 1.0, -1.0)

    def group(g):
        lam = lam_ref[g]
        a_re, a_im = lam[0:1], lam[1:2]
        dt = jnp.exp(lam[2:3])
        log_re, log_im = a_re * dt, a_im * dt

        def cpow(e):
            mag, cos, sin = jnp.exp(e * log_re), jnp.cos(e * log_im), jnp.sin(e * log_im)
            inv = 1.0 / mag
            return mag * cos, mag * sin, inv * cos, -(inv * sin)

        kr, ki, _, _ = cpow(consts)
        pr, pi = kr[0:1] - 1.0, ki[0:1]
        den = a_re * a_re + a_im * a_im
        cf_r = (pr * a_re + pi * a_im) / den
        cf_i = (pi * a_re - pr * a_im) / den
        bt_c, bt_s = bt_ref[g, 0], bt_ref[g, 1]
        bb_c = cf_r * bt_c + cf_i * bt_s
        bb_s = cf_r * bt_s - cf_i * bt_c
        c_c, c_s = c_ref[g, 0], c_ref[g, 1]

        e1r, e1i, e2r, e2i = cpow(tau_col - half)
        e3r, e3i = cmul(e1r, e1i, kr[1:2], ki[1:2])
        e4r, e4i = cmul(e2r, e2i, kr[2:3], ki[2:3])

        def outer(er, ei, w_c, w_s):
            return (er[:, None, :] * w_c[None, :, :] + ei[:, None, :] * w_s[None, :, :]).reshape(width, n2)

        qm = outer(e1r, e1i, c_c, c_s)
        km_conj = outer(e2r, e2i, bb_c * conj, bb_s * conj)
        qc = outer(e3r, e3i, c_c, c_s)
        wz = outer(e4r, e4i, bb_c, bb_s)

        chans = pl.ds(pl.multiple_of(g * SSM_GROUP, SSM_GROUP), SSM_GROUP)
        ub = vt[:, chans, :].reshape(width, n_chunks).astype(BF16)
        yield

        tt = _dot_nt(qm.astype(BF16), km_conj.astype(BF16))
        x = _dot(wz.T.astype(BF16), ub).T
        yield
        y = _dot((tt * tmask[...]).astype(BF16), ub)
        p_r, p_i = kr[3:4], ki[3:4]
        d = 1
        while d < n_chunks:
            s = jnp.where(row >= d, pltpu.roll(x, d, 0), 0.0)
            x = x + s * p_r - pltpu.roll(s, SSM_STATE, 1) * (p_i * conj)
            p_r, p_i = cmul(p_r, p_i, p_r, p_i)
            d *= 2
        prev_conj = jnp.where(row >= 1, pltpu.roll(x, 1, 0), 0.0) * conj
        yield
        y = y + _dot_nt(qc.astype(BF16), prev_conj.astype(BF16))
        yt[:, chans, :] = y.reshape(L, SSM_GROUP, n_chunks)

    def group_batch(k, carry):
        running = [group(S5_GROUPS_PER_TRIP * k + j) for j in range(S5_GROUPS_PER_TRIP)]
        finished = object()
        while running:
            running = [gen for gen in running if next(gen, finished) is not finished]
        return carry

    lax.fori_loop(0, LANES // SSM_GROUP // S5_GROUPS_PER_TRIP, group_batch, 0)
    by_chunk = jnp.swapaxes(jnp.stack([yt[tau].T for tau in range(L)], axis=0), 0, 1)
    y = by_chunk.reshape(n_chunks * L, LANES) + d_ref[...] * x_ref[...]
    y_ref[...] = jax.nn.gelu(y).astype(y_ref.dtype)


def _s5_mixer(h, hn, a_re, a_im, log_dt, b_re, b_im, c_re, c_im, d_skip, w_glu, layer):
    s = h.shape[0]
    L = S5_CHUNK
    n_chunks = s // L
    width = L * SSM_GROUP
    gpb = LANES // SSM_GROUP
    def packed(re, im):
        return jnp.stack([jnp.concatenate([re, im], -1), jnp.concatenate([-im, re], -1)], axis=1).astype(F32)

    lam = jnp.stack([a_re, a_im, jnp.broadcast_to(log_dt[:, None], a_re.shape)], axis=1).astype(F32)
    lam = jnp.concatenate([lam, lam], axis=-1)
    bt = packed(b_re.transpose(0, 2, 1), b_im.transpose(0, 2, 1))
    cc = packed(c_re, c_im)
    dd = d_skip.astype(F32).reshape(1, D_MODEL)
    y = pl.pallas_call(
        functools.partial(_s5_kernel, chunk=L),
        out_shape=jax.ShapeDtypeStruct((s, D_MODEL), BF16),
        grid=(SSM_GROUPS // gpb,),
        in_specs=[
            pl.BlockSpec((s, LANES), lambda b: (0, b)),
            pl.BlockSpec((gpb, 3, 2 * SSM_STATE), lambda b: (b, 0, 0)),
            pl.BlockSpec((gpb, 2, SSM_GROUP, 2 * SSM_STATE), lambda b: (b, 0, 0, 0)),
            pl.BlockSpec((gpb, 2, SSM_GROUP, 2 * SSM_STATE), lambda b: (b, 0, 0, 0)),
            pl.BlockSpec((1, LANES), lambda b: (0, b)),
        ],
        out_specs=pl.BlockSpec((s, LANES), lambda b: (0, b)),
        scratch_shapes=[
            pltpu.VMEM((L, LANES, n_chunks), hn.dtype),
            pltpu.VMEM((L, LANES, n_chunks), F32),
            pltpu.VMEM((width, width), F32),
        ],
        compiler_params=_params("arbitrary"),
    )(hn, lam, bt, cc, dd)
    return _matmul(_ep_glu_resid, y, w_glu, layer, D_MODEL, F32, 1024, 512, b_halves=2, tile_extra=h)


def kernel(x, rel_bias, mix_norm_g, ffn_norm_g, final_norm_g, nsa_w_in, nsa_w_out, cmp_pos_k, cmp_w1_k, cmp_w2_k, cmp_pos_v, cmp_w1_v, cmp_w2_v, s5_A_re, s5_A_im, s5_log_dt, s5_B_re, s5_B_im, s5_C_re, s5_C_im, s5_D, s5_w_glu, ffn_w_in, ffn_w_out):
    assert x.shape[0] == 1
    h = x[0]
    hn = _rmsnorm(h, mix_norm_g[0], BF16)
    h, hn = _nsa_mixer(h, hn, rel_bias, nsa_w_in, nsa_w_out, 0, cmp_pos_k[0], cmp_w1_k[0], cmp_w2_k[0],
                       cmp_pos_v[0], cmp_w1_v[0], cmp_w2_v[0], ffn_norm_g[0])
    h = _swiglu_ffn(h, hn, ffn_w_in, ffn_w_out, 0)
    hn = _rmsnorm(h, mix_norm_g[1], BF16)
    h = _s5_mixer(h, hn, s5_A_re[0], s5_A_im[0], s5_log_dt[0], s5_B_re[0], s5_B_im[0], s5_C_re[0], s5_C_im[0],
                  s5_D[0], s5_w_glu, 0)
    h = _swiglu_ffn(h, _rmsnorm(h, ffn_norm_g[1], BF16), ffn_w_in, ffn_w_out, 1)
    return _rmsnorm(h, final_norm_g, x.dtype)[None]
```

```python
import functools
import math

import numpy as np
import jax
import jax.numpy as jnp
from jax import lax
from jax.experimental import pallas as pl
from jax.experimental.pallas import tpu as pltpu

D_MODEL = 2048
N_HEADS = 16
HEAD_DIM = 128
N_KV_GROUPS = 4
HPG = N_HEADS // N_KV_GROUPS
CMP_BLOCK = 32
CMP_STRIDE = 16
CMP_HIDDEN = 2 * HEAD_DIM
SEL_BLOCK = 64
SEL_TOPK = 16
N_FORCED = 3
WINDOW = 512
N_GATES = 3
KV_WIDTH = N_KV_GROUPS * HEAD_DIM
Q_WIDTH = N_HEADS * HEAD_DIM
PROJ_MAIN = Q_WIDTH + 6 * KV_WIDTH
REL_BUCKETS = 32
REL_MAX_DIST = 128
SSM_GROUP = 16
SSM_GROUPS = D_MODEL // SSM_GROUP
SSM_STATE = 64
D_FF = ((8 * D_MODEL + 2) // 3 + 255) // 256 * 256
RMS_EPS = 1e-6
NEG = -1e30
LOG2E = math.log2(math.e)

LANES = 128
SUBLANES = 8
VMEM_LIMIT = 52 * 1024 * 1024

ATT_TILE = WINDOW // 2
FAR_TILES = 4
Q_TILES = 2
CMP_TQ = 256
CMP_PAD = 8
S5_CHUNK = 32
S5_GROUPS_PER_TRIP = 4

BF16 = jnp.bfloat16
F32 = jnp.float32


def _dot(a, b):
    return jnp.dot(a, b, preferred_element_type=F32)


def _dot_nt(a, b):
    return lax.dot_general(a, b, (((1,), (1,)), ((), ())), preferred_element_type=F32)


def _params(*sem):
    return pltpu.CompilerParams(dimension_semantics=sem, vmem_limit_bytes=VMEM_LIMIT)


def _rmsnorm_kernel(x_ref, g_ref, o_ref):
    x = x_ref[...]
    ms = jnp.mean(x * x, axis=-1, keepdims=True)
    o_ref[...] = (x * lax.rsqrt(ms + RMS_EPS) * g_ref[...]).astype(o_ref.dtype)


def _rmsnorm(x, g, out_dtype, tm=1024):
    s, d = x.shape
    return pl.pallas_call(
        _rmsnorm_kernel,
        out_shape=jax.ShapeDtypeStruct((s, d), out_dtype),
        grid=(s // tm,),
        in_specs=[pl.BlockSpec((tm, d), lambda i: (i, 0)), pl.BlockSpec((1, d), lambda i: (0, 0))],
        out_specs=pl.BlockSpec((tm, d), lambda i: (i, 0)),
        compiler_params=_params("parallel"),
    )(x, g.reshape(1, d))


def _mm_kernel(*refs, n_b, n_extra, epilogue, w_rows_are_outputs):
    a_ref = refs[0]
    b_refs = refs[1:1 + n_b]
    extra_refs = refs[1 + n_b:1 + n_b + n_extra]
    o_ref = refs[1 + n_b + n_extra]
    w_scr = refs[2 + n_b + n_extra:]

    @pl.when(pl.program_id(1) == 0)
    def _():
        for b_ref, w in zip(b_refs, w_scr):
            w[...] = b_ref[...].astype(BF16)

    a = a_ref[...]
    z = [(_dot_nt if w_rows_are_outputs else _dot)(a, w[...]) for w in w_scr]
    o_ref[...] = epilogue(*z, *[e[...] for e in extra_refs]).astype(o_ref.dtype)


def _ep_scale(z, scale):
    return z * scale


def _ep_resid(z, resid):
    return resid + z


def _ep_swiglu(za, zb):
    return jax.nn.silu(za) * zb


def _ep_glu_resid(za, zb, resid):
    return resid + za * jax.nn.sigmoid(zb)


def _matmul(epilogue, a, b, layer, n_out, out_dtype, tm, tn, *, b_halves=1, row_extra=None, tile_extra=None,
            w_rows_are_outputs=False):
    m, k = a.shape
    nj = n_out // tn
    estimate = (b_halves * k * tn * (2 * 4 + 2) + 2 * tm * k * 2 + 4 * tm * tn * 4 + b_halves * tm * tn * 4)
    b_mode = pl.Buffered(1) if estimate > VMEM_LIMIT else None
    in_specs = [pl.BlockSpec((tm, k), lambda j, i: (i, 0))]
    args = [a]
    for half in range(b_halves):
        if w_rows_are_outputs:
            spec = pl.BlockSpec((None, tn, k), functools.partial(lambda j, i, o: (layer, j + o, 0), o=half * nj),
                                pipeline_mode=b_mode)
        else:
            spec = pl.BlockSpec((None, k, tn), functools.partial(lambda j, i, o: (layer, 0, j + o), o=half * nj),
                                pipeline_mode=b_mode)
        in_specs.append(spec)
        args.append(b)
    extras = []
    if row_extra is not None:
        in_specs.append(pl.BlockSpec((1, tn), lambda j, i: (0, j)))
        extras.append(row_extra)
    if tile_extra is not None:
        in_specs.append(pl.BlockSpec((tm, tn), lambda j, i: (i, j)))
        extras.append(tile_extra)
    kern = functools.partial(_mm_kernel, n_b=b_halves, n_extra=len(extras), epilogue=epilogue,
                             w_rows_are_outputs=w_rows_are_outputs)
    return pl.pallas_call(
        kern,
        out_shape=jax.ShapeDtypeStruct((m, n_out), out_dtype),
        grid=(nj, m // tm),
        in_specs=in_specs,
        out_specs=pl.BlockSpec((tm, tn), lambda j, i: (i, j)),
        scratch_shapes=[pltpu.VMEM((tn, k) if w_rows_are_outputs else (k, tn), BF16) for _ in range(b_halves)],
        compiler_params=_params("arbitrary", "arbitrary"),
    )(*args, *extras)


def _mm_resid_norm_kernel(a_ref, b_ref, r_ref, g_ref, h_ref, hn_ref, w_scr):
    @pl.when(pl.program_id(0) == 0)
    def _():
        w_scr[...] = b_ref[...].astype(BF16)

    h = r_ref[...] + _dot(a_ref[...], w_scr[...])
    h_ref[...] = h
    ms = jnp.mean(h * h, axis=-1, keepdims=True)
    hn_ref[...] = (h * lax.rsqrt(ms + RMS_EPS) * g_ref[...]).astype(hn_ref.dtype)


def _matmul_resid_norm(a, b, layer, resid, g, tm):
    m, k = a.shape
    n = b.shape[2]
    return pl.pallas_call(
        _mm_resid_norm_kernel,
        out_shape=(jax.ShapeDtypeStruct((m, n), F32), jax.ShapeDtypeStruct((m, n), BF16)),
        grid=(m // tm,),
        in_specs=[
            pl.BlockSpec((tm, k), lambda i: (i, 0)),
            pl.BlockSpec((None, k, n), lambda i: (layer, 0, 0), pipeline_mode=pl.Buffered(1)),
            pl.BlockSpec((tm, n), lambda i: (i, 0)),
            pl.BlockSpec((1, n), lambda i: (0, 0)),
        ],
        out_specs=(pl.BlockSpec((tm, n), lambda i: (i, 0)), pl.BlockSpec((tm, n), lambda i: (i, 0))),
        scratch_shapes=[pltpu.VMEM((k, n), BF16)],
        compiler_params=_params("arbitrary"),
    )(a, b, resid, g.reshape(1, n))


def _swiglu_ffn(h, hn, w_in, w_out, layer):
    act = _matmul(_ep_swiglu, hn, w_in, layer, D_FF, BF16, 1024, 512, b_halves=2)
    return _matmul(_ep_resid, act, w_out, layer, D_MODEL, F32, 512, 512, tile_extra=h)


def _compress_kernel(x_ref, w1_ref, w2_ref, pos_ref, o_ref):
    ncp = x_ref.shape[0] // CMP_STRIDE
    half = CMP_STRIDE * HEAD_DIM
    by_offset = jnp.swapaxes(x_ref[...].reshape(ncp, CMP_STRIDE, HEAD_DIM), 0, 1)
    rows = jnp.concatenate([by_offset[r] for r in range(CMP_STRIDE)], axis=1)
    top = _dot(rows, w1_ref[0, :half, :])
    bot = _dot(rows, w1_ref[0, half:, :])
    posb = _dot(pos_ref[0], w1_ref[0])[0:1]
    hid = top + pltpu.roll(bot, ncp - 1, 0) + posb
    o_ref[0, 0] = _dot(jax.nn.gelu(hid).astype(BF16), w2_ref[0]).astype(o_ref.dtype)


def _compress(proj, w1, w2, pos):
    s = proj.shape[0]
    ncp = s // CMP_STRIDE
    kc_blk = Q_WIDTH // HEAD_DIM
    return pl.pallas_call(
        _compress_kernel,
        out_shape=jax.ShapeDtypeStruct((2, N_KV_GROUPS, ncp, HEAD_DIM), BF16),
        grid=(2, N_KV_GROUPS),
        in_specs=[
            pl.BlockSpec((s, HEAD_DIM), lambda kv, gi: (0, kc_blk + kv * N_KV_GROUPS + gi)),
            pl.BlockSpec((1, CMP_BLOCK * HEAD_DIM, CMP_HIDDEN), lambda kv, gi: (kv, 0, 0)),
            pl.BlockSpec((1, CMP_HIDDEN, HEAD_DIM), lambda kv, gi: (kv, 0, 0)),
            pl.BlockSpec((1, SUBLANES, CMP_BLOCK * HEAD_DIM), lambda kv, gi: (kv, 0, 0)),
        ],
        out_specs=pl.BlockSpec((1, 1, ncp, HEAD_DIM), lambda kv, gi: (kv, gi, 0, 0)),
        compiler_params=_params("parallel", "parallel"),
    )(proj, w1, w2, pos)


def _first_argmax(score):
    n_rows, n = score.shape
    sub = lax.broadcasted_iota(jnp.int32, (SUBLANES, n), 0)
    pairs = [(score[r:r + SUBLANES], sub + r) for r in range(0, n_rows, SUBLANES)]
    while len(pairs) > 2:
        merged = []
        for k in range(0, len(pairs) - 1, 2):
            (va, ia), (vb, ib) = pairs[k], pairs[k + 1]
            take = vb > va
            merged.append((jnp.where(take, vb, va), jnp.where(take, ib, ia)))
        if len(pairs) % 2:
            merged.append(pairs[-1])
        pairs = merged
    val = jnp.concatenate([v for v, _ in pairs], axis=0)
    idx = jnp.concatenate([i for _, i in pairs], axis=0)
    best = jnp.max(val, axis=0, keepdims=True)
    return jnp.min(jnp.where(val == best, idx, n_rows), axis=0, keepdims=True)


def _cmp_attn_kernel(q_ref, kcc_ref, vcct_ref, band_ref, mt_ref, oc_ref, nm_ref, s_scr, *, tq, nb):
    i = pl.program_id(1)
    band_start = pl.multiple_of(i * (tq // CMP_STRIDE), SUBLANES)
    n_rows = kcc_ref.shape[1]
    n_blk = mt_ref.shape[0]

    def body(n_r, n_b):
        kcc = kcc_ref[0, :n_r, :]
        vcct = vcct_ref[0, :, :n_r]
        rows = lax.broadcasted_iota(jnp.int32, (n_r, tq), 0)
        row_bias = jnp.where(rows >= CMP_PAD, jnp.where(rows < band_start + nb, 0.0, NEG), NEG)
        imp = jnp.zeros((n_r, tq), F32)

        def scores(h):
            qh = q_ref[:, h * HEAD_DIM:(h + 1) * HEAD_DIM]
            s_scr[h, :n_r, :] = _dot_nt(kcc, qh) + row_bias
            s_scr[h, pl.ds(band_start, nb), :] += band_ref[h]

        scores(0)
        probs = []
        for h in range(HPG):
            if h + 1 < HPG:
                scores(h + 1)
            s = s_scr[h, :n_r, :]
            m = jnp.maximum(jnp.max(s, axis=0, keepdims=True), 0.5 * NEG)
            e = jnp.exp2(s - m)
            l = jnp.sum(e, axis=0, keepdims=True)
            pn = e * (1.0 / jnp.where(l > 0.0, l, 1.0))
            imp = imp + pn
            probs.append(pn.astype(BF16))

        def outputs():
            for h in range(HPG):
                oct_h = _dot(vcct, probs[h])
                oc_ref[:, h * HEAD_DIM:(h + 1) * HEAD_DIM] = oct_h.T
                yield
        mt = mt_ref[:n_b, :n_r]
        hi = imp.astype(BF16)
        rest = imp - hi.astype(F32)
        mid = rest.astype(BF16)
        lo = (rest - mid.astype(F32)).astype(BF16)
        slc = _dot(mt, hi) + _dot(mt, mid) + _dot(mt, lo)
        jj = lax.broadcasted_iota(jnp.int32, (n_b, tq), 0)
        t_blk = (i * tq + lax.broadcasted_iota(jnp.int32, (n_b, tq), 1)) // SEL_BLOCK
        forced = jnp.where(jj == 0, 1, jnp.where(jj == t_blk, 1, jnp.where(jj == t_blk - 1, 1, 0)))
        picked = jnp.where(forced == 1, 1.0, 0.0)
        score = jnp.where(forced == 1, -jnp.inf, jnp.where(jj <= t_blk, slc, NEG))
        pending = outputs()
        for r in range(SEL_TOPK - N_FORCED):
            hit = jj == _first_argmax(score)
            picked = jnp.where(hit, 1.0, picked)
            score = jnp.where(hit, -jnp.inf, score)
            if r % 3 == 0:
                next(pending, None)
        for _ in pending:
            pass
        neg_mask = jnp.where(picked > 0.5, jnp.where(jj <= t_blk, 0.0, NEG), NEG)
        if n_b < n_blk:
            neg_mask = jnp.concatenate([neg_mask, jnp.full((n_blk - n_b, tq), NEG, F32)], axis=0)
        nm_ref[...] = neg_mask.T.astype(nm_ref.dtype)

    step = LANES
    extents = list(range(step, n_rows, step)) + [n_rows]
    need_rows = band_start + nb
    for v, n_r in enumerate(extents):
        lower = extents[v - 1] if v else 0
        n_b = min(n_blk, -(-(n_r * CMP_STRIDE // SEL_BLOCK) // SUBLANES) * SUBLANES)
        pl.when(jnp.logical_and(need_rows > lower, need_rows <= n_r))(functools.partial(body, n_r, n_b))


def _cmp_attn(proj, kcc, vcct, band, mt, tq):
    s = proj.shape[0]
    n_rows = kcc.shape[1]
    nb = band.shape[1]
    kern = functools.partial(_cmp_attn_kernel, tq=tq, nb=nb)
    return pl.pallas_call(
        kern,
        out_shape=(jax.ShapeDtypeStruct((s, Q_WIDTH), F32),
                   jax.ShapeDtypeStruct((s, N_KV_GROUPS * LANES), BF16)),
        grid=(N_KV_GROUPS, s // tq),
        in_specs=[
            pl.BlockSpec((tq, HPG * HEAD_DIM), lambda g, i: (i, g)),
            pl.BlockSpec((1, n_rows, HEAD_DIM), lambda g, i: (g, 0, 0)),
            pl.BlockSpec((1, HEAD_DIM, n_rows), lambda g, i: (g, 0, 0)),
            pl.BlockSpec((HPG, nb, tq), lambda g, i: (g, 0, 0)),
            pl.BlockSpec((LANES, n_rows), lambda g, i: (0, 0)),
        ],
        out_specs=(pl.BlockSpec((tq, HPG * HEAD_DIM), lambda g, i: (i, g)),
                   pl.BlockSpec((tq, LANES), lambda g, i: (i, g))),
        scratch_shapes=[pltpu.VMEM((HPG, n_rows, tq), F32)],
        compiler_params=_params("parallel", "parallel"),
    )(proj, kcc, vcct, band, mt)


def _lane_tile(x, width):
    return jnp.concatenate([x] * (width // x.shape[1]), axis=1)


def _softmax_update(s, v_aug, m_ref, acc_ref, rows):
    m_prev = m_ref[rows, :]
    m_next = jnp.maximum(m_prev, jnp.max(s, axis=1, keepdims=True))
    alpha = jnp.exp2(m_prev - m_next)
    p = jnp.exp2(s - _lane_tile(m_next, s.shape[1]))
    acc_ref[rows, :] = _lane_tile(alpha, acc_ref.shape[1]) * acc_ref[rows, :] + _dot(p.astype(BF16), v_aug)
    m_ref[rows, :] = m_next


def _softmax_once(s, v_aug):
    m = jnp.broadcast_to(jnp.max(s, axis=1, keepdims=True), (s.shape[0], LANES))
    p = jnp.exp2(s - _lane_tile(m, s.shape[1]))
    return _dot(p.astype(BF16), v_aug)


def _sel_win_kernel(q_ref, nm_ref, ks_ref, vs_ref, kw_ref, vw_ref, blk_ref, per_ref, g_ref, oc_ref, o_ref,
                    q4, m_s, a_s, bias_ref, *, tile):
    step = pl.program_id(1)
    t = tile
    first = step * Q_TILES

    @pl.when(step == 0)
    def _():
        tl = lax.broadcasted_iota(jnp.int32, (t, t), 0)
        kl = lax.broadcasted_iota(jnp.int32, (t, t), 1)
        edge = jnp.where(tl < kl, 0.0, NEG)
        for h in range(HPG):
            bias_ref[h, :, :t] = edge
            for which in range(2):
                rows = jnp.broadcast_to(per_ref[h, which:which + 1, :], (t, 2 * t))
                table = pltpu.roll(rows, 0, 1, stride=1, stride_axis=0)
                bias_ref[h, :, (which + 1) * t:(which + 2) * t] = table[:, :t]

    def stacked(qt, h):
        return slice((qt * HPG + h) * t, (qt * HPG + h + 1) * t)

    def tile_rows(qt):
        return slice(qt * t, (qt + 1) * t)

    for qt in range(Q_TILES):
        for h in range(HPG):
            q4[stacked(qt, h), :HEAD_DIM] = q_ref[tile_rows(qt), h * HEAD_DIM:(h + 1) * HEAD_DIM]
            q4[stacked(qt, h), HEAD_DIM:] = nm_ref[tile_rows(qt), :]
    m_s[...] = jnp.full(m_s.shape, -3e38, F32)
    a_s[...] = jnp.zeros(a_s.shape, F32)

    def sel_operands(row0, n_tiles):
        keys = pl.ds(row0, n_tiles * t)
        k_aug = jnp.concatenate([ks_ref[keys, :], blk_ref[keys, :]], axis=1)
        v_aug = jnp.concatenate([vs_ref[keys, :], jnp.ones((n_tiles * t, LANES), BF16)], axis=1)
        return k_aug, v_aug

    def sel_pass(qt, operands, n_biased):
        k_aug, v_aug = operands

        def score(h):
            s = _dot_nt(q4[stacked(qt, h), :], k_aug)
            plain = s.shape[1] - n_biased * t
            if n_biased:
                near = s[:, plain:] + bias_ref[h, :, (3 - n_biased) * t:]
                s = jnp.concatenate([s[:, :plain], near], axis=1) if plain else near
            return s

        def consume(h, s):
            _softmax_update(s, v_aug, m_s, a_s, stacked(qt, h))

        return score, consume

    def win_pass(qt, row0, n_tiles):
        keys = pl.ds(row0, n_tiles * t)
        v_aug = jnp.concatenate([vw_ref[keys, :], jnp.ones((n_tiles * t, LANES), BF16)], axis=1)

        def score(h):
            return _dot_nt(q4[stacked(qt, h), :HEAD_DIM], kw_ref[keys, :]) + bias_ref[h, :, (3 - n_tiles) * t:]

        def consume(h, s):
            acc_w = _softmax_once(s, v_aug)
            acc_s = a_s[stacked(qt, h), :]
            o_w = acc_w[:, :HEAD_DIM] * (1.0 / acc_w[:, HEAD_DIM:])
            o_s = acc_s[:, :HEAD_DIM] * (1.0 / acc_s[:, HEAD_DIM:])
            cols = slice(h * HEAD_DIM, (h + 1) * HEAD_DIM)
            gates = g_ref[tile_rows(qt), :]
            o = (gates[:, h:h + 1] * oc_ref[tile_rows(qt), cols] + gates[:, HPG + h:HPG + h + 1] * o_s
                 + gates[:, 2 * HPG + h:2 * HPG + h + 1] * o_w)
            o_ref[tile_rows(qt), cols] = o.astype(o_ref.dtype)

        return score, consume

    def run(passes):
        work = [(score, consume, h0) for score, consume in passes for h0 in range(0, HPG, 2)]
        ahead = [work[0][0](h) for h in (0, 1)]
        for n, (_, consume, h0) in enumerate(work):
            now = ahead
            if n + 1 < len(work):
                nxt_score, _, nxt_h0 = work[n + 1]
                ahead = [nxt_score(h) for h in (nxt_h0, nxt_h0 + 1)]
            for h, s in zip((h0, h0 + 1), now):
                consume(h, s)

    def far_passes(row0):
        operands = sel_operands(row0, FAR_TILES)
        return [sel_pass(qt, operands, 0) for qt in range(Q_TILES)]

    n_far = jnp.maximum(first - 1, 0)
    n_chunks = n_far // FAR_TILES
    chunk = FAR_TILES * t

    def far_pair(c, carry):
        row0 = pl.multiple_of(c * 2 * chunk, 2 * chunk)
        run(far_passes(row0) + far_passes(row0 + chunk))
        return carry

    lax.fori_loop(0, n_chunks // 2, far_pair, 0)

    @pl.when(n_chunks % 2 == 1)
    def _():
        run(far_passes(pl.multiple_of((n_chunks - 1) * chunk, chunk)))

    def tail(left):
        row0 = pl.multiple_of((n_far - left) * t, t)
        passes = []
        for qt in range(Q_TILES):
            n_sel = left + 2 + qt
            passes.append(sel_pass(qt, sel_operands(row0, n_sel), 2))
            passes.append(win_pass(qt, pl.multiple_of((first + qt - 2) * t, t), 3))
        run(passes)

    for left in range(FAR_TILES):
        if (left + 1) % Q_TILES == 0:
            pl.when(jnp.logical_and(step >= 1, n_far % FAR_TILES == left))(functools.partial(tail, left))

    @pl.when(step == 0)
    def _():
        passes = []
        for qt in range(Q_TILES):
            n_keys = min(qt + 1, 3)
            passes.append(sel_pass(qt, sel_operands(0, qt + 1), min(qt + 1, 2)))
            passes.append(win_pass(qt, (qt + 1 - n_keys) * t, n_keys))
        run(passes)


def _sel_win(proj, neg_mask, blk_onehot, periods, gates, o_c, tile):
    s = proj.shape[0]
    ks_blk = (Q_WIDTH + 2 * KV_WIDTH) // HEAD_DIM
    vs_blk = (Q_WIDTH + 3 * KV_WIDTH) // HEAD_DIM
    kw_blk = (Q_WIDTH + 4 * KV_WIDTH) // HEAD_DIM
    vw_blk = (Q_WIDTH + 5 * KV_WIDTH) // HEAD_DIM
    once = pl.Buffered(1)
    rows = Q_TILES * tile
    stacked_rows = HPG * rows
    kern = functools.partial(_sel_win_kernel, tile=tile)
    return pl.pallas_call(
        kern,
        out_shape=jax.ShapeDtypeStruct((s, Q_WIDTH), BF16),
        grid=(N_KV_GROUPS, s // rows),
        in_specs=[
            pl.BlockSpec((rows, HPG * HEAD_DIM), lambda g, i: (i, g)),
            pl.BlockSpec((rows, LANES), lambda g, i: (i, g)),
            pl.BlockSpec((s, HEAD_DIM), lambda g, i: (0, ks_blk + g), pipeline_mode=once),
            pl.BlockSpec((s, HEAD_DIM), lambda g, i: (0, vs_blk + g), pipeline_mode=once),
            pl.BlockSpec((s, HEAD_DIM), lambda g, i: (0, kw_blk + g), pipeline_mode=once),
            pl.BlockSpec((s, HEAD_DIM), lambda g, i: (0, vw_blk + g), pipeline_mode=once),
            pl.BlockSpec((s, LANES), lambda g, i: (0, 0), pipeline_mode=once),
            pl.BlockSpec((HPG, 2, 2 * tile), lambda g, i: (g, 0, 0)),
            pl.BlockSpec((rows, LANES), lambda g, i: (i, g)),
            pl.BlockSpec((rows, HPG * HEAD_DIM), lambda g, i: (i, g)),
        ],
        out_specs=pl.BlockSpec((rows, HPG * HEAD_DIM), lambda g, i: (i, g)),
        scratch_shapes=[
            pltpu.VMEM((stacked_rows, 2 * HEAD_DIM), BF16),
            pltpu.VMEM((stacked_rows, LANES), F32),
            pltpu.VMEM((stacked_rows, 2 * HEAD_DIM), F32),
            pltpu.VMEM((HPG, tile, 3 * tile), F32),
        ],
        compiler_params=_params("arbitrary", "arbitrary"),
    )(proj, neg_mask, proj, proj, proj, proj, blk_onehot, periods, gates, o_c)


def _rel_bucket_np(dist):
    n = np.maximum(dist, 0)
    max_exact = REL_BUCKETS // 2
    large = max_exact + (np.log(np.maximum(n, 1).astype(np.float32) / np.float32(max_exact))
                         / np.float32(math.log(REL_MAX_DIST / max_exact))
                         * np.float32(REL_BUCKETS - max_exact)).astype(np.int32)
    large = np.minimum(large, REL_BUCKETS - 1)
    return np.where(n < max_exact, n, large).astype(np.int32)


def _far_distance():
    d = np.arange(4 * REL_MAX_DIST)
    b = _rel_bucket_np(d)
    assert b[-1] == REL_BUCKETS - 1
    return int(np.max(np.nonzero(b != REL_BUCKETS - 1)[0])) + 1


def _bias_table(rel_bias, dist):
    bucket = jnp.asarray(_rel_bucket_np(dist))
    shifted = (rel_bias[bucket] - rel_bias[REL_BUCKETS - 1]) * LOG2E
    tab = jnp.where(jnp.asarray(dist >= 0)[..., None], shifted, NEG)
    return jnp.moveaxis(tab, -1, 0).astype(F32)


def _bias_period(rel_bias, offset, t):
    x = np.arange(2 * t)
    return _bias_table(rel_bias, np.where(x < t, offset - x, offset + 2 * t - x))


def _nsa_mixer(h, hn, rel_bias, w_in, w_out, layer, pos_k, w1_k, w2_k, pos_v, w1_v, w2_v, next_norm_g):
    s = h.shape[0]
    n_sel = s // SEL_BLOCK
    ncp = s // CMP_STRIDE
    tile = ATT_TILE
    far = _far_distance()
    assert n_sel <= LANES and s % CMP_TQ == 0 and s % (Q_TILES * tile) == 0 and FAR_TILES % Q_TILES == 0
    assert far <= tile and far <= CMP_STRIDE * CMP_PAD - CMP_BLOCK + 1 + CMP_STRIDE

    col_scale = jnp.where(jnp.arange(PROJ_MAIN) < Q_WIDTH, HEAD_DIM ** -0.5 * LOG2E, 1.0).astype(F32)[None, :]
    w_in_t = jnp.swapaxes(w_in, 1, 2)
    proj = _matmul(_ep_scale, hn, w_in_t, layer, PROJ_MAIN, BF16, 2048, 512, row_extra=col_scale,
                   w_rows_are_outputs=True)
    w_gate = w_in_t[layer, PROJ_MAIN:].reshape(N_GATES, N_KV_GROUPS, HPG, D_MODEL).transpose(1, 0, 2, 3)
    w_gate = w_gate.reshape(N_KV_GROUPS, N_GATES * HPG, D_MODEL)
    w_gate = jnp.pad(w_gate, ((0, 0), (0, LANES - N_GATES * HPG), (0, 0))).reshape(1, N_KV_GROUPS * LANES, D_MODEL)
    gates = _matmul(jax.nn.sigmoid, hn, w_gate, 0, N_KV_GROUPS * LANES, F32, 2048, 512, w_rows_are_outputs=True)

    w1 = jnp.stack([w1_k, w1_v]).astype(BF16)
    w2 = jnp.stack([w2_k, w2_v]).astype(BF16)
    pos = jnp.stack([pos_k.reshape(1, -1), pos_v.reshape(1, -1)]).astype(BF16)
    pos = jnp.broadcast_to(pos, (2, SUBLANES, CMP_BLOCK * HEAD_DIM))
    cc = _compress(proj, w1, w2, pos)
    kcc = jnp.pad(cc[0], ((0, 0), (CMP_PAD, 0), (0, 0)))
    vcct = jnp.pad(cc[1], ((0, 0), (CMP_PAD, 0), (0, 0))).transpose(0, 2, 1)
    n_rows = ncp + CMP_PAD

    nb = CMP_TQ // CMP_STRIDE + CMP_PAD
    first_dist = -CMP_STRIDE * (nb - 1 - CMP_PAD) - (CMP_BLOCK - 1)
    by_dist = _bias_table(rel_bias, first_dist + np.arange(CMP_TQ + CMP_STRIDE * (nb - 1)))
    band = jnp.stack([by_dist[:, CMP_STRIDE * (nb - 1 - b):CMP_STRIDE * (nb - 1 - b) + CMP_TQ] for b in range(nb)], axis=1)
    ratio = SEL_BLOCK // CMP_STRIDE
    lo = CMP_BLOCK // CMP_STRIDE - 1
    c_of_row = np.arange(n_rows)[None, :] - CMP_PAD
    j_of = np.arange(LANES)[:, None]
    mt = ((c_of_row >= ratio * j_of - lo) & (c_of_row <= ratio * j_of + ratio - 1)
          & (c_of_row >= 0) & (c_of_row <= ncp - 2) & (j_of < n_sel)).astype(np.float32)
    o_c, neg_mask = _cmp_attn(proj, kcc, vcct, band, jnp.asarray(mt, BF16), CMP_TQ)

    onehot = (np.arange(s)[:, None] // SEL_BLOCK == np.arange(LANES)[None, :]).astype(np.float32)
    periods = jnp.stack([_bias_period(rel_bias, tile, tile), _bias_period(rel_bias, 0, tile)], axis=1)
    o = _sel_win(proj, neg_mask, jnp.asarray(onehot, BF16), periods, gates, o_c, tile)
    return _matmul_resid_norm(o, w_out, layer, h, next_norm_g, 256)


def _s5_kernel(x_ref, lam_ref, bt_ref, c_ref, d_ref, y_ref, vt, yt, tmask, *, chunk):
    L = chunk
    half = L // 2
    width = L * SSM_GROUP
    n_chunks = x_ref.shape[0] // L
    by_step = jnp.swapaxes(x_ref[...].reshape(n_chunks, L, LANES), 0, 1)
    for tau in range(L):
        vt[tau] = by_step[tau].T

    @pl.when(pl.program_id(0) == 0)
    def _():
        dst = lax.broadcasted_iota(jnp.int32, (width, width), 0) // SSM_GROUP
        src = lax.broadcasted_iota(jnp.int32, (width, width), 1) // SSM_GROUP
        tmask[...] = jnp.where(src <= dst, 1.0, 0.0)

    def cmul(xr, xi, yr, yi):
        return xr * yr - xi * yi, xr * yi + xi * yr

    n2 = 2 * SSM_STATE
    sub = lax.broadcasted_iota(jnp.int32, (SUBLANES, 1), 0)
    consts = jnp.where(sub == 0, 1.0, jnp.where(sub == 1, float(half + 1), jnp.where(sub == 2, float(half - 1), float(L))))
    tau_col = lax.broadcasted_iota(jnp.int32, (L, 1), 0).astype(F32)
    row = lax.broadcasted_iota(jnp.int32, (n_chunks, n2), 0)
    conj = jnp.where(lax.broadcasted_iota(jnp.int32, (1, n2), 1) < SSM_STATE, 1.0, -1.0)

    def group(g):
        lam = lam_ref[g]
        a_re, a_im = lam[0:1], lam[1:2]
        dt = jnp.exp(lam[2:3])
        log_re, log_im = a_re * dt, a_im * dt

        def cpow(e):
            mag, cos, sin = jnp.exp(e * log_re), jnp.cos(e * log_im), jnp.sin(e * log_im)
            inv = 1.0 / mag
            return mag * cos, mag * sin, inv * cos, -(inv * sin)

        kr, ki, _, _ = cpow(consts)
        pr, pi = kr[0:1] - 1.0, ki[0:1]
        den = a_re * a_re + a_im * a_im
        cf_r = (pr * a_re + pi * a_im) / den
        cf_i = (pi * a_re - pr * a_im) / den
        bt_c, bt_s = bt_ref[g, 0], bt_ref[g, 1]
        bb_c = cf_r * bt_c + cf_i * bt_s
        bb_s = cf_r * bt_s - cf_i * bt_c
        c_c, c_s = c_ref[g, 0], c_ref[g, 1]

        e1r, e1i, e2r, e2i = cpow(tau_col - half)
        e3r, e3i = cmul(e1r, e1i, kr[1:2], ki[1:2])
        e4r, e4i = cmul(e2r, e2i, kr[2:3], ki[2:3])

        def outer(er, ei, w_c, w_s):
            return (er[:, None, :] * w_c[None, :, :] + ei[:, None, :] * w_s[None, :, :]).reshape(width, n2)

        qm = outer(e1r, e1i, c_c, c_s)
        km_conj = outer(e2r, e2i, bb_c * conj, bb_s * conj)
        qc = outer(e3r, e3i, c_c, c_s)
        wz = outer(e4r, e4i, bb_c, bb_s)

        chans = pl.ds(pl.multiple_of(g * SSM_GROUP, SSM_GROUP), SSM_GROUP)
        ub = vt[:, chans, :].reshape(width, n_chunks).astype(BF16)
        yield

        tt = _dot_nt(qm.astype(BF16), km_conj.astype(BF16))
        x = _dot(wz.T.astype(BF16), ub).T
        yield
        y = _dot((tt * tmask[...]).astype(BF16), ub)
        p_r, p_i = kr[3:4], ki[3:4]
        d = 1
        while d < n_chunks:
            s = jnp.where(row >= d, pltpu.roll(x, d, 0), 0.0)
            x = x + s * p_r - pltpu.roll(s, SSM_STATE, 1) * (p_i * conj)
            p_r, p_i = cmul(p_r, p_i, p_r, p_i)
            d *= 2
        prev_conj = jnp.where(row >= 1, pltpu.roll(x, 1, 0), 0.0) * conj
        yield
        y = y + _dot_nt(qc.astype(BF16), prev_conj.astype(BF16))
        yt[:, chans, :] = y.reshape(L, SSM_GROUP, n_chunks)

    def group_batch(k, carry):
        running = [group(S5_GROUPS_PER_TRIP * k + j) for j in range(S5_GROUPS_PER_TRIP)]
        finished = object()
        while running:
            running = [gen for gen in running if next(gen, finished) is not finished]
        return carry

    lax.fori_loop(0, LANES // SSM_GROUP // S5_GROUPS_PER_TRIP, group_batch, 0)
    by_chunk = jnp.swapaxes(jnp.stack([yt[tau].T for tau in range(L)], axis=0), 0, 1)
    y = by_chunk.reshape(n_chunks * L, LANES) + d_ref[...] * x_ref[...]
    y_ref[...] = jax.nn.gelu(y).astype(y_ref.dtype)


def _s5_mixer(h, hn, a_re, a_im, log_dt, b_re, b_im, c_re, c_im, d_skip, w_glu, layer):
    s = h.shape[0]
    L = S5_CHUNK
    n_chunks = s // L
    width = L * SSM_GROUP
    gpb = LANES // SSM_GROUP
    def packed(re, im):
        return jnp.stack([jnp.concatenate([re, im], -1), jnp.concatenate([-im, re], -1)], axis=1).astype(F32)

    lam = jnp.stack([a_re, a_im, jnp.broadcast_to(log_dt[:, None], a_re.shape)], axis=1).astype(F32)
    lam = jnp.concatenate([lam, lam], axis=-1)
    bt = packed(b_re.transpose(0, 2, 1), b_im.transpose(0, 2, 1))
    cc = packed(c_re, c_im)
    dd = d_skip.astype(F32).reshape(1, D_MODEL)
    y = pl.pallas_call(
        functools.partial(_s5_kernel, chunk=L),
        out_shape=jax.ShapeDtypeStruct((s, D_MODEL), BF16),
        grid=(SSM_GROUPS // gpb,),
        in_specs=[
            pl.BlockSpec((s, LANES), lambda b: (0, b)),
            pl.BlockSpec((gpb, 3, 2 * SSM_STATE), lambda b: (b, 0, 0)),
            pl.BlockSpec((gpb, 2, SSM_GROUP, 2 * SSM_STATE), lambda b: (b, 0, 0, 0)),
            pl.BlockSpec((gpb, 2, SSM_GROUP, 2 * SSM_STATE), lambda b: (b, 0, 0, 0)),
            pl.BlockSpec((1, LANES), lambda b: (0, b)),
        ],
        out_specs=pl.BlockSpec((s, LANES), lambda b: (0, b)),
        scratch_shapes=[
            pltpu.VMEM((L, LANES, n_chunks), hn.dtype),
            pltpu.VMEM((L, LANES, n_chunks), F32),
            pltpu.VMEM((width, width), F32),
        ],
        compiler_params=_params("arbitrary"),
    )(hn, lam, bt, cc, dd)
    return _matmul(_ep_glu_resid, y, w_glu, layer, D_MODEL, F32, 1024, 512, b_halves=2, tile_extra=h)


def kernel(x, rel_bias, mix_norm_g, ffn_norm_g, final_norm_g, nsa_w_in, nsa_w_out, cmp_pos_k, cmp_w1_k, cmp_w2_k, cmp_pos_v, cmp_w1_v, cmp_w2_v, s5_A_re, s5_A_im, s5_log_dt, s5_B_re, s5_B_im, s5_C_re, s5_C_im, s5_D, s5_w_glu, ffn_w_in, ffn_w_out):
    assert x.shape[0] == 1
    h = x[0]
    hn = _rmsnorm(h, mix_norm_g[0], BF16)
    h, hn = _nsa_mixer(h, hn, rel_bias, nsa_w_in, nsa_w_out, 0, cmp_pos_k[0], cmp_w1_k[0], cmp_w2_k[0],
                       cmp_pos_v[0], cmp_w1_v[0], cmp_w2_v[0], ffn_norm_g[0])
    h = _swiglu_ffn(h, hn, ffn_w_in, ffn_w_out, 0)
    hn = _rmsnorm(h, mix_norm_g[1], BF16)
    h = _s5_mixer(h, hn, s5_A_re[0], s5_A_im[0], s5_log_dt[0], s5_B_re[0], s5_B_im[0], s5_C_re[0], s5_C_im[0],
                  s5_D[0], s5_w_glu, 0)
    h = _swiglu_ffn(h, _rmsnorm(h, ffn_norm_g[1], BF16), ffn_w_in, ffn_w_out, 1)
    return _rmsnorm(h, final_norm_g, x.dtype)[None]
```

```python
import functools
import math

import numpy as np
import jax
import jax.numpy as jnp
from jax import lax
from jax.experimental import pallas as pl
from jax.experimental.pallas import tpu as pltpu

D_MODEL = 2048
N_HEADS = 16
HEAD_DIM = 128
N_KV_GROUPS = 4
HPG = N_HEADS // N_KV_GROUPS
CMP_BLOCK = 32
CMP_STRIDE = 16
CMP_HIDDEN = 2 * HEAD_DIM
SEL_BLOCK = 64
SEL_TOPK = 16
N_FORCED = 3
WINDOW = 512
N_GATES = 3
KV_WIDTH = N_KV_GROUPS * HEAD_DIM
Q_WIDTH = N_HEADS * HEAD_DIM
PROJ_MAIN = Q_WIDTH + 6 * KV_WIDTH
REL_BUCKETS = 32
REL_MAX_DIST = 128
SSM_GROUP = 16
SSM_GROUPS = D_MODEL // SSM_GROUP
SSM_STATE = 64
D_FF = ((8 * D_MODEL + 2) // 3 + 255) // 256 * 256
RMS_EPS = 1e-6
NEG = -1e30
LOG2E = math.log2(math.e)

LANES = 128
SUBLANES = 8
VMEM_LIMIT = 52 * 1024 * 1024

ATT_TILE = WINDOW // 2
FAR_TILES = 4
Q_TILES = 2
CMP_TQ = 256
CMP_PAD = 8
S5_CHUNK = 32
S5_GROUPS_PER_TRIP = 4

BF16 = jnp.bfloat16
F32 = jnp.float32


def _dot(a, b):
    return jnp.dot(a, b, preferred_element_type=F32)


def _dot_nt(a, b):
    return lax.dot_general(a, b, (((1,), (1,)), ((), ())), preferred_element_type=F32)


def _params(*sem):
    return pltpu.CompilerParams(dimension_semantics=sem, vmem_limit_bytes=VMEM_LIMIT)


def _rmsnorm_kernel(x_ref, g_ref, o_ref):
    x = x_ref[...]
    ms = jnp.mean(x * x, axis=-1, keepdims=True)
    o_ref[...] = (x * lax.rsqrt(ms + RMS_EPS) * g_ref[...]).astype(o_ref.dtype)


def _rmsnorm(x, g, out_dtype, tm=1024):
    s, d = x.shape
    return pl.pallas_call(
        _rmsnorm_kernel,
        out_shape=jax.ShapeDtypeStruct((s, d), out_dtype),
        grid=(s // tm,),
        in_specs=[pl.BlockSpec((tm, d), lambda i: (i, 0)), pl.BlockSpec((1, d), lambda i: (0, 0))],
        out_specs=pl.BlockSpec((tm, d), lambda i: (i, 0)),
        compiler_params=_params("parallel"),
    )(x, g.reshape(1, d))


def _mm_kernel(*refs, n_b, n_extra, epilogue, w_rows_are_outputs):
    a_ref = refs[0]
    b_refs = refs[1:1 + n_b]
    extra_refs = refs[1 + n_b:1 + n_b + n_extra]
    o_ref = refs[1 + n_b + n_extra]
    w_scr = refs[2 + n_b + n_extra:]

    @pl.when(pl.program_id(1) == 0)
    def _():
        for b_ref, w in zip(b_refs, w_scr):
            w[...] = b_ref[...].astype(BF16)

    a = a_ref[...]
    z = [(_dot_nt if w_rows_are_outputs else _dot)(a, w[...]) for w in w_scr]
    o_ref[...] = epilogue(*z, *[e[...] for e in extra_refs]).astype(o_ref.dtype)


def _ep_scale(z, scale):
    return z * scale


def _ep_resid(z, resid):
    return resid + z


def _ep_swiglu(za, zb):
    return jax.nn.silu(za) * zb


def _ep_glu_resid(za, zb, resid):
    return resid + za * jax.nn.sigmoid(zb)


def _matmul(epilogue, a, b, layer, n_out, out_dtype, tm, tn, *, b_halves=1, row_extra=None, tile_extra=None,
            w_rows_are_outputs=False):
    m, k = a.shape
    nj = n_out // tn
    estimate = (b_halves * k * tn * (2 * 4 + 2) + 2 * tm * k * 2 + 4 * tm * tn * 4 + b_halves * tm * tn * 4)
    b_mode = pl.Buffered(1) if estimate > VMEM_LIMIT else None
    in_specs = [pl.BlockSpec((tm, k), lambda j, i: (i, 0))]
    args = [a]
    for half in range(b_halves):
        if w_rows_are_outputs:
            spec = pl.BlockSpec((None, tn, k), functools.partial(lambda j, i, o: (layer, j + o, 0), o=half * nj),
                                pipeline_mode=b_mode)
        else:
            spec = pl.BlockSpec((None, k, tn), functools.partial(lambda j, i, o: (layer, 0, j + o), o=half * nj),
                                pipeline_mode=b_mode)
        in_specs.append(spec)
        args.append(b)
    extras = []
    if row_extra is not None:
        in_specs.append(pl.BlockSpec((1, tn), lambda j, i: (0, j)))
        extras.append(row_extra)
    if tile_extra is not None:
        in_specs.append(pl.BlockSpec((tm, tn), lambda j, i: (i, j)))
        extras.append(tile_extra)
    kern = functools.partial(_mm_kernel, n_b=b_halves, n_extra=len(extras), epilogue=epilogue,
                             w_rows_are_outputs=w_rows_are_outputs)
    return pl.pallas_call(
        kern,
        out_shape=jax.ShapeDtypeStruct((m, n_out), out_dtype),
        grid=(nj, m // tm),
        in_specs=in_specs,
        out_specs=pl.BlockSpec((tm, tn), lambda j, i: (i, j)),
        scratch_shapes=[pltpu.VMEM((tn, k) if w_rows_are_outputs else (k, tn), BF16) for _ in range(b_halves)],
        compiler_params=_params("arbitrary", "arbitrary"),
    )(*args, *extras)


def _mm_resid_norm_kernel(a_ref, b_ref, r_ref, g_ref, h_ref, hn_ref, w_scr):
    @pl.when(pl.program_id(0) == 0)
    def _():
        w_scr[...] = b_ref[...].astype(BF16)

    h = r_ref[...] + _dot(a_ref[...], w_scr[...])
    h_ref[...] = h
    ms = jnp.mean(h * h, axis=-1, keepdims=True)
    hn_ref[...] = (h * lax.rsqrt(ms + RMS_EPS) * g_ref[...]).astype(hn_ref.dtype)


def _matmul_resid_norm(a, b, layer, resid, g, tm):
    m, k = a.shape
    n = b.shape[2]
    return pl.pallas_call(
        _mm_resid_norm_kernel,
        out_shape=(jax.ShapeDtypeStruct((m, n), F32), jax.ShapeDtypeStruct((m, n), BF16)),
        grid=(m // tm,),
        in_specs=[
            pl.BlockSpec((tm, k), lambda i: (i, 0)),
            pl.BlockSpec((None, k, n), lambda i: (layer, 0, 0), pipeline_mode=pl.Buffered(1)),
            pl.BlockSpec((tm, n), lambda i: (i, 0)),
            pl.BlockSpec((1, n), lambda i: (0, 0)),
        ],
        out_specs=(pl.BlockSpec((tm, n), lambda i: (i, 0)), pl.BlockSpec((tm, n), lambda i: (i, 0))),
        scratch_shapes=[pltpu.VMEM((k, n), BF16)],
        compiler_params=_params("arbitrary"),
    )(a, b, resid, g.reshape(1, n))


def _swiglu_ffn(h, hn, w_in, w_out, layer):
    act = _matmul(_ep_swiglu, hn, w_in, layer, D_FF, BF16, 1024, 512, b_halves=2)
    return _matmul(_ep_resid, act, w_out, layer, D_MODEL, F32, 512, 512, tile_extra=h)


def _compress_kernel(x_ref, w1_ref, w2_ref, pos_ref, o_ref):
    ncp = x_ref.shape[0] // CMP_STRIDE
    half = CMP_STRIDE * HEAD_DIM
    by_offset = jnp.swapaxes(x_ref[...].reshape(ncp, CMP_STRIDE, HEAD_DIM), 0, 1)
    rows = jnp.concatenate([by_offset[r] for r in range(CMP_STRIDE)], axis=1)
    top = _dot(rows, w1_ref[0, :half, :])
    bot = _dot(rows, w1_ref[0, half:, :])
    posb = _dot(pos_ref[0], w1_ref[0])[0:1]
    hid = top + pltpu.roll(bot, ncp - 1, 0) + posb
    o_ref[0, 0] = _dot(jax.nn.gelu(hid).astype(BF16), w2_ref[0]).astype(o_ref.dtype)


def _compress(proj, w1, w2, pos):
    s = proj.shape[0]
    ncp = s // CMP_STRIDE
    kc_blk = Q_WIDTH // HEAD_DIM
    return pl.pallas_call(
        _compress_kernel,
        out_shape=jax.ShapeDtypeStruct((2, N_KV_GROUPS, ncp, HEAD_DIM), BF16),
        grid=(2, N_KV_GROUPS),
        in_specs=[
            pl.BlockSpec((s, HEAD_DIM), lambda kv, gi: (0, kc_blk + kv * N_KV_GROUPS + gi)),
            pl.BlockSpec((1, CMP_BLOCK * HEAD_DIM, CMP_HIDDEN), lambda kv, gi: (kv, 0, 0)),
            pl.BlockSpec((1, CMP_HIDDEN, HEAD_DIM), lambda kv, gi: (kv, 0, 0)),
            pl.BlockSpec((1, SUBLANES, CMP_BLOCK * HEAD_DIM), lambda kv, gi: (kv, 0, 0)),
        ],
        out_specs=pl.BlockSpec((1, 1, ncp, HEAD_DIM), lambda kv, gi: (kv, gi, 0, 0)),
        compiler_params=_params("parallel", "parallel"),
    )(proj, w1, w2, pos)


def _first_argmax(score):
    n_rows, n = score.shape
    sub = lax.broadcasted_iota(jnp.int32, (SUBLANES, n), 0)
    pairs = [(score[r:r + SUBLANES], sub + r) for r in range(0, n_rows, SUBLANES)]
    while len(pairs) > 2:
        merged = []
        for k in range(0, len(pairs) - 1, 2):
            (va, ia), (vb, ib) = pairs[k], pairs[k + 1]
            take = vb > va
            merged.append((jnp.where(take, vb, va), jnp.where(take, ib, ia)))
        if len(pairs) % 2:
            merged.append(pairs[-1])
        pairs = merged
    val = jnp.concatenate([v for v, _ in pairs], axis=0)
    idx = jnp.concatenate([i for _, i in pairs], axis=0)
    best = jnp.max(val, axis=0, keepdims=True)
    return jnp.min(jnp.where(val == best, idx, n_rows), axis=0, keepdims=True)


def _cmp_attn_kernel(q_ref, kcc_ref, vcct_ref, band_ref, mt_ref, oc_ref, nm_ref, s_scr, *, tq, nb):
    i = pl.program_id(1)
    band_start = pl.multiple_of(i * (tq // CMP_STRIDE), SUBLANES)
    n_rows = kcc_ref.shape[1]
    n_blk = mt_ref.shape[0]

    def body(n_r, n_b):
        kcc = kcc_ref[0, :n_r, :]
        vcct = vcct_ref[0, :, :n_r]
        rows = lax.broadcasted_iota(jnp.int32, (n_r, tq), 0)
        row_bias = jnp.where(rows >= CMP_PAD, jnp.where(rows < band_start + nb, 0.0, NEG), NEG)
        imp = jnp.zeros((n_r, tq), F32)

        def scores(h):
            qh = q_ref[:, h * HEAD_DIM:(h + 1) * HEAD_DIM]
            s_scr[h, :n_r, :] = _dot_nt(kcc, qh) + row_bias
            s_scr[h, pl.ds(band_start, nb), :] += band_ref[h]

        scores(0)
        probs = []
        for h in range(HPG):
            if h + 1 < HPG:
                scores(h + 1)
            s = s_scr[h, :n_r, :]
            m = jnp.maximum(jnp.max(s, axis=0, keepdims=True), 0.5 * NEG)
            e = jnp.exp2(s - m)
            l = jnp.sum(e, axis=0, keepdims=True)
            pn = e * (1.0 / jnp.where(l > 0.0, l, 1.0))
            imp = imp + pn
            probs.append(pn.astype(BF16))

        def outputs():
            for h in range(HPG):
                oct_h = _dot(vcct, probs[h])
                oc_ref[:, h * HEAD_DIM:(h + 1) * HEAD_DIM] = oct_h.T
                yield
        mt = mt_ref[:n_b, :n_r]
        hi = imp.astype(BF16)
        rest = imp - hi.astype(F32)
        mid = rest.astype(BF16)
        lo = (rest - mid.astype(F32)).astype(BF16)
        slc = _dot(mt, hi) + _dot(mt, mid) + _dot(mt, lo)
        jj = lax.broadcasted_iota(jnp.int32, (n_b, tq), 0)
        t_blk = (i * tq + lax.broadcasted_iota(jnp.int32, (n_b, tq), 1)) // SEL_BLOCK
        forced = jnp.where(jj == 0, 1, jnp.where(jj == t_blk, 1, jnp.where(jj == t_blk - 1, 1, 0)))
        picked = jnp.where(forced == 1, 1.0, 0.0)
        score = jnp.where(forced == 1, -jnp.inf, jnp.where(jj <= t_blk, slc, NEG))
        pending = outputs()
        for r in range(SEL_TOPK - N_FORCED):
            hit = jj == _first_argmax(score)
            picked = jnp.where(hit, 1.0, picked)
            score = jnp.where(hit, -jnp.inf, score)
            if r % 3 == 0:
                next(pending, None)
        for _ in pending:
            pass
        neg_mask = jnp.where(picked > 0.5, jnp.where(jj <= t_blk, 0.0, NEG), NEG)
        if n_b < n_blk:
            neg_mask = jnp.concatenate([neg_mask, jnp.full((n_blk - n_b, tq), NEG, F32)], axis=0)
        nm_ref[...] = neg_mask.T.astype(nm_ref.dtype)

    step = LANES
    extents = list(range(step, n_rows, step)) + [n_rows]
    need_rows = band_start + nb
    for v, n_r in enumerate(extents):
        lower = extents[v - 1] if v else 0
        n_b = min(n_blk, -(-(n_r * CMP_STRIDE // SEL_BLOCK) // SUBLANES) * SUBLANES)
        pl.when(jnp.logical_and(need_rows > lower, need_rows <= n_r))(functools.partial(body, n_r, n_b))


def _cmp_attn(proj, kcc, vcct, band, mt, tq):
    s = proj.shape[0]
    n_rows = kcc.shape[1]
    nb = band.shape[1]
    kern = functools.partial(_cmp_attn_kernel, tq=tq, nb=nb)
    return pl.pallas_call(
        kern,
        out_shape=(jax.ShapeDtypeStruct((s, Q_WIDTH), F32),
                   jax.ShapeDtypeStruct((s, N_KV_GROUPS * LANES), BF16)),
        grid=(N_KV_GROUPS, s // tq),
        in_specs=[
            pl.BlockSpec((tq, HPG * HEAD_DIM), lambda g, i: (i, g)),
            pl.BlockSpec((1, n_rows, HEAD_DIM), lambda g, i: (g, 0, 0)),
            pl.BlockSpec((1, HEAD_DIM, n_rows), lambda g, i: (g, 0, 0)),
            pl.BlockSpec((HPG, nb, tq), lambda g, i: (g, 0, 0)),
            pl.BlockSpec((LANES, n_rows), lambda g, i: (0, 0)),
        ],
        out_specs=(pl.BlockSpec((tq, HPG * HEAD_DIM), lambda g, i: (i, g)),
                   pl.BlockSpec((tq, LANES), lambda g, i: (i, g))),
        scratch_shapes=[pltpu.VMEM((HPG, n_rows, tq), F32)],
        compiler_params=_params("parallel", "parallel"),
    )(proj, kcc, vcct, band, mt)


def _lane_tile(x, width):
    return jnp.concatenate([x] * (width // x.shape[1]), axis=1)


def _softmax_update(s, v_aug, m_ref, acc_ref, rows):
    m_prev = m_ref[rows, :]
    m_next = jnp.maximum(m_prev, jnp.max(s, axis=1, keepdims=True))
    alpha = jnp.exp2(m_prev - m_next)
    p = jnp.exp2(s - _lane_tile(m_next, s.shape[1]))
    acc_ref[rows, :] = _lane_tile(alpha, acc_ref.shape[1]) * acc_ref[rows, :] + _dot(p.astype(BF16), v_aug)
    m_ref[rows, :] = m_next


def _softmax_once(s, v_aug):
    m = jnp.broadcast_to(jnp.max(s, axis=1, keepdims=True), (s.shape[0], LANES))
    p = jnp.exp2(s - _lane_tile(m, s.shape[1]))
    return _dot(p.astype(BF16), v_aug)


def _sel_win_kernel(q_ref, nm_ref, ks_ref, vs_ref, kw_ref, vw_ref, blk_ref, per_ref, g_ref, pick_ref, oc_ref, o_ref,
                    q4, m_s, a_s, bias_ref, gate_scr, *, tile):
    step = pl.program_id(1)
    t = tile
    first = step * Q_TILES

    @pl.when(step == 0)
    def _():
        tl = lax.broadcasted_iota(jnp.int32, (t, t), 0)
        kl = lax.broadcasted_iota(jnp.int32, (t, t), 1)
        edge = jnp.where(tl < kl, 0.0, NEG)
        for h in range(HPG):
            bias_ref[h, :, :t] = edge
            for which in range(2):
                rows = jnp.broadcast_to(per_ref[h, which:which + 1, :], (t, 2 * t))
                table = pltpu.roll(rows, 0, 1, stride=1, stride_axis=0)
                bias_ref[h, :, (which + 1) * t:(which + 2) * t] = table[:, :t]

    def stacked(qt, h):
        return slice((qt * HPG + h) * t, (qt * HPG + h + 1) * t)

    def tile_rows(qt):
        return slice(qt * t, (qt + 1) * t)

    for qt in range(Q_TILES):
        for h in range(HPG):
            q4[stacked(qt, h), :HEAD_DIM] = q_ref[tile_rows(qt), h * HEAD_DIM:(h + 1) * HEAD_DIM]
            q4[stacked(qt, h), HEAD_DIM:] = nm_ref[tile_rows(qt), :]
    m_s[...] = jnp.full(m_s.shape, -3e38, F32)
    a_s[...] = jnp.zeros(a_s.shape, F32)
    g_hi = g_ref[...].astype(BF16)
    g_rest = g_ref[...] - g_hi.astype(F32)
    g_mid = g_rest.astype(BF16)
    g_lo = (g_rest - g_mid.astype(F32)).astype(BF16)
    gate_scr[...] = _dot(g_hi, pick_ref[0]) + _dot(g_mid, pick_ref[0]) + _dot(g_lo, pick_ref[0])

    def sel_operands(row0, n_tiles):
        keys = pl.ds(row0, n_tiles * t)
        k_aug = jnp.concatenate([ks_ref[keys, :], blk_ref[keys, :]], axis=1)
        v_aug = jnp.concatenate([vs_ref[keys, :], jnp.ones((n_tiles * t, LANES), BF16)], axis=1)
        return k_aug, v_aug

    def sel_pass(qt, operands, n_biased):
        k_aug, v_aug = operands

        def score(h):
            s = _dot_nt(q4[stacked(qt, h), :], k_aug)
            plain = s.shape[1] - n_biased * t
            if n_biased:
                near = s[:, plain:] + bias_ref[h, :, (3 - n_biased) * t:]
                s = jnp.concatenate([s[:, :plain], near], axis=1) if plain else near
            return s

        def consume(h, s):
            _softmax_update(s, v_aug, m_s, a_s, stacked(qt, h))

        return score, consume

    def win_pass(qt, row0, n_tiles):
        keys = pl.ds(row0, n_tiles * t)
        v_aug = jnp.concatenate([vw_ref[keys, :], jnp.ones((n_tiles * t, LANES), BF16)], axis=1)

        def score(h):
            return _dot_nt(q4[stacked(qt, h), :HEAD_DIM], kw_ref[keys, :]) + bias_ref[h, :, (3 - n_tiles) * t:]

        def consume(h, s):
            acc_w = _softmax_once(s, v_aug)
            acc_s = a_s[stacked(qt, h), :]
            o_w = acc_w[:, :HEAD_DIM] * (1.0 / acc_w[:, HEAD_DIM:])
            o_s = acc_s[:, :HEAD_DIM] * (1.0 / acc_s[:, HEAD_DIM:])
            cols = slice(h * HEAD_DIM, (h + 1) * HEAD_DIM)
            gates = gate_scr[tile_rows(qt), :]
            o = (gates[:, h:h + 1] * oc_ref[tile_rows(qt), cols] + gates[:, HPG + h:HPG + h + 1] * o_s
                 + gates[:, 2 * HPG + h:2 * HPG + h + 1] * o_w)
            o_ref[tile_rows(qt), cols] = o.astype(o_ref.dtype)

        return score, consume

    def run(passes):
        work = [(score, consume, h0) for score, consume in passes for h0 in range(0, HPG, 2)]
        ahead = [work[0][0](h) for h in (0, 1)]
        for n, (_, consume, h0) in enumerate(work):
            now = ahead
            if n + 1 < len(work):
                nxt_score, _, nxt_h0 = work[n + 1]
                ahead = [nxt_score(h) for h in (nxt_h0, nxt_h0 + 1)]
            for h, s in zip((h0, h0 + 1), now):
                consume(h, s)

    def far_passes(row0):
        operands = sel_operands(row0, FAR_TILES)
        return [sel_pass(qt, operands, 0) for qt in range(Q_TILES)]

    n_far = jnp.maximum(first - 1, 0)
    n_chunks = n_far // FAR_TILES
    chunk = FAR_TILES * t

    def far_pair(c, carry):
        row0 = pl.multiple_of(c * 2 * chunk, 2 * chunk)
        run(far_passes(row0) + far_passes(row0 + chunk))
        return carry

    lax.fori_loop(0, n_chunks // 2, far_pair, 0)

    @pl.when(n_chunks % 2 == 1)
    def _():
        run(far_passes(pl.multiple_of((n_chunks - 1) * chunk, chunk)))

    def tail(left):
        row0 = pl.multiple_of((n_far - left) * t, t)
        passes = []
        for qt in range(Q_TILES):
            n_sel = left + 2 + qt
            passes.append(sel_pass(qt, sel_operands(row0, n_sel), 2))
            passes.append(win_pass(qt, pl.multiple_of((first + qt - 2) * t, t), 3))
        run(passes)

    for left in range(FAR_TILES):
        if (left + 1) % Q_TILES == 0:
            pl.when(jnp.logical_and(step >= 1, n_far % FAR_TILES == left))(functools.partial(tail, left))

    @pl.when(step == 0)
    def _():
        passes = []
        for qt in range(Q_TILES):
            n_keys = min(qt + 1, 3)
            passes.append(sel_pass(qt, sel_operands(0, qt + 1), min(qt + 1, 2)))
            passes.append(win_pass(qt, (qt + 1 - n_keys) * t, n_keys))
        run(passes)


def _sel_win(proj, neg_mask, blk_onehot, periods, gates, gate_pick, o_c, tile):
    s = proj.shape[0]
    ks_blk = (Q_WIDTH + 2 * KV_WIDTH) // HEAD_DIM
    vs_blk = (Q_WIDTH + 3 * KV_WIDTH) // HEAD_DIM
    kw_blk = (Q_WIDTH + 4 * KV_WIDTH) // HEAD_DIM
    vw_blk = (Q_WIDTH + 5 * KV_WIDTH) // HEAD_DIM
    once = pl.Buffered(1)
    rows = Q_TILES * tile
    stacked_rows = HPG * rows
    kern = functools.partial(_sel_win_kernel, tile=tile)
    return pl.pallas_call(
        kern,
        out_shape=jax.ShapeDtypeStruct((s, Q_WIDTH), BF16),
        grid=(N_KV_GROUPS, s // rows),
        in_specs=[
            pl.BlockSpec((rows, HPG * HEAD_DIM), lambda g, i: (i, g)),
            pl.BlockSpec((rows, LANES), lambda g, i: (i, g)),
            pl.BlockSpec((s, HEAD_DIM), lambda g, i: (0, ks_blk + g), pipeline_mode=once),
            pl.BlockSpec((s, HEAD_DIM), lambda g, i: (0, vs_blk + g), pipeline_mode=once),
            pl.BlockSpec((s, HEAD_DIM), lambda g, i: (0, kw_blk + g), pipeline_mode=once),
            pl.BlockSpec((s, HEAD_DIM), lambda g, i: (0, vw_blk + g), pipeline_mode=once),
            pl.BlockSpec((s, LANES), lambda g, i: (0, 0), pipeline_mode=once),
            pl.BlockSpec((HPG, 2, 2 * tile), lambda g, i: (g, 0, 0)),
            pl.BlockSpec((rows, LANES), lambda g, i: (i, 0)),
            pl.BlockSpec((1, LANES, LANES), lambda g, i: (g, 0, 0)),
            pl.BlockSpec((rows, HPG * HEAD_DIM), lambda g, i: (i, g)),
        ],
        out_specs=pl.BlockSpec((rows, HPG * HEAD_DIM), lambda g, i: (i, g)),
        scratch_shapes=[
            pltpu.VMEM((stacked_rows, 2 * HEAD_DIM), BF16),
            pltpu.VMEM((stacked_rows, LANES), F32),
            pltpu.VMEM((stacked_rows, 2 * HEAD_DIM), F32),
            pltpu.VMEM((HPG, tile, 3 * tile), F32),
            pltpu.VMEM((rows, LANES), F32),
        ],
        compiler_params=_params("arbitrary", "arbitrary"),
    )(proj, neg_mask, proj, proj, proj, proj, blk_onehot, periods, gates, gate_pick, o_c)


def _rel_bucket_np(dist):
    n = np.maximum(dist, 0)
    max_exact = REL_BUCKETS // 2
    large = max_exact + (np.log(np.maximum(n, 1).astype(np.float32) / np.float32(max_exact))
                         / np.float32(math.log(REL_MAX_DIST / max_exact))
                         * np.float32(REL_BUCKETS - max_exact)).astype(np.int32)
    large = np.minimum(large, REL_BUCKETS - 1)
    return np.where(n < max_exact, n, large).astype(np.int32)


def _far_distance():
    d = np.arange(4 * REL_MAX_DIST)
    b = _rel_bucket_np(d)
    assert b[-1] == REL_BUCKETS - 1
    return int(np.max(np.nonzero(b != REL_BUCKETS - 1)[0])) + 1


def _bias_table(rel_bias, dist):
    bucket = jnp.asarray(_rel_bucket_np(dist))
    shifted = (rel_bias[bucket] - rel_bias[REL_BUCKETS - 1]) * LOG2E
    tab = jnp.where(jnp.asarray(dist >= 0)[..., None], shifted, NEG)
    return jnp.moveaxis(tab, -1, 0).astype(F32)


def _bias_period(rel_bias, offset, t):
    x = np.arange(2 * t)
    return _bias_table(rel_bias, np.where(x < t, offset - x, offset + 2 * t - x))


def _nsa_mixer(h, hn, rel_bias, w_in, w_out, layer, pos_k, w1_k, w2_k, pos_v, w1_v, w2_v, next_norm_g):
    s = h.shape[0]
    n_sel = s // SEL_BLOCK
    ncp = s // CMP_STRIDE
    tile = ATT_TILE
    far = _far_distance()
    assert n_sel <= LANES and s % CMP_TQ == 0 and s % (Q_TILES * tile) == 0 and FAR_TILES % Q_TILES == 0
    assert far <= tile and far <= CMP_STRIDE * CMP_PAD - CMP_BLOCK + 1 + CMP_STRIDE

    col_scale = jnp.where(jnp.arange(PROJ_MAIN) < Q_WIDTH, HEAD_DIM ** -0.5 * LOG2E, 1.0).astype(F32)[None, :]
    w_in_t = jnp.swapaxes(w_in, 1, 2)
    proj = _matmul(_ep_scale, hn, w_in_t, layer, PROJ_MAIN, BF16, 2048, 512, row_extra=col_scale,
                   w_rows_are_outputs=True)
    w_gate = jnp.pad(w_in_t[layer, PROJ_MAIN:], ((0, LANES - N_GATES * N_HEADS), (0, 0)))[None]
    gates = _matmul(jax.nn.sigmoid, hn, w_gate, 0, LANES, F32, 2048, LANES, w_rows_are_outputs=True)
    pick = np.zeros((N_KV_GROUPS, LANES, LANES), np.float32)
    for g, n, hh in np.ndindex(N_KV_GROUPS, N_GATES, HPG):
        pick[g, n * N_HEADS + g * HPG + hh, n * HPG + hh] = 1.0
    gate_pick = jnp.asarray(pick, BF16)

    w1 = jnp.stack([w1_k, w1_v]).astype(BF16)
    w2 = jnp.stack([w2_k, w2_v]).astype(BF16)
    pos = jnp.stack([pos_k.reshape(1, -1), pos_v.reshape(1, -1)]).astype(BF16)
    pos = jnp.broadcast_to(pos, (2, SUBLANES, CMP_BLOCK * HEAD_DIM))
    cc = _compress(proj, w1, w2, pos)
    kcc = jnp.pad(cc[0], ((0, 0), (CMP_PAD, 0), (0, 0)))
    vcct = jnp.pad(cc[1], ((0, 0), (CMP_PAD, 0), (0, 0))).transpose(0, 2, 1)
    n_rows = ncp + CMP_PAD

    nb = CMP_TQ // CMP_STRIDE + CMP_PAD
    first_dist = -CMP_STRIDE * (nb - 1 - CMP_PAD) - (CMP_BLOCK - 1)
    by_dist = _bias_table(rel_bias, first_dist + np.arange(CMP_TQ + CMP_STRIDE * (nb - 1)))
    band = jnp.stack([by_dist[:, CMP_STRIDE * (nb - 1 - b):CMP_STRIDE * (nb - 1 - b) + CMP_TQ] for b in range(nb)], axis=1)
    ratio = SEL_BLOCK // CMP_STRIDE
    lo = CMP_BLOCK // CMP_STRIDE - 1
    c_of_row = np.arange(n_rows)[None, :] - CMP_PAD
    j_of = np.arange(LANES)[:, None]
    mt = ((c_of_row >= ratio * j_of - lo) & (c_of_row <= ratio * j_of + ratio - 1)
          & (c_of_row >= 0) & (c_of_row <= ncp - 2) & (j_of < n_sel)).astype(np.float32)
    o_c, neg_mask = _cmp_attn(proj, kcc, vcct, band, jnp.asarray(mt, BF16), CMP_TQ)

    onehot = (np.arange(s)[:, None] // SEL_BLOCK == np.arange(LANES)[None, :]).astype(np.float32)
    periods = jnp.stack([_bias_period(rel_bias, tile, tile), _bias_period(rel_bias, 0, tile)], axis=1)
    o = _sel_win(proj, neg_mask, jnp.asarray(onehot, BF16), periods, gates, gate_pick, o_c, tile)
    return _matmul_resid_norm(o, w_out, layer, h, next_norm_g, 256)


def _s5_kernel(x_ref, lam_ref, bt_ref, c_ref, d_ref, y_ref, vt, yt, tmask, *, chunk):
    L = chunk
    half = L // 2
    width = L * SSM_GROUP
    n_chunks = x_ref.shape[0] // L
    by_step = jnp.swapaxes(x_ref[...].reshape(n_chunks, L, LANES), 0, 1)
    for tau in range(L):
        vt[tau] = by_step[tau].T

    @pl.when(pl.program_id(0) == 0)
    def _():
        dst = lax.broadcasted_iota(jnp.int32, (width, width), 0) // SSM_GROUP
        src = lax.broadcasted_iota(jnp.int32, (width, width), 1) // SSM_GROUP
        tmask[...] = jnp.where(src <= dst, 1.0, 0.0)

    def cmul(xr, xi, yr, yi):
        return xr * yr - xi * yi, xr * yi + xi * yr

    n2 = 2 * SSM_STATE
    sub = lax.broadcasted_iota(jnp.int32, (SUBLANES, 1), 0)
    consts = jnp.where(sub == 0, 1.0, jnp.where(sub == 1, float(half + 1), jnp.where(sub == 2, float(half - 1), float(L))))
    tau_col = lax.broadcasted_iota(jnp.int32, (L, 1), 0).astype(F32)
    row = lax.broadcasted_iota(jnp.int32, (n_chunks, n2), 0)
    conj = jnp.where(lax.broadcasted_iota(jnp.int32, (1, n2), 1) < SSM_STATE, 1.0, -1.0)

    def group(g):
        lam = lam_ref[g]
        a_re, a_im = lam[0:1], lam[1:2]
        dt = jnp.exp(lam[2:3])
        log_re, log_im = a_re * dt, a_im * dt

        def cpow(e):
            mag, cos, sin = jnp.exp(e * log_re), jnp.cos(e * log_im), jnp.sin(e * log_im)
            inv = 1.0 / mag
            return mag * cos, mag * sin, inv * cos, -(inv * sin)

        kr, ki, _, _ = cpow(consts)
        pr, pi = kr[0:1] - 1.0, ki[0:1]
        den = a_re * a_re + a_im * a_im
        cf_r = (pr * a_re + pi * a_im) / den
        cf_i = (pi * a_re - pr * a_im) / den
        bt_c, bt_s = bt_ref[g, 0], bt_ref[g, 1]
        bb_c = cf_r * bt_c + cf_i * bt_s
        bb_s = cf_r * bt_s - cf_i * bt_c
        c_c, c_s = c_ref[g, 0], c_ref[g, 1]

        e1r, e1i, e2r, e2i = cpow(tau_col - half)
        e3r, e3i = cmul(e1r, e1i, kr[1:2], ki[1:2])
        e4r, e4i = cmul(e2r, e2i, kr[2:3], ki[2:3])

        def outer(er, ei, w_c, w_s):
            return (er[:, None, :] * w_c[None, :, :] + ei[:, None, :] * w_s[None, :, :]).reshape(width, n2)

        qm = outer(e1r, e1i, c_c, c_s)
        km_conj = outer(e2r, e2i, bb_c * conj, bb_s * conj)
        qc = outer(e3r, e3i, c_c, c_s)
        wz = outer(e4r, e4i, bb_c, bb_s)

        chans = pl.ds(pl.multiple_of(g * SSM_GROUP, SSM_GROUP), SSM_GROUP)
        ub = vt[:, chans, :].reshape(width, n_chunks).astype(BF16)
        yield

        tt = _dot_nt(qm.astype(BF16), km_conj.astype(BF16))
        x = _dot(wz.T.astype(BF16), ub).T
        yield
        y = _dot((tt * tmask[...]).astype(BF16), ub)
        p_r, p_i = kr[3:4], ki[3:4]
        d = 1
        while d < n_chunks:
            s = jnp.where(row >= d, pltpu.roll(x, d, 0), 0.0)
            x = x + s * p_r - pltpu.roll(s, SSM_STATE, 1) * (p_i * conj)
            p_r, p_i = cmul(p_r, p_i, p_r, p_i)
            d *= 2
        prev_conj = jnp.where(row >= 1, pltpu.roll(x, 1, 0), 0.0) * conj
        yield
        y = y + _dot_nt(qc.astype(BF16), prev_conj.astype(BF16))
        yt[:, chans, :] = y.reshape(L, SSM_GROUP, n_chunks)

    def group_batch(k, carry):
        running = [group(S5_GROUPS_PER_TRIP * k + j) for j in range(S5_GROUPS_PER_TRIP)]
        finished = object()
        while running:
            running = [gen for gen in running if next(gen, finished) is not finished]
        return carry

    lax.fori_loop(0, LANES // SSM_GROUP // S5_GROUPS_PER_TRIP, group_batch, 0)
    by_chunk = jnp.swapaxes(jnp.stack([yt[tau].T for tau in range(L)], axis=0), 0, 1)
    y = by_chunk.reshape(n_chunks * L, LANES) + d_ref[...] * x_ref[...]
    y_ref[...] = jax.nn.gelu(y).astype(y_ref.dtype)


def _s5_mixer(h, hn, a_re, a_im, log_dt, b_re, b_im, c_re, c_im, d_skip, w_glu, layer):
    s = h.shape[0]
    L = S5_CHUNK
    n_chunks = s // L
    width = L * SSM_GROUP
    gpb = LANES // SSM_GROUP
    def packed(re, im):
        return jnp.stack([jnp.concatenate([re, im], -1), jnp.concatenate([-im, re], -1)], axis=1).astype(F32)

    lam = jnp.stack([a_re, a_im, jnp.broadcast_to(log_dt[:, None], a_re.shape)], axis=1).astype(F32)
    lam = jnp.concatenate([lam, lam], axis=-1)
    bt = packed(b_re.transpose(0, 2, 1), b_im.transpose(0, 2, 1))
    cc = packed(c_re, c_im)
    dd = d_skip.astype(F32).reshape(1, D_MODEL)
    y = pl.pallas_call(
        functools.partial(_s5_kernel, chunk=L),
        out_shape=jax.ShapeDtypeStruct((s, D_MODEL), BF16),
        grid=(SSM_GROUPS // gpb,),
        in_specs=[
            pl.BlockSpec((s, LANES), lambda b: (0, b)),
            pl.BlockSpec((gpb, 3, 2 * SSM_STATE), lambda b: (b, 0, 0)),
            pl.BlockSpec((gpb, 2, SSM_GROUP, 2 * SSM_STATE), lambda b: (b, 0, 0, 0)),
            pl.BlockSpec((gpb, 2, SSM_GROUP, 2 * SSM_STATE), lambda b: (b, 0, 0, 0)),
            pl.BlockSpec((1, LANES), lambda b: (0, b)),
        ],
        out_specs=pl.BlockSpec((s, LANES), lambda b: (0, b)),
        scratch_shapes=[
            pltpu.VMEM((L, LANES, n_chunks), hn.dtype),
            pltpu.VMEM((L, LANES, n_chunks), F32),
            pltpu.VMEM((width, width), F32),
        ],
        compiler_params=_params("arbitrary"),
    )(hn, lam, bt, cc, dd)
    return _matmul(_ep_glu_resid, y, w_glu, layer, D_MODEL, F32, 1024, 512, b_halves=2, tile_extra=h)


def kernel(x, rel_bias, mix_norm_g, ffn_norm_g, final_norm_g, nsa_w_in, nsa_w_out, cmp_pos_k, cmp_w1_k, cmp_w2_k, cmp_pos_v, cmp_w1_v, cmp_w2_v, s5_A_re, s5_A_im, s5_log_dt, s5_B_re, s5_B_im, s5_C_re, s5_C_im, s5_D, s5_w_glu, ffn_w_in, ffn_w_out):
    assert x.shape[0] == 1
    h = x[0]
    hn = _rmsnorm(h, mix_norm_g[0], BF16)
    h, hn = _nsa_mixer(h, hn, rel_bias, nsa_w_in, nsa_w_out, 0, cmp_pos_k[0], cmp_w1_k[0], cmp_w2_k[0],
                       cmp_pos_v[0], cmp_w1_v[0], cmp_w2_v[0], ffn_norm_g[0])
    h = _swiglu_ffn(h, hn, ffn_w_in, ffn_w_out, 0)
    hn = _rmsnorm(h, mix_norm_g[1], BF16)
    h = _s5_mixer(h, hn, s5_A_re[0], s5_A_im[0], s5_log_dt[0], s5_B_re[0], s5_B_im[0], s5_C_re[0], s5_C_im[0],
                  s5_D[0], s5_w_glu, 0)
    h = _swiglu_ffn(h, _rmsnorm(h, ffn_norm_g[1], BF16), ffn_w_in, ffn_w_out, 1)
    return _rmsnorm(h, final_norm_g, x.dtype)[None]
```

```python
import functools
import math

import numpy as np
import jax
import jax.numpy as jnp
from jax import lax
from jax.experimental import pallas as pl
from jax.experimental.pallas import tpu as pltpu

D_MODEL = 2048
N_HEADS = 16
HEAD_DIM = 128
N_KV_GROUPS = 4
HPG = N_HEADS // N_KV_GROUPS
CMP_BLOCK = 32
CMP_STRIDE = 16
CMP_HIDDEN = 2 * HEAD_DIM
SEL_BLOCK = 64
SEL_TOPK = 16
N_FORCED = 3
WINDOW = 512
N_GATES = 3
KV_WIDTH = N_KV_GROUPS * HEAD_DIM
Q_WIDTH = N_HEADS * HEAD_DIM
PROJ_MAIN = Q_WIDTH + 6 * KV_WIDTH
REL_BUCKETS = 32
REL_MAX_DIST = 128
SSM_GROUP = 16
SSM_GROUPS = D_MODEL // SSM_GROUP
SSM_STATE = 64
D_FF = ((8 * D_MODEL + 2) // 3 + 255) // 256 * 256
RMS_EPS = 1e-6
NEG = -1e30
LOG2E = math.log2(math.e)

LANES = 128
SUBLANES = 8
VMEM_LIMIT = 52 * 1024 * 1024
RING_VMEM_LIMIT = 57 * 1024 * 1024

ATT_TILE = WINDOW // 2
FAR_TILES = 4
Q_TILES = 2
CMP_TQ = 256
CMP_PAD = 8
S5_CHUNK = 32
S5_GROUPS_PER_TRIP = 4

BF16 = jnp.bfloat16
F32 = jnp.float32


def _dot(a, b):
    return jnp.dot(a, b, preferred_element_type=F32)


def _dot_nt(a, b):
    return lax.dot_general(a, b, (((1,), (1,)), ((), ())), preferred_element_type=F32)


def _params(*sem, flags=None):
    return pltpu.CompilerParams(dimension_semantics=sem, vmem_limit_bytes=VMEM_LIMIT, flags=flags)


def _rmsnorm_kernel(x_ref, g_ref, o_ref):
    x = x_ref[...]
    ms = jnp.mean(x * x, axis=-1, keepdims=True)
    o_ref[...] = (x * lax.rsqrt(ms + RMS_EPS) * g_ref[...]).astype(o_ref.dtype)


def _rmsnorm(x, g, out_dtype, tm=1024):
    s, d = x.shape
    return pl.pallas_call(
        _rmsnorm_kernel,
        out_shape=jax.ShapeDtypeStruct((s, d), out_dtype),
        grid=(s // tm,),
        in_specs=[pl.BlockSpec((tm, d), lambda i: (i, 0)), pl.BlockSpec((1, d), lambda i: (0, 0))],
        out_specs=pl.BlockSpec((tm, d), lambda i: (i, 0)),
        compiler_params=_params("parallel"),
    )(x, g.reshape(1, d))


def _mm_kernel(*refs, n_b, n_extra, epilogue, w_rows_are_outputs):
    a_ref = refs[0]
    b_refs = refs[1:1 + n_b]
    extra_refs = refs[1 + n_b:1 + n_b + n_extra]
    o_ref = refs[1 + n_b + n_extra]
    w_scr = refs[2 + n_b + n_extra:]

    @pl.when(pl.program_id(1) == 0)
    def _():
        for b_ref, w in zip(b_refs, w_scr):
            w[...] = b_ref[...].astype(BF16)

    a = a_ref[...]
    z = [(_dot_nt if w_rows_are_outputs else _dot)(a, w[...]) for w in w_scr]
    o_ref[...] = epilogue(*z, *[e[...] for e in extra_refs]).astype(o_ref.dtype)


def _ep_scale(z, scale):
    return z * scale


def _ep_resid(z, resid):
    return resid + z


def _ep_swiglu(za, zb):
    return jax.nn.silu(za) * zb


def _ep_glu_resid(za, zb, resid):
    return resid + za * jax.nn.sigmoid(zb)


def _matmul(epilogue, a, b, layer, n_out, out_dtype, tm, tn, *, b_halves=1, row_extra=None, tile_extra=None,
            w_rows_are_outputs=False):
    m, k = a.shape
    nj = n_out // tn
    estimate = (b_halves * k * tn * (2 * 4 + 2) + 2 * tm * k * 2 + 4 * tm * tn * 4 + b_halves * tm * tn * 4)
    b_mode = pl.Buffered(1) if estimate > VMEM_LIMIT else None
    in_specs = [pl.BlockSpec((tm, k), lambda j, i: (i, 0))]
    args = [a]
    for half in range(b_halves):
        if w_rows_are_outputs:
            spec = pl.BlockSpec((None, tn, k), functools.partial(lambda j, i, o: (layer, j + o, 0), o=half * nj),
                                pipeline_mode=b_mode)
        else:
            spec = pl.BlockSpec((None, k, tn), functools.partial(lambda j, i, o: (layer, 0, j + o), o=half * nj),
                                pipeline_mode=b_mode)
        in_specs.append(spec)
        args.append(b)
    extras = []
    if row_extra is not None:
        in_specs.append(pl.BlockSpec((1, tn), lambda j, i: (0, j)))
        extras.append(row_extra)
    if tile_extra is not None:
        in_specs.append(pl.BlockSpec((tm, tn), lambda j, i: (i, j)))
        extras.append(tile_extra)
    kern = functools.partial(_mm_kernel, n_b=b_halves, n_extra=len(extras), epilogue=epilogue,
                             w_rows_are_outputs=w_rows_are_outputs)
    return pl.pallas_call(
        kern,
        out_shape=jax.ShapeDtypeStruct((m, n_out), out_dtype),
        grid=(nj, m // tm),
        in_specs=in_specs,
        out_specs=pl.BlockSpec((tm, tn), lambda j, i: (i, j)),
        scratch_shapes=[pltpu.VMEM((tn, k) if w_rows_are_outputs else (k, tn), BF16) for _ in range(b_halves)],
        compiler_params=_params("arbitrary", "arbitrary"),
    )(*args, *extras)


def _mm_resid_norm_kernel(a_ref, b_ref, r_ref, g_ref, h_ref, hn_ref, w_scr):
    @pl.when(pl.program_id(0) == 0)
    def _():
        w_scr[...] = b_ref[...].astype(BF16)

    h = r_ref[...] + _dot(a_ref[...], w_scr[...])
    h_ref[...] = h
    ms = jnp.mean(h * h, axis=-1, keepdims=True)
    hn_ref[...] = (h * lax.rsqrt(ms + RMS_EPS) * g_ref[...]).astype(hn_ref.dtype)


def _matmul_resid_norm(a, b, layer, resid, g, tm):
    m, k = a.shape
    n = b.shape[2]
    return pl.pallas_call(
        _mm_resid_norm_kernel,
        out_shape=(jax.ShapeDtypeStruct((m, n), F32), jax.ShapeDtypeStruct((m, n), BF16)),
        grid=(m // tm,),
        in_specs=[
            pl.BlockSpec((tm, k), lambda i: (i, 0)),
            pl.BlockSpec((None, k, n), lambda i: (layer, 0, 0), pipeline_mode=pl.Buffered(1)),
            pl.BlockSpec((tm, n), lambda i: (i, 0)),
            pl.BlockSpec((1, n), lambda i: (0, 0)),
        ],
        out_specs=(pl.BlockSpec((tm, n), lambda i: (i, 0)), pl.BlockSpec((tm, n), lambda i: (i, 0))),
        scratch_shapes=[pltpu.VMEM((k, n), BF16)],
        compiler_params=_params("arbitrary"),
    )(a, b, resid, g.reshape(1, n))


RING = 3


def _mm_resid_ring_kernel(a_hbm, b_ref, r_ref, o_ref, w_scr, a_ring, sems, *, tm, n_row_tiles, n_steps):
    i = pl.program_id(1)
    step = pl.program_id(0) * n_row_tiles + i

    def tile_copy(s):
        rows = pl.ds(pl.multiple_of((s % n_row_tiles) * tm, tm), tm)
        return pltpu.make_async_copy(a_hbm.at[rows, :], a_ring.at[s % RING], sems.at[s % RING])

    @pl.when(step == 0)
    def _():
        for s in range(RING - 1):
            tile_copy(s).start()

    @pl.when(step + RING - 1 < n_steps)
    def _():
        tile_copy(step + RING - 1).start()

    @pl.when(i == 0)
    def _():
        w_scr[...] = b_ref[...].astype(BF16)

    tile_copy(step).wait()
    o_ref[...] = r_ref[...] + _dot(a_ring[step % RING], w_scr[...])


def _matmul_resid_ring(a, b, layer, resid, tm, tn):
    m, k = a.shape
    n = b.shape[2]
    n_row_tiles = m // tm
    n_steps = (n // tn) * n_row_tiles
    assert n_steps >= RING - 1
    kern = functools.partial(_mm_resid_ring_kernel, tm=tm, n_row_tiles=n_row_tiles, n_steps=n_steps)
    return pl.pallas_call(
        kern,
        out_shape=jax.ShapeDtypeStruct((m, n), F32),
        grid=(n // tn, n_row_tiles),
        in_specs=[
            pl.BlockSpec(memory_space=pl.ANY),
            pl.BlockSpec((None, k, tn), lambda j, i: (layer, 0, j)),
            pl.BlockSpec((tm, tn), lambda j, i: (i, j)),
        ],
        out_specs=pl.BlockSpec((tm, tn), lambda j, i: (i, j)),
        scratch_shapes=[pltpu.VMEM((k, tn), BF16), pltpu.VMEM((RING, tm, k), BF16), pltpu.SemaphoreType.DMA((RING,))],
        compiler_params=pltpu.CompilerParams(dimension_semantics=("arbitrary", "arbitrary"),
                                             vmem_limit_bytes=RING_VMEM_LIMIT),
    )(a, b, resid)


def _swiglu_ffn(h, hn, w_in, w_out, layer):
    act = _matmul(_ep_swiglu, hn, w_in, layer, D_FF, BF16, 1024, 512, b_halves=2)
    return _matmul_resid_ring(act, w_out, layer, h, 512, 512)


def _compress_kernel(x_ref, w1_ref, w2_ref, pos_ref, o_ref):
    ncp = x_ref.shape[0] // CMP_STRIDE
    half = CMP_STRIDE * HEAD_DIM
    by_offset = jnp.swapaxes(x_ref[...].reshape(ncp, CMP_STRIDE, HEAD_DIM), 0, 1)
    rows = jnp.concatenate([by_offset[r] for r in range(CMP_STRIDE)], axis=1)
    top = _dot(rows, w1_ref[0, :half, :])
    bot = _dot(rows, w1_ref[0, half:, :])
    posb = _dot(pos_ref[0], w1_ref[0])[0:1]
    hid = top + pltpu.roll(bot, ncp - 1, 0) + posb
    o_ref[0, 0] = _dot(jax.nn.gelu(hid).astype(BF16), w2_ref[0]).astype(o_ref.dtype)


def _compress(proj, w1, w2, pos):
    s = proj.shape[0]
    ncp = s // CMP_STRIDE
    kc_blk = Q_WIDTH // HEAD_DIM
    return pl.pallas_call(
        _compress_kernel,
        out_shape=jax.ShapeDtypeStruct((2, N_KV_GROUPS, ncp, HEAD_DIM), BF16),
        grid=(2, N_KV_GROUPS),
        in_specs=[
            pl.BlockSpec((s, HEAD_DIM), lambda kv, gi: (0, kc_blk + kv * N_KV_GROUPS + gi)),
            pl.BlockSpec((1, CMP_BLOCK * HEAD_DIM, CMP_HIDDEN), lambda kv, gi: (kv, 0, 0)),
            pl.BlockSpec((1, CMP_HIDDEN, HEAD_DIM), lambda kv, gi: (kv, 0, 0)),
            pl.BlockSpec((1, SUBLANES, CMP_BLOCK * HEAD_DIM), lambda kv, gi: (kv, 0, 0)),
        ],
        out_specs=pl.BlockSpec((1, 1, ncp, HEAD_DIM), lambda kv, gi: (kv, gi, 0, 0)),
        compiler_params=_params("parallel", "parallel"),
    )(proj, w1, w2, pos)


def _first_argmax(score):
    n_rows, n = score.shape
    sub = lax.broadcasted_iota(jnp.int32, (SUBLANES, n), 0)
    pairs = [(score[r:r + SUBLANES], sub + r) for r in range(0, n_rows, SUBLANES)]
    while len(pairs) > 2:
        merged = []
        for k in range(0, len(pairs) - 1, 2):
            (va, ia), (vb, ib) = pairs[k], pairs[k + 1]
            take = vb > va
            merged.append((jnp.where(take, vb, va), jnp.where(take, ib, ia)))
        if len(pairs) % 2:
            merged.append(pairs[-1])
        pairs = merged
    val = jnp.concatenate([v for v, _ in pairs], axis=0)
    idx = jnp.concatenate([i for _, i in pairs], axis=0)
    best = jnp.max(val, axis=0, keepdims=True)
    return jnp.min(jnp.where(val == best, idx, n_rows), axis=0, keepdims=True)


def _cmp_attn_kernel(q_ref, kcc_ref, vcct_ref, band_ref, mt_ref, oc_ref, nm_ref, s_scr, *, tq, nb):
    i = pl.program_id(1)
    band_start = pl.multiple_of(i * (tq // CMP_STRIDE), SUBLANES)
    n_rows = kcc_ref.shape[1]
    n_blk = mt_ref.shape[0]

    def body(n_r, n_b):
        kcc = kcc_ref[0, :n_r, :]
        vcct = vcct_ref[0, :, :n_r]
        rows = lax.broadcasted_iota(jnp.int32, (n_r, tq), 0)
        row_bias = jnp.where(rows >= CMP_PAD, jnp.where(rows < band_start + nb, 0.0, NEG), NEG)
        imp = jnp.zeros((n_r, tq), F32)

        def scores(h):
            qh = q_ref[:, h * HEAD_DIM:(h + 1) * HEAD_DIM]
            s_scr[h, :n_r, :] = _dot_nt(kcc, qh) + row_bias
            s_scr[h, pl.ds(band_start, nb), :] += band_ref[h]

        scores(0)
        probs = []
        for h in range(HPG):
            if h + 1 < HPG:
                scores(h + 1)
            s = s_scr[h, :n_r, :]
            m = jnp.maximum(jnp.max(s, axis=0, keepdims=True), 0.5 * NEG)
            e = jnp.exp2(s - m)
            l = jnp.sum(e, axis=0, keepdims=True)
            pn = e * (1.0 / jnp.where(l > 0.0, l, 1.0))
            imp = imp + pn
            probs.append(pn.astype(BF16))

        def outputs():
            for h in range(HPG):
                oct_h = _dot(vcct, probs[h])
                oc_ref[:, h * HEAD_DIM:(h + 1) * HEAD_DIM] = oct_h.T
                yield
        mt = mt_ref[:n_b, :n_r]
        hi = imp.astype(BF16)
        rest = imp - hi.astype(F32)
        mid = rest.astype(BF16)
        lo = (rest - mid.astype(F32)).astype(BF16)
        slc = _dot(mt, hi) + _dot(mt, mid) + _dot(mt, lo)
        jj = lax.broadcasted_iota(jnp.int32, (n_b, tq), 0)
        t_blk = (i * tq + lax.broadcasted_iota(jnp.int32, (n_b, tq), 1)) // SEL_BLOCK
        forced = jnp.where(jj == 0, 1, jnp.where(jj == t_blk, 1, jnp.where(jj == t_blk - 1, 1, 0)))
        picked = jnp.where(forced == 1, 1.0, 0.0)
        score = jnp.where(forced == 1, -jnp.inf, jnp.where(jj <= t_blk, slc, NEG))
        pending = outputs()
        for r in range(SEL_TOPK - N_FORCED):
            hit = jj == _first_argmax(score)
            picked = jnp.where(hit, 1.0, picked)
            score = jnp.where(hit, -jnp.inf, score)
            if r % 3 == 0:
                next(pending, None)
        for _ in pending:
            pass
        neg_mask = jnp.where(picked > 0.5, jnp.where(jj <= t_blk, 0.0, NEG), NEG)
        if n_b < n_blk:
            neg_mask = jnp.concatenate([neg_mask, jnp.full((n_blk - n_b, tq), NEG, F32)], axis=0)
        nm_ref[...] = neg_mask.T.astype(nm_ref.dtype)

    step = LANES
    extents = list(range(step, n_rows, step)) + [n_rows]
    need_rows = band_start + nb
    for v, n_r in enumerate(extents):
        lower = extents[v - 1] if v else 0
        n_b = min(n_blk, -(-(n_r * CMP_STRIDE // SEL_BLOCK) // SUBLANES) * SUBLANES)
        pl.when(jnp.logical_and(need_rows > lower, need_rows <= n_r))(functools.partial(body, n_r, n_b))


def _cmp_attn(proj, kcc, vcct, band, mt, tq):
    s = proj.shape[0]
    n_rows = kcc.shape[1]
    nb = band.shape[1]
    kern = functools.partial(_cmp_attn_kernel, tq=tq, nb=nb)
    return pl.pallas_call(
        kern,
        out_shape=(jax.ShapeDtypeStruct((s, Q_WIDTH), F32),
                   jax.ShapeDtypeStruct((s, N_KV_GROUPS * LANES), BF16)),
        grid=(N_KV_GROUPS, s // tq),
        in_specs=[
            pl.BlockSpec((tq, HPG * HEAD_DIM), lambda g, i: (i, g)),
            pl.BlockSpec((1, n_rows, HEAD_DIM), lambda g, i: (g, 0, 0)),
            pl.BlockSpec((1, HEAD_DIM, n_rows), lambda g, i: (g, 0, 0)),
            pl.BlockSpec((HPG, nb, tq), lambda g, i: (g, 0, 0)),
            pl.BlockSpec((LANES, n_rows), lambda g, i: (0, 0)),
        ],
        out_specs=(pl.BlockSpec((tq, HPG * HEAD_DIM), lambda g, i: (i, g)),
                   pl.BlockSpec((tq, LANES), lambda g, i: (i, g))),
        scratch_shapes=[pltpu.VMEM((HPG, n_rows, tq), F32)],
        compiler_params=_params("parallel", "parallel"),
    )(proj, kcc, vcct, band, mt)


def _lane_tile(x, width):
    return jnp.concatenate([x] * (width // x.shape[1]), axis=1)


def _softmax_update(s, v_aug, m_ref, acc_ref, rows):
    m_prev = m_ref[rows, :]
    m_next = jnp.maximum(m_prev, jnp.max(s, axis=1, keepdims=True))
    alpha = jnp.exp2(m_prev - m_next)
    p = jnp.exp2(s - _lane_tile(m_next, s.shape[1]))
    acc_ref[rows, :] = _lane_tile(alpha, acc_ref.shape[1]) * acc_ref[rows, :] + _dot(p.astype(BF16), v_aug)
    m_ref[rows, :] = m_next


def _softmax_once(s, v_aug):
    m = jnp.broadcast_to(jnp.max(s, axis=1, keepdims=True), (s.shape[0], LANES))
    p = jnp.exp2(s - _lane_tile(m, s.shape[1]))
    return _dot(p.astype(BF16), v_aug)


def _sel_win_kernel(q_ref, nm_ref, ks_ref, vs_ref, kw_ref, vw_ref, blk_ref, per_ref, g_ref, oc_ref, o_ref,
                    q4, m_s, a_s, bias_ref, *, tile):
    step = pl.program_id(1)
    t = tile
    first = step * Q_TILES

    @pl.when(step == 0)
    def _():
        tl = lax.broadcasted_iota(jnp.int32, (t, t), 0)
        kl = lax.broadcasted_iota(jnp.int32, (t, t), 1)
        edge = jnp.where(tl < kl, 0.0, NEG)
        for h in range(HPG):
            bias_ref[h, :, :t] = edge
            for which in range(2):
                rows = jnp.broadcast_to(per_ref[h, which:which + 1, :], (t, 2 * t))
                table = pltpu.roll(rows, 0, 1, stride=1, stride_axis=0)
                bias_ref[h, :, (which + 1) * t:(which + 2) * t] = table[:, :t]

    def stacked(qt, h):
        return slice((qt * HPG + h) * t, (qt * HPG + h + 1) * t)

    def tile_rows(qt):
        return slice(qt * t, (qt + 1) * t)

    for qt in range(Q_TILES):
        for h in range(HPG):
            q4[stacked(qt, h), :HEAD_DIM] = q_ref[tile_rows(qt), h * HEAD_DIM:(h + 1) * HEAD_DIM]
            q4[stacked(qt, h), HEAD_DIM:] = nm_ref[tile_rows(qt), :]
    m_s[...] = jnp.full(m_s.shape, -3e38, F32)
    a_s[...] = jnp.zeros(a_s.shape, F32)

    def sel_operands(row0, n_tiles):
        keys = pl.ds(row0, n_tiles * t)
        k_aug = jnp.concatenate([ks_ref[keys, :], blk_ref[keys, :]], axis=1)
        v_aug = jnp.concatenate([vs_ref[keys, :], jnp.ones((n_tiles * t, LANES), BF16)], axis=1)
        return k_aug, v_aug

    def sel_pass(qt, operands, n_biased):
        k_aug, v_aug = operands

        def score(h):
            s = _dot_nt(q4[stacked(qt, h), :], k_aug)
            plain = s.shape[1] - n_biased * t
            if n_biased:
                near = s[:, plain:] + bias_ref[h, :, (3 - n_biased) * t:]
                s = jnp.concatenate([s[:, :plain], near], axis=1) if plain else near
            return s

        def consume(h, s):
            _softmax_update(s, v_aug, m_s, a_s, stacked(qt, h))

        return score, consume

    def win_pass(qt, row0, n_tiles):
        keys = pl.ds(row0, n_tiles * t)
        v_aug = jnp.concatenate([vw_ref[keys, :], jnp.ones((n_tiles * t, LANES), BF16)], axis=1)

        def score(h):
            return _dot_nt(q4[stacked(qt, h), :HEAD_DIM], kw_ref[keys, :]) + bias_ref[h, :, (3 - n_tiles) * t:]

        def consume(h, s):
            acc_w = _softmax_once(s, v_aug)
            acc_s = a_s[stacked(qt, h), :]
            o_w = acc_w[:, :HEAD_DIM] * (1.0 / acc_w[:, HEAD_DIM:])
            o_s = acc_s[:, :HEAD_DIM] * (1.0 / acc_s[:, HEAD_DIM:])
            cols = slice(h * HEAD_DIM, (h + 1) * HEAD_DIM)
            gates = g_ref[tile_rows(qt), :]
            o = (gates[:, h:h + 1] * oc_ref[tile_rows(qt), cols] + gates[:, HPG + h:HPG + h + 1] * o_s
                 + gates[:, 2 * HPG + h:2 * HPG + h + 1] * o_w)
            o_ref[tile_rows(qt), cols] = o.astype(o_ref.dtype)

        return score, consume

    def run(passes):
        work = [(score, consume, h0) for score, consume in passes for h0 in range(0, HPG, 2)]
        ahead = [work[0][0](h) for h in (0, 1)]
        for n, (_, consume, h0) in enumerate(work):
            now = ahead
            if n + 1 < len(work):
                nxt_score, _, nxt_h0 = work[n + 1]
                ahead = [nxt_score(h) for h in (nxt_h0, nxt_h0 + 1)]
            for h, s in zip((h0, h0 + 1), now):
                consume(h, s)

    def far_passes(row0):
        operands = sel_operands(row0, FAR_TILES)
        return [sel_pass(qt, operands, 0) for qt in range(Q_TILES)]

    n_far = jnp.maximum(first - 1, 0)
    n_chunks = n_far // FAR_TILES
    chunk = FAR_TILES * t

    def far_pair(c, carry):
        row0 = pl.multiple_of(c * 2 * chunk, 2 * chunk)
        run(far_passes(row0) + far_passes(row0 + chunk))
        return carry

    lax.fori_loop(0, n_chunks // 2, far_pair, 0)

    @pl.when(n_chunks % 2 == 1)
    def _():
        run(far_passes(pl.multiple_of((n_chunks - 1) * chunk, chunk)))

    def tail(left):
        row0 = pl.multiple_of((n_far - left) * t, t)
        passes = []
        for qt in range(Q_TILES):
            n_sel = left + 2 + qt
            passes.append(sel_pass(qt, sel_operands(row0, n_sel), 2))
            passes.append(win_pass(qt, pl.multiple_of((first + qt - 2) * t, t), 3))
        run(passes)

    for left in range(FAR_TILES):
        if (left + 1) % Q_TILES == 0:
            pl.when(jnp.logical_and(step >= 1, n_far % FAR_TILES == left))(functools.partial(tail, left))

    @pl.when(step == 0)
    def _():
        passes = []
        for qt in range(Q_TILES):
            n_keys = min(qt + 1, 3)
            passes.append(sel_pass(qt, sel_operands(0, qt + 1), min(qt + 1, 2)))
            passes.append(win_pass(qt, (qt + 1 - n_keys) * t, n_keys))
        run(passes)


def _sel_win(proj, neg_mask, blk_onehot, periods, gates, o_c, tile):
    s = proj.shape[0]
    ks_blk = (Q_WIDTH + 2 * KV_WIDTH) // HEAD_DIM
    vs_blk = (Q_WIDTH + 3 * KV_WIDTH) // HEAD_DIM
    kw_blk = (Q_WIDTH + 4 * KV_WIDTH) // HEAD_DIM
    vw_blk = (Q_WIDTH + 5 * KV_WIDTH) // HEAD_DIM
    once = pl.Buffered(1)
    rows = Q_TILES * tile
    stacked_rows = HPG * rows
    kern = functools.partial(_sel_win_kernel, tile=tile)
    return pl.pallas_call(
        kern,
        out_shape=jax.ShapeDtypeStruct((s, Q_WIDTH), BF16),
        grid=(N_KV_GROUPS, s // rows),
        in_specs=[
            pl.BlockSpec((rows, HPG * HEAD_DIM), lambda g, i: (i, g)),
            pl.BlockSpec((rows, LANES), lambda g, i: (i, g)),
            pl.BlockSpec((s, HEAD_DIM), lambda g, i: (0, ks_blk + g), pipeline_mode=once),
            pl.BlockSpec((s, HEAD_DIM), lambda g, i: (0, vs_blk + g), pipeline_mode=once),
            pl.BlockSpec((s, HEAD_DIM), lambda g, i: (0, kw_blk + g), pipeline_mode=once),
            pl.BlockSpec((s, HEAD_DIM), lambda g, i: (0, vw_blk + g), pipeline_mode=once),
            pl.BlockSpec((s, LANES), lambda g, i: (0, 0), pipeline_mode=once),
            pl.BlockSpec((HPG, 2, 2 * tile), lambda g, i: (g, 0, 0)),
            pl.BlockSpec((rows, LANES), lambda g, i: (i, g)),
            pl.BlockSpec((rows, HPG * HEAD_DIM), lambda g, i: (i, g)),
        ],
        out_specs=pl.BlockSpec((rows, HPG * HEAD_DIM), lambda g, i: (i, g)),
        scratch_shapes=[
            pltpu.VMEM((stacked_rows, 2 * HEAD_DIM), BF16),
            pltpu.VMEM((stacked_rows, LANES), F32),
            pltpu.VMEM((stacked_rows, 2 * HEAD_DIM), F32),
            pltpu.VMEM((HPG, tile, 3 * tile), F32),
        ],
        compiler_params=_params("arbitrary", "arbitrary"),
    )(proj, neg_mask, proj, proj, proj, proj, blk_onehot, periods, gates, o_c)


def _rel_bucket_np(dist):
    n = np.maximum(dist, 0)
    max_exact = REL_BUCKETS // 2
    large = max_exact + (np.log(np.maximum(n, 1).astype(np.float32) / np.float32(max_exact))
                         / np.float32(math.log(REL_MAX_DIST / max_exact))
                         * np.float32(REL_BUCKETS - max_exact)).astype(np.int32)
    large = np.minimum(large, REL_BUCKETS - 1)
    return np.where(n < max_exact, n, large).astype(np.int32)


def _far_distance():
    d = np.arange(4 * REL_MAX_DIST)
    b = _rel_bucket_np(d)
    assert b[-1] == REL_BUCKETS - 1
    return int(np.max(np.nonzero(b != REL_BUCKETS - 1)[0])) + 1


def _bias_table(rel_bias, dist):
    bucket = jnp.asarray(_rel_bucket_np(dist))
    shifted = (rel_bias[bucket] - rel_bias[REL_BUCKETS - 1]) * LOG2E
    tab = jnp.where(jnp.asarray(dist >= 0)[..., None], shifted, NEG)
    return jnp.moveaxis(tab, -1, 0).astype(F32)


def _bias_period(rel_bias, offset, t):
    x = np.arange(2 * t)
    return _bias_table(rel_bias, np.where(x < t, offset - x, offset + 2 * t - x))


def _nsa_mixer(h, hn, rel_bias, w_in, w_out, layer, pos_k, w1_k, w2_k, pos_v, w1_v, w2_v, next_norm_g):
    s = h.shape[0]
    n_sel = s // SEL_BLOCK
    ncp = s // CMP_STRIDE
    tile = ATT_TILE
    far = _far_distance()
    assert n_sel <= LANES and s % CMP_TQ == 0 and s % (Q_TILES * tile) == 0 and FAR_TILES % Q_TILES == 0
    assert far <= tile and far <= CMP_STRIDE * CMP_PAD - CMP_BLOCK + 1 + CMP_STRIDE

    col_scale = jnp.where(jnp.arange(PROJ_MAIN) < Q_WIDTH, HEAD_DIM ** -0.5 * LOG2E, 1.0).astype(F32)[None, :]
    w_in_t = jnp.swapaxes(w_in, 1, 2)
    proj = _matmul(_ep_scale, hn, w_in_t, layer, PROJ_MAIN, BF16, 2048, 512, row_extra=col_scale,
                   w_rows_are_outputs=True)
    w_gate = w_in_t[layer, PROJ_MAIN:].reshape(N_GATES, N_KV_GROUPS, HPG, D_MODEL).transpose(1, 0, 2, 3)
    w_gate = w_gate.reshape(N_KV_GROUPS, N_GATES * HPG, D_MODEL)
    w_gate = jnp.pad(w_gate, ((0, 0), (0, LANES - N_GATES * HPG), (0, 0))).reshape(1, N_KV_GROUPS * LANES, D_MODEL)
    gates = _matmul(jax.nn.sigmoid, hn, w_gate, 0, N_KV_GROUPS * LANES, F32, 2048, 512, w_rows_are_outputs=True)

    w1 = jnp.stack([w1_k, w1_v]).astype(BF16)
    w2 = jnp.stack([w2_k, w2_v]).astype(BF16)
    pos = jnp.stack([pos_k.reshape(1, -1), pos_v.reshape(1, -1)]).astype(BF16)
    pos = jnp.broadcast_to(pos, (2, SUBLANES, CMP_BLOCK * HEAD_DIM))
    cc = _compress(proj, w1, w2, pos)
    kcc = jnp.pad(cc[0], ((0, 0), (CMP_PAD, 0), (0, 0)))
    vcct = jnp.pad(cc[1], ((0, 0), (CMP_PAD, 0), (0, 0))).transpose(0, 2, 1)
    n_rows = ncp + CMP_PAD

    nb = CMP_TQ // CMP_STRIDE + CMP_PAD
    first_dist = -CMP_STRIDE * (nb - 1 - CMP_PAD) - (CMP_BLOCK - 1)
    by_dist = _bias_table(rel_bias, first_dist + np.arange(CMP_TQ + CMP_STRIDE * (nb - 1)))
    band = jnp.stack([by_dist[:, CMP_STRIDE * (nb - 1 - b):CMP_STRIDE * (nb - 1 - b) + CMP_TQ] for b in range(nb)], axis=1)
    ratio = SEL_BLOCK // CMP_STRIDE
    lo = CMP_BLOCK // CMP_STRIDE - 1
    c_of_row = np.arange(n_rows)[None, :] - CMP_PAD
    j_of = np.arange(LANES)[:, None]
    mt = ((c_of_row >= ratio * j_of - lo) & (c_of_row <= ratio * j_of + ratio - 1)
          & (c_of_row >= 0) & (c_of_row <= ncp - 2) & (j_of < n_sel)).astype(np.float32)
    o_c, neg_mask = _cmp_attn(proj, kcc, vcct, band, jnp.asarray(mt, BF16), CMP_TQ)

    onehot = (np.arange(s)[:, None] // SEL_BLOCK == np.arange(LANES)[None, :]).astype(np.float32)
    periods = jnp.stack([_bias_period(rel_bias, tile, tile), _bias_period(rel_bias, 0, tile)], axis=1)
    o = _sel_win(proj, neg_mask, jnp.asarray(onehot, BF16), periods, gates, o_c, tile)
    return _matmul_resid_norm(o, w_out, layer, h, next_norm_g, 256)


def _s5_kernel(x_ref, lam_ref, bt_ref, c_ref, d_ref, y_ref, vt, yt, tmask, *, chunk):
    L = chunk
    half = L // 2
    width = L * SSM_GROUP
    n_chunks = x_ref.shape[0] // L
    by_step = jnp.swapaxes(x_ref[...].reshape(n_chunks, L, LANES), 0, 1)
    for tau in range(L):
        vt[tau] = by_step[tau].T

    @pl.when(pl.program_id(0) == 0)
    def _():
        dst = lax.broadcasted_iota(jnp.int32, (width, width), 0) // SSM_GROUP
        src = lax.broadcasted_iota(jnp.int32, (width, width), 1) // SSM_GROUP
        tmask[...] = jnp.where(src <= dst, 1.0, 0.0)

    def cmul(xr, xi, yr, yi):
        return xr * yr - xi * yi, xr * yi + xi * yr

    n2 = 2 * SSM_STATE
    sub = lax.broadcasted_iota(jnp.int32, (SUBLANES, 1), 0)
    consts = jnp.where(sub == 0, 1.0, jnp.where(sub == 1, float(half + 1), jnp.where(sub == 2, float(half - 1), float(L))))
    tau_col = lax.broadcasted_iota(jnp.int32, (L, 1), 0).astype(F32)
    row = lax.broadcasted_iota(jnp.int32, (n_chunks, n2), 0)
    conj = jnp.where(lax.broadcasted_iota(jnp.int32, (1, n2), 1) < SSM_STATE, 1.0, -1.0)

    def group(g):
        lam = lam_ref[g]
        a_re, a_im = lam[0:1], lam[1:2]
        dt = jnp.exp(lam[2:3])
        log_re, log_im = a_re * dt, a_im * dt

        def cpow(e):
            mag, cos, sin = jnp.exp(e * log_re), jnp.cos(e * log_im), jnp.sin(e * log_im)
            inv = 1.0 / mag
            return mag * cos, mag * sin, inv * cos, -(inv * sin)

        kr, ki, _, _ = cpow(consts)
        pr, pi = kr[0:1] - 1.0, ki[0:1]
        den = a_re * a_re + a_im * a_im
        cf_r = (pr * a_re + pi * a_im) / den
        cf_i = (pi * a_re - pr * a_im) / den
        bt_c, bt_s = bt_ref[g, 0], bt_ref[g, 1]
        bb_c = cf_r * bt_c + cf_i * bt_s
        bb_s = cf_r * bt_s - cf_i * bt_c
        c_c, c_s = c_ref[g, 0], c_ref[g, 1]

        e1r, e1i, e2r, e2i = cpow(tau_col - half)
        e3r, e3i = cmul(e1r, e1i, kr[1:2], ki[1:2])
        e4r, e4i = cmul(e2r, e2i, kr[2:3], ki[2:3])

        def outer(er, ei, w_c, w_s):
            return (er[:, None, :] * w_c[None, :, :] + ei[:, None, :] * w_s[None, :, :]).reshape(width, n2)

        qm = outer(e1r, e1i, c_c, c_s)
        km_conj = outer(e2r, e2i, bb_c * conj, bb_s * conj)
        qc = outer(e3r, e3i, c_c, c_s)
        wz = outer(e4r, e4i, bb_c, bb_s)

        chans = pl.ds(pl.multiple_of(g * SSM_GROUP, SSM_GROUP), SSM_GROUP)
        ub = vt[:, chans, :].reshape(width, n_chunks).astype(BF16)
        yield

        tt = _dot_nt(qm.astype(BF16), km_conj.astype(BF16))
        x = _dot(wz.T.astype(BF16), ub).T
        yield
        y = _dot((tt * tmask[...]).astype(BF16), ub)
        p_r, p_i = kr[3:4], ki[3:4]
        d = 1
        while d < n_chunks:
            s = jnp.where(row >= d, pltpu.roll(x, d, 0), 0.0)
            x = x + s * p_r - pltpu.roll(s, SSM_STATE, 1) * (p_i * conj)
            p_r, p_i = cmul(p_r, p_i, p_r, p_i)
            d *= 2
        prev_conj = jnp.where(row >= 1, pltpu.roll(x, 1, 0), 0.0) * conj
        yield
        y = y + _dot_nt(qc.astype(BF16), prev_conj.astype(BF16))
        yt[:, chans, :] = y.reshape(L, SSM_GROUP, n_chunks)

    def group_batch(k, carry):
        running = [group(S5_GROUPS_PER_TRIP * k + j) for j in range(S5_GROUPS_PER_TRIP)]
        finished = object()
        while running:
            running = [gen for gen in running if next(gen, finished) is not finished]
        return carry

    lax.fori_loop(0, LANES // SSM_GROUP // S5_GROUPS_PER_TRIP, group_batch, 0)
    by_chunk = jnp.swapaxes(jnp.stack([yt[tau].T for tau in range(L)], axis=0), 0, 1)
    y = by_chunk.reshape(n_chunks * L, LANES) + d_ref[...] * x_ref[...]
    y_ref[...] = jax.nn.gelu(y).astype(y_ref.dtype)


def _s5_mixer(h, hn, a_re, a_im, log_dt, b_re, b_im, c_re, c_im, d_skip, w_glu, layer):
    s = h.shape[0]
    L = S5_CHUNK
    n_chunks = s // L
    width = L * SSM_GROUP
    gpb = LANES // SSM_GROUP
    def packed(re, im):
        return jnp.stack([jnp.concatenate([re, im], -1), jnp.concatenate([-im, re], -1)], axis=1).astype(F32)

    lam = jnp.stack([a_re, a_im, jnp.broadcast_to(log_dt[:, None], a_re.shape)], axis=1).astype(F32)
    lam = jnp.concatenate([lam, lam], axis=-1)
    bt = packed(b_re.transpose(0, 2, 1), b_im.transpose(0, 2, 1))
    cc = packed(c_re, c_im)
    dd = d_skip.astype(F32).reshape(1, D_MODEL)
    y = pl.pallas_call(
        functools.partial(_s5_kernel, chunk=L),
        out_shape=jax.ShapeDtypeStruct((s, D_MODEL), BF16),
        grid=(SSM_GROUPS // gpb,),
        in_specs=[
            pl.BlockSpec((s, LANES), lambda b: (0, b)),
            pl.BlockSpec((gpb, 3, 2 * SSM_STATE), lambda b: (b, 0, 0)),
            pl.BlockSpec((gpb, 2, SSM_GROUP, 2 * SSM_STATE), lambda b: (b, 0, 0, 0)),
            pl.BlockSpec((gpb, 2, SSM_GROUP, 2 * SSM_STATE), lambda b: (b, 0, 0, 0)),
            pl.BlockSpec((1, LANES), lambda b: (0, b)),
        ],
        out_specs=pl.BlockSpec((s, LANES), lambda b: (0, b)),
        scratch_shapes=[
            pltpu.VMEM((L, LANES, n_chunks), hn.dtype),
            pltpu.VMEM((L, LANES, n_chunks), F32),
            pltpu.VMEM((width, width), F32),
        ],
        compiler_params=_params("arbitrary"),
    )(hn, lam, bt, cc, dd)
    return _matmul(_ep_glu_resid, y, w_glu, layer, D_MODEL, F32, 1024, 512, b_halves=2, tile_extra=h)


def kernel(x, rel_bias, mix_norm_g, ffn_norm_g, final_norm_g, nsa_w_in, nsa_w_out, cmp_pos_k, cmp_w1_k, cmp_w2_k, cmp_pos_v, cmp_w1_v, cmp_w2_v, s5_A_re, s5_A_im, s5_log_dt, s5_B_re, s5_B_im, s5_C_re, s5_C_im, s5_D, s5_w_glu, ffn_w_in, ffn_w_out):
    assert x.shape[0] == 1
    h = x[0]
    hn = _rmsnorm(h, mix_norm_g[0], BF16)
    h, hn = _nsa_mixer(h, hn, rel_bias, nsa_w_in, nsa_w_out, 0, cmp_pos_k[0], cmp_w1_k[0], cmp_w2_k[0],
                       cmp_pos_v[0], cmp_w1_v[0], cmp_w2_v[0], ffn_norm_g[0])
    h = _swiglu_ffn(h, hn, ffn_w_in, ffn_w_out, 0)
    hn = _rmsnorm(h, mix_norm_g[1], BF16)
    h = _s5_mixer(h, hn, s5_A_re[0], s5_A_im[0], s5_log_dt[0], s5_B_re[0], s5_B_im[0], s5_C_re[0], s5_C_im[0],
                  s5_D[0], s5_w_glu, 0)
    h = _swiglu_ffn(h, _rmsnorm(h, ffn_norm_g[1], BF16), ffn_w_in, ffn_w_out, 1)
    return _rmsnorm(h, final_norm_g, x.dtype)[None]
```

```python
import functools
import math

import numpy as np
import jax
import jax.numpy as jnp
from jax import lax
from jax.experimental import pallas as pl
from jax.experimental.pallas import tpu as pltpu

D_MODEL = 2048
N_HEADS = 16
HEAD_DIM = 128
N_KV_GROUPS = 4
HPG = N_HEADS // N_KV_GROUPS
CMP_BLOCK = 32
CMP_STRIDE = 16
CMP_HIDDEN = 2 * HEAD_DIM
SEL_BLOCK = 64
SEL_TOPK = 16
N_FORCED = 3
WINDOW = 512
N_GATES = 3
KV_WIDTH = N_KV_GROUPS * HEAD_DIM
Q_WIDTH = N_HEADS * HEAD_DIM
PROJ_MAIN = Q_WIDTH + 6 * KV_WIDTH
REL_BUCKETS = 32
REL_MAX_DIST = 128
SSM_GROUP = 16
SSM_GROUPS = D_MODEL // SSM_GROUP
SSM_STATE = 64
D_FF = ((8 * D_MODEL + 2) // 3 + 255) // 256 * 256
RMS_EPS = 1e-6
NEG = -1e30
LOG2E = math.log2(math.e)

LANES = 128
SUBLANES = 8
VMEM_LIMIT = 52 * 1024 * 1024
RING_VMEM_LIMIT = 57 * 1024 * 1024

ATT_TILE = WINDOW // 2
FAR_TILES = 4
Q_TILES = 2
CMP_TQ = 256
CMP_PAD = 8
S5_CHUNK = 32
S5_GROUPS_PER_TRIP = 4

BF16 = jnp.bfloat16
F32 = jnp.float32


def _dot(a, b):
    return jnp.dot(a, b, preferred_element_type=F32)


def _dot_nt(a, b):
    return lax.dot_general(a, b, (((1,), (1,)), ((), ())), preferred_element_type=F32)


def _params(*sem, flags=None):
    return pltpu.CompilerParams(dimension_semantics=sem, vmem_limit_bytes=VMEM_LIMIT, flags=flags)


def _rmsnorm_kernel(x_ref, g_ref, o_ref):
    x = x_ref[...]
    ms = jnp.mean(x * x, axis=-1, keepdims=True)
    o_ref[...] = (x * lax.rsqrt(ms + RMS_EPS) * g_ref[...]).astype(o_ref.dtype)


def _rmsnorm(x, g, out_dtype, tm=1024):
    s, d = x.shape
    return pl.pallas_call(
        _rmsnorm_kernel,
        out_shape=jax.ShapeDtypeStruct((s, d), out_dtype),
        grid=(s // tm,),
        in_specs=[pl.BlockSpec((tm, d), lambda i: (i, 0)), pl.BlockSpec((1, d), lambda i: (0, 0))],
        out_specs=pl.BlockSpec((tm, d), lambda i: (i, 0)),
        compiler_params=_params("parallel"),
    )(x, g.reshape(1, d))


def _mm_kernel(*refs, n_b, n_extra, epilogue, w_rows_are_outputs):
    a_ref = refs[0]
    b_refs = refs[1:1 + n_b]
    extra_refs = refs[1 + n_b:1 + n_b + n_extra]
    o_ref = refs[1 + n_b + n_extra]
    w_scr = refs[2 + n_b + n_extra:]

    @pl.when(pl.program_id(1) == 0)
    def _():
        for b_ref, w in zip(b_refs, w_scr):
            w[...] = b_ref[...].astype(BF16)

    a = a_ref[...]
    z = [(_dot_nt if w_rows_are_outputs else _dot)(a, w[...]) for w in w_scr]
    o_ref[...] = epilogue(*z, *[e[...] for e in extra_refs]).astype(o_ref.dtype)


def _ep_scale(z, scale):
    return z * scale


def _ep_resid(z, resid):
    return resid + z


def _ep_swiglu(za, zb):
    return jax.nn.silu(za) * zb


def _ep_glu_resid(za, zb, resid):
    return resid + za * jax.nn.sigmoid(zb)


def _matmul(epilogue, a, b, layer, n_out, out_dtype, tm, tn, *, b_halves=1, row_extra=None, tile_extra=None,
            w_rows_are_outputs=False):
    m, k = a.shape
    nj = n_out // tn
    estimate = (b_halves * k * tn * (2 * 4 + 2) + 2 * tm * k * 2 + 4 * tm * tn * 4 + b_halves * tm * tn * 4)
    b_mode = pl.Buffered(1) if estimate > VMEM_LIMIT else None
    in_specs = [pl.BlockSpec((tm, k), lambda j, i: (i, 0))]
    args = [a]
    for half in range(b_halves):
        if w_rows_are_outputs:
            spec = pl.BlockSpec((None, tn, k), functools.partial(lambda j, i, o: (layer, j + o, 0), o=half * nj),
                                pipeline_mode=b_mode)
        else:
            spec = pl.BlockSpec((None, k, tn), functools.partial(lambda j, i, o: (layer, 0, j + o), o=half * nj),
                                pipeline_mode=b_mode)
        in_specs.append(spec)
        args.append(b)
    extras = []
    if row_extra is not None:
        in_specs.append(pl.BlockSpec((1, tn), lambda j, i: (0, j)))
        extras.append(row_extra)
    if tile_extra is not None:
        in_specs.append(pl.BlockSpec((tm, tn), lambda j, i: (i, j)))
        extras.append(tile_extra)
    kern = functools.partial(_mm_kernel, n_b=b_halves, n_extra=len(extras), epilogue=epilogue,
                             w_rows_are_outputs=w_rows_are_outputs)
    return pl.pallas_call(
        kern,
        out_shape=jax.ShapeDtypeStruct((m, n_out), out_dtype),
        grid=(nj, m // tm),
        in_specs=in_specs,
        out_specs=pl.BlockSpec((tm, tn), lambda j, i: (i, j)),
        scratch_shapes=[pltpu.VMEM((tn, k) if w_rows_are_outputs else (k, tn), BF16) for _ in range(b_halves)],
        compiler_params=_params("arbitrary", "arbitrary"),
    )(*args, *extras)


def _mm_resid_norm_kernel(a_ref, b_ref, r_ref, g_ref, h_ref, hn_ref, w_scr):
    @pl.when(pl.program_id(0) == 0)
    def _():
        w_scr[...] = b_ref[...].astype(BF16)

    h = r_ref[...] + _dot(a_ref[...], w_scr[...])
    h_ref[...] = h
    ms = jnp.mean(h * h, axis=-1, keepdims=True)
    hn_ref[...] = (h * lax.rsqrt(ms + RMS_EPS) * g_ref[...]).astype(hn_ref.dtype)


def _matmul_resid_norm(a, b, layer, resid, g, tm):
    m, k = a.shape
    n = b.shape[2]
    return pl.pallas_call(
        _mm_resid_norm_kernel,
        out_shape=(jax.ShapeDtypeStruct((m, n), F32), jax.ShapeDtypeStruct((m, n), BF16)),
        grid=(m // tm,),
        in_specs=[
            pl.BlockSpec((tm, k), lambda i: (i, 0)),
            pl.BlockSpec((None, k, n), lambda i: (layer, 0, 0), pipeline_mode=pl.Buffered(1)),
            pl.BlockSpec((tm, n), lambda i: (i, 0)),
            pl.BlockSpec((1, n), lambda i: (0, 0)),
        ],
        out_specs=(pl.BlockSpec((tm, n), lambda i: (i, 0)), pl.BlockSpec((tm, n), lambda i: (i, 0))),
        scratch_shapes=[pltpu.VMEM((k, n), BF16)],
        compiler_params=_params("arbitrary"),
    )(a, b, resid, g.reshape(1, n))


RING = 3


def _mm_resid_ring_kernel(a_hbm, b_ref, r_ref, o_ref, w_scr, a_ring, sems, *, tm, n_row_tiles, n_steps):
    i = pl.program_id(1)
    step = pl.program_id(0) * n_row_tiles + i

    def tile_copy(s):
        rows = pl.ds(pl.multiple_of((s % n_row_tiles) * tm, tm), tm)
        return pltpu.make_async_copy(a_hbm.at[rows, :], a_ring.at[s % RING], sems.at[s % RING])

    @pl.when(step == 0)
    def _():
        for s in range(RING - 1):
            tile_copy(s).start()

    @pl.when(step + RING - 1 < n_steps)
    def _():
        tile_copy(step + RING - 1).start()

    @pl.when(i == 0)
    def _():
        w_scr[...] = b_ref[...].astype(BF16)

    tile_copy(step).wait()
    o_ref[...] = r_ref[...] + _dot(a_ring[step % RING], w_scr[...])


def _matmul_resid_ring(a, b, layer, resid, tm, tn):
    m, k = a.shape
    n = b.shape[2]
    n_row_tiles = m // tm
    n_steps = (n // tn) * n_row_tiles
    assert n_steps >= RING - 1
    kern = functools.partial(_mm_resid_ring_kernel, tm=tm, n_row_tiles=n_row_tiles, n_steps=n_steps)
    return pl.pallas_call(
        kern,
        out_shape=jax.ShapeDtypeStruct((m, n), F32),
        grid=(n // tn, n_row_tiles),
        in_specs=[
            pl.BlockSpec(memory_space=pl.ANY),
            pl.BlockSpec((None, k, tn), lambda j, i: (layer, 0, j)),
            pl.BlockSpec((tm, tn), lambda j, i: (i, j)),
        ],
        out_specs=pl.BlockSpec((tm, tn), lambda j, i: (i, j)),
        scratch_shapes=[pltpu.VMEM((k, tn), BF16), pltpu.VMEM((RING, tm, k), BF16), pltpu.SemaphoreType.DMA((RING,))],
        compiler_params=pltpu.CompilerParams(dimension_semantics=("arbitrary", "arbitrary"),
                                             vmem_limit_bytes=RING_VMEM_LIMIT),
    )(a, b, resid)


def _swiglu_ffn(h, hn, w_in, w_out, layer):
    act = _matmul(_ep_swiglu, hn, w_in, layer, D_FF, BF16, 1024, 512, b_halves=2)
    return _matmul_resid_ring(act, w_out, layer, h, 512, 512)


def _compress_kernel(x_ref, w1_ref, w2_ref, pos_ref, o_ref):
    ncp = x_ref.shape[0] // CMP_STRIDE
    half = CMP_STRIDE * HEAD_DIM
    by_offset = jnp.swapaxes(x_ref[...].reshape(ncp, CMP_STRIDE, HEAD_DIM), 0, 1)
    rows = jnp.concatenate([by_offset[r] for r in range(CMP_STRIDE)], axis=1)
    top = _dot(rows, w1_ref[0, :half, :])
    bot = _dot(rows, w1_ref[0, half:, :])
    posb = _dot(pos_ref[0], w1_ref[0])[0:1]
    hid = top + pltpu.roll(bot, ncp - 1, 0) + posb
    o_ref[0, 0] = _dot(jax.nn.gelu(hid).astype(BF16), w2_ref[0]).astype(o_ref.dtype)


def _compress(proj, w1, w2, pos):
    s = proj.shape[0]
    ncp = s // CMP_STRIDE
    kc_blk = Q_WIDTH // HEAD_DIM
    return pl.pallas_call(
        _compress_kernel,
        out_shape=jax.ShapeDtypeStruct((2, N_KV_GROUPS, ncp, HEAD_DIM), BF16),
        grid=(2, N_KV_GROUPS),
        in_specs=[
            pl.BlockSpec((s, HEAD_DIM), lambda kv, gi: (0, kc_blk + kv * N_KV_GROUPS + gi)),
            pl.BlockSpec((1, CMP_BLOCK * HEAD_DIM, CMP_HIDDEN), lambda kv, gi: (kv, 0, 0)),
            pl.BlockSpec((1, CMP_HIDDEN, HEAD_DIM), lambda kv, gi: (kv, 0, 0)),
            pl.BlockSpec((1, SUBLANES, CMP_BLOCK * HEAD_DIM), lambda kv, gi: (kv, 0, 0)),
        ],
        out_specs=pl.BlockSpec((1, 1, ncp, HEAD_DIM), lambda kv, gi: (kv, gi, 0, 0)),
        compiler_params=_params("parallel", "parallel"),
    )(proj, w1, w2, pos)


def _first_argmax(score):
    n_rows, n = score.shape
    sub = lax.broadcasted_iota(jnp.int32, (SUBLANES, n), 0)
    pairs = [(score[r:r + SUBLANES], sub + r) for r in range(0, n_rows, SUBLANES)]
    while len(pairs) > 2:
        merged = []
        for k in range(0, len(pairs) - 1, 2):
            (va, ia), (vb, ib) = pairs[k], pairs[k + 1]
            take = vb > va
            merged.append((jnp.where(take, vb, va), jnp.where(take, ib, ia)))
        if len(pairs) % 2:
            merged.append(pairs[-1])
        pairs = merged
    val = jnp.concatenate([v for v, _ in pairs], axis=0)
    idx = jnp.concatenate([i for _, i in pairs], axis=0)
    best = jnp.max(val, axis=0, keepdims=True)
    return jnp.min(jnp.where(val == best, idx, n_rows), axis=0, keepdims=True)


def _cmp_attn_kernel(q_ref, kcc_ref, vcct_ref, band_ref, mt_ref, oc_ref, nm_ref, s_scr, *, tq, nb):
    i = pl.program_id(1)
    band_start = pl.multiple_of(i * (tq // CMP_STRIDE), SUBLANES)
    n_rows = kcc_ref.shape[1]
    n_blk = mt_ref.shape[0]

    def body(n_r, n_b):
        kcc = kcc_ref[0, :n_r, :]
        vcct = vcct_ref[0, :, :n_r]
        rows = lax.broadcasted_iota(jnp.int32, (n_r, tq), 0)
        row_bias = jnp.where(rows >= CMP_PAD, jnp.where(rows < band_start + nb, 0.0, NEG), NEG)
        imp = jnp.zeros((n_r, tq), F32)

        def scores(h):
            qh = q_ref[:, h * HEAD_DIM:(h + 1) * HEAD_DIM]
            s_scr[h, :n_r, :] = _dot_nt(kcc, qh) + row_bias
            s_scr[h, pl.ds(band_start, nb), :] += band_ref[h]

        scores(0)
        probs = []
        for h in range(HPG):
            if h + 1 < HPG:
                scores(h + 1)
            s = s_scr[h, :n_r, :]
            m = jnp.maximum(jnp.max(s, axis=0, keepdims=True), 0.5 * NEG)
            e = jnp.exp2(s - m)
            l = jnp.sum(e, axis=0, keepdims=True)
            pn = e * (1.0 / jnp.where(l > 0.0, l, 1.0))
            imp = imp + pn
            probs.append(pn.astype(BF16))

        def outputs():
            for h in range(HPG):
                oct_h = _dot(vcct, probs[h])
                oc_ref[:, h * HEAD_DIM:(h + 1) * HEAD_DIM] = oct_h.T
                yield
        mt = mt_ref[:n_b, :n_r]
        hi = imp.astype(BF16)
        rest = imp - hi.astype(F32)
        mid = rest.astype(BF16)
        lo = (rest - mid.astype(F32)).astype(BF16)
        slc = _dot(mt, hi) + _dot(mt, mid) + _dot(mt, lo)
        jj = lax.broadcasted_iota(jnp.int32, (n_b, tq), 0)
        t_blk = (i * tq + lax.broadcasted_iota(jnp.int32, (n_b, tq), 1)) // SEL_BLOCK
        forced = jnp.where(jj == 0, 1, jnp.where(jj == t_blk, 1, jnp.where(jj == t_blk - 1, 1, 0)))
        picked = jnp.where(forced == 1, 1.0, 0.0)
        score = jnp.where(forced == 1, -jnp.inf, jnp.where(jj <= t_blk, slc, NEG))
        pending = outputs()
        for r in range(SEL_TOPK - N_FORCED):
            hit = jj == _first_argmax(score)
            picked = jnp.where(hit, 1.0, picked)
            score = jnp.where(hit, -jnp.inf, score)
            if r % 3 == 0:
                next(pending, None)
        for _ in pending:
            pass
        neg_mask = jnp.where(picked > 0.5, jnp.where(jj <= t_blk, 0.0, NEG), NEG)
        if n_b < n_blk:
            neg_mask = jnp.concatenate([neg_mask, jnp.full((n_blk - n_b, tq), NEG, F32)], axis=0)
        nm_ref[...] = neg_mask.T.astype(nm_ref.dtype)

    step = LANES
    extents = list(range(step, n_rows, step)) + [n_rows]
    need_rows = band_start + nb
    for v, n_r in enumerate(extents):
        lower = extents[v - 1] if v else 0
        n_b = min(n_blk, -(-(n_r * CMP_STRIDE // SEL_BLOCK) // SUBLANES) * SUBLANES)
        pl.when(jnp.logical_and(need_rows > lower, need_rows <= n_r))(functools.partial(body, n_r, n_b))


def _cmp_attn(proj, kcc, vcct, band, mt, tq):
    s = proj.shape[0]
    n_rows = kcc.shape[1]
    nb = band.shape[1]
    kern = functools.partial(_cmp_attn_kernel, tq=tq, nb=nb)
    return pl.pallas_call(
        kern,
        out_shape=(jax.ShapeDtypeStruct((s, Q_WIDTH), F32),
                   jax.ShapeDtypeStruct((s, N_KV_GROUPS * LANES), BF16)),
        grid=(N_KV_GROUPS, s // tq),
        in_specs=[
            pl.BlockSpec((tq, HPG * HEAD_DIM), lambda g, i: (i, g)),
            pl.BlockSpec((1, n_rows, HEAD_DIM), lambda g, i: (g, 0, 0)),
            pl.BlockSpec((1, HEAD_DIM, n_rows), lambda g, i: (g, 0, 0)),
            pl.BlockSpec((HPG, nb, tq), lambda g, i: (g, 0, 0)),
            pl.BlockSpec((LANES, n_rows), lambda g, i: (0, 0)),
        ],
        out_specs=(pl.BlockSpec((tq, HPG * HEAD_DIM), lambda g, i: (i, g)),
                   pl.BlockSpec((tq, LANES), lambda g, i: (i, g))),
        scratch_shapes=[pltpu.VMEM((HPG, n_rows, tq), F32)],
        compiler_params=_params("parallel", "parallel"),
    )(proj, kcc, vcct, band, mt)


def _lane_tile(x, width):
    return jnp.concatenate([x] * (width // x.shape[1]), axis=1)


def _softmax_update(s, v_aug, m_ref, acc_ref, rows):
    m_prev = m_ref[rows, :]
    m_next = jnp.maximum(m_prev, jnp.max(s, axis=1, keepdims=True))
    alpha = jnp.exp2(m_prev - m_next)
    p = jnp.exp2(s - _lane_tile(m_next, s.shape[1]))
    acc_ref[rows, :] = _lane_tile(alpha, acc_ref.shape[1]) * acc_ref[rows, :] + _dot(p.astype(BF16), v_aug)
    m_ref[rows, :] = m_next


def _softmax_once(s, v_aug):
    m = jnp.broadcast_to(jnp.max(s, axis=1, keepdims=True), (s.shape[0], LANES))
    p = jnp.exp2(s - _lane_tile(m, s.shape[1]))
    return _dot(p.astype(BF16), v_aug)


def _sel_win_kernel(q_ref, nm_ref, ks_ref, vs_ref, kw_ref, vw_ref, blk_ref, per_ref, g_ref, oc_ref, o_ref,
                    q4, m_s, a_s, bias_ref, *, tile):
    step = pl.program_id(1)
    t = tile
    first = step * Q_TILES

    @pl.when(step == 0)
    def _():
        tl = lax.broadcasted_iota(jnp.int32, (t, t), 0)
        kl = lax.broadcasted_iota(jnp.int32, (t, t), 1)
        edge = jnp.where(tl < kl, 0.0, NEG)
        for h in range(HPG):
            bias_ref[h, :, :t] = edge
            for which in range(2):
                rows = jnp.broadcast_to(per_ref[h, which:which + 1, :], (t, 2 * t))
                table = pltpu.roll(rows, 0, 1, stride=1, stride_axis=0)
                bias_ref[h, :, (which + 1) * t:(which + 2) * t] = table[:, :t]

    def stacked(qt, h):
        return slice((qt * HPG + h) * t, (qt * HPG + h + 1) * t)

    def tile_rows(qt):
        return slice(qt * t, (qt + 1) * t)

    for qt in range(Q_TILES):
        for h in range(HPG):
            q4[stacked(qt, h), :HEAD_DIM] = q_ref[tile_rows(qt), h * HEAD_DIM:(h + 1) * HEAD_DIM]
            q4[stacked(qt, h), HEAD_DIM:] = nm_ref[tile_rows(qt), :]
    m_s[...] = jnp.full(m_s.shape, -3e38, F32)
    a_s[...] = jnp.zeros(a_s.shape, F32)

    def sel_operands(row0, n_tiles):
        keys = pl.ds(row0, n_tiles * t)
        k_aug = jnp.concatenate([ks_ref[keys, :], blk_ref[keys, :]], axis=1)
        v_aug = jnp.concatenate([vs_ref[keys, :], jnp.ones((n_tiles * t, LANES), BF16)], axis=1)
        return k_aug, v_aug

    def sel_pass(qt, operands, n_biased):
        k_aug, v_aug = operands

        def score(h):
            s = _dot_nt(q4[stacked(qt, h), :], k_aug)
            plain = s.shape[1] - n_biased * t
            if n_biased:
                near = s[:, plain:] + bias_ref[h, :, (3 - n_biased) * t:]
                s = jnp.concatenate([s[:, :plain], near], axis=1) if plain else near
            return s

        def consume(h, s):
            _softmax_update(s, v_aug, m_s, a_s, stacked(qt, h))

        return score, consume

    def win_pass(qt, row0, n_tiles):
        keys = pl.ds(row0, n_tiles * t)
        v_aug = jnp.concatenate([vw_ref[keys, :], jnp.ones((n_tiles * t, LANES), BF16)], axis=1)

        def score(h):
            return _dot_nt(q4[stacked(qt, h), :HEAD_DIM], kw_ref[keys, :]) + bias_ref[h, :, (3 - n_tiles) * t:]

        def consume(h, s):
            acc_w = _softmax_once(s, v_aug)
            acc_s = a_s[stacked(qt, h), :]
            o_w = acc_w[:, :HEAD_DIM] * (1.0 / acc_w[:, HEAD_DIM:])
            o_s = acc_s[:, :HEAD_DIM] * (1.0 / acc_s[:, HEAD_DIM:])
            cols = slice(h * HEAD_DIM, (h + 1) * HEAD_DIM)
            gates = g_ref[tile_rows(qt), :]
            o = (gates[:, h:h + 1] * oc_ref[tile_rows(qt), cols] + gates[:, HPG + h:HPG + h + 1] * o_s
                 + gates[:, 2 * HPG + h:2 * HPG + h + 1] * o_w)
            o_ref[tile_rows(qt), cols] = o.astype(o_ref.dtype)

        return score, consume

    def run(passes):
        work = [(score, consume, h0) for score, consume in passes for h0 in range(0, HPG, 2)]
        ahead = [work[0][0](h) for h in (0, 1)]
        for n, (_, consume, h0) in enumerate(work):
            now = ahead
            if n + 1 < len(work):
                nxt_score, _, nxt_h0 = work[n + 1]
                ahead = [nxt_score(h) for h in (nxt_h0, nxt_h0 + 1)]
            for h, s in zip((h0, h0 + 1), now):
                consume(h, s)

    def far_passes(row0):
        operands = sel_operands(row0, FAR_TILES)
        return [sel_pass(qt, operands, 0) for qt in range(Q_TILES)]

    n_far = jnp.maximum(first - 1, 0)
    n_chunks = n_far // FAR_TILES
    chunk = FAR_TILES * t

    def far_pair(c, carry):
        row0 = pl.multiple_of(c * 2 * chunk, 2 * chunk)
        run(far_passes(row0) + far_passes(row0 + chunk))
        return carry

    lax.fori_loop(0, n_chunks // 2, far_pair, 0)

    @pl.when(n_chunks % 2 == 1)
    def _():
        run(far_passes(pl.multiple_of((n_chunks - 1) * chunk, chunk)))

    def tail(left):
        row0 = pl.multiple_of((n_far - left) * t, t)
        passes = []
        for qt in range(Q_TILES):
            n_sel = left + 2 + qt
            passes.append(sel_pass(qt, sel_operands(row0, n_sel), 2))
            passes.append(win_pass(qt, pl.multiple_of((first + qt - 2) * t, t), 3))
        run(passes)

    for left in range(FAR_TILES):
        if (left + 1) % Q_TILES == 0:
            pl.when(jnp.logical_and(step >= 1, n_far % FAR_TILES == left))(functools.partial(tail, left))

    @pl.when(step == 0)
    def _():
        passes = []
        for qt in range(Q_TILES):
            n_keys = min(qt + 1, 3)
            passes.append(sel_pass(qt, sel_operands(0, qt + 1), min(qt + 1, 2)))
            passes.append(win_pass(qt, (qt + 1 - n_keys) * t, n_keys))
        run(passes)


def _sel_win(proj, neg_mask, blk_onehot, periods, gates, o_c, tile):
    s = proj.shape[0]
    ks_blk = (Q_WIDTH + 2 * KV_WIDTH) // HEAD_DIM
    vs_blk = (Q_WIDTH + 3 * KV_WIDTH) // HEAD_DIM
    kw_blk = (Q_WIDTH + 4 * KV_WIDTH) // HEAD_DIM
    vw_blk = (Q_WIDTH + 5 * KV_WIDTH) // HEAD_DIM
    once = pl.Buffered(1)
    rows = Q_TILES * tile
    stacked_rows = HPG * rows
    kern = functools.partial(_sel_win_kernel, tile=tile)
    return pl.pallas_call(
        kern,
        out_shape=jax.ShapeDtypeStruct((s, Q_WIDTH), BF16),
        grid=(N_KV_GROUPS, s // rows),
        in_specs=[
            pl.BlockSpec((rows, HPG * HEAD_DIM), lambda g, i: (i, g)),
            pl.BlockSpec((rows, LANES), lambda g, i: (i, g)),
            pl.BlockSpec((s, HEAD_DIM), lambda g, i: (0, ks_blk + g)),
            pl.BlockSpec((s, HEAD_DIM), lambda g, i: (0, vs_blk + g)),
            pl.BlockSpec((s, HEAD_DIM), lambda g, i: (0, kw_blk + g)),
            pl.BlockSpec((s, HEAD_DIM), lambda g, i: (0, vw_blk + g)),
            pl.BlockSpec((s, LANES), lambda g, i: (0, 0), pipeline_mode=once),
            pl.BlockSpec((HPG, 2, 2 * tile), lambda g, i: (g, 0, 0)),
            pl.BlockSpec((rows, LANES), lambda g, i: (i, g)),
            pl.BlockSpec((rows, HPG * HEAD_DIM), lambda g, i: (i, g)),
        ],
        out_specs=pl.BlockSpec((rows, HPG * HEAD_DIM), lambda g, i: (i, g)),
        scratch_shapes=[
            pltpu.VMEM((stacked_rows, 2 * HEAD_DIM), BF16),
            pltpu.VMEM((stacked_rows, LANES), F32),
            pltpu.VMEM((stacked_rows, 2 * HEAD_DIM), F32),
            pltpu.VMEM((HPG, tile, 3 * tile), F32),
        ],
        compiler_params=_params("arbitrary", "arbitrary"),
    )(proj, neg_mask, proj, proj, proj, proj, blk_onehot, periods, gates, o_c)


def _rel_bucket_np(dist):
    n = np.maximum(dist, 0)
    max_exact = REL_BUCKETS // 2
    large = max_exact + (np.log(np.maximum(n, 1).astype(np.float32) / np.float32(max_exact))
                         / np.float32(math.log(REL_MAX_DIST / max_exact))
                         * np.float32(REL_BUCKETS - max_exact)).astype(np.int32)
    large = np.minimum(large, REL_BUCKETS - 1)
    return np.where(n < max_exact, n, large).astype(np.int32)


def _far_distance():
    d = np.arange(4 * REL_MAX_DIST)
    b = _rel_bucket_np(d)
    assert b[-1] == REL_BUCKETS - 1
    return int(np.max(np.nonzero(b != REL_BUCKETS - 1)[0])) + 1


def _bias_table(rel_bias, dist):
    bucket = jnp.asarray(_rel_bucket_np(dist))
    shifted = (rel_bias[bucket] - rel_bias[REL_BUCKETS - 1]) * LOG2E
    tab = jnp.where(jnp.asarray(dist >= 0)[..., None], shifted, NEG)
    return jnp.moveaxis(tab, -1, 0).astype(F32)


def _bias_period(rel_bias, offset, t):
    x = np.arange(2 * t)
    return _bias_table(rel_bias, np.where(x < t, offset - x, offset + 2 * t - x))


def _nsa_mixer(h, hn, rel_bias, w_in, w_out, layer, pos_k, w1_k, w2_k, pos_v, w1_v, w2_v, next_norm_g):
    s = h.shape[0]
    n_sel = s // SEL_BLOCK
    ncp = s // CMP_STRIDE
    tile = ATT_TILE
    far = _far_distance()
    assert n_sel <= LANES and s % CMP_TQ == 0 and s % (Q_TILES * tile) == 0 and FAR_TILES % Q_TILES == 0
    assert far <= tile and far <= CMP_STRIDE * CMP_PAD - CMP_BLOCK + 1 + CMP_STRIDE

    col_scale = jnp.where(jnp.arange(PROJ_MAIN) < Q_WIDTH, HEAD_DIM ** -0.5 * LOG2E, 1.0).astype(F32)[None, :]
    w_in_t = jnp.swapaxes(w_in, 1, 2)
    proj = _matmul(_ep_scale, hn, w_in_t, layer, PROJ_MAIN, BF16, 2048, 512, row_extra=col_scale,
                   w_rows_are_outputs=True)
    w_gate = w_in_t[layer, PROJ_MAIN:].reshape(N_GATES, N_KV_GROUPS, HPG, D_MODEL).transpose(1, 0, 2, 3)
    w_gate = w_gate.reshape(N_KV_GROUPS, N_GATES * HPG, D_MODEL)
    w_gate = jnp.pad(w_gate, ((0, 0), (0, LANES - N_GATES * HPG), (0, 0))).reshape(1, N_KV_GROUPS * LANES, D_MODEL)
    gates = _matmul(jax.nn.sigmoid, hn, w_gate, 0, N_KV_GROUPS * LANES, F32, 2048, 512, w_rows_are_outputs=True)

    w1 = jnp.stack([w1_k, w1_v]).astype(BF16)
    w2 = jnp.stack([w2_k, w2_v]).astype(BF16)
    pos = jnp.stack([pos_k.reshape(1, -1), pos_v.reshape(1, -1)]).astype(BF16)
    pos = jnp.broadcast_to(pos, (2, SUBLANES, CMP_BLOCK * HEAD_DIM))
    cc = _compress(proj, w1, w2, pos)
    kcc = jnp.pad(cc[0], ((0, 0), (CMP_PAD, 0), (0, 0)))
    vcct = jnp.pad(cc[1], ((0, 0), (CMP_PAD, 0), (0, 0))).transpose(0, 2, 1)
    n_rows = ncp + CMP_PAD

    nb = CMP_TQ // CMP_STRIDE + CMP_PAD
    first_dist = -CMP_STRIDE * (nb - 1 - CMP_PAD) - (CMP_BLOCK - 1)
    by_dist = _bias_table(rel_bias, first_dist + np.arange(CMP_TQ + CMP_STRIDE * (nb - 1)))
    band = jnp.stack([by_dist[:, CMP_STRIDE * (nb - 1 - b):CMP_STRIDE * (nb - 1 - b) + CMP_TQ] for b in range(nb)], axis=1)
    ratio = SEL_BLOCK // CMP_STRIDE
    lo = CMP_BLOCK // CMP_STRIDE - 1
    c_of_row = np.arange(n_rows)[None, :] - CMP_PAD
    j_of = np.arange(LANES)[:, None]
    mt = ((c_of_row >= ratio * j_of - lo) & (c_of_row <= ratio * j_of + ratio - 1)
          & (c_of_row >= 0) & (c_of_row <= ncp - 2) & (j_of < n_sel)).astype(np.float32)
    o_c, neg_mask = _cmp_attn(proj, kcc, vcct, band, jnp.asarray(mt, BF16), CMP_TQ)

    onehot = (np.arange(s)[:, None] // SEL_BLOCK == np.arange(LANES)[None, :]).astype(np.float32)
    periods = jnp.stack([_bias_period(rel_bias, tile, tile), _bias_period(rel_bias, 0, tile)], axis=1)
    o = _sel_win(proj, neg_mask, jnp.asarray(onehot, BF16), periods, gates, o_c, tile)
    return _matmul_resid_norm(o, w_out, layer, h, next_norm_g, 256)


def _s5_kernel(x_ref, lam_ref, bt_ref, c_ref, d_ref, y_ref, vt, yt, tmask, *, chunk):
    L = chunk
    half = L // 2
    width = L * SSM_GROUP
    n_chunks = x_ref.shape[0] // L
    by_step = jnp.swapaxes(x_ref[...].reshape(n_chunks, L, LANES), 0, 1)
    for tau in range(L):
        vt[tau] = by_step[tau].T

    @pl.when(pl.program_id(0) == 0)
    def _():
        dst = lax.broadcasted_iota(jnp.int32, (width, width), 0) // SSM_GROUP
        src = lax.broadcasted_iota(jnp.int32, (width, width), 1) // SSM_GROUP
        tmask[...] = jnp.where(src <= dst, 1.0, 0.0)

    def cmul(xr, xi, yr, yi):
        return xr * yr - xi * yi, xr * yi + xi * yr

    n2 = 2 * SSM_STATE
    sub = lax.broadcasted_iota(jnp.int32, (SUBLANES, 1), 0)
    consts = jnp.where(sub == 0, 1.0, jnp.where(sub == 1, float(half + 1), jnp.where(sub == 2, float(half - 1), float(L))))
    tau_col = lax.broadcasted_iota(jnp.int32, (L, 1), 0).astype(F32)
    row = lax.broadcasted_iota(jnp.int32, (n_chunks, n2), 0)
    conj = jnp.where(lax.broadcasted_iota(jnp.int32, (1, n2), 1) < SSM_STATE, 1.0, -1.0)

    def group(g):
        lam = lam_ref[g]
        a_re, a_im = lam[0:1], lam[1:2]
        dt = jnp.exp(lam[2:3])
        log_re, log_im = a_re * dt, a_im * dt

        def cpow(e):
            mag, cos, sin = jnp.exp(e * log_re), jnp.cos(e * log_im), jnp.sin(e * log_im)
            inv = 1.0 / mag
            return mag * cos, mag * sin, inv * cos, -(inv * sin)

        kr, ki, _, _ = cpow(consts)
        pr, pi = kr[0:1] - 1.0, ki[0:1]
        den = a_re * a_re + a_im * a_im
        cf_r = (pr * a_re + pi * a_im) / den
        cf_i = (pi * a_re - pr * a_im) / den
        bt_c, bt_s = bt_ref[g, 0], bt_ref[g, 1]
        bb_c = cf_r * bt_c + cf_i * bt_s
        bb_s = cf_r * bt_s - cf_i * bt_c
        c_c, c_s = c_ref[g, 0], c_ref[g, 1]

        e1r, e1i, e2r, e2i = cpow(tau_col - half)
        e3r, e3i = cmul(e1r, e1i, kr[1:2], ki[1:2])
        e4r, e4i = cmul(e2r, e2i, kr[2:3], ki[2:3])

        def outer(er, ei, w_c, w_s):
            return (er[:, None, :] * w_c[None, :, :] + ei[:, None, :] * w_s[None, :, :]).reshape(width, n2)

        qm = outer(e1r, e1i, c_c, c_s)
        km_conj = outer(e2r, e2i, bb_c * conj, bb_s * conj)
        qc = outer(e3r, e3i, c_c, c_s)
        wz = outer(e4r, e4i, bb_c, bb_s)

        chans = pl.ds(pl.multiple_of(g * SSM_GROUP, SSM_GROUP), SSM_GROUP)
        ub = vt[:, chans, :].reshape(width, n_chunks).astype(BF16)
        yield

        tt = _dot_nt(qm.astype(BF16), km_conj.astype(BF16))
        x = _dot(wz.T.astype(BF16), ub).T
        yield
        y = _dot((tt * tmask[...]).astype(BF16), ub)
        p_r, p_i = kr[3:4], ki[3:4]
        d = 1
        while d < n_chunks:
            s = jnp.where(row >= d, pltpu.roll(x, d, 0), 0.0)
            x = x + s * p_r - pltpu.roll(s, SSM_STATE, 1) * (p_i * conj)
            p_r, p_i = cmul(p_r, p_i, p_r, p_i)
            d *= 2
        prev_conj = jnp.where(row >= 1, pltpu.roll(x, 1, 0), 0.0) * conj
        yield
        y = y + _dot_nt(qc.astype(BF16), prev_conj.astype(BF16))
        yt[:, chans, :] = y.reshape(L, SSM_GROUP, n_chunks)

    def group_batch(k, carry):
        running = [group(S5_GROUPS_PER_TRIP * k + j) for j in range(S5_GROUPS_PER_TRIP)]
        finished = object()
        while running:
            running = [gen for gen in running if next(gen, finished) is not finished]
        return carry

    lax.fori_loop(0, LANES // SSM_GROUP // S5_GROUPS_PER_TRIP, group_batch, 0)
    by_chunk = jnp.swapaxes(jnp.stack([yt[tau].T for tau in range(L)], axis=0), 0, 1)
    y = by_chunk.reshape(n_chunks * L, LANES) + d_ref[...] * x_ref[...]
    y_ref[...] = jax.nn.gelu(y).astype(y_ref.dtype)


def _s5_mixer(h, hn, a_re, a_im, log_dt, b_re, b_im, c_re, c_im, d_skip, w_glu, layer):
    s = h.shape[0]
    L = S5_CHUNK
    n_chunks = s // L
    width = L * SSM_GROUP
    gpb = LANES // SSM_GROUP
    def packed(re, im):
        return jnp.stack([jnp.concatenate([re, im], -1), jnp.concatenate([-im, re], -1)], axis=1).astype(F32)

    lam = jnp.stack([a_re, a_im, jnp.broadcast_to(log_dt[:, None], a_re.shape)], axis=1).astype(F32)
    lam = jnp.concatenate([lam, lam], axis=-1)
    bt = packed(b_re.transpose(0, 2, 1), b_im.transpose(0, 2, 1))
    cc = packed(c_re, c_im)
    dd = d_skip.astype(F32).reshape(1, D_MODEL)
    y = pl.pallas_call(
        functools.partial(_s5_kernel, chunk=L),
        out_shape=jax.ShapeDtypeStruct((s, D_MODEL), BF16),
        grid=(SSM_GROUPS // gpb,),
        in_specs=[
            pl.BlockSpec((s, LANES), lambda b: (0, b)),
            pl.BlockSpec((gpb, 3, 2 * SSM_STATE), lambda b: (b, 0, 0)),
            pl.BlockSpec((gpb, 2, SSM_GROUP, 2 * SSM_STATE), lambda b: (b, 0, 0, 0)),
            pl.BlockSpec((gpb, 2, SSM_GROUP, 2 * SSM_STATE), lambda b: (b, 0, 0, 0)),
            pl.BlockSpec((1, LANES), lambda b: (0, b)),
        ],
        out_specs=pl.BlockSpec((s, LANES), lambda b: (0, b)),
        scratch_shapes=[
            pltpu.VMEM((L, LANES, n_chunks), hn.dtype),
            pltpu.VMEM((L, LANES, n_chunks), F32),
            pltpu.VMEM((width, width), F32),
        ],
        compiler_params=_params("arbitrary"),
    )(hn, lam, bt, cc, dd)
    return _matmul(_ep_glu_resid, y, w_glu, layer, D_MODEL, F32, 1024, 512, b_halves=2, tile_extra=h)


def kernel(x, rel_bias, mix_norm_g, ffn_norm_g, final_norm_g, nsa_w_in, nsa_w_out, cmp_pos_k, cmp_w1_k, cmp_w2_k, cmp_pos_v, cmp_w1_v, cmp_w2_v, s5_A_re, s5_A_im, s5_log_dt, s5_B_re, s5_B_im, s5_C_re, s5_C_im, s5_D, s5_w_glu, ffn_w_in, ffn_w_out):
    assert x.shape[0] == 1
    h = x[0]
    hn = _rmsnorm(h, mix_norm_g[0], BF16)
    h, hn = _nsa_mixer(h, hn, rel_bias, nsa_w_in, nsa_w_out, 0, cmp_pos_k[0], cmp_w1_k[0], cmp_w2_k[0],
                       cmp_pos_v[0], cmp_w1_v[0], cmp_w2_v[0], ffn_norm_g[0])
    h = _swiglu_ffn(h, hn, ffn_w_in, ffn_w_out, 0)
    hn = _rmsnorm(h, mix_norm_g[1], BF16)
    h = _s5_mixer(h, hn, s5_A_re[0], s5_A_im[0], s5_log_dt[0], s5_B_re[0], s5_B_im[0], s5_C_re[0], s5_C_im[0],
                  s5_D[0], s5_w_glu, 0)
    h = _swiglu_ffn(h, _rmsnorm(h, ffn_norm_g[1], BF16), ffn_w_in, ffn_w_out, 1)
    return _rmsnorm(h, final_norm_g, x.dtype)[None]
```

```python
import functools
import math

import numpy as np
import jax
import jax.numpy as jnp
from jax import lax
from jax.experimental import pallas as pl
from jax.experimental.pallas import tpu as pltpu

D_MODEL = 2048
N_HEADS = 16
HEAD_DIM = 128
N_KV_GROUPS = 4
HPG = N_HEADS // N_KV_GROUPS
CMP_BLOCK = 32
CMP_STRIDE = 16
CMP_HIDDEN = 2 * HEAD_DIM
SEL_BLOCK = 64
SEL_TOPK = 16
N_FORCED = 3
WINDOW = 512
N_GATES = 3
KV_WIDTH = N_KV_GROUPS * HEAD_DIM
Q_WIDTH = N_HEADS * HEAD_DIM
PROJ_MAIN = Q_WIDTH + 6 * KV_WIDTH
REL_BUCKETS = 32
REL_MAX_DIST = 128
SSM_GROUP = 16
SSM_GROUPS = D_MODEL // SSM_GROUP
SSM_STATE = 64
D_FF = ((8 * D_MODEL + 2) // 3 + 255) // 256 * 256
RMS_EPS = 1e-6
NEG = -1e30
LOG2E = math.log2(math.e)

LANES = 128
SUBLANES = 8
VMEM_LIMIT = 52 * 1024 * 1024
RING_VMEM_LIMIT = 57 * 1024 * 1024

ATT_TILE = WINDOW // 2
FAR_TILES = 4
Q_TILES = 2
CMP_TQ = 256
CMP_PAD = 8
S5_CHUNK = 32
S5_GROUPS_PER_TRIP = 4

BF16 = jnp.bfloat16
F32 = jnp.float32


def _dot(a, b):
    return jnp.dot(a, b, preferred_element_type=F32)


def _dot_nt(a, b):
    return lax.dot_general(a, b, (((1,), (1,)), ((), ())), preferred_element_type=F32)


def _params(*sem, flags=None):
    return pltpu.CompilerParams(dimension_semantics=sem, vmem_limit_bytes=VMEM_LIMIT, flags=flags)


def _rmsnorm_kernel(x_ref, g_ref, o_ref):
    x = x_ref[...]
    ms = jnp.mean(x * x, axis=-1, keepdims=True)
    o_ref[...] = (x * lax.rsqrt(ms + RMS_EPS) * g_ref[...]).astype(o_ref.dtype)


def _rmsnorm(x, g, out_dtype, tm=1024):
    s, d = x.shape
    return pl.pallas_call(
        _rmsnorm_kernel,
        out_shape=jax.ShapeDtypeStruct((s, d), out_dtype),
        grid=(s // tm,),
        in_specs=[pl.BlockSpec((tm, d), lambda i: (i, 0)), pl.BlockSpec((1, d), lambda i: (0, 0))],
        out_specs=pl.BlockSpec((tm, d), lambda i: (i, 0)),
        compiler_params=_params("parallel"),
    )(x, g.reshape(1, d))


def _mm_kernel(*refs, n_b, n_extra, epilogue, w_rows_are_outputs):
    a_ref = refs[0]
    b_refs = refs[1:1 + n_b]
    extra_refs = refs[1 + n_b:1 + n_b + n_extra]
    o_ref = refs[1 + n_b + n_extra]
    w_scr = refs[2 + n_b + n_extra:]

    @pl.when(pl.program_id(1) == 0)
    def _():
        for b_ref, w in zip(b_refs, w_scr):
            w[...] = b_ref[...].astype(BF16)

    a = a_ref[...]
    z = [(_dot_nt if w_rows_are_outputs else _dot)(a, w[...]) for w in w_scr]
    o_ref[...] = epilogue(*z, *[e[...] for e in extra_refs]).astype(o_ref.dtype)


def _ep_scale(z, scale):
    return z * scale


def _ep_resid(z, resid):
    return resid + z


def _ep_swiglu(za, zb):
    return jax.nn.silu(za) * zb


def _ep_glu_resid(za, zb, resid):
    return resid + za * jax.nn.sigmoid(zb)


def _matmul(epilogue, a, b, layer, n_out, out_dtype, tm, tn, *, b_halves=1, row_extra=None, tile_extra=None,
            w_rows_are_outputs=False):
    m, k = a.shape
    nj = n_out // tn
    estimate = (b_halves * k * tn * (2 * 4 + 2) + 2 * tm * k * 2 + 4 * tm * tn * 4 + b_halves * tm * tn * 4)
    b_mode = pl.Buffered(1) if estimate > VMEM_LIMIT else None
    in_specs = [pl.BlockSpec((tm, k), lambda j, i: (i, 0))]
    args = [a]
    for half in range(b_halves):
        if w_rows_are_outputs:
            spec = pl.BlockSpec((None, tn, k), functools.partial(lambda j, i, o: (layer, j + o, 0), o=half * nj),
                                pipeline_mode=b_mode)
        else:
            spec = pl.BlockSpec((None, k, tn), functools.partial(lambda j, i, o: (layer, 0, j + o), o=half * nj),
                                pipeline_mode=b_mode)
        in_specs.append(spec)
        args.append(b)
    extras = []
    if row_extra is not None:
        in_specs.append(pl.BlockSpec((1, tn), lambda j, i: (0, j)))
        extras.append(row_extra)
    if tile_extra is not None:
        in_specs.append(pl.BlockSpec((tm, tn), lambda j, i: (i, j)))
        extras.append(tile_extra)
    kern = functools.partial(_mm_kernel, n_b=b_halves, n_extra=len(extras), epilogue=epilogue,
                             w_rows_are_outputs=w_rows_are_outputs)
    return pl.pallas_call(
        kern,
        out_shape=jax.ShapeDtypeStruct((m, n_out), out_dtype),
        grid=(nj, m // tm),
        in_specs=in_specs,
        out_specs=pl.BlockSpec((tm, tn), lambda j, i: (i, j)),
        scratch_shapes=[pltpu.VMEM((tn, k) if w_rows_are_outputs else (k, tn), BF16) for _ in range(b_halves)],
        compiler_params=_params("arbitrary", "arbitrary"),
    )(*args, *extras)


def _mm_resid_norm_kernel(a_ref, b_ref, r_ref, g_ref, h_ref, hn_ref, w_scr):
    @pl.when(pl.program_id(0) == 0)
    def _():
        w_scr[...] = b_ref[...].astype(BF16)

    h = r_ref[...] + _dot(a_ref[...], w_scr[...])
    h_ref[...] = h
    ms = jnp.mean(h * h, axis=-1, keepdims=True)
    hn_ref[...] = (h * lax.rsqrt(ms + RMS_EPS) * g_ref[...]).astype(hn_ref.dtype)


def _matmul_resid_norm(a, b, layer, resid, g, tm):
    m, k = a.shape
    n = b.shape[2]
    return pl.pallas_call(
        _mm_resid_norm_kernel,
        out_shape=(jax.ShapeDtypeStruct((m, n), F32), jax.ShapeDtypeStruct((m, n), BF16)),
        grid=(m // tm,),
        in_specs=[
            pl.BlockSpec((tm, k), lambda i: (i, 0)),
            pl.BlockSpec((None, k, n), lambda i: (layer, 0, 0), pipeline_mode=pl.Buffered(1)),
            pl.BlockSpec((tm, n), lambda i: (i, 0)),
            pl.BlockSpec((1, n), lambda i: (0, 0)),
        ],
        out_specs=(pl.BlockSpec((tm, n), lambda i: (i, 0)), pl.BlockSpec((tm, n), lambda i: (i, 0))),
        scratch_shapes=[pltpu.VMEM((k, n), BF16)],
        compiler_params=_params("arbitrary"),
    )(a, b, resid, g.reshape(1, n))


RING = 3


def _mm_resid_ring_kernel(a_hbm, b_hbm, r_ref, o_ref, w_scr, w_stage, a_ring, sems, w_sems, *, layer, tm, tn,
                          n_row_tiles, n_col_blocks, n_steps):
    j = pl.program_id(0)
    i = pl.program_id(1)
    step = j * n_row_tiles + i

    def tile_copy(s):
        rows = pl.ds(pl.multiple_of((s % n_row_tiles) * tm, tm), tm)
        return pltpu.make_async_copy(a_hbm.at[rows, :], a_ring.at[s % RING], sems.at[s % RING])

    def weight_copy(c):
        cols = pl.ds(pl.multiple_of(c * tn, tn), tn)
        return pltpu.make_async_copy(b_hbm.at[layer, :, cols], w_stage.at[c % 2], w_sems.at[c % 2])

    @pl.when(step == 0)
    def _():
        weight_copy(0).start()
        for s in range(RING - 1):
            tile_copy(s).start()

    @pl.when(step + RING - 1 < n_steps)
    def _():
        tile_copy(step + RING - 1).start()

    @pl.when(i == 0)
    def _():
        @pl.when(j + 1 < n_col_blocks)
        def _():
            weight_copy(j + 1).start()

        weight_copy(j).wait()
        w_scr[...] = w_stage[j % 2].astype(BF16)

    tile_copy(step).wait()
    o_ref[...] = r_ref[...] + _dot(a_ring[step % RING], w_scr[...])


def _matmul_resid_ring(a, b, layer, resid, tm, tn):
    m, k = a.shape
    n = b.shape[2]
    n_row_tiles = m // tm
    n_steps = (n // tn) * n_row_tiles
    assert n_steps >= RING - 1
    kern = functools.partial(_mm_resid_ring_kernel, layer=layer, tm=tm, tn=tn, n_row_tiles=n_row_tiles,
                             n_col_blocks=n // tn, n_steps=n_steps)
    return pl.pallas_call(
        kern,
        out_shape=jax.ShapeDtypeStruct((m, n), F32),
        grid=(n // tn, n_row_tiles),
        in_specs=[
            pl.BlockSpec(memory_space=pl.ANY),
            pl.BlockSpec(memory_space=pl.ANY),
            pl.BlockSpec((tm, tn), lambda j, i: (i, j)),
        ],
        out_specs=pl.BlockSpec((tm, tn), lambda j, i: (i, j)),
        scratch_shapes=[pltpu.VMEM((k, tn), BF16), pltpu.VMEM((2, k, tn), F32), pltpu.VMEM((RING, tm, k), BF16),
                        pltpu.SemaphoreType.DMA((RING,)), pltpu.SemaphoreType.DMA((2,))],
        compiler_params=pltpu.CompilerParams(dimension_semantics=("arbitrary", "arbitrary"),
                                             vmem_limit_bytes=RING_VMEM_LIMIT),
    )(a, b, resid)


def _swiglu_ffn(h, hn, w_in, w_out, layer):
    act = _matmul(_ep_swiglu, hn, w_in, layer, D_FF, BF16, 1024, 512, b_halves=2)
    return _matmul_resid_ring(act, w_out, layer, h, 512, 512)


def _compress_kernel(x_ref, w1_ref, w2_ref, pos_ref, o_ref):
    ncp = x_ref.shape[0] // CMP_STRIDE
    half = CMP_STRIDE * HEAD_DIM
    by_offset = jnp.swapaxes(x_ref[...].reshape(ncp, CMP_STRIDE, HEAD_DIM), 0, 1)
    rows = jnp.concatenate([by_offset[r] for r in range(CMP_STRIDE)], axis=1)
    top = _dot(rows, w1_ref[0, :half, :])
    bot = _dot(rows, w1_ref[0, half:, :])
    posb = _dot(pos_ref[0], w1_ref[0])[0:1]
    hid = top + pltpu.roll(bot, ncp - 1, 0) + posb
    o_ref[0, 0] = _dot(jax.nn.gelu(hid).astype(BF16), w2_ref[0]).astype(o_ref.dtype)


def _compress(proj, w1, w2, pos):
    s = proj.shape[0]
    ncp = s // CMP_STRIDE
    kc_blk = Q_WIDTH // HEAD_DIM
    return pl.pallas_call(
        _compress_kernel,
        out_shape=jax.ShapeDtypeStruct((2, N_KV_GROUPS, ncp, HEAD_DIM), BF16),
        grid=(2, N_KV_GROUPS),
        in_specs=[
            pl.BlockSpec((s, HEAD_DIM), lambda kv, gi: (0, kc_blk + kv * N_KV_GROUPS + gi)),
            pl.BlockSpec((1, CMP_BLOCK * HEAD_DIM, CMP_HIDDEN), lambda kv, gi: (kv, 0, 0)),
            pl.BlockSpec((1, CMP_HIDDEN, HEAD_DIM), lambda kv, gi: (kv, 0, 0)),
            pl.BlockSpec((1, SUBLANES, CMP_BLOCK * HEAD_DIM), lambda kv, gi: (kv, 0, 0)),
        ],
        out_specs=pl.BlockSpec((1, 1, ncp, HEAD_DIM), lambda kv, gi: (kv, gi, 0, 0)),
        compiler_params=_params("parallel", "parallel"),
    )(proj, w1, w2, pos)


def _first_argmax(score):
    n_rows, n = score.shape
    sub = lax.broadcasted_iota(jnp.int32, (SUBLANES, n), 0)
    pairs = [(score[r:r + SUBLANES], sub + r) for r in range(0, n_rows, SUBLANES)]
    while len(pairs) > 2:
        merged = []
        for k in range(0, len(pairs) - 1, 2):
            (va, ia), (vb, ib) = pairs[k], pairs[k + 1]
            take = vb > va
            merged.append((jnp.where(take, vb, va), jnp.where(take, ib, ia)))
        if len(pairs) % 2:
            merged.append(pairs[-1])
        pairs = merged
    val = jnp.concatenate([v for v, _ in pairs], axis=0)
    idx = jnp.concatenate([i for _, i in pairs], axis=0)
    best = jnp.max(val, axis=0, keepdims=True)
    return jnp.min(jnp.where(val == best, idx, n_rows), axis=0, keepdims=True)


def _cmp_attn_kernel(q_ref, kcc_ref, vcct_ref, band_ref, mt_ref, oc_ref, nm_ref, s_scr, *, tq, nb):
    i = pl.program_id(1)
    band_start = pl.multiple_of(i * (tq // CMP_STRIDE), SUBLANES)
    n_rows = kcc_ref.shape[1]
    n_blk = mt_ref.shape[0]

    def body(n_r, n_b):
        kcc = kcc_ref[0, :n_r, :]
        vcct = vcct_ref[0, :, :n_r]
        rows = lax.broadcasted_iota(jnp.int32, (n_r, tq), 0)
        row_bias = jnp.where(rows >= CMP_PAD, jnp.where(rows < band_start + nb, 0.0, NEG), NEG)
        imp = jnp.zeros((n_r, tq), F32)

        def scores(h):
            qh = q_ref[:, h * HEAD_DIM:(h + 1) * HEAD_DIM]
            s_scr[h, :n_r, :] = _dot_nt(kcc, qh) + row_bias
            s_scr[h, pl.ds(band_start, nb), :] += band_ref[h]

        scores(0)
        probs = []
        for h in range(HPG):
            if h + 1 < HPG:
                scores(h + 1)
            s = s_scr[h, :n_r, :]
            m = jnp.maximum(jnp.max(s, axis=0, keepdims=True), 0.5 * NEG)
            e = jnp.exp2(s - m)
            l = jnp.sum(e, axis=0, keepdims=True)
            pn = e * (1.0 / jnp.where(l > 0.0, l, 1.0))
            imp = imp + pn
            probs.append(pn.astype(BF16))

        def outputs():
            for h in range(HPG):
                oct_h = _dot(vcct, probs[h])
                oc_ref[:, h * HEAD_DIM:(h + 1) * HEAD_DIM] = oct_h.T
                yield
        mt = mt_ref[:n_b, :n_r]
        hi = imp.astype(BF16)
        rest = imp - hi.astype(F32)
        mid = rest.astype(BF16)
        lo = (rest - mid.astype(F32)).astype(BF16)
        slc = _dot(mt, hi) + _dot(mt, mid) + _dot(mt, lo)
        jj = lax.broadcasted_iota(jnp.int32, (n_b, tq), 0)
        t_blk = (i * tq + lax.broadcasted_iota(jnp.int32, (n_b, tq), 1)) // SEL_BLOCK
        forced = jnp.where(jj == 0, 1, jnp.where(jj == t_blk, 1, jnp.where(jj == t_blk - 1, 1, 0)))
        picked = jnp.where(forced == 1, 1.0, 0.0)
        score = jnp.where(forced == 1, -jnp.inf, jnp.where(jj <= t_blk, slc, NEG))
        pending = outputs()
        for r in range(SEL_TOPK - N_FORCED):
            hit = jj == _first_argmax(score)
            picked = jnp.where(hit, 1.0, picked)
            score = jnp.where(hit, -jnp.inf, score)
            if r % 3 == 0:
                next(pending, None)
        for _ in pending:
            pass
        neg_mask = jnp.where(picked > 0.5, jnp.where(jj <= t_blk, 0.0, NEG), NEG)
        if n_b < n_blk:
            neg_mask = jnp.concatenate([neg_mask, jnp.full((n_blk - n_b, tq), NEG, F32)], axis=0)
        nm_ref[...] = neg_mask.T.astype(nm_ref.dtype)

    step = LANES
    extents = list(range(step, n_rows, step)) + [n_rows]
    need_rows = band_start + nb
    for v, n_r in enumerate(extents):
        lower = extents[v - 1] if v else 0
        n_b = min(n_blk, -(-(n_r * CMP_STRIDE // SEL_BLOCK) // SUBLANES) * SUBLANES)
        pl.when(jnp.logical_and(need_rows > lower, need_rows <= n_r))(functools.partial(body, n_r, n_b))


def _cmp_attn(proj, kcc, vcct, band, mt, tq):
    s = proj.shape[0]
    n_rows = kcc.shape[1]
    nb = band.shape[1]
    kern = functools.partial(_cmp_attn_kernel, tq=tq, nb=nb)
    return pl.pallas_call(
        kern,
        out_shape=(jax.ShapeDtypeStruct((s, Q_WIDTH), F32),
                   jax.ShapeDtypeStruct((s, N_KV_GROUPS * LANES), BF16)),
        grid=(N_KV_GROUPS, s // tq),
        in_specs=[
            pl.BlockSpec((tq, HPG * HEAD_DIM), lambda g, i: (i, g)),
            pl.BlockSpec((1, n_rows, HEAD_DIM), lambda g, i: (g, 0, 0)),
            pl.BlockSpec((1, HEAD_DIM, n_rows), lambda g, i: (g, 0, 0)),
            pl.BlockSpec((HPG, nb, tq), lambda g, i: (g, 0, 0)),
            pl.BlockSpec((LANES, n_rows), lambda g, i: (0, 0)),
        ],
        out_specs=(pl.BlockSpec((tq, HPG * HEAD_DIM), lambda g, i: (i, g)),
                   pl.BlockSpec((tq, LANES), lambda g, i: (i, g))),
        scratch_shapes=[pltpu.VMEM((HPG, n_rows, tq), F32)],
        compiler_params=_params("parallel", "parallel"),
    )(proj, kcc, vcct, band, mt)


def _lane_tile(x, width):
    return jnp.concatenate([x] * (width // x.shape[1]), axis=1)


def _softmax_update(s, v_aug, m_ref, acc_ref, rows):
    m_prev = m_ref[rows, :]
    m_next = jnp.maximum(m_prev, jnp.max(s, axis=1, keepdims=True))
    alpha = jnp.exp2(m_prev - m_next)
    p = jnp.exp2(s - _lane_tile(m_next, s.shape[1]))
    acc_ref[rows, :] = _lane_tile(alpha, acc_ref.shape[1]) * acc_ref[rows, :] + _dot(p.astype(BF16), v_aug)
    m_ref[rows, :] = m_next


def _softmax_once(s, v_aug):
    m = jnp.broadcast_to(jnp.max(s, axis=1, keepdims=True), (s.shape[0], LANES))
    p = jnp.exp2(s - _lane_tile(m, s.shape[1]))
    return _dot(p.astype(BF16), v_aug)


def _sel_win_kernel(q_ref, nm_ref, ks_ref, vs_ref, kw_ref, vw_ref, blk_ref, per_ref, g_ref, oc_ref, o_ref,
                    q4, m_s, a_s, bias_ref, *, tile):
    step = pl.program_id(1)
    t = tile
    first = step * Q_TILES

    @pl.when(step == 0)
    def _():
        tl = lax.broadcasted_iota(jnp.int32, (t, t), 0)
        kl = lax.broadcasted_iota(jnp.int32, (t, t), 1)
        edge = jnp.where(tl < kl, 0.0, NEG)
        for h in range(HPG):
            bias_ref[h, :, :t] = edge
            for which in range(2):
                rows = jnp.broadcast_to(per_ref[h, which:which + 1, :], (t, 2 * t))
                table = pltpu.roll(rows, 0, 1, stride=1, stride_axis=0)
                bias_ref[h, :, (which + 1) * t:(which + 2) * t] = table[:, :t]

    def stacked(qt, h):
        return slice((qt * HPG + h) * t, (qt * HPG + h + 1) * t)

    def tile_rows(qt):
        return slice(qt * t, (qt + 1) * t)

    for qt in range(Q_TILES):
        for h in range(HPG):
            q4[stacked(qt, h), :HEAD_DIM] = q_ref[tile_rows(qt), h * HEAD_DIM:(h + 1) * HEAD_DIM]
            q4[stacked(qt, h), HEAD_DIM:] = nm_ref[tile_rows(qt), :]
    m_s[...] = jnp.full(m_s.shape, -3e38, F32)
    a_s[...] = jnp.zeros(a_s.shape, F32)

    def sel_operands(row0, n_tiles):
        keys = pl.ds(row0, n_tiles * t)
        k_aug = jnp.concatenate([ks_ref[keys, :], blk_ref[keys, :]], axis=1)
        v_aug = jnp.concatenate([vs_ref[keys, :], jnp.ones((n_tiles * t, LANES), BF16)], axis=1)
        return k_aug, v_aug

    def sel_pass(qt, operands, n_biased):
        k_aug, v_aug = operands

        def score(h):
            s = _dot_nt(q4[stacked(qt, h), :], k_aug)
            plain = s.shape[1] - n_biased * t
            if n_biased:
                near = s[:, plain:] + bias_ref[h, :, (3 - n_biased) * t:]
                s = jnp.concatenate([s[:, :plain], near], axis=1) if plain else near
            return s

        def consume(h, s):
            _softmax_update(s, v_aug, m_s, a_s, stacked(qt, h))

        return score, consume

    def win_pass(qt, row0, n_tiles):
        keys = pl.ds(row0, n_tiles * t)
        v_aug = jnp.concatenate([vw_ref[keys, :], jnp.ones((n_tiles * t, LANES), BF16)], axis=1)

        def score(h):
            return _dot_nt(q4[stacked(qt, h), :HEAD_DIM], kw_ref[keys, :]) + bias_ref[h, :, (3 - n_tiles) * t:]

        def consume(h, s):
            acc_w = _softmax_once(s, v_aug)
            acc_s = a_s[stacked(qt, h), :]
            o_w = acc_w[:, :HEAD_DIM] * (1.0 / acc_w[:, HEAD_DIM:])
            o_s = acc_s[:, :HEAD_DIM] * (1.0 / acc_s[:, HEAD_DIM:])
            cols = slice(h * HEAD_DIM, (h + 1) * HEAD_DIM)
            gates = g_ref[tile_rows(qt), :]
            o = (gates[:, h:h + 1] * oc_ref[tile_rows(qt), cols] + gates[:, HPG + h:HPG + h + 1] * o_s
                 + gates[:, 2 * HPG + h:2 * HPG + h + 1] * o_w)
            o_ref[tile_rows(qt), cols] = o.astype(o_ref.dtype)

        return score, consume

    def run(passes):
        work = [(score, consume, h0) for score, consume in passes for h0 in range(0, HPG, 2)]
        ahead = [work[0][0](h) for h in (0, 1)]
        for n, (_, consume, h0) in enumerate(work):
            now = ahead
            if n + 1 < len(work):
                nxt_score, _, nxt_h0 = work[n + 1]
                ahead = [nxt_score(h) for h in (nxt_h0, nxt_h0 + 1)]
            for h, s in zip((h0, h0 + 1), now):
                consume(h, s)

    def far_passes(row0):
        operands = sel_operands(row0, FAR_TILES)
        return [sel_pass(qt, operands, 0) for qt in range(Q_TILES)]

    n_far = jnp.maximum(first - 1, 0)
    n_chunks = n_far // FAR_TILES
    chunk = FAR_TILES * t

    def far_pair(c, carry):
        row0 = pl.multiple_of(c * 2 * chunk, 2 * chunk)
        run(far_passes(row0) + far_passes(row0 + chunk))
        return carry

    lax.fori_loop(0, n_chunks // 2, far_pair, 0)

    @pl.when(n_chunks % 2 == 1)
    def _():
        run(far_passes(pl.multiple_of((n_chunks - 1) * chunk, chunk)))

    def tail(left):
        row0 = pl.multiple_of((n_far - left) * t, t)
        passes = []
        for qt in range(Q_TILES):
            n_sel = left + 2 + qt
            passes.append(sel_pass(qt, sel_operands(row0, n_sel), 2))
            passes.append(win_pass(qt, pl.multiple_of((first + qt - 2) * t, t), 3))
        run(passes)

    for left in range(FAR_TILES):
        if (left + 1) % Q_TILES == 0:
            pl.when(jnp.logical_and(step >= 1, n_far % FAR_TILES == left))(functools.partial(tail, left))

    @pl.when(step == 0)
    def _():
        passes = []
        for qt in range(Q_TILES):
            n_keys = min(qt + 1, 3)
            passes.append(sel_pass(qt, sel_operands(0, qt + 1), min(qt + 1, 2)))
            passes.append(win_pass(qt, (qt + 1 - n_keys) * t, n_keys))
        run(passes)


def _sel_win(proj, neg_mask, blk_onehot, periods, gates, o_c, tile):
    s = proj.shape[0]
    ks_blk = (Q_WIDTH + 2 * KV_WIDTH) // HEAD_DIM
    vs_blk = (Q_WIDTH + 3 * KV_WIDTH) // HEAD_DIM
    kw_blk = (Q_WIDTH + 4 * KV_WIDTH) // HEAD_DIM
    vw_blk = (Q_WIDTH + 5 * KV_WIDTH) // HEAD_DIM
    once = pl.Buffered(1)
    rows = Q_TILES * tile
    stacked_rows = HPG * rows
    kern = functools.partial(_sel_win_kernel, tile=tile)
    return pl.pallas_call(
        kern,
        out_shape=jax.ShapeDtypeStruct((s, Q_WIDTH), BF16),
        grid=(N_KV_GROUPS, s // rows),
        in_specs=[
            pl.BlockSpec((rows, HPG * HEAD_DIM), lambda g, i: (i, g)),
            pl.BlockSpec((rows, LANES), lambda g, i: (i, g)),
            pl.BlockSpec((s, HEAD_DIM), lambda g, i: (0, ks_blk + g)),
            pl.BlockSpec((s, HEAD_DIM), lambda g, i: (0, vs_blk + g)),
            pl.BlockSpec((s, HEAD_DIM), lambda g, i: (0, kw_blk + g)),
            pl.BlockSpec((s, HEAD_DIM), lambda g, i: (0, vw_blk + g)),
            pl.BlockSpec((s, LANES), lambda g, i: (0, 0), pipeline_mode=once),
            pl.BlockSpec((HPG, 2, 2 * tile), lambda g, i: (g, 0, 0)),
            pl.BlockSpec((rows, LANES), lambda g, i: (i, g)),
            pl.BlockSpec((rows, HPG * HEAD_DIM), lambda g, i: (i, g)),
        ],
        out_specs=pl.BlockSpec((rows, HPG * HEAD_DIM), lambda g, i: (i, g)),
        scratch_shapes=[
            pltpu.VMEM((stacked_rows, 2 * HEAD_DIM), BF16),
            pltpu.VMEM((stacked_rows, LANES), F32),
            pltpu.VMEM((stacked_rows, 2 * HEAD_DIM), F32),
            pltpu.VMEM((HPG, tile, 3 * tile), F32),
        ],
        compiler_params=_params("arbitrary", "arbitrary"),
    )(proj, neg_mask, proj, proj, proj, proj, blk_onehot, periods, gates, o_c)


def _rel_bucket_np(dist):
    n = np.maximum(dist, 0)
    max_exact = REL_BUCKETS // 2
    large = max_exact + (np.log(np.maximum(n, 1).astype(np.float32) / np.float32(max_exact))
                         / np.float32(math.log(REL_MAX_DIST / max_exact))
                         * np.float32(REL_BUCKETS - max_exact)).astype(np.int32)
    large = np.minimum(large, REL_BUCKETS - 1)
    return np.where(n < max_exact, n, large).astype(np.int32)


def _far_distance():
    d = np.arange(4 * REL_MAX_DIST)
    b = _rel_bucket_np(d)
    assert b[-1] == REL_BUCKETS - 1
    return int(np.max(np.nonzero(b != REL_BUCKETS - 1)[0])) + 1


def _bias_table(rel_bias, dist):
    bucket = jnp.asarray(_rel_bucket_np(dist))
    shifted = (rel_bias[bucket] - rel_bias[REL_BUCKETS - 1]) * LOG2E
    tab = jnp.where(jnp.asarray(dist >= 0)[..., None], shifted, NEG)
    return jnp.moveaxis(tab, -1, 0).astype(F32)


def _bias_period(rel_bias, offset, t):
    x = np.arange(2 * t)
    return _bias_table(rel_bias, np.where(x < t, offset - x, offset + 2 * t - x))


def _nsa_mixer(h, hn, rel_bias, w_in, w_out, layer, pos_k, w1_k, w2_k, pos_v, w1_v, w2_v, next_norm_g):
    s = h.shape[0]
    n_sel = s // SEL_BLOCK
    ncp = s // CMP_STRIDE
    tile = ATT_TILE
    far = _far_distance()
    assert n_sel <= LANES and s % CMP_TQ == 0 and s % (Q_TILES * tile) == 0 and FAR_TILES % Q_TILES == 0
    assert far <= tile and far <= CMP_STRIDE * CMP_PAD - CMP_BLOCK + 1 + CMP_STRIDE

    col_scale = jnp.where(jnp.arange(PROJ_MAIN) < Q_WIDTH, HEAD_DIM ** -0.5 * LOG2E, 1.0).astype(F32)[None, :]
    w_in_t = jnp.swapaxes(w_in, 1, 2)
    proj = _matmul(_ep_scale, hn, w_in_t, layer, PROJ_MAIN, BF16, 2048, 512, row_extra=col_scale,
                   w_rows_are_outputs=True)
    w_gate = w_in_t[layer, PROJ_MAIN:].reshape(N_GATES, N_KV_GROUPS, HPG, D_MODEL).transpose(1, 0, 2, 3)
    w_gate = w_gate.reshape(N_KV_GROUPS, N_GATES * HPG, D_MODEL)
    w_gate = jnp.pad(w_gate, ((0, 0), (0, LANES - N_GATES * HPG), (0, 0))).reshape(1, N_KV_GROUPS * LANES, D_MODEL)
    gates = _matmul(jax.nn.sigmoid, hn, w_gate, 0, N_KV_GROUPS * LANES, F32, 2048, 512, w_rows_are_outputs=True)

    w1 = jnp.stack([w1_k, w1_v]).astype(BF16)
    w2 = jnp.stack([w2_k, w2_v]).astype(BF16)
    pos = jnp.stack([pos_k.reshape(1, -1), pos_v.reshape(1, -1)]).astype(BF16)
    pos = jnp.broadcast_to(pos, (2, SUBLANES, CMP_BLOCK * HEAD_DIM))
    cc = _compress(proj, w1, w2, pos)
    kcc = jnp.pad(cc[0], ((0, 0), (CMP_PAD, 0), (0, 0)))
    vcct = jnp.pad(cc[1], ((0, 0), (CMP_PAD, 0), (0, 0))).transpose(0, 2, 1)
    n_rows = ncp + CMP_PAD

    nb = CMP_TQ // CMP_STRIDE + CMP_PAD
    first_dist = -CMP_STRIDE * (nb - 1 - CMP_PAD) - (CMP_BLOCK - 1)
    by_dist = _bias_table(rel_bias, first_dist + np.arange(CMP_TQ + CMP_STRIDE * (nb - 1)))
    band = jnp.stack([by_dist[:, CMP_STRIDE * (nb - 1 - b):CMP_STRIDE * (nb - 1 - b) + CMP_TQ] for b in range(nb)], axis=1)
    ratio = SEL_BLOCK // CMP_STRIDE
    lo = CMP_BLOCK // CMP_STRIDE - 1
    c_of_row = np.arange(n_rows)[None, :] - CMP_PAD
    j_of = np.arange(LANES)[:, None]
    mt = ((c_of_row >= ratio * j_of - lo) & (c_of_row <= ratio * j_of + ratio - 1)
          & (c_of_row >= 0) & (c_of_row <= ncp - 2) & (j_of < n_sel)).astype(np.float32)
    o_c, neg_mask = _cmp_attn(proj, kcc, vcct, band, jnp.asarray(mt, BF16), CMP_TQ)

    onehot = (np.arange(s)[:, None] // SEL_BLOCK == np.arange(LANES)[None, :]).astype(np.float32)
    periods = jnp.stack([_bias_period(rel_bias, tile, tile), _bias_period(rel_bias, 0, tile)], axis=1)
    o = _sel_win(proj, neg_mask, jnp.asarray(onehot, BF16), periods, gates, o_c, tile)
    return _matmul_resid_norm(o, w_out, layer, h, next_norm_g, 256)


def _s5_kernel(x_ref, lam_ref, bt_ref, c_ref, d_ref, y_ref, vt, yt, tmask, *, chunk):
    L = chunk
    half = L // 2
    width = L * SSM_GROUP
    n_chunks = x_ref.shape[0] // L
    by_step = jnp.swapaxes(x_ref[...].reshape(n_chunks, L, LANES), 0, 1)
    for tau in range(L):
        vt[tau] = by_step[tau].T

    @pl.when(pl.program_id(0) == 0)
    def _():
        dst = lax.broadcasted_iota(jnp.int32, (width, width), 0) // SSM_GROUP
        src = lax.broadcasted_iota(jnp.int32, (width, width), 1) // SSM_GROUP
        tmask[...] = jnp.where(src <= dst, 1.0, 0.0)

    def cmul(xr, xi, yr, yi):
        return xr * yr - xi * yi, xr * yi + xi * yr

    n2 = 2 * SSM_STATE
    sub = lax.broadcasted_iota(jnp.int32, (SUBLANES, 1), 0)
    consts = jnp.where(sub == 0, 1.0, jnp.where(sub == 1, float(half + 1), jnp.where(sub == 2, float(half - 1), float(L))))
    tau_col = lax.broadcasted_iota(jnp.int32, (L, 1), 0).astype(F32)
    row = lax.broadcasted_iota(jnp.int32, (n_chunks, n2), 0)
    conj = jnp.where(lax.broadcasted_iota(jnp.int32, (1, n2), 1) < SSM_STATE, 1.0, -1.0)

    def group(g):
        lam = lam_ref[g]
        a_re, a_im = lam[0:1], lam[1:2]
        dt = jnp.exp(lam[2:3])
        log_re, log_im = a_re * dt, a_im * dt

        def cpow(e):
            mag, cos, sin = jnp.exp(e * log_re), jnp.cos(e * log_im), jnp.sin(e * log_im)
            inv = 1.0 / mag
            return mag * cos, mag * sin, inv * cos, -(inv * sin)

        kr, ki, _, _ = cpow(consts)
        pr, pi = kr[0:1] - 1.0, ki[0:1]
        den = a_re * a_re + a_im * a_im
        cf_r = (pr * a_re + pi * a_im) / den
        cf_i = (pi * a_re - pr * a_im) / den
        bt_c, bt_s = bt_ref[g, 0], bt_ref[g, 1]
        bb_c = cf_r * bt_c + cf_i * bt_s
        bb_s = cf_r * bt_s - cf_i * bt_c
        c_c, c_s = c_ref[g, 0], c_ref[g, 1]

        e1r, e1i, e2r, e2i = cpow(tau_col - half)
        e3r, e3i = cmul(e1r, e1i, kr[1:2], ki[1:2])
        e4r, e4i = cmul(e2r, e2i, kr[2:3], ki[2:3])

        def outer(er, ei, w_c, w_s):
            return (er[:, None, :] * w_c[None, :, :] + ei[:, None, :] * w_s[None, :, :]).reshape(width, n2)

        qm = outer(e1r, e1i, c_c, c_s)
        km_conj = outer(e2r, e2i, bb_c * conj, bb_s * conj)
        qc = outer(e3r, e3i, c_c, c_s)
        wz = outer(e4r, e4i, bb_c, bb_s)

        chans = pl.ds(pl.multiple_of(g * SSM_GROUP, SSM_GROUP), SSM_GROUP)
        ub = vt[:, chans, :].reshape(width, n_chunks).astype(BF16)
        yield

        tt = _dot_nt(qm.astype(BF16), km_conj.astype(BF16))
        x = _dot(wz.T.astype(BF16), ub).T
        yield
        y = _dot((tt * tmask[...]).astype(BF16), ub)
        p_r, p_i = kr[3:4], ki[3:4]
        d = 1
        while d < n_chunks:
            s = jnp.where(row >= d, pltpu.roll(x, d, 0), 0.0)
            x = x + s * p_r - pltpu.roll(s, SSM_STATE, 1) * (p_i * conj)
            p_r, p_i = cmul(p_r, p_i, p_r, p_i)
            d *= 2
        prev_conj = jnp.where(row >= 1, pltpu.roll(x, 1, 0), 0.0) * conj
        yield
        y = y + _dot_nt(qc.astype(BF16), prev_conj.astype(BF16))
        yt[:, chans, :] = y.reshape(L, SSM_GROUP, n_chunks)

    def group_batch(k, carry):
        running = [group(S5_GROUPS_PER_TRIP * k + j) for j in range(S5_GROUPS_PER_TRIP)]
        finished = object()
        while running:
            running = [gen for gen in running if next(gen, finished) is not finished]
        return carry

    lax.fori_loop(0, LANES // SSM_GROUP // S5_GROUPS_PER_TRIP, group_batch, 0)
    by_chunk = jnp.swapaxes(jnp.stack([yt[tau].T for tau in range(L)], axis=0), 0, 1)
    y = by_chunk.reshape(n_chunks * L, LANES) + d_ref[...] * x_ref[...]
    y_ref[...] = jax.nn.gelu(y).astype(y_ref.dtype)


def _s5_mixer(h, hn, a_re, a_im, log_dt, b_re, b_im, c_re, c_im, d_skip, w_glu, layer):
    s = h.shape[0]
    L = S5_CHUNK
    n_chunks = s // L
    width = L * SSM_GROUP
    gpb = LANES // SSM_GROUP
    def packed(re, im):
        return jnp.stack([jnp.concatenate([re, im], -1), jnp.concatenate([-im, re], -1)], axis=1).astype(F32)

    lam = jnp.stack([a_re, a_im, jnp.broadcast_to(log_dt[:, None], a_re.shape)], axis=1).astype(F32)
    lam = jnp.concatenate([lam, lam], axis=-1)
    bt = packed(b_re.transpose(0, 2, 1), b_im.transpose(0, 2, 1))
    cc = packed(c_re, c_im)
    dd = d_skip.astype(F32).reshape(1, D_MODEL)
    y = pl.pallas_call(
        functools.partial(_s5_kernel, chunk=L),
        out_shape=jax.ShapeDtypeStruct((s, D_MODEL), BF16),
        grid=(SSM_GROUPS // gpb,),
        in_specs=[
            pl.BlockSpec((s, LANES), lambda b: (0, b)),
            pl.BlockSpec((gpb, 3, 2 * SSM_STATE), lambda b: (b, 0, 0)),
            pl.BlockSpec((gpb, 2, SSM_GROUP, 2 * SSM_STATE), lambda b: (b, 0, 0, 0)),
            pl.BlockSpec((gpb, 2, SSM_GROUP, 2 * SSM_STATE), lambda b: (b, 0, 0, 0)),
            pl.BlockSpec((1, LANES), lambda b: (0, b)),
        ],
        out_specs=pl.BlockSpec((s, LANES), lambda b: (0, b)),
        scratch_shapes=[
            pltpu.VMEM((L, LANES, n_chunks), hn.dtype),
            pltpu.VMEM((L, LANES, n_chunks), F32),
            pltpu.VMEM((width, width), F32),
        ],
        compiler_params=_params("arbitrary"),
    )(hn, lam, bt, cc, dd)
    return _matmul(_ep_glu_resid, y, w_glu, layer, D_MODEL, F32, 1024, 512, b_halves=2, tile_extra=h)


def kernel(x, rel_bias, mix_norm_g, ffn_norm_g, final_norm_g, nsa_w_in, nsa_w_out, cmp_pos_k, cmp_w1_k, cmp_w2_k, cmp_pos_v, cmp_w1_v, cmp_w2_v, s5_A_re, s5_A_im, s5_log_dt, s5_B_re, s5_B_im, s5_C_re, s5_C_im, s5_D, s5_w_glu, ffn_w_in, ffn_w_out):
    assert x.shape[0] == 1
    h = x[0]
    hn = _rmsnorm(h, mix_norm_g[0], BF16)
    h, hn = _nsa_mixer(h, hn, rel_bias, nsa_w_in, nsa_w_out, 0, cmp_pos_k[0], cmp_w1_k[0], cmp_w2_k[0],
                       cmp_pos_v[0], cmp_w1_v[0], cmp_w2_v[0], ffn_norm_g[0])
    h = _swiglu_ffn(h, hn, ffn_w_in, ffn_w_out, 0)
    hn = _rmsnorm(h, mix_norm_g[1], BF16)
    h = _s5_mixer(h, hn, s5_A_re[0], s5_A_im[0], s5_log_dt[0], s5_B_re[0], s5_B_im[0], s5_C_re[0], s5_C_im[0],
                  s5_D[0], s5_w_glu, 0)
    h = _swiglu_ffn(h, _rmsnorm(h, ffn_norm_g[1], BF16), ffn_w_in, ffn_w_out, 1)
    return _rmsnorm(h, final_norm_g, x.dtype)[None]
```

```python
import functools
import math

import numpy as np
import jax
import jax.numpy as jnp
from jax import lax
from jax.experimental import pallas as pl
from jax.experimental.pallas import tpu as pltpu

D_MODEL = 2048
N_HEADS = 16
HEAD_DIM = 128
N_KV_GROUPS = 4
HPG = N_HEADS // N_KV_GROUPS
CMP_BLOCK = 32
CMP_STRIDE = 16
CMP_HIDDEN = 2 * HEAD_DIM
SEL_BLOCK = 64
SEL_TOPK = 16
N_FORCED = 3
WINDOW = 512
N_GATES = 3
KV_WIDTH = N_KV_GROUPS * HEAD_DIM
Q_WIDTH = N_HEADS * HEAD_DIM
PROJ_MAIN = Q_WIDTH + 6 * KV_WIDTH
REL_BUCKETS = 32
REL_MAX_DIST = 128
SSM_GROUP = 16
SSM_GROUPS = D_MODEL // SSM_GROUP
SSM_STATE = 64
D_FF = ((8 * D_MODEL + 2) // 3 + 255) // 256 * 256
RMS_EPS = 1e-6
NEG = -1e30
LOG2E = math.log2(math.e)

LANES = 128
SUBLANES = 8
VMEM_LIMIT = 52 * 1024 * 1024
RING_VMEM_LIMIT = 57 * 1024 * 1024

ATT_TILE = WINDOW // 2
FAR_TILES = 4
Q_TILES = 2
CMP_TQ = 256
CMP_PAD = 8
S5_CHUNK = 32
S5_GROUPS_PER_TRIP = 4

BF16 = jnp.bfloat16
F32 = jnp.float32


def _dot(a, b):
    return jnp.dot(a, b, preferred_element_type=F32)


def _dot_nt(a, b):
    return lax.dot_general(a, b, (((1,), (1,)), ((), ())), preferred_element_type=F32)


def _params(*sem, flags=None):
    return pltpu.CompilerParams(dimension_semantics=sem, vmem_limit_bytes=VMEM_LIMIT, flags=flags)


def _rmsnorm_kernel(x_ref, g_ref, o_ref):
    x = x_ref[...]
    ms = jnp.mean(x * x, axis=-1, keepdims=True)
    o_ref[...] = (x * lax.rsqrt(ms + RMS_EPS) * g_ref[...]).astype(o_ref.dtype)


def _rmsnorm(x, g, out_dtype, tm=1024):
    s, d = x.shape
    return pl.pallas_call(
        _rmsnorm_kernel,
        out_shape=jax.ShapeDtypeStruct((s, d), out_dtype),
        grid=(s // tm,),
        in_specs=[pl.BlockSpec((tm, d), lambda i: (i, 0)), pl.BlockSpec((1, d), lambda i: (0, 0))],
        out_specs=pl.BlockSpec((tm, d), lambda i: (i, 0)),
        compiler_params=_params("parallel"),
    )(x, g.reshape(1, d))


def _mm_kernel(*refs, n_b, n_extra, epilogue, w_rows_are_outputs):
    a_ref = refs[0]
    b_refs = refs[1:1 + n_b]
    extra_refs = refs[1 + n_b:1 + n_b + n_extra]
    o_ref = refs[1 + n_b + n_extra]
    w_scr = refs[2 + n_b + n_extra:]

    @pl.when(pl.program_id(1) == 0)
    def _():
        for b_ref, w in zip(b_refs, w_scr):
            w[...] = b_ref[...].astype(BF16)

    a = a_ref[...]
    z = [(_dot_nt if w_rows_are_outputs else _dot)(a, w[...]) for w in w_scr]
    o_ref[...] = epilogue(*z, *[e[...] for e in extra_refs]).astype(o_ref.dtype)


def _ep_scale(z, scale):
    return z * scale


def _ep_resid(z, resid):
    return resid + z


def _ep_swiglu(za, zb):
    return jax.nn.silu(za) * zb


def _ep_glu_resid(za, zb, resid):
    return resid + za * jax.nn.sigmoid(zb)


def _matmul(epilogue, a, b, layer, n_out, out_dtype, tm, tn, *, b_halves=1, row_extra=None, tile_extra=None,
            w_rows_are_outputs=False):
    m, k = a.shape
    nj = n_out // tn
    estimate = (b_halves * k * tn * (2 * 4 + 2) + 2 * tm * k * 2 + 4 * tm * tn * 4 + b_halves * tm * tn * 4)
    b_mode = pl.Buffered(1) if estimate > VMEM_LIMIT else None
    in_specs = [pl.BlockSpec((tm, k), lambda j, i: (i, 0))]
    args = [a]
    for half in range(b_halves):
        if w_rows_are_outputs:
            spec = pl.BlockSpec((None, tn, k), functools.partial(lambda j, i, o: (layer, j + o, 0), o=half * nj),
                                pipeline_mode=b_mode)
        else:
            spec = pl.BlockSpec((None, k, tn), functools.partial(lambda j, i, o: (layer, 0, j + o), o=half * nj),
                                pipeline_mode=b_mode)
        in_specs.append(spec)
        args.append(b)
    extras = []
    if row_extra is not None:
        in_specs.append(pl.BlockSpec((1, tn), lambda j, i: (0, j)))
        extras.append(row_extra)
    if tile_extra is not None:
        in_specs.append(pl.BlockSpec((tm, tn), lambda j, i: (i, j)))
        extras.append(tile_extra)
    kern = functools.partial(_mm_kernel, n_b=b_halves, n_extra=len(extras), epilogue=epilogue,
                             w_rows_are_outputs=w_rows_are_outputs)
    return pl.pallas_call(
        kern,
        out_shape=jax.ShapeDtypeStruct((m, n_out), out_dtype),
        grid=(nj, m // tm),
        in_specs=in_specs,
        out_specs=pl.BlockSpec((tm, tn), lambda j, i: (i, j)),
        scratch_shapes=[pltpu.VMEM((tn, k) if w_rows_are_outputs else (k, tn), BF16) for _ in range(b_halves)],
        compiler_params=_params("arbitrary", "arbitrary"),
    )(*args, *extras)


def _mm_resid_norm_kernel(a_ref, b_ref, r_ref, g_ref, h_ref, hn_ref, w_scr):
    @pl.when(pl.program_id(0) == 0)
    def _():
        w_scr[...] = b_ref[...].astype(BF16)

    h = r_ref[...] + _dot(a_ref[...], w_scr[...])
    h_ref[...] = h
    ms = jnp.mean(h * h, axis=-1, keepdims=True)
    hn_ref[...] = (h * lax.rsqrt(ms + RMS_EPS) * g_ref[...]).astype(hn_ref.dtype)


def _matmul_resid_norm(a, b, layer, resid, g, tm):
    m, k = a.shape
    n = b.shape[2]
    return pl.pallas_call(
        _mm_resid_norm_kernel,
        out_shape=(jax.ShapeDtypeStruct((m, n), F32), jax.ShapeDtypeStruct((m, n), BF16)),
        grid=(m // tm,),
        in_specs=[
            pl.BlockSpec((tm, k), lambda i: (i, 0)),
            pl.BlockSpec((None, k, n), lambda i: (layer, 0, 0), pipeline_mode=pl.Buffered(1)),
            pl.BlockSpec((tm, n), lambda i: (i, 0)),
            pl.BlockSpec((1, n), lambda i: (0, 0)),
        ],
        out_specs=(pl.BlockSpec((tm, n), lambda i: (i, 0)), pl.BlockSpec((tm, n), lambda i: (i, 0))),
        scratch_shapes=[pltpu.VMEM((k, n), BF16)],
        compiler_params=_params("arbitrary"),
    )(a, b, resid, g.reshape(1, n))


RING = 3


def _mm_resid_ring_kernel(a_hbm, b_hbm, r_ref, o_ref, w_scr, w_stage, a_ring, sems, w_sems, *, layer, tm, tn, k_dim,
                          n_row_tiles, n_col_blocks, n_steps):
    j = pl.program_id(0)
    i = pl.program_id(1)
    step = j * n_row_tiles + i

    def tile_copy(s):
        rows = pl.ds(pl.multiple_of((s % n_row_tiles) * tm, tm), tm)
        return pltpu.make_async_copy(a_hbm.at[rows, :], a_ring.at[s % RING], sems.at[s % RING])

    def weight_copy(c):
        cols = pl.ds(pl.multiple_of(c * tn, tn), tn)
        return pltpu.make_async_copy(b_hbm.at[layer, :, cols], w_stage, w_sems.at[0])

    half = n_row_tiles // 2
    rows_per = k_dim // (n_row_tiles - half)
    has_next = j + 1 < n_col_blocks

    @pl.when(step == 0)
    def _():
        weight_copy(0).start()
        for s in range(RING - 1):
            tile_copy(s).start()
        weight_copy(0).wait()
        w_scr[0] = w_stage[...].astype(BF16)

    @pl.when(step + RING - 1 < n_steps)
    def _():
        tile_copy(step + RING - 1).start()

    @pl.when(jnp.logical_and(i == 0, has_next))
    def _():
        weight_copy(j + 1).start()

    @pl.when(jnp.logical_and(i == half, has_next))
    def _():
        weight_copy(j + 1).wait()

    @pl.when(jnp.logical_and(i >= half, has_next))
    def _():
        r = pl.ds(pl.multiple_of((i - half) * rows_per, rows_per), rows_per)
        w_scr[(j + 1) % 2, r, :] = w_stage[r, :].astype(BF16)

    tile_copy(step).wait()
    o_ref[...] = r_ref[...] + _dot(a_ring[step % RING], w_scr[j % 2])


def _matmul_resid_ring(a, b, layer, resid, tm, tn):
    m, k = a.shape
    n = b.shape[2]
    n_row_tiles = m // tm
    n_steps = (n // tn) * n_row_tiles
    assert n_steps >= RING - 1
    assert n_row_tiles >= 2 and k % (n_row_tiles - n_row_tiles // 2) == 0
    kern = functools.partial(_mm_resid_ring_kernel, layer=layer, tm=tm, tn=tn, k_dim=k, n_row_tiles=n_row_tiles,
                             n_col_blocks=n // tn, n_steps=n_steps)
    return pl.pallas_call(
        kern,
        out_shape=jax.ShapeDtypeStruct((m, n), F32),
        grid=(n // tn, n_row_tiles),
        in_specs=[
            pl.BlockSpec(memory_space=pl.ANY),
            pl.BlockSpec(memory_space=pl.ANY),
            pl.BlockSpec((tm, tn), lambda j, i: (i, j)),
        ],
        out_specs=pl.BlockSpec((tm, tn), lambda j, i: (i, j)),
        scratch_shapes=[pltpu.VMEM((2, k, tn), BF16), pltpu.VMEM((k, tn), F32), pltpu.VMEM((RING, tm, k), BF16),
                        pltpu.SemaphoreType.DMA((RING,)), pltpu.SemaphoreType.DMA((1,))],
        compiler_params=pltpu.CompilerParams(dimension_semantics=("arbitrary", "arbitrary"),
                                             vmem_limit_bytes=RING_VMEM_LIMIT),
    )(a, b, resid)


def _swiglu_ffn(h, hn, w_in, w_out, layer):
    act = _matmul(_ep_swiglu, hn, w_in, layer, D_FF, BF16, 1024, 512, b_halves=2)
    return _matmul_resid_ring(act, w_out, layer, h, 512, 512)


def _compress_kernel(x_ref, w1_ref, w2_ref, pos_ref, o_ref):
    ncp = x_ref.shape[0] // CMP_STRIDE
    half = CMP_STRIDE * HEAD_DIM
    by_offset = jnp.swapaxes(x_ref[...].reshape(ncp, CMP_STRIDE, HEAD_DIM), 0, 1)
    rows = jnp.concatenate([by_offset[r] for r in range(CMP_STRIDE)], axis=1)
    top = _dot(rows, w1_ref[0, :half, :])
    bot = _dot(rows, w1_ref[0, half:, :])
    posb = _dot(pos_ref[0], w1_ref[0])[0:1]
    hid = top + pltpu.roll(bot, ncp - 1, 0) + posb
    o_ref[0, 0] = _dot(jax.nn.gelu(hid).astype(BF16), w2_ref[0]).astype(o_ref.dtype)


def _compress(proj, w1, w2, pos):
    s = proj.shape[0]
    ncp = s // CMP_STRIDE
    kc_blk = Q_WIDTH // HEAD_DIM
    return pl.pallas_call(
        _compress_kernel,
        out_shape=jax.ShapeDtypeStruct((2, N_KV_GROUPS, ncp, HEAD_DIM), BF16),
        grid=(2, N_KV_GROUPS),
        in_specs=[
            pl.BlockSpec((s, HEAD_DIM), lambda kv, gi: (0, kc_blk + kv * N_KV_GROUPS + gi)),
            pl.BlockSpec((1, CMP_BLOCK * HEAD_DIM, CMP_HIDDEN), lambda kv, gi: (kv, 0, 0)),
            pl.BlockSpec((1, CMP_HIDDEN, HEAD_DIM), lambda kv, gi: (kv, 0, 0)),
            pl.BlockSpec((1, SUBLANES, CMP_BLOCK * HEAD_DIM), lambda kv, gi: (kv, 0, 0)),
        ],
        out_specs=pl.BlockSpec((1, 1, ncp, HEAD_DIM), lambda kv, gi: (kv, gi, 0, 0)),
        compiler_params=_params("parallel", "parallel"),
    )(proj, w1, w2, pos)


def _first_argmax(score):
    n_rows, n = score.shape
    sub = lax.broadcasted_iota(jnp.int32, (SUBLANES, n), 0)
    pairs = [(score[r:r + SUBLANES], sub + r) for r in range(0, n_rows, SUBLANES)]
    while len(pairs) > 2:
        merged = []
        for k in range(0, len(pairs) - 1, 2):
            (va, ia), (vb, ib) = pairs[k], pairs[k + 1]
            take = vb > va
            merged.append((jnp.where(take, vb, va), jnp.where(take, ib, ia)))
        if len(pairs) % 2:
            merged.append(pairs[-1])
        pairs = merged
    val = jnp.concatenate([v for v, _ in pairs], axis=0)
    idx = jnp.concatenate([i for _, i in pairs], axis=0)
    best = jnp.max(val, axis=0, keepdims=True)
    return jnp.min(jnp.where(val == best, idx, n_rows), axis=0, keepdims=True)


def _cmp_attn_kernel(q_ref, kcc_ref, vcct_ref, band_ref, mt_ref, oc_ref, nm_ref, s_scr, *, tq, nb):
    i = pl.program_id(1)
    band_start = pl.multiple_of(i * (tq // CMP_STRIDE), SUBLANES)
    n_rows = kcc_ref.shape[1]
    n_blk = mt_ref.shape[0]

    def body(n_r, n_b):
        kcc = kcc_ref[0, :n_r, :]
        vcct = vcct_ref[0, :, :n_r]
        rows = lax.broadcasted_iota(jnp.int32, (n_r, tq), 0)
        row_bias = jnp.where(rows >= CMP_PAD, jnp.where(rows < band_start + nb, 0.0, NEG), NEG)
        imp = jnp.zeros((n_r, tq), F32)

        def scores(h):
            qh = q_ref[:, h * HEAD_DIM:(h + 1) * HEAD_DIM]
            s_scr[h, :n_r, :] = _dot_nt(kcc, qh) + row_bias
            s_scr[h, pl.ds(band_start, nb), :] += band_ref[h]

        scores(0)
        probs = []
        for h in range(HPG):
            if h + 1 < HPG:
                scores(h + 1)
            s = s_scr[h, :n_r, :]
            m = jnp.maximum(jnp.max(s, axis=0, keepdims=True), 0.5 * NEG)
            e = jnp.exp2(s - m)
            l = jnp.sum(e, axis=0, keepdims=True)
            pn = e * (1.0 / jnp.where(l > 0.0, l, 1.0))
            imp = imp + pn
            probs.append(pn.astype(BF16))

        def outputs():
            for h in range(HPG):
                oct_h = _dot(vcct, probs[h])
                oc_ref[:, h * HEAD_DIM:(h + 1) * HEAD_DIM] = oct_h.T
                yield
        mt = mt_ref[:n_b, :n_r]
        hi = imp.astype(BF16)
        rest = imp - hi.astype(F32)
        mid = rest.astype(BF16)
        lo = (rest - mid.astype(F32)).astype(BF16)
        slc = _dot(mt, hi) + _dot(mt, mid) + _dot(mt, lo)
        jj = lax.broadcasted_iota(jnp.int32, (n_b, tq), 0)
        t_blk = (i * tq + lax.broadcasted_iota(jnp.int32, (n_b, tq), 1)) // SEL_BLOCK
        forced = jnp.where(jj == 0, 1, jnp.where(jj == t_blk, 1, jnp.where(jj == t_blk - 1, 1, 0)))
        picked = jnp.where(forced == 1, 1.0, 0.0)
        score = jnp.where(forced == 1, -jnp.inf, jnp.where(jj <= t_blk, slc, NEG))
        pending = outputs()
        for r in range(SEL_TOPK - N_FORCED):
            hit = jj == _first_argmax(score)
            picked = jnp.where(hit, 1.0, picked)
            score = jnp.where(hit, -jnp.inf, score)
            if r % 3 == 0:
                next(pending, None)
        for _ in pending:
            pass
        neg_mask = jnp.where(picked > 0.5, jnp.where(jj <= t_blk, 0.0, NEG), NEG)
        if n_b < n_blk:
            neg_mask = jnp.concatenate([neg_mask, jnp.full((n_blk - n_b, tq), NEG, F32)], axis=0)
        nm_ref[...] = neg_mask.T.astype(nm_ref.dtype)

    step = LANES
    extents = list(range(step, n_rows, step)) + [n_rows]
    need_rows = band_start + nb
    for v, n_r in enumerate(extents):
        lower = extents[v - 1] if v else 0
        n_b = min(n_blk, -(-(n_r * CMP_STRIDE // SEL_BLOCK) // SUBLANES) * SUBLANES)
        pl.when(jnp.logical_and(need_rows > lower, need_rows <= n_r))(functools.partial(body, n_r, n_b))


def _cmp_attn(proj, kcc, vcct, band, mt, tq):
    s = proj.shape[0]
    n_rows = kcc.shape[1]
    nb = band.shape[1]
    kern = functools.partial(_cmp_attn_kernel, tq=tq, nb=nb)
    return pl.pallas_call(
        kern,
        out_shape=(jax.ShapeDtypeStruct((s, Q_WIDTH), F32),
                   jax.ShapeDtypeStruct((s, N_KV_GROUPS * LANES), BF16)),
        grid=(N_KV_GROUPS, s // tq),
        in_specs=[
            pl.BlockSpec((tq, HPG * HEAD_DIM), lambda g, i: (i, g)),
            pl.BlockSpec((1, n_rows, HEAD_DIM), lambda g, i: (g, 0, 0)),
            pl.BlockSpec((1, HEAD_DIM, n_rows), lambda g, i: (g, 0, 0)),
            pl.BlockSpec((HPG, nb, tq), lambda g, i: (g, 0, 0)),
            pl.BlockSpec((LANES, n_rows), lambda g, i: (0, 0)),
        ],
        out_specs=(pl.BlockSpec((tq, HPG * HEAD_DIM), lambda g, i: (i, g)),
                   pl.BlockSpec((tq, LANES), lambda g, i: (i, g))),
        scratch_shapes=[pltpu.VMEM((HPG, n_rows, tq), F32)],
        compiler_params=_params("parallel", "parallel"),
    )(proj, kcc, vcct, band, mt)


def _lane_tile(x, width):
    return jnp.concatenate([x] * (width // x.shape[1]), axis=1)


def _softmax_update(s, v_aug, m_ref, acc_ref, rows):
    m_prev = m_ref[rows, :]
    m_next = jnp.maximum(m_prev, jnp.max(s, axis=1, keepdims=True))
    alpha = jnp.exp2(m_prev - m_next)
    p = jnp.exp2(s - _lane_tile(m_next, s.shape[1]))
    acc_ref[rows, :] = _lane_tile(alpha, acc_ref.shape[1]) * acc_ref[rows, :] + _dot(p.astype(BF16), v_aug)
    m_ref[rows, :] = m_next


def _softmax_once(s, v_aug):
    m = jnp.broadcast_to(jnp.max(s, axis=1, keepdims=True), (s.shape[0], LANES))
    p = jnp.exp2(s - _lane_tile(m, s.shape[1]))
    return _dot(p.astype(BF16), v_aug)


def _sel_win_kernel(q_ref, nm_ref, ks_ref, vs_ref, kw_ref, vw_ref, blk_ref, per_ref, g_ref, oc_ref, o_ref,
                    q4, m_s, a_s, bias_ref, *, tile):
    step = pl.program_id(1)
    t = tile
    first = step * Q_TILES

    @pl.when(step == 0)
    def _():
        tl = lax.broadcasted_iota(jnp.int32, (t, t), 0)
        kl = lax.broadcasted_iota(jnp.int32, (t, t), 1)
        edge = jnp.where(tl < kl, 0.0, NEG)
        for h in range(HPG):
            bias_ref[h, :, :t] = edge
            for which in range(2):
                rows = jnp.broadcast_to(per_ref[h, which:which + 1, :], (t, 2 * t))
                table = pltpu.roll(rows, 0, 1, stride=1, stride_axis=0)
                bias_ref[h, :, (which + 1) * t:(which + 2) * t] = table[:, :t]

    def stacked(qt, h):
        return slice((qt * HPG + h) * t, (qt * HPG + h + 1) * t)

    def tile_rows(qt):
        return slice(qt * t, (qt + 1) * t)

    for qt in range(Q_TILES):
        for h in range(HPG):
            q4[stacked(qt, h), :HEAD_DIM] = q_ref[tile_rows(qt), h * HEAD_DIM:(h + 1) * HEAD_DIM]
            q4[stacked(qt, h), HEAD_DIM:] = nm_ref[tile_rows(qt), :]
    m_s[...] = jnp.full(m_s.shape, -3e38, F32)
    a_s[...] = jnp.zeros(a_s.shape, F32)

    def sel_operands(row0, n_tiles):
        keys = pl.ds(row0, n_tiles * t)
        k_aug = jnp.concatenate([ks_ref[keys, :], blk_ref[keys, :]], axis=1)
        v_aug = jnp.concatenate([vs_ref[keys, :], jnp.ones((n_tiles * t, LANES), BF16)], axis=1)
        return k_aug, v_aug

    def sel_pass(qt, operands, n_biased):
        k_aug, v_aug = operands

        def score(h):
            s = _dot_nt(q4[stacked(qt, h), :], k_aug)
            plain = s.shape[1] - n_biased * t
            if n_biased:
                near = s[:, plain:] + bias_ref[h, :, (3 - n_biased) * t:]
                s = jnp.concatenate([s[:, :plain], near], axis=1) if plain else near
            return s

        def consume(h, s):
            _softmax_update(s, v_aug, m_s, a_s, stacked(qt, h))

        return score, consume

    def win_pass(qt, row0, n_tiles):
        keys = pl.ds(row0, n_tiles * t)
        v_aug = jnp.concatenate([vw_ref[keys, :], jnp.ones((n_tiles * t, LANES), BF16)], axis=1)

        def score(h):
            return _dot_nt(q4[stacked(qt, h), :HEAD_DIM], kw_ref[keys, :]) + bias_ref[h, :, (3 - n_tiles) * t:]

        def consume(h, s):
            acc_w = _softmax_once(s, v_aug)
            acc_s = a_s[stacked(qt, h), :]
            o_w = acc_w[:, :HEAD_DIM] * (1.0 / acc_w[:, HEAD_DIM:])
            o_s = acc_s[:, :HEAD_DIM] * (1.0 / acc_s[:, HEAD_DIM:])
            cols = slice(h * HEAD_DIM, (h + 1) * HEAD_DIM)
            gates = g_ref[tile_rows(qt), :]
            o = (gates[:, h:h + 1] * oc_ref[tile_rows(qt), cols] + gates[:, HPG + h:HPG + h + 1] * o_s
                 + gates[:, 2 * HPG + h:2 * HPG + h + 1] * o_w)
            o_ref[tile_rows(qt), cols] = o.astype(o_ref.dtype)

        return score, consume

    def run(passes):
        work = [(score, consume, h0) for score, consume in passes for h0 in range(0, HPG, 2)]
        ahead = [work[0][0](h) for h in (0, 1)]
        for n, (_, consume, h0) in enumerate(work):
            now = ahead
            if n + 1 < len(work):
                nxt_score, _, nxt_h0 = work[n + 1]
                ahead = [nxt_score(h) for h in (nxt_h0, nxt_h0 + 1)]
            for h, s in zip((h0, h0 + 1), now):
                consume(h, s)

    def far_passes(row0):
        operands = sel_operands(row0, FAR_TILES)
        return [sel_pass(qt, operands, 0) for qt in range(Q_TILES)]

    n_far = jnp.maximum(first - 1, 0)
    n_chunks = n_far // FAR_TILES
    chunk = FAR_TILES * t

    def far_pair(c, carry):
        row0 = pl.multiple_of(c * 2 * chunk, 2 * chunk)
        run(far_passes(row0) + far_passes(row0 + chunk))
        return carry

    lax.fori_loop(0, n_chunks // 2, far_pair, 0)

    @pl.when(n_chunks % 2 == 1)
    def _():
        run(far_passes(pl.multiple_of((n_chunks - 1) * chunk, chunk)))

    def tail(left):
        row0 = pl.multiple_of((n_far - left) * t, t)
        passes = []
        for qt in range(Q_TILES):
            n_sel = left + 2 + qt
            passes.append(sel_pass(qt, sel_operands(row0, n_sel), 2))
            passes.append(win_pass(qt, pl.multiple_of((first + qt - 2) * t, t), 3))
        run(passes)

    for left in range(FAR_TILES):
        if (left + 1) % Q_TILES == 0:
            pl.when(jnp.logical_and(step >= 1, n_far % FAR_TILES == left))(functools.partial(tail, left))

    @pl.when(step == 0)
    def _():
        passes = []
        for qt in range(Q_TILES):
            n_keys = min(qt + 1, 3)
            passes.append(sel_pass(qt, sel_operands(0, qt + 1), min(qt + 1, 2)))
            passes.append(win_pass(qt, (qt + 1 - n_keys) * t, n_keys))
        run(passes)


def _sel_win(proj, neg_mask, blk_onehot, periods, gates, o_c, tile):
    s = proj.shape[0]
    ks_blk = (Q_WIDTH + 2 * KV_WIDTH) // HEAD_DIM
    vs_blk = (Q_WIDTH + 3 * KV_WIDTH) // HEAD_DIM
    kw_blk = (Q_WIDTH + 4 * KV_WIDTH) // HEAD_DIM
    vw_blk = (Q_WIDTH + 5 * KV_WIDTH) // HEAD_DIM
    once = pl.Buffered(1)
    rows = Q_TILES * tile
    stacked_rows = HPG * rows
    kern = functools.partial(_sel_win_kernel, tile=tile)
    return pl.pallas_call(
        kern,
        out_shape=jax.ShapeDtypeStruct((s, Q_WIDTH), BF16),
        grid=(N_KV_GROUPS, s // rows),
        in_specs=[
            pl.BlockSpec((rows, HPG * HEAD_DIM), lambda g, i: (i, g)),
            pl.BlockSpec((rows, LANES), lambda g, i: (i, g)),
            pl.BlockSpec((s, HEAD_DIM), lambda g, i: (0, ks_blk + g)),
            pl.BlockSpec((s, HEAD_DIM), lambda g, i: (0, vs_blk + g)),
            pl.BlockSpec((s, HEAD_DIM), lambda g, i: (0, kw_blk + g)),
            pl.BlockSpec((s, HEAD_DIM), lambda g, i: (0, vw_blk + g)),
            pl.BlockSpec((s, LANES), lambda g, i: (0, 0), pipeline_mode=once),
            pl.BlockSpec((HPG, 2, 2 * tile), lambda g, i: (g, 0, 0)),
            pl.BlockSpec((rows, LANES), lambda g, i: (i, g)),
            pl.BlockSpec((rows, HPG * HEAD_DIM), lambda g, i: (i, g)),
        ],
        out_specs=pl.BlockSpec((rows, HPG * HEAD_DIM), lambda g, i: (i, g)),
        scratch_shapes=[
            pltpu.VMEM((stacked_rows, 2 * HEAD_DIM), BF16),
            pltpu.VMEM((stacked_rows, LANES), F32),
            pltpu.VMEM((stacked_rows, 2 * HEAD_DIM), F32),
            pltpu.VMEM((HPG, tile, 3 * tile), F32),
        ],
        compiler_params=_params("arbitrary", "arbitrary"),
    )(proj, neg_mask, proj, proj, proj, proj, blk_onehot, periods, gates, o_c)


def _rel_bucket_np(dist):
    n = np.maximum(dist, 0)
    max_exact = REL_BUCKETS // 2
    large = max_exact + (np.log(np.maximum(n, 1).astype(np.float32) / np.float32(max_exact))
                         / np.float32(math.log(REL_MAX_DIST / max_exact))
                         * np.float32(REL_BUCKETS - max_exact)).astype(np.int32)
    large = np.minimum(large, REL_BUCKETS - 1)
    return np.where(n < max_exact, n, large).astype(np.int32)


def _far_distance():
    d = np.arange(4 * REL_MAX_DIST)
    b = _rel_bucket_np(d)
    assert b[-1] == REL_BUCKETS - 1
    return int(np.max(np.nonzero(b != REL_BUCKETS - 1)[0])) + 1


def _bias_table(rel_bias, dist):
    bucket = jnp.asarray(_rel_bucket_np(dist))
    shifted = (rel_bias[bucket] - rel_bias[REL_BUCKETS - 1]) * LOG2E
    tab = jnp.where(jnp.asarray(dist >= 0)[..., None], shifted, NEG)
    return jnp.moveaxis(tab, -1, 0).astype(F32)


def _bias_period(rel_bias, offset, t):
    x = np.arange(2 * t)
    return _bias_table(rel_bias, np.where(x < t, offset - x, offset + 2 * t - x))


def _nsa_mixer(h, hn, rel_bias, w_in, w_out, layer, pos_k, w1_k, w2_k, pos_v, w1_v, w2_v, next_norm_g):
    s = h.shape[0]
    n_sel = s // SEL_BLOCK
    ncp = s // CMP_STRIDE
    tile = ATT_TILE
    far = _far_distance()
    assert n_sel <= LANES and s % CMP_TQ == 0 and s % (Q_TILES * tile) == 0 and FAR_TILES % Q_TILES == 0
    assert far <= tile and far <= CMP_STRIDE * CMP_PAD - CMP_BLOCK + 1 + CMP_STRIDE

    col_scale = jnp.where(jnp.arange(PROJ_MAIN) < Q_WIDTH, HEAD_DIM ** -0.5 * LOG2E, 1.0).astype(F32)[None, :]
    w_in_t = jnp.swapaxes(w_in, 1, 2)
    proj = _matmul(_ep_scale, hn, w_in_t, layer, PROJ_MAIN, BF16, 2048, 512, row_extra=col_scale,
                   w_rows_are_outputs=True)
    w_gate = w_in_t[layer, PROJ_MAIN:].reshape(N_GATES, N_KV_GROUPS, HPG, D_MODEL).transpose(1, 0, 2, 3)
    w_gate = w_gate.reshape(N_KV_GROUPS, N_GATES * HPG, D_MODEL)
    w_gate = jnp.pad(w_gate, ((0, 0), (0, LANES - N_GATES * HPG), (0, 0))).reshape(1, N_KV_GROUPS * LANES, D_MODEL)
    gates = _matmul(jax.nn.sigmoid, hn, w_gate, 0, N_KV_GROUPS * LANES, F32, 2048, 512, w_rows_are_outputs=True)

    w1 = jnp.stack([w1_k, w1_v]).astype(BF16)
    w2 = jnp.stack([w2_k, w2_v]).astype(BF16)
    pos = jnp.stack([pos_k.reshape(1, -1), pos_v.reshape(1, -1)]).astype(BF16)
    pos = jnp.broadcast_to(pos, (2, SUBLANES, CMP_BLOCK * HEAD_DIM))
    cc = _compress(proj, w1, w2, pos)
    kcc = jnp.pad(cc[0], ((0, 0), (CMP_PAD, 0), (0, 0)))
    vcct = jnp.pad(cc[1], ((0, 0), (CMP_PAD, 0), (0, 0))).transpose(0, 2, 1)
    n_rows = ncp + CMP_PAD

    nb = CMP_TQ // CMP_STRIDE + CMP_PAD
    first_dist = -CMP_STRIDE * (nb - 1 - CMP_PAD) - (CMP_BLOCK - 1)
    by_dist = _bias_table(rel_bias, first_dist + np.arange(CMP_TQ + CMP_STRIDE * (nb - 1)))
    band = jnp.stack([by_dist[:, CMP_STRIDE * (nb - 1 - b):CMP_STRIDE * (nb - 1 - b) + CMP_TQ] for b in range(nb)], axis=1)
    ratio = SEL_BLOCK // CMP_STRIDE
    lo = CMP_BLOCK // CMP_STRIDE - 1
    c_of_row = np.arange(n_rows)[None, :] - CMP_PAD
    j_of = np.arange(LANES)[:, None]
    mt = ((c_of_row >= ratio * j_of - lo) & (c_of_row <= ratio * j_of + ratio - 1)
          & (c_of_row >= 0) & (c_of_row <= ncp - 2) & (j_of < n_sel)).astype(np.float32)
    o_c, neg_mask = _cmp_attn(proj, kcc, vcct, band, jnp.asarray(mt, BF16), CMP_TQ)

    onehot = (np.arange(s)[:, None] // SEL_BLOCK == np.arange(LANES)[None, :]).astype(np.float32)
    periods = jnp.stack([_bias_period(rel_bias, tile, tile), _bias_period(rel_bias, 0, tile)], axis=1)
    o = _sel_win(proj, neg_mask, jnp.asarray(onehot, BF16), periods, gates, o_c, tile)
    return _matmul_resid_norm(o, w_out, layer, h, next_norm_g, 256)


def _s5_kernel(x_ref, lam_ref, bt_ref, c_ref, d_ref, y_ref, vt, yt, tmask, *, chunk):
    L = chunk
    half = L // 2
    width = L * SSM_GROUP
    n_chunks = x_ref.shape[0] // L
    by_step = jnp.swapaxes(x_ref[...].reshape(n_chunks, L, LANES), 0, 1)
    for tau in range(L):
        vt[tau] = by_step[tau].T

    @pl.when(pl.program_id(0) == 0)
    def _():
        dst = lax.broadcasted_iota(jnp.int32, (width, width), 0) // SSM_GROUP
        src = lax.broadcasted_iota(jnp.int32, (width, width), 1) // SSM_GROUP
        tmask[...] = jnp.where(src <= dst, 1.0, 0.0)

    def cmul(xr, xi, yr, yi):
        return xr * yr - xi * yi, xr * yi + xi * yr

    n2 = 2 * SSM_STATE
    sub = lax.broadcasted_iota(jnp.int32, (SUBLANES, 1), 0)
    consts = jnp.where(sub == 0, 1.0, jnp.where(sub == 1, float(half + 1), jnp.where(sub == 2, float(half - 1), float(L))))
    tau_col = lax.broadcasted_iota(jnp.int32, (L, 1), 0).astype(F32)
    row = lax.broadcasted_iota(jnp.int32, (n_chunks, n2), 0)
    conj = jnp.where(lax.broadcasted_iota(jnp.int32, (1, n2), 1) < SSM_STATE, 1.0, -1.0)

    def group(g):
        lam = lam_ref[g]
        a_re, a_im = lam[0:1], lam[1:2]
        dt = jnp.exp(lam[2:3])
        log_re, log_im = a_re * dt, a_im * dt

        def cpow(e):
            mag, cos, sin = jnp.exp(e * log_re), jnp.cos(e * log_im), jnp.sin(e * log_im)
            inv = 1.0 / mag
            return mag * cos, mag * sin, inv * cos, -(inv * sin)

        kr, ki, _, _ = cpow(consts)
        pr, pi = kr[0:1] - 1.0, ki[0:1]
        den = a_re * a_re + a_im * a_im
        cf_r = (pr * a_re + pi * a_im) / den
        cf_i = (pi * a_re - pr * a_im) / den
        bt_c, bt_s = bt_ref[g, 0], bt_ref[g, 1]
        bb_c = cf_r * bt_c + cf_i * bt_s
        bb_s = cf_r * bt_s - cf_i * bt_c
        c_c, c_s = c_ref[g, 0], c_ref[g, 1]

        e1r, e1i, e2r, e2i = cpow(tau_col - half)
        e3r, e3i = cmul(e1r, e1i, kr[1:2], ki[1:2])
        e4r, e4i = cmul(e2r, e2i, kr[2:3], ki[2:3])

        def outer(er, ei, w_c, w_s):
            return (er[:, None, :] * w_c[None, :, :] + ei[:, None, :] * w_s[None, :, :]).reshape(width, n2)

        qm = outer(e1r, e1i, c_c, c_s)
        km_conj = outer(e2r, e2i, bb_c * conj, bb_s * conj)
        qc = outer(e3r, e3i, c_c, c_s)
        wz = outer(e4r, e4i, bb_c, bb_s)

        chans = pl.ds(pl.multiple_of(g * SSM_GROUP, SSM_GROUP), SSM_GROUP)
        ub = vt[:, chans, :].reshape(width, n_chunks).astype(BF16)
        yield

        tt = _dot_nt(qm.astype(BF16), km_conj.astype(BF16))
        x = _dot(wz.T.astype(BF16), ub).T
        yield
        y = _dot((tt * tmask[...]).astype(BF16), ub)
        p_r, p_i = kr[3:4], ki[3:4]
        d = 1
        while d < n_chunks:
            s = jnp.where(row >= d, pltpu.roll(x, d, 0), 0.0)
            x = x + s * p_r - pltpu.roll(s, SSM_STATE, 1) * (p_i * conj)
            p_r, p_i = cmul(p_r, p_i, p_r, p_i)
            d *= 2
        prev_conj = jnp.where(row >= 1, pltpu.roll(x, 1, 0), 0.0) * conj
        yield
        y = y + _dot_nt(qc.astype(BF16), prev_conj.astype(BF16))
        yt[:, chans, :] = y.reshape(L, SSM_GROUP, n_chunks)

    def group_batch(k, carry):
        running = [group(S5_GROUPS_PER_TRIP * k + j) for j in range(S5_GROUPS_PER_TRIP)]
        finished = object()
        while running:
            running = [gen for gen in running if next(gen, finished) is not finished]
        return carry

    lax.fori_loop(0, LANES // SSM_GROUP // S5_GROUPS_PER_TRIP, group_batch, 0)
    by_chunk = jnp.swapaxes(jnp.stack([yt[tau].T for tau in range(L)], axis=0), 0, 1)
    y = by_chunk.reshape(n_chunks * L, LANES) + d_ref[...] * x_ref[...]
    y_ref[...] = jax.nn.gelu(y).astype(y_ref.dtype)


def _s5_mixer(h, hn, a_re, a_im, log_dt, b_re, b_im, c_re, c_im, d_skip, w_glu, layer):
    s = h.shape[0]
    L = S5_CHUNK
    n_chunks = s // L
    width = L * SSM_GROUP
    gpb = LANES // SSM_GROUP
    def packed(re, im):
        return jnp.stack([jnp.concatenate([re, im], -1), jnp.concatenate([-im, re], -1)], axis=1).astype(F32)

    lam = jnp.stack([a_re, a_im, jnp.broadcast_to(log_dt[:, None], a_re.shape)], axis=1).astype(F32)
    lam = jnp.concatenate([lam, lam], axis=-1)
    bt = packed(b_re.transpose(0, 2, 1), b_im.transpose(0, 2, 1))
    cc = packed(c_re, c_im)
    dd = d_skip.astype(F32).reshape(1, D_MODEL)
    y = pl.pallas_call(
        functools.partial(_s5_kernel, chunk=L),
        out_shape=jax.ShapeDtypeStruct((s, D_MODEL), BF16),
        grid=(SSM_GROUPS // gpb,),
        in_specs=[
            pl.BlockSpec((s, LANES), lambda b: (0, b)),
            pl.BlockSpec((gpb, 3, 2 * SSM_STATE), lambda b: (b, 0, 0)),
            pl.BlockSpec((gpb, 2, SSM_GROUP, 2 * SSM_STATE), lambda b: (b, 0, 0, 0)),
            pl.BlockSpec((gpb, 2, SSM_GROUP, 2 * SSM_STATE), lambda b: (b, 0, 0, 0)),
            pl.BlockSpec((1, LANES), lambda b: (0, b)),
        ],
        out_specs=pl.BlockSpec((s, LANES), lambda b: (0, b)),
        scratch_shapes=[
            pltpu.VMEM((L, LANES, n_chunks), hn.dtype),
            pltpu.VMEM((L, LANES, n_chunks), F32),
            pltpu.VMEM((width, width), F32),
        ],
        compiler_params=_params("arbitrary"),
    )(hn, lam, bt, cc, dd)
    return _matmul(_ep_glu_resid, y, w_glu, layer, D_MODEL, F32, 1024, 512, b_halves=2, tile_extra=h)


def kernel(x, rel_bias, mix_norm_g, ffn_norm_g, final_norm_g, nsa_w_in, nsa_w_out, cmp_pos_k, cmp_w1_k, cmp_w2_k, cmp_pos_v, cmp_w1_v, cmp_w2_v, s5_A_re, s5_A_im, s5_log_dt, s5_B_re, s5_B_im, s5_C_re, s5_C_im, s5_D, s5_w_glu, ffn_w_in, ffn_w_out):
    assert x.shape[0] == 1
    h = x[0]
    hn = _rmsnorm(h, mix_norm_g[0], BF16)
    h, hn = _nsa_mixer(h, hn, rel_bias, nsa_w_in, nsa_w_out, 0, cmp_pos_k[0], cmp_w1_k[0], cmp_w2_k[0],
                       cmp_pos_v[0], cmp_w1_v[0], cmp_w2_v[0], ffn_norm_g[0])
    h = _swiglu_ffn(h, hn, ffn_w_in, ffn_w_out, 0)
    hn = _rmsnorm(h, mix_norm_g[1], BF16)
    h = _s5_mixer(h, hn, s5_A_re[0], s5_A_im[0], s5_log_dt[0], s5_B_re[0], s5_B_im[0], s5_C_re[0], s5_C_im[0],
                  s5_D[0], s5_w_glu, 0)
    h = _swiglu_ffn(h, _rmsnorm(h, ffn_norm_g[1], BF16), ffn_w_in, ffn_w_out, 1)
    return _rmsnorm(h, final_norm_g, x.dtype)[None]
```
